```python
import math
import jax, jax.numpy as jnp
from jax import lax
import numpy as np

D_MODEL = 2048
BATCH = 8
SEQ = 2048
DEPTH = 4

N_A_LAYERS = DEPTH // 2
N_B_LAYERS = DEPTH - N_A_LAYERS
CHUNK = 128
A_GROUPS = 8
A_HALF = D_MODEL
HEAD_DIM = 64
N_HEADS = D_MODEL // HEAD_DIM
N_KV_HEADS = N_HEADS // 8
WINDOW = 128
BLOCK = WINDOW
N_BUCKETS = 32
MAX_DISTANCE = 128
D_FF = ((8 * D_MODEL // 3 + 255) // 256) * 256
RMS_EPS = 1e-5
NEG_INF = -1e30

kernel_name = "yoco_gmlp_swa_sink_hybrid"


def rmsnorm(x, g):
    x32 = x.astype(jnp.float32)
    y = x32 * lax.rsqrt(jnp.mean(x32 * x32, axis=-1, keepdims=True) + RMS_EPS)
    return (y * g.astype(jnp.float32)).astype(x.dtype)


def swiglu(x, w_gate, w_up, w_down):
    return (jax.nn.silu(x @ w_gate) * (x @ w_up)) @ w_down


def gmlp_mixer(xn, w_in, norm_v, w_s, b_s, w_out):
    B, S, _ = xn.shape
    n_chunks = S // CHUNK
    z = jax.nn.gelu(xn @ w_in)
    u, v = jnp.split(z, 2, axis=-1)
    v = rmsnorm(v, norm_v)
    v = v.reshape(B, n_chunks, CHUNK, A_GROUPS, A_HALF // A_GROUPS)
    w_causal = jnp.tril(w_s)
    s = jnp.einsum('gts,bcsgd->bctgd', w_causal, v) + b_s.T[None, None, :, :, None]
    gated = u * s.reshape(B, S, A_HALF)
    return gated @ w_out


def t5_bucket(dist):
    max_exact = N_BUCKETS // 2
    is_small = dist < max_exact
    d = jnp.maximum(dist, 1).astype(jnp.float32)
    large = max_exact + (jnp.log(d / max_exact) / math.log(MAX_DISTANCE / max_exact)
                         * (N_BUCKETS - max_exact)).astype(jnp.int32)
    large = jnp.minimum(large, N_BUCKETS - 1)
    return jnp.where(is_small, dist, large)


def banded_sink_attention(q, k, v, sinks, rel_bias):
    B, S = q.shape[0], q.shape[1]
    nb = S // BLOCK
    grp = N_HEADS // N_KV_HEADS
    qb = q.reshape(B, nb, BLOCK, N_KV_HEADS, grp, HEAD_DIM).astype(jnp.float32)

    def band(t):
        tp = jnp.pad(t, ((0, 0), (BLOCK, 0), (0, 0), (0, 0)))
        prev = tp[:, :S].reshape(B, nb, BLOCK, N_KV_HEADS, HEAD_DIM)
        cur = t.reshape(B, nb, BLOCK, N_KV_HEADS, HEAD_DIM)
        return jnp.concatenate([prev, cur], axis=2)

    kb = band(k).astype(jnp.float32)
    vb = band(v)
    scores = jnp.einsum('bcqhgd,bckhd->bchgqk', qb, kb) / math.sqrt(HEAD_DIM)

    dist = np.arange(BLOCK)[:, None] + BLOCK - np.arange(2 * BLOCK)[None, :]
    in_window = (dist >= 0) & (dist < WINDOW)
    bucket = t5_bucket(jnp.asarray(np.clip(dist, 0, None), dtype=jnp.int32))
    bias = rel_bias[bucket].astype(jnp.float32)
    bias = bias.transpose(2, 0, 1).reshape(N_KV_HEADS, grp, BLOCK, 2 * BLOCK)
    key_exists = (np.arange(nb)[:, None] * BLOCK - BLOCK + np.arange(2 * BLOCK)[None, :]) >= 0
    mask = in_window[None] & key_exists[:, None, :]
    mask = jnp.asarray(mask)[None, :, None, None]
    scores = jnp.where(mask, scores + bias, NEG_INF)

    sink = sinks.astype(jnp.float32).reshape(N_KV_HEADS, grp)[None, None, :, :, None, None]
    m = jnp.maximum(scores.max(axis=-1, keepdims=True), sink)
    p = jnp.exp(scores - m)
    denom = p.sum(axis=-1, keepdims=True) + jnp.exp(sink - m)
    probs = (p / denom).astype(v.dtype)
    out = jnp.einsum('bchgqk,bckhd->bcqhgd', probs, vb)
    return out.reshape(B, S, N_HEADS * HEAD_DIM)


def _fwd_setup_inputs(seed: int = 0) -> dict:
    key = jax.random.key(seed)
    ks = jax.random.split(key, 24)
    f32 = jnp.float32
    out_scale = (2.0 * DEPTH) ** -0.5

    def nrm(k, shape, scale):
        return jax.random.normal(k, shape, f32) * scale

    kv_dim = 2 * N_KV_HEADS * HEAD_DIM
    return {
        "x": nrm(ks[0], (BATCH, SEQ, D_MODEL), 1.0),
        "mix_norm": 1.0 + nrm(ks[1], (DEPTH, D_MODEL), 0.1),
        "ffn_norm": 1.0 + nrm(ks[2], (DEPTH, D_MODEL), 0.1),
        "a_w_in": nrm(ks[3], (N_A_LAYERS, D_MODEL, 2 * A_HALF), D_MODEL ** -0.5),
        "a_norm_v": 1.0 + nrm(ks[4], (N_A_LAYERS, A_HALF), 0.1),
        "a_w_s": nrm(ks[5], (N_A_LAYERS, A_GROUPS, CHUNK, CHUNK), CHUNK ** -0.5),
        "a_b_s": 1.0 + nrm(ks[6], (N_A_LAYERS, A_GROUPS, CHUNK), 0.1),
        "a_w_out": nrm(ks[7], (N_A_LAYERS, A_HALF, D_MODEL), A_HALF ** -0.5 * out_scale),
        "kv_norm": 1.0 + nrm(ks[8], (D_MODEL,), 0.1),
        "w_kv": nrm(ks[9], (D_MODEL, kv_dim), D_MODEL ** -0.5),
        "b_kv": nrm(ks[10], (kv_dim,), 0.02),
        "b_w_q": nrm(ks[11], (N_B_LAYERS, D_MODEL, N_HEADS * HEAD_DIM), D_MODEL ** -0.5),
        "b_b_q": nrm(ks[12], (N_B_LAYERS, N_HEADS * HEAD_DIM), 0.02),
        "b_sinks": nrm(ks[13], (N_B_LAYERS, N_HEADS), 1.0),
        "b_w_o": nrm(ks[14], (N_B_LAYERS, N_HEADS * HEAD_DIM, D_MODEL), (N_HEADS * HEAD_DIM) ** -0.5 * out_scale),
        "b_b_o": nrm(ks[15], (N_B_LAYERS, D_MODEL), 0.02),
        "rel_bias": nrm(ks[16], (N_BUCKETS, N_HEADS), 0.5),
        "ffn_w_gate": nrm(ks[17], (DEPTH, D_MODEL, D_FF), D_MODEL ** -0.5),
        "ffn_w_up": nrm(ks[18], (DEPTH, D_MODEL, D_FF), D_MODEL ** -0.5),
        "ffn_w_down": nrm(ks[19], (DEPTH, D_FF, D_MODEL), D_FF ** -0.5 * out_scale),
        "final_norm": 1.0 + nrm(ks[20], (D_MODEL,), 0.1),
    }


def _fwd_reference(x, mix_norm, ffn_norm, a_w_in, a_norm_v, a_w_s, a_b_s, a_w_out,
              kv_norm, w_kv, b_kv, b_w_q, b_b_q, b_sinks, b_w_o, b_b_o, rel_bias,
              ffn_w_gate, ffn_w_up, ffn_w_down, final_norm):
    B, S, _ = x.shape
    h = x
    k_shared = None
    v_shared = None
    for layer in range(DEPTH):
        xn = rmsnorm(h, mix_norm[layer])
        if layer < N_A_LAYERS:
            i = layer
            h = h + gmlp_mixer(xn, a_w_in[i], a_norm_v[i], a_w_s[i], a_b_s[i], a_w_out[i])
        else:
            i = layer - N_A_LAYERS
            q = (xn @ b_w_q[i] + b_b_q[i]).reshape(B, S, N_HEADS, HEAD_DIM)
            attn = banded_sink_attention(q, k_shared, v_shared, b_sinks[i], rel_bias)
            h = h + attn @ b_w_o[i] + b_b_o[i]
        h = h + swiglu(rmsnorm(h, ffn_norm[layer]), ffn_w_gate[layer], ffn_w_up[layer], ffn_w_down[layer])
        if layer == N_A_LAYERS - 1:
            kv = rmsnorm(h, kv_norm) @ w_kv + b_kv
            k_flat, v_flat = jnp.split(kv, 2, axis=-1)
            k_shared = k_flat.reshape(B, S, N_KV_HEADS, HEAD_DIM)
            v_shared = v_flat.reshape(B, S, N_KV_HEADS, HEAD_DIM)
    return rmsnorm(h, final_norm)


import jax as _jax
import jax.numpy as _jnp

TWIN_FORMAT = 'train_step'
FWD_PARAMS = ['x', 'mix_norm', 'ffn_norm', 'a_w_in', 'a_norm_v', 'a_w_s', 'a_b_s', 'a_w_out', 'kv_norm', 'w_kv', 'b_kv', 'b_w_q', 'b_b_q', 'b_sinks', 'b_w_o', 'b_b_o', 'rel_bias', 'ffn_w_gate', 'ffn_w_up', 'ffn_w_down', 'final_norm']
TWIN_WEIGHTS = ['mix_norm', 'ffn_norm', 'a_w_in', 'a_norm_v', 'a_w_s', 'a_b_s', 'a_w_out', 'kv_norm', 'w_kv', 'b_kv', 'b_w_q', 'b_b_q', 'b_sinks', 'b_w_o', 'b_b_o', 'rel_bias', 'ffn_w_gate', 'ffn_w_up', 'ffn_w_down', 'final_norm']
TWIN_DIFF_INPUT = 'x'
TWIN_INPUTS = ['x', 'mix_norm', 'ffn_norm', 'a_w_in', 'a_norm_v', 'a_w_s', 'a_b_s', 'a_w_out', 'kv_norm', 'w_kv', 'b_kv', 'b_w_q', 'b_b_q', 'b_sinks', 'b_w_o', 'b_b_o', 'rel_bias', 'ffn_w_gate', 'ffn_w_up', 'ffn_w_down', 'final_norm', 'loss_target', 'm_mix_norm', 'm_ffn_norm', 'm_a_w_in', 'm_a_norm_v', 'm_a_w_s', 'm_a_b_s', 'm_a_w_out', 'm_kv_norm', 'm_w_kv', 'm_b_kv', 'm_b_w_q', 'm_b_b_q', 'm_b_sinks', 'm_b_w_o', 'm_b_b_o', 'm_rel_bias', 'm_ffn_w_gate', 'm_ffn_w_up', 'm_ffn_w_down', 'm_final_norm', 'v_mix_norm', 'v_ffn_norm', 'v_a_w_in', 'v_a_norm_v', 'v_a_w_s', 'v_a_b_s', 'v_a_w_out', 'v_kv_norm', 'v_w_kv', 'v_b_kv', 'v_b_w_q', 'v_b_b_q', 'v_b_sinks', 'v_b_w_o', 'v_b_b_o', 'v_rel_bias', 'v_ffn_w_gate', 'v_ffn_w_up', 'v_ffn_w_down', 'v_final_norm']
TWIN_OUTPUTS = ['loss', 'grad_x', 'grad_mix_norm', 'grad_ffn_norm', 'grad_a_w_in', 'grad_a_norm_v', 'grad_a_w_s', 'grad_a_b_s', 'grad_a_w_out', 'grad_kv_norm', 'grad_w_kv', 'grad_b_kv', 'grad_b_w_q', 'grad_b_b_q', 'grad_b_sinks', 'grad_b_w_o', 'grad_b_b_o', 'grad_rel_bias', 'grad_ffn_w_gate', 'grad_ffn_w_up', 'grad_ffn_w_down', 'grad_final_norm', 'delta_mix_norm', 'delta_ffn_norm', 'delta_a_w_in', 'delta_a_norm_v', 'delta_a_w_s', 'delta_a_b_s', 'delta_a_w_out', 'delta_kv_norm', 'delta_w_kv', 'delta_b_kv', 'delta_b_w_q', 'delta_b_b_q', 'delta_b_sinks', 'delta_b_w_o', 'delta_b_b_o', 'delta_rel_bias', 'delta_ffn_w_gate', 'delta_ffn_w_up', 'delta_ffn_w_down', 'delta_final_norm', 'new_m_mix_norm', 'new_m_ffn_norm', 'new_m_a_w_in', 'new_m_a_norm_v', 'new_m_a_w_s', 'new_m_a_b_s', 'new_m_a_w_out', 'new_m_kv_norm', 'new_m_w_kv', 'new_m_b_kv', 'new_m_b_w_q', 'new_m_b_b_q', 'new_m_b_sinks', 'new_m_b_w_o', 'new_m_b_b_o', 'new_m_rel_bias', 'new_m_ffn_w_gate', 'new_m_ffn_w_up', 'new_m_ffn_w_down', 'new_m_final_norm', 'new_v_mix_norm', 'new_v_ffn_norm', 'new_v_a_w_in', 'new_v_a_norm_v', 'new_v_a_w_s', 'new_v_a_b_s', 'new_v_a_w_out', 'new_v_kv_norm', 'new_v_w_kv', 'new_v_b_kv', 'new_v_b_w_q', 'new_v_b_b_q', 'new_v_b_sinks', 'new_v_b_w_o', 'new_v_b_b_o', 'new_v_rel_bias', 'new_v_ffn_w_gate', 'new_v_ffn_w_up', 'new_v_ffn_w_down', 'new_v_final_norm']
TWIN_LEAF_KINDS = {'loss': 'loss', 'grad_x': 'grad_x', 'grad_mix_norm': 'grad_w', 'grad_ffn_norm': 'grad_w', 'grad_a_w_in': 'grad_w', 'grad_a_norm_v': 'grad_w', 'grad_a_w_s': 'grad_w', 'grad_a_b_s': 'grad_w', 'grad_a_w_out': 'grad_w', 'grad_kv_norm': 'grad_w', 'grad_w_kv': 'grad_w', 'grad_b_kv': 'grad_w', 'grad_b_w_q': 'grad_w', 'grad_b_b_q': 'grad_w', 'grad_b_sinks': 'grad_w', 'grad_b_w_o': 'grad_w', 'grad_b_b_o': 'grad_w', 'grad_rel_bias': 'grad_w', 'grad_ffn_w_gate': 'grad_w', 'grad_ffn_w_up': 'grad_w', 'grad_ffn_w_down': 'grad_w', 'grad_final_norm': 'grad_w', 'delta_mix_norm': 'delta_w', 'delta_ffn_norm': 'delta_w', 'delta_a_w_in': 'delta_w', 'delta_a_norm_v': 'delta_w', 'delta_a_w_s': 'delta_w', 'delta_a_b_s': 'delta_w', 'delta_a_w_out': 'delta_w', 'delta_kv_norm': 'delta_w', 'delta_w_kv': 'delta_w', 'delta_b_kv': 'delta_w', 'delta_b_w_q': 'delta_w', 'delta_b_b_q': 'delta_w', 'delta_b_sinks': 'delta_w', 'delta_b_w_o': 'delta_w', 'delta_b_b_o': 'delta_w', 'delta_rel_bias': 'delta_w', 'delta_ffn_w_gate': 'delta_w', 'delta_ffn_w_up': 'delta_w', 'delta_ffn_w_down': 'delta_w', 'delta_final_norm': 'delta_w', 'new_m_mix_norm': 'new_m', 'new_m_ffn_norm': 'new_m', 'new_m_a_w_in': 'new_m', 'new_m_a_norm_v': 'new_m', 'new_m_a_w_s': 'new_m', 'new_m_a_b_s': 'new_m', 'new_m_a_w_out': 'new_m', 'new_m_kv_norm': 'new_m', 'new_m_w_kv': 'new_m', 'new_m_b_kv': 'new_m', 'new_m_b_w_q': 'new_m', 'new_m_b_b_q': 'new_m', 'new_m_b_sinks': 'new_m', 'new_m_b_w_o': 'new_m', 'new_m_b_b_o': 'new_m', 'new_m_rel_bias': 'new_m', 'new_m_ffn_w_gate': 'new_m', 'new_m_ffn_w_up': 'new_m', 'new_m_ffn_w_down': 'new_m', 'new_m_final_norm': 'new_m', 'new_v_mix_norm': 'new_v', 'new_v_ffn_norm': 'new_v', 'new_v_a_w_in': 'new_v', 'new_v_a_norm_v': 'new_v', 'new_v_a_w_s': 'new_v', 'new_v_a_b_s': 'new_v', 'new_v_a_w_out': 'new_v', 'new_v_kv_norm': 'new_v', 'new_v_w_kv': 'new_v', 'new_v_b_kv': 'new_v', 'new_v_b_w_q': 'new_v', 'new_v_b_b_q': 'new_v', 'new_v_b_sinks': 'new_v', 'new_v_b_w_o': 'new_v', 'new_v_b_b_o': 'new_v', 'new_v_rel_bias': 'new_v', 'new_v_ffn_w_gate': 'new_v', 'new_v_ffn_w_up': 'new_v', 'new_v_ffn_w_down': 'new_v', 'new_v_final_norm': 'new_v'}


def _forward(args):
    return _fwd_reference(*[args[k] for k in FWD_PARAMS])


def _output_shape():
    out = _jax.eval_shape(lambda: _forward(_fwd_setup_inputs(0)))
    return out.shape, out.dtype

N_MICROBATCH = 1
ADAM_LR = 0.001
ADAM_B1 = 0.9
ADAM_B2 = 0.999
ADAM_EPS = 1e-08
ADAM_WD = 0.01
ADAM_STEP = 10
PER_EXAMPLE_BATCH_AXIS = {'x': 0, 'loss_target': 0}
SHARED_INPUTS = []
_WEIGHT_DTYPES = {'mix_norm': _jnp.float32, 'ffn_norm': _jnp.float32, 'a_w_in': _jnp.float32, 'a_norm_v': _jnp.float32, 'a_w_s': _jnp.float32, 'a_b_s': _jnp.float32, 'a_w_out': _jnp.float32, 'kv_norm': _jnp.float32, 'w_kv': _jnp.float32, 'b_kv': _jnp.float32, 'b_w_q': _jnp.float32, 'b_b_q': _jnp.float32, 'b_sinks': _jnp.float32, 'b_w_o': _jnp.float32, 'b_b_o': _jnp.float32, 'rel_bias': _jnp.float32, 'ffn_w_gate': _jnp.float32, 'ffn_w_up': _jnp.float32, 'ffn_w_down': _jnp.float32, 'final_norm': _jnp.float32}
MOMENT_SCALE = {'mix_norm': 1.597189e-02, 'ffn_norm': 1.826418e-02, 'a_w_in': 1.576744e-02, 'a_norm_v': 1.049411e-02, 'a_w_s': 1.521432e-02, 'a_b_s': 2.153008e-02, 'a_w_out': 6.811236e-02, 'kv_norm': 1.040568e-02, 'w_kv': 2.192688e-02, 'b_kv': 1.867833e-01, 'b_w_q': 3.897655e-03, 'b_b_q': 4.036370e-03, 'b_sinks': 4.064431e-03, 'b_w_o': 1.992592e-02, 'b_b_o': 1.686135e-01, 'rel_bias': 6.920013e-03, 'ffn_w_gate': 7.904876e-03, 'ffn_w_up': 7.745528e-03, 'ffn_w_down': 3.637172e-02, 'final_norm': 8.083806e+00}


def _to_microbatches(a, axis):
    t = _jnp.moveaxis(a, axis, 0)
    t = t.reshape((N_MICROBATCH, t.shape[0] // N_MICROBATCH) + t.shape[1:])
    return _jnp.moveaxis(t, 1, axis + 1)


def setup_inputs(seed: int = 0) -> dict:
    inp = _fwd_setup_inputs(seed)
    key = _jax.random.fold_in(_jax.random.key(seed), 7919)
    shape, _ = _output_shape()
    out = dict(inp)
    out["loss_target"] = _jax.random.normal(_jax.random.fold_in(key, 0), shape, _jnp.float32)
    for i, name in enumerate(TWIN_WEIGHTS):
        w = inp[name].astype(_jnp.float32)
        if MOMENT_SCALE is None:
            s = _jnp.sqrt(_jnp.mean(_jnp.square(w)) + 1e-30)
        else:
            s = MOMENT_SCALE[name]
        km, kv = _jax.random.split(_jax.random.fold_in(key, i + 1))
        out[name] = w
        out["m_" + name] = s * _jax.random.normal(km, w.shape, _jnp.float32)
        out["v_" + name] = (s * s) * _jax.random.uniform(kv, w.shape, _jnp.float32, 0.5, 1.5)
    if N_MICROBATCH > 1:
        for name, axis in PER_EXAMPLE_BATCH_AXIS.items():
            out[name] = _to_microbatches(out[name], axis)
    return {'x': out['x'], 'mix_norm': out['mix_norm'], 'ffn_norm': out['ffn_norm'], 'a_w_in': out['a_w_in'], 'a_norm_v': out['a_norm_v'], 'a_w_s': out['a_w_s'], 'a_b_s': out['a_b_s'], 'a_w_out': out['a_w_out'], 'kv_norm': out['kv_norm'], 'w_kv': out['w_kv'], 'b_kv': out['b_kv'], 'b_w_q': out['b_w_q'], 'b_b_q': out['b_b_q'], 'b_sinks': out['b_sinks'], 'b_w_o': out['b_w_o'], 'b_b_o': out['b_b_o'], 'rel_bias': out['rel_bias'], 'ffn_w_gate': out['ffn_w_gate'], 'ffn_w_up': out['ffn_w_up'], 'ffn_w_down': out['ffn_w_down'], 'final_norm': out['final_norm'], 'loss_target': out['loss_target'], 'm_mix_norm': out['m_mix_norm'], 'm_ffn_norm': out['m_ffn_norm'], 'm_a_w_in': out['m_a_w_in'], 'm_a_norm_v': out['m_a_norm_v'], 'm_a_w_s': out['m_a_w_s'], 'm_a_b_s': out['m_a_b_s'], 'm_a_w_out': out['m_a_w_out'], 'm_kv_norm': out['m_kv_norm'], 'm_w_kv': out['m_w_kv'], 'm_b_kv': out['m_b_kv'], 'm_b_w_q': out['m_b_w_q'], 'm_b_b_q': out['m_b_b_q'], 'm_b_sinks': out['m_b_sinks'], 'm_b_w_o': out['m_b_w_o'], 'm_b_b_o': out['m_b_b_o'], 'm_rel_bias': out['m_rel_bias'], 'm_ffn_w_gate': out['m_ffn_w_gate'], 'm_ffn_w_up': out['m_ffn_w_up'], 'm_ffn_w_down': out['m_ffn_w_down'], 'm_final_norm': out['m_final_norm'], 'v_mix_norm': out['v_mix_norm'], 'v_ffn_norm': out['v_ffn_norm'], 'v_a_w_in': out['v_a_w_in'], 'v_a_norm_v': out['v_a_norm_v'], 'v_a_w_s': out['v_a_w_s'], 'v_a_b_s': out['v_a_b_s'], 'v_a_w_out': out['v_a_w_out'], 'v_kv_norm': out['v_kv_norm'], 'v_w_kv': out['v_w_kv'], 'v_b_kv': out['v_b_kv'], 'v_b_w_q': out['v_b_w_q'], 'v_b_b_q': out['v_b_b_q'], 'v_b_sinks': out['v_b_sinks'], 'v_b_w_o': out['v_b_w_o'], 'v_b_b_o': out['v_b_b_o'], 'v_rel_bias': out['v_rel_bias'], 'v_ffn_w_gate': out['v_ffn_w_gate'], 'v_ffn_w_up': out['v_ffn_w_up'], 'v_ffn_w_down': out['v_ffn_w_down'], 'v_final_norm': out['v_final_norm']}


def _loss(weights, diff, rest, loss_target):
    with _jax.named_scope("forward"):
        args = {**rest, TWIN_DIFF_INPUT: diff, **{k: w.astype(_WEIGHT_DTYPES[k]) for k, w in weights.items()}}
        y = _forward(args)
    with _jax.named_scope("loss_head"):
        err = _jnp.square(y.astype(_jnp.float32) - loss_target)
        return 0.5 * _jnp.sum(_jnp.mean(err, axis=-1)) if err.ndim else 0.5 * err


def _adamw(w, g, m, v):
    m = ADAM_B1 * m + (1.0 - ADAM_B1) * g
    v = ADAM_B2 * v + (1.0 - ADAM_B2) * _jnp.square(g)
    m_hat = m / (1.0 - ADAM_B1 ** ADAM_STEP)
    v_hat = v / (1.0 - ADAM_B2 ** ADAM_STEP)
    delta = -ADAM_LR * (m_hat / (_jnp.sqrt(v_hat) + ADAM_EPS) + ADAM_WD * w)
    return delta, m, v


def reference(x, mix_norm, ffn_norm, a_w_in, a_norm_v, a_w_s, a_b_s, a_w_out, kv_norm, w_kv, b_kv, b_w_q, b_b_q, b_sinks, b_w_o, b_b_o, rel_bias, ffn_w_gate, ffn_w_up, ffn_w_down, final_norm, loss_target, m_mix_norm, m_ffn_norm, m_a_w_in, m_a_norm_v, m_a_w_s, m_a_b_s, m_a_w_out, m_kv_norm, m_w_kv, m_b_kv, m_b_w_q, m_b_b_q, m_b_sinks, m_b_w_o, m_b_b_o, m_rel_bias, m_ffn_w_gate, m_ffn_w_up, m_ffn_w_down, m_final_norm, v_mix_norm, v_ffn_norm, v_a_w_in, v_a_norm_v, v_a_w_s, v_a_b_s, v_a_w_out, v_kv_norm, v_w_kv, v_b_kv, v_b_w_q, v_b_b_q, v_b_sinks, v_b_w_o, v_b_b_o, v_rel_bias, v_ffn_w_gate, v_ffn_w_up, v_ffn_w_down, v_final_norm):
    given = dict(x=x, mix_norm=mix_norm, ffn_norm=ffn_norm, a_w_in=a_w_in, a_norm_v=a_norm_v, a_w_s=a_w_s, a_b_s=a_b_s, a_w_out=a_w_out, kv_norm=kv_norm, w_kv=w_kv, b_kv=b_kv, b_w_q=b_w_q, b_b_q=b_b_q, b_sinks=b_sinks, b_w_o=b_w_o, b_b_o=b_b_o, rel_bias=rel_bias, ffn_w_gate=ffn_w_gate, ffn_w_up=ffn_w_up, ffn_w_down=ffn_w_down, final_norm=final_norm, loss_target=loss_target, m_mix_norm=m_mix_norm, m_ffn_norm=m_ffn_norm, m_a_w_in=m_a_w_in, m_a_norm_v=m_a_norm_v, m_a_w_s=m_a_w_s, m_a_b_s=m_a_b_s, m_a_w_out=m_a_w_out, m_kv_norm=m_kv_norm, m_w_kv=m_w_kv, m_b_kv=m_b_kv, m_b_w_q=m_b_w_q, m_b_b_q=m_b_b_q, m_b_sinks=m_b_sinks, m_b_w_o=m_b_w_o, m_b_b_o=m_b_b_o, m_rel_bias=m_rel_bias, m_ffn_w_gate=m_ffn_w_gate, m_ffn_w_up=m_ffn_w_up, m_ffn_w_down=m_ffn_w_down, m_final_norm=m_final_norm, v_mix_norm=v_mix_norm, v_ffn_norm=v_ffn_norm, v_a_w_in=v_a_w_in, v_a_norm_v=v_a_norm_v, v_a_w_s=v_a_w_s, v_a_b_s=v_a_b_s, v_a_w_out=v_a_w_out, v_kv_norm=v_kv_norm, v_w_kv=v_w_kv, v_b_kv=v_b_kv, v_b_w_q=v_b_w_q, v_b_b_q=v_b_b_q, v_b_sinks=v_b_sinks, v_b_w_o=v_b_w_o, v_b_b_o=v_b_b_o, v_rel_bias=v_rel_bias, v_ffn_w_gate=v_ffn_w_gate, v_ffn_w_up=v_ffn_w_up, v_ffn_w_down=v_ffn_w_down, v_final_norm=v_final_norm)
    weights = {n: given[n] for n in TWIN_WEIGHTS}
    shared = {n: given[n] for n in SHARED_INPUTS}
    per_example = {n: given[n] for n in ['x']}
    grad_fn = _jax.value_and_grad(_loss, argnums=(0, 1))

    def one_microbatch(ex, loss_target):
        ex = dict(ex)
        diff = ex.pop(TWIN_DIFF_INPUT)
        return grad_fn(weights, diff, {**shared, **ex}, loss_target)

    if N_MICROBATCH == 1:
        loss, (grad_w, grad_x) = one_microbatch(per_example, given["loss_target"])
    else:
        def body(carry, xs):
            loss_sum, grad_sum = carry
            l_k, (gw_k, gx_k) = one_microbatch(xs[0], xs[1])
            with _jax.named_scope("update"):
                return (loss_sum + l_k, _jax.tree.map(_jnp.add, grad_sum, gw_k)), gx_k

        init = (_jnp.zeros((), _jnp.float32), _jax.tree.map(_jnp.zeros_like, weights))
        (loss, grad_w), grad_x = _jax.lax.scan(body, init, (per_example, given["loss_target"]))
    with _jax.named_scope("update"):
        delta_w, new_m, new_v = {}, {}, {}
        for n in TWIN_WEIGHTS:
            delta_w[n], new_m[n], new_v[n] = _adamw(weights[n], grad_w[n], given["m_" + n], given["v_" + n])
    return (loss, grad_x, *[grad_w[n] for n in TWIN_WEIGHTS], *[delta_w[n] for n in TWIN_WEIGHTS],
            *[new_m[n] for n in TWIN_WEIGHTS], *[new_v[n] for n in TWIN_WEIGHTS])
```

```python
import math

import numpy as np
import jax
import jax.numpy as jnp
from jax import lax
from jax.experimental import pallas as pl
from jax.experimental.pallas import tpu as pltpu

F32 = jnp.float32
BF16 = jnp.bfloat16
AXES = ("x", "y", "c")
NDEV = 8
NCHIP = 4
CHUNK = 128
GROUPS = 8
HEAD_DIM = 64
KV_GROUP = 8
BLOCK = 128
N_BUCKETS = 32
MAX_DISTANCE = 128
RMS_EPS = 1e-5
NEG_INF = -1e30
ADAM_LR, ADAM_B1, ADAM_B2, ADAM_EPS, ADAM_WD, ADAM_STEP = 0.001, 0.9, 0.999, 1e-08, 0.01, 10
VMEM_LIMIT_BYTES = 56 * 1024 * 1024

NN = (((1,), (0,)), ((), ()))
NT = (((1,), (1,)), ((), ()))
TN = (((0,), (0,)), ((), ()))
ANY = pl.BlockSpec(memory_space=pl.ANY)
MESH = pl.DeviceIdType.MESH


def _pcall(body, *, name, out_shape, in_specs, out_specs, grid=(), scratch=(), aliases=None, prefetch=0):
    params = dict(vmem_limit_bytes=VMEM_LIMIT_BYTES)
    if grid:
        params["dimension_semantics"] = ("arbitrary",) * len(grid)
    kw = dict(name=name, out_shape=out_shape, compiler_params=pltpu.CompilerParams(**params),
              input_output_aliases=aliases or {})
    if prefetch:
        kw["grid_spec"] = pltpu.PrefetchScalarGridSpec(num_scalar_prefetch=prefetch, grid=grid, in_specs=in_specs,
                                                       out_specs=out_specs, scratch_shapes=list(scratch))
    else:
        kw.update(grid=grid, in_specs=in_specs, out_specs=out_specs, scratch_shapes=list(scratch))
    return pl.pallas_call(body, **kw)


def _sds(shape, dtype):
    return jax.ShapeDtypeStruct(tuple(shape), dtype)


def _position():
    x, y, c = lax.axis_index("x"), lax.axis_index("y"), lax.axis_index("c")
    chips = [(1 - x, y), (x, 1 - y), (1 - x, 1 - y)]
    return x, y, c, chips


def _all_gather(name, shards):
    n = len(shards)

    def body(*refs):
        src, dst = refs[:n], refs[n:2 * n]
        send_sems, recv_sems, local_sems = refs[2 * n:]
        x, y, c, chips = _position()
        me, sibling = (x, y, c), (x, y, 1 - c)

        def copy(t, k, block, to, from_shard=False):
            slot = dst[t].at[4 * block[0] + 2 * block[1] + block[2]]
            return pltpu.make_async_remote_copy(
                src_ref=src[t] if from_shard else slot, dst_ref=slot,
                send_sem=send_sems.at[t, k], recv_sem=recv_sems.at[t, k], device_id=to, device_id_type=MESH)

        mine = [pltpu.make_async_copy(src[t], dst[t].at[4 * x + 2 * y + c], local_sems.at[t]) for t in range(n)]
        first, passed = [], []
        for t in range(n):
            mine[t].start()
            first.append(copy(t, 0, me, sibling, True))
            first += [copy(t, 1 + j, me, (*chip, c), True) for j, chip in enumerate(chips)]
        for cp in first:
            cp.start()
        for j, chip in enumerate(chips):
            for t in range(n):
                copy(t, 1 + j, (*chip, c), me).wait_recv()
                fwd = copy(t, 4 + j, (*chip, c), sibling)
                fwd.start()
                passed.append(fwd)
        for t in range(n):
            copy(t, 0, sibling, me).wait_recv()
            for j, chip in enumerate(chips):
                copy(t, 4 + j, (*chip, 1 - c), me).wait_recv()
        for cp in first + passed:
            cp.wait_send()
        for t in range(n):
            mine[t].wait()

    outs = _pcall(
        body, name=name, out_shape=[_sds((NDEV,) + s.shape, s.dtype) for s in shards],
        in_specs=[ANY] * n, out_specs=[ANY] * n,
        scratch=[pltpu.SemaphoreType.DMA((n, 7)), pltpu.SemaphoreType.DMA((n, 7)), pltpu.SemaphoreType.DMA((n,))],
    )(*shards)
    return list(outs)


def _sibling_exchange(name, grads):
    n = len(grads)

    def body(*refs):
        src, dst = refs[:n], refs[n:2 * n]
        send_sems, recv_sems = refs[2 * n:]
        x, y, c, _ = _position()
        copies = []
        for t in range(n):
            for k in range(NCHIP):
                copies.append(pltpu.make_async_remote_copy(
                    src_ref=src[t].at[2 * k + (1 - c)], dst_ref=dst[t].at[k],
                    send_sem=send_sems.at[t, k], recv_sem=recv_sems.at[t, k],
                    device_id=(x, y, 1 - c), device_id_type=MESH))
        for cp in copies:
            cp.start()
        for cp in copies:
            cp.wait()

    outs = _pcall(
        body, name=name, out_shape=[_sds((NCHIP,) + g.shape[1:], g.dtype) for g in grads],
        in_specs=[ANY] * n, out_specs=[ANY] * n,
        scratch=[pltpu.SemaphoreType.DMA((n, NCHIP)), pltpu.SemaphoreType.DMA((n, NCHIP))],
    )(*grads)
    return list(outs)


def _chip_exchange(name, parts):
    n = len(parts)

    def body(*refs):
        src, dst = refs[:n], refs[n:2 * n]
        send_sems, recv_sems, local_sems = refs[2 * n:]
        x, y, c, chips = _position()
        my_chip = 2 * x + y
        local = [pltpu.make_async_copy(src[t].at[my_chip], dst[t].at[my_chip], local_sems.at[t]) for t in range(n)]
        copies = []
        for t in range(n):
            local[t].start()
            for j, chip in enumerate(chips):
                copies.append(pltpu.make_async_remote_copy(
                    src_ref=src[t].at[2 * chip[0] + chip[1]], dst_ref=dst[t].at[my_chip],
                    send_sem=send_sems.at[t, j], recv_sem=recv_sems.at[t, j],
                    device_id=(*chip, c), device_id_type=MESH))
        for cp in copies:
            cp.start()
        for t in range(n):
            for j, chip in enumerate(chips):
                pltpu.make_async_remote_copy(
                    src_ref=src[t].at[my_chip], dst_ref=dst[t].at[2 * chip[0] + chip[1]],
                    send_sem=send_sems.at[t, j], recv_sem=recv_sems.at[t, j],
                    device_id=(*chip, c), device_id_type=MESH).wait_recv()
        for cp in copies:
            cp.wait_send()
        for t in range(n):
            local[t].wait()

    outs = _pcall(
        body, name=name, out_shape=[_sds(p.shape, p.dtype) for p in parts],
        in_specs=[ANY] * n, out_specs=[ANY] * n,
        scratch=[pltpu.SemaphoreType.DMA((n, 3)), pltpu.SemaphoreType.DMA((n, 3)), pltpu.SemaphoreType.DMA((n,))],
    )(*parts)
    return list(outs)


def _pair_sum(name, grad, recv, core):
    _, r, w = grad.shape
    tr = _row_tile(r, w)
    g4 = grad.reshape(NCHIP, 2, r, w)

    def body(core_ref, g_ref, r_ref, o_ref):
        o_ref[...] = (g_ref[...].astype(F32) + r_ref[...].astype(F32)).astype(o_ref.dtype)

    return _pcall(
        body, name=name, out_shape=_sds((NCHIP, r, w), grad.dtype), grid=(NCHIP, r // tr), prefetch=1,
        in_specs=[pl.BlockSpec((None, None, tr, w), lambda k, i, cr: (k, cr[0], i, 0)),
                  pl.BlockSpec((None, tr, w), lambda k, i, cr: (k, i, 0))],
        out_specs=pl.BlockSpec((None, tr, w), lambda k, i, cr: (k, i, 0)),
    )(core, g4, recv)


def _row_tile(rows, width, budget=2 * 1024 * 1024):
    best = None
    for t in range(16, rows + 1, 16):
        if rows % t == 0 and t * width * 4 <= budget:
            best = t
    assert best is not None, (rows, width)
    return best


def _gemm(name, grid, operands, prods, extras, outs, epilogue, *, nk=1, acc_shape=None, aliases=None, separate=False):
    n_op, n_ex, n_out = len(operands), len(extras), len(outs)

    def body(*refs):
        ops, ex, out_refs = refs[:n_op], refs[n_op:n_op + n_ex], refs[n_op + n_ex:n_op + n_ex + n_out]
        parts = []
        for pr in prods:
            a, b = ops[pr[0]], ops[pr[1]]
            av = pr[3](a) if len(pr) > 3 and pr[3] else a[...]
            bv = pr[4](b) if len(pr) > 4 and pr[4] else b[...]
            parts.append(lax.dot_general(av, bv, pr[2], preferred_element_type=F32))
        if separate:
            epilogue(parts, ex, out_refs)
            return
        part = parts[0]
        for p in parts[1:]:
            part = part + p
        if nk == 1:
            epilogue(part, ex, out_refs)
        else:
            acc = refs[-1]
            k = pl.program_id(len(grid) - 1)

            @pl.when(k == 0)
            def _():
                acc[...] = part

            @pl.when(k > 0)
            def _():
                acc[...] += part

            @pl.when(k == nk - 1)
            def _():
                epilogue(acc[...], ex, out_refs)

    res = _pcall(
        body, name=name, out_shape=[o[0] for o in outs], grid=grid,
        in_specs=[o[1] for o in operands] + [e[1] for e in extras], out_specs=[o[1] for o in outs],
        scratch=[pltpu.VMEM(acc_shape, F32)] if nk > 1 else [], aliases=aliases,
    )(*[o[0] for o in operands], *[e[0] for e in extras])
    return list(res)


def _store(dtype=None):
    def ep(acc, ex, outs):
        outs[0][...] = acc.astype(outs[0].dtype)
    return ep


def _store_add_extra(acc, ex, outs):
    v = acc
    for e in ex:
        v = v + e[...]
    outs[0][...] = v.astype(outs[0].dtype)


def _stacked(ref):
    b = ref[...]
    return b.reshape(b.shape[0] * b.shape[1], b.shape[2])


def _gelu_parts(z):
    c = math.sqrt(2.0 / math.pi)
    t = jnp.tanh(c * (z + 0.044715 * (z * z * z)))
    val = 0.5 * z * (1.0 + t)
    grad = 0.5 * (1.0 + t) + 0.5 * z * (1.0 - t * t) * (c * (1.0 + 3.0 * 0.044715 * z * z))
    return val, grad


def _rms_fwd(name, h, g):
    s, d = h.shape
    tr = _row_tile(s, d)

    def body(h_ref, g_ref, o_ref):
        hv = h_ref[...]
        r = lax.rsqrt(jnp.mean(hv * hv, axis=-1, keepdims=True) + RMS_EPS)
        o_ref[...] = (hv * r * g_ref[...]).astype(o_ref.dtype)

    return _pcall(
        body, name=name, out_shape=_sds((s, d), BF16), grid=(s // tr,),
        in_specs=[pl.BlockSpec((tr, d), lambda i: (i, 0)), pl.BlockSpec((1, d), lambda i: (0, 0))],
        out_specs=pl.BlockSpec((tr, d), lambda i: (i, 0)),
    )(h, g.reshape(1, d))


def _accumulate(ref, val, first):
    @pl.when(first)
    def _():
        ref[...] = val

    @pl.when(jnp.logical_not(first))
    def _():
        ref[...] += val


def _rms_bwd(name, h, g, dy, res):
    s, d = h.shape
    tr = _row_tile(s, d, budget=1024 * 1024)

    def body(h_ref, g_ref, dy_ref, res_ref, dh_ref, dhb_ref, dg_ref, cs_ref):
        hv = h_ref[...]
        r = lax.rsqrt(jnp.mean(hv * hv, axis=-1, keepdims=True) + RMS_EPS)
        xhat = hv * r
        dyv = dy_ref[...]
        dxh = dyv * g_ref[...]
        dh = res_ref[...] + r * (dxh - xhat * jnp.mean(dxh * xhat, axis=-1, keepdims=True))
        dh_ref[...] = dh
        dhb_ref[...] = dh.astype(BF16)
        first = pl.program_id(0) == 0
        _accumulate(dg_ref, jnp.sum(dyv * xhat, axis=0, keepdims=True), first)
        _accumulate(cs_ref, jnp.sum(dh, axis=0, keepdims=True), first)

    row = pl.BlockSpec((tr, d), lambda i: (i, 0))
    vec = pl.BlockSpec((1, d), lambda i: (0, 0))
    return _pcall(
        body, name=name, out_shape=[_sds((s, d), F32), _sds((s, d), BF16), _sds((1, d), F32), _sds((1, d), F32)],
        grid=(s // tr,), in_specs=[row, vec, row, row], out_specs=[row, row, vec, vec],
    )(h, g.reshape(1, d), dy, res)


def _loss_bwd(name, h, g, target):
    s, d = h.shape
    tr = _row_tile(s, d, budget=1024 * 1024)

    def body(h_ref, g_ref, t_ref, loss_ref, dh_ref, dhb_ref, dg_ref):
        hv = h_ref[...]
        r = lax.rsqrt(jnp.mean(hv * hv, axis=-1, keepdims=True) + RMS_EPS)
        xhat = hv * r
        diff = xhat * g_ref[...] - t_ref[...]
        part = jnp.sum(jnp.sum(diff * diff, axis=1, keepdims=True), axis=0, keepdims=True) * (0.5 / d)
        dyv = diff * (1.0 / d)
        dxh = dyv * g_ref[...]
        dh = r * (dxh - xhat * jnp.mean(dxh * xhat, axis=-1, keepdims=True))
        dh_ref[...] = dh
        dhb_ref[...] = dh.astype(BF16)
        first = pl.program_id(0) == 0
        _accumulate(loss_ref, part, first)
        _accumulate(dg_ref, jnp.sum(dyv * xhat, axis=0, keepdims=True), first)

    row = pl.BlockSpec((tr, d), lambda i: (i, 0))
    vec = pl.BlockSpec((1, d), lambda i: (0, 0))
    one = pl.BlockSpec((1, 1), lambda i: (0, 0))
    return _pcall(
        body, name=name, out_shape=[_sds((1, 1), F32), _sds((s, d), F32), _sds((s, d), BF16), _sds((1, d), F32)],
        grid=(s // tr,), in_specs=[row, vec, row], out_specs=[one, row, row, vec],
    )(h, g.reshape(1, d), target)


def _tril_mask():
    return lax.broadcasted_iota(jnp.int32, (CHUNK, CHUNK), 0) >= lax.broadcasted_iota(jnp.int32, (CHUNK, CHUNK), 1)


def _gmlp_fwd(name, zp, gv, ws, bst):
    s, d2 = zp.shape
    d = d2 // 2
    gw = d // GROUPS

    def body(zp_ref, gv_ref, ws_ref, bst_ref, o_ref):
        u, _ = _gelu_parts(zp_ref[:, :d])
        v, _ = _gelu_parts(zp_ref[:, d:])
        rv = lax.rsqrt(jnp.mean(v * v, axis=-1, keepdims=True) + RMS_EPS)
        vn = (v * rv * gv_ref[...]).astype(BF16)
        tril = _tril_mask()
        for g in range(GROUPS):
            sl = slice(g * gw, (g + 1) * gw)
            wc = jnp.where(tril, ws_ref[g], 0.0).astype(BF16)
            sg = jnp.dot(wc, vn[:, sl], preferred_element_type=F32) + bst_ref[:, g:g + 1]
            o_ref[:, sl] = (u[:, sl] * sg).astype(o_ref.dtype)

    return _pcall(
        body, name=name, out_shape=_sds((s, d), BF16), grid=(s // CHUNK,),
        in_specs=[pl.BlockSpec((CHUNK, d2), lambda i: (i, 0)), pl.BlockSpec((1, d), lambda i: (0, 0)),
                  pl.BlockSpec((GROUPS, CHUNK, CHUNK), lambda i: (0, 0, 0)),
                  pl.BlockSpec((CHUNK, GROUPS), lambda i: (0, 0))],
        out_specs=pl.BlockSpec((CHUNK, d), lambda i: (i, 0)),
    )(zp, gv, ws, bst)


def _gmlp_bwd(name, zp, dgated, gv, ws, bst):
    s, d2 = zp.shape
    d = d2 // 2
    gw = d // GROUPS

    def body(zp_ref, dg_ref, gv_ref, ws_ref, bst_ref, dzp_ref, dws_ref, dbs_ref, dgv_ref, dvn_ref):
        u, gu = _gelu_parts(zp_ref[:, :d])
        v, gvv = _gelu_parts(zp_ref[:, d:])
        rv = lax.rsqrt(jnp.mean(v * v, axis=-1, keepdims=True) + RMS_EPS)
        vhat = v * rv
        vn = (vhat * gv_ref[...]).astype(BF16)
        tril = _tril_mask()
        first = pl.program_id(0) == 0
        ones = jnp.ones((8, gw), F32)

        @pl.when(first)
        def _():
            dws_ref[...] = jnp.zeros_like(dws_ref)
            dbs_ref[...] = jnp.zeros_like(dbs_ref)

        for g in range(GROUPS):
            sl = slice(g * gw, (g + 1) * gw)
            wc = jnp.where(tril, ws_ref[g], 0.0).astype(BF16)
            sg = jnp.dot(wc, vn[:, sl], preferred_element_type=F32) + bst_ref[:, g:g + 1]
            dgs = dg_ref[:, sl]
            ds = dgs * u[:, sl]
            dsb = ds.astype(BF16)
            dzp_ref[:, sl] = (dgs * sg * gu[:, sl]).astype(dzp_ref.dtype)
            dvn_ref[:, sl] = lax.dot_general(wc, dsb, TN, preferred_element_type=F32)
            dw = lax.dot_general(dsb, vn[:, sl], NT, preferred_element_type=F32)
            dws_ref[g] += jnp.where(tril, dw, 0.0)
            dbs_ref[g] += lax.dot_general(ones, ds, NT, preferred_element_type=F32, precision=lax.Precision.HIGHEST)
        dvn = dvn_ref[...]
        dvh = dvn * gv_ref[...]
        dv = rv * (dvh - vhat * jnp.mean(dvh * vhat, axis=-1, keepdims=True))
        dzp_ref[:, d:] = (dv * gvv).astype(dzp_ref.dtype)
        _accumulate(dgv_ref, jnp.sum(dvn * vhat, axis=0, keepdims=True), first)

    return _pcall(
        body, name=name,
        out_shape=[_sds((s, d2), BF16), _sds((GROUPS, CHUNK, CHUNK), F32), _sds((GROUPS, 8, CHUNK), F32),
                   _sds((1, d), F32)],
        grid=(s // CHUNK,),
        in_specs=[pl.BlockSpec((CHUNK, d2), lambda i: (i, 0)), pl.BlockSpec((CHUNK, d), lambda i: (i, 0)),
                  pl.BlockSpec((1, d), lambda i: (0, 0)), pl.BlockSpec((GROUPS, CHUNK, CHUNK), lambda i: (0, 0, 0)),
                  pl.BlockSpec((CHUNK, GROUPS), lambda i: (0, 0))],
        out_specs=[pl.BlockSpec((CHUNK, d2), lambda i: (i, 0)),
                   pl.BlockSpec((GROUPS, CHUNK, CHUNK), lambda i: (0, 0, 0)),
                   pl.BlockSpec((GROUPS, 8, CHUNK), lambda i: (0, 0, 0)), pl.BlockSpec((1, d), lambda i: (0, 0))],
        scratch=[pltpu.VMEM((CHUNK, d), F32)],
    )(zp, dgated, gv, ws, bst)


def _bucket_table():
    dist = np.arange(BLOCK)[:, None] + BLOCK - np.arange(2 * BLOCK)[None, :]
    in_window = (dist >= 0) & (dist < BLOCK)
    dd = np.clip(dist, 0, None)
    max_exact = N_BUCKETS // 2
    dl = np.maximum(dd, 1).astype(np.float32)
    large = max_exact + (np.log(dl / np.float32(max_exact)) / np.float32(math.log(MAX_DISTANCE / max_exact))
                         * np.float32(N_BUCKETS - max_exact)).astype(np.int32)
    large = np.minimum(large, N_BUCKETS - 1)
    bucket = np.where(dd < max_exact, dd, large)
    return np.where(in_window, bucket, -1).astype(np.int32).reshape(1, -1)


def _bias_table(name, rel_bias_t, buckets):
    nh = rel_bias_t.shape[0]
    p = buckets.shape[1]
    tp = 4096

    def body(rb_ref, bk_ref, o_ref):
        bk = bk_ref[...]
        onehot = (lax.broadcasted_iota(jnp.int32, (N_BUCKETS, tp), 0) == bk).astype(F32)
        val = jnp.dot(rb_ref[...], onehot, preferred_element_type=F32, precision=lax.Precision.HIGHEST)
        o_ref[...] = jnp.where(bk >= 0, val, NEG_INF)

    return _pcall(
        body, name=name, out_shape=_sds((nh, p), F32), grid=(p // tp,),
        in_specs=[pl.BlockSpec((nh, N_BUCKETS), lambda i: (0, 0)), pl.BlockSpec((1, tp), lambda i: (0, i))],
        out_specs=pl.BlockSpec((nh, tp), lambda i: (0, i)),
    )(rel_bias_t, buckets)


def _bias_grad(name, dbiases, buckets):
    nh, p = dbiases[0].shape
    n = len(dbiases)
    tp = 4096

    def body(*refs):
        bk_ref, o_ref = refs[n], refs[n + 1]
        onehot = (lax.broadcasted_iota(jnp.int32, (N_BUCKETS, tp), 0) == bk_ref[...]).astype(F32)
        db = refs[0][...]
        for r in refs[1:n]:
            db = db + r[...]
        part = lax.dot_general(onehot, db, NT, preferred_element_type=F32, precision=lax.Precision.HIGHEST)
        _accumulate(o_ref, part, pl.program_id(0) == 0)

    return _pcall(
        body, name=name, out_shape=_sds((N_BUCKETS, nh), F32), grid=(p // tp,),
        in_specs=[pl.BlockSpec((nh, tp), lambda i: (0, i))] * n + [pl.BlockSpec((1, tp), lambda i: (0, i))],
        out_specs=pl.BlockSpec((N_BUCKETS, nh), lambda i: (0, 0)),
    )(*dbiases, buckets)


def _attn_probs(qh, kp, kc, bias, sink, prev_penalty):
    sp = lax.dot_general(qh, kp, NT, preferred_element_type=F32) * 0.125 + bias[:, :BLOCK] + prev_penalty
    sc = lax.dot_general(qh, kc, NT, preferred_element_type=F32) * 0.125 + bias[:, BLOCK:]
    m = jnp.maximum(jnp.maximum(jnp.max(sp, axis=-1, keepdims=True), jnp.max(sc, axis=-1, keepdims=True)), sink)
    pp, pc = jnp.exp(sp - m), jnp.exp(sc - m)
    es = jnp.exp(sink - m)
    inv = 1.0 / (jnp.sum(pp, axis=-1, keepdims=True) + jnp.sum(pc, axis=-1, keepdims=True) + es)
    return pp * inv, pc * inv, es * inv


def _attn_specs(nkv):
    gq = KV_GROUP * HEAD_DIM
    q_spec = pl.BlockSpec((BLOCK, gq), lambda kh, i: (i, kh))
    prev = pl.BlockSpec((None, BLOCK, HEAD_DIM), lambda kh, i: (kh, jnp.maximum(i - 1, 0), 0))
    cur = pl.BlockSpec((None, BLOCK, HEAD_DIM), lambda kh, i: (kh, i, 0))
    bias = pl.BlockSpec((KV_GROUP, BLOCK, 2 * BLOCK), lambda kh, i: (kh, 0, 0))
    smem = pl.BlockSpec(memory_space=pltpu.SMEM)
    return q_spec, prev, cur, bias, smem


def _attn_fwd(name, q, k, v, bias, sinks):
    s, dq = q.shape
    nkv = k.shape[0]
    q_spec, prev, cur, bias_spec, smem = _attn_specs(nkv)

    def body(q_ref, kp_ref, kc_ref, vp_ref, vc_ref, b_ref, s_ref, o_ref):
        kh, i = pl.program_id(0), pl.program_id(1)
        penalty = jnp.where(i > 0, 0.0, NEG_INF).astype(F32)
        kp, kc, vp, vc = kp_ref[...], kc_ref[...], vp_ref[...], vc_ref[...]
        for hh in range(KV_GROUP):
            sl = slice(hh * HEAD_DIM, (hh + 1) * HEAD_DIM)
            pp, pc, _ = _attn_probs(q_ref[:, sl], kp, kc, b_ref[hh], s_ref[kh * KV_GROUP + hh], penalty)
            o = (jnp.dot(pp.astype(BF16), vp, preferred_element_type=F32)
                 + jnp.dot(pc.astype(BF16), vc, preferred_element_type=F32))
            o_ref[:, sl] = o.astype(o_ref.dtype)

    return _pcall(
        body, name=name, out_shape=_sds((s, dq), BF16), grid=(nkv, s // BLOCK),
        in_specs=[q_spec, prev, cur, prev, cur, bias_spec, smem], out_specs=q_spec,
    )(q, k, k, v, v, bias, sinks)


def _attn_bwd(name, q, k, v, do, bias, sinks):
    s, dq = q.shape
    nkv = k.shape[0]
    gq = KV_GROUP * HEAD_DIM
    q_spec, prev, cur, bias_spec, smem = _attn_specs(nkv)

    def body(q_ref, do_ref, kp_ref, kc_ref, vp_ref, vc_ref, b_ref, s_ref,
             dq_ref, dbq_ref, dkc_ref, dkp_ref, dvc_ref, dvp_ref, dbias_ref, dsink_ref):
        kh, i = pl.program_id(0), pl.program_id(1)
        first = i == 0
        penalty = jnp.where(i > 0, 0.0, NEG_INF).astype(F32)
        kp, kc, vp, vc = kp_ref[...], kc_ref[...], vp_ref[...], vc_ref[...]

        @pl.when(first)
        def _():
            dbias_ref[...] = jnp.zeros_like(dbias_ref)
            dsink_ref[...] = jnp.zeros_like(dsink_ref)
            dbq_ref[...] = jnp.zeros_like(dbq_ref)

        dkc = dkp = dvc = dvp = None
        for hh in range(KV_GROUP):
            sl = slice(hh * HEAD_DIM, (hh + 1) * HEAD_DIM)
            qh, doh = q_ref[:, sl], do_ref[:, sl]
            pp, pc, ps = _attn_probs(qh, kp, kc, b_ref[hh], s_ref[kh * KV_GROUP + hh], penalty)
            dpp = lax.dot_general(doh, vp, NT, preferred_element_type=F32)
            dpc = lax.dot_general(doh, vc, NT, preferred_element_type=F32)
            delta = jnp.sum(pp * dpp, axis=-1, keepdims=True) + jnp.sum(pc * dpc, axis=-1, keepdims=True)
            dsp, dsc = pp * (dpp - delta), pc * (dpc - delta)
            dspb, dscb = dsp.astype(BF16), dsc.astype(BF16)
            dqh = (jnp.dot(dspb, kp, preferred_element_type=F32) + jnp.dot(dscb, kc, preferred_element_type=F32)) * 0.125
            dq_ref[:, sl] = dqh.astype(dq_ref.dtype)
            dbq_ref[:, sl] += jnp.sum(dqh, axis=0, keepdims=True)
            terms = (lax.dot_general(dscb, qh, TN, preferred_element_type=F32) * 0.125,
                     lax.dot_general(dspb, qh, TN, preferred_element_type=F32) * 0.125,
                     lax.dot_general(pc.astype(BF16), doh, TN, preferred_element_type=F32),
                     lax.dot_general(pp.astype(BF16), doh, TN, preferred_element_type=F32))
            if hh == 0:
                dkc, dkp, dvc, dvp = terms
            else:
                dkc, dkp, dvc, dvp = dkc + terms[0], dkp + terms[1], dvc + terms[2], dvp + terms[3]
            dbias_ref[hh, :, :BLOCK] += dsp
            dbias_ref[hh, :, BLOCK:] += dsc
            dsink_ref[:, hh:hh + 1] += jnp.sum(-(ps * delta), axis=0, keepdims=True)
        dkc_ref[...], dkp_ref[...], dvc_ref[...], dvp_ref[...] = dkc, dkp, dvc, dvp

    kv_out = _sds((nkv, s, HEAD_DIM), F32)
    return _pcall(
        body, name=name,
        out_shape=[_sds((s, dq), BF16), _sds((1, dq), F32), kv_out, kv_out, kv_out, kv_out,
                   _sds((nkv * KV_GROUP, BLOCK, 2 * BLOCK), F32), _sds((nkv, 1, KV_GROUP), F32)],
        grid=(nkv, s // BLOCK),
        in_specs=[q_spec, q_spec, prev, cur, prev, cur, bias_spec, smem],
        out_specs=[q_spec, pl.BlockSpec((1, gq), lambda kh, i: (0, kh)), cur, cur, cur, cur, bias_spec,
                   pl.BlockSpec((None, 1, KV_GROUP), lambda kh, i: (kh, 0, 0))],
    )(q, do, k, k, v, v, bias, sinks)


def _kv_grad(name, parts):
    nkv, s, _ = parts[0][0].shape
    nb = s // BLOCK
    w = 2 * nkv * HEAD_DIM
    n = len(parts)

    def body(*refs):
        o_ref, cs_ref = refs[4 * n], refs[4 * n + 1]
        i = pl.program_id(0)
        keep = jnp.where(i < nb - 1, 1.0, 0.0).astype(F32)

        @pl.when(i == 0)
        def _():
            cs_ref[...] = jnp.zeros_like(cs_ref)

        for which in range(2):
            for hh in range(nkv):
                val = None
                for l in range(n):
                    cur_ref, nxt_ref = refs[4 * l + 2 * which], refs[4 * l + 2 * which + 1]
                    t = cur_ref[hh] + keep * nxt_ref[hh]
                    val = t if val is None else val + t
                sl = slice((which * nkv + hh) * HEAD_DIM, (which * nkv + hh + 1) * HEAD_DIM)
                o_ref[:, sl] = val.astype(o_ref.dtype)
                cs_ref[:, sl] += jnp.sum(val, axis=0, keepdims=True)

    cur = pl.BlockSpec((nkv, BLOCK, HEAD_DIM), lambda i: (0, i, 0))
    nxt = pl.BlockSpec((nkv, BLOCK, HEAD_DIM), lambda i: (0, jnp.minimum(i + 1, nb - 1), 0))
    flat = [a for p in parts for a in p]
    return _pcall(
        body, name=name, out_shape=[_sds((s, w), BF16), _sds((1, w), F32)], grid=(nb,),
        in_specs=[cur, nxt] * (2 * n),
        out_specs=[pl.BlockSpec((BLOCK, w), lambda i: (i, 0)), pl.BlockSpec((1, w), lambda i: (0, 0))],
    )(*flat)


def _adamw_math(w, g, m, v):
    m = ADAM_B1 * m + (1.0 - ADAM_B1) * g
    v = ADAM_B2 * v + (1.0 - ADAM_B2) * (g * g)
    m_hat = m / (1.0 - ADAM_B1 ** ADAM_STEP)
    v_hat = v / (1.0 - ADAM_B2 ** ADAM_STEP)
    delta = -ADAM_LR * (m_hat / (jnp.sqrt(v_hat) + ADAM_EPS) + ADAM_WD * w)
    return delta, m, v


def _adamw_shard(name, w, m, v, parts, row0, stride):
    nl, r, wd = w.shape
    tr = _row_tile(r, wd, budget=1024 * 1024)
    assert row0 % tr == 0 and stride % tr == 0

    def body(w_ref, m_ref, v_ref, p_ref, g_ref, d_ref, nm_ref, nv_ref):
        g = p_ref[0].astype(F32)
        for k in range(1, NCHIP):
            g = g + p_ref[k].astype(F32)
        delta, nm, nv = _adamw_math(w_ref[...], g, m_ref[...], v_ref[...])
        g_ref[...], d_ref[...], nm_ref[...], nv_ref[...] = g, delta, nm, nv

    par = pl.BlockSpec((None, tr, wd), lambda l, i: (l, i, 0))
    out = _sds(w.shape, F32)
    return _pcall(
        body, name=name, out_shape=[out, out, out, out], grid=(nl, r // tr),
        in_specs=[par, par, par,
                  pl.BlockSpec((NCHIP, tr, wd), lambda l, i: (0, (row0 + l * stride) // tr + i, 0))],
        out_specs=[par, par, par, par],
    )(w, m, v, parts)


def _sum_devices(name, gathered):
    _, r, wd = gathered.shape

    def body(g_ref, o_ref):
        acc = g_ref[0]
        for k in range(1, NDEV):
            acc = acc + g_ref[k]
        o_ref[...] = acc

    return _pcall(body, name=name, out_shape=_sds((r, wd), F32), grid=(1,),
                  in_specs=[pl.BlockSpec((NDEV, r, wd), lambda i: (0, 0, 0))],
                  out_specs=pl.BlockSpec((r, wd), lambda i: (0, 0)))(gathered)


def _adamw_flat(name, w, g, m, v):
    shape = w.shape

    def body(w_ref, g_ref, m_ref, v_ref, d_ref, nm_ref, nv_ref):
        d_ref[...], nm_ref[...], nv_ref[...] = _adamw_math(w_ref[...], g_ref[...], m_ref[...], v_ref[...])

    spec = pl.BlockSpec(shape, lambda i: (0, 0))
    out = _sds(shape, F32)
    return _pcall(body, name=name, out_shape=[out, out, out], grid=(1,), in_specs=[spec] * 4,
                  out_specs=[spec] * 3)(w, g, m, v)


def _cast_rows(name, src, buf, row0, stride=0):
    nl, r, wd = src.shape
    tr = _row_tile(r, wd)
    assert row0 % tr == 0 and stride % tr == 0

    def body(s_ref, b_ref, o_ref):
        o_ref[...] = s_ref[...].astype(o_ref.dtype)

    return _pcall(
        body, name=name, out_shape=_sds(buf.shape, buf.dtype), grid=(nl, r // tr),
        in_specs=[pl.BlockSpec((None, tr, wd), lambda l, i: (l, i, 0)), ANY],
        out_specs=pl.BlockSpec((tr, wd), lambda l, i: ((row0 + l * stride) // tr + i, 0)), aliases={1: 0},
    )(src, buf)


def _pack(arrays):
    rows = []
    for a in arrays:
        flat = a.reshape(-1).astype(F32)
        pad = (-flat.shape[0]) % 1024
        rows.append(jnp.pad(flat, (0, pad)).reshape(-1, 128))
    return jnp.concatenate(rows, axis=0)


def _unpack(packed, shapes):
    out, r = [], 0
    for shp in shapes:
        n = int(np.prod(shp))
        nr = (n + 1023) // 1024 * 8
        out.append(packed[r:r + nr].reshape(-1)[:n].reshape(shp))
        r += nr
    return out


def kernel(x, mix_norm, ffn_norm, a_w_in, a_norm_v, a_w_s, a_b_s, a_w_out, kv_norm, w_kv, b_kv, b_w_q, b_b_q, b_sinks, b_w_o, b_b_o, rel_bias, ffn_w_gate, ffn_w_up, ffn_w_down, final_norm, loss_target, m_mix_norm, m_ffn_norm, m_a_w_in, m_a_norm_v, m_a_w_s, m_a_b_s, m_a_w_out, m_kv_norm, m_w_kv, m_b_kv, m_b_w_q, m_b_b_q, m_b_sinks, m_b_w_o, m_b_b_o, m_rel_bias, m_ffn_w_gate, m_ffn_w_up, m_ffn_w_down, m_final_norm, v_mix_norm, v_ffn_norm, v_a_w_in, v_a_norm_v, v_a_w_s, v_a_b_s, v_a_w_out, v_kv_norm, v_w_kv, v_b_kv, v_b_w_q, v_b_b_q, v_b_sinks, v_b_w_o, v_b_b_o, v_rel_bias, v_ffn_w_gate, v_ffn_w_up, v_ffn_w_down, v_final_norm):
    _, S, D = x.shape
    LA, LB, L = a_w_in.shape[0], b_w_q.shape[0], ffn_w_gate.shape[0]
    F = ffn_w_gate.shape[2]
    DS = D // NDEV
    ZC = a_w_in.shape[2]
    KVW = w_kv.shape[1]
    NKV = KVW // (2 * HEAD_DIM)
    NH = D // HEAD_DIM
    assert ZC == KVW and ZC * NDEV == 2 * D and NH == NKV * KV_GROUP and S % BLOCK == 0
    TM = min(1024, S)
    TN_ = min(1024, D)
    TS = min(512, D)

    off_down = lambda l: l * F
    off_wout = lambda i: L * F + i * DS
    off_wq = lambda i: L * F + (LA + i) * DS
    off_wo = lambda i: L * F + (LA + LB + i) * DS
    R_D = L * F + (LA + 2 * LB) * DS
    off_gate = lambda l: (2 * l) * D
    off_up = lambda l: (2 * l + 1) * D
    R_F = 2 * L * D
    off_win = lambda i: i * D
    off_wkv = LA * D
    R_Z = LA * D + DS
    assert (L * F) % DS == 0

    core = lax.axis_index("c").astype(jnp.int32).reshape(1)
    me = 4 * lax.axis_index("x") + 2 * lax.axis_index("y") + lax.axis_index("c")

    sh_d = _cast_rows("cast_down", ffn_w_down, lax.empty((R_D, D), BF16), off_down(0), F)
    sh_d = _cast_rows("cast_wout", a_w_out, sh_d, off_wout(0), DS)
    sh_d = _cast_rows("cast_wq", b_w_q, sh_d, off_wq(0), DS)
    sh_d = _cast_rows("cast_wo", b_w_o, sh_d, off_wo(0), DS)
    sh_f = _cast_rows("cast_gate", ffn_w_gate, lax.empty((R_F, F), BF16), off_gate(0), 2 * D)
    sh_f = _cast_rows("cast_up", ffn_w_up, sh_f, off_up(0), 2 * D)
    sh_z = _cast_rows("cast_win", a_w_in, lax.empty((R_Z, ZC), BF16), off_win(0), D)
    sh_z = _cast_rows("cast_wkv", w_kv.reshape((1,) + w_kv.shape), sh_z, off_wkv)
    nv_rows = _pack([a_norm_v])
    wb_d, wb_f, wb_z, nv_all = _all_gather("gather_weights", [sh_d, sh_f, sh_z, nv_rows])
    norm_v = jnp.transpose(nv_all.reshape(NDEV, -1)[:, :LA * DS].reshape(NDEV, LA, DS), (1, 0, 2)).reshape(LA, D)

    gb_d = lax.empty((NDEV, R_D, D), BF16)
    gb_f = lax.empty((NDEV, R_F, F), BF16)
    gb_z = lax.empty((NDEV, R_Z, ZC), BF16)

    buckets = jnp.asarray(_bucket_table())
    bias = _bias_table("bias_table", rel_bias.T, buckets).reshape(NH, BLOCK, 2 * BLOCK)

    def rows_full(tm):
        return pl.BlockSpec((tm, D), lambda i, j: (i, 0))

    def tile(tm, tn):
        return pl.BlockSpec((tm, tn), lambda i, j: (i, j))

    def ffn_forward(l, h_mid, tag):
        xf = _rms_fwd(f"ffn_norm_fwd{tag}", h_mid, ffn_norm[l])

        def ep(parts, ex, outs):
            a, b = parts
            outs[0][0] = a.astype(BF16)
            outs[0][1] = b.astype(BF16)
            outs[1][...] = (a * jax.nn.sigmoid(a) * b).astype(BF16)

        ab, hid = _gemm(
            f"ffn_up{tag}", (S // TM, NDEV),
            [(xf, rows_full(TM)),
             (wb_f, pl.BlockSpec((None, D, F), lambda i, e: (e, off_gate(l) // D, 0))),
             (wb_f, pl.BlockSpec((None, D, F), lambda i, e: (e, off_up(l) // D, 0)))],
            [(0, 1, NN), (0, 2, NN)], [],
            [(_sds((2, NDEV, S, F), BF16), pl.BlockSpec((2, None, TM, F), lambda i, e: (0, e, i, 0))),
             (_sds((NDEV, S, F), BF16), pl.BlockSpec((None, TM, F), lambda i, e: (e, i, 0)))],
            ep, separate=True)
        (h_out,) = _gemm(
            f"ffn_down{tag}", (S // TM, D // TN_, NDEV),
            [(hid, pl.BlockSpec((None, TM, F), lambda i, j, e: (e, i, 0))),
             (wb_d, pl.BlockSpec((None, F, TN_), lambda i, j, e: (e, off_down(l) // F, j)))],
            [(0, 1, NN)], [(h_mid, pl.BlockSpec((TM, TN_), lambda i, j, e: (i, j)))],
            [(_sds((S, D), F32), pl.BlockSpec((TM, TN_), lambda i, j, e: (i, j)))],
            _store_add_extra, nk=NDEV, acc_shape=(TM, TN_))
        return dict(h_mid=h_mid, xf=xf, ab=ab, hid=hid), h_out

    def stacked_rows_gemm(name, a, row0, extras, ep, out_dtype):
        return _gemm(
            name, (S // TM, D // TN_),
            [(a, rows_full(TM)), (wb_d, pl.BlockSpec((NDEV, DS, TN_), lambda i, j: (0, row0 // DS, j)))],
            [(0, 1, NN, None, _stacked)], extras,
            [(_sds((S, D), out_dtype), tile(TM, TN_))], ep)[0]

    def back_rows_gemm(name, a, row0, out_dtype):
        return _gemm(
            name, (S // TM, NDEV),
            [(a, rows_full(TM)), (wb_d, pl.BlockSpec((None, DS, D), lambda i, e: (e, row0 // DS, 0)))],
            [(0, 1, NT)], [], [(_sds((S, D), out_dtype), pl.BlockSpec((TM, DS), lambda i, e: (i, e)))], _store())[0]

    def grad_rows_gemm(name, act, d_bf, buf, row0):
        return _gemm(
            name, (NDEV,),
            [(act, pl.BlockSpec((S, DS), lambda e: (0, e))), (d_bf, pl.BlockSpec((S, D), lambda e: (0, 0)))],
            [(0, 1, TN)], [(buf, ANY)],
            [(_sds(buf.shape, BF16), pl.BlockSpec((None, DS, D), lambda e: (e, row0 // DS, 0)))],
            _store(), aliases={2: 0})[0]

    vec_tile = pl.BlockSpec((1, TN_), lambda i, j: (0, j))

    saved = []
    h = x.reshape(S, D)
    k_heads = v_heads = hn = h_kv = None
    for layer in range(L):
        sv = dict(h_in=h)
        xn = _rms_fwd(f"mix_norm_fwd{layer}", h, mix_norm[layer])
        sv["xn"] = xn
        if layer < LA:
            i_a = layer
            (zp,) = _gemm(
                f"gmlp_in{layer}", (S // TM, NDEV),
                [(xn, rows_full(TM)), (wb_z, pl.BlockSpec((None, D, ZC), lambda i, e: (e, off_win(i_a) // D, 0)))],
                [(0, 1, NN)], [], [(_sds((S, 2 * D), F32), pl.BlockSpec((TM, ZC), lambda i, e: (i, e)))], _store())
            bst = a_b_s[i_a].T
            gated = _gmlp_fwd(f"gmlp_gate{layer}", zp, norm_v[i_a].reshape(1, D), a_w_s[i_a], bst)
            sv.update(zp=zp, gated=gated, bst=bst)
            h_mid = stacked_rows_gemm(f"gmlp_out{layer}", gated, off_wout(i_a), [(h, tile(TM, TN_))],
                                      _store_add_extra, F32)
        else:
            i_b = layer - LA
            q = stacked_rows_gemm(f"attn_q{layer}", xn, off_wq(i_b), [(b_b_q[i_b].reshape(1, D), vec_tile)],
                                  _store_add_extra, BF16)
            attn = _attn_fwd(f"attn_fwd{layer}", q, k_heads, v_heads, bias, b_sinks[i_b])
            sv.update(q=q, attn=attn)
            h_mid = stacked_rows_gemm(f"attn_o{layer}", attn, off_wo(i_b),
                                      [(h, tile(TM, TN_)), (b_b_o[i_b].reshape(1, D), vec_tile)],
                                      _store_add_extra, F32)
        fsv, h = ffn_forward(layer, h_mid, str(layer))
        sv.update(fsv)
        saved.append(sv)
        if layer == LA - 1:
            h_kv = h
            hn = _rms_fwd("kv_norm_fwd", h, kv_norm)

            def kv_ep(acc, ex, outs):
                val = acc + ex[0][...]
                for hh in range(NKV):
                    outs[0][hh] = val[:, hh * HEAD_DIM:(hh + 1) * HEAD_DIM].astype(BF16)
                    outs[1][hh] = val[:, (NKV + hh) * HEAD_DIM:(NKV + hh + 1) * HEAD_DIM].astype(BF16)

            k_heads, v_heads = _gemm(
                "kv_proj", (S // TM,),
                [(hn, pl.BlockSpec((TM, D), lambda i: (i, 0))),
                 (wb_z, pl.BlockSpec((NDEV, DS, KVW), lambda i: (0, off_wkv // DS, 0)))],
                [(0, 1, NN, None, _stacked)], [(b_kv.reshape(1, KVW), pl.BlockSpec((1, KVW), lambda i: (0, 0)))],
                [(_sds((NKV, S, HEAD_DIM), BF16), pl.BlockSpec((NKV, TM, HEAD_DIM), lambda i: (0, i, 0)))] * 2,
                kv_ep)

    loss11, d, d_bf, g_final = _loss_bwd("loss_bwd", h, final_norm, loss_target.reshape(S, D))
    loss = lax.psum(loss11[0, 0], AXES)

    g_mix, g_ffn = [None] * L, [None] * L
    g_ws, g_bs, g_nv = [None] * LA, [None] * LA, [None] * LA
    g_bq, g_sink, g_bo = [None] * LB, [None] * LB, [None] * LB
    dbiases = []
    kv_parts = []
    g_kvn = g_bkv = None

    for layer in reversed(range(L)):
        sv = saved[layer]
        tag = str(layer)
        def dhid_ep(acc, ex, outs):
            a, b = ex[0][0].astype(F32), ex[0][1].astype(F32)
            sg = jax.nn.sigmoid(a)
            outs[0][0] = (acc * b * (sg * (1.0 + a * (1.0 - sg)))).astype(BF16)
            outs[0][1] = (acc * (a * sg)).astype(BF16)

        ab_spec = pl.BlockSpec((2, None, TM, F), lambda i, e: (0, e, i, 0))
        (dab,) = _gemm(
            f"ffn_dhid{tag}", (S // TM, NDEV),
            [(d_bf, rows_full(TM)), (wb_d, pl.BlockSpec((None, F, D), lambda i, e: (e, off_down(layer) // F, 0)))],
            [(0, 1, NT)], [(sv["ab"], ab_spec)], [(_sds((2, NDEV, S, F), BF16), ab_spec)], dhid_ep)
        (gb_d,) = _gemm(
            f"ffn_dwdown{tag}", (NDEV, D // TN_),
            [(sv["hid"], pl.BlockSpec((None, S, F), lambda e, j: (e, 0, 0))),
             (d_bf, pl.BlockSpec((S, TN_), lambda e, j: (0, j)))],
            [(0, 1, TN)], [(gb_d, ANY)],
            [(_sds(gb_d.shape, BF16), pl.BlockSpec((None, F, TN_), lambda e, j: (e, off_down(layer) // F, j)))],
            _store(), aliases={2: 0})
        (gb_f,) = _gemm(
            f"ffn_dwup{tag}", (2, NDEV, D // TS),
            [(sv["xf"], pl.BlockSpec((S, TS), lambda w, e, i: (0, i))),
             (dab, pl.BlockSpec((None, None, S, F), lambda w, e, i: (w, e, 0, 0)))],
            [(0, 1, TN)], [(gb_f, ANY)],
            [(_sds(gb_f.shape, BF16),
              pl.BlockSpec((None, TS, F), lambda w, e, i: (e, (off_gate(layer) + w * D) // TS + i, 0)))],
            _store(), aliases={2: 0})
        (dxf,) = _gemm(
            f"ffn_dx{tag}", (S // TM, D // TN_, 2 * NDEV),
            [(dab, pl.BlockSpec((None, None, TM, F), lambda i, j, k: (k // NDEV, k % NDEV, i, 0))),
             (wb_f, pl.BlockSpec((None, TN_, F),
                                 lambda i, j, k: (k % NDEV, (off_gate(layer) + (k // NDEV) * D) // TN_ + j, 0)))],
            [(0, 1, NT)], [], [(_sds((S, D), F32), pl.BlockSpec((TM, TN_), lambda i, j, k: (i, j)))],
            _store(), nk=2 * NDEV, acc_shape=(TM, TN_))
        d, d_bf, g_ffn[layer], colsum = _rms_bwd(f"ffn_norm_bwd{tag}", sv["h_mid"], ffn_norm[layer], dxf, d)
        if layer < LA:
            i_a = layer
            dgated = back_rows_gemm(f"gmlp_dgated{tag}", d_bf, off_wout(i_a), F32)
            gb_d = grad_rows_gemm(f"gmlp_dwout{tag}", sv["gated"], d_bf, gb_d, off_wout(i_a))
            dzp, g_ws[i_a], dbs, g_nv[i_a] = _gmlp_bwd(f"gmlp_bwd{tag}", sv["zp"], dgated,
                                                       norm_v[i_a].reshape(1, D), a_w_s[i_a], sv["bst"])
            g_bs[i_a] = dbs[:, 0, :]
            (gb_z,) = _gemm(
                f"gmlp_dwin{tag}", (NDEV, D // TS),
                [(sv["xn"], pl.BlockSpec((S, TS), lambda e, i: (0, i))),
                 (dzp, pl.BlockSpec((S, ZC), lambda e, i: (0, e)))],
                [(0, 1, TN)], [(gb_z, ANY)],
                [(_sds(gb_z.shape, BF16), pl.BlockSpec((None, TS, ZC), lambda e, i: (e, off_win(i_a) // TS + i, 0)))],
                _store(), aliases={2: 0})
            (dxn,) = _gemm(
                f"gmlp_dx{tag}", (S // TM, D // TN_, NDEV),
                [(dzp, pl.BlockSpec((TM, ZC), lambda i, j, e: (i, e))),
                 (wb_z, pl.BlockSpec((None, TN_, ZC), lambda i, j, e: (e, off_win(i_a) // TN_ + j, 0)))],
                [(0, 1, NT)], [], [(_sds((S, D), F32), pl.BlockSpec((TM, TN_), lambda i, j, e: (i, j)))],
                _store(), nk=NDEV, acc_shape=(TM, TN_))
        else:
            i_b = layer - LA
            g_bo[i_b] = colsum
            dattn = back_rows_gemm(f"attn_dout{tag}", d_bf, off_wo(i_b), BF16)
            gb_d = grad_rows_gemm(f"attn_dwo{tag}", sv["attn"], d_bf, gb_d, off_wo(i_b))
            dq, g_bq[i_b], dkc, dkp, dvc, dvp, dbias, dsink = _attn_bwd(
                f"attn_bwd{tag}", sv["q"], k_heads, v_heads, dattn, bias, b_sinks[i_b])
            kv_parts.append((dkc, dkp, dvc, dvp))
            g_sink[i_b] = dsink.reshape(NH)
            dbiases.append(dbias.reshape(NH, BLOCK * 2 * BLOCK))
            gb_d = grad_rows_gemm(f"attn_dwq{tag}", sv["xn"], dq, gb_d, off_wq(i_b))
            dxn = back_rows_gemm(f"attn_dx{tag}", dq, off_wq(i_b), F32)
        d, d_bf, g_mix[layer], _ = _rms_bwd(f"mix_norm_bwd{tag}", sv["h_in"], mix_norm[layer], dxn, d)
        if layer == LA:
            dkv, g_bkv = _kv_grad("kv_grad", kv_parts)
            (gb_z,) = _gemm(
                "kv_dw", (NDEV,),
                [(hn, pl.BlockSpec((S, DS), lambda e: (0, e))), (dkv, pl.BlockSpec((S, KVW), lambda e: (0, 0)))],
                [(0, 1, TN)], [(gb_z, ANY)],
                [(_sds(gb_z.shape, BF16), pl.BlockSpec((None, DS, KVW), lambda e: (e, off_wkv // DS, 0)))],
                _store(), aliases={2: 0})
            (dhn,) = _gemm(
                "kv_dx", (S // TM, NDEV),
                [(dkv, pl.BlockSpec((TM, KVW), lambda i, e: (i, 0))),
                 (wb_z, pl.BlockSpec((None, DS, KVW), lambda i, e: (e, off_wkv // DS, 0)))],
                [(0, 1, NT)], [], [(_sds((S, D), F32), pl.BlockSpec((TM, DS), lambda i, e: (i, e)))], _store())
            d, d_bf, g_kvn, _ = _rms_bwd("kv_norm_bwd", h_kv, kv_norm, dhn, d)
    grad_x = d.reshape(x.shape)

    recv = _sibling_exchange("grads_to_sibling", [gb_d, gb_f, gb_z])
    sums = [_pair_sum(f"pair_sum{t}", g, r, core) for t, (g, r) in enumerate(zip([gb_d, gb_f, gb_z], recv))]
    pd, pf, pz = _chip_exchange("grads_to_chips", sums)

    def upd(name, w, m, v, parts, row0, stride=0):
        w3 = w if w.ndim == 3 else w.reshape((1,) + w.shape)
        res = _adamw_shard(name, w3, m.reshape(w3.shape), v.reshape(w3.shape), parts, row0, stride)
        return [r.reshape(w.shape) for r in res]

    u_win = upd("adamw_win", a_w_in, m_a_w_in, v_a_w_in, pz, off_win(0), D)
    u_wout = upd("adamw_wout", a_w_out, m_a_w_out, v_a_w_out, pd, off_wout(0), DS)
    u_wkv = upd("adamw_wkv", w_kv, m_w_kv, v_w_kv, pz, off_wkv)
    u_wq = upd("adamw_wq", b_w_q, m_b_w_q, v_b_w_q, pd, off_wq(0), DS)
    u_wo = upd("adamw_wo", b_w_o, m_b_w_o, v_b_w_o, pd, off_wo(0), DS)
    u_gate = upd("adamw_gate", ffn_w_gate, m_ffn_w_gate, v_ffn_w_gate, pf, off_gate(0), 2 * D)
    u_up = upd("adamw_up", ffn_w_up, m_ffn_w_up, v_ffn_w_up, pf, off_up(0), 2 * D)
    u_down = upd("adamw_down", ffn_w_down, m_ffn_w_down, v_ffn_w_down, pd, off_down(0), F)

    g_rel = _bias_grad("bias_grad", dbiases, buckets)
    small_local = [jnp.concatenate(g_mix, axis=0), jnp.concatenate(g_ffn, axis=0), jnp.stack(g_ws), jnp.stack(g_bs),
                   g_kvn, g_bkv, jnp.concatenate(g_bq, axis=0), jnp.stack(g_sink), jnp.concatenate(g_bo, axis=0),
                   g_rel, g_final, jnp.concatenate(g_nv, axis=0)]
    small_w = [mix_norm, ffn_norm, a_w_s, a_b_s, kv_norm, b_kv, b_b_q, b_sinks, b_b_o, rel_bias, final_norm]
    small_m = [m_mix_norm, m_ffn_norm, m_a_w_s, m_a_b_s, m_kv_norm, m_b_kv, m_b_b_q, m_b_sinks, m_b_b_o, m_rel_bias,
               m_final_norm]
    small_v = [v_mix_norm, v_ffn_norm, v_a_w_s, v_a_b_s, v_kv_norm, v_b_kv, v_b_b_q, v_b_sinks, v_b_b_o, v_rel_bias,
               v_final_norm]
    shapes = [w.shape for w in small_w] + [(LA, D)]
    (small_all,) = _all_gather("gather_small_grads", [_pack(small_local)])
    small_sum = _sum_devices("sum_small_grads", small_all)
    small_g = _unpack(small_sum, shapes)
    g_normv = lax.dynamic_slice_in_dim(small_g[-1], me * DS, DS, axis=1)
    small_g = small_g[:-1] + [g_normv]
    small_w, small_m, small_v = small_w + [a_norm_v], small_m + [m_a_norm_v], small_v + [v_a_norm_v]
    shapes = [w.shape for w in small_w]
    s_delta, s_m, s_v = _adamw_flat("adamw_small", _pack(small_w), _pack(small_g), _pack(small_m), _pack(small_v))
    s_delta, s_m, s_v = _unpack(s_delta, shapes), _unpack(s_m, shapes), _unpack(s_v, shapes)

    names = ["mix_norm", "ffn_norm", "a_w_in", "a_norm_v", "a_w_s", "a_b_s", "a_w_out", "kv_norm", "w_kv", "b_kv",
             "b_w_q", "b_b_q", "b_sinks", "b_w_o", "b_b_o", "rel_bias", "ffn_w_gate", "ffn_w_up", "ffn_w_down",
             "final_norm"]
    small_names = ["mix_norm", "ffn_norm", "a_w_s", "a_b_s", "kv_norm", "b_kv", "b_b_q", "b_sinks", "b_b_o", "rel_bias",
                   "final_norm", "a_norm_v"]
    big = dict(a_w_in=u_win, a_w_out=u_wout, w_kv=u_wkv, b_w_q=u_wq, b_w_o=u_wo, ffn_w_gate=u_gate, ffn_w_up=u_up,
               ffn_w_down=u_down)
    res = {}
    for idx, nm in enumerate(small_names):
        res[nm] = (small_g[idx].reshape(shapes[idx]), s_delta[idx], s_m[idx], s_v[idx])
    for nm, u in big.items():
        res[nm] = tuple(u)
    out = [loss, grad_x]
    for part in range(4):
        out += [res[nm][part] for nm in names]
    return tuple(out)
```

```python
import math

import numpy as np
import jax
import jax.numpy as jnp
from jax import lax
from jax.experimental import pallas as pl
from jax.experimental.pallas import tpu as pltpu

F32 = jnp.float32
BF16 = jnp.bfloat16
AXES = ("x", "y", "c")
NDEV = 8
NCHIP = 4
CHUNK = 128
GROUPS = 8
HEAD_DIM = 64
KV_GROUP = 8
BLOCK = 128
N_BUCKETS = 32
MAX_DISTANCE = 128
RMS_EPS = 1e-5
NEG_INF = -1e30
ADAM_LR, ADAM_B1, ADAM_B2, ADAM_EPS, ADAM_WD, ADAM_STEP = 0.001, 0.9, 0.999, 1e-08, 0.01, 10
VMEM_LIMIT_BYTES = 56 * 1024 * 1024

NN = (((1,), (0,)), ((), ()))
NT = (((1,), (1,)), ((), ()))
TN = (((0,), (0,)), ((), ()))
ANY = pl.BlockSpec(memory_space=pl.ANY)
HBM = pl.BlockSpec(memory_space=pltpu.HBM)
SEM = pl.BlockSpec(memory_space=pltpu.SEMAPHORE)
MESH = pl.DeviceIdType.MESH
EFFECT = pltpu.SideEffectType.DATAFLOW_SIDE_EFFECTING


def _pcall(body, *, name, out_shape, in_specs, out_specs, grid=(), scratch=(), aliases=None, prefetch=0, deps=()):
    n_in, n_dep = len(in_specs), len(deps)
    if n_dep:
        inner = body

        def body(*refs):
            return inner(*refs[:prefetch + n_in], *refs[prefetch + n_in + n_dep:])

        in_specs = list(in_specs) + [ANY] * n_dep
    params = dict(vmem_limit_bytes=VMEM_LIMIT_BYTES)
    if grid:
        params["dimension_semantics"] = ("arbitrary",) * len(grid)
    kw = dict(name=name, out_shape=out_shape, compiler_params=pltpu.CompilerParams(**params),
              input_output_aliases=aliases or {})
    if prefetch:
        kw["grid_spec"] = pltpu.PrefetchScalarGridSpec(num_scalar_prefetch=prefetch, grid=grid, in_specs=in_specs,
                                                       out_specs=out_specs, scratch_shapes=list(scratch))
    else:
        kw.update(grid=grid, in_specs=in_specs, out_specs=out_specs, scratch_shapes=list(scratch))
    call = pl.pallas_call(body, **kw)
    return lambda *args: call(*args, *deps)


def _sds(shape, dtype):
    return jax.ShapeDtypeStruct(tuple(shape), dtype)


def _position():
    x, y, c = lax.axis_index("x"), lax.axis_index("y"), lax.axis_index("c")
    chips = [(1 - x, y), (x, 1 - y), (1 - x, 1 - y)]
    return x, y, c, chips


def _slot(px, py, pc):
    return 4 * px + 2 * py + pc


def _remote(ref_src, ref_dst, send, recv, to):
    return pltpu.make_async_remote_copy(src_ref=ref_src, dst_ref=ref_dst, send_sem=send, recv_sem=recv,
                                        device_id=to, device_id_type=MESH)


def _hbm(arrays):
    return [pltpu.with_memory_space_constraint(a, pltpu.HBM) for a in arrays]


def _split_call(body, name, out_shape, in_specs, out_specs, aliases):
    return pl.pallas_call(body, name=name, out_shape=out_shape, in_specs=in_specs, out_specs=out_specs,
                          input_output_aliases=aliases, compiler_params=pltpu.CompilerParams(has_side_effects=EFFECT))


def _token_shape():
    return _sds((8, 128), F32)


def _gather_start(name, bufs, deps):
    n, nd = len(bufs), len(deps)

    def body(*refs):
        ins, send, recv, token = refs[:n], refs[n + nd], refs[n + nd + 1], refs[2 * n + nd + 2]
        x, y, c, chips = _position()
        peers = [(x, y, 1 - c)] + [(*chip, c) for chip in chips]
        for t in range(n):
            mine = ins[t].at[_slot(x, y, c)]
            for k, peer in enumerate(peers):
                _remote(mine, mine, send.at[4 * t + k], recv.at[4 * t + k], peer).start()
        token[...] = jnp.zeros_like(token)

    res = _split_call(
        body, name,
        (pltpu.SemaphoreType.DMA((4 * n,)), pltpu.SemaphoreType.DMA((4 * n,)), *[pltpu.HBM(b.shape, b.dtype) for b in bufs],
         _token_shape()),
        [HBM] * n + [ANY] * nd, (SEM, SEM, *[HBM] * n, pl.BlockSpec(memory_space=pltpu.VMEM)),
        {t: 2 + t for t in range(n)})(*_hbm(bufs), *deps)
    return res[0], res[1], list(res[2:2 + n]), res[2 + n]


def _gather_forward(name, bufs, send, recv, deps):
    n, nd = len(bufs), len(deps)

    def body(*refs):
        ins, send_in, recv_in = refs[:n], refs[n], refs[n + 1]
        fsend, frecv = refs[n + 2 + nd], refs[n + 3 + nd]
        x, y, c, chips = _position()
        for j, chip in enumerate(chips):
            for t in range(n):
                blk = ins[t].at[_slot(*chip, c)]
                _remote(blk, blk, send_in.at[4 * t + 1 + j], recv_in.at[4 * t + 1 + j], (*chip, c)).wait_recv()
                _remote(blk, blk, fsend.at[3 * t + j], frecv.at[3 * t + j], (x, y, 1 - c)).start()

    res = _split_call(
        body, name,
        (pltpu.SemaphoreType.DMA((3 * n,)), pltpu.SemaphoreType.DMA((3 * n,)), *[pltpu.HBM(b.shape, b.dtype) for b in bufs]),
        [HBM] * n + [SEM, SEM] + [ANY] * nd, (SEM, SEM, *[HBM] * n),
        {t: 2 + t for t in range(n)})(*_hbm(bufs), send, recv, *deps)
    return res[0], res[1], list(res[2:])


def _gather_finish(name, bufs, send, recv, fsend, frecv):
    n = len(bufs)

    def body(*refs):
        ins, send_in, recv_in, fs_in, fr_in = refs[:n], refs[n], refs[n + 1], refs[n + 2], refs[n + 3]
        x, y, c, chips = _position()
        sibling = (x, y, 1 - c)
        peers = [sibling] + [(*chip, c) for chip in chips]
        for t in range(n):
            blk = ins[t].at[_slot(x, y, 1 - c)]
            _remote(blk, blk, send_in.at[4 * t], recv_in.at[4 * t], sibling).wait_recv()
            for j, chip in enumerate(chips):
                blk = ins[t].at[_slot(*chip, 1 - c)]
                _remote(blk, blk, fs_in.at[3 * t + j], fr_in.at[3 * t + j], sibling).wait_recv()
            mine = ins[t].at[_slot(x, y, c)]
            for k, peer in enumerate(peers):
                _remote(mine, mine, send_in.at[4 * t + k], recv_in.at[4 * t + k], peer).wait_send()
            for j, chip in enumerate(chips):
                blk = ins[t].at[_slot(*chip, c)]
                _remote(blk, blk, fs_in.at[3 * t + j], fr_in.at[3 * t + j], sibling).wait_send()

    res = _split_call(
        body, name, tuple(pltpu.HBM(b.shape, b.dtype) for b in bufs),
        [HBM] * n + [SEM] * 4, tuple([HBM] * n), {t: t for t in range(n)})(*_hbm(bufs), send, recv, fsend, frecv)
    return list(res)


def _sibling_start(name, grads, lands, deps):
    n, nd = len(grads), len(deps)

    def body(*refs):
        g_in, l_in = refs[:n], refs[n:2 * n]
        send, recv, token = refs[2 * n + nd], refs[2 * n + nd + 1], refs[4 * n + nd + 2]
        x, y, c, _ = _position()
        for t in range(n):
            for k in range(NCHIP):
                _remote(g_in[t].at[2 * k + (1 - c)], l_in[t].at[k], send.at[NCHIP * t + k], recv.at[NCHIP * t + k],
                        (x, y, 1 - c)).start()
        token[...] = jnp.zeros_like(token)

    both = list(grads) + list(lands)
    res = _split_call(
        body, name,
        (pltpu.SemaphoreType.DMA((NCHIP * n,)), pltpu.SemaphoreType.DMA((NCHIP * n,)),
         *[pltpu.HBM(b.shape, b.dtype) for b in both], _token_shape()),
        [HBM] * (2 * n) + [ANY] * nd, (SEM, SEM, *[HBM] * (2 * n), pl.BlockSpec(memory_space=pltpu.VMEM)),
        {t: 2 + t for t in range(2 * n)})(*_hbm(both), *deps)
    return res[0], res[1], list(res[2:2 + n]), list(res[2 + n:2 + 2 * n]), res[2 + 2 * n]


def _sibling_finish(name, grads, lands, send, recv, deps):
    n, nd = len(grads), len(deps)

    def body(*refs):
        g_in, l_in, send_in, recv_in = refs[:n], refs[n:2 * n], refs[2 * n], refs[2 * n + 1]
        x, y, c, _ = _position()
        for t in range(n):
            for k in range(NCHIP):
                cp = _remote(g_in[t].at[2 * k + (1 - c)], l_in[t].at[k], send_in.at[NCHIP * t + k],
                             recv_in.at[NCHIP * t + k], (x, y, 1 - c))
                cp.wait_send()
                cp.wait_recv()

    both = list(grads) + list(lands)
    res = _split_call(
        body, name, tuple(pltpu.HBM(b.shape, b.dtype) for b in both),
        [HBM] * (2 * n) + [SEM, SEM] + [ANY] * nd, tuple([HBM] * (2 * n)),
        {t: t for t in range(2 * n)})(*_hbm(both), send, recv, *deps)
    return list(res[:n]), list(res[n:])


def _chips_start(name, parts, lands, deps):
    n, nd = len(parts), len(deps)

    def body(*refs):
        p_in, l_in = refs[:n], refs[n:2 * n]
        send, recv, token = refs[2 * n + nd], refs[2 * n + nd + 1], refs[4 * n + nd + 2]
        x, y, c, chips = _position()
        for t in range(n):
            for j, chip in enumerate(chips):
                _remote(p_in[t].at[2 * chip[0] + chip[1]], l_in[t].at[2 * x + y], send.at[3 * t + j], recv.at[3 * t + j],
                        (*chip, c)).start()
        token[...] = jnp.zeros_like(token)

    both = list(parts) + list(lands)
    res = _split_call(
        body, name,
        (pltpu.SemaphoreType.DMA((3 * n,)), pltpu.SemaphoreType.DMA((3 * n,)), *[pltpu.HBM(b.shape, b.dtype) for b in both],
         _token_shape()),
        [HBM] * (2 * n) + [ANY] * nd, (SEM, SEM, *[HBM] * (2 * n), pl.BlockSpec(memory_space=pltpu.VMEM)),
        {t: 2 + t for t in range(2 * n)})(*_hbm(both), *deps)
    return res[0], res[1], list(res[2:2 + n]), list(res[2 + n:2 + 2 * n]), res[2 + 2 * n]


def _chips_finish(name, parts, lands, send, recv, deps):
    n, nd = len(parts), len(deps)

    def body(*refs):
        p_in, l_in, send_in, recv_in = refs[:n], refs[n:2 * n], refs[2 * n], refs[2 * n + 1]
        x, y, c, chips = _position()
        for t in range(n):
            for j, chip in enumerate(chips):
                k = 2 * chip[0] + chip[1]
                _remote(p_in[t].at[k], l_in[t].at[k], send_in.at[3 * t + j], recv_in.at[3 * t + j], (*chip, c)).wait_recv()
                _remote(p_in[t].at[k], l_in[t].at[2 * x + y], send_in.at[3 * t + j], recv_in.at[3 * t + j],
                        (*chip, c)).wait_send()

    both = list(parts) + list(lands)
    res = _split_call(
        body, name, tuple(pltpu.HBM(b.shape, b.dtype) for b in both),
        [HBM] * (2 * n) + [SEM, SEM] + [ANY] * nd, tuple([HBM] * (2 * n)),
        {t: t for t in range(2 * n)})(*_hbm(both), send, recv, *deps)
    return list(res[n:])


def _all_gather(name, shards):
    n = len(shards)

    def body(*refs):
        src, dst = refs[:n], refs[n:2 * n]
        send_sems, recv_sems, local_sems = refs[2 * n:]
        x, y, c, chips = _position()
        me, sibling = (x, y, c), (x, y, 1 - c)

        def copy(t, k, block, to, from_shard=False):
            slot = dst[t].at[_slot(*block)]
            return _remote(src[t] if from_shard else slot, slot, send_sems.at[t, k], recv_sems.at[t, k], to)

        mine = [pltpu.make_async_copy(src[t], dst[t].at[_slot(x, y, c)], local_sems.at[t]) for t in range(n)]
        first, passed = [], []
        for t in range(n):
            mine[t].start()
            first.append(copy(t, 0, me, sibling, True))
            first += [copy(t, 1 + j, me, (*chip, c), True) for j, chip in enumerate(chips)]
        for cp in first:
            cp.start()
        for j, chip in enumerate(chips):
            for t in range(n):
                copy(t, 1 + j, (*chip, c), me).wait_recv()
                fwd = copy(t, 4 + j, (*chip, c), sibling)
                fwd.start()
                passed.append(fwd)
        for t in range(n):
            copy(t, 0, sibling, me).wait_recv()
            for j, chip in enumerate(chips):
                copy(t, 4 + j, (*chip, 1 - c), me).wait_recv()
        for cp in first + passed:
            cp.wait_send()
        for t in range(n):
            mine[t].wait()

    outs = _pcall(
        body, name=name, out_shape=[_sds((NDEV,) + s.shape, s.dtype) for s in shards],
        in_specs=[ANY] * n, out_specs=[ANY] * n,
        scratch=[pltpu.SemaphoreType.DMA((n, 7)), pltpu.SemaphoreType.DMA((n, 7)), pltpu.SemaphoreType.DMA((n,))],
    )(*shards)
    return list(outs)


def _pair_sum(name, grad, recv, where):
    _, r, w = grad.shape
    tr = _row_tile(r, w)
    g4 = grad.reshape(NCHIP, 2, r, w)

    def body(where_ref, g_ref, r_ref, o_ref, own_ref):
        val = (g_ref[...].astype(F32) + r_ref[...].astype(F32)).astype(o_ref.dtype)
        o_ref[...] = val

        @pl.when(pl.program_id(1) == where_ref[1])
        def _():
            own_ref[...] = val

    out = _sds((NCHIP, r, w), grad.dtype)
    return _pcall(
        body, name=name, out_shape=[out, out], grid=(r // tr, NCHIP), prefetch=1,
        in_specs=[pl.BlockSpec((None, None, tr, w), lambda i, k, wr: (k, wr[0], i, 0)),
                  pl.BlockSpec((None, tr, w), lambda i, k, wr: (k, i, 0))],
        out_specs=[pl.BlockSpec((None, tr, w), lambda i, k, wr: (k, i, 0)),
                   pl.BlockSpec((None, tr, w), lambda i, k, wr: (wr[1], i, 0))],
    )(where, g4, recv)


def _row_tile(rows, width, budget=2 * 1024 * 1024):
    best = None
    for t in range(16, rows + 1, 16):
        if rows % t == 0 and t * width * 4 <= budget:
            best = t
    if best is None and rows * width * 4 <= budget:
        best = rows
    assert best is not None, (rows, width)
    return best


def _gemm(name, grid, operands, prods, extras, outs, epilogue, *, nk=1, acc_shape=None, aliases=None, separate=False,
          deps=()):
    n_op, n_ex, n_out = len(operands), len(extras), len(outs)

    def body(*refs):
        ops, ex, out_refs = refs[:n_op], refs[n_op:n_op + n_ex], refs[n_op + n_ex:n_op + n_ex + n_out]
        parts = []
        for pr in prods:
            a, b = ops[pr[0]], ops[pr[1]]
            av = pr[3](a) if len(pr) > 3 and pr[3] else a[...]
            bv = pr[4](b) if len(pr) > 4 and pr[4] else b[...]
            parts.append(lax.dot_general(av, bv, pr[2], preferred_element_type=F32))
        if separate:
            epilogue(parts, ex, out_refs)
            return
        part = parts[0]
        for p in parts[1:]:
            part = part + p
        if nk == 1:
            epilogue(part, ex, out_refs)
        else:
            acc = refs[-1]
            k = pl.program_id(len(grid) - 1)

            @pl.when(k == 0)
            def _():
                acc[...] = part

            @pl.when(k > 0)
            def _():
                acc[...] += part

            @pl.when(k == nk - 1)
            def _():
                epilogue(acc[...], ex, out_refs)

    res = _pcall(
        body, name=name, out_shape=[o[0] for o in outs], grid=grid,
        in_specs=[o[1] for o in operands] + [e[1] for e in extras], out_specs=[o[1] for o in outs],
        scratch=[pltpu.VMEM(acc_shape, F32)] if nk > 1 else [], aliases=aliases, deps=deps,
    )(*[o[0] for o in operands], *[e[0] for e in extras])
    return list(res)


def _store(acc, ex, outs):
    outs[0][...] = acc.astype(outs[0].dtype)


def _store_add_extra(acc, ex, outs):
    v = acc
    for e in ex:
        v = v + e[...]
    outs[0][...] = v.astype(outs[0].dtype)


def _stacked(ref):
    b = ref[...]
    return b.reshape(b.shape[0] * b.shape[1], b.shape[2])


def _gelu_parts(z):
    c = math.sqrt(2.0 / math.pi)
    t = jnp.tanh(c * (z + 0.044715 * (z * z * z)))
    val = 0.5 * z * (1.0 + t)
    grad = 0.5 * (1.0 + t) + 0.5 * z * (1.0 - t * t) * (c * (1.0 + 3.0 * 0.044715 * z * z))
    return val, grad


def _rms_fwd(name, h, g, deps=()):
    s, d = h.shape
    tr = _row_tile(s, d)

    def body(h_ref, g_ref, o_ref):
        hv = h_ref[...]
        r = lax.rsqrt(jnp.mean(hv * hv, axis=-1, keepdims=True) + RMS_EPS)
        o_ref[...] = (hv * r * g_ref[...]).astype(o_ref.dtype)

    return _pcall(
        body, name=name, out_shape=_sds((s, d), BF16), grid=(s // tr,),
        in_specs=[pl.BlockSpec((tr, d), lambda i: (i, 0)), pl.BlockSpec((1, d), lambda i: (0, 0))],
        out_specs=pl.BlockSpec((tr, d), lambda i: (i, 0)), deps=deps,
    )(h, g.reshape(1, d))


def _accumulate(ref, val, first):
    @pl.when(first)
    def _():
        ref[...] = val

    @pl.when(jnp.logical_not(first))
    def _():
        ref[...] += val


def _rms_bwd(name, h, g, dy, res):
    s, d = h.shape
    tr = _row_tile(s, d, budget=1024 * 1024)

    def body(h_ref, g_ref, dy_ref, res_ref, dh_ref, dhb_ref, dg_ref, cs_ref):
        hv = h_ref[...]
        r = lax.rsqrt(jnp.mean(hv * hv, axis=-1, keepdims=True) + RMS_EPS)
        xhat = hv * r
        dyv = dy_ref[...]
        dxh = dyv * g_ref[...]
        dh = res_ref[...] + r * (dxh - xhat * jnp.mean(dxh * xhat, axis=-1, keepdims=True))
        dh_ref[...] = dh
        dhb_ref[...] = dh.astype(BF16)
        first = pl.program_id(0) == 0
        _accumulate(dg_ref, jnp.sum(dyv * xhat, axis=0, keepdims=True), first)
        _accumulate(cs_ref, jnp.sum(dh, axis=0, keepdims=True), first)

    row = pl.BlockSpec((tr, d), lambda i: (i, 0))
    vec = pl.BlockSpec((1, d), lambda i: (0, 0))
    return _pcall(
        body, name=name, out_shape=[_sds((s, d), F32), _sds((s, d), BF16), _sds((1, d), F32), _sds((1, d), F32)],
        grid=(s // tr,), in_specs=[row, vec, row, row], out_specs=[row, row, vec, vec],
    )(h, g.reshape(1, d), dy, res)


def _loss_bwd(name, h, g, target):
    s, d = h.shape
    tr = _row_tile(s, d, budget=1024 * 1024)

    def body(h_ref, g_ref, t_ref, loss_ref, dh_ref, dhb_ref, dg_ref):
        hv = h_ref[...]
        r = lax.rsqrt(jnp.mean(hv * hv, axis=-1, keepdims=True) + RMS_EPS)
        xhat = hv * r
        diff = xhat * g_ref[...] - t_ref[...]
        part = jnp.sum(jnp.sum(diff * diff, axis=1, keepdims=True), axis=0, keepdims=True) * (0.5 / d)
        dyv = diff * (1.0 / d)
        dxh = dyv * g_ref[...]
        dh = r * (dxh - xhat * jnp.mean(dxh * xhat, axis=-1, keepdims=True))
        dh_ref[...] = dh
        dhb_ref[...] = dh.astype(BF16)
        first = pl.program_id(0) == 0
        _accumulate(loss_ref, part, first)
        _accumulate(dg_ref, jnp.sum(dyv * xhat, axis=0, keepdims=True), first)

    row = pl.BlockSpec((tr, d), lambda i: (i, 0))
    vec = pl.BlockSpec((1, d), lambda i: (0, 0))
    one = pl.BlockSpec((1, 1), lambda i: (0, 0))
    return _pcall(
        body, name=name, out_shape=[_sds((1, 1), F32), _sds((s, d), F32), _sds((s, d), BF16), _sds((1, d), F32)],
        grid=(s // tr,), in_specs=[row, vec, row], out_specs=[one, row, row, vec],
    )(h, g.reshape(1, d), target)


def _tril_mask():
    return lax.broadcasted_iota(jnp.int32, (CHUNK, CHUNK), 0) >= lax.broadcasted_iota(jnp.int32, (CHUNK, CHUNK), 1)


def _gmlp_fwd(name, zp, gv, ws, bst):
    s, d2 = zp.shape
    d = d2 // 2
    gw = d // GROUPS

    def body(zp_ref, gv_ref, ws_ref, bst_ref, o_ref):
        u, _ = _gelu_parts(zp_ref[:, :d])
        v, _ = _gelu_parts(zp_ref[:, d:])
        rv = lax.rsqrt(jnp.mean(v * v, axis=-1, keepdims=True) + RMS_EPS)
        vn = (v * rv * gv_ref[...]).astype(BF16)
        tril = _tril_mask()
        for g in range(GROUPS):
            sl = slice(g * gw, (g + 1) * gw)
            wc = jnp.where(tril, ws_ref[g], 0.0).astype(BF16)
            sg = jnp.dot(wc, vn[:, sl], preferred_element_type=F32) + bst_ref[:, g:g + 1]
            o_ref[:, sl] = (u[:, sl] * sg).astype(o_ref.dtype)

    return _pcall(
        body, name=name, out_shape=_sds((s, d), BF16), grid=(s // CHUNK,),
        in_specs=[pl.BlockSpec((CHUNK, d2), lambda i: (i, 0)), pl.BlockSpec((1, d), lambda i: (0, 0)),
                  pl.BlockSpec((GROUPS, CHUNK, CHUNK), lambda i: (0, 0, 0)),
                  pl.BlockSpec((CHUNK, GROUPS), lambda i: (0, 0))],
        out_specs=pl.BlockSpec((CHUNK, d), lambda i: (i, 0)),
    )(zp, gv, ws, bst)


def _gmlp_bwd(name, zp, dgated, gv, ws, bst):
    s, d2 = zp.shape
    d = d2 // 2
    gw = d // GROUPS

    def body(zp_ref, dg_ref, gv_ref, ws_ref, bst_ref, dzp_ref, dws_ref, dbs_ref, dgv_ref, dvn_ref):
        u, gu = _gelu_parts(zp_ref[:, :d])
        v, gvv = _gelu_parts(zp_ref[:, d:])
        rv = lax.rsqrt(jnp.mean(v * v, axis=-1, keepdims=True) + RMS_EPS)
        vhat = v * rv
        vn = (vhat * gv_ref[...]).astype(BF16)
        tril = _tril_mask()
        first = pl.program_id(0) == 0
        ones = jnp.ones((8, gw), F32)

        @pl.when(first)
        def _():
            dws_ref[...] = jnp.zeros_like(dws_ref)
            dbs_ref[...] = jnp.zeros_like(dbs_ref)

        for g in range(GROUPS):
            sl = slice(g * gw, (g + 1) * gw)
            wc = jnp.where(tril, ws_ref[g], 0.0).astype(BF16)
            sg = jnp.dot(wc, vn[:, sl], preferred_element_type=F32) + bst_ref[:, g:g + 1]
            dgs = dg_ref[:, sl]
            ds = dgs * u[:, sl]
            dsb = ds.astype(BF16)
            dzp_ref[:, sl] = (dgs * sg * gu[:, sl]).astype(dzp_ref.dtype)
            dvn_ref[:, sl] = lax.dot_general(wc, dsb, TN, preferred_element_type=F32)
            dw = lax.dot_general(dsb, vn[:, sl], NT, preferred_element_type=F32)
            dws_ref[g] += jnp.where(tril, dw, 0.0)
            dbs_ref[g] += lax.dot_general(ones, ds, NT, preferred_element_type=F32, precision=lax.Precision.HIGHEST)
        dvn = dvn_ref[...]
        dvh = dvn * gv_ref[...]
        dv = rv * (dvh - vhat * jnp.mean(dvh * vhat, axis=-1, keepdims=True))
        dzp_ref[:, d:] = (dv * gvv).astype(dzp_ref.dtype)
        _accumulate(dgv_ref, jnp.sum(dvn * vhat, axis=0, keepdims=True), first)

    return _pcall(
        body, name=name,
        out_shape=[_sds((s, d2), BF16), _sds((GROUPS, CHUNK, CHUNK), F32), _sds((GROUPS, 8, CHUNK), F32),
                   _sds((1, d), F32)],
        grid=(s // CHUNK,),
        in_specs=[pl.BlockSpec((CHUNK, d2), lambda i: (i, 0)), pl.BlockSpec((CHUNK, d), lambda i: (i, 0)),
                  pl.BlockSpec((1, d), lambda i: (0, 0)), pl.BlockSpec((GROUPS, CHUNK, CHUNK), lambda i: (0, 0, 0)),
                  pl.BlockSpec((CHUNK, GROUPS), lambda i: (0, 0))],
        out_specs=[pl.BlockSpec((CHUNK, d2), lambda i: (i, 0)),
                   pl.BlockSpec((GROUPS, CHUNK, CHUNK), lambda i: (0, 0, 0)),
                   pl.BlockSpec((GROUPS, 8, CHUNK), lambda i: (0, 0, 0)), pl.BlockSpec((1, d), lambda i: (0, 0))],
        scratch=[pltpu.VMEM((CHUNK, d), F32)],
    )(zp, dgated, gv, ws, bst)


def _bucket_table():
    dist = np.arange(BLOCK)[:, None] + BLOCK - np.arange(2 * BLOCK)[None, :]
    in_window = (dist >= 0) & (dist < BLOCK)
    dd = np.clip(dist, 0, None)
    max_exact = N_BUCKETS // 2
    dl = np.maximum(dd, 1).astype(np.float32)
    large = max_exact + (np.log(dl / np.float32(max_exact)) / np.float32(math.log(MAX_DISTANCE / max_exact))
                         * np.float32(N_BUCKETS - max_exact)).astype(np.int32)
    large = np.minimum(large, N_BUCKETS - 1)
    bucket = np.where(dd < max_exact, dd, large)
    return np.where(in_window, bucket, -1).astype(np.int32).reshape(1, -1)


def _bias_table(name, rel_bias_t, buckets):
    nh = rel_bias_t.shape[0]
    p = buckets.shape[1]
    tp = 4096

    def body(rb_ref, bk_ref, o_ref):
        bk = bk_ref[...]
        onehot = (lax.broadcasted_iota(jnp.int32, (N_BUCKETS, tp), 0) == bk).astype(F32)
        val = jnp.dot(rb_ref[...], onehot, preferred_element_type=F32, precision=lax.Precision.HIGHEST)
        o_ref[...] = jnp.where(bk >= 0, val, NEG_INF)

    return _pcall(
        body, name=name, out_shape=_sds((nh, p), F32), grid=(p // tp,),
        in_specs=[pl.BlockSpec((nh, N_BUCKETS), lambda i: (0, 0)), pl.BlockSpec((1, tp), lambda i: (0, i))],
        out_specs=pl.BlockSpec((nh, tp), lambda i: (0, i)),
    )(rel_bias_t, buckets)


def _bias_grad(name, dbiases, buckets):
    nh, p = dbiases[0].shape
    n = len(dbiases)
    tp = 4096

    def body(*refs):
        bk_ref, o_ref = refs[n], refs[n + 1]
        onehot = (lax.broadcasted_iota(jnp.int32, (N_BUCKETS, tp), 0) == bk_ref[...]).astype(F32)
        db = refs[0][...]
        for r in refs[1:n]:
            db = db + r[...]
        part = lax.dot_general(onehot, db, NT, preferred_element_type=F32, precision=lax.Precision.HIGHEST)
        _accumulate(o_ref, part, pl.program_id(0) == 0)

    return _pcall(
        body, name=name, out_shape=_sds((N_BUCKETS, nh), F32), grid=(p // tp,),
        in_specs=[pl.BlockSpec((nh, tp), lambda i: (0, i))] * n + [pl.BlockSpec((1, tp), lambda i: (0, i))],
        out_specs=pl.BlockSpec((N_BUCKETS, nh), lambda i: (0, 0)),
    )(*dbiases, buckets)


def _attn_probs(qh, kp, kc, bias, sink, prev_penalty):
    sp = lax.dot_general(qh, kp, NT, preferred_element_type=F32) * 0.125 + bias[:, :BLOCK] + prev_penalty
    sc = lax.dot_general(qh, kc, NT, preferred_element_type=F32) * 0.125 + bias[:, BLOCK:]
    m = jnp.maximum(jnp.maximum(jnp.max(sp, axis=-1, keepdims=True), jnp.max(sc, axis=-1, keepdims=True)), sink)
    pp, pc = jnp.exp(sp - m), jnp.exp(sc - m)
    es = jnp.exp(sink - m)
    inv = 1.0 / (jnp.sum(pp, axis=-1, keepdims=True) + jnp.sum(pc, axis=-1, keepdims=True) + es)
    return pp * inv, pc * inv, es * inv


def _attn_specs(nkv):
    gq = KV_GROUP * HEAD_DIM
    q_spec = pl.BlockSpec((BLOCK, gq), lambda kh, i: (i, kh))
    prev = pl.BlockSpec((None, BLOCK, HEAD_DIM), lambda kh, i: (kh, jnp.maximum(i - 1, 0), 0))
    cur = pl.BlockSpec((None, BLOCK, HEAD_DIM), lambda kh, i: (kh, i, 0))
    bias = pl.BlockSpec((KV_GROUP, BLOCK, 2 * BLOCK), lambda kh, i: (kh, 0, 0))
    smem = pl.BlockSpec(memory_space=pltpu.SMEM)
    return q_spec, prev, cur, bias, smem


def _attn_fwd(name, q, k, v, bias, sinks):
    s, dq = q.shape
    nkv = k.shape[0]
    q_spec, prev, cur, bias_spec, smem = _attn_specs(nkv)

    def body(q_ref, kp_ref, kc_ref, vp_ref, vc_ref, b_ref, s_ref, o_ref):
        kh, i = pl.program_id(0), pl.program_id(1)
        penalty = jnp.where(i > 0, 0.0, NEG_INF).astype(F32)
        kp, kc, vp, vc = kp_ref[...], kc_ref[...], vp_ref[...], vc_ref[...]
        for hh in range(KV_GROUP):
            sl = slice(hh * HEAD_DIM, (hh + 1) * HEAD_DIM)
            pp, pc, _ = _attn_probs(q_ref[:, sl], kp, kc, b_ref[hh], s_ref[kh * KV_GROUP + hh], penalty)
            o = (jnp.dot(pp.astype(BF16), vp, preferred_element_type=F32)
                 + jnp.dot(pc.astype(BF16), vc, preferred_element_type=F32))
            o_ref[:, sl] = o.astype(o_ref.dtype)

    return _pcall(
        body, name=name, out_shape=_sds((s, dq), BF16), grid=(nkv, s // BLOCK),
        in_specs=[q_spec, prev, cur, prev, cur, bias_spec, smem], out_specs=q_spec,
    )(q, k, k, v, v, bias, sinks)


def _attn_bwd(name, q, k, v, do, bias, sinks):
    s, dq = q.shape
    nkv = k.shape[0]
    gq = KV_GROUP * HEAD_DIM
    q_spec, prev, cur, bias_spec, smem = _attn_specs(nkv)

    def body(q_ref, do_ref, kp_ref, kc_ref, vp_ref, vc_ref, b_ref, s_ref,
             dq_ref, dbq_ref, dkc_ref, dkp_ref, dvc_ref, dvp_ref, dbias_ref, dsink_ref):
        kh, i = pl.program_id(0), pl.program_id(1)
        first = i == 0
        penalty = jnp.where(i > 0, 0.0, NEG_INF).astype(F32)
        kp, kc, vp, vc = kp_ref[...], kc_ref[...], vp_ref[...], vc_ref[...]

        @pl.when(first)
        def _():
            dbias_ref[...] = jnp.zeros_like(dbias_ref)
            dsink_ref[...] = jnp.zeros_like(dsink_ref)
            dbq_ref[...] = jnp.zeros_like(dbq_ref)

        dkc = dkp = dvc = dvp = None
        for hh in range(KV_GROUP):
            sl = slice(hh * HEAD_DIM, (hh + 1) * HEAD_DIM)
            qh, doh = q_ref[:, sl], do_ref[:, sl]
            pp, pc, ps = _attn_probs(qh, kp, kc, b_ref[hh], s_ref[kh * KV_GROUP + hh], penalty)
            dpp = lax.dot_general(doh, vp, NT, preferred_element_type=F32)
            dpc = lax.dot_general(doh, vc, NT, preferred_element_type=F32)
            delta = jnp.sum(pp * dpp, axis=-1, keepdims=True) + jnp.sum(pc * dpc, axis=-1, keepdims=True)
            dsp, dsc = pp * (dpp - delta), pc * (dpc - delta)
            dspb, dscb = dsp.astype(BF16), dsc.astype(BF16)
            dqh = (jnp.dot(dspb, kp, preferred_element_type=F32) + jnp.dot(dscb, kc, preferred_element_type=F32)) * 0.125
            dq_ref[:, sl] = dqh.astype(dq_ref.dtype)
            dbq_ref[:, sl] += jnp.sum(dqh, axis=0, keepdims=True)
            terms = (lax.dot_general(dscb, qh, TN, preferred_element_type=F32) * 0.125,
                     lax.dot_general(dspb, qh, TN, preferred_element_type=F32) * 0.125,
                     lax.dot_general(pc.astype(BF16), doh, TN, preferred_element_type=F32),
                     lax.dot_general(pp.astype(BF16), doh, TN, preferred_element_type=F32))
            if hh == 0:
                dkc, dkp, dvc, dvp = terms
            else:
                dkc, dkp, dvc, dvp = dkc + terms[0], dkp + terms[1], dvc + terms[2], dvp + terms[3]
            dbias_ref[hh, :, :BLOCK] += dsp
            dbias_ref[hh, :, BLOCK:] += dsc
            dsink_ref[:, hh:hh + 1] += jnp.sum(-(ps * delta), axis=0, keepdims=True)
        dkc_ref[...], dkp_ref[...], dvc_ref[...], dvp_ref[...] = dkc, dkp, dvc, dvp

    kv_out = _sds((nkv, s, HEAD_DIM), F32)
    return _pcall(
        body, name=name,
        out_shape=[_sds((s, dq), BF16), _sds((1, dq), F32), kv_out, kv_out, kv_out, kv_out,
                   _sds((nkv * KV_GROUP, BLOCK, 2 * BLOCK), F32), _sds((nkv, 1, KV_GROUP), F32)],
        grid=(nkv, s // BLOCK),
        in_specs=[q_spec, q_spec, prev, cur, prev, cur, bias_spec, smem],
        out_specs=[q_spec, pl.BlockSpec((1, gq), lambda kh, i: (0, kh)), cur, cur, cur, cur, bias_spec,
                   pl.BlockSpec((None, 1, KV_GROUP), lambda kh, i: (kh, 0, 0))],
    )(q, do, k, k, v, v, bias, sinks)


def _kv_grad(name, parts):
    nkv, s, _ = parts[0][0].shape
    nb = s // BLOCK
    w = 2 * nkv * HEAD_DIM
    n = len(parts)

    def body(*refs):
        o_ref, cs_ref = refs[4 * n], refs[4 * n + 1]
        i = pl.program_id(0)
        keep = jnp.where(i < nb - 1, 1.0, 0.0).astype(F32)

        @pl.when(i == 0)
        def _():
            cs_ref[...] = jnp.zeros_like(cs_ref)

        for which in range(2):
            for hh in range(nkv):
                val = None
                for l in range(n):
                    cur_ref, nxt_ref = refs[4 * l + 2 * which], refs[4 * l + 2 * which + 1]
                    t = cur_ref[hh] + keep * nxt_ref[hh]
                    val = t if val is None else val + t
                sl = slice((which * nkv + hh) * HEAD_DIM, (which * nkv + hh + 1) * HEAD_DIM)
                o_ref[:, sl] = val.astype(o_ref.dtype)
                cs_ref[:, sl] += jnp.sum(val, axis=0, keepdims=True)

    cur = pl.BlockSpec((nkv, BLOCK, HEAD_DIM), lambda i: (0, i, 0))
    nxt = pl.BlockSpec((nkv, BLOCK, HEAD_DIM), lambda i: (0, jnp.minimum(i + 1, nb - 1), 0))
    flat = [a for p in parts for a in p]
    return _pcall(
        body, name=name, out_shape=[_sds((s, w), BF16), _sds((1, w), F32)], grid=(nb,),
        in_specs=[cur, nxt] * (2 * n),
        out_specs=[pl.BlockSpec((BLOCK, w), lambda i: (i, 0)), pl.BlockSpec((1, w), lambda i: (0, 0))],
    )(*flat)


def _adamw_math(w, g, m, v):
    m = ADAM_B1 * m + (1.0 - ADAM_B1) * g
    v = ADAM_B2 * v + (1.0 - ADAM_B2) * (g * g)
    m_hat = m / (1.0 - ADAM_B1 ** ADAM_STEP)
    v_hat = v / (1.0 - ADAM_B2 ** ADAM_STEP)
    delta = -ADAM_LR * (m_hat / (jnp.sqrt(v_hat) + ADAM_EPS) + ADAM_WD * w)
    return delta, m, v


def _adamw_shard(name, w, m, v, parts, row0, layer, prev, deps=()):
    _, r, wd = w.shape
    tr = _row_tile(r, wd, budget=3 * 512 * 1024)
    assert row0 % tr == 0

    def body(w_ref, m_ref, v_ref, p_ref, a0, a1, a2, a3, g_ref, d_ref, nm_ref, nv_ref):
        g = p_ref[0].astype(F32)
        for k in range(1, NCHIP):
            g = g + p_ref[k].astype(F32)
        delta, nm, nv = _adamw_math(w_ref[...], g, m_ref[...], v_ref[...])
        g_ref[...], d_ref[...], nm_ref[...], nv_ref[...] = g, delta, nm, nv

    par = pl.BlockSpec((None, tr, wd), lambda i: (layer, i, 0))
    out = _sds(w.shape, F32)
    return _pcall(
        body, name=name, out_shape=[out, out, out, out], grid=(r // tr,),
        in_specs=[par, par, par, pl.BlockSpec((NCHIP, tr, wd), lambda i: (0, row0 // tr + i, 0)), ANY, ANY, ANY, ANY],
        out_specs=[par, par, par, par], aliases={4: 0, 5: 1, 6: 2, 7: 3}, deps=deps,
    )(w, m, v, parts, *prev)


def _sum_devices(name, gathered):
    _, r, wd = gathered.shape

    def body(g_ref, o_ref):
        acc = g_ref[0]
        for k in range(1, NDEV):
            acc = acc + g_ref[k]
        o_ref[...] = acc

    return _pcall(body, name=name, out_shape=_sds((r, wd), F32), grid=(1,),
                  in_specs=[pl.BlockSpec((NDEV, r, wd), lambda i: (0, 0, 0))],
                  out_specs=pl.BlockSpec((r, wd), lambda i: (0, 0)))(gathered)


def _adamw_flat(name, w, g, m, v):
    shape = w.shape

    def body(w_ref, g_ref, m_ref, v_ref, d_ref, nm_ref, nv_ref):
        d_ref[...], nm_ref[...], nv_ref[...] = _adamw_math(w_ref[...], g_ref[...], m_ref[...], v_ref[...])

    spec = pl.BlockSpec(shape, lambda i: (0, 0))
    out = _sds(shape, F32)
    return _pcall(body, name=name, out_shape=[out, out, out], grid=(1,), in_specs=[spec] * 4,
                  out_specs=[spec] * 3)(w, g, m, v)


def _cast_into(name, src, layer, buf, row0, me):
    _, r, wd = src.shape
    tr = _row_tile(r, wd)
    assert row0 % tr == 0

    def body(me_ref, s_ref, b_ref, o_ref):
        o_ref[...] = s_ref[...].astype(o_ref.dtype)

    return _pcall(
        body, name=name, out_shape=_sds(buf.shape, buf.dtype), grid=(r // tr,), prefetch=1,
        in_specs=[pl.BlockSpec((None, tr, wd), lambda i, mr: (layer, i, 0)), ANY],
        out_specs=pl.BlockSpec((None, tr, wd), lambda i, mr: (mr[0], row0 // tr + i, 0)), aliases={2: 0},
    )(me, src, buf)


def _pack(arrays):
    rows = []
    for a in arrays:
        flat = a.reshape(-1).astype(F32)
        pad = (-flat.shape[0]) % 1024
        rows.append(jnp.pad(flat, (0, pad)).reshape(-1, 128))
    return jnp.concatenate(rows, axis=0)


def _unpack(packed, shapes):
    out, r = [], 0
    for shp in shapes:
        n = int(np.prod(shp))
        nr = (n + 1023) // 1024 * 8
        out.append(packed[r:r + nr].reshape(-1)[:n].reshape(shp))
        r += nr
    return out


def kernel(x, mix_norm, ffn_norm, a_w_in, a_norm_v, a_w_s, a_b_s, a_w_out, kv_norm, w_kv, b_kv, b_w_q, b_b_q, b_sinks, b_w_o, b_b_o, rel_bias, ffn_w_gate, ffn_w_up, ffn_w_down, final_norm, loss_target, m_mix_norm, m_ffn_norm, m_a_w_in, m_a_norm_v, m_a_w_s, m_a_b_s, m_a_w_out, m_kv_norm, m_w_kv, m_b_kv, m_b_w_q, m_b_b_q, m_b_sinks, m_b_w_o, m_b_b_o, m_rel_bias, m_ffn_w_gate, m_ffn_w_up, m_ffn_w_down, m_final_norm, v_mix_norm, v_ffn_norm, v_a_w_in, v_a_norm_v, v_a_w_s, v_a_b_s, v_a_w_out, v_kv_norm, v_w_kv, v_b_kv, v_b_w_q, v_b_b_q, v_b_sinks, v_b_w_o, v_b_b_o, v_rel_bias, v_ffn_w_gate, v_ffn_w_up, v_ffn_w_down, v_final_norm):
    _, S, D = x.shape
    LA, LB, L = a_w_in.shape[0], b_w_q.shape[0], ffn_w_gate.shape[0]
    F = ffn_w_gate.shape[2]
    DS = D // NDEV
    ZC = a_w_in.shape[2]
    KVW = w_kv.shape[1]
    NKV = KVW // (2 * HEAD_DIM)
    NH = D // HEAD_DIM
    assert ZC * NDEV == 2 * D and NH == NKV * KV_GROUP and S % BLOCK == 0
    TM = min(1024, S)
    TN_ = min(1024, D)
    TS = min(512, D)

    ix, iy, ic = lax.axis_index("x"), lax.axis_index("y"), lax.axis_index("c")
    me = (4 * ix + 2 * iy + ic).astype(jnp.int32)
    me1 = me.reshape(1)
    where = jnp.stack([ic, 2 * ix + iy]).astype(jnp.int32)

    def tr3(a):
        return jnp.transpose(a, (0, 2, 1))

    gate_t, up_t = tr3(ffn_w_gate), tr3(ffn_w_up)
    w_kv3 = w_kv.reshape((1,) + w_kv.shape)

    def layer_arrays(l):
        arrs = [("gu", 2 * F, D, [(gate_t, l, 0), (up_t, l, F)]), ("down", F, D, [(ffn_w_down, l, 0)])]
        if l < LA:
            arrs += [("win", D, ZC, [(a_w_in, l, 0)]), ("wout", DS, D, [(a_w_out, l, 0)])]
            if l == LA - 1:
                arrs.append(("wkv", DS, KVW, [(w_kv3, 0, 0)]))
        else:
            i_b = l - LA
            arrs.append(("wqo", 2 * DS, D, [(b_w_q, i_b, 0), (b_w_o, i_b, DS)]))
        return arrs

    started = []
    token = None
    for l in range(L):
        keys, bufs = [], []
        for key, rows, width, sources in layer_arrays(l):
            buf = lax.empty((NDEV, rows, width), BF16)
            for si, (src, li, row0) in enumerate(sources):
                buf = _cast_into(f"cast_{key}{l}_{si}", src, li, buf, row0, me1)
            keys.append(key)
            bufs.append(buf)
        if l == 0:
            nv_rows = _pack([a_norm_v])
            nv = _cast_into("put_norm_v", nv_rows.reshape((1,) + nv_rows.shape), 0,
                            lax.empty((NDEV,) + nv_rows.shape, F32), 0, me1)
            keys.append("norm_v")
            bufs.append(nv)
        send, recv, bufs, token = _gather_start(f"gather_start{l}", bufs, [] if token is None else [token])
        started.append((keys, bufs, send, recv))

    def finish_gather(l, deps):
        keys, bufs, send, recv = started[l]
        fsend, frecv, bufs = _gather_forward(f"gather_forward{l}", bufs, send, recv, deps)
        bufs = _gather_finish(f"gather_finish{l}", bufs, send, recv, fsend, frecv)
        return dict(zip(keys, bufs))

    buckets = jnp.asarray(_bucket_table())
    bias = _bias_table("bias_table", rel_bias.T, buckets).reshape(NH, BLOCK, 2 * BLOCK)

    def rows_full(tm):
        return pl.BlockSpec((tm, D), lambda i, j: (i, 0))

    def tile(tm, tn):
        return pl.BlockSpec((tm, tn), lambda i, j: (i, j))

    vec_tile = pl.BlockSpec((1, TN_), lambda i, j: (0, j))

    def ffn_forward(l, wl, h_mid, tag):
        xf = _rms_fwd(f"ffn_norm_fwd{tag}", h_mid, ffn_norm[l])

        def ep(parts, ex, outs):
            a, b = parts
            outs[0][0] = a.astype(BF16)
            outs[0][1] = b.astype(BF16)
            outs[1][...] = (a * jax.nn.sigmoid(a) * b).astype(BF16)

        ab, hid = _gemm(
            f"ffn_up{tag}", (S // TM, NDEV),
            [(xf, rows_full(TM)),
             (wl["gu"], pl.BlockSpec((None, F, D), lambda i, e: (e, 0, 0))),
             (wl["gu"], pl.BlockSpec((None, F, D), lambda i, e: (e, 1, 0)))],
            [(0, 1, NT), (0, 2, NT)], [],
            [(_sds((2, NDEV, S, F), BF16), pl.BlockSpec((2, None, TM, F), lambda i, e: (0, e, i, 0))),
             (_sds((NDEV, S, F), BF16), pl.BlockSpec((None, TM, F), lambda i, e: (e, i, 0)))],
            ep, separate=True)
        (h_out,) = _gemm(
            f"ffn_down{tag}", (S // TM, D // TN_, NDEV),
            [(hid, pl.BlockSpec((None, TM, F), lambda i, j, e: (e, i, 0))),
             (wl["down"], pl.BlockSpec((None, F, TN_), lambda i, j, e: (e, 0, j)))],
            [(0, 1, NN)], [(h_mid, pl.BlockSpec((TM, TN_), lambda i, j, e: (i, j)))],
            [(_sds((S, D), F32), pl.BlockSpec((TM, TN_), lambda i, j, e: (i, j)))],
            _store_add_extra, nk=NDEV, acc_shape=(TM, TN_))
        return dict(h_mid=h_mid, xf=xf, ab=ab, hid=hid), h_out

    def stacked_rows_gemm(name, a, wmat, blk, extras, ep, out_dtype):
        return _gemm(
            name, (S // TM, D // TN_),
            [(a, rows_full(TM)), (wmat, pl.BlockSpec((NDEV, DS, TN_), lambda i, j: (0, blk, j)))],
            [(0, 1, NN, None, _stacked)], extras,
            [(_sds((S, D), out_dtype), tile(TM, TN_))], ep)[0]

    def back_rows_gemm(name, a, wmat, blk, out_dtype, deps=()):
        return _gemm(
            name, (S // TM, NDEV),
            [(a, rows_full(TM)), (wmat, pl.BlockSpec((None, DS, D), lambda i, e: (e, blk, 0)))],
            [(0, 1, NT)], [], [(_sds((S, D), out_dtype), pl.BlockSpec((TM, DS), lambda i, e: (i, e)))], _store,
            deps=deps)[0]

    def grad_rows_gemm(name, act, d_bf, buf, blk):
        return _gemm(
            name, (NDEV,),
            [(act, pl.BlockSpec((S, DS), lambda e: (0, e))), (d_bf, pl.BlockSpec((S, D), lambda e: (0, 0)))],
            [(0, 1, TN)], [(buf, ANY)],
            [(_sds(buf.shape, BF16), pl.BlockSpec((None, DS, D), lambda e: (e, blk, 0)))],
            _store, aliases={2: 0})[0]

    saved, weights = [], []
    h = x.reshape(S, D)
    k_heads = v_heads = hn = h_kv = norm_v = None
    for layer in range(L):
        wl = finish_gather(layer, [token] if layer == 0 else [h])
        weights.append(wl)
        if layer == 0:
            nv_all = wl["norm_v"]
            norm_v = jnp.transpose(nv_all.reshape(NDEV, -1)[:, :LA * DS].reshape(NDEV, LA, DS), (1, 0, 2)).reshape(LA, D)
        sv = dict(h_in=h)
        xn = _rms_fwd(f"mix_norm_fwd{layer}", h, mix_norm[layer])
        sv["xn"] = xn
        if layer < LA:
            i_a = layer
            (zp,) = _gemm(
                f"gmlp_in{layer}", (S // TM, NDEV),
                [(xn, rows_full(TM)), (wl["win"], pl.BlockSpec((None, D, ZC), lambda i, e: (e, 0, 0)))],
                [(0, 1, NN)], [], [(_sds((S, 2 * D), F32), pl.BlockSpec((TM, ZC), lambda i, e: (i, e)))], _store)
            bst = a_b_s[i_a].T
            gated = _gmlp_fwd(f"gmlp_gate{layer}", zp, norm_v[i_a].reshape(1, D), a_w_s[i_a], bst)
            sv.update(zp=zp, gated=gated, bst=bst)
            h_mid = stacked_rows_gemm(f"gmlp_out{layer}", gated, wl["wout"], 0, [(h, tile(TM, TN_))],
                                      _store_add_extra, F32)
        else:
            i_b = layer - LA
            q = stacked_rows_gemm(f"attn_q{layer}", xn, wl["wqo"], 0, [(b_b_q[i_b].reshape(1, D), vec_tile)],
                                  _store_add_extra, BF16)
            attn = _attn_fwd(f"attn_fwd{layer}", q, k_heads, v_heads, bias, b_sinks[i_b])
            sv.update(q=q, attn=attn)
            h_mid = stacked_rows_gemm(f"attn_o{layer}", attn, wl["wqo"], 1,
                                      [(h, tile(TM, TN_)), (b_b_o[i_b].reshape(1, D), vec_tile)],
                                      _store_add_extra, F32)
        fsv, h = ffn_forward(layer, wl, h_mid, str(layer))
        sv.update(fsv)
        saved.append(sv)
        if layer == LA - 1:
            h_kv = h
            hn = _rms_fwd("kv_norm_fwd", h, kv_norm)

            def kv_ep(acc, ex, outs):
                val = acc + ex[0][...]
                for hh in range(NKV):
                    outs[0][hh] = val[:, hh * HEAD_DIM:(hh + 1) * HEAD_DIM].astype(BF16)
                    outs[1][hh] = val[:, (NKV + hh) * HEAD_DIM:(NKV + hh + 1) * HEAD_DIM].astype(BF16)

            k_heads, v_heads = _gemm(
                "kv_proj", (S // TM,),
                [(hn, pl.BlockSpec((TM, D), lambda i: (i, 0))),
                 (wl["wkv"], pl.BlockSpec((NDEV, DS, KVW), lambda i: (0, 0, 0)))],
                [(0, 1, NN, None, _stacked)], [(b_kv.reshape(1, KVW), pl.BlockSpec((1, KVW), lambda i: (0, 0)))],
                [(_sds((NKV, S, HEAD_DIM), BF16), pl.BlockSpec((NKV, TM, HEAD_DIM), lambda i: (0, i, 0)))] * 2,
                kv_ep)

    loss11, d, d_bf, g_final = _loss_bwd("loss_bwd", h, final_norm, loss_target.reshape(S, D))
    loss = lax.psum(loss11[0, 0], AXES)

    g_mix, g_ffn = [None] * L, [None] * L
    g_ws, g_bs, g_nv = [None] * LA, [None] * LA, [None] * LA
    g_bq, g_sink, g_bo = [None] * LB, [None] * LB, [None] * LB
    dbiases = []
    kv_parts = []
    g_kvn = g_bkv = None
    landed = [None] * L
    pending = None
    grads_wkv = None

    def new_grads(l):
        return {key: lax.empty((NDEV, rows, width), BF16) for key, rows, width, _ in layer_arrays(l)}

    def exchange_begin(l, gl):
        keys = list(gl.keys())
        grads = [gl[k] for k in keys]
        lands = [lax.empty((NCHIP,) + g.shape[1:], BF16) for g in grads]
        send, recv, grads, lands, tok = _sibling_start(f"rs_sibling_start{l}", grads, lands, [])
        return dict(layer=l, keys=keys, grads=grads, lands=lands, send=send, recv=recv, token=tok)

    def exchange_middle(st, dep):
        l = st["layer"]
        grads, lands = _sibling_finish(f"rs_sibling_finish{l}", st["grads"], st["lands"], st["send"], st["recv"], [dep])
        sums, own = [], []
        for t, key in enumerate(st["keys"]):
            s_, o_ = _pair_sum(f"pair_sum_{key}{l}", grads[t], lands[t], where)
            sums.append(s_)
            own.append(o_)
        send, recv, sums, own, tok = _chips_start(f"rs_chips_start{l}", sums, own, [])
        st.update(sums=sums, own=own, send2=send, recv2=recv, token=tok)

    def exchange_end(st, dep):
        l = st["layer"]
        lands = _chips_finish(f"rs_chips_finish{l}", st["sums"], st["own"], st["send2"], st["recv2"], [dep])
        landed[l] = dict(zip(st["keys"], lands))

    for layer in reversed(range(L)):
        sv, wl = saved[layer], weights[layer]
        tag = str(layer)
        gl = new_grads(layer)
        if grads_wkv is not None and layer == LA - 1:
            gl["wkv"] = grads_wkv
        def dhid_ep(acc, ex, outs):
            a, b = ex[0][0].astype(F32), ex[0][1].astype(F32)
            sg = jax.nn.sigmoid(a)
            outs[0][0] = (acc * b * (sg * (1.0 + a * (1.0 - sg)))).astype(BF16)
            outs[0][1] = (acc * (a * sg)).astype(BF16)

        ab_spec = pl.BlockSpec((2, None, TM, F), lambda i, e: (0, e, i, 0))
        (dab,) = _gemm(
            f"ffn_dhid{tag}", (S // TM, NDEV),
            [(d_bf, rows_full(TM)), (wl["down"], pl.BlockSpec((None, F, D), lambda i, e: (e, 0, 0)))],
            [(0, 1, NT)], [(sv["ab"], ab_spec)], [(_sds((2, NDEV, S, F), BF16), ab_spec)], dhid_ep,
            deps=[pending["token"]] if pending else [])
        if pending:
            exchange_middle(pending, dab)
        (gl["down"],) = _gemm(
            f"ffn_dwdown{tag}", (NDEV,),
            [(sv["hid"], pl.BlockSpec((None, S, F), lambda e: (e, 0, 0))),
             (d_bf, pl.BlockSpec((S, D), lambda e: (0, 0)))],
            [(0, 1, TN)], [(gl["down"], ANY)],
            [(_sds(gl["down"].shape, BF16), pl.BlockSpec((None, F, D), lambda e: (e, 0, 0)))],
            _store, aliases={2: 0}, deps=[pending["token"]] if pending else [])
        (gl["gu"],) = _gemm(
            f"ffn_dwup{tag}", (2, NDEV),
            [(dab, pl.BlockSpec((None, None, S, F), lambda w, e: (w, e, 0, 0))),
             (sv["xf"], pl.BlockSpec((S, D), lambda w, e: (0, 0)))],
            [(0, 1, TN)], [(gl["gu"], ANY)],
            [(_sds(gl["gu"].shape, BF16), pl.BlockSpec((None, F, D), lambda w, e: (e, w, 0)))],
            _store, aliases={2: 0})
        (dxf,) = _gemm(
            f"ffn_dx{tag}", (S // TM, D // TN_, 2 * NDEV),
            [(dab, pl.BlockSpec((None, None, TM, F), lambda i, j, k: (k // NDEV, k % NDEV, i, 0))),
             (wl["gu"], pl.BlockSpec((None, F, TN_), lambda i, j, k: (k % NDEV, k // NDEV, j)))],
            [(0, 1, NN)], [], [(_sds((S, D), F32), pl.BlockSpec((TM, TN_), lambda i, j, k: (i, j)))],
            _store, nk=2 * NDEV, acc_shape=(TM, TN_))
        d, d_bf, g_ffn[layer], colsum = _rms_bwd(f"ffn_norm_bwd{tag}", sv["h_mid"], ffn_norm[layer], dxf, d)
        if layer < LA:
            i_a = layer
            dgated = back_rows_gemm(f"gmlp_dgated{tag}", d_bf, wl["wout"], 0, F32)
            gl["wout"] = grad_rows_gemm(f"gmlp_dwout{tag}", sv["gated"], d_bf, gl["wout"], 0)
            dzp, g_ws[i_a], dbs, g_nv[i_a] = _gmlp_bwd(f"gmlp_bwd{tag}", sv["zp"], dgated,
                                                       norm_v[i_a].reshape(1, D), a_w_s[i_a], sv["bst"])
            g_bs[i_a] = dbs[:, 0, :]
            (gl["win"],) = _gemm(
                f"gmlp_dwin{tag}", (NDEV, D // TS),
                [(sv["xn"], pl.BlockSpec((S, TS), lambda e, i: (0, i))),
                 (dzp, pl.BlockSpec((S, ZC), lambda e, i: (0, e)))],
                [(0, 1, TN)], [(gl["win"], ANY)],
                [(_sds(gl["win"].shape, BF16), pl.BlockSpec((None, TS, ZC), lambda e, i: (e, i, 0)))],
                _store, aliases={2: 0})
            (dxn,) = _gemm(
                f"gmlp_dx{tag}", (S // TM, D // TN_, NDEV),
                [(dzp, pl.BlockSpec((TM, ZC), lambda i, j, e: (i, e))),
                 (wl["win"], pl.BlockSpec((None, TN_, ZC), lambda i, j, e: (e, j, 0)))],
                [(0, 1, NT)], [], [(_sds((S, D), F32), pl.BlockSpec((TM, TN_), lambda i, j, e: (i, j)))],
                _store, nk=NDEV, acc_shape=(TM, TN_))
        else:
            i_b = layer - LA
            g_bo[i_b] = colsum
            dattn = back_rows_gemm(f"attn_dout{tag}", d_bf, wl["wqo"], 1, BF16)
            gl["wqo"] = grad_rows_gemm(f"attn_dwo{tag}", sv["attn"], d_bf, gl["wqo"], 1)
            dq, g_bq[i_b], dkc, dkp, dvc, dvp, dbias, dsink = _attn_bwd(
                f"attn_bwd{tag}", sv["q"], k_heads, v_heads, dattn, bias, b_sinks[i_b])
            kv_parts.append((dkc, dkp, dvc, dvp))
            g_sink[i_b] = dsink.reshape(NH)
            dbiases.append(dbias.reshape(NH, BLOCK * 2 * BLOCK))
            gl["wqo"] = grad_rows_gemm(f"attn_dwq{tag}", sv["xn"], dq, gl["wqo"], 0)
            dxn = back_rows_gemm(f"attn_dx{tag}", dq, wl["wqo"], 0, F32)
        d, d_bf, g_mix[layer], _ = _rms_bwd(f"mix_norm_bwd{tag}", sv["h_in"], mix_norm[layer], dxn, d)
        if pending:
            exchange_end(pending, d)
        pending = exchange_begin(layer, gl)
        if layer == LA:
            wkv = weights[LA - 1]["wkv"]
            dkv, g_bkv = _kv_grad("kv_grad", kv_parts)
            (grads_wkv,) = _gemm(
                "kv_dw", (NDEV,),
                [(hn, pl.BlockSpec((S, DS), lambda e: (0, e))), (dkv, pl.BlockSpec((S, KVW), lambda e: (0, 0)))],
                [(0, 1, TN)], [(lax.empty((NDEV, DS, KVW), BF16), ANY)],
                [(_sds((NDEV, DS, KVW), BF16), pl.BlockSpec((None, DS, KVW), lambda e: (e, 0, 0)))],
                _store, aliases={2: 0}, deps=[pending["token"]])
            (dhn,) = _gemm(
                "kv_dx", (S // TM, NDEV),
                [(dkv, pl.BlockSpec((TM, KVW), lambda i, e: (i, 0))),
                 (wkv, pl.BlockSpec((None, DS, KVW), lambda i, e: (e, 0, 0)))],
                [(0, 1, NT)], [], [(_sds((S, D), F32), pl.BlockSpec((TM, DS), lambda i, e: (i, e)))], _store)
            d, d_bf, g_kvn, _ = _rms_bwd("kv_norm_bwd", h_kv, kv_norm, dhn, d)
    grad_x = d.reshape(x.shape)

    exchange_middle(pending, d)

    results = {}

    def upd(pname, w, m, v, l, li, key, row0, dep=None):
        w3 = w if w.ndim == 3 else w.reshape((1,) + w.shape)
        prev = results.get(pname) or [lax.empty(w3.shape, F32) for _ in range(4)]
        results[pname] = _adamw_shard(f"adamw_{pname}{l}", w3, m.reshape(w3.shape), v.reshape(w3.shape), landed[l][key],
                                      row0, li, prev, deps=[dep] if dep is not None else [])

    first_dep = pending["token"]
    for l in list(range(1, L)) + [0]:
        if l == 0:
            exchange_end(pending, results["ffn_w_down"][0])
        dep, first_dep = first_dep, None
        upd("ffn_w_gate", gate_t, tr3(m_ffn_w_gate), tr3(v_ffn_w_gate), l, l, "gu", 0, dep)
        upd("ffn_w_up", up_t, tr3(m_ffn_w_up), tr3(v_ffn_w_up), l, l, "gu", F)
        upd("ffn_w_down", ffn_w_down, m_ffn_w_down, v_ffn_w_down, l, l, "down", 0)
        if l < LA:
            upd("a_w_in", a_w_in, m_a_w_in, v_a_w_in, l, l, "win", 0)
            upd("a_w_out", a_w_out, m_a_w_out, v_a_w_out, l, l, "wout", 0)
            if l == LA - 1:
                upd("w_kv", w_kv, m_w_kv, v_w_kv, l, 0, "wkv", 0)
        else:
            upd("b_w_q", b_w_q, m_b_w_q, v_b_w_q, l, l - LA, "wqo", 0)
            upd("b_w_o", b_w_o, m_b_w_o, v_b_w_o, l, l - LA, "wqo", DS)
    for pname in ("ffn_w_gate", "ffn_w_up"):
        results[pname] = [tr3(r) for r in results[pname]]
    results["w_kv"] = [r.reshape(w_kv.shape) for r in results["w_kv"]]

    g_rel = _bias_grad("bias_grad", dbiases, buckets)
    small_local = [jnp.concatenate(g_mix, axis=0), jnp.concatenate(g_ffn, axis=0), jnp.stack(g_ws), jnp.stack(g_bs),
                   g_kvn, g_bkv, jnp.concatenate(g_bq, axis=0), jnp.stack(g_sink), jnp.concatenate(g_bo, axis=0),
                   g_rel, g_final, jnp.concatenate(g_nv, axis=0)]
    small_w = [mix_norm, ffn_norm, a_w_s, a_b_s, kv_norm, b_kv, b_b_q, b_sinks, b_b_o, rel_bias, final_norm]
    small_m = [m_mix_norm, m_ffn_norm, m_a_w_s, m_a_b_s, m_kv_norm, m_b_kv, m_b_b_q, m_b_sinks, m_b_b_o, m_rel_bias,
               m_final_norm]
    small_v = [v_mix_norm, v_ffn_norm, v_a_w_s, v_a_b_s, v_kv_norm, v_b_kv, v_b_b_q, v_b_sinks, v_b_b_o, v_rel_bias,
               v_final_norm]
    shapes = [w.shape for w in small_w] + [(LA, D)]
    (small_all,) = _all_gather("gather_small_grads", [_pack(small_local)])
    small_sum = _sum_devices("sum_small_grads", small_all)
    small_g = _unpack(small_sum, shapes)
    g_normv = lax.dynamic_slice_in_dim(small_g[-1], me * DS, DS, axis=1)
    small_g = small_g[:-1] + [g_normv]
    small_w, small_m, small_v = small_w + [a_norm_v], small_m + [m_a_norm_v], small_v + [v_a_norm_v]
    shapes = [w.shape for w in small_w]
    s_delta, s_m, s_v = _adamw_flat("adamw_small", _pack(small_w), _pack(small_g), _pack(small_m), _pack(small_v))
    s_delta, s_m, s_v = _unpack(s_delta, shapes), _unpack(s_m, shapes), _unpack(s_v, shapes)

    names = ["mix_norm", "ffn_norm", "a_w_in", "a_norm_v", "a_w_s", "a_b_s", "a_w_out", "kv_norm", "w_kv", "b_kv",
             "b_w_q", "b_b_q", "b_sinks", "b_w_o", "b_b_o", "rel_bias", "ffn_w_gate", "ffn_w_up", "ffn_w_down",
             "final_norm"]
    small_names = ["mix_norm", "ffn_norm", "a_w_s", "a_b_s", "kv_norm", "b_kv", "b_b_q", "b_sinks", "b_b_o", "rel_bias",
                   "final_norm", "a_norm_v"]
    res = {}
    for idx, nm in enumerate(small_names):
        res[nm] = (small_g[idx].reshape(shapes[idx]), s_delta[idx], s_m[idx], s_v[idx])
    for nm, u in results.items():
        res[nm] = tuple(u)
    out = [loss, grad_x]
    for part in range(4):
        out += [res[nm][part] for nm in names]
    return tuple(out)
```

```python
import math

import numpy as np
import jax
import jax.numpy as jnp
from jax import lax
from jax.experimental import pallas as pl
from jax.experimental.pallas import tpu as pltpu

F32 = jnp.float32
BF16 = jnp.bfloat16
AXES = ("x", "y", "c")
NDEV = 8
NCHIP = 4
CHUNK = 128
GROUPS = 8
HEAD_DIM = 64
KV_GROUP = 8
BLOCK = 128
N_BUCKETS = 32
MAX_DISTANCE = 128
RMS_EPS = 1e-5
NEG_INF = -1e30
ADAM_LR, ADAM_B1, ADAM_B2, ADAM_EPS, ADAM_WD, ADAM_STEP = 0.001, 0.9, 0.999, 1e-08, 0.01, 10
VMEM_LIMIT_BYTES = 56 * 1024 * 1024

NN = (((1,), (0,)), ((), ()))
NT = (((1,), (1,)), ((), ()))
TN = (((0,), (0,)), ((), ()))
ANY = pl.BlockSpec(memory_space=pl.ANY)
HBM = pl.BlockSpec(memory_space=pltpu.HBM)
SEM = pl.BlockSpec(memory_space=pltpu.SEMAPHORE)
MESH = pl.DeviceIdType.MESH
EFFECT = pltpu.SideEffectType.DATAFLOW_SIDE_EFFECTING


def _pcall(body, *, name, out_shape, in_specs, out_specs, grid=(), scratch=(), aliases=None, prefetch=0, deps=()):
    n_in, n_dep = len(in_specs), len(deps)
    if n_dep:
        inner = body

        def body(*refs):
            return inner(*refs[:prefetch + n_in], *refs[prefetch + n_in + n_dep:])

        in_specs = list(in_specs) + [ANY] * n_dep
    params = dict(vmem_limit_bytes=VMEM_LIMIT_BYTES)
    if grid:
        params["dimension_semantics"] = ("arbitrary",) * len(grid)
    kw = dict(name=name, out_shape=out_shape, compiler_params=pltpu.CompilerParams(**params),
              input_output_aliases=aliases or {})
    if prefetch:
        kw["grid_spec"] = pltpu.PrefetchScalarGridSpec(num_scalar_prefetch=prefetch, grid=grid, in_specs=in_specs,
                                                       out_specs=out_specs, scratch_shapes=list(scratch))
    else:
        kw.update(grid=grid, in_specs=in_specs, out_specs=out_specs, scratch_shapes=list(scratch))
    call = pl.pallas_call(body, **kw)
    return lambda *args: call(*args, *deps)


def _sds(shape, dtype):
    return jax.ShapeDtypeStruct(tuple(shape), dtype)


def _position():
    x, y, c = lax.axis_index("x"), lax.axis_index("y"), lax.axis_index("c")
    chips = [(1 - x, y), (x, 1 - y), (1 - x, 1 - y)]
    return x, y, c, chips


def _slot(px, py, pc):
    return 4 * px + 2 * py + pc


def _remote(ref_src, ref_dst, send, recv, to):
    return pltpu.make_async_remote_copy(src_ref=ref_src, dst_ref=ref_dst, send_sem=send, recv_sem=recv,
                                        device_id=to, device_id_type=MESH)


def _hbm(arrays):
    return [pltpu.with_memory_space_constraint(a, pltpu.HBM) for a in arrays]


def _split_call(body, name, out_shape, in_specs, out_specs, aliases):
    return pl.pallas_call(body, name=name, out_shape=out_shape, in_specs=in_specs, out_specs=out_specs,
                          input_output_aliases=aliases, compiler_params=pltpu.CompilerParams(has_side_effects=EFFECT))


def _token_shape():
    return _sds((8, 128), F32)


def _gather_start(name, bufs, deps):
    n, nd = len(bufs), len(deps)

    def body(*refs):
        ins, send, recv, token = refs[:n], refs[n + nd], refs[n + nd + 1], refs[2 * n + nd + 2]
        x, y, c, chips = _position()
        peers = [(x, y, 1 - c)] + [(*chip, c) for chip in chips]
        for t in range(n):
            mine = ins[t].at[_slot(x, y, c)]
            for k, peer in enumerate(peers):
                _remote(mine, mine, send.at[4 * t + k], recv.at[4 * t + k], peer).start()
        token[...] = jnp.zeros_like(token)

    res = _split_call(
        body, name,
        (pltpu.SemaphoreType.DMA((4 * n,)), pltpu.SemaphoreType.DMA((4 * n,)), *[pltpu.HBM(b.shape, b.dtype) for b in bufs],
         _token_shape()),
        [HBM] * n + [ANY] * nd, (SEM, SEM, *[HBM] * n, pl.BlockSpec(memory_space=pltpu.VMEM)),
        {t: 2 + t for t in range(n)})(*_hbm(bufs), *deps)
    return res[0], res[1], list(res[2:2 + n]), res[2 + n]


def _gather_forward(name, bufs, send, recv, deps):
    n, nd = len(bufs), len(deps)

    def body(*refs):
        ins, send_in, recv_in = refs[:n], refs[n], refs[n + 1]
        fsend, frecv = refs[n + 2 + nd], refs[n + 3 + nd]
        x, y, c, chips = _position()
        for j, chip in enumerate(chips):
            for t in range(n):
                blk = ins[t].at[_slot(*chip, c)]
                _remote(blk, blk, send_in.at[4 * t + 1 + j], recv_in.at[4 * t + 1 + j], (*chip, c)).wait_recv()
                _remote(blk, blk, fsend.at[3 * t + j], frecv.at[3 * t + j], (x, y, 1 - c)).start()

    res = _split_call(
        body, name,
        (pltpu.SemaphoreType.DMA((3 * n,)), pltpu.SemaphoreType.DMA((3 * n,)), *[pltpu.HBM(b.shape, b.dtype) for b in bufs]),
        [HBM] * n + [SEM, SEM] + [ANY] * nd, (SEM, SEM, *[HBM] * n),
        {t: 2 + t for t in range(n)})(*_hbm(bufs), send, recv, *deps)
    return res[0], res[1], list(res[2:])


def _gather_finish(name, bufs, send, recv, fsend, frecv):
    n = len(bufs)

    def body(*refs):
        ins, send_in, recv_in, fs_in, fr_in = refs[:n], refs[n], refs[n + 1], refs[n + 2], refs[n + 3]
        x, y, c, chips = _position()
        sibling = (x, y, 1 - c)
        peers = [sibling] + [(*chip, c) for chip in chips]
        for t in range(n):
            blk = ins[t].at[_slot(x, y, 1 - c)]
            _remote(blk, blk, send_in.at[4 * t], recv_in.at[4 * t], sibling).wait_recv()
            for j, chip in enumerate(chips):
                blk = ins[t].at[_slot(*chip, 1 - c)]
                _remote(blk, blk, fs_in.at[3 * t + j], fr_in.at[3 * t + j], sibling).wait_recv()
            mine = ins[t].at[_slot(x, y, c)]
            for k, peer in enumerate(peers):
                _remote(mine, mine, send_in.at[4 * t + k], recv_in.at[4 * t + k], peer).wait_send()
            for j, chip in enumerate(chips):
                blk = ins[t].at[_slot(*chip, c)]
                _remote(blk, blk, fs_in.at[3 * t + j], fr_in.at[3 * t + j], sibling).wait_send()

    res = _split_call(
        body, name, tuple(pltpu.HBM(b.shape, b.dtype) for b in bufs),
        [HBM] * n + [SEM] * 4, tuple([HBM] * n), {t: t for t in range(n)})(*_hbm(bufs), send, recv, fsend, frecv)
    return list(res)


def _sibling_start(name, grads, lands, deps):
    n, nd = len(grads), len(deps)

    def body(*refs):
        g_in, l_in = refs[:n], refs[n:2 * n]
        send, recv, token = refs[2 * n + nd], refs[2 * n + nd + 1], refs[4 * n + nd + 2]
        x, y, c, _ = _position()
        for t in range(n):
            for k in range(NCHIP):
                _remote(g_in[t].at[2 * k + (1 - c)], l_in[t].at[k], send.at[NCHIP * t + k], recv.at[NCHIP * t + k],
                        (x, y, 1 - c)).start()
        token[...] = jnp.zeros_like(token)

    both = list(grads) + list(lands)
    res = _split_call(
        body, name,
        (pltpu.SemaphoreType.DMA((NCHIP * n,)), pltpu.SemaphoreType.DMA((NCHIP * n,)),
         *[pltpu.HBM(b.shape, b.dtype) for b in both], _token_shape()),
        [HBM] * (2 * n) + [ANY] * nd, (SEM, SEM, *[HBM] * (2 * n), pl.BlockSpec(memory_space=pltpu.VMEM)),
        {t: 2 + t for t in range(2 * n)})(*_hbm(both), *deps)
    return res[0], res[1], list(res[2:2 + n]), list(res[2 + n:2 + 2 * n]), res[2 + 2 * n]


def _sibling_finish(name, grads, lands, send, recv, deps):
    n, nd = len(grads), len(deps)

    def body(*refs):
        g_in, l_in, send_in, recv_in = refs[:n], refs[n:2 * n], refs[2 * n], refs[2 * n + 1]
        x, y, c, _ = _position()
        for t in range(n):
            for k in range(NCHIP):
                cp = _remote(g_in[t].at[2 * k + (1 - c)], l_in[t].at[k], send_in.at[NCHIP * t + k],
                             recv_in.at[NCHIP * t + k], (x, y, 1 - c))
                cp.wait_send()
                cp.wait_recv()

    both = list(grads) + list(lands)
    res = _split_call(
        body, name, tuple(pltpu.HBM(b.shape, b.dtype) for b in both),
        [HBM] * (2 * n) + [SEM, SEM] + [ANY] * nd, tuple([HBM] * (2 * n)),
        {t: t for t in range(2 * n)})(*_hbm(both), send, recv, *deps)
    return list(res[:n]), list(res[n:])


def _chips_start(name, parts, lands, deps):
    n, nd = len(parts), len(deps)

    def body(*refs):
        p_in, l_in = refs[:n], refs[n:2 * n]
        send, recv, token = refs[2 * n + nd], refs[2 * n + nd + 1], refs[4 * n + nd + 2]
        x, y, c, chips = _position()
        for t in range(n):
            for j, chip in enumerate(chips):
                _remote(p_in[t].at[2 * chip[0] + chip[1]], l_in[t].at[2 * x + y], send.at[3 * t + j], recv.at[3 * t + j],
                        (*chip, c)).start()
        token[...] = jnp.zeros_like(token)

    both = list(parts) + list(lands)
    res = _split_call(
        body, name,
        (pltpu.SemaphoreType.DMA((3 * n,)), pltpu.SemaphoreType.DMA((3 * n,)), *[pltpu.HBM(b.shape, b.dtype) for b in both],
         _token_shape()),
        [HBM] * (2 * n) + [ANY] * nd, (SEM, SEM, *[HBM] * (2 * n), pl.BlockSpec(memory_space=pltpu.VMEM)),
        {t: 2 + t for t in range(2 * n)})(*_hbm(both), *deps)
    return res[0], res[1], list(res[2:2 + n]), list(res[2 + n:2 + 2 * n]), res[2 + 2 * n]


def _chips_finish(name, parts, lands, send, recv, deps):
    n, nd = len(parts), len(deps)

    def body(*refs):
        p_in, l_in, send_in, recv_in = refs[:n], refs[n:2 * n], refs[2 * n], refs[2 * n + 1]
        x, y, c, chips = _position()
        for t in range(n):
            for j, chip in enumerate(chips):
                k = 2 * chip[0] + chip[1]
                _remote(p_in[t].at[k], l_in[t].at[k], send_in.at[3 * t + j], recv_in.at[3 * t + j], (*chip, c)).wait_recv()
                _remote(p_in[t].at[k], l_in[t].at[2 * x + y], send_in.at[3 * t + j], recv_in.at[3 * t + j],
                        (*chip, c)).wait_send()

    both = list(parts) + list(lands)
    res = _split_call(
        body, name, tuple(pltpu.HBM(b.shape, b.dtype) for b in both),
        [HBM] * (2 * n) + [SEM, SEM] + [ANY] * nd, tuple([HBM] * (2 * n)),
        {t: t for t in range(2 * n)})(*_hbm(both), send, recv, *deps)
    return list(res[n:])


def _all_gather(name, shards):
    n = len(shards)

    def body(*refs):
        src, dst = refs[:n], refs[n:2 * n]
        send_sems, recv_sems, local_sems = refs[2 * n:]
        x, y, c, chips = _position()
        me, sibling = (x, y, c), (x, y, 1 - c)

        def copy(t, k, block, to, from_shard=False):
            slot = dst[t].at[_slot(*block)]
            return _remote(src[t] if from_shard else slot, slot, send_sems.at[t, k], recv_sems.at[t, k], to)

        mine = [pltpu.make_async_copy(src[t], dst[t].at[_slot(x, y, c)], local_sems.at[t]) for t in range(n)]
        first, passed = [], []
        for t in range(n):
            mine[t].start()
            first.append(copy(t, 0, me, sibling, True))
            first += [copy(t, 1 + j, me, (*chip, c), True) for j, chip in enumerate(chips)]
        for cp in first:
            cp.start()
        for j, chip in enumerate(chips):
            for t in range(n):
                copy(t, 1 + j, (*chip, c), me).wait_recv()
                fwd = copy(t, 4 + j, (*chip, c), sibling)
                fwd.start()
                passed.append(fwd)
        for t in range(n):
            copy(t, 0, sibling, me).wait_recv()
            for j, chip in enumerate(chips):
                copy(t, 4 + j, (*chip, 1 - c), me).wait_recv()
        for cp in first + passed:
            cp.wait_send()
        for t in range(n):
            mine[t].wait()

    outs = _pcall(
        body, name=name, out_shape=[_sds((NDEV,) + s.shape, s.dtype) for s in shards],
        in_specs=[ANY] * n, out_specs=[ANY] * n,
        scratch=[pltpu.SemaphoreType.DMA((n, 7)), pltpu.SemaphoreType.DMA((n, 7)), pltpu.SemaphoreType.DMA((n,))],
    )(*shards)
    return list(outs)


def _pair_sum(name, grad, recv, where):
    _, r, w = grad.shape
    tr = _row_tile(r, w)
    g4 = grad.reshape(NCHIP, 2, r, w)

    def body(where_ref, g_ref, r_ref, o_ref, own_ref):
        val = (g_ref[...].astype(F32) + r_ref[...].astype(F32)).astype(o_ref.dtype)
        o_ref[...] = val

        @pl.when(pl.program_id(1) == where_ref[1])
        def _():
            own_ref[...] = val

    out = _sds((NCHIP, r, w), grad.dtype)
    return _pcall(
        body, name=name, out_shape=[out, out], grid=(r // tr, NCHIP), prefetch=1,
        in_specs=[pl.BlockSpec((None, None, tr, w), lambda i, k, wr: (k, wr[0], i, 0)),
                  pl.BlockSpec((None, tr, w), lambda i, k, wr: (k, i, 0))],
        out_specs=[pl.BlockSpec((None, tr, w), lambda i, k, wr: (k, i, 0)),
                   pl.BlockSpec((None, tr, w), lambda i, k, wr: (wr[1], i, 0))],
    )(where, g4, recv)


def _row_tile(rows, width, budget=2 * 1024 * 1024):
    best = None
    for t in range(16, rows + 1, 16):
        if rows % t == 0 and t * width * 4 <= budget:
            best = t
    if best is None and rows * width * 4 <= budget:
        best = rows
    assert best is not None, (rows, width)
    return best


def _gemm(name, grid, operands, prods, extras, outs, epilogue, *, nk=1, acc_shape=None, aliases=None, separate=False,
          deps=()):
    n_op, n_ex, n_out = len(operands), len(extras), len(outs)

    def body(*refs):
        ops, ex, out_refs = refs[:n_op], refs[n_op:n_op + n_ex], refs[n_op + n_ex:n_op + n_ex + n_out]
        parts = []
        for pr in prods:
            a, b = ops[pr[0]], ops[pr[1]]
            av = pr[3](a) if len(pr) > 3 and pr[3] else a[...]
            bv = pr[4](b) if len(pr) > 4 and pr[4] else b[...]
            parts.append(lax.dot_general(av, bv, pr[2], preferred_element_type=F32))
        if separate:
            epilogue(parts, ex, out_refs)
            return
        part = parts[0]
        for p in parts[1:]:
            part = part + p
        if nk == 1:
            epilogue(part, ex, out_refs)
        else:
            acc = refs[-1]
            k = pl.program_id(len(grid) - 1)

            @pl.when(k == 0)
            def _():
                acc[...] = part

            @pl.when(k > 0)
            def _():
                acc[...] += part

            @pl.when(k == nk - 1)
            def _():
                epilogue(acc[...], ex, out_refs)

    res = _pcall(
        body, name=name, out_shape=[o[0] for o in outs], grid=grid,
        in_specs=[o[1] for o in operands] + [e[1] for e in extras], out_specs=[o[1] for o in outs],
        scratch=[pltpu.VMEM(acc_shape, F32)] if nk > 1 else [], aliases=aliases, deps=deps,
    )(*[o[0] for o in operands], *[e[0] for e in extras])
    return list(res)


def _store(acc, ex, outs):
    outs[0][...] = acc.astype(outs[0].dtype)


def _store_add_extra(acc, ex, outs):
    v = acc
    for e in ex:
        v = v + e[...]
    outs[0][...] = v.astype(outs[0].dtype)


def _stacked(ref):
    b = ref[...]
    return b.reshape(b.shape[0] * b.shape[1], b.shape[2])


def _gelu_parts(z):
    c = math.sqrt(2.0 / math.pi)
    t = jnp.tanh(c * (z + 0.044715 * (z * z * z)))
    val = 0.5 * z * (1.0 + t)
    grad = 0.5 * (1.0 + t) + 0.5 * z * (1.0 - t * t) * (c * (1.0 + 3.0 * 0.044715 * z * z))
    return val, grad


def _rms_fwd(name, h, g, deps=()):
    s, d = h.shape
    tr = _row_tile(s, d)

    def body(h_ref, g_ref, o_ref):
        hv = h_ref[...]
        r = lax.rsqrt(jnp.mean(hv * hv, axis=-1, keepdims=True) + RMS_EPS)
        o_ref[...] = (hv * r * g_ref[...]).astype(o_ref.dtype)

    return _pcall(
        body, name=name, out_shape=_sds((s, d), BF16), grid=(s // tr,),
        in_specs=[pl.BlockSpec((tr, d), lambda i: (i, 0)), pl.BlockSpec((1, d), lambda i: (0, 0))],
        out_specs=pl.BlockSpec((tr, d), lambda i: (i, 0)), deps=deps,
    )(h, g.reshape(1, d))


def _accumulate(ref, val, first):
    @pl.when(first)
    def _():
        ref[...] = val

    @pl.when(jnp.logical_not(first))
    def _():
        ref[...] += val


def _rms_bwd(name, h, g, dy, res, deps=()):
    s, d = h.shape
    tr = _row_tile(s, d, budget=1024 * 1024)

    def body(h_ref, g_ref, dy_ref, res_ref, dh_ref, dhb_ref, dg_ref, cs_ref):
        hv = h_ref[...]
        r = lax.rsqrt(jnp.mean(hv * hv, axis=-1, keepdims=True) + RMS_EPS)
        xhat = hv * r
        dyv = dy_ref[...]
        dxh = dyv * g_ref[...]
        dh = res_ref[...] + r * (dxh - xhat * jnp.mean(dxh * xhat, axis=-1, keepdims=True))
        dh_ref[...] = dh
        dhb_ref[...] = dh.astype(BF16)
        first = pl.program_id(0) == 0
        _accumulate(dg_ref, jnp.sum(dyv * xhat, axis=0, keepdims=True), first)
        _accumulate(cs_ref, jnp.sum(dh, axis=0, keepdims=True), first)

    row = pl.BlockSpec((tr, d), lambda i: (i, 0))
    vec = pl.BlockSpec((1, d), lambda i: (0, 0))
    return _pcall(
        body, name=name, out_shape=[_sds((s, d), F32), _sds((s, d), BF16), _sds((1, d), F32), _sds((1, d), F32)],
        grid=(s // tr,), in_specs=[row, vec, row, row], out_specs=[row, row, vec, vec], deps=deps,
    )(h, g.reshape(1, d), dy, res)


def _loss_bwd(name, h, g, target):
    s, d = h.shape
    tr = _row_tile(s, d, budget=1024 * 1024)

    def body(h_ref, g_ref, t_ref, loss_ref, dh_ref, dhb_ref, dg_ref):
        hv = h_ref[...]
        r = lax.rsqrt(jnp.mean(hv * hv, axis=-1, keepdims=True) + RMS_EPS)
        xhat = hv * r
        diff = xhat * g_ref[...] - t_ref[...]
        part = jnp.sum(jnp.sum(diff * diff, axis=1, keepdims=True), axis=0, keepdims=True) * (0.5 / d)
        dyv = diff * (1.0 / d)
        dxh = dyv * g_ref[...]
        dh = r * (dxh - xhat * jnp.mean(dxh * xhat, axis=-1, keepdims=True))
        dh_ref[...] = dh
        dhb_ref[...] = dh.astype(BF16)
        first = pl.program_id(0) == 0
        _accumulate(loss_ref, part, first)
        _accumulate(dg_ref, jnp.sum(dyv * xhat, axis=0, keepdims=True), first)

    row = pl.BlockSpec((tr, d), lambda i: (i, 0))
    vec = pl.BlockSpec((1, d), lambda i: (0, 0))
    one = pl.BlockSpec((1, 1), lambda i: (0, 0))
    return _pcall(
        body, name=name, out_shape=[_sds((1, 1), F32), _sds((s, d), F32), _sds((s, d), BF16), _sds((1, d), F32)],
        grid=(s // tr,), in_specs=[row, vec, row], out_specs=[one, row, row, vec],
    )(h, g.reshape(1, d), target)


def _tril_mask():
    return lax.broadcasted_iota(jnp.int32, (CHUNK, CHUNK), 0) >= lax.broadcasted_iota(jnp.int32, (CHUNK, CHUNK), 1)


def _gmlp_fwd(name, zp, gv, ws, bst):
    s, d2 = zp.shape
    d = d2 // 2
    gw = d // GROUPS

    def body(zp_ref, gv_ref, ws_ref, bst_ref, o_ref):
        u, _ = _gelu_parts(zp_ref[:, :d])
        v, _ = _gelu_parts(zp_ref[:, d:])
        rv = lax.rsqrt(jnp.mean(v * v, axis=-1, keepdims=True) + RMS_EPS)
        vn = (v * rv * gv_ref[...]).astype(BF16)
        tril = _tril_mask()
        for g in range(GROUPS):
            sl = slice(g * gw, (g + 1) * gw)
            wc = jnp.where(tril, ws_ref[g], 0.0).astype(BF16)
            sg = jnp.dot(wc, vn[:, sl], preferred_element_type=F32) + bst_ref[:, g:g + 1]
            o_ref[:, sl] = (u[:, sl] * sg).astype(o_ref.dtype)

    return _pcall(
        body, name=name, out_shape=_sds((s, d), BF16), grid=(s // CHUNK,),
        in_specs=[pl.BlockSpec((CHUNK, d2), lambda i: (i, 0)), pl.BlockSpec((1, d), lambda i: (0, 0)),
                  pl.BlockSpec((GROUPS, CHUNK, CHUNK), lambda i: (0, 0, 0)),
                  pl.BlockSpec((CHUNK, GROUPS), lambda i: (0, 0))],
        out_specs=pl.BlockSpec((CHUNK, d), lambda i: (i, 0)),
    )(zp, gv, ws, bst)


def _gmlp_bwd(name, zp, dgated, gv, ws, bst):
    s, d2 = zp.shape
    d = d2 // 2
    gw = d // GROUPS

    def body(zp_ref, dg_ref, gv_ref, ws_ref, bst_ref, dzp_ref, dws_ref, dbs_ref, dgv_ref, dvn_ref):
        u, gu = _gelu_parts(zp_ref[:, :d])
        v, gvv = _gelu_parts(zp_ref[:, d:])
        rv = lax.rsqrt(jnp.mean(v * v, axis=-1, keepdims=True) + RMS_EPS)
        vhat = v * rv
        vn = (vhat * gv_ref[...]).astype(BF16)
        tril = _tril_mask()
        first = pl.program_id(0) == 0
        ones = jnp.ones((8, gw), F32)

        @pl.when(first)
        def _():
            dws_ref[...] = jnp.zeros_like(dws_ref)
            dbs_ref[...] = jnp.zeros_like(dbs_ref)

        for g in range(GROUPS):
            sl = slice(g * gw, (g + 1) * gw)
            wc = jnp.where(tril, ws_ref[g], 0.0).astype(BF16)
            sg = jnp.dot(wc, vn[:, sl], preferred_element_type=F32) + bst_ref[:, g:g + 1]
            dgs = dg_ref[:, sl]
            ds = dgs * u[:, sl]
            dsb = ds.astype(BF16)
            dzp_ref[:, sl] = (dgs * sg * gu[:, sl]).astype(dzp_ref.dtype)
            dvn_ref[:, sl] = lax.dot_general(wc, dsb, TN, preferred_element_type=F32)
            dw = lax.dot_general(dsb, vn[:, sl], NT, preferred_element_type=F32)
            dws_ref[g] += jnp.where(tril, dw, 0.0)
            dbs_ref[g] += lax.dot_general(ones, ds, NT, preferred_element_type=F32, precision=lax.Precision.HIGHEST)
        dvn = dvn_ref[...]
        dvh = dvn * gv_ref[...]
        dv = rv * (dvh - vhat * jnp.mean(dvh * vhat, axis=-1, keepdims=True))
        dzp_ref[:, d:] = (dv * gvv).astype(dzp_ref.dtype)
        _accumulate(dgv_ref, jnp.sum(dvn * vhat, axis=0, keepdims=True), first)

    return _pcall(
        body, name=name,
        out_shape=[_sds((s, d2), BF16), _sds((GROUPS, CHUNK, CHUNK), F32), _sds((GROUPS, 8, CHUNK), F32),
                   _sds((1, d), F32)],
        grid=(s // CHUNK,),
        in_specs=[pl.BlockSpec((CHUNK, d2), lambda i: (i, 0)), pl.BlockSpec((CHUNK, d), lambda i: (i, 0)),
                  pl.BlockSpec((1, d), lambda i: (0, 0)), pl.BlockSpec((GROUPS, CHUNK, CHUNK), lambda i: (0, 0, 0)),
                  pl.BlockSpec((CHUNK, GROUPS), lambda i: (0, 0))],
        out_specs=[pl.BlockSpec((CHUNK, d2), lambda i: (i, 0)),
                   pl.BlockSpec((GROUPS, CHUNK, CHUNK), lambda i: (0, 0, 0)),
                   pl.BlockSpec((GROUPS, 8, CHUNK), lambda i: (0, 0, 0)), pl.BlockSpec((1, d), lambda i: (0, 0))],
        scratch=[pltpu.VMEM((CHUNK, d), F32)],
    )(zp, dgated, gv, ws, bst)


def _bucket_table():
    dist = np.arange(BLOCK)[:, None] + BLOCK - np.arange(2 * BLOCK)[None, :]
    in_window = (dist >= 0) & (dist < BLOCK)
    dd = np.clip(dist, 0, None)
    max_exact = N_BUCKETS // 2
    dl = np.maximum(dd, 1).astype(np.float32)
    large = max_exact + (np.log(dl / np.float32(max_exact)) / np.float32(math.log(MAX_DISTANCE / max_exact))
                         * np.float32(N_BUCKETS - max_exact)).astype(np.int32)
    large = np.minimum(large, N_BUCKETS - 1)
    bucket = np.where(dd < max_exact, dd, large)
    return np.where(in_window, bucket, -1).astype(np.int32).reshape(1, -1)


def _bias_table(name, rel_bias_t, buckets):
    nh = rel_bias_t.shape[0]
    p = buckets.shape[1]
    tp = 4096

    def body(rb_ref, bk_ref, o_ref):
        bk = bk_ref[...]
        onehot = (lax.broadcasted_iota(jnp.int32, (N_BUCKETS, tp), 0) == bk).astype(F32)
        val = jnp.dot(rb_ref[...], onehot, preferred_element_type=F32, precision=lax.Precision.HIGHEST)
        o_ref[...] = jnp.where(bk >= 0, val, NEG_INF)

    return _pcall(
        body, name=name, out_shape=_sds((nh, p), F32), grid=(p // tp,),
        in_specs=[pl.BlockSpec((nh, N_BUCKETS), lambda i: (0, 0)), pl.BlockSpec((1, tp), lambda i: (0, i))],
        out_specs=pl.BlockSpec((nh, tp), lambda i: (0, i)),
    )(rel_bias_t, buckets)


def _bias_grad(name, dbiases, buckets):
    nh, p = dbiases[0].shape
    n = len(dbiases)
    tp = 4096

    def body(*refs):
        bk_ref, o_ref = refs[n], refs[n + 1]
        onehot = (lax.broadcasted_iota(jnp.int32, (N_BUCKETS, tp), 0) == bk_ref[...]).astype(F32)
        db = refs[0][...]
        for r in refs[1:n]:
            db = db + r[...]
        part = lax.dot_general(onehot, db, NT, preferred_element_type=F32, precision=lax.Precision.HIGHEST)
        _accumulate(o_ref, part, pl.program_id(0) == 0)

    return _pcall(
        body, name=name, out_shape=_sds((N_BUCKETS, nh), F32), grid=(p // tp,),
        in_specs=[pl.BlockSpec((nh, tp), lambda i: (0, i))] * n + [pl.BlockSpec((1, tp), lambda i: (0, i))],
        out_specs=pl.BlockSpec((N_BUCKETS, nh), lambda i: (0, 0)),
    )(*dbiases, buckets)


def _stack_heads(ref):
    return jnp.concatenate([ref[:, hh * HEAD_DIM:(hh + 1) * HEAD_DIM] for hh in range(KV_GROUP)], axis=0)


def _attn_probs(q, kb, b_ref, s_ref):
    kh, i = pl.program_id(0), pl.program_id(1)
    penalty = jnp.where(i > 0, 0.0, NEG_INF).astype(F32)
    col = lax.broadcasted_iota(jnp.int32, (1, 2 * BLOCK), 1)
    bias = b_ref[...].reshape(KV_GROUP * BLOCK, 2 * BLOCK) + jnp.where(col < BLOCK, penalty, 0.0)
    sink = jnp.concatenate([jnp.full((BLOCK, 1), s_ref[kh * KV_GROUP + hh], F32) for hh in range(KV_GROUP)], axis=0)
    s = lax.dot_general(q, kb, NT, preferred_element_type=F32) * 0.125 + bias
    m = jnp.maximum(jnp.max(s, axis=-1, keepdims=True), sink)
    p = jnp.exp(s - m)
    es = jnp.exp(sink - m)
    inv = 1.0 / (jnp.sum(p, axis=-1, keepdims=True) + es)
    return p * inv, es * inv


def _attn_specs(nkv):
    gq = KV_GROUP * HEAD_DIM
    q_spec = pl.BlockSpec((BLOCK, gq), lambda kh, i: (i, kh))
    prev = pl.BlockSpec((None, BLOCK, HEAD_DIM), lambda kh, i: (kh, jnp.maximum(i - 1, 0), 0))
    cur = pl.BlockSpec((None, BLOCK, HEAD_DIM), lambda kh, i: (kh, i, 0))
    bias = pl.BlockSpec((KV_GROUP, BLOCK, 2 * BLOCK), lambda kh, i: (kh, 0, 0))
    smem = pl.BlockSpec(memory_space=pltpu.SMEM)
    return q_spec, prev, cur, bias, smem


def _attn_fwd(name, q, k, v, bias, sinks):
    s, dq = q.shape
    nkv = k.shape[0]
    q_spec, prev, cur, bias_spec, smem = _attn_specs(nkv)

    def body(q_ref, kp_ref, kc_ref, vp_ref, vc_ref, b_ref, s_ref, o_ref):
        kb = jnp.concatenate([kp_ref[...], kc_ref[...]], axis=0)
        vb = jnp.concatenate([vp_ref[...], vc_ref[...]], axis=0)
        p, _ = _attn_probs(_stack_heads(q_ref), kb, b_ref, s_ref)
        o = jnp.dot(p.astype(BF16), vb, preferred_element_type=F32)
        for hh in range(KV_GROUP):
            o_ref[:, hh * HEAD_DIM:(hh + 1) * HEAD_DIM] = o[hh * BLOCK:(hh + 1) * BLOCK].astype(o_ref.dtype)

    return _pcall(
        body, name=name, out_shape=_sds((s, dq), BF16), grid=(nkv, s // BLOCK),
        in_specs=[q_spec, prev, cur, prev, cur, bias_spec, smem], out_specs=q_spec,
    )(q, k, k, v, v, bias, sinks)


def _attn_bwd(name, q, k, v, do, bias, sinks):
    s, dq = q.shape
    nkv = k.shape[0]
    gq = KV_GROUP * HEAD_DIM
    q_spec, prev, cur, bias_spec, smem = _attn_specs(nkv)

    def body(q_ref, do_ref, kp_ref, kc_ref, vp_ref, vc_ref, b_ref, s_ref,
             dq_ref, dbq_ref, dkc_ref, dkp_ref, dvc_ref, dvp_ref, dbias_ref, dsink_ref):
        @pl.when(pl.program_id(1) == 0)
        def _():
            dbias_ref[...] = jnp.zeros_like(dbias_ref)
            dsink_ref[...] = jnp.zeros_like(dsink_ref)
            dbq_ref[...] = jnp.zeros_like(dbq_ref)

        kb = jnp.concatenate([kp_ref[...], kc_ref[...]], axis=0)
        vb = jnp.concatenate([vp_ref[...], vc_ref[...]], axis=0)
        q, do = _stack_heads(q_ref), _stack_heads(do_ref)
        p, ps = _attn_probs(q, kb, b_ref, s_ref)
        dp = lax.dot_general(do, vb, NT, preferred_element_type=F32)
        delta = jnp.sum(p * dp, axis=-1, keepdims=True)
        ds = p * (dp - delta)
        dsb = ds.astype(BF16)
        dq = jnp.dot(dsb, kb, preferred_element_type=F32) * 0.125
        dsk = -(ps * delta)
        for hh in range(KV_GROUP):
            sl, rows = slice(hh * HEAD_DIM, (hh + 1) * HEAD_DIM), slice(hh * BLOCK, (hh + 1) * BLOCK)
            dq_ref[:, sl] = dq[rows].astype(dq_ref.dtype)
            dbq_ref[:, sl] += jnp.sum(dq[rows], axis=0, keepdims=True)
            dsink_ref[:, hh:hh + 1] += jnp.sum(dsk[rows], axis=0, keepdims=True)
        dkb = lax.dot_general(dsb, q, TN, preferred_element_type=F32) * 0.125
        dvb = lax.dot_general(p.astype(BF16), do, TN, preferred_element_type=F32)
        dkp_ref[...], dkc_ref[...] = dkb[:BLOCK], dkb[BLOCK:]
        dvp_ref[...], dvc_ref[...] = dvb[:BLOCK], dvb[BLOCK:]
        dbias_ref[...] += ds.reshape(KV_GROUP, BLOCK, 2 * BLOCK)

    kv_out = _sds((nkv, s, HEAD_DIM), F32)
    return _pcall(
        body, name=name,
        out_shape=[_sds((s, dq), BF16), _sds((1, dq), F32), kv_out, kv_out, kv_out, kv_out,
                   _sds((nkv * KV_GROUP, BLOCK, 2 * BLOCK), F32), _sds((nkv, 1, KV_GROUP), F32)],
        grid=(nkv, s // BLOCK),
        in_specs=[q_spec, q_spec, prev, cur, prev, cur, bias_spec, smem],
        out_specs=[q_spec, pl.BlockSpec((1, gq), lambda kh, i: (0, kh)), cur, cur, cur, cur, bias_spec,
                   pl.BlockSpec((None, 1, KV_GROUP), lambda kh, i: (kh, 0, 0))],
    )(q, do, k, k, v, v, bias, sinks)


def _kv_grad(name, parts):
    nkv, s, _ = parts[0][0].shape
    nb = s // BLOCK
    w = 2 * nkv * HEAD_DIM
    n = len(parts)

    def body(*refs):
        o_ref, cs_ref = refs[4 * n], refs[4 * n + 1]
        i = pl.program_id(0)
        keep = jnp.where(i < nb - 1, 1.0, 0.0).astype(F32)

        @pl.when(i == 0)
        def _():
            cs_ref[...] = jnp.zeros_like(cs_ref)

        for which in range(2):
            for hh in range(nkv):
                val = None
                for l in range(n):
                    cur_ref, nxt_ref = refs[4 * l + 2 * which], refs[4 * l + 2 * which + 1]
                    t = cur_ref[hh] + keep * nxt_ref[hh]
                    val = t if val is None else val + t
                sl = slice((which * nkv + hh) * HEAD_DIM, (which * nkv + hh + 1) * HEAD_DIM)
                o_ref[:, sl] = val.astype(o_ref.dtype)
                cs_ref[:, sl] += jnp.sum(val, axis=0, keepdims=True)

    cur = pl.BlockSpec((nkv, BLOCK, HEAD_DIM), lambda i: (0, i, 0))
    nxt = pl.BlockSpec((nkv, BLOCK, HEAD_DIM), lambda i: (0, jnp.minimum(i + 1, nb - 1), 0))
    flat = [a for p in parts for a in p]
    return _pcall(
        body, name=name, out_shape=[_sds((s, w), BF16), _sds((1, w), F32)], grid=(nb,),
        in_specs=[cur, nxt] * (2 * n),
        out_specs=[pl.BlockSpec((BLOCK, w), lambda i: (i, 0)), pl.BlockSpec((1, w), lambda i: (0, 0))],
    )(*flat)


def _adamw_math(w, g, m, v):
    m = ADAM_B1 * m + (1.0 - ADAM_B1) * g
    v = ADAM_B2 * v + (1.0 - ADAM_B2) * (g * g)
    m_hat = m / (1.0 - ADAM_B1 ** ADAM_STEP)
    v_hat = v / (1.0 - ADAM_B2 ** ADAM_STEP)
    delta = -ADAM_LR * (m_hat / (jnp.sqrt(v_hat) + ADAM_EPS) + ADAM_WD * w)
    return delta, m, v


def _adamw_shard(name, w, m, v, parts, row0, layer, prev, deps=()):
    _, r, wd = w.shape
    tr = _row_tile(r, wd, budget=3 * 512 * 1024)
    assert row0 % tr == 0

    def body(w_ref, m_ref, v_ref, p_ref, a0, a1, a2, a3, g_ref, d_ref, nm_ref, nv_ref):
        g = p_ref[0].astype(F32)
        for k in range(1, NCHIP):
            g = g + p_ref[k].astype(F32)
        delta, nm, nv = _adamw_math(w_ref[...], g, m_ref[...], v_ref[...])
        g_ref[...], d_ref[...], nm_ref[...], nv_ref[...] = g, delta, nm, nv

    par = pl.BlockSpec((None, tr, wd), lambda i: (layer, i, 0))
    out = _sds(w.shape, F32)
    return _pcall(
        body, name=name, out_shape=[out, out, out, out], grid=(r // tr,),
        in_specs=[par, par, par, pl.BlockSpec((NCHIP, tr, wd), lambda i: (0, row0 // tr + i, 0)), ANY, ANY, ANY, ANY],
        out_specs=[par, par, par, par], aliases={4: 0, 5: 1, 6: 2, 7: 3}, deps=deps,
    )(w, m, v, parts, *prev)


def _sum_devices(name, gathered):
    _, r, wd = gathered.shape

    def body(g_ref, o_ref):
        acc = g_ref[0]
        for k in range(1, NDEV):
            acc = acc + g_ref[k]
        o_ref[...] = acc

    return _pcall(body, name=name, out_shape=_sds((r, wd), F32), grid=(1,),
                  in_specs=[pl.BlockSpec((NDEV, r, wd), lambda i: (0, 0, 0))],
                  out_specs=pl.BlockSpec((r, wd), lambda i: (0, 0)))(gathered)


def _adamw_flat(name, w, g, m, v):
    shape = w.shape

    def body(w_ref, g_ref, m_ref, v_ref, d_ref, nm_ref, nv_ref):
        d_ref[...], nm_ref[...], nv_ref[...] = _adamw_math(w_ref[...], g_ref[...], m_ref[...], v_ref[...])

    spec = pl.BlockSpec(shape, lambda i: (0, 0))
    out = _sds(shape, F32)
    return _pcall(body, name=name, out_shape=[out, out, out], grid=(1,), in_specs=[spec] * 4,
                  out_specs=[spec] * 3)(w, g, m, v)


def _cast_into(name, src, layer, buf, row0, me):
    _, r, wd = src.shape
    tr = _row_tile(r, wd)
    assert row0 % tr == 0

    def body(me_ref, s_ref, b_ref, o_ref):
        o_ref[...] = s_ref[...].astype(o_ref.dtype)

    return _pcall(
        body, name=name, out_shape=_sds(buf.shape, buf.dtype), grid=(r // tr,), prefetch=1,
        in_specs=[pl.BlockSpec((None, tr, wd), lambda i, mr: (layer, i, 0)), ANY],
        out_specs=pl.BlockSpec((None, tr, wd), lambda i, mr: (mr[0], row0 // tr + i, 0)), aliases={2: 0},
    )(me, src, buf)


def _pack(arrays):
    rows = []
    for a in arrays:
        flat = a.reshape(-1).astype(F32)
        pad = (-flat.shape[0]) % 1024
        rows.append(jnp.pad(flat, (0, pad)).reshape(-1, 128))
    return jnp.concatenate(rows, axis=0)


def _unpack(packed, shapes):
    out, r = [], 0
    for shp in shapes:
        n = int(np.prod(shp))
        nr = (n + 1023) // 1024 * 8
        out.append(packed[r:r + nr].reshape(-1)[:n].reshape(shp))
        r += nr
    return out


def kernel(x, mix_norm, ffn_norm, a_w_in, a_norm_v, a_w_s, a_b_s, a_w_out, kv_norm, w_kv, b_kv, b_w_q, b_b_q, b_sinks, b_w_o, b_b_o, rel_bias, ffn_w_gate, ffn_w_up, ffn_w_down, final_norm, loss_target, m_mix_norm, m_ffn_norm, m_a_w_in, m_a_norm_v, m_a_w_s, m_a_b_s, m_a_w_out, m_kv_norm, m_w_kv, m_b_kv, m_b_w_q, m_b_b_q, m_b_sinks, m_b_w_o, m_b_b_o, m_rel_bias, m_ffn_w_gate, m_ffn_w_up, m_ffn_w_down, m_final_norm, v_mix_norm, v_ffn_norm, v_a_w_in, v_a_norm_v, v_a_w_s, v_a_b_s, v_a_w_out, v_kv_norm, v_w_kv, v_b_kv, v_b_w_q, v_b_b_q, v_b_sinks, v_b_w_o, v_b_b_o, v_rel_bias, v_ffn_w_gate, v_ffn_w_up, v_ffn_w_down, v_final_norm):
    _, S, D = x.shape
    LA, LB, L = a_w_in.shape[0], b_w_q.shape[0], ffn_w_gate.shape[0]
    F = ffn_w_gate.shape[2]
    DS = D // NDEV
    ZC = a_w_in.shape[2]
    KVW = w_kv.shape[1]
    NKV = KVW // (2 * HEAD_DIM)
    NH = D // HEAD_DIM
    assert ZC * NDEV == 2 * D and NH == NKV * KV_GROUP and S % BLOCK == 0
    TM = min(1024, S)
    TN_ = min(1024, D)
    TS = min(512, D)

    ix, iy, ic = lax.axis_index("x"), lax.axis_index("y"), lax.axis_index("c")
    me = (4 * ix + 2 * iy + ic).astype(jnp.int32)
    me1 = me.reshape(1)
    where = jnp.stack([ic, 2 * ix + iy]).astype(jnp.int32)

    def tr3(a):
        return jnp.transpose(a, (0, 2, 1))

    gate_t, up_t = tr3(ffn_w_gate), tr3(ffn_w_up)
    w_kv3 = w_kv.reshape((1,) + w_kv.shape)

    def layer_arrays(l):
        arrs = [("gu", 2 * F, D, [(gate_t, l, 0), (up_t, l, F)]), ("down", F, D, [(ffn_w_down, l, 0)])]
        if l < LA:
            arrs += [("win", D, ZC, [(a_w_in, l, 0)]), ("wout", DS, D, [(a_w_out, l, 0)])]
            if l == LA - 1:
                arrs.append(("wkv", DS, KVW, [(w_kv3, 0, 0)]))
        else:
            i_b = l - LA
            arrs.append(("wqo", 2 * DS, D, [(b_w_q, i_b, 0), (b_w_o, i_b, DS)]))
        return arrs

    started = []
    token = None
    for l in range(L):
        keys, bufs = [], []
        for key, rows, width, sources in layer_arrays(l):
            buf = lax.empty((NDEV, rows, width), BF16)
            for si, (src, li, row0) in enumerate(sources):
                buf = _cast_into(f"cast_{key}{l}_{si}", src, li, buf, row0, me1)
            keys.append(key)
            bufs.append(buf)
        if l == 0:
            nv_rows = _pack([a_norm_v])
            nv = _cast_into("put_norm_v", nv_rows.reshape((1,) + nv_rows.shape), 0,
                            lax.empty((NDEV,) + nv_rows.shape, F32), 0, me1)
            keys.append("norm_v")
            bufs.append(nv)
        send, recv, bufs, token = _gather_start(f"gather_start{l}", bufs, [] if token is None else [token])
        started.append((keys, bufs, send, recv))

    def finish_gather(l, deps):
        keys, bufs, send, recv = started[l]
        fsend, frecv, bufs = _gather_forward(f"gather_forward{l}", bufs, send, recv, deps)
        bufs = _gather_finish(f"gather_finish{l}", bufs, send, recv, fsend, frecv)
        return dict(zip(keys, bufs))

    buckets = jnp.asarray(_bucket_table())
    bias = _bias_table("bias_table", rel_bias.T, buckets).reshape(NH, BLOCK, 2 * BLOCK)

    def rows_full(tm):
        return pl.BlockSpec((tm, D), lambda i, j: (i, 0))

    def tile(tm, tn):
        return pl.BlockSpec((tm, tn), lambda i, j: (i, j))

    vec_tile = pl.BlockSpec((1, TN_), lambda i, j: (0, j))

    def ffn_forward(l, wl, h_mid, tag):
        xf = _rms_fwd(f"ffn_norm_fwd{tag}", h_mid, ffn_norm[l])

        def ep(parts, ex, outs):
            a, b = parts
            outs[0][0] = a.astype(BF16)
            outs[0][1] = b.astype(BF16)
            outs[1][...] = (a * jax.nn.sigmoid(a) * b).astype(BF16)

        ab, hid = _gemm(
            f"ffn_up{tag}", (S // TM, NDEV),
            [(xf, rows_full(TM)),
             (wl["gu"], pl.BlockSpec((None, F, D), lambda i, e: (e, 0, 0))),
             (wl["gu"], pl.BlockSpec((None, F, D), lambda i, e: (e, 1, 0)))],
            [(0, 1, NT), (0, 2, NT)], [],
            [(_sds((2, NDEV, S, F), BF16), pl.BlockSpec((2, None, TM, F), lambda i, e: (0, e, i, 0))),
             (_sds((NDEV, S, F), BF16), pl.BlockSpec((None, TM, F), lambda i, e: (e, i, 0)))],
            ep, separate=True)
        (h_out,) = _gemm(
            f"ffn_down{tag}", (S // TM, D // TN_, NDEV),
            [(hid, pl.BlockSpec((None, TM, F), lambda i, j, e: (e, i, 0))),
             (wl["down"], pl.BlockSpec((None, F, TN_), lambda i, j, e: (e, 0, j)))],
            [(0, 1, NN)], [(h_mid, pl.BlockSpec((TM, TN_), lambda i, j, e: (i, j)))],
            [(_sds((S, D), F32), pl.BlockSpec((TM, TN_), lambda i, j, e: (i, j)))],
            _store_add_extra, nk=NDEV, acc_shape=(TM, TN_))
        return dict(h_mid=h_mid, xf=xf, ab=ab, hid=hid), h_out

    def stacked_rows_gemm(name, a, wmat, blk, extras, ep, out_dtype):
        return _gemm(
            name, (S // TM, D // TN_),
            [(a, rows_full(TM)), (wmat, pl.BlockSpec((NDEV, DS, TN_), lambda i, j: (0, blk, j)))],
            [(0, 1, NN, None, _stacked)], extras,
            [(_sds((S, D), out_dtype), tile(TM, TN_))], ep)[0]

    def back_rows_gemm(name, a, wmat, blk, out_dtype, deps=()):
        return _gemm(
            name, (S // TM, NDEV),
            [(a, rows_full(TM)), (wmat, pl.BlockSpec((None, DS, D), lambda i, e: (e, blk, 0)))],
            [(0, 1, NT)], [], [(_sds((S, D), out_dtype), pl.BlockSpec((TM, DS), lambda i, e: (i, e)))], _store,
            deps=deps)[0]

    def grad_rows_gemm(name, act, d_bf, buf, blk):
        return _gemm(
            name, (NDEV,),
            [(act, pl.BlockSpec((S, DS), lambda e: (0, e))), (d_bf, pl.BlockSpec((S, D), lambda e: (0, 0)))],
            [(0, 1, TN)], [(buf, ANY)],
            [(_sds(buf.shape, BF16), pl.BlockSpec((None, DS, D), lambda e: (e, blk, 0)))],
            _store, aliases={2: 0})[0]

    saved, weights = [], []
    h = x.reshape(S, D)
    k_heads = v_heads = hn = h_kv = norm_v = None
    for layer in range(L):
        wl = finish_gather(layer, [token] if layer == 0 else [h])
        weights.append(wl)
        if layer == 0:
            nv_all = wl["norm_v"]
            norm_v = jnp.transpose(nv_all.reshape(NDEV, -1)[:, :LA * DS].reshape(NDEV, LA, DS), (1, 0, 2)).reshape(LA, D)
        sv = dict(h_in=h)
        xn = _rms_fwd(f"mix_norm_fwd{layer}", h, mix_norm[layer])
        sv["xn"] = xn
        if layer < LA:
            i_a = layer
            (zp,) = _gemm(
                f"gmlp_in{layer}", (S // TM, NDEV),
                [(xn, rows_full(TM)), (wl["win"], pl.BlockSpec((None, D, ZC), lambda i, e: (e, 0, 0)))],
                [(0, 1, NN)], [], [(_sds((S, 2 * D), F32), pl.BlockSpec((TM, ZC), lambda i, e: (i, e)))], _store)
            bst = a_b_s[i_a].T
            gated = _gmlp_fwd(f"gmlp_gate{layer}", zp, norm_v[i_a].reshape(1, D), a_w_s[i_a], bst)
            sv.update(zp=zp, gated=gated, bst=bst)
            h_mid = stacked_rows_gemm(f"gmlp_out{layer}", gated, wl["wout"], 0, [(h, tile(TM, TN_))],
                                      _store_add_extra, F32)
        else:
            i_b = layer - LA
            q = stacked_rows_gemm(f"attn_q{layer}", xn, wl["wqo"], 0, [(b_b_q[i_b].reshape(1, D), vec_tile)],
                                  _store_add_extra, BF16)
            attn = _attn_fwd(f"attn_fwd{layer}", q, k_heads, v_heads, bias, b_sinks[i_b])
            sv.update(q=q, attn=attn)
            h_mid = stacked_rows_gemm(f"attn_o{layer}", attn, wl["wqo"], 1,
                                      [(h, tile(TM, TN_)), (b_b_o[i_b].reshape(1, D), vec_tile)],
                                      _store_add_extra, F32)
        fsv, h = ffn_forward(layer, wl, h_mid, str(layer))
        sv.update(fsv)
        saved.append(sv)
        if layer == LA - 1:
            h_kv = h
            hn = _rms_fwd("kv_norm_fwd", h, kv_norm)

            def kv_ep(acc, ex, outs):
                val = acc + ex[0][...]
                for hh in range(NKV):
                    outs[0][hh] = val[:, hh * HEAD_DIM:(hh + 1) * HEAD_DIM].astype(BF16)
                    outs[1][hh] = val[:, (NKV + hh) * HEAD_DIM:(NKV + hh + 1) * HEAD_DIM].astype(BF16)

            k_heads, v_heads = _gemm(
                "kv_proj", (S // TM,),
                [(hn, pl.BlockSpec((TM, D), lambda i: (i, 0))),
                 (wl["wkv"], pl.BlockSpec((NDEV, DS, KVW), lambda i: (0, 0, 0)))],
                [(0, 1, NN, None, _stacked)], [(b_kv.reshape(1, KVW), pl.BlockSpec((1, KVW), lambda i: (0, 0)))],
                [(_sds((NKV, S, HEAD_DIM), BF16), pl.BlockSpec((NKV, TM, HEAD_DIM), lambda i: (0, i, 0)))] * 2,
                kv_ep)

    loss11, d, d_bf, g_final = _loss_bwd("loss_bwd", h, final_norm, loss_target.reshape(S, D))
    loss = lax.psum(loss11[0, 0], AXES)

    g_mix, g_ffn = [None] * L, [None] * L
    g_ws, g_bs, g_nv = [None] * LA, [None] * LA, [None] * LA
    g_bq, g_sink, g_bo = [None] * LB, [None] * LB, [None] * LB
    dbiases = []
    kv_parts = []
    g_kvn = g_bkv = None
    exchanges = [[] for _ in range(L)]
    pending = None
    grads_wkv = None
    newest = []

    def new_grads(l):
        return {key: lax.empty((NDEV, rows, width), BF16) for key, rows, width, _ in layer_arrays(l)}

    def exchange_begin(tag, l, gl, keys):
        grads = [gl[k] for k in keys]
        lands = [lax.empty((NCHIP,) + g.shape[1:], BF16) for g in grads]
        send, recv, grads, lands, tok = _sibling_start(f"rs_sibling_start{tag}", grads, lands, [])
        newest[:] = [tok]
        return dict(tag=tag, layer=l, keys=keys, grads=grads, lands=lands, send=send, recv=recv)

    def exchange_middle(st, dep):
        tag = st["tag"]
        grads, lands = _sibling_finish(f"rs_sibling_finish{tag}", st["grads"], st["lands"], st["send"], st["recv"], [dep])
        sums, own = [], []
        for t, key in enumerate(st["keys"]):
            s_, o_ = _pair_sum(f"pair_sum_{key}{tag}", grads[t], lands[t], where)
            sums.append(s_)
            own.append(o_)
        send, recv, sums, own, tok = _chips_start(f"rs_chips_start{tag}", sums, own, [])
        newest[:] = [tok]
        st.update(sums=sums, own=own, send2=send, recv2=recv)
        exchanges[st["layer"]].append(st)

    def exchange_end(st, dep):
        lands = _chips_finish(f"rs_chips_finish{st['tag']}", st["sums"], st["own"], st["send2"], st["recv2"], [dep])
        return dict(zip(st["keys"], lands))

    for layer in reversed(range(L)):
        sv, wl = saved[layer], weights[layer]
        tag = str(layer)
        gl = new_grads(layer)
        if grads_wkv is not None and layer == LA - 1:
            gl["wkv"] = grads_wkv
        def dhid_ep(acc, ex, outs):
            a, b = ex[0][0].astype(F32), ex[0][1].astype(F32)
            sg = jax.nn.sigmoid(a)
            outs[0][0] = (acc * b * (sg * (1.0 + a * (1.0 - sg)))).astype(BF16)
            outs[0][1] = (acc * (a * sg)).astype(BF16)

        ab_spec = pl.BlockSpec((2, None, TM, F), lambda i, e: (0, e, i, 0))
        (dab,) = _gemm(
            f"ffn_dhid{tag}", (S // TM, NDEV),
            [(d_bf, rows_full(TM)), (wl["down"], pl.BlockSpec((None, F, D), lambda i, e: (e, 0, 0)))],
            [(0, 1, NT)], [(sv["ab"], ab_spec)], [(_sds((2, NDEV, S, F), BF16), ab_spec)], dhid_ep,
            deps=list(newest))
        if pending:
            exchange_middle(pending, dab)
        (gl["down"],) = _gemm(
            f"ffn_dwdown{tag}", (NDEV,),
            [(sv["hid"], pl.BlockSpec((None, S, F), lambda e: (e, 0, 0))),
             (d_bf, pl.BlockSpec((S, D), lambda e: (0, 0)))],
            [(0, 1, TN)], [(gl["down"], ANY)],
            [(_sds(gl["down"].shape, BF16), pl.BlockSpec((None, F, D), lambda e: (e, 0, 0)))],
            _store, aliases={2: 0}, deps=list(newest))
        (gl["gu"],) = _gemm(
            f"ffn_dwup{tag}", (2, NDEV),
            [(dab, pl.BlockSpec((None, None, S, F), lambda w, e: (w, e, 0, 0))),
             (sv["xf"], pl.BlockSpec((S, D), lambda w, e: (0, 0)))],
            [(0, 1, TN)], [(gl["gu"], ANY)],
            [(_sds(gl["gu"].shape, BF16), pl.BlockSpec((None, F, D), lambda w, e: (e, w, 0)))],
            _store, aliases={2: 0})
        ffn_group = exchange_begin(f"_ffn{tag}", layer, gl, ["gu", "down"])
        (dxf,) = _gemm(
            f"ffn_dx{tag}", (S // TM, D // TN_, 2 * NDEV),
            [(dab, pl.BlockSpec((None, None, TM, F), lambda i, j, k: (k // NDEV, k % NDEV, i, 0))),
             (wl["gu"], pl.BlockSpec((None, F, TN_), lambda i, j, k: (k % NDEV, k // NDEV, j)))],
            [(0, 1, NN)], [], [(_sds((S, D), F32), pl.BlockSpec((TM, TN_), lambda i, j, k: (i, j)))],
            _store, nk=2 * NDEV, acc_shape=(TM, TN_), deps=list(newest))
        exchange_middle(ffn_group, dxf)
        d, d_bf, g_ffn[layer], colsum = _rms_bwd(f"ffn_norm_bwd{tag}", sv["h_mid"], ffn_norm[layer], dxf, d,
                                                 deps=list(newest))
        if layer < LA:
            i_a = layer
            dgated = back_rows_gemm(f"gmlp_dgated{tag}", d_bf, wl["wout"], 0, F32)
            gl["wout"] = grad_rows_gemm(f"gmlp_dwout{tag}", sv["gated"], d_bf, gl["wout"], 0)
            dzp, g_ws[i_a], dbs, g_nv[i_a] = _gmlp_bwd(f"gmlp_bwd{tag}", sv["zp"], dgated,
                                                       norm_v[i_a].reshape(1, D), a_w_s[i_a], sv["bst"])
            g_bs[i_a] = dbs[:, 0, :]
            (gl["win"],) = _gemm(
                f"gmlp_dwin{tag}", (NDEV, D // TS),
                [(sv["xn"], pl.BlockSpec((S, TS), lambda e, i: (0, i))),
                 (dzp, pl.BlockSpec((S, ZC), lambda e, i: (0, e)))],
                [(0, 1, TN)], [(gl["win"], ANY)],
                [(_sds(gl["win"].shape, BF16), pl.BlockSpec((None, TS, ZC), lambda e, i: (e, i, 0)))],
                _store, aliases={2: 0})
            (dxn,) = _gemm(
                f"gmlp_dx{tag}", (S // TM, D // TN_, NDEV),
                [(dzp, pl.BlockSpec((TM, ZC), lambda i, j, e: (i, e))),
                 (wl["win"], pl.BlockSpec((None, TN_, ZC), lambda i, j, e: (e, j, 0)))],
                [(0, 1, NT)], [], [(_sds((S, D), F32), pl.BlockSpec((TM, TN_), lambda i, j, e: (i, j)))],
                _store, nk=NDEV, acc_shape=(TM, TN_))
        else:
            i_b = layer - LA
            g_bo[i_b] = colsum
            dattn = back_rows_gemm(f"attn_dout{tag}", d_bf, wl["wqo"], 1, BF16)
            gl["wqo"] = grad_rows_gemm(f"attn_dwo{tag}", sv["attn"], d_bf, gl["wqo"], 1)
            dq, g_bq[i_b], dkc, dkp, dvc, dvp, dbias, dsink = _attn_bwd(
                f"attn_bwd{tag}", sv["q"], k_heads, v_heads, dattn, bias, b_sinks[i_b])
            kv_parts.append((dkc, dkp, dvc, dvp))
            g_sink[i_b] = dsink.reshape(NH)
            dbiases.append(dbias.reshape(NH, BLOCK * 2 * BLOCK))
            gl["wqo"] = grad_rows_gemm(f"attn_dwq{tag}", sv["xn"], dq, gl["wqo"], 0)
            dxn = back_rows_gemm(f"attn_dx{tag}", dq, wl["wqo"], 0, F32)
        d, d_bf, g_mix[layer], _ = _rms_bwd(f"mix_norm_bwd{tag}", sv["h_in"], mix_norm[layer], dxn, d)
        pending = exchange_begin(f"_mix{tag}", layer, gl, [k for k in gl if k not in ("gu", "down")])
        if layer == LA:
            wkv = weights[LA - 1]["wkv"]
            dkv, g_bkv = _kv_grad("kv_grad", kv_parts)
            (grads_wkv,) = _gemm(
                "kv_dw", (NDEV,),
                [(hn, pl.BlockSpec((S, DS), lambda e: (0, e))), (dkv, pl.BlockSpec((S, KVW), lambda e: (0, 0)))],
                [(0, 1, TN)], [(lax.empty((NDEV, DS, KVW), BF16), ANY)],
                [(_sds((NDEV, DS, KVW), BF16), pl.BlockSpec((None, DS, KVW), lambda e: (e, 0, 0)))],
                _store, aliases={2: 0}, deps=list(newest))
            (dhn,) = _gemm(
                "kv_dx", (S // TM, NDEV),
                [(dkv, pl.BlockSpec((TM, KVW), lambda i, e: (i, 0))),
                 (wkv, pl.BlockSpec((None, DS, KVW), lambda i, e: (e, 0, 0)))],
                [(0, 1, NT)], [], [(_sds((S, D), F32), pl.BlockSpec((TM, DS), lambda i, e: (i, e)))], _store)
            d, d_bf, g_kvn, _ = _rms_bwd("kv_norm_bwd", h_kv, kv_norm, dhn, d)
    grad_x = d.reshape(x.shape)

    exchange_middle(pending, d)

    results = {}
    after = list(newest)

    def upd(pname, w, m, v, l, li, lands, row0):
        w3 = w if w.ndim == 3 else w.reshape((1,) + w.shape)
        prev = results.get(pname) or [lax.empty(w3.shape, F32) for _ in range(4)]
        results[pname] = _adamw_shard(f"adamw_{pname}{l}", w3, m.reshape(w3.shape), v.reshape(w3.shape), lands,
                                      row0, li, prev, deps=list(after))
        after[:] = [results[pname][0]]

    for l in reversed(range(L)):
        for st in exchanges[l]:
            lands = exchange_end(st, after[0])
            if "gu" in lands:
                upd("ffn_w_gate", gate_t, tr3(m_ffn_w_gate), tr3(v_ffn_w_gate), l, l, lands["gu"], 0)
                upd("ffn_w_up", up_t, tr3(m_ffn_w_up), tr3(v_ffn_w_up), l, l, lands["gu"], F)
                upd("ffn_w_down", ffn_w_down, m_ffn_w_down, v_ffn_w_down, l, l, lands["down"], 0)
            if "win" in lands:
                upd("a_w_in", a_w_in, m_a_w_in, v_a_w_in, l, l, lands["win"], 0)
                upd("a_w_out", a_w_out, m_a_w_out, v_a_w_out, l, l, lands["wout"], 0)
            if "wkv" in lands:
                upd("w_kv", w_kv, m_w_kv, v_w_kv, l, 0, lands["wkv"], 0)
            if "wqo" in lands:
                upd("b_w_q", b_w_q, m_b_w_q, v_b_w_q, l, l - LA, lands["wqo"], 0)
                upd("b_w_o", b_w_o, m_b_w_o, v_b_w_o, l, l - LA, lands["wqo"], DS)
    for pname in ("ffn_w_gate", "ffn_w_up"):
        results[pname] = [tr3(r) for r in results[pname]]
    results["w_kv"] = [r.reshape(w_kv.shape) for r in results["w_kv"]]

    g_rel = _bias_grad("bias_grad", dbiases, buckets)
    small_local = [jnp.concatenate(g_mix, axis=0), jnp.concatenate(g_ffn, axis=0), jnp.stack(g_ws), jnp.stack(g_bs),
                   g_kvn, g_bkv, jnp.concatenate(g_bq, axis=0), jnp.stack(g_sink), jnp.concatenate(g_bo, axis=0),
                   g_rel, g_final, jnp.concatenate(g_nv, axis=0)]
    small_w = [mix_norm, ffn_norm, a_w_s, a_b_s, kv_norm, b_kv, b_b_q, b_sinks, b_b_o, rel_bias, final_norm]
    small_m = [m_mix_norm, m_ffn_norm, m_a_w_s, m_a_b_s, m_kv_norm, m_b_kv, m_b_b_q, m_b_sinks, m_b_b_o, m_rel_bias,
               m_final_norm]
    small_v = [v_mix_norm, v_ffn_norm, v_a_w_s, v_a_b_s, v_kv_norm, v_b_kv, v_b_b_q, v_b_sinks, v_b_b_o, v_rel_bias,
               v_final_norm]
    shapes = [w.shape for w in small_w] + [(LA, D)]
    (small_all,) = _all_gather("gather_small_grads", [_pack(small_local)])
    small_sum = _sum_devices("sum_small_grads", small_all)
    small_g = _unpack(small_sum, shapes)
    g_normv = lax.dynamic_slice_in_dim(small_g[-1], me * DS, DS, axis=1)
    small_g = small_g[:-1] + [g_normv]
    small_w, small_m, small_v = small_w + [a_norm_v], small_m + [m_a_norm_v], small_v + [v_a_norm_v]
    shapes = [w.shape for w in small_w]
    s_delta, s_m, s_v = _adamw_flat("adamw_small", _pack(small_w), _pack(small_g), _pack(small_m), _pack(small_v))
    s_delta, s_m, s_v = _unpack(s_delta, shapes), _unpack(s_m, shapes), _unpack(s_v, shapes)

    names = ["mix_norm", "ffn_norm", "a_w_in", "a_norm_v", "a_w_s", "a_b_s", "a_w_out", "kv_norm", "w_kv", "b_kv",
             "b_w_q", "b_b_q", "b_sinks", "b_w_o", "b_b_o", "rel_bias", "ffn_w_gate", "ffn_w_up", "ffn_w_down",
             "final_norm"]
    small_names = ["mix_norm", "ffn_norm", "a_w_s", "a_b_s", "kv_norm", "b_kv", "b_b_q", "b_sinks", "b_b_o", "rel_bias",
                   "final_norm", "a_norm_v"]
    res = {}
    for idx, nm in enumerate(small_names):
        res[nm] = (small_g[idx].reshape(shapes[idx]), s_delta[idx], s_m[idx], s_v[idx])
    for nm, u in results.items():
        res[nm] = tuple(u)
    out = [loss, grad_x]
    for part in range(4):
        out += [res[nm][part] for nm in names]
    return tuple(out)
```

```python
import math

import numpy as np
import jax
import jax.numpy as jnp
from jax import lax
from jax.experimental import pallas as pl
from jax.experimental.pallas import tpu as pltpu

F32 = jnp.float32
BF16 = jnp.bfloat16
AXES = ("x", "y", "c")
NDEV = 8
NCHIP = 4
CHUNK = 128
GROUPS = 8
HEAD_DIM = 64
KV_GROUP = 8
BLOCK = 128
N_BUCKETS = 32
MAX_DISTANCE = 128
RMS_EPS = 1e-5
NEG_INF = -1e30
ADAM_LR, ADAM_B1, ADAM_B2, ADAM_EPS, ADAM_WD, ADAM_STEP = 0.001, 0.9, 0.999, 1e-08, 0.01, 10
VMEM_LIMIT_BYTES = 56 * 1024 * 1024

NN = (((1,), (0,)), ((), ()))
NT = (((1,), (1,)), ((), ()))
TN = (((0,), (0,)), ((), ()))
ANY = pl.BlockSpec(memory_space=pl.ANY)
HBM = pl.BlockSpec(memory_space=pltpu.HBM)
SEM = pl.BlockSpec(memory_space=pltpu.SEMAPHORE)
MESH = pl.DeviceIdType.MESH
EFFECT = pltpu.SideEffectType.DATAFLOW_SIDE_EFFECTING


def _pcall(body, *, name, out_shape, in_specs, out_specs, grid=(), scratch=(), aliases=None, prefetch=0, deps=()):
    n_in, n_dep = len(in_specs), len(deps)
    if n_dep:
        inner = body

        def body(*refs):
            return inner(*refs[:prefetch + n_in], *refs[prefetch + n_in + n_dep:])

        in_specs = list(in_specs) + [ANY] * n_dep
    params = dict(vmem_limit_bytes=VMEM_LIMIT_BYTES)
    if grid:
        params["dimension_semantics"] = ("arbitrary",) * len(grid)
    kw = dict(name=name, out_shape=out_shape, compiler_params=pltpu.CompilerParams(**params),
              input_output_aliases=aliases or {})
    if prefetch:
        kw["grid_spec"] = pltpu.PrefetchScalarGridSpec(num_scalar_prefetch=prefetch, grid=grid, in_specs=in_specs,
                                                       out_specs=out_specs, scratch_shapes=list(scratch))
    else:
        kw.update(grid=grid, in_specs=in_specs, out_specs=out_specs, scratch_shapes=list(scratch))
    call = pl.pallas_call(body, **kw)
    return lambda *args: call(*args, *deps)


def _sds(shape, dtype):
    return jax.ShapeDtypeStruct(tuple(shape), dtype)


def _position():
    x, y, c = lax.axis_index("x"), lax.axis_index("y"), lax.axis_index("c")
    chips = [(1 - x, y), (x, 1 - y), (1 - x, 1 - y)]
    return x, y, c, chips


def _slot(px, py, pc):
    return 4 * px + 2 * py + pc


def _remote(ref_src, ref_dst, send, recv, to):
    return pltpu.make_async_remote_copy(src_ref=ref_src, dst_ref=ref_dst, send_sem=send, recv_sem=recv,
                                        device_id=to, device_id_type=MESH)


def _hbm(arrays):
    return [pltpu.with_memory_space_constraint(a, pltpu.HBM) for a in arrays]


def _split_call(body, name, out_shape, in_specs, out_specs, aliases):
    return pl.pallas_call(body, name=name, out_shape=out_shape, in_specs=in_specs, out_specs=out_specs,
                          input_output_aliases=aliases, compiler_params=pltpu.CompilerParams(has_side_effects=EFFECT))


def _token_shape():
    return _sds((8, 128), F32)


def _gather_start(name, bufs, deps):
    n, nd = len(bufs), len(deps)

    def body(*refs):
        ins, send, recv, token = refs[:n], refs[n + nd], refs[n + nd + 1], refs[2 * n + nd + 2]
        x, y, c, chips = _position()
        peers = [(x, y, 1 - c)] + [(*chip, c) for chip in chips]
        for t in range(n):
            mine = ins[t].at[_slot(x, y, c)]
            for k, peer in enumerate(peers):
                _remote(mine, mine, send.at[4 * t + k], recv.at[4 * t + k], peer).start()
        token[...] = jnp.zeros_like(token)

    res = _split_call(
        body, name,
        (pltpu.SemaphoreType.DMA((4 * n,)), pltpu.SemaphoreType.DMA((4 * n,)), *[pltpu.HBM(b.shape, b.dtype) for b in bufs],
         _token_shape()),
        [HBM] * n + [ANY] * nd, (SEM, SEM, *[HBM] * n, pl.BlockSpec(memory_space=pltpu.VMEM)),
        {t: 2 + t for t in range(n)})(*_hbm(bufs), *deps)
    return res[0], res[1], list(res[2:2 + n]), res[2 + n]


def _gather_forward(name, bufs, send, recv, deps):
    n, nd = len(bufs), len(deps)

    def body(*refs):
        ins, send_in, recv_in = refs[:n], refs[n], refs[n + 1]
        fsend, frecv = refs[n + 2 + nd], refs[n + 3 + nd]
        x, y, c, chips = _position()
        for j, chip in enumerate(chips):
            for t in range(n):
                blk = ins[t].at[_slot(*chip, c)]
                _remote(blk, blk, send_in.at[4 * t + 1 + j], recv_in.at[4 * t + 1 + j], (*chip, c)).wait_recv()
                _remote(blk, blk, fsend.at[3 * t + j], frecv.at[3 * t + j], (x, y, 1 - c)).start()

    res = _split_call(
        body, name,
        (pltpu.SemaphoreType.DMA((3 * n,)), pltpu.SemaphoreType.DMA((3 * n,)), *[pltpu.HBM(b.shape, b.dtype) for b in bufs]),
        [HBM] * n + [SEM, SEM] + [ANY] * nd, (SEM, SEM, *[HBM] * n),
        {t: 2 + t for t in range(n)})(*_hbm(bufs), send, recv, *deps)
    return res[0], res[1], list(res[2:])


def _gather_finish(name, bufs, send, recv, fsend, frecv):
    n = len(bufs)

    def body(*refs):
        ins, send_in, recv_in, fs_in, fr_in = refs[:n], refs[n], refs[n + 1], refs[n + 2], refs[n + 3]
        x, y, c, chips = _position()
        sibling = (x, y, 1 - c)
        peers = [sibling] + [(*chip, c) for chip in chips]
        for t in range(n):
            blk = ins[t].at[_slot(x, y, 1 - c)]
            _remote(blk, blk, send_in.at[4 * t], recv_in.at[4 * t], sibling).wait_recv()
            for j, chip in enumerate(chips):
                blk = ins[t].at[_slot(*chip, 1 - c)]
                _remote(blk, blk, fs_in.at[3 * t + j], fr_in.at[3 * t + j], sibling).wait_recv()
            mine = ins[t].at[_slot(x, y, c)]
            for k, peer in enumerate(peers):
                _remote(mine, mine, send_in.at[4 * t + k], recv_in.at[4 * t + k], peer).wait_send()
            for j, chip in enumerate(chips):
                blk = ins[t].at[_slot(*chip, c)]
                _remote(blk, blk, fs_in.at[3 * t + j], fr_in.at[3 * t + j], sibling).wait_send()

    res = _split_call(
        body, name, tuple(pltpu.HBM(b.shape, b.dtype) for b in bufs),
        [HBM] * n + [SEM] * 4, tuple([HBM] * n), {t: t for t in range(n)})(*_hbm(bufs), send, recv, fsend, frecv)
    return list(res)


def _sibling_start(name, grads, lands, deps):
    n, nd = len(grads), len(deps)

    def body(*refs):
        g_in, l_in = refs[:n], refs[n:2 * n]
        send, recv, token = refs[2 * n + nd], refs[2 * n + nd + 1], refs[4 * n + nd + 2]
        x, y, c, _ = _position()
        for t in range(n):
            for k in range(NCHIP):
                _remote(g_in[t].at[2 * k + (1 - c)], l_in[t].at[k], send.at[NCHIP * t + k], recv.at[NCHIP * t + k],
                        (x, y, 1 - c)).start()
        token[...] = jnp.zeros_like(token)

    both = list(grads) + list(lands)
    res = _split_call(
        body, name,
        (pltpu.SemaphoreType.DMA((NCHIP * n,)), pltpu.SemaphoreType.DMA((NCHIP * n,)),
         *[pltpu.HBM(b.shape, b.dtype) for b in both], _token_shape()),
        [HBM] * (2 * n) + [ANY] * nd, (SEM, SEM, *[HBM] * (2 * n), pl.BlockSpec(memory_space=pltpu.VMEM)),
        {t: 2 + t for t in range(2 * n)})(*_hbm(both), *deps)
    return res[0], res[1], list(res[2:2 + n]), list(res[2 + n:2 + 2 * n]), res[2 + 2 * n]


def _sibling_finish(name, grads, lands, send, recv, deps):
    n, nd = len(grads), len(deps)

    def body(*refs):
        g_in, l_in, send_in, recv_in = refs[:n], refs[n:2 * n], refs[2 * n], refs[2 * n + 1]
        x, y, c, _ = _position()
        for t in range(n):
            for k in range(NCHIP):
                cp = _remote(g_in[t].at[2 * k + (1 - c)], l_in[t].at[k], send_in.at[NCHIP * t + k],
                             recv_in.at[NCHIP * t + k], (x, y, 1 - c))
                cp.wait_send()
                cp.wait_recv()

    both = list(grads) + list(lands)
    res = _split_call(
        body, name, tuple(pltpu.HBM(b.shape, b.dtype) for b in both),
        [HBM] * (2 * n) + [SEM, SEM] + [ANY] * nd, tuple([HBM] * (2 * n)),
        {t: t for t in range(2 * n)})(*_hbm(both), send, recv, *deps)
    return list(res[:n]), list(res[n:])


def _chips_start(name, parts, lands, deps):
    n, nd = len(parts), len(deps)

    def body(*refs):
        p_in, l_in = refs[:n], refs[n:2 * n]
        send, recv, token = refs[2 * n + nd], refs[2 * n + nd + 1], refs[4 * n + nd + 2]
        x, y, c, chips = _position()
        for t in range(n):
            for j, chip in enumerate(chips):
                _remote(p_in[t].at[2 * chip[0] + chip[1]], l_in[t].at[2 * x + y], send.at[3 * t + j], recv.at[3 * t + j],
                        (*chip, c)).start()
        token[...] = jnp.zeros_like(token)

    both = list(parts) + list(lands)
    res = _split_call(
        body, name,
        (pltpu.SemaphoreType.DMA((3 * n,)), pltpu.SemaphoreType.DMA((3 * n,)), *[pltpu.HBM(b.shape, b.dtype) for b in both],
         _token_shape()),
        [HBM] * (2 * n) + [ANY] * nd, (SEM, SEM, *[HBM] * (2 * n), pl.BlockSpec(memory_space=pltpu.VMEM)),
        {t: 2 + t for t in range(2 * n)})(*_hbm(both), *deps)
    return res[0], res[1], list(res[2:2 + n]), list(res[2 + n:2 + 2 * n]), res[2 + 2 * n]


def _chips_finish(name, parts, lands, send, recv, deps):
    n, nd = len(parts), len(deps)

    def body(*refs):
        p_in, l_in, send_in, recv_in = refs[:n], refs[n:2 * n], refs[2 * n], refs[2 * n + 1]
        x, y, c, chips = _position()
        for t in range(n):
            for j, chip in enumerate(chips):
                k = 2 * chip[0] + chip[1]
                _remote(p_in[t].at[k], l_in[t].at[k], send_in.at[3 * t + j], recv_in.at[3 * t + j], (*chip, c)).wait_recv()
                _remote(p_in[t].at[k], l_in[t].at[2 * x + y], send_in.at[3 * t + j], recv_in.at[3 * t + j],
                        (*chip, c)).wait_send()

    both = list(parts) + list(lands)
    res = _split_call(
        body, name, tuple(pltpu.HBM(b.shape, b.dtype) for b in both),
        [HBM] * (2 * n) + [SEM, SEM] + [ANY] * nd, tuple([HBM] * (2 * n)),
        {t: t for t in range(2 * n)})(*_hbm(both), send, recv, *deps)
    return list(res[n:])


def _all_gather(name, shards):
    n = len(shards)

    def body(*refs):
        src, dst = refs[:n], refs[n:2 * n]
        send_sems, recv_sems, local_sems = refs[2 * n:]
        x, y, c, chips = _position()
        me, sibling = (x, y, c), (x, y, 1 - c)

        def copy(t, k, block, to, from_shard=False):
            slot = dst[t].at[_slot(*block)]
            return _remote(src[t] if from_shard else slot, slot, send_sems.at[t, k], recv_sems.at[t, k], to)

        mine = [pltpu.make_async_copy(src[t], dst[t].at[_slot(x, y, c)], local_sems.at[t]) for t in range(n)]
        first, passed = [], []
        for t in range(n):
            mine[t].start()
            first.append(copy(t, 0, me, sibling, True))
            first += [copy(t, 1 + j, me, (*chip, c), True) for j, chip in enumerate(chips)]
        for cp in first:
            cp.start()
        for j, chip in enumerate(chips):
            for t in range(n):
                copy(t, 1 + j, (*chip, c), me).wait_recv()
                fwd = copy(t, 4 + j, (*chip, c), sibling)
                fwd.start()
                passed.append(fwd)
        for t in range(n):
            copy(t, 0, sibling, me).wait_recv()
            for j, chip in enumerate(chips):
                copy(t, 4 + j, (*chip, 1 - c), me).wait_recv()
        for cp in first + passed:
            cp.wait_send()
        for t in range(n):
            mine[t].wait()

    outs = _pcall(
        body, name=name, out_shape=[_sds((NDEV,) + s.shape, s.dtype) for s in shards],
        in_specs=[ANY] * n, out_specs=[ANY] * n,
        scratch=[pltpu.SemaphoreType.DMA((n, 7)), pltpu.SemaphoreType.DMA((n, 7)), pltpu.SemaphoreType.DMA((n,))],
    )(*shards)
    return list(outs)


def _pair_sum(name, grad, recv, where):
    _, r, w = grad.shape
    tr = _row_tile(r, w)
    g4 = grad.reshape(NCHIP, 2, r, w)

    def body(where_ref, g_ref, r_ref, o_ref, own_ref):
        val = (g_ref[...].astype(F32) + r_ref[...].astype(F32)).astype(o_ref.dtype)
        o_ref[...] = val

        @pl.when(pl.program_id(1) == where_ref[1])
        def _():
            own_ref[...] = val

    out = _sds((NCHIP, r, w), grad.dtype)
    return _pcall(
        body, name=name, out_shape=[out, out], grid=(r // tr, NCHIP), prefetch=1,
        in_specs=[pl.BlockSpec((None, None, tr, w), lambda i, k, wr: (k, wr[0], i, 0)),
                  pl.BlockSpec((None, tr, w), lambda i, k, wr: (k, i, 0))],
        out_specs=[pl.BlockSpec((None, tr, w), lambda i, k, wr: (k, i, 0)),
                   pl.BlockSpec((None, tr, w), lambda i, k, wr: (wr[1], i, 0))],
    )(where, g4, recv)


def _row_tile(rows, width, budget=2 * 1024 * 1024):
    best = None
    for t in range(16, rows + 1, 16):
        if rows % t == 0 and t * width * 4 <= budget:
            best = t
    if best is None and rows * width * 4 <= budget:
        best = rows
    assert best is not None, (rows, width)
    return best


def _gemm(name, grid, operands, prods, extras, outs, epilogue, *, nk=1, acc_shape=None, aliases=None, separate=False,
          deps=()):
    n_op, n_ex, n_out = len(operands), len(extras), len(outs)

    def body(*refs):
        ops, ex, out_refs = refs[:n_op], refs[n_op:n_op + n_ex], refs[n_op + n_ex:n_op + n_ex + n_out]
        parts = []
        for pr in prods:
            a, b = ops[pr[0]], ops[pr[1]]
            av = pr[3](a) if len(pr) > 3 and pr[3] else a[...]
            bv = pr[4](b) if len(pr) > 4 and pr[4] else b[...]
            parts.append(lax.dot_general(av, bv, pr[2], preferred_element_type=F32))
        if separate:
            epilogue(parts, ex, out_refs)
            return
        part = parts[0]
        for p in parts[1:]:
            part = part + p
        if nk == 1:
            epilogue(part, ex, out_refs)
        else:
            acc = refs[-1]
            k = pl.program_id(len(grid) - 1)

            @pl.when(k == 0)
            def _():
                acc[...] = part

            @pl.when(k > 0)
            def _():
                acc[...] += part

            @pl.when(k == nk - 1)
            def _():
                epilogue(acc[...], ex, out_refs)

    res = _pcall(
        body, name=name, out_shape=[o[0] for o in outs], grid=grid,
        in_specs=[o[1] for o in operands] + [e[1] for e in extras], out_specs=[o[1] for o in outs],
        scratch=[pltpu.VMEM(acc_shape, F32)] if nk > 1 else [], aliases=aliases, deps=deps,
    )(*[o[0] for o in operands], *[e[0] for e in extras])
    return list(res)


def _store(acc, ex, outs):
    outs[0][...] = acc.astype(outs[0].dtype)


def _store_add_extra(acc, ex, outs):
    v = acc
    for e in ex:
        v = v + e[...]
    outs[0][...] = v.astype(outs[0].dtype)


def _stacked(ref):
    b = ref[...]
    return b.reshape(b.shape[0] * b.shape[1], b.shape[2])


def _pick(c):
    return lambda ref: ref[c]


def _gelu_parts(z):
    c = math.sqrt(2.0 / math.pi)
    t = jnp.tanh(c * (z + 0.044715 * (z * z * z)))
    val = 0.5 * z * (1.0 + t)
    grad = 0.5 * (1.0 + t) + 0.5 * z * (1.0 - t * t) * (c * (1.0 + 3.0 * 0.044715 * z * z))
    return val, grad


def _rms_fwd(name, h, g, deps=()):
    s, d = h.shape
    tr = _row_tile(s, d)

    def body(h_ref, g_ref, o_ref):
        hv = h_ref[...]
        r = lax.rsqrt(jnp.mean(hv * hv, axis=-1, keepdims=True) + RMS_EPS)
        o_ref[...] = (hv * r * g_ref[...]).astype(o_ref.dtype)

    return _pcall(
        body, name=name, out_shape=_sds((s, d), BF16), grid=(s // tr,),
        in_specs=[pl.BlockSpec((tr, d), lambda i: (i, 0)), pl.BlockSpec((1, d), lambda i: (0, 0))],
        out_specs=pl.BlockSpec((tr, d), lambda i: (i, 0)), deps=deps,
    )(h, g.reshape(1, d))


def _accumulate(ref, val, first):
    @pl.when(first)
    def _():
        ref[...] = val

    @pl.when(jnp.logical_not(first))
    def _():
        ref[...] += val


def _rms_bwd(name, h, g, dy, res, deps=()):
    s, d = h.shape
    tr = _row_tile(s, d, budget=1024 * 1024)

    def body(h_ref, g_ref, dy_ref, res_ref, dh_ref, dhb_ref, dg_ref, cs_ref):
        hv = h_ref[...]
        r = lax.rsqrt(jnp.mean(hv * hv, axis=-1, keepdims=True) + RMS_EPS)
        xhat = hv * r
        dyv = dy_ref[...]
        dxh = dyv * g_ref[...]
        dh = res_ref[...] + r * (dxh - xhat * jnp.mean(dxh * xhat, axis=-1, keepdims=True))
        dh_ref[...] = dh
        dhb_ref[...] = dh.astype(BF16)
        first = pl.program_id(0) == 0
        _accumulate(dg_ref, jnp.sum(dyv * xhat, axis=0, keepdims=True), first)
        _accumulate(cs_ref, jnp.sum(dh, axis=0, keepdims=True), first)

    row = pl.BlockSpec((tr, d), lambda i: (i, 0))
    vec = pl.BlockSpec((1, d), lambda i: (0, 0))
    return _pcall(
        body, name=name, out_shape=[_sds((s, d), F32), _sds((s, d), BF16), _sds((1, d), F32), _sds((1, d), F32)],
        grid=(s // tr,), in_specs=[row, vec, row, row], out_specs=[row, row, vec, vec], deps=deps,
    )(h, g.reshape(1, d), dy, res)


def _loss_bwd(name, h, g, target):
    s, d = h.shape
    tr = _row_tile(s, d, budget=1024 * 1024)

    def body(h_ref, g_ref, t_ref, loss_ref, dh_ref, dhb_ref, dg_ref):
        hv = h_ref[...]
        r = lax.rsqrt(jnp.mean(hv * hv, axis=-1, keepdims=True) + RMS_EPS)
        xhat = hv * r
        diff = xhat * g_ref[...] - t_ref[...]
        part = jnp.sum(jnp.sum(diff * diff, axis=1, keepdims=True), axis=0, keepdims=True) * (0.5 / d)
        dyv = diff * (1.0 / d)
        dxh = dyv * g_ref[...]
        dh = r * (dxh - xhat * jnp.mean(dxh * xhat, axis=-1, keepdims=True))
        dh_ref[...] = dh
        dhb_ref[...] = dh.astype(BF16)
        first = pl.program_id(0) == 0
        _accumulate(loss_ref, part, first)
        _accumulate(dg_ref, jnp.sum(dyv * xhat, axis=0, keepdims=True), first)

    row = pl.BlockSpec((tr, d), lambda i: (i, 0))
    vec = pl.BlockSpec((1, d), lambda i: (0, 0))
    one = pl.BlockSpec((1, 1), lambda i: (0, 0))
    return _pcall(
        body, name=name, out_shape=[_sds((1, 1), F32), _sds((s, d), F32), _sds((s, d), BF16), _sds((1, d), F32)],
        grid=(s // tr,), in_specs=[row, vec, row], out_specs=[one, row, row, vec],
    )(h, g.reshape(1, d), target)


def _tril_mask():
    return lax.broadcasted_iota(jnp.int32, (CHUNK, CHUNK), 0) >= lax.broadcasted_iota(jnp.int32, (CHUNK, CHUNK), 1)


def _gmlp_fwd(name, zp, gv, ws, bst):
    s, d2 = zp.shape
    d = d2 // 2
    gw = d // GROUPS

    def body(zp_ref, gv_ref, ws_ref, bst_ref, o_ref):
        u, _ = _gelu_parts(zp_ref[:, :d])
        v, _ = _gelu_parts(zp_ref[:, d:])
        rv = lax.rsqrt(jnp.mean(v * v, axis=-1, keepdims=True) + RMS_EPS)
        vn = (v * rv * gv_ref[...]).astype(BF16)
        tril = _tril_mask()
        for g in range(GROUPS):
            sl = slice(g * gw, (g + 1) * gw)
            wc = jnp.where(tril, ws_ref[g], 0.0).astype(BF16)
            sg = jnp.dot(wc, vn[:, sl], preferred_element_type=F32) + bst_ref[:, g:g + 1]
            o_ref[:, sl] = (u[:, sl] * sg).astype(o_ref.dtype)

    return _pcall(
        body, name=name, out_shape=_sds((s, d), BF16), grid=(s // CHUNK,),
        in_specs=[pl.BlockSpec((CHUNK, d2), lambda i: (i, 0)), pl.BlockSpec((1, d), lambda i: (0, 0)),
                  pl.BlockSpec((GROUPS, CHUNK, CHUNK), lambda i: (0, 0, 0)),
                  pl.BlockSpec((CHUNK, GROUPS), lambda i: (0, 0))],
        out_specs=pl.BlockSpec((CHUNK, d), lambda i: (i, 0)),
    )(zp, gv, ws, bst)


def _gmlp_bwd(name, zp, dgated, gv, ws, bst):
    s, d2 = zp.shape
    d = d2 // 2
    gw = d // GROUPS

    def body(zp_ref, dg_ref, gv_ref, ws_ref, bst_ref, dzp_ref, dws_ref, dbs_ref, dgv_ref, dvn_ref):
        u, gu = _gelu_parts(zp_ref[:, :d])
        v, gvv = _gelu_parts(zp_ref[:, d:])
        rv = lax.rsqrt(jnp.mean(v * v, axis=-1, keepdims=True) + RMS_EPS)
        vhat = v * rv
        vn = (vhat * gv_ref[...]).astype(BF16)
        tril = _tril_mask()
        first = pl.program_id(0) == 0
        ones = jnp.ones((8, gw), F32)

        @pl.when(first)
        def _():
            dws_ref[...] = jnp.zeros_like(dws_ref)
            dbs_ref[...] = jnp.zeros_like(dbs_ref)

        for g in range(GROUPS):
            sl = slice(g * gw, (g + 1) * gw)
            wc = jnp.where(tril, ws_ref[g], 0.0).astype(BF16)
            sg = jnp.dot(wc, vn[:, sl], preferred_element_type=F32) + bst_ref[:, g:g + 1]
            dgs = dg_ref[:, sl]
            ds = dgs * u[:, sl]
            dsb = ds.astype(BF16)
            dzp_ref[:, sl] = (dgs * sg * gu[:, sl]).astype(dzp_ref.dtype)
            dvn_ref[:, sl] = lax.dot_general(wc, dsb, TN, preferred_element_type=F32)
            dw = lax.dot_general(dsb, vn[:, sl], NT, preferred_element_type=F32)
            dws_ref[g] += jnp.where(tril, dw, 0.0)
            dbs_ref[g] += lax.dot_general(ones, ds, NT, preferred_element_type=F32, precision=lax.Precision.HIGHEST)
        dvn = dvn_ref[...]
        dvh = dvn * gv_ref[...]
        dv = rv * (dvh - vhat * jnp.mean(dvh * vhat, axis=-1, keepdims=True))
        dzp_ref[:, d:] = (dv * gvv).astype(dzp_ref.dtype)
        _accumulate(dgv_ref, jnp.sum(dvn * vhat, axis=0, keepdims=True), first)

    return _pcall(
        body, name=name,
        out_shape=[_sds((s, d2), BF16), _sds((GROUPS, CHUNK, CHUNK), F32), _sds((GROUPS, 8, CHUNK), F32),
                   _sds((1, d), F32)],
        grid=(s // CHUNK,),
        in_specs=[pl.BlockSpec((CHUNK, d2), lambda i: (i, 0)), pl.BlockSpec((CHUNK, d), lambda i: (i, 0)),
                  pl.BlockSpec((1, d), lambda i: (0, 0)), pl.BlockSpec((GROUPS, CHUNK, CHUNK), lambda i: (0, 0, 0)),
                  pl.BlockSpec((CHUNK, GROUPS), lambda i: (0, 0))],
        out_specs=[pl.BlockSpec((CHUNK, d2), lambda i: (i, 0)),
                   pl.BlockSpec((GROUPS, CHUNK, CHUNK), lambda i: (0, 0, 0)),
                   pl.BlockSpec((GROUPS, 8, CHUNK), lambda i: (0, 0, 0)), pl.BlockSpec((1, d), lambda i: (0, 0))],
        scratch=[pltpu.VMEM((CHUNK, d), F32)],
    )(zp, dgated, gv, ws, bst)


def _bucket_table():
    dist = np.arange(BLOCK)[:, None] + BLOCK - np.arange(2 * BLOCK)[None, :]
    in_window = (dist >= 0) & (dist < BLOCK)
    dd = np.clip(dist, 0, None)
    max_exact = N_BUCKETS // 2
    dl = np.maximum(dd, 1).astype(np.float32)
    large = max_exact + (np.log(dl / np.float32(max_exact)) / np.float32(math.log(MAX_DISTANCE / max_exact))
                         * np.float32(N_BUCKETS - max_exact)).astype(np.int32)
    large = np.minimum(large, N_BUCKETS - 1)
    bucket = np.where(dd < max_exact, dd, large)
    return np.where(in_window, bucket, -1).astype(np.int32).reshape(1, -1)


def _bias_table(name, rel_bias_t, buckets):
    nh = rel_bias_t.shape[0]
    p = buckets.shape[1]
    tp = 4096

    def body(rb_ref, bk_ref, o_ref):
        bk = bk_ref[...]
        onehot = (lax.broadcasted_iota(jnp.int32, (N_BUCKETS, tp), 0) == bk).astype(F32)
        val = jnp.dot(rb_ref[...], onehot, preferred_element_type=F32, precision=lax.Precision.HIGHEST)
        o_ref[...] = jnp.where(bk >= 0, val, NEG_INF)

    return _pcall(
        body, name=name, out_shape=_sds((nh, p), F32), grid=(p // tp,),
        in_specs=[pl.BlockSpec((nh, N_BUCKETS), lambda i: (0, 0)), pl.BlockSpec((1, tp), lambda i: (0, i))],
        out_specs=pl.BlockSpec((nh, tp), lambda i: (0, i)),
    )(rel_bias_t, buckets)


def _bias_grad(name, dbiases, buckets):
    nh, p = dbiases[0].shape
    n = len(dbiases)
    tp = 4096

    def body(*refs):
        bk_ref, o_ref = refs[n], refs[n + 1]
        onehot = (lax.broadcasted_iota(jnp.int32, (N_BUCKETS, tp), 0) == bk_ref[...]).astype(F32)
        db = refs[0][...]
        for r in refs[1:n]:
            db = db + r[...]
        part = lax.dot_general(onehot, db, NT, preferred_element_type=F32, precision=lax.Precision.HIGHEST)
        _accumulate(o_ref, part, pl.program_id(0) == 0)

    return _pcall(
        body, name=name, out_shape=_sds((N_BUCKETS, nh), F32), grid=(p // tp,),
        in_specs=[pl.BlockSpec((nh, tp), lambda i: (0, i))] * n + [pl.BlockSpec((1, tp), lambda i: (0, i))],
        out_specs=pl.BlockSpec((N_BUCKETS, nh), lambda i: (0, 0)),
    )(*dbiases, buckets)


def _stack_heads(ref):
    return jnp.concatenate([ref[:, hh * HEAD_DIM:(hh + 1) * HEAD_DIM] for hh in range(KV_GROUP)], axis=0)


def _attn_probs(q, kb, b_ref, s_ref):
    kh, i = pl.program_id(0), pl.program_id(1)
    penalty = jnp.where(i > 0, 0.0, NEG_INF).astype(F32)
    col = lax.broadcasted_iota(jnp.int32, (1, 2 * BLOCK), 1)
    bias = b_ref[...].reshape(KV_GROUP * BLOCK, 2 * BLOCK) + jnp.where(col < BLOCK, penalty, 0.0)
    sink = jnp.concatenate([jnp.full((BLOCK, 1), s_ref[kh * KV_GROUP + hh], F32) for hh in range(KV_GROUP)], axis=0)
    s = lax.dot_general(q, kb, NT, preferred_element_type=F32) * 0.125 + bias
    m = jnp.maximum(jnp.max(s, axis=-1, keepdims=True), sink)
    p = jnp.exp(s - m)
    es = jnp.exp(sink - m)
    inv = 1.0 / (jnp.sum(p, axis=-1, keepdims=True) + es)
    return p * inv, es * inv


def _attn_specs(nkv):
    gq = KV_GROUP * HEAD_DIM
    q_spec = pl.BlockSpec((BLOCK, gq), lambda kh, i: (i, kh))
    prev = pl.BlockSpec((None, BLOCK, HEAD_DIM), lambda kh, i: (kh, jnp.maximum(i - 1, 0), 0))
    cur = pl.BlockSpec((None, BLOCK, HEAD_DIM), lambda kh, i: (kh, i, 0))
    bias = pl.BlockSpec((KV_GROUP, BLOCK, 2 * BLOCK), lambda kh, i: (kh, 0, 0))
    smem = pl.BlockSpec(memory_space=pltpu.SMEM)
    return q_spec, prev, cur, bias, smem


def _attn_fwd(name, q, k, v, bias, sinks):
    s, dq = q.shape
    nkv = k.shape[0]
    q_spec, prev, cur, bias_spec, smem = _attn_specs(nkv)

    def body(q_ref, kp_ref, kc_ref, vp_ref, vc_ref, b_ref, s_ref, o_ref):
        kb = jnp.concatenate([kp_ref[...], kc_ref[...]], axis=0)
        vb = jnp.concatenate([vp_ref[...], vc_ref[...]], axis=0)
        p, _ = _attn_probs(_stack_heads(q_ref), kb, b_ref, s_ref)
        o = jnp.dot(p.astype(BF16), vb, preferred_element_type=F32)
        for hh in range(KV_GROUP):
            o_ref[:, hh * HEAD_DIM:(hh + 1) * HEAD_DIM] = o[hh * BLOCK:(hh + 1) * BLOCK].astype(o_ref.dtype)

    return _pcall(
        body, name=name, out_shape=_sds((s, dq), BF16), grid=(nkv, s // BLOCK),
        in_specs=[q_spec, prev, cur, prev, cur, bias_spec, smem], out_specs=q_spec,
    )(q, k, k, v, v, bias, sinks)


def _attn_bwd(name, q, k, v, do, bias, sinks):
    s, dq = q.shape
    nkv = k.shape[0]
    gq = KV_GROUP * HEAD_DIM
    q_spec, prev, cur, bias_spec, smem = _attn_specs(nkv)

    def body(q_ref, do_ref, kp_ref, kc_ref, vp_ref, vc_ref, b_ref, s_ref,
             dq_ref, dbq_ref, dkc_ref, dkp_ref, dvc_ref, dvp_ref, dbias_ref, dsink_ref):
        @pl.when(pl.program_id(1) == 0)
        def _():
            dbias_ref[...] = jnp.zeros_like(dbias_ref)
            dsink_ref[...] = jnp.zeros_like(dsink_ref)
            dbq_ref[...] = jnp.zeros_like(dbq_ref)

        kb = jnp.concatenate([kp_ref[...], kc_ref[...]], axis=0)
        vb = jnp.concatenate([vp_ref[...], vc_ref[...]], axis=0)
        q, do = _stack_heads(q_ref), _stack_heads(do_ref)
        p, ps = _attn_probs(q, kb, b_ref, s_ref)
        dp = lax.dot_general(do, vb, NT, preferred_element_type=F32)
        delta = jnp.sum(p * dp, axis=-1, keepdims=True)
        ds = p * (dp - delta)
        dsb = ds.astype(BF16)
        dq = jnp.dot(dsb, kb, preferred_element_type=F32) * 0.125
        dsk = -(ps * delta)
        for hh in range(KV_GROUP):
            sl, rows = slice(hh * HEAD_DIM, (hh + 1) * HEAD_DIM), slice(hh * BLOCK, (hh + 1) * BLOCK)
            dq_ref[:, sl] = dq[rows].astype(dq_ref.dtype)
            dbq_ref[:, sl] += jnp.sum(dq[rows], axis=0, keepdims=True)
            dsink_ref[:, hh:hh + 1] += jnp.sum(dsk[rows], axis=0, keepdims=True)
        dkb = lax.dot_general(dsb, q, TN, preferred_element_type=F32) * 0.125
        dvb = lax.dot_general(p.astype(BF16), do, TN, preferred_element_type=F32)
        dkp_ref[...], dkc_ref[...] = dkb[:BLOCK], dkb[BLOCK:]
        dvp_ref[...], dvc_ref[...] = dvb[:BLOCK], dvb[BLOCK:]
        dbias_ref[...] += ds.reshape(KV_GROUP, BLOCK, 2 * BLOCK)

    kv_out = _sds((nkv, s, HEAD_DIM), F32)
    return _pcall(
        body, name=name,
        out_shape=[_sds((s, dq), BF16), _sds((1, dq), F32), kv_out, kv_out, kv_out, kv_out,
                   _sds((nkv * KV_GROUP, BLOCK, 2 * BLOCK), F32), _sds((nkv, 1, KV_GROUP), F32)],
        grid=(nkv, s // BLOCK),
        in_specs=[q_spec, q_spec, prev, cur, prev, cur, bias_spec, smem],
        out_specs=[q_spec, pl.BlockSpec((1, gq), lambda kh, i: (0, kh)), cur, cur, cur, cur, bias_spec,
                   pl.BlockSpec((None, 1, KV_GROUP), lambda kh, i: (kh, 0, 0))],
    )(q, do, k, k, v, v, bias, sinks)


def _kv_grad(name, parts):
    nkv, s, _ = parts[0][0].shape
    nb = s // BLOCK
    w = 2 * nkv * HEAD_DIM
    n = len(parts)

    def body(*refs):
        o_ref, cs_ref = refs[4 * n], refs[4 * n + 1]
        i = pl.program_id(0)
        keep = jnp.where(i < nb - 1, 1.0, 0.0).astype(F32)

        @pl.when(i == 0)
        def _():
            cs_ref[...] = jnp.zeros_like(cs_ref)

        for which in range(2):
            for hh in range(nkv):
                val = None
                for l in range(n):
                    cur_ref, nxt_ref = refs[4 * l + 2 * which], refs[4 * l + 2 * which + 1]
                    t = cur_ref[hh] + keep * nxt_ref[hh]
                    val = t if val is None else val + t
                sl = slice((which * nkv + hh) * HEAD_DIM, (which * nkv + hh + 1) * HEAD_DIM)
                o_ref[:, sl] = val.astype(o_ref.dtype)
                cs_ref[:, sl] += jnp.sum(val, axis=0, keepdims=True)

    cur = pl.BlockSpec((nkv, BLOCK, HEAD_DIM), lambda i: (0, i, 0))
    nxt = pl.BlockSpec((nkv, BLOCK, HEAD_DIM), lambda i: (0, jnp.minimum(i + 1, nb - 1), 0))
    flat = [a for p in parts for a in p]
    return _pcall(
        body, name=name, out_shape=[_sds((s, w), BF16), _sds((1, w), F32)], grid=(nb,),
        in_specs=[cur, nxt] * (2 * n),
        out_specs=[pl.BlockSpec((BLOCK, w), lambda i: (i, 0)), pl.BlockSpec((1, w), lambda i: (0, 0))],
    )(*flat)


def _adamw_math(w, g, m, v):
    m = ADAM_B1 * m + (1.0 - ADAM_B1) * g
    v = ADAM_B2 * v + (1.0 - ADAM_B2) * (g * g)
    m_hat = m / (1.0 - ADAM_B1 ** ADAM_STEP)
    v_hat = v / (1.0 - ADAM_B2 ** ADAM_STEP)
    delta = -ADAM_LR * (m_hat / (jnp.sqrt(v_hat) + ADAM_EPS) + ADAM_WD * w)
    return delta, m, v


def _adamw_shard(name, w, m, v, parts, row0, layer, prev, deps=()):
    _, r, wd = w.shape
    tr = _row_tile(r, wd, budget=3 * 512 * 1024)
    assert row0 % tr == 0

    def body(w_ref, m_ref, v_ref, p_ref, a0, a1, a2, a3, g_ref, d_ref, nm_ref, nv_ref):
        g = p_ref[0].astype(F32)
        for k in range(1, NCHIP):
            g = g + p_ref[k].astype(F32)
        delta, nm, nv = _adamw_math(w_ref[...], g, m_ref[...], v_ref[...])
        g_ref[...], d_ref[...], nm_ref[...], nv_ref[...] = g, delta, nm, nv

    par = pl.BlockSpec((None, tr, wd), lambda i: (layer, i, 0))
    out = _sds(w.shape, F32)
    return _pcall(
        body, name=name, out_shape=[out, out, out, out], grid=(r // tr,),
        in_specs=[par, par, par, pl.BlockSpec((NCHIP, tr, wd), lambda i: (0, row0 // tr + i, 0)), ANY, ANY, ANY, ANY],
        out_specs=[par, par, par, par], aliases={4: 0, 5: 1, 6: 2, 7: 3}, deps=deps,
    )(w, m, v, parts, *prev)


def _sum_devices(name, gathered):
    _, r, wd = gathered.shape

    def body(g_ref, o_ref):
        acc = g_ref[0]
        for k in range(1, NDEV):
            acc = acc + g_ref[k]
        o_ref[...] = acc

    return _pcall(body, name=name, out_shape=_sds((r, wd), F32), grid=(1,),
                  in_specs=[pl.BlockSpec((NDEV, r, wd), lambda i: (0, 0, 0))],
                  out_specs=pl.BlockSpec((r, wd), lambda i: (0, 0)))(gathered)


def _adamw_flat(name, w, g, m, v):
    shape = w.shape

    def body(w_ref, g_ref, m_ref, v_ref, d_ref, nm_ref, nv_ref):
        d_ref[...], nm_ref[...], nv_ref[...] = _adamw_math(w_ref[...], g_ref[...], m_ref[...], v_ref[...])

    spec = pl.BlockSpec(shape, lambda i: (0, 0))
    out = _sds(shape, F32)
    return _pcall(body, name=name, out_shape=[out, out, out], grid=(1,), in_specs=[spec] * 4,
                  out_specs=[spec] * 3)(w, g, m, v)


def _cast_into(name, src, layer, buf, row0, me):
    _, r, wd = src.shape
    tr = _row_tile(r, wd)
    assert row0 % tr == 0

    def body(me_ref, s_ref, b_ref, o_ref):
        o_ref[...] = s_ref[...].astype(o_ref.dtype)

    return _pcall(
        body, name=name, out_shape=_sds(buf.shape, buf.dtype), grid=(r // tr,), prefetch=1,
        in_specs=[pl.BlockSpec((None, tr, wd), lambda i, mr: (layer, i, 0)), ANY],
        out_specs=pl.BlockSpec((None, tr, wd), lambda i, mr: (mr[0], row0 // tr + i, 0)), aliases={2: 0},
    )(me, src, buf)


def _pack(arrays):
    rows = []
    for a in arrays:
        flat = a.reshape(-1).astype(F32)
        pad = (-flat.shape[0]) % 1024
        rows.append(jnp.pad(flat, (0, pad)).reshape(-1, 128))
    return jnp.concatenate(rows, axis=0)


def _unpack(packed, shapes):
    out, r = [], 0
    for shp in shapes:
        n = int(np.prod(shp))
        nr = (n + 1023) // 1024 * 8
        out.append(packed[r:r + nr].reshape(-1)[:n].reshape(shp))
        r += nr
    return out


def kernel(x, mix_norm, ffn_norm, a_w_in, a_norm_v, a_w_s, a_b_s, a_w_out, kv_norm, w_kv, b_kv, b_w_q, b_b_q, b_sinks, b_w_o, b_b_o, rel_bias, ffn_w_gate, ffn_w_up, ffn_w_down, final_norm, loss_target, m_mix_norm, m_ffn_norm, m_a_w_in, m_a_norm_v, m_a_w_s, m_a_b_s, m_a_w_out, m_kv_norm, m_w_kv, m_b_kv, m_b_w_q, m_b_b_q, m_b_sinks, m_b_w_o, m_b_b_o, m_rel_bias, m_ffn_w_gate, m_ffn_w_up, m_ffn_w_down, m_final_norm, v_mix_norm, v_ffn_norm, v_a_w_in, v_a_norm_v, v_a_w_s, v_a_b_s, v_a_w_out, v_kv_norm, v_w_kv, v_b_kv, v_b_w_q, v_b_b_q, v_b_sinks, v_b_w_o, v_b_b_o, v_rel_bias, v_ffn_w_gate, v_ffn_w_up, v_ffn_w_down, v_final_norm):
    _, S, D = x.shape
    LA, LB, L = a_w_in.shape[0], b_w_q.shape[0], ffn_w_gate.shape[0]
    F = ffn_w_gate.shape[2]
    DS = D // NDEV
    ZC = a_w_in.shape[2]
    KVW = w_kv.shape[1]
    NKV = KVW // (2 * HEAD_DIM)
    NH = D // HEAD_DIM
    assert ZC * NDEV == 2 * D and NH == NKV * KV_GROUP and S % BLOCK == 0
    TM = min(1024, S)
    TN_ = min(1024, D)
    TS = min(512, D)
    KC = 4

    ix, iy, ic = lax.axis_index("x"), lax.axis_index("y"), lax.axis_index("c")
    me = (4 * ix + 2 * iy + ic).astype(jnp.int32)
    me1 = me.reshape(1)
    where = jnp.stack([ic, 2 * ix + iy]).astype(jnp.int32)

    def tr3(a):
        return jnp.transpose(a, (0, 2, 1))

    gate_t, up_t = tr3(ffn_w_gate), tr3(ffn_w_up)
    w_kv3 = w_kv.reshape((1,) + w_kv.shape)

    def layer_arrays(l):
        arrs = [("gu", 2 * F, D, [(gate_t, l, 0), (up_t, l, F)]), ("down", F, D, [(ffn_w_down, l, 0)])]
        if l < LA:
            arrs += [("win", D, ZC, [(a_w_in, l, 0)]), ("wout", DS, D, [(a_w_out, l, 0)])]
            if l == LA - 1:
                arrs.append(("wkv", DS, KVW, [(w_kv3, 0, 0)]))
        else:
            i_b = l - LA
            arrs.append(("wqo", 2 * DS, D, [(b_w_q, i_b, 0), (b_w_o, i_b, DS)]))
        return arrs

    started = []
    token = None
    for l in range(L):
        keys, bufs = [], []
        for key, rows, width, sources in layer_arrays(l):
            buf = lax.empty((NDEV, rows, width), BF16)
            for si, (src, li, row0) in enumerate(sources):
                buf = _cast_into(f"cast_{key}{l}_{si}", src, li, buf, row0, me1)
            keys.append(key)
            bufs.append(buf)
        if l == 0:
            nv_rows = _pack([a_norm_v])
            nv = _cast_into("put_norm_v", nv_rows.reshape((1,) + nv_rows.shape), 0,
                            lax.empty((NDEV,) + nv_rows.shape, F32), 0, me1)
            keys.append("norm_v")
            bufs.append(nv)
        send, recv, bufs, token = _gather_start(f"gather_start{l}", bufs, [] if token is None else [token])
        started.append((keys, bufs, send, recv))

    def finish_gather(l, deps):
        keys, bufs, send, recv = started[l]
        fsend, frecv, bufs = _gather_forward(f"gather_forward{l}", bufs, send, recv, deps)
        bufs = _gather_finish(f"gather_finish{l}", bufs, send, recv, fsend, frecv)
        return dict(zip(keys, bufs))

    buckets = jnp.asarray(_bucket_table())
    bias = _bias_table("bias_table", rel_bias.T, buckets).reshape(NH, BLOCK, 2 * BLOCK)

    def rows_full(tm):
        return pl.BlockSpec((tm, D), lambda i, j: (i, 0))

    def tile(tm, tn):
        return pl.BlockSpec((tm, tn), lambda i, j: (i, j))

    vec_tile = pl.BlockSpec((1, TN_), lambda i, j: (0, j))

    def ffn_forward(l, wl, h_mid, tag):
        xf = _rms_fwd(f"ffn_norm_fwd{tag}", h_mid, ffn_norm[l])

        def ep(parts, ex, outs):
            a, b = parts
            outs[0][0] = a.astype(BF16)
            outs[0][1] = b.astype(BF16)
            outs[1][...] = (a * jax.nn.sigmoid(a) * b).astype(BF16)

        ab, hid = _gemm(
            f"ffn_up{tag}", (S // TM, NDEV),
            [(xf, rows_full(TM)),
             (wl["gu"], pl.BlockSpec((None, F, D), lambda i, e: (e, 0, 0))),
             (wl["gu"], pl.BlockSpec((None, F, D), lambda i, e: (e, 1, 0)))],
            [(0, 1, NT), (0, 2, NT)], [],
            [(_sds((2, NDEV, S, F), BF16), pl.BlockSpec((2, None, TM, F), lambda i, e: (0, e, i, 0))),
             (_sds((NDEV, S, F), BF16), pl.BlockSpec((None, TM, F), lambda i, e: (e, i, 0)))],
            ep, separate=True)
        (h_out,) = _gemm(
            f"ffn_down{tag}", (S // TM, D // TN_, NDEV // KC),
            [(hid, pl.BlockSpec((KC, TM, F), lambda i, j, k: (k, i, 0))),
             (wl["down"], pl.BlockSpec((KC, F, TN_), lambda i, j, k: (k, 0, j)))],
            [(0, 1, NN, _pick(c), _pick(c)) for c in range(KC)],
            [(h_mid, pl.BlockSpec((TM, TN_), lambda i, j, k: (i, j)))],
            [(_sds((S, D), F32), pl.BlockSpec((TM, TN_), lambda i, j, k: (i, j)))],
            _store_add_extra, nk=NDEV // KC, acc_shape=(TM, TN_))
        return dict(h_mid=h_mid, xf=xf, ab=ab, hid=hid), h_out

    def stacked_rows_gemm(name, a, wmat, blk, extras, ep, out_dtype):
        return _gemm(
            name, (S // TM, D // TN_),
            [(a, rows_full(TM)), (wmat, pl.BlockSpec((NDEV, DS, TN_), lambda i, j: (0, blk, j)))],
            [(0, 1, NN, None, _stacked)], extras,
            [(_sds((S, D), out_dtype), tile(TM, TN_))], ep)[0]

    def back_rows_gemm(name, a, wmat, blk, out_dtype, deps=()):
        return _gemm(
            name, (S // TM, NDEV),
            [(a, rows_full(TM)), (wmat, pl.BlockSpec((None, DS, D), lambda i, e: (e, blk, 0)))],
            [(0, 1, NT)], [], [(_sds((S, D), out_dtype), pl.BlockSpec((TM, DS), lambda i, e: (i, e)))], _store,
            deps=deps)[0]

    def grad_rows_gemm(name, act, d_bf, buf, blk):
        return _gemm(
            name, (NDEV,),
            [(act, pl.BlockSpec((S, DS), lambda e: (0, e))), (d_bf, pl.BlockSpec((S, D), lambda e: (0, 0)))],
            [(0, 1, TN)], [(buf, ANY)],
            [(_sds(buf.shape, BF16), pl.BlockSpec((None, DS, D), lambda e: (e, blk, 0)))],
            _store, aliases={2: 0})[0]

    saved, weights = [], []
    h = x.reshape(S, D)
    k_heads = v_heads = hn = h_kv = norm_v = None
    for layer in range(L):
        wl = finish_gather(layer, [token] if layer == 0 else [h])
        weights.append(wl)
        if layer == 0:
            nv_all = wl["norm_v"]
            norm_v = jnp.transpose(nv_all.reshape(NDEV, -1)[:, :LA * DS].reshape(NDEV, LA, DS), (1, 0, 2)).reshape(LA, D)
        sv = dict(h_in=h)
        xn = _rms_fwd(f"mix_norm_fwd{layer}", h, mix_norm[layer])
        sv["xn"] = xn
        if layer < LA:
            i_a = layer
            (zp,) = _gemm(
                f"gmlp_in{layer}", (S // TM, NDEV),
                [(xn, rows_full(TM)), (wl["win"], pl.BlockSpec((None, D, ZC), lambda i, e: (e, 0, 0)))],
                [(0, 1, NN)], [], [(_sds((S, 2 * D), F32), pl.BlockSpec((TM, ZC), lambda i, e: (i, e)))], _store)
            bst = a_b_s[i_a].T
            gated = _gmlp_fwd(f"gmlp_gate{layer}", zp, norm_v[i_a].reshape(1, D), a_w_s[i_a], bst)
            sv.update(zp=zp, gated=gated, bst=bst)
            h_mid = stacked_rows_gemm(f"gmlp_out{layer}", gated, wl["wout"], 0, [(h, tile(TM, TN_))],
                                      _store_add_extra, F32)
        else:
            i_b = layer - LA
            q = stacked_rows_gemm(f"attn_q{layer}", xn, wl["wqo"], 0, [(b_b_q[i_b].reshape(1, D), vec_tile)],
                                  _store_add_extra, BF16)
            attn = _attn_fwd(f"attn_fwd{layer}", q, k_heads, v_heads, bias, b_sinks[i_b])
            sv.update(q=q, attn=attn)
            h_mid = stacked_rows_gemm(f"attn_o{layer}", attn, wl["wqo"], 1,
                                      [(h, tile(TM, TN_)), (b_b_o[i_b].reshape(1, D), vec_tile)],
                                      _store_add_extra, F32)
        fsv, h = ffn_forward(layer, wl, h_mid, str(layer))
        sv.update(fsv)
        saved.append(sv)
        if layer == LA - 1:
            h_kv = h
            hn = _rms_fwd("kv_norm_fwd", h, kv_norm)

            def kv_ep(acc, ex, outs):
                val = acc + ex[0][...]
                for hh in range(NKV):
                    outs[0][hh] = val[:, hh * HEAD_DIM:(hh + 1) * HEAD_DIM].astype(BF16)
                    outs[1][hh] = val[:, (NKV + hh) * HEAD_DIM:(NKV + hh + 1) * HEAD_DIM].astype(BF16)

            k_heads, v_heads = _gemm(
                "kv_proj", (S // TM,),
                [(hn, pl.BlockSpec((TM, D), lambda i: (i, 0))),
                 (wl["wkv"], pl.BlockSpec((NDEV, DS, KVW), lambda i: (0, 0, 0)))],
                [(0, 1, NN, None, _stacked)], [(b_kv.reshape(1, KVW), pl.BlockSpec((1, KVW), lambda i: (0, 0)))],
                [(_sds((NKV, S, HEAD_DIM), BF16), pl.BlockSpec((NKV, TM, HEAD_DIM), lambda i: (0, i, 0)))] * 2,
                kv_ep)

    loss11, d, d_bf, g_final = _loss_bwd("loss_bwd", h, final_norm, loss_target.reshape(S, D))
    loss = lax.psum(loss11[0, 0], AXES)

    g_mix, g_ffn = [None] * L, [None] * L
    g_ws, g_bs, g_nv = [None] * LA, [None] * LA, [None] * LA
    g_bq, g_sink, g_bo = [None] * LB, [None] * LB, [None] * LB
    dbiases = []
    kv_parts = []
    g_kvn = g_bkv = None
    exchanges = [[] for _ in range(L)]
    pending = None
    grads_wkv = None
    newest = []

    def new_grads(l):
        return {key: lax.empty((NDEV, rows, width), BF16) for key, rows, width, _ in layer_arrays(l)}

    def exchange_begin(tag, l, gl, keys):
        grads = [gl[k] for k in keys]
        lands = [lax.empty((NCHIP,) + g.shape[1:], BF16) for g in grads]
        send, recv, grads, lands, tok = _sibling_start(f"rs_sibling_start{tag}", grads, lands, [])
        newest[:] = [tok]
        return dict(tag=tag, layer=l, keys=keys, grads=grads, lands=lands, send=send, recv=recv)

    def exchange_middle(st, dep):
        tag = st["tag"]
        grads, lands = _sibling_finish(f"rs_sibling_finish{tag}", st["grads"], st["lands"], st["send"], st["recv"], [dep])
        sums, own = [], []
        for t, key in enumerate(st["keys"]):
            s_, o_ = _pair_sum(f"pair_sum_{key}{tag}", grads[t], lands[t], where)
            sums.append(s_)
            own.append(o_)
        send, recv, sums, own, tok = _chips_start(f"rs_chips_start{tag}", sums, own, [])
        newest[:] = [tok]
        st.update(sums=sums, own=own, send2=send, recv2=recv)
        exchanges[st["layer"]].append(st)

    def exchange_end(st, dep):
        lands = _chips_finish(f"rs_chips_finish{st['tag']}", st["sums"], st["own"], st["send2"], st["recv2"], [dep])
        return dict(zip(st["keys"], lands))

    for layer in reversed(range(L)):
        sv, wl = saved[layer], weights[layer]
        tag = str(layer)
        gl = new_grads(layer)
        if grads_wkv is not None and layer == LA - 1:
            gl["wkv"] = grads_wkv
        def dhid_ep(acc, ex, outs):
            a, b = ex[0][0].astype(F32), ex[0][1].astype(F32)
            sg = jax.nn.sigmoid(a)
            outs[0][0] = (acc * b * (sg * (1.0 + a * (1.0 - sg)))).astype(BF16)
            outs[0][1] = (acc * (a * sg)).astype(BF16)

        ab_spec = pl.BlockSpec((2, None, TM, F), lambda i, e: (0, e, i, 0))
        (dab,) = _gemm(
            f"ffn_dhid{tag}", (S // TM, NDEV),
            [(d_bf, rows_full(TM)), (wl["down"], pl.BlockSpec((None, F, D), lambda i, e: (e, 0, 0)))],
            [(0, 1, NT)], [(sv["ab"], ab_spec)], [(_sds((2, NDEV, S, F), BF16), ab_spec)], dhid_ep,
            deps=list(newest))
        if pending:
            exchange_middle(pending, dab)
        (gl["down"],) = _gemm(
            f"ffn_dwdown{tag}", (NDEV,),
            [(sv["hid"], pl.BlockSpec((None, S, F), lambda e: (e, 0, 0))),
             (d_bf, pl.BlockSpec((S, D), lambda e: (0, 0)))],
            [(0, 1, TN)], [(gl["down"], ANY)],
            [(_sds(gl["down"].shape, BF16), pl.BlockSpec((None, F, D), lambda e: (e, 0, 0)))],
            _store, aliases={2: 0}, deps=list(newest))
        (gl["gu"],) = _gemm(
            f"ffn_dwup{tag}", (2, NDEV),
            [(dab, pl.BlockSpec((None, None, S, F), lambda w, e: (w, e, 0, 0))),
             (sv["xf"], pl.BlockSpec((S, D), lambda w, e: (0, 0)))],
            [(0, 1, TN)], [(gl["gu"], ANY)],
            [(_sds(gl["gu"].shape, BF16), pl.BlockSpec((None, F, D), lambda w, e: (e, w, 0)))],
            _store, aliases={2: 0})
        ffn_group = exchange_begin(f"_ffn{tag}", layer, gl, ["gu", "down"])
        (dxf,) = _gemm(
            f"ffn_dx{tag}", (S // TM, D // TN_, 2 * NDEV // KC),
            [(dab.reshape(2 * NDEV // KC, KC, S, F), pl.BlockSpec((None, KC, TM, F), lambda i, j, k: (k, 0, i, 0))),
             (wl["gu"], pl.BlockSpec((KC, F, TN_), lambda i, j, k: (k % (NDEV // KC), k // (NDEV // KC), j)))],
            [(0, 1, NN, _pick(c), _pick(c)) for c in range(KC)], [],
            [(_sds((S, D), F32), pl.BlockSpec((TM, TN_), lambda i, j, k: (i, j)))],
            _store, nk=2 * NDEV // KC, acc_shape=(TM, TN_), deps=list(newest))
        exchange_middle(ffn_group, dxf)
        d, d_bf, g_ffn[layer], colsum = _rms_bwd(f"ffn_norm_bwd{tag}", sv["h_mid"], ffn_norm[layer], dxf, d,
                                                 deps=list(newest))
        if layer < LA:
            i_a = layer
            dgated = back_rows_gemm(f"gmlp_dgated{tag}", d_bf, wl["wout"], 0, F32)
            gl["wout"] = grad_rows_gemm(f"gmlp_dwout{tag}", sv["gated"], d_bf, gl["wout"], 0)
            dzp, g_ws[i_a], dbs, g_nv[i_a] = _gmlp_bwd(f"gmlp_bwd{tag}", sv["zp"], dgated,
                                                       norm_v[i_a].reshape(1, D), a_w_s[i_a], sv["bst"])
            g_bs[i_a] = dbs[:, 0, :]
            (gl["win"],) = _gemm(
                f"gmlp_dwin{tag}", (NDEV, D // TS),
                [(sv["xn"], pl.BlockSpec((S, TS), lambda e, i: (0, i))),
                 (dzp, pl.BlockSpec((S, ZC), lambda e, i: (0, e)))],
                [(0, 1, TN)], [(gl["win"], ANY)],
                [(_sds(gl["win"].shape, BF16), pl.BlockSpec((None, TS, ZC), lambda e, i: (e, i, 0)))],
                _store, aliases={2: 0})
            (dxn,) = _gemm(
                f"gmlp_dx{tag}", (S // TM, D // TN_, NDEV),
                [(dzp, pl.BlockSpec((TM, ZC), lambda i, j, e: (i, e))),
                 (wl["win"], pl.BlockSpec((None, TN_, ZC), lambda i, j, e: (e, j, 0)))],
                [(0, 1, NT)], [], [(_sds((S, D), F32), pl.BlockSpec((TM, TN_), lambda i, j, e: (i, j)))],
                _store, nk=NDEV, acc_shape=(TM, TN_))
        else:
            i_b = layer - LA
            g_bo[i_b] = colsum
            dattn = back_rows_gemm(f"attn_dout{tag}", d_bf, wl["wqo"], 1, BF16)
            gl["wqo"] = grad_rows_gemm(f"attn_dwo{tag}", sv["attn"], d_bf, gl["wqo"], 1)
            dq, g_bq[i_b], dkc, dkp, dvc, dvp, dbias, dsink = _attn_bwd(
                f"attn_bwd{tag}", sv["q"], k_heads, v_heads, dattn, bias, b_sinks[i_b])
            kv_parts.append((dkc, dkp, dvc, dvp))
            g_sink[i_b] = dsink.reshape(NH)
            dbiases.append(dbias.reshape(NH, BLOCK * 2 * BLOCK))
            gl["wqo"] = grad_rows_gemm(f"attn_dwq{tag}", sv["xn"], dq, gl["wqo"], 0)
            dxn = back_rows_gemm(f"attn_dx{tag}", dq, wl["wqo"], 0, F32)
        d, d_bf, g_mix[layer], _ = _rms_bwd(f"mix_norm_bwd{tag}", sv["h_in"], mix_norm[layer], dxn, d)
        pending = exchange_begin(f"_mix{tag}", layer, gl, [k for k in gl if k not in ("gu", "down")])
        if layer == LA:
            wkv = weights[LA - 1]["wkv"]
            dkv, g_bkv = _kv_grad("kv_grad", kv_parts)
            (grads_wkv,) = _gemm(
                "kv_dw", (NDEV,),
                [(hn, pl.BlockSpec((S, DS), lambda e: (0, e))), (dkv, pl.BlockSpec((S, KVW), lambda e: (0, 0)))],
                [(0, 1, TN)], [(lax.empty((NDEV, DS, KVW), BF16), ANY)],
                [(_sds((NDEV, DS, KVW), BF16), pl.BlockSpec((None, DS, KVW), lambda e: (e, 0, 0)))],
                _store, aliases={2: 0}, deps=list(newest))
            (dhn,) = _gemm(
                "kv_dx", (S // TM, NDEV),
                [(dkv, pl.BlockSpec((TM, KVW), lambda i, e: (i, 0))),
                 (wkv, pl.BlockSpec((None, DS, KVW), lambda i, e: (e, 0, 0)))],
                [(0, 1, NT)], [], [(_sds((S, D), F32), pl.BlockSpec((TM, DS), lambda i, e: (i, e)))], _store)
            d, d_bf, g_kvn, _ = _rms_bwd("kv_norm_bwd", h_kv, kv_norm, dhn, d)
    grad_x = d.reshape(x.shape)

    exchange_middle(pending, d)

    g_rel = _bias_grad("bias_grad", dbiases, buckets)
    small_local = _pack([jnp.concatenate(g_mix, axis=0), jnp.concatenate(g_ffn, axis=0), jnp.stack(g_ws),
                         jnp.stack(g_bs), g_kvn, g_bkv, jnp.concatenate(g_bq, axis=0), jnp.stack(g_sink),
                         jnp.concatenate(g_bo, axis=0), g_rel, g_final, jnp.concatenate(g_nv, axis=0)])
    small_slot = _cast_into("put_small_grads", small_local.reshape((1,) + small_local.shape), 0,
                            lax.empty((NDEV,) + small_local.shape, F32), 0, me1)
    s_send, s_recv, s_bufs, s_tok = _gather_start("gather_small_start", [small_slot], list(newest))

    results = {}
    after = [s_tok]

    def upd(pname, w, m, v, l, li, lands, row0):
        w3 = w if w.ndim == 3 else w.reshape((1,) + w.shape)
        prev = results.get(pname) or [lax.empty(w3.shape, F32) for _ in range(4)]
        results[pname] = _adamw_shard(f"adamw_{pname}{l}", w3, m.reshape(w3.shape), v.reshape(w3.shape), lands,
                                      row0, li, prev, deps=list(after))
        after[:] = [results[pname][0]]

    for l in reversed(range(L)):
        for st in exchanges[l]:
            lands = exchange_end(st, after[0])
            if "gu" in lands:
                upd("ffn_w_gate", gate_t, tr3(m_ffn_w_gate), tr3(v_ffn_w_gate), l, l, lands["gu"], 0)
                upd("ffn_w_up", up_t, tr3(m_ffn_w_up), tr3(v_ffn_w_up), l, l, lands["gu"], F)
                upd("ffn_w_down", ffn_w_down, m_ffn_w_down, v_ffn_w_down, l, l, lands["down"], 0)
            if "win" in lands:
                upd("a_w_in", a_w_in, m_a_w_in, v_a_w_in, l, l, lands["win"], 0)
                upd("a_w_out", a_w_out, m_a_w_out, v_a_w_out, l, l, lands["wout"], 0)
            if "wkv" in lands:
                upd("w_kv", w_kv, m_w_kv, v_w_kv, l, 0, lands["wkv"], 0)
            if "wqo" in lands:
                upd("b_w_q", b_w_q, m_b_w_q, v_b_w_q, l, l - LA, lands["wqo"], 0)
                upd("b_w_o", b_w_o, m_b_w_o, v_b_w_o, l, l - LA, lands["wqo"], DS)
    for pname in ("ffn_w_gate", "ffn_w_up"):
        results[pname] = [tr3(r) for r in results[pname]]
    results["w_kv"] = [r.reshape(w_kv.shape) for r in results["w_kv"]]

    small_w = [mix_norm, ffn_norm, a_w_s, a_b_s, kv_norm, b_kv, b_b_q, b_sinks, b_b_o, rel_bias, final_norm]
    small_m = [m_mix_norm, m_ffn_norm, m_a_w_s, m_a_b_s, m_kv_norm, m_b_kv, m_b_b_q, m_b_sinks, m_b_b_o, m_rel_bias,
               m_final_norm]
    small_v = [v_mix_norm, v_ffn_norm, v_a_w_s, v_a_b_s, v_kv_norm, v_b_kv, v_b_b_q, v_b_sinks, v_b_b_o, v_rel_bias,
               v_final_norm]
    shapes = [w.shape for w in small_w] + [(LA, D)]
    s_fsend, s_frecv, s_bufs = _gather_forward("gather_small_forward", s_bufs, s_send, s_recv, list(after))
    (small_all,) = _gather_finish("gather_small_finish", s_bufs, s_send, s_recv, s_fsend, s_frecv)
    small_sum = _sum_devices("sum_small_grads", small_all)
    small_g = _unpack(small_sum, shapes)
    g_normv = lax.dynamic_slice_in_dim(small_g[-1], me * DS, DS, axis=1)
    small_g = small_g[:-1] + [g_normv]
    small_w, small_m, small_v = small_w + [a_norm_v], small_m + [m_a_norm_v], small_v + [v_a_norm_v]
    shapes = [w.shape for w in small_w]
    s_delta, s_m, s_v = _adamw_flat("adamw_small", _pack(small_w), _pack(small_g), _pack(small_m), _pack(small_v))
    s_delta, s_m, s_v = _unpack(s_delta, shapes), _unpack(s_m, shapes), _unpack(s_v, shapes)

    names = ["mix_norm", "ffn_norm", "a_w_in", "a_norm_v", "a_w_s", "a_b_s", "a_w_out", "kv_norm", "w_kv", "b_kv",
             "b_w_q", "b_b_q", "b_sinks", "b_w_o", "b_b_o", "rel_bias", "ffn_w_gate", "ffn_w_up", "ffn_w_down",
             "final_norm"]
    small_names = ["mix_norm", "ffn_norm", "a_w_s", "a_b_s", "kv_norm", "b_kv", "b_b_q", "b_sinks", "b_b_o", "rel_bias",
                   "final_norm", "a_norm_v"]
    res = {}
    for idx, nm in enumerate(small_names):
        res[nm] = (small_g[idx].reshape(shapes[idx]), s_delta[idx], s_m[idx], s_v[idx])
    for nm, u in results.items():
        res[nm] = tuple(u)
    out = [loss, grad_x]
    for part in range(4):
        out += [res[nm][part] for nm in names]
    return tuple(out)
```

```python
import math

import numpy as np
import jax
import jax.numpy as jnp
from jax import lax
from jax.experimental import pallas as pl
from jax.experimental.pallas import tpu as pltpu

F32 = jnp.float32
BF16 = jnp.bfloat16
AXES = ("x", "y", "c")
NDEV = 8
NCHIP = 4
CHUNK = 128
GROUPS = 8
HEAD_DIM = 64
KV_GROUP = 8
BLOCK = 128
N_BUCKETS = 32
MAX_DISTANCE = 128
RMS_EPS = 1e-5
NEG_INF = -1e30
ADAM_LR, ADAM_B1, ADAM_B2, ADAM_EPS, ADAM_WD, ADAM_STEP = 0.001, 0.9, 0.999, 1e-08, 0.01, 10
VMEM_LIMIT_BYTES = 56 * 1024 * 1024

NN = (((1,), (0,)), ((), ()))
NT = (((1,), (1,)), ((), ()))
TN = (((0,), (0,)), ((), ()))
ANY = pl.BlockSpec(memory_space=pl.ANY)
HBM = pl.BlockSpec(memory_space=pltpu.HBM)
SEM = pl.BlockSpec(memory_space=pltpu.SEMAPHORE)
MESH = pl.DeviceIdType.MESH
EFFECT = pltpu.SideEffectType.DATAFLOW_SIDE_EFFECTING


def _pcall(body, *, name, out_shape, in_specs, out_specs, grid=(), scratch=(), aliases=None, prefetch=0, deps=()):
    n_in, n_dep = len(in_specs), len(deps)
    if n_dep:
        inner = body

        def body(*refs):
            return inner(*refs[:prefetch + n_in], *refs[prefetch + n_in + n_dep:])

        in_specs = list(in_specs) + [ANY] * n_dep
    params = dict(vmem_limit_bytes=VMEM_LIMIT_BYTES)
    if grid:
        params["dimension_semantics"] = ("arbitrary",) * len(grid)
    kw = dict(name=name, out_shape=out_shape, compiler_params=pltpu.CompilerParams(**params),
              input_output_aliases=aliases or {})
    if prefetch:
        kw["grid_spec"] = pltpu.PrefetchScalarGridSpec(num_scalar_prefetch=prefetch, grid=grid, in_specs=in_specs,
                                                       out_specs=out_specs, scratch_shapes=list(scratch))
    else:
        kw.update(grid=grid, in_specs=in_specs, out_specs=out_specs, scratch_shapes=list(scratch))
    call = pl.pallas_call(body, **kw)
    return lambda *args: call(*args, *deps)


def _sds(shape, dtype):
    return jax.ShapeDtypeStruct(tuple(shape), dtype)


def _position():
    x, y, c = lax.axis_index("x"), lax.axis_index("y"), lax.axis_index("c")
    chips = [(1 - x, y), (x, 1 - y), (1 - x, 1 - y)]
    return x, y, c, chips


def _slot(px, py, pc):
    return 4 * px + 2 * py + pc


def _remote(ref_src, ref_dst, send, recv, to):
    return pltpu.make_async_remote_copy(src_ref=ref_src, dst_ref=ref_dst, send_sem=send, recv_sem=recv,
                                        device_id=to, device_id_type=MESH)


def _hbm(arrays):
    return [pltpu.with_memory_space_constraint(a, pltpu.HBM) for a in arrays]


def _split_call(body, name, out_shape, in_specs, out_specs, aliases):
    return pl.pallas_call(body, name=name, out_shape=out_shape, in_specs=in_specs, out_specs=out_specs,
                          input_output_aliases=aliases, compiler_params=pltpu.CompilerParams(has_side_effects=EFFECT))


def _token_shape():
    return _sds((8, 128), F32)


def _gather_start(name, bufs, deps):
    n, nd = len(bufs), len(deps)

    def body(*refs):
        ins, send, recv, token = refs[:n], refs[n + nd], refs[n + nd + 1], refs[2 * n + nd + 2]
        x, y, c, chips = _position()
        peers = [(x, y, 1 - c)] + [(*chip, c) for chip in chips]
        for t in range(n):
            mine = ins[t].at[_slot(x, y, c)]
            for k, peer in enumerate(peers):
                _remote(mine, mine, send.at[4 * t + k], recv.at[4 * t + k], peer).start()
        token[...] = jnp.zeros_like(token)

    res = _split_call(
        body, name,
        (pltpu.SemaphoreType.DMA((4 * n,)), pltpu.SemaphoreType.DMA((4 * n,)), *[pltpu.HBM(b.shape, b.dtype) for b in bufs],
         _token_shape()),
        [HBM] * n + [ANY] * nd, (SEM, SEM, *[HBM] * n, pl.BlockSpec(memory_space=pltpu.VMEM)),
        {t: 2 + t for t in range(n)})(*_hbm(bufs), *deps)
    return res[0], res[1], list(res[2:2 + n]), res[2 + n]


def _gather_forward(name, bufs, send, recv, deps):
    n, nd = len(bufs), len(deps)

    def body(*refs):
        ins, send_in, recv_in = refs[:n], refs[n], refs[n + 1]
        fsend, frecv = refs[n + 2 + nd], refs[n + 3 + nd]
        x, y, c, chips = _position()
        for j, chip in enumerate(chips):
            for t in range(n):
                blk = ins[t].at[_slot(*chip, c)]
                _remote(blk, blk, send_in.at[4 * t + 1 + j], recv_in.at[4 * t + 1 + j], (*chip, c)).wait_recv()
                _remote(blk, blk, fsend.at[3 * t + j], frecv.at[3 * t + j], (x, y, 1 - c)).start()

    res = _split_call(
        body, name,
        (pltpu.SemaphoreType.DMA((3 * n,)), pltpu.SemaphoreType.DMA((3 * n,)), *[pltpu.HBM(b.shape, b.dtype) for b in bufs]),
        [HBM] * n + [SEM, SEM] + [ANY] * nd, (SEM, SEM, *[HBM] * n),
        {t: 2 + t for t in range(n)})(*_hbm(bufs), send, recv, *deps)
    return res[0], res[1], list(res[2:])


def _gather_finish(name, bufs, send, recv, fsend, frecv):
    n = len(bufs)

    def body(*refs):
        ins, send_in, recv_in, fs_in, fr_in = refs[:n], refs[n], refs[n + 1], refs[n + 2], refs[n + 3]
        x, y, c, chips = _position()
        sibling = (x, y, 1 - c)
        peers = [sibling] + [(*chip, c) for chip in chips]
        for t in range(n):
            blk = ins[t].at[_slot(x, y, 1 - c)]
            _remote(blk, blk, send_in.at[4 * t], recv_in.at[4 * t], sibling).wait_recv()
            for j, chip in enumerate(chips):
                blk = ins[t].at[_slot(*chip, 1 - c)]
                _remote(blk, blk, fs_in.at[3 * t + j], fr_in.at[3 * t + j], sibling).wait_recv()
            mine = ins[t].at[_slot(x, y, c)]
            for k, peer in enumerate(peers):
                _remote(mine, mine, send_in.at[4 * t + k], recv_in.at[4 * t + k], peer).wait_send()
            for j, chip in enumerate(chips):
                blk = ins[t].at[_slot(*chip, c)]
                _remote(blk, blk, fs_in.at[3 * t + j], fr_in.at[3 * t + j], sibling).wait_send()

    res = _split_call(
        body, name, tuple(pltpu.HBM(b.shape, b.dtype) for b in bufs),
        [HBM] * n + [SEM] * 4, tuple([HBM] * n), {t: t for t in range(n)})(*_hbm(bufs), send, recv, fsend, frecv)
    return list(res)


def _sibling_start(name, grads, lands, deps):
    n, nd = len(grads), len(deps)

    def body(*refs):
        g_in, l_in = refs[:n], refs[n:2 * n]
        send, recv, token = refs[2 * n + nd], refs[2 * n + nd + 1], refs[4 * n + nd + 2]
        x, y, c, _ = _position()
        for t in range(n):
            for k in range(NCHIP):
                _remote(g_in[t].at[2 * k + (1 - c)], l_in[t].at[k], send.at[NCHIP * t + k], recv.at[NCHIP * t + k],
                        (x, y, 1 - c)).start()
        token[...] = jnp.zeros_like(token)

    both = list(grads) + list(lands)
    res = _split_call(
        body, name,
        (pltpu.SemaphoreType.DMA((NCHIP * n,)), pltpu.SemaphoreType.DMA((NCHIP * n,)),
         *[pltpu.HBM(b.shape, b.dtype) for b in both], _token_shape()),
        [HBM] * (2 * n) + [ANY] * nd, (SEM, SEM, *[HBM] * (2 * n), pl.BlockSpec(memory_space=pltpu.VMEM)),
        {t: 2 + t for t in range(2 * n)})(*_hbm(both), *deps)
    return res[0], res[1], list(res[2:2 + n]), list(res[2 + n:2 + 2 * n]), res[2 + 2 * n]


def _sibling_finish(name, grads, lands, send, recv, deps):
    n, nd = len(grads), len(deps)

    def body(*refs):
        g_in, l_in, send_in, recv_in = refs[:n], refs[n:2 * n], refs[2 * n], refs[2 * n + 1]
        x, y, c, _ = _position()
        for t in range(n):
            for k in range(NCHIP):
                cp = _remote(g_in[t].at[2 * k + (1 - c)], l_in[t].at[k], send_in.at[NCHIP * t + k],
                             recv_in.at[NCHIP * t + k], (x, y, 1 - c))
                cp.wait_send()
                cp.wait_recv()

    both = list(grads) + list(lands)
    res = _split_call(
        body, name, tuple(pltpu.HBM(b.shape, b.dtype) for b in both),
        [HBM] * (2 * n) + [SEM, SEM] + [ANY] * nd, tuple([HBM] * (2 * n)),
        {t: t for t in range(2 * n)})(*_hbm(both), send, recv, *deps)
    return list(res[:n]), list(res[n:])


def _chips_start(name, parts, lands, deps):
    n, nd = len(parts), len(deps)

    def body(*refs):
        p_in, l_in = refs[:n], refs[n:2 * n]
        send, recv, token = refs[2 * n + nd], refs[2 * n + nd + 1], refs[4 * n + nd + 2]
        x, y, c, chips = _position()
        for t in range(n):
            for j, chip in enumerate(chips):
                _remote(p_in[t].at[2 * chip[0] + chip[1]], l_in[t].at[2 * x + y], send.at[3 * t + j], recv.at[3 * t + j],
                        (*chip, c)).start()
        token[...] = jnp.zeros_like(token)

    both = list(parts) + list(lands)
    res = _split_call(
        body, name,
        (pltpu.SemaphoreType.DMA((3 * n,)), pltpu.SemaphoreType.DMA((3 * n,)), *[pltpu.HBM(b.shape, b.dtype) for b in both],
         _token_shape()),
        [HBM] * (2 * n) + [ANY] * nd, (SEM, SEM, *[HBM] * (2 * n), pl.BlockSpec(memory_space=pltpu.VMEM)),
        {t: 2 + t for t in range(2 * n)})(*_hbm(both), *deps)
    return res[0], res[1], list(res[2:2 + n]), list(res[2 + n:2 + 2 * n]), res[2 + 2 * n]


def _chips_finish(name, parts, lands, send, recv, deps):
    n, nd = len(parts), len(deps)

    def body(*refs):
        p_in, l_in, send_in, recv_in = refs[:n], refs[n:2 * n], refs[2 * n], refs[2 * n + 1]
        x, y, c, chips = _position()
        for t in range(n):
            for j, chip in enumerate(chips):
                k = 2 * chip[0] + chip[1]
                _remote(p_in[t].at[k], l_in[t].at[k], send_in.at[3 * t + j], recv_in.at[3 * t + j], (*chip, c)).wait_recv()
                _remote(p_in[t].at[k], l_in[t].at[2 * x + y], send_in.at[3 * t + j], recv_in.at[3 * t + j],
                        (*chip, c)).wait_send()

    both = list(parts) + list(lands)
    res = _split_call(
        body, name, tuple(pltpu.HBM(b.shape, b.dtype) for b in both),
        [HBM] * (2 * n) + [SEM, SEM] + [ANY] * nd, tuple([HBM] * (2 * n)),
        {t: t for t in range(2 * n)})(*_hbm(both), send, recv, *deps)
    return list(res[n:])


def _all_gather(name, shards):
    n = len(shards)

    def body(*refs):
        src, dst = refs[:n], refs[n:2 * n]
        send_sems, recv_sems, local_sems = refs[2 * n:]
        x, y, c, chips = _position()
        me, sibling = (x, y, c), (x, y, 1 - c)

        def copy(t, k, block, to, from_shard=False):
            slot = dst[t].at[_slot(*block)]
            return _remote(src[t] if from_shard else slot, slot, send_sems.at[t, k], recv_sems.at[t, k], to)

        mine = [pltpu.make_async_copy(src[t], dst[t].at[_slot(x, y, c)], local_sems.at[t]) for t in range(n)]
        first, passed = [], []
        for t in range(n):
            mine[t].start()
            first.append(copy(t, 0, me, sibling, True))
            first += [copy(t, 1 + j, me, (*chip, c), True) for j, chip in enumerate(chips)]
        for cp in first:
            cp.start()
        for j, chip in enumerate(chips):
            for t in range(n):
                copy(t, 1 + j, (*chip, c), me).wait_recv()
                fwd = copy(t, 4 + j, (*chip, c), sibling)
                fwd.start()
                passed.append(fwd)
        for t in range(n):
            copy(t, 0, sibling, me).wait_recv()
            for j, chip in enumerate(chips):
                copy(t, 4 + j, (*chip, 1 - c), me).wait_recv()
        for cp in first + passed:
            cp.wait_send()
        for t in range(n):
            mine[t].wait()

    outs = _pcall(
        body, name=name, out_shape=[_sds((NDEV,) + s.shape, s.dtype) for s in shards],
        in_specs=[ANY] * n, out_specs=[ANY] * n,
        scratch=[pltpu.SemaphoreType.DMA((n, 7)), pltpu.SemaphoreType.DMA((n, 7)), pltpu.SemaphoreType.DMA((n,))],
    )(*shards)
    return list(outs)


def _pair_sum(name, grad, recv, where):
    _, r, w = grad.shape
    tr = _row_tile(r, w)
    g4 = grad.reshape(NCHIP, 2, r, w)

    def body(where_ref, g_ref, r_ref, o_ref, own_ref):
        val = (g_ref[...].astype(F32) + r_ref[...].astype(F32)).astype(o_ref.dtype)
        o_ref[...] = val

        @pl.when(pl.program_id(1) == where_ref[1])
        def _():
            own_ref[...] = val

    out = _sds((NCHIP, r, w), grad.dtype)
    return _pcall(
        body, name=name, out_shape=[out, out], grid=(r // tr, NCHIP), prefetch=1,
        in_specs=[pl.BlockSpec((None, None, tr, w), lambda i, k, wr: (k, wr[0], i, 0)),
                  pl.BlockSpec((None, tr, w), lambda i, k, wr: (k, i, 0))],
        out_specs=[pl.BlockSpec((None, tr, w), lambda i, k, wr: (k, i, 0)),
                   pl.BlockSpec((None, tr, w), lambda i, k, wr: (wr[1], i, 0))],
    )(where, g4, recv)


def _row_tile(rows, width, budget=2 * 1024 * 1024):
    best = None
    for t in range(16, rows + 1, 16):
        if rows % t == 0 and t * width * 4 <= budget:
            best = t
    if best is None and rows * width * 4 <= budget:
        best = rows
    assert best is not None, (rows, width)
    return best


def _gemm(name, grid, operands, prods, extras, outs, epilogue, *, nk=1, acc_shape=None, aliases=None, separate=False,
          deps=()):
    n_op, n_ex, n_out = len(operands), len(extras), len(outs)

    def body(*refs):
        ops, ex, out_refs = refs[:n_op], refs[n_op:n_op + n_ex], refs[n_op + n_ex:n_op + n_ex + n_out]
        parts = []
        for pr in prods:
            a, b = ops[pr[0]], ops[pr[1]]
            av = pr[3](a) if len(pr) > 3 and pr[3] else a[...]
            bv = pr[4](b) if len(pr) > 4 and pr[4] else b[...]
            parts.append(lax.dot_general(av, bv, pr[2], preferred_element_type=F32))
        if separate:
            epilogue(parts, ex, out_refs)
            return
        part = parts[0]
        for p in parts[1:]:
            part = part + p
        if nk == 1:
            epilogue(part, ex, out_refs)
        else:
            acc = refs[-1]
            k = pl.program_id(len(grid) - 1)

            @pl.when(k == 0)
            def _():
                acc[...] = part

            @pl.when(k > 0)
            def _():
                acc[...] += part

            @pl.when(k == nk - 1)
            def _():
                epilogue(acc[...], ex, out_refs)

    res = _pcall(
        body, name=name, out_shape=[o[0] for o in outs], grid=grid,
        in_specs=[o[1] for o in operands] + [e[1] for e in extras], out_specs=[o[1] for o in outs],
        scratch=[pltpu.VMEM(acc_shape, F32)] if nk > 1 else [], aliases=aliases, deps=deps,
    )(*[o[0] for o in operands], *[e[0] for e in extras])
    return list(res)


def _store(acc, ex, outs):
    outs[0][...] = acc.astype(outs[0].dtype)


def _store_add_extra(acc, ex, outs):
    v = acc
    for e in ex:
        v = v + e[...]
    outs[0][...] = v.astype(outs[0].dtype)


def _stacked(ref):
    b = ref[...]
    return b.reshape(b.shape[0] * b.shape[1], b.shape[2])


def _pick(c):
    return lambda ref: ref[c]


def _cols(c, width):
    return lambda ref: ref[:, c * width:(c + 1) * width]


def _gelu_parts(z):
    c = math.sqrt(2.0 / math.pi)
    t = jnp.tanh(c * (z + 0.044715 * (z * z * z)))
    val = 0.5 * z * (1.0 + t)
    grad = 0.5 * (1.0 + t) + 0.5 * z * (1.0 - t * t) * (c * (1.0 + 3.0 * 0.044715 * z * z))
    return val, grad


def _rms_fwd(name, h, g, deps=()):
    s, d = h.shape
    tr = _row_tile(s, d)

    def body(h_ref, g_ref, o_ref):
        hv = h_ref[...]
        r = lax.rsqrt(jnp.mean(hv * hv, axis=-1, keepdims=True) + RMS_EPS)
        o_ref[...] = (hv * r * g_ref[...]).astype(o_ref.dtype)

    return _pcall(
        body, name=name, out_shape=_sds((s, d), BF16), grid=(s // tr,),
        in_specs=[pl.BlockSpec((tr, d), lambda i: (i, 0)), pl.BlockSpec((1, d), lambda i: (0, 0))],
        out_specs=pl.BlockSpec((tr, d), lambda i: (i, 0)), deps=deps,
    )(h, g.reshape(1, d))


def _accumulate(ref, val, first):
    @pl.when(first)
    def _():
        ref[...] = val

    @pl.when(jnp.logical_not(first))
    def _():
        ref[...] += val


def _rms_bwd(name, h, g, dy, res, deps=()):
    s, d = h.shape
    tr = _row_tile(s, d, budget=1024 * 1024)

    def body(h_ref, g_ref, dy_ref, res_ref, dh_ref, dhb_ref, dg_ref, cs_ref):
        hv = h_ref[...]
        r = lax.rsqrt(jnp.mean(hv * hv, axis=-1, keepdims=True) + RMS_EPS)
        xhat = hv * r
        dyv = dy_ref[...]
        dxh = dyv * g_ref[...]
        dh = res_ref[...] + r * (dxh - xhat * jnp.mean(dxh * xhat, axis=-1, keepdims=True))
        dh_ref[...] = dh
        dhb_ref[...] = dh.astype(BF16)
        first = pl.program_id(0) == 0
        _accumulate(dg_ref, jnp.sum(dyv * xhat, axis=0, keepdims=True), first)
        _accumulate(cs_ref, jnp.sum(dh, axis=0, keepdims=True), first)

    row = pl.BlockSpec((tr, d), lambda i: (i, 0))
    vec = pl.BlockSpec((1, d), lambda i: (0, 0))
    return _pcall(
        body, name=name, out_shape=[_sds((s, d), F32), _sds((s, d), BF16), _sds((1, d), F32), _sds((1, d), F32)],
        grid=(s // tr,), in_specs=[row, vec, row, row], out_specs=[row, row, vec, vec], deps=deps,
    )(h, g.reshape(1, d), dy, res)


def _loss_bwd(name, h, g, target):
    s, d = h.shape
    tr = _row_tile(s, d, budget=1024 * 1024)

    def body(h_ref, g_ref, t_ref, loss_ref, dh_ref, dhb_ref, dg_ref):
        hv = h_ref[...]
        r = lax.rsqrt(jnp.mean(hv * hv, axis=-1, keepdims=True) + RMS_EPS)
        xhat = hv * r
        diff = xhat * g_ref[...] - t_ref[...]
        part = jnp.sum(jnp.sum(diff * diff, axis=1, keepdims=True), axis=0, keepdims=True) * (0.5 / d)
        dyv = diff * (1.0 / d)
        dxh = dyv * g_ref[...]
        dh = r * (dxh - xhat * jnp.mean(dxh * xhat, axis=-1, keepdims=True))
        dh_ref[...] = dh
        dhb_ref[...] = dh.astype(BF16)
        first = pl.program_id(0) == 0
        _accumulate(loss_ref, part, first)
        _accumulate(dg_ref, jnp.sum(dyv * xhat, axis=0, keepdims=True), first)

    row = pl.BlockSpec((tr, d), lambda i: (i, 0))
    vec = pl.BlockSpec((1, d), lambda i: (0, 0))
    one = pl.BlockSpec((1, 1), lambda i: (0, 0))
    return _pcall(
        body, name=name, out_shape=[_sds((1, 1), F32), _sds((s, d), F32), _sds((s, d), BF16), _sds((1, d), F32)],
        grid=(s // tr,), in_specs=[row, vec, row], out_specs=[one, row, row, vec],
    )(h, g.reshape(1, d), target)


def _tril_mask():
    return lax.broadcasted_iota(jnp.int32, (CHUNK, CHUNK), 0) >= lax.broadcasted_iota(jnp.int32, (CHUNK, CHUNK), 1)


def _gmlp_fwd(name, zp, gv, ws, bst):
    s, d2 = zp.shape
    d = d2 // 2
    gw = d // GROUPS

    def body(zp_ref, gv_ref, ws_ref, bst_ref, o_ref):
        u, _ = _gelu_parts(zp_ref[:, :d])
        v, _ = _gelu_parts(zp_ref[:, d:])
        rv = lax.rsqrt(jnp.mean(v * v, axis=-1, keepdims=True) + RMS_EPS)
        vn = (v * rv * gv_ref[...]).astype(BF16)
        tril = _tril_mask()
        for g in range(GROUPS):
            sl = slice(g * gw, (g + 1) * gw)
            wc = jnp.where(tril, ws_ref[g], 0.0).astype(BF16)
            sg = jnp.dot(wc, vn[:, sl], preferred_element_type=F32) + bst_ref[:, g:g + 1]
            o_ref[:, sl] = (u[:, sl] * sg).astype(o_ref.dtype)

    return _pcall(
        body, name=name, out_shape=_sds((s, d), BF16), grid=(s // CHUNK,),
        in_specs=[pl.BlockSpec((CHUNK, d2), lambda i: (i, 0)), pl.BlockSpec((1, d), lambda i: (0, 0)),
                  pl.BlockSpec((GROUPS, CHUNK, CHUNK), lambda i: (0, 0, 0)),
                  pl.BlockSpec((CHUNK, GROUPS), lambda i: (0, 0))],
        out_specs=pl.BlockSpec((CHUNK, d), lambda i: (i, 0)),
    )(zp, gv, ws, bst)


def _gmlp_bwd(name, zp, dgated, gv, ws, bst):
    s, d2 = zp.shape
    d = d2 // 2
    gw = d // GROUPS

    def body(zp_ref, dg_ref, gv_ref, ws_ref, bst_ref, dzp_ref, dws_ref, dbs_ref, dgv_ref, dvn_ref):
        u, gu = _gelu_parts(zp_ref[:, :d])
        v, gvv = _gelu_parts(zp_ref[:, d:])
        rv = lax.rsqrt(jnp.mean(v * v, axis=-1, keepdims=True) + RMS_EPS)
        vhat = v * rv
        vn = (vhat * gv_ref[...]).astype(BF16)
        tril = _tril_mask()
        first = pl.program_id(0) == 0
        ones = jnp.ones((8, gw), F32)

        @pl.when(first)
        def _():
            dws_ref[...] = jnp.zeros_like(dws_ref)
            dbs_ref[...] = jnp.zeros_like(dbs_ref)

        for g in range(GROUPS):
            sl = slice(g * gw, (g + 1) * gw)
            wc = jnp.where(tril, ws_ref[g], 0.0).astype(BF16)
            sg = jnp.dot(wc, vn[:, sl], preferred_element_type=F32) + bst_ref[:, g:g + 1]
            dgs = dg_ref[:, sl]
            ds = dgs * u[:, sl]
            dsb = ds.astype(BF16)
            dzp_ref[:, sl] = (dgs * sg * gu[:, sl]).astype(dzp_ref.dtype)
            dvn_ref[:, sl] = lax.dot_general(wc, dsb, TN, preferred_element_type=F32)
            dw = lax.dot_general(dsb, vn[:, sl], NT, preferred_element_type=F32)
            dws_ref[g] += jnp.where(tril, dw, 0.0)
            dbs_ref[g] += lax.dot_general(ones, ds, NT, preferred_element_type=F32, precision=lax.Precision.HIGHEST)
        dvn = dvn_ref[...]
        dvh = dvn * gv_ref[...]
        dv = rv * (dvh - vhat * jnp.mean(dvh * vhat, axis=-1, keepdims=True))
        dzp_ref[:, d:] = (dv * gvv).astype(dzp_ref.dtype)
        _accumulate(dgv_ref, jnp.sum(dvn * vhat, axis=0, keepdims=True), first)

    return _pcall(
        body, name=name,
        out_shape=[_sds((s, d2), BF16), _sds((GROUPS, CHUNK, CHUNK), F32), _sds((GROUPS, 8, CHUNK), F32),
                   _sds((1, d), F32)],
        grid=(s // CHUNK,),
        in_specs=[pl.BlockSpec((CHUNK, d2), lambda i: (i, 0)), pl.BlockSpec((CHUNK, d), lambda i: (i, 0)),
                  pl.BlockSpec((1, d), lambda i: (0, 0)), pl.BlockSpec((GROUPS, CHUNK, CHUNK), lambda i: (0, 0, 0)),
                  pl.BlockSpec((CHUNK, GROUPS), lambda i: (0, 0))],
        out_specs=[pl.BlockSpec((CHUNK, d2), lambda i: (i, 0)),
                   pl.BlockSpec((GROUPS, CHUNK, CHUNK), lambda i: (0, 0, 0)),
                   pl.BlockSpec((GROUPS, 8, CHUNK), lambda i: (0, 0, 0)), pl.BlockSpec((1, d), lambda i: (0, 0))],
        scratch=[pltpu.VMEM((CHUNK, d), F32)],
    )(zp, dgated, gv, ws, bst)


def _bucket_table():
    dist = np.arange(BLOCK)[:, None] + BLOCK - np.arange(2 * BLOCK)[None, :]
    in_window = (dist >= 0) & (dist < BLOCK)
    dd = np.clip(dist, 0, None)
    max_exact = N_BUCKETS // 2
    dl = np.maximum(dd, 1).astype(np.float32)
    large = max_exact + (np.log(dl / np.float32(max_exact)) / np.float32(math.log(MAX_DISTANCE / max_exact))
                         * np.float32(N_BUCKETS - max_exact)).astype(np.int32)
    large = np.minimum(large, N_BUCKETS - 1)
    bucket = np.where(dd < max_exact, dd, large)
    return np.where(in_window, bucket, -1).astype(np.int32).reshape(1, -1)


def _bias_table(name, rel_bias_t, buckets):
    nh = rel_bias_t.shape[0]
    p = buckets.shape[1]
    tp = 4096

    def body(rb_ref, bk_ref, o_ref):
        bk = bk_ref[...]
        onehot = (lax.broadcasted_iota(jnp.int32, (N_BUCKETS, tp), 0) == bk).astype(F32)
        val = jnp.dot(rb_ref[...], onehot, preferred_element_type=F32, precision=lax.Precision.HIGHEST)
        o_ref[...] = jnp.where(bk >= 0, val, NEG_INF)

    return _pcall(
        body, name=name, out_shape=_sds((nh, p), F32), grid=(p // tp,),
        in_specs=[pl.BlockSpec((nh, N_BUCKETS), lambda i: (0, 0)), pl.BlockSpec((1, tp), lambda i: (0, i))],
        out_specs=pl.BlockSpec((nh, tp), lambda i: (0, i)),
    )(rel_bias_t, buckets)


def _bias_grad(name, dbiases, buckets):
    nh, p = dbiases[0].shape
    n = len(dbiases)
    tp = 4096

    def body(*refs):
        bk_ref, o_ref = refs[n], refs[n + 1]
        onehot = (lax.broadcasted_iota(jnp.int32, (N_BUCKETS, tp), 0) == bk_ref[...]).astype(F32)
        db = refs[0][...]
        for r in refs[1:n]:
            db = db + r[...]
        part = lax.dot_general(onehot, db, NT, preferred_element_type=F32, precision=lax.Precision.HIGHEST)
        _accumulate(o_ref, part, pl.program_id(0) == 0)

    return _pcall(
        body, name=name, out_shape=_sds((N_BUCKETS, nh), F32), grid=(p // tp,),
        in_specs=[pl.BlockSpec((nh, tp), lambda i: (0, i))] * n + [pl.BlockSpec((1, tp), lambda i: (0, i))],
        out_specs=pl.BlockSpec((N_BUCKETS, nh), lambda i: (0, 0)),
    )(*dbiases, buckets)


def _stack_heads(ref):
    return jnp.concatenate([ref[:, hh * HEAD_DIM:(hh + 1) * HEAD_DIM] for hh in range(KV_GROUP)], axis=0)


def _attn_probs(q, kb, b_ref, s_ref):
    kh, i = pl.program_id(0), pl.program_id(1)
    penalty = jnp.where(i > 0, 0.0, NEG_INF).astype(F32)
    col = lax.broadcasted_iota(jnp.int32, (1, 2 * BLOCK), 1)
    bias = b_ref[...].reshape(KV_GROUP * BLOCK, 2 * BLOCK) + jnp.where(col < BLOCK, penalty, 0.0)
    sink = jnp.concatenate([jnp.full((BLOCK, 1), s_ref[kh * KV_GROUP + hh], F32) for hh in range(KV_GROUP)], axis=0)
    s = lax.dot_general(q, kb, NT, preferred_element_type=F32) * 0.125 + bias
    m = jnp.maximum(jnp.max(s, axis=-1, keepdims=True), sink)
    p = jnp.exp(s - m)
    es = jnp.exp(sink - m)
    inv = 1.0 / (jnp.sum(p, axis=-1, keepdims=True) + es)
    return p * inv, es * inv


def _attn_specs(nkv):
    gq = KV_GROUP * HEAD_DIM
    q_spec = pl.BlockSpec((BLOCK, gq), lambda kh, i: (i, kh))
    prev = pl.BlockSpec((None, BLOCK, HEAD_DIM), lambda kh, i: (kh, jnp.maximum(i - 1, 0), 0))
    cur = pl.BlockSpec((None, BLOCK, HEAD_DIM), lambda kh, i: (kh, i, 0))
    bias = pl.BlockSpec((KV_GROUP, BLOCK, 2 * BLOCK), lambda kh, i: (kh, 0, 0))
    smem = pl.BlockSpec(memory_space=pltpu.SMEM)
    return q_spec, prev, cur, bias, smem


def _prob_specs():
    probs = pl.BlockSpec((None, None, KV_GROUP * BLOCK, 2 * BLOCK), lambda kh, i: (kh, i, 0, 0))
    sink_probs = pl.BlockSpec((None, BLOCK, KV_GROUP), lambda kh, i: (kh, i, 0))
    return probs, sink_probs


def _attn_fwd(name, q, k, v, bias, sinks):
    s, dq = q.shape
    nkv = k.shape[0]
    q_spec, prev, cur, bias_spec, smem = _attn_specs(nkv)
    p_spec, ps_spec = _prob_specs()

    def body(q_ref, kp_ref, kc_ref, vp_ref, vc_ref, b_ref, s_ref, o_ref, p_ref, ps_ref):
        kb = jnp.concatenate([kp_ref[...], kc_ref[...]], axis=0)
        vb = jnp.concatenate([vp_ref[...], vc_ref[...]], axis=0)
        p, ps = _attn_probs(_stack_heads(q_ref), kb, b_ref, s_ref)
        pb = p.astype(BF16)
        p_ref[...] = pb
        o = jnp.dot(pb, vb, preferred_element_type=F32)
        for hh in range(KV_GROUP):
            rows = slice(hh * BLOCK, (hh + 1) * BLOCK)
            o_ref[:, hh * HEAD_DIM:(hh + 1) * HEAD_DIM] = o[rows].astype(o_ref.dtype)
            ps_ref[:, hh:hh + 1] = ps[rows]

    return _pcall(
        body, name=name,
        out_shape=[_sds((s, dq), BF16), _sds((nkv, s // BLOCK, KV_GROUP * BLOCK, 2 * BLOCK), BF16),
                   _sds((nkv, s, KV_GROUP), F32)],
        grid=(nkv, s // BLOCK),
        in_specs=[q_spec, prev, cur, prev, cur, bias_spec, smem], out_specs=[q_spec, p_spec, ps_spec],
    )(q, k, k, v, v, bias, sinks)


def _attn_bwd(name, q, k, v, do, probs, sink_probs):
    s, dq = q.shape
    nkv = k.shape[0]
    gq = KV_GROUP * HEAD_DIM
    q_spec, prev, cur, bias_spec, _ = _attn_specs(nkv)
    p_spec, ps_spec = _prob_specs()

    def body(q_ref, do_ref, kp_ref, kc_ref, vp_ref, vc_ref, p_ref, ps_ref,
             dq_ref, dbq_ref, dkc_ref, dkp_ref, dvc_ref, dvp_ref, dbias_ref, dsink_ref):
        @pl.when(pl.program_id(1) == 0)
        def _():
            dbias_ref[...] = jnp.zeros_like(dbias_ref)
            dsink_ref[...] = jnp.zeros_like(dsink_ref)
            dbq_ref[...] = jnp.zeros_like(dbq_ref)

        kb = jnp.concatenate([kp_ref[...], kc_ref[...]], axis=0)
        vb = jnp.concatenate([vp_ref[...], vc_ref[...]], axis=0)
        q, do = _stack_heads(q_ref), _stack_heads(do_ref)
        pb = p_ref[...]
        p = pb.astype(F32)
        dp = lax.dot_general(do, vb, NT, preferred_element_type=F32)
        delta = jnp.sum(p * dp, axis=-1, keepdims=True)
        ds = p * (dp - delta)
        dsb = ds.astype(BF16)
        dq = jnp.dot(dsb, kb, preferred_element_type=F32) * 0.125
        for hh in range(KV_GROUP):
            sl, rows = slice(hh * HEAD_DIM, (hh + 1) * HEAD_DIM), slice(hh * BLOCK, (hh + 1) * BLOCK)
            dq_ref[:, sl] = dq[rows].astype(dq_ref.dtype)
            dbq_ref[:, sl] += jnp.sum(dq[rows], axis=0, keepdims=True)
            dsink_ref[:, hh:hh + 1] += jnp.sum(-(ps_ref[:, hh:hh + 1] * delta[rows]), axis=0, keepdims=True)
        dkb = lax.dot_general(dsb, q, TN, preferred_element_type=F32) * 0.125
        dvb = lax.dot_general(pb, do, TN, preferred_element_type=F32)
        dkp_ref[...], dkc_ref[...] = dkb[:BLOCK], dkb[BLOCK:]
        dvp_ref[...], dvc_ref[...] = dvb[:BLOCK], dvb[BLOCK:]
        dbias_ref[...] += ds.reshape(KV_GROUP, BLOCK, 2 * BLOCK)

    kv_out = _sds((nkv, s, HEAD_DIM), F32)
    return _pcall(
        body, name=name,
        out_shape=[_sds((s, dq), BF16), _sds((1, dq), F32), kv_out, kv_out, kv_out, kv_out,
                   _sds((nkv * KV_GROUP, BLOCK, 2 * BLOCK), F32), _sds((nkv, 1, KV_GROUP), F32)],
        grid=(nkv, s // BLOCK),
        in_specs=[q_spec, q_spec, prev, cur, prev, cur, p_spec, ps_spec],
        out_specs=[q_spec, pl.BlockSpec((1, gq), lambda kh, i: (0, kh)), cur, cur, cur, cur, bias_spec,
                   pl.BlockSpec((None, 1, KV_GROUP), lambda kh, i: (kh, 0, 0))],
    )(q, do, k, k, v, v, probs, sink_probs)


def _kv_grad(name, parts):
    nkv, s, _ = parts[0][0].shape
    nb = s // BLOCK
    w = 2 * nkv * HEAD_DIM
    n = len(parts)

    def body(*refs):
        o_ref, cs_ref = refs[4 * n], refs[4 * n + 1]
        i = pl.program_id(0)
        keep = jnp.where(i < nb - 1, 1.0, 0.0).astype(F32)

        @pl.when(i == 0)
        def _():
            cs_ref[...] = jnp.zeros_like(cs_ref)

        for which in range(2):
            for hh in range(nkv):
                val = None
                for l in range(n):
                    cur_ref, nxt_ref = refs[4 * l + 2 * which], refs[4 * l + 2 * which + 1]
                    t = cur_ref[hh] + keep * nxt_ref[hh]
                    val = t if val is None else val + t
                sl = slice((which * nkv + hh) * HEAD_DIM, (which * nkv + hh + 1) * HEAD_DIM)
                o_ref[:, sl] = val.astype(o_ref.dtype)
                cs_ref[:, sl] += jnp.sum(val, axis=0, keepdims=True)

    cur = pl.BlockSpec((nkv, BLOCK, HEAD_DIM), lambda i: (0, i, 0))
    nxt = pl.BlockSpec((nkv, BLOCK, HEAD_DIM), lambda i: (0, jnp.minimum(i + 1, nb - 1), 0))
    flat = [a for p in parts for a in p]
    return _pcall(
        body, name=name, out_shape=[_sds((s, w), BF16), _sds((1, w), F32)], grid=(nb,),
        in_specs=[cur, nxt] * (2 * n),
        out_specs=[pl.BlockSpec((BLOCK, w), lambda i: (i, 0)), pl.BlockSpec((1, w), lambda i: (0, 0))],
    )(*flat)


def _adamw_math(w, g, m, v):
    m = ADAM_B1 * m + (1.0 - ADAM_B1) * g
    v = ADAM_B2 * v + (1.0 - ADAM_B2) * (g * g)
    m_hat = m / (1.0 - ADAM_B1 ** ADAM_STEP)
    v_hat = v / (1.0 - ADAM_B2 ** ADAM_STEP)
    delta = -ADAM_LR * (m_hat / (jnp.sqrt(v_hat) + ADAM_EPS) + ADAM_WD * w)
    return delta, m, v


def _adamw_shard(name, w, m, v, parts, row0, layer, prev, deps=()):
    _, r, wd = w.shape
    tr = _row_tile(r, wd, budget=3 * 512 * 1024)
    assert row0 % tr == 0

    def body(w_ref, m_ref, v_ref, p_ref, a0, a1, a2, a3, g_ref, d_ref, nm_ref, nv_ref):
        g = p_ref[0].astype(F32)
        for k in range(1, NCHIP):
            g = g + p_ref[k].astype(F32)
        delta, nm, nv = _adamw_math(w_ref[...], g, m_ref[...], v_ref[...])
        g_ref[...], d_ref[...], nm_ref[...], nv_ref[...] = g, delta, nm, nv

    par = pl.BlockSpec((None, tr, wd), lambda i: (layer, i, 0))
    out = _sds(w.shape, F32)
    return _pcall(
        body, name=name, out_shape=[out, out, out, out], grid=(r // tr,),
        in_specs=[par, par, par, pl.BlockSpec((NCHIP, tr, wd), lambda i: (0, row0 // tr + i, 0)), ANY, ANY, ANY, ANY],
        out_specs=[par, par, par, par], aliases={4: 0, 5: 1, 6: 2, 7: 3}, deps=deps,
    )(w, m, v, parts, *prev)


def _sum_devices(name, gathered):
    _, r, wd = gathered.shape

    def body(g_ref, o_ref):
        acc = g_ref[0]
        for k in range(1, NDEV):
            acc = acc + g_ref[k]
        o_ref[...] = acc

    return _pcall(body, name=name, out_shape=_sds((r, wd), F32), grid=(1,),
                  in_specs=[pl.BlockSpec((NDEV, r, wd), lambda i: (0, 0, 0))],
                  out_specs=pl.BlockSpec((r, wd), lambda i: (0, 0)))(gathered)


def _adamw_flat(name, w, g, m, v):
    shape = w.shape

    def body(w_ref, g_ref, m_ref, v_ref, d_ref, nm_ref, nv_ref):
        d_ref[...], nm_ref[...], nv_ref[...] = _adamw_math(w_ref[...], g_ref[...], m_ref[...], v_ref[...])

    spec = pl.BlockSpec(shape, lambda i: (0, 0))
    out = _sds(shape, F32)
    return _pcall(body, name=name, out_shape=[out, out, out], grid=(1,), in_specs=[spec] * 4,
                  out_specs=[spec] * 3)(w, g, m, v)


def _cast_into(name, src, layer, buf, row0, me):
    _, r, wd = src.shape
    tr = _row_tile(r, wd)
    assert row0 % tr == 0

    def body(me_ref, s_ref, b_ref, o_ref):
        o_ref[...] = s_ref[...].astype(o_ref.dtype)

    return _pcall(
        body, name=name, out_shape=_sds(buf.shape, buf.dtype), grid=(r // tr,), prefetch=1,
        in_specs=[pl.BlockSpec((None, tr, wd), lambda i, mr: (layer, i, 0)), ANY],
        out_specs=pl.BlockSpec((None, tr, wd), lambda i, mr: (mr[0], row0 // tr + i, 0)), aliases={2: 0},
    )(me, src, buf)


def _pack(arrays):
    rows = []
    for a in arrays:
        flat = a.reshape(-1).astype(F32)
        pad = (-flat.shape[0]) % 1024
        rows.append(jnp.pad(flat, (0, pad)).reshape(-1, 128))
    return jnp.concatenate(rows, axis=0)


def _unpack(packed, shapes):
    out, r = [], 0
    for shp in shapes:
        n = int(np.prod(shp))
        nr = (n + 1023) // 1024 * 8
        out.append(packed[r:r + nr].reshape(-1)[:n].reshape(shp))
        r += nr
    return out


def kernel(x, mix_norm, ffn_norm, a_w_in, a_norm_v, a_w_s, a_b_s, a_w_out, kv_norm, w_kv, b_kv, b_w_q, b_b_q, b_sinks, b_w_o, b_b_o, rel_bias, ffn_w_gate, ffn_w_up, ffn_w_down, final_norm, loss_target, m_mix_norm, m_ffn_norm, m_a_w_in, m_a_norm_v, m_a_w_s, m_a_b_s, m_a_w_out, m_kv_norm, m_w_kv, m_b_kv, m_b_w_q, m_b_b_q, m_b_sinks, m_b_w_o, m_b_b_o, m_rel_bias, m_ffn_w_gate, m_ffn_w_up, m_ffn_w_down, m_final_norm, v_mix_norm, v_ffn_norm, v_a_w_in, v_a_norm_v, v_a_w_s, v_a_b_s, v_a_w_out, v_kv_norm, v_w_kv, v_b_kv, v_b_w_q, v_b_b_q, v_b_sinks, v_b_w_o, v_b_b_o, v_rel_bias, v_ffn_w_gate, v_ffn_w_up, v_ffn_w_down, v_final_norm):
    _, S, D = x.shape
    LA, LB, L = a_w_in.shape[0], b_w_q.shape[0], ffn_w_gate.shape[0]
    F = ffn_w_gate.shape[2]
    DS = D // NDEV
    ZC = a_w_in.shape[2]
    KVW = w_kv.shape[1]
    NKV = KVW // (2 * HEAD_DIM)
    NH = D // HEAD_DIM
    assert ZC * NDEV == 2 * D and NH == NKV * KV_GROUP and S % BLOCK == 0
    TM = min(1024, S)
    TN_ = min(1024, D)
    TS = min(512, D)
    KC = 4

    ix, iy, ic = lax.axis_index("x"), lax.axis_index("y"), lax.axis_index("c")
    me = (4 * ix + 2 * iy + ic).astype(jnp.int32)
    me1 = me.reshape(1)
    where = jnp.stack([ic, 2 * ix + iy]).astype(jnp.int32)

    def tr3(a):
        return jnp.transpose(a, (0, 2, 1))

    gate_t, up_t = tr3(ffn_w_gate), tr3(ffn_w_up)
    w_kv3 = w_kv.reshape((1,) + w_kv.shape)

    def layer_arrays(l):
        arrs = [("gu", 2 * F, D, [(gate_t, l, 0), (up_t, l, F)]), ("down", F, D, [(ffn_w_down, l, 0)])]
        if l < LA:
            arrs += [("win", D, ZC, [(a_w_in, l, 0)]), ("wout", DS, D, [(a_w_out, l, 0)])]
            if l == LA - 1:
                arrs.append(("wkv", DS, KVW, [(w_kv3, 0, 0)]))
        else:
            i_b = l - LA
            arrs.append(("wqo", 2 * DS, D, [(b_w_q, i_b, 0), (b_w_o, i_b, DS)]))
        return arrs

    started = []
    token = None
    for l in range(L):
        keys, bufs = [], []
        for key, rows, width, sources in layer_arrays(l):
            buf = lax.empty((NDEV, rows, width), BF16)
            for si, (src, li, row0) in enumerate(sources):
                buf = _cast_into(f"cast_{key}{l}_{si}", src, li, buf, row0, me1)
            keys.append(key)
            bufs.append(buf)
        if l == 0:
            nv_rows = _pack([a_norm_v])
            nv = _cast_into("put_norm_v", nv_rows.reshape((1,) + nv_rows.shape), 0,
                            lax.empty((NDEV,) + nv_rows.shape, F32), 0, me1)
            keys.append("norm_v")
            bufs.append(nv)
        send, recv, bufs, token = _gather_start(f"gather_start{l}", bufs, [] if token is None else [token])
        started.append((keys, bufs, send, recv))

    def finish_gather(l, deps):
        keys, bufs, send, recv = started[l]
        fsend, frecv, bufs = _gather_forward(f"gather_forward{l}", bufs, send, recv, deps)
        bufs = _gather_finish(f"gather_finish{l}", bufs, send, recv, fsend, frecv)
        return dict(zip(keys, bufs))

    buckets = jnp.asarray(_bucket_table())
    bias = _bias_table("bias_table", rel_bias.T, buckets).reshape(NH, BLOCK, 2 * BLOCK)

    def rows_full(tm):
        return pl.BlockSpec((tm, D), lambda i, j: (i, 0))

    def tile(tm, tn):
        return pl.BlockSpec((tm, tn), lambda i, j: (i, j))

    vec_tile = pl.BlockSpec((1, TN_), lambda i, j: (0, j))

    def ffn_forward(l, wl, h_mid, tag):
        xf = _rms_fwd(f"ffn_norm_fwd{tag}", h_mid, ffn_norm[l])

        def ep(parts, ex, outs):
            a, b = parts
            sg = jax.nn.sigmoid(a)
            silu = a * sg
            outs[0][0] = (b * (sg * (1.0 + a * (1.0 - sg)))).astype(BF16)
            outs[0][1] = silu.astype(BF16)
            outs[1][...] = (silu * b).astype(BF16)

        ab, hid = _gemm(
            f"ffn_up{tag}", (S // TM, NDEV),
            [(xf, rows_full(TM)),
             (wl["gu"], pl.BlockSpec((None, F, D), lambda i, e: (e, 0, 0))),
             (wl["gu"], pl.BlockSpec((None, F, D), lambda i, e: (e, 1, 0)))],
            [(0, 1, NT), (0, 2, NT)], [],
            [(_sds((2, NDEV, S, F), BF16), pl.BlockSpec((2, None, TM, F), lambda i, e: (0, e, i, 0))),
             (_sds((NDEV, S, F), BF16), pl.BlockSpec((None, TM, F), lambda i, e: (e, i, 0)))],
            ep, separate=True)
        (h_out,) = _gemm(
            f"ffn_down{tag}", (S // TM, D // TN_, NDEV // KC),
            [(hid, pl.BlockSpec((KC, TM, F), lambda i, j, k: (k, i, 0))),
             (wl["down"], pl.BlockSpec((KC, F, TN_), lambda i, j, k: (k, 0, j)))],
            [(0, 1, NN, _pick(c), _pick(c)) for c in range(KC)],
            [(h_mid, pl.BlockSpec((TM, TN_), lambda i, j, k: (i, j)))],
            [(_sds((S, D), F32), pl.BlockSpec((TM, TN_), lambda i, j, k: (i, j)))],
            _store_add_extra, nk=NDEV // KC, acc_shape=(TM, TN_))
        return dict(h_mid=h_mid, xf=xf, ab=ab, hid=hid), h_out

    def stacked_rows_gemm(name, a, wmat, blk, extras, ep, out_dtype):
        return _gemm(
            name, (S // TM, D // TN_),
            [(a, rows_full(TM)), (wmat, pl.BlockSpec((NDEV, DS, TN_), lambda i, j: (0, blk, j)))],
            [(0, 1, NN, None, _stacked)], extras,
            [(_sds((S, D), out_dtype), tile(TM, TN_))], ep)[0]

    def back_rows_gemm(name, a, wmat, blk, out_dtype, deps=()):
        return _gemm(
            name, (S // TM, NDEV),
            [(a, rows_full(TM)), (wmat, pl.BlockSpec((None, DS, D), lambda i, e: (e, blk, 0)))],
            [(0, 1, NT)], [], [(_sds((S, D), out_dtype), pl.BlockSpec((TM, DS), lambda i, e: (i, e)))], _store,
            deps=deps)[0]

    def grad_rows_gemm(name, act, d_bf, buf, blk):
        return _gemm(
            name, (NDEV,),
            [(act, pl.BlockSpec((S, DS), lambda e: (0, e))), (d_bf, pl.BlockSpec((S, D), lambda e: (0, 0)))],
            [(0, 1, TN)], [(buf, ANY)],
            [(_sds(buf.shape, BF16), pl.BlockSpec((None, DS, D), lambda e: (e, blk, 0)))],
            _store, aliases={2: 0})[0]

    saved, weights = [], []
    h = x.reshape(S, D)
    k_heads = v_heads = hn = h_kv = norm_v = None
    for layer in range(L):
        wl = finish_gather(layer, [token] if layer == 0 else [h])
        weights.append(wl)
        if layer == 0:
            nv_all = wl["norm_v"]
            norm_v = jnp.transpose(nv_all.reshape(NDEV, -1)[:, :LA * DS].reshape(NDEV, LA, DS), (1, 0, 2)).reshape(LA, D)
        sv = dict(h_in=h)
        xn = _rms_fwd(f"mix_norm_fwd{layer}", h, mix_norm[layer])
        sv["xn"] = xn
        if layer < LA:
            i_a = layer
            (zp,) = _gemm(
                f"gmlp_in{layer}", (S // TM, NDEV),
                [(xn, rows_full(TM)), (wl["win"], pl.BlockSpec((None, D, ZC), lambda i, e: (e, 0, 0)))],
                [(0, 1, NN)], [], [(_sds((S, 2 * D), F32), pl.BlockSpec((TM, ZC), lambda i, e: (i, e)))], _store)
            bst = a_b_s[i_a].T
            gated = _gmlp_fwd(f"gmlp_gate{layer}", zp, norm_v[i_a].reshape(1, D), a_w_s[i_a], bst)
            sv.update(zp=zp, gated=gated, bst=bst)
            h_mid = stacked_rows_gemm(f"gmlp_out{layer}", gated, wl["wout"], 0, [(h, tile(TM, TN_))],
                                      _store_add_extra, F32)
        else:
            i_b = layer - LA
            q = stacked_rows_gemm(f"attn_q{layer}", xn, wl["wqo"], 0, [(b_b_q[i_b].reshape(1, D), vec_tile)],
                                  _store_add_extra, BF16)
            attn, probs, sink_probs = _attn_fwd(f"attn_fwd{layer}", q, k_heads, v_heads, bias, b_sinks[i_b])
            sv.update(q=q, attn=attn, probs=probs, sink_probs=sink_probs)
            h_mid = stacked_rows_gemm(f"attn_o{layer}", attn, wl["wqo"], 1,
                                      [(h, tile(TM, TN_)), (b_b_o[i_b].reshape(1, D), vec_tile)],
                                      _store_add_extra, F32)
        fsv, h = ffn_forward(layer, wl, h_mid, str(layer))
        sv.update(fsv)
        saved.append(sv)
        if layer == LA - 1:
            h_kv = h
            hn = _rms_fwd("kv_norm_fwd", h, kv_norm)

            def kv_ep(acc, ex, outs):
                val = acc + ex[0][...]
                for hh in range(NKV):
                    outs[0][hh] = val[:, hh * HEAD_DIM:(hh + 1) * HEAD_DIM].astype(BF16)
                    outs[1][hh] = val[:, (NKV + hh) * HEAD_DIM:(NKV + hh + 1) * HEAD_DIM].astype(BF16)

            k_heads, v_heads = _gemm(
                "kv_proj", (S // TM,),
                [(hn, pl.BlockSpec((TM, D), lambda i: (i, 0))),
                 (wl["wkv"], pl.BlockSpec((NDEV, DS, KVW), lambda i: (0, 0, 0)))],
                [(0, 1, NN, None, _stacked)], [(b_kv.reshape(1, KVW), pl.BlockSpec((1, KVW), lambda i: (0, 0)))],
                [(_sds((NKV, S, HEAD_DIM), BF16), pl.BlockSpec((NKV, TM, HEAD_DIM), lambda i: (0, i, 0)))] * 2,
                kv_ep)

    loss11, d, d_bf, g_final = _loss_bwd("loss_bwd", h, final_norm, loss_target.reshape(S, D))
    loss = lax.psum(loss11[0, 0], AXES)

    g_mix, g_ffn = [None] * L, [None] * L
    g_ws, g_bs, g_nv = [None] * LA, [None] * LA, [None] * LA
    g_bq, g_sink, g_bo = [None] * LB, [None] * LB, [None] * LB
    dbiases = []
    kv_parts = []
    g_kvn = g_bkv = None
    exchanges = [[] for _ in range(L)]
    pending = None
    grads_wkv = None
    newest = []

    def new_grads(l):
        return {key: lax.empty((NDEV, rows, width), BF16) for key, rows, width, _ in layer_arrays(l)}

    def exchange_begin(tag, l, gl, keys):
        grads = [gl[k] for k in keys]
        lands = [lax.empty((NCHIP,) + g.shape[1:], BF16) for g in grads]
        send, recv, grads, lands, tok = _sibling_start(f"rs_sibling_start{tag}", grads, lands, [])
        newest[:] = [tok]
        return dict(tag=tag, layer=l, keys=keys, grads=grads, lands=lands, send=send, recv=recv)

    def exchange_middle(st, dep):
        tag = st["tag"]
        grads, lands = _sibling_finish(f"rs_sibling_finish{tag}", st["grads"], st["lands"], st["send"], st["recv"], [dep])
        sums, own = [], []
        for t, key in enumerate(st["keys"]):
            s_, o_ = _pair_sum(f"pair_sum_{key}{tag}", grads[t], lands[t], where)
            sums.append(s_)
            own.append(o_)
        send, recv, sums, own, tok = _chips_start(f"rs_chips_start{tag}", sums, own, [])
        newest[:] = [tok]
        st.update(sums=sums, own=own, send2=send, recv2=recv)
        exchanges[st["layer"]].append(st)

    def exchange_end(st, dep):
        lands = _chips_finish(f"rs_chips_finish{st['tag']}", st["sums"], st["own"], st["send2"], st["recv2"], [dep])
        return dict(zip(st["keys"], lands))

    for layer in reversed(range(L)):
        sv, wl = saved[layer], weights[layer]
        tag = str(layer)
        gl = new_grads(layer)
        if grads_wkv is not None and layer == LA - 1:
            gl["wkv"] = grads_wkv
        def dhid_ep(acc, ex, outs):
            outs[0][0] = (acc * ex[0][0].astype(F32)).astype(BF16)
            outs[0][1] = (acc * ex[0][1].astype(F32)).astype(BF16)

        ab_spec = pl.BlockSpec((2, None, TM, F), lambda i, e: (0, e, i, 0))
        (dab,) = _gemm(
            f"ffn_dhid{tag}", (S // TM, NDEV),
            [(d_bf, rows_full(TM)), (wl["down"], pl.BlockSpec((None, F, D), lambda i, e: (e, 0, 0)))],
            [(0, 1, NT)], [(sv["ab"], ab_spec)], [(_sds((2, NDEV, S, F), BF16), ab_spec)], dhid_ep,
            deps=list(newest))
        if pending:
            exchange_middle(pending, dab)
        (gl["down"],) = _gemm(
            f"ffn_dwdown{tag}", (NDEV,),
            [(sv["hid"], pl.BlockSpec((None, S, F), lambda e: (e, 0, 0))),
             (d_bf, pl.BlockSpec((S, D), lambda e: (0, 0)))],
            [(0, 1, TN)], [(gl["down"], ANY)],
            [(_sds(gl["down"].shape, BF16), pl.BlockSpec((None, F, D), lambda e: (e, 0, 0)))],
            _store, aliases={2: 0}, deps=list(newest))
        (gl["gu"],) = _gemm(
            f"ffn_dwup{tag}", (2, NDEV),
            [(dab, pl.BlockSpec((None, None, S, F), lambda w, e: (w, e, 0, 0))),
             (sv["xf"], pl.BlockSpec((S, D), lambda w, e: (0, 0)))],
            [(0, 1, TN)], [(gl["gu"], ANY)],
            [(_sds(gl["gu"].shape, BF16), pl.BlockSpec((None, F, D), lambda w, e: (e, w, 0)))],
            _store, aliases={2: 0})
        ffn_group = exchange_begin(f"_ffn{tag}", layer, gl, ["gu", "down"])
        (dxf,) = _gemm(
            f"ffn_dx{tag}", (S // TM, D // TN_, 2 * NDEV // KC),
            [(dab.reshape(2 * NDEV // KC, KC, S, F), pl.BlockSpec((None, KC, TM, F), lambda i, j, k: (k, 0, i, 0))),
             (wl["gu"], pl.BlockSpec((KC, F, TN_), lambda i, j, k: (k % (NDEV // KC), k // (NDEV // KC), j)))],
            [(0, 1, NN, _pick(c), _pick(c)) for c in range(KC)], [],
            [(_sds((S, D), F32), pl.BlockSpec((TM, TN_), lambda i, j, k: (i, j)))],
            _store, nk=2 * NDEV // KC, acc_shape=(TM, TN_), deps=list(newest))
        exchange_middle(ffn_group, dxf)
        d, d_bf, g_ffn[layer], colsum = _rms_bwd(f"ffn_norm_bwd{tag}", sv["h_mid"], ffn_norm[layer], dxf, d,
                                                 deps=list(newest))
        if layer < LA:
            i_a = layer
            dgated = back_rows_gemm(f"gmlp_dgated{tag}", d_bf, wl["wout"], 0, F32)
            gl["wout"] = grad_rows_gemm(f"gmlp_dwout{tag}", sv["gated"], d_bf, gl["wout"], 0)
            dzp, g_ws[i_a], dbs, g_nv[i_a] = _gmlp_bwd(f"gmlp_bwd{tag}", sv["zp"], dgated,
                                                       norm_v[i_a].reshape(1, D), a_w_s[i_a], sv["bst"])
            g_bs[i_a] = dbs[:, 0, :]
            (gl["win"],) = _gemm(
                f"gmlp_dwin{tag}", (NDEV, D // TS),
                [(sv["xn"], pl.BlockSpec((S, TS), lambda e, i: (0, i))),
                 (dzp, pl.BlockSpec((S, ZC), lambda e, i: (0, e)))],
                [(0, 1, TN)], [(gl["win"], ANY)],
                [(_sds(gl["win"].shape, BF16), pl.BlockSpec((None, TS, ZC), lambda e, i: (e, i, 0)))],
                _store, aliases={2: 0})
            (dxn,) = _gemm(
                f"gmlp_dx{tag}", (S // TM, D // TN_, NDEV // KC),
                [(dzp, pl.BlockSpec((TM, KC * ZC), lambda i, j, k: (i, k))),
                 (wl["win"], pl.BlockSpec((KC, TN_, ZC), lambda i, j, k: (k, j, 0)))],
                [(0, 1, NT, _cols(c, ZC), _pick(c)) for c in range(KC)], [],
                [(_sds((S, D), F32), pl.BlockSpec((TM, TN_), lambda i, j, k: (i, j)))],
                _store, nk=NDEV // KC, acc_shape=(TM, TN_))
        else:
            i_b = layer - LA
            g_bo[i_b] = colsum
            dattn = back_rows_gemm(f"attn_dout{tag}", d_bf, wl["wqo"], 1, BF16)
            gl["wqo"] = grad_rows_gemm(f"attn_dwo{tag}", sv["attn"], d_bf, gl["wqo"], 1)
            dq, g_bq[i_b], dkc, dkp, dvc, dvp, dbias, dsink = _attn_bwd(
                f"attn_bwd{tag}", sv["q"], k_heads, v_heads, dattn, sv["probs"], sv["sink_probs"])
            kv_parts.append((dkc, dkp, dvc, dvp))
            g_sink[i_b] = dsink.reshape(NH)
            dbiases.append(dbias.reshape(NH, BLOCK * 2 * BLOCK))
            gl["wqo"] = grad_rows_gemm(f"attn_dwq{tag}", sv["xn"], dq, gl["wqo"], 0)
            dxn = back_rows_gemm(f"attn_dx{tag}", dq, wl["wqo"], 0, F32)
        d, d_bf, g_mix[layer], _ = _rms_bwd(f"mix_norm_bwd{tag}", sv["h_in"], mix_norm[layer], dxn, d)
        pending = exchange_begin(f"_mix{tag}", layer, gl, [k for k in gl if k not in ("gu", "down")])
        if layer == LA:
            wkv = weights[LA - 1]["wkv"]
            dkv, g_bkv = _kv_grad("kv_grad", kv_parts)
            (grads_wkv,) = _gemm(
                "kv_dw", (NDEV,),
                [(hn, pl.BlockSpec((S, DS), lambda e: (0, e))), (dkv, pl.BlockSpec((S, KVW), lambda e: (0, 0)))],
                [(0, 1, TN)], [(lax.empty((NDEV, DS, KVW), BF16), ANY)],
                [(_sds((NDEV, DS, KVW), BF16), pl.BlockSpec((None, DS, KVW), lambda e: (e, 0, 0)))],
                _store, aliases={2: 0}, deps=list(newest))
            (dhn,) = _gemm(
                "kv_dx", (S // TM, NDEV),
                [(dkv, pl.BlockSpec((TM, KVW), lambda i, e: (i, 0))),
                 (wkv, pl.BlockSpec((None, DS, KVW), lambda i, e: (e, 0, 0)))],
                [(0, 1, NT)], [], [(_sds((S, D), F32), pl.BlockSpec((TM, DS), lambda i, e: (i, e)))], _store)
            d, d_bf, g_kvn, _ = _rms_bwd("kv_norm_bwd", h_kv, kv_norm, dhn, d)
    grad_x = d.reshape(x.shape)

    exchange_middle(pending, d)

    g_rel = _bias_grad("bias_grad", dbiases, buckets)
    small_local = _pack([jnp.concatenate(g_mix, axis=0), jnp.concatenate(g_ffn, axis=0), jnp.stack(g_ws),
                         jnp.stack(g_bs), g_kvn, g_bkv, jnp.concatenate(g_bq, axis=0), jnp.stack(g_sink),
                         jnp.concatenate(g_bo, axis=0), g_rel, g_final, jnp.concatenate(g_nv, axis=0)])
    small_slot = _cast_into("put_small_grads", small_local.reshape((1,) + small_local.shape), 0,
                            lax.empty((NDEV,) + small_local.shape, F32), 0, me1)
    s_send, s_recv, s_bufs, s_tok = _gather_start("gather_small_start", [small_slot], list(newest))

    results = {}
    after = [s_tok]

    def upd(pname, w, m, v, l, li, lands, row0):
        w3 = w if w.ndim == 3 else w.reshape((1,) + w.shape)
        prev = results.get(pname) or [lax.empty(w3.shape, F32) for _ in range(4)]
        results[pname] = _adamw_shard(f"adamw_{pname}{l}", w3, m.reshape(w3.shape), v.reshape(w3.shape), lands,
                                      row0, li, prev, deps=list(after))
        after[:] = [results[pname][0]]

    for l in reversed(range(L)):
        for st in exchanges[l]:
            lands = exchange_end(st, after[0])
            if "gu" in lands:
                upd("ffn_w_gate", gate_t, tr3(m_ffn_w_gate), tr3(v_ffn_w_gate), l, l, lands["gu"], 0)
                upd("ffn_w_up", up_t, tr3(m_ffn_w_up), tr3(v_ffn_w_up), l, l, lands["gu"], F)
                upd("ffn_w_down", ffn_w_down, m_ffn_w_down, v_ffn_w_down, l, l, lands["down"], 0)
            if "win" in lands:
                upd("a_w_in", a_w_in, m_a_w_in, v_a_w_in, l, l, lands["win"], 0)
                upd("a_w_out", a_w_out, m_a_w_out, v_a_w_out, l, l, lands["wout"], 0)
            if "wkv" in lands:
                upd("w_kv", w_kv, m_w_kv, v_w_kv, l, 0, lands["wkv"], 0)
            if "wqo" in lands:
                upd("b_w_q", b_w_q, m_b_w_q, v_b_w_q, l, l - LA, lands["wqo"], 0)
                upd("b_w_o", b_w_o, m_b_w_o, v_b_w_o, l, l - LA, lands["wqo"], DS)
    for pname in ("ffn_w_gate", "ffn_w_up"):
        results[pname] = [tr3(r) for r in results[pname]]
    results["w_kv"] = [r.reshape(w_kv.shape) for r in results["w_kv"]]

    small_w = [mix_norm, ffn_norm, a_w_s, a_b_s, kv_norm, b_kv, b_b_q, b_sinks, b_b_o, rel_bias, final_norm]
    small_m = [m_mix_norm, m_ffn_norm, m_a_w_s, m_a_b_s, m_kv_norm, m_b_kv, m_b_b_q, m_b_sinks, m_b_b_o, m_rel_bias,
               m_final_norm]
    small_v = [v_mix_norm, v_ffn_norm, v_a_w_s, v_a_b_s, v_kv_norm, v_b_kv, v_b_b_q, v_b_sinks, v_b_b_o, v_rel_bias,
               v_final_norm]
    shapes = [w.shape for w in small_w] + [(LA, D)]
    s_fsend, s_frecv, s_bufs = _gather_forward("gather_small_forward", s_bufs, s_send, s_recv, list(after))
    (small_all,) = _gather_finish("gather_small_finish", s_bufs, s_send, s_recv, s_fsend, s_frecv)
    small_sum = _sum_devices("sum_small_grads", small_all)
    small_g = _unpack(small_sum, shapes)
    g_normv = lax.dynamic_slice_in_dim(small_g[-1], me * DS, DS, axis=1)
    small_g = small_g[:-1] + [g_normv]
    small_w, small_m, small_v = small_w + [a_norm_v], small_m + [m_a_norm_v], small_v + [v_a_norm_v]
    shapes = [w.shape for w in small_w]
    s_delta, s_m, s_v = _adamw_flat("adamw_small", _pack(small_w), _pack(small_g), _pack(small_m), _pack(small_v))
    s_delta, s_m, s_v = _unpack(s_delta, shapes), _unpack(s_m, shapes), _unpack(s_v, shapes)

    names = ["mix_norm", "ffn_norm", "a_w_in", "a_norm_v", "a_w_s", "a_b_s", "a_w_out", "kv_norm", "w_kv", "b_kv",
             "b_w_q", "b_b_q", "b_sinks", "b_w_o", "b_b_o", "rel_bias", "ffn_w_gate", "ffn_w_up", "ffn_w_down",
             "final_norm"]
    small_names = ["mix_norm", "ffn_norm", "a_w_s", "a_b_s", "kv_norm", "b_kv", "b_b_q", "b_sinks", "b_b_o", "rel_bias",
                   "final_norm", "a_norm_v"]
    res = {}
    for idx, nm in enumerate(small_names):
        res[nm] = (small_g[idx].reshape(shapes[idx]), s_delta[idx], s_m[idx], s_v[idx])
    for nm, u in results.items():
        res[nm] = tuple(u)
    out = [loss, grad_x]
    for part in range(4):
        out += [res[nm][part] for nm in names]
    return tuple(out)
```

```python
import math

import numpy as np
import jax
import jax.numpy as jnp
from jax import lax
from jax.experimental import pallas as pl
from jax.experimental.pallas import tpu as pltpu

F32 = jnp.float32
BF16 = jnp.bfloat16
AXES = ("x", "y", "c")
NDEV = 8
NCHIP = 4
CHUNK = 128
GROUPS = 8
HEAD_DIM = 64
KV_GROUP = 8
BLOCK = 128
N_BUCKETS = 32
MAX_DISTANCE = 128
RMS_EPS = 1e-5
NEG_INF = -1e30
ADAM_LR, ADAM_B1, ADAM_B2, ADAM_EPS, ADAM_WD, ADAM_STEP = 0.001, 0.9, 0.999, 1e-08, 0.01, 10
VMEM_LIMIT_BYTES = 56 * 1024 * 1024

NN = (((1,), (0,)), ((), ()))
NT = (((1,), (1,)), ((), ()))
TN = (((0,), (0,)), ((), ()))
ANY = pl.BlockSpec(memory_space=pl.ANY)
HBM = pl.BlockSpec(memory_space=pltpu.HBM)
SEM = pl.BlockSpec(memory_space=pltpu.SEMAPHORE)
MESH = pl.DeviceIdType.MESH
EFFECT = pltpu.SideEffectType.DATAFLOW_SIDE_EFFECTING


def _pcall(body, *, name, out_shape, in_specs, out_specs, grid=(), scratch=(), aliases=None, prefetch=0, deps=()):
    n_in, n_dep = len(in_specs), len(deps)
    if n_dep:
        inner = body

        def body(*refs):
            return inner(*refs[:prefetch + n_in], *refs[prefetch + n_in + n_dep:])

        in_specs = list(in_specs) + [ANY] * n_dep
    params = dict(vmem_limit_bytes=VMEM_LIMIT_BYTES)
    if grid:
        params["dimension_semantics"] = ("arbitrary",) * len(grid)
    kw = dict(name=name, out_shape=out_shape, compiler_params=pltpu.CompilerParams(**params),
              input_output_aliases=aliases or {})
    if prefetch:
        kw["grid_spec"] = pltpu.PrefetchScalarGridSpec(num_scalar_prefetch=prefetch, grid=grid, in_specs=in_specs,
                                                       out_specs=out_specs, scratch_shapes=list(scratch))
    else:
        kw.update(grid=grid, in_specs=in_specs, out_specs=out_specs, scratch_shapes=list(scratch))
    call = pl.pallas_call(body, **kw)
    return lambda *args: call(*args, *deps)


def _sds(shape, dtype):
    return jax.ShapeDtypeStruct(tuple(shape), dtype)


def _position():
    x, y, c = lax.axis_index("x"), lax.axis_index("y"), lax.axis_index("c")
    chips = [(1 - x, y), (x, 1 - y), (1 - x, 1 - y)]
    return x, y, c, chips


def _slot(px, py, pc):
    return 4 * px + 2 * py + pc


def _remote(ref_src, ref_dst, send, recv, to):
    return pltpu.make_async_remote_copy(src_ref=ref_src, dst_ref=ref_dst, send_sem=send, recv_sem=recv,
                                        device_id=to, device_id_type=MESH)


def _hbm(arrays):
    return [pltpu.with_memory_space_constraint(a, pltpu.HBM) for a in arrays]


def _split_call(body, name, out_shape, in_specs, out_specs, aliases):
    return pl.pallas_call(body, name=name, out_shape=out_shape, in_specs=in_specs, out_specs=out_specs,
                          input_output_aliases=aliases, compiler_params=pltpu.CompilerParams(has_side_effects=EFFECT))


def _token_shape():
    return _sds((8, 128), F32)


def _gather_start(name, bufs, deps):
    n, nd = len(bufs), len(deps)

    def body(*refs):
        ins, send, recv, token = refs[:n], refs[n + nd], refs[n + nd + 1], refs[2 * n + nd + 2]
        x, y, c, chips = _position()
        peers = [(x, y, 1 - c)] + [(*chip, c) for chip in chips]
        for t in range(n):
            mine = ins[t].at[_slot(x, y, c)]
            for k, peer in enumerate(peers):
                _remote(mine, mine, send.at[4 * t + k], recv.at[4 * t + k], peer).start()
        token[...] = jnp.zeros_like(token)

    res = _split_call(
        body, name,
        (pltpu.SemaphoreType.DMA((4 * n,)), pltpu.SemaphoreType.DMA((4 * n,)), *[pltpu.HBM(b.shape, b.dtype) for b in bufs],
         _token_shape()),
        [HBM] * n + [ANY] * nd, (SEM, SEM, *[HBM] * n, pl.BlockSpec(memory_space=pltpu.VMEM)),
        {t: 2 + t for t in range(n)})(*_hbm(bufs), *deps)
    return res[0], res[1], list(res[2:2 + n]), res[2 + n]


def _gather_forward(name, bufs, send, recv, deps):
    n, nd = len(bufs), len(deps)

    def body(*refs):
        ins, send_in, recv_in = refs[:n], refs[n], refs[n + 1]
        fsend, frecv = refs[n + 2 + nd], refs[n + 3 + nd]
        x, y, c, chips = _position()
        for j, chip in enumerate(chips):
            for t in range(n):
                blk = ins[t].at[_slot(*chip, c)]
                _remote(blk, blk, send_in.at[4 * t + 1 + j], recv_in.at[4 * t + 1 + j], (*chip, c)).wait_recv()
                _remote(blk, blk, fsend.at[3 * t + j], frecv.at[3 * t + j], (x, y, 1 - c)).start()

    res = _split_call(
        body, name,
        (pltpu.SemaphoreType.DMA((3 * n,)), pltpu.SemaphoreType.DMA((3 * n,)), *[pltpu.HBM(b.shape, b.dtype) for b in bufs]),
        [HBM] * n + [SEM, SEM] + [ANY] * nd, (SEM, SEM, *[HBM] * n),
        {t: 2 + t for t in range(n)})(*_hbm(bufs), send, recv, *deps)
    return res[0], res[1], list(res[2:])


def _gather_finish(name, bufs, send, recv, fsend, frecv):
    n = len(bufs)

    def body(*refs):
        ins, send_in, recv_in, fs_in, fr_in = refs[:n], refs[n], refs[n + 1], refs[n + 2], refs[n + 3]
        x, y, c, chips = _position()
        sibling = (x, y, 1 - c)
        peers = [sibling] + [(*chip, c) for chip in chips]
        for t in range(n):
            blk = ins[t].at[_slot(x, y, 1 - c)]
            _remote(blk, blk, send_in.at[4 * t], recv_in.at[4 * t], sibling).wait_recv()
            for j, chip in enumerate(chips):
                blk = ins[t].at[_slot(*chip, 1 - c)]
                _remote(blk, blk, fs_in.at[3 * t + j], fr_in.at[3 * t + j], sibling).wait_recv()
            mine = ins[t].at[_slot(x, y, c)]
            for k, peer in enumerate(peers):
                _remote(mine, mine, send_in.at[4 * t + k], recv_in.at[4 * t + k], peer).wait_send()
            for j, chip in enumerate(chips):
                blk = ins[t].at[_slot(*chip, c)]
                _remote(blk, blk, fs_in.at[3 * t + j], fr_in.at[3 * t + j], sibling).wait_send()

    res = _split_call(
        body, name, tuple(pltpu.HBM(b.shape, b.dtype) for b in bufs),
        [HBM] * n + [SEM] * 4, tuple([HBM] * n), {t: t for t in range(n)})(*_hbm(bufs), send, recv, fsend, frecv)
    return list(res)


def _halves(ref):
    rows = ref.shape[0] // 2
    return ref.at[pl.ds(0, rows)], ref.at[pl.ds(rows, rows)]


def _relay_start(name, bufs, deps):
    n, nd = len(bufs), len(deps)

    def body(*refs):
        ins, send, recv, token = refs[:n], refs[n + nd], refs[n + nd + 1], refs[2 * n + nd + 2]
        x, y, c, _ = _position()
        peers = [(x, y, 1 - c), (1 - x, y, c), (x, 1 - y, c)]
        for t in range(n):
            mine = ins[t].at[_slot(x, y, c)]
            for k, peer in enumerate(peers):
                _remote(mine, mine, send.at[3 * t + k], recv.at[3 * t + k], peer).start()
        token[...] = jnp.zeros_like(token)

    res = _split_call(
        body, name,
        (pltpu.SemaphoreType.DMA((3 * n,)), pltpu.SemaphoreType.DMA((3 * n,)), *[pltpu.HBM(b.shape, b.dtype) for b in bufs],
         _token_shape()),
        [HBM] * n + [ANY] * nd, (SEM, SEM, *[HBM] * n, pl.BlockSpec(memory_space=pltpu.VMEM)),
        {t: 2 + t for t in range(n)})(*_hbm(bufs), *deps)
    return res[0], res[1], list(res[2:2 + n]), res[2 + n]


def _relay_neighbors(name, bufs, send, recv, deps):
    n, nd = len(bufs), len(deps)

    def body(*refs):
        ins, send_in, recv_in = refs[:n], refs[n], refs[n + 1]
        fsend, frecv, token = refs[n + 2 + nd], refs[n + 3 + nd], refs[2 * n + 4 + nd]
        x, y, c, _ = _position()
        sibling, xn, yn = (x, y, 1 - c), (1 - x, y, c), (x, 1 - y, c)
        for t in range(n):
            blk = ins[t].at[_slot(*xn)]
            _remote(blk, blk, send_in.at[3 * t + 1], recv_in.at[3 * t + 1], xn).wait_recv()
            _remote(blk, blk, fsend.at[4 * t], frecv.at[4 * t], sibling).start()
            half = _halves(blk)[0]
            _remote(half, half, fsend.at[4 * t + 1], frecv.at[4 * t + 1], yn).start()
        for t in range(n):
            blk = ins[t].at[_slot(*yn)]
            _remote(blk, blk, send_in.at[3 * t + 2], recv_in.at[3 * t + 2], yn).wait_recv()
            _remote(blk, blk, fsend.at[4 * t + 2], frecv.at[4 * t + 2], sibling).start()
            half = _halves(blk)[1]
            _remote(half, half, fsend.at[4 * t + 3], frecv.at[4 * t + 3], xn).start()
        token[...] = jnp.zeros_like(token)

    res = _split_call(
        body, name,
        (pltpu.SemaphoreType.DMA((4 * n,)), pltpu.SemaphoreType.DMA((4 * n,)), *[pltpu.HBM(b.shape, b.dtype) for b in bufs],
         _token_shape()),
        [HBM] * n + [SEM, SEM] + [ANY] * nd, (SEM, SEM, *[HBM] * n, pl.BlockSpec(memory_space=pltpu.VMEM)),
        {t: 2 + t for t in range(n)})(*_hbm(bufs), send, recv, *deps)
    return res[0], res[1], list(res[2:2 + n]), res[2 + n]


def _relay_diagonal(name, bufs, fsend, frecv, deps):
    n, nd = len(bufs), len(deps)

    def body(*refs):
        ins, fs_in, fr_in = refs[:n], refs[n], refs[n + 1]
        gsend, grecv = refs[n + 2 + nd], refs[n + 3 + nd]
        x, y, c, _ = _position()
        for t in range(n):
            blk = ins[t].at[_slot(1 - x, 1 - y, c)]
            first, second = _halves(blk)
            _remote(first, first, fs_in.at[4 * t + 1], fr_in.at[4 * t + 1], (x, 1 - y, c)).wait_recv()
            _remote(second, second, fs_in.at[4 * t + 3], fr_in.at[4 * t + 3], (1 - x, y, c)).wait_recv()
            _remote(blk, blk, gsend.at[t], grecv.at[t], (x, y, 1 - c)).start()

    res = _split_call(
        body, name,
        (pltpu.SemaphoreType.DMA((n,)), pltpu.SemaphoreType.DMA((n,)), *[pltpu.HBM(b.shape, b.dtype) for b in bufs]),
        [HBM] * n + [SEM, SEM] + [ANY] * nd, (SEM, SEM, *[HBM] * n),
        {t: 2 + t for t in range(n)})(*_hbm(bufs), fsend, frecv, *deps)
    return res[0], res[1], list(res[2:])


def _relay_finish(name, bufs, send, recv, fsend, frecv, gsend, grecv):
    n = len(bufs)

    def body(*refs):
        ins = refs[:n]
        send_in, recv_in, fs_in, fr_in, gs_in, gr_in = refs[n:n + 6]
        x, y, c, _ = _position()
        sibling, xn, yn = (x, y, 1 - c), (1 - x, y, c), (x, 1 - y, c)
        for t in range(n):
            blk = ins[t].at[_slot(x, y, 1 - c)]
            _remote(blk, blk, send_in.at[3 * t], recv_in.at[3 * t], sibling).wait_recv()
            blk = ins[t].at[_slot(1 - x, y, 1 - c)]
            _remote(blk, blk, fs_in.at[4 * t], fr_in.at[4 * t], sibling).wait_recv()
            blk = ins[t].at[_slot(x, 1 - y, 1 - c)]
            _remote(blk, blk, fs_in.at[4 * t + 2], fr_in.at[4 * t + 2], sibling).wait_recv()
            blk = ins[t].at[_slot(1 - x, 1 - y, 1 - c)]
            _remote(blk, blk, gs_in.at[t], gr_in.at[t], sibling).wait_recv()
            mine = ins[t].at[_slot(x, y, c)]
            for k, peer in enumerate([sibling, xn, yn]):
                _remote(mine, mine, send_in.at[3 * t + k], recv_in.at[3 * t + k], peer).wait_send()
            bx, by = ins[t].at[_slot(*xn)], ins[t].at[_slot(*yn)]
            _remote(bx, bx, fs_in.at[4 * t], fr_in.at[4 * t], sibling).wait_send()
            _remote(_halves(bx)[0], _halves(bx)[0], fs_in.at[4 * t + 1], fr_in.at[4 * t + 1], yn).wait_send()
            _remote(by, by, fs_in.at[4 * t + 2], fr_in.at[4 * t + 2], sibling).wait_send()
            _remote(_halves(by)[1], _halves(by)[1], fs_in.at[4 * t + 3], fr_in.at[4 * t + 3], xn).wait_send()
            bd = ins[t].at[_slot(1 - x, 1 - y, c)]
            _remote(bd, bd, gs_in.at[t], gr_in.at[t], sibling).wait_send()

    res = _split_call(
        body, name, tuple(pltpu.HBM(b.shape, b.dtype) for b in bufs),
        [HBM] * n + [SEM] * 6, tuple([HBM] * n), {t: t for t in range(n)})(
            *_hbm(bufs), send, recv, fsend, frecv, gsend, grecv)
    return list(res)


def _sibling_start(name, grads, lands, deps):
    n, nd = len(grads), len(deps)

    def body(*refs):
        g_in, l_in = refs[:n], refs[n:2 * n]
        send, recv, token = refs[2 * n + nd], refs[2 * n + nd + 1], refs[4 * n + nd + 2]
        x, y, c, _ = _position()
        for t in range(n):
            for k in range(NCHIP):
                _remote(g_in[t].at[2 * k + (1 - c)], l_in[t].at[k], send.at[NCHIP * t + k], recv.at[NCHIP * t + k],
                        (x, y, 1 - c)).start()
        token[...] = jnp.zeros_like(token)

    both = list(grads) + list(lands)
    res = _split_call(
        body, name,
        (pltpu.SemaphoreType.DMA((NCHIP * n,)), pltpu.SemaphoreType.DMA((NCHIP * n,)),
         *[pltpu.HBM(b.shape, b.dtype) for b in both], _token_shape()),
        [HBM] * (2 * n) + [ANY] * nd, (SEM, SEM, *[HBM] * (2 * n), pl.BlockSpec(memory_space=pltpu.VMEM)),
        {t: 2 + t for t in range(2 * n)})(*_hbm(both), *deps)
    return res[0], res[1], list(res[2:2 + n]), list(res[2 + n:2 + 2 * n]), res[2 + 2 * n]


def _sibling_finish(name, grads, lands, send, recv, deps):
    n, nd = len(grads), len(deps)

    def body(*refs):
        g_in, l_in, send_in, recv_in = refs[:n], refs[n:2 * n], refs[2 * n], refs[2 * n + 1]
        x, y, c, _ = _position()
        for t in range(n):
            for k in range(NCHIP):
                cp = _remote(g_in[t].at[2 * k + (1 - c)], l_in[t].at[k], send_in.at[NCHIP * t + k],
                             recv_in.at[NCHIP * t + k], (x, y, 1 - c))
                cp.wait_send()
                cp.wait_recv()

    both = list(grads) + list(lands)
    res = _split_call(
        body, name, tuple(pltpu.HBM(b.shape, b.dtype) for b in both),
        [HBM] * (2 * n) + [SEM, SEM] + [ANY] * nd, tuple([HBM] * (2 * n)),
        {t: t for t in range(2 * n)})(*_hbm(both), send, recv, *deps)
    return list(res[:n]), list(res[n:])


def _chips_start(name, parts, lands, deps):
    n, nd = len(parts), len(deps)

    def body(*refs):
        p_in, l_in = refs[:n], refs[n:2 * n]
        send, recv, token = refs[2 * n + nd], refs[2 * n + nd + 1], refs[4 * n + nd + 2]
        x, y, c, chips = _position()
        for t in range(n):
            for j, chip in enumerate(chips):
                _remote(p_in[t].at[2 * chip[0] + chip[1]], l_in[t].at[2 * x + y], send.at[3 * t + j], recv.at[3 * t + j],
                        (*chip, c)).start()
        token[...] = jnp.zeros_like(token)

    both = list(parts) + list(lands)
    res = _split_call(
        body, name,
        (pltpu.SemaphoreType.DMA((3 * n,)), pltpu.SemaphoreType.DMA((3 * n,)), *[pltpu.HBM(b.shape, b.dtype) for b in both],
         _token_shape()),
        [HBM] * (2 * n) + [ANY] * nd, (SEM, SEM, *[HBM] * (2 * n), pl.BlockSpec(memory_space=pltpu.VMEM)),
        {t: 2 + t for t in range(2 * n)})(*_hbm(both), *deps)
    return res[0], res[1], list(res[2:2 + n]), list(res[2 + n:2 + 2 * n]), res[2 + 2 * n]


def _chips_finish(name, parts, lands, send, recv, deps):
    n, nd = len(parts), len(deps)

    def body(*refs):
        p_in, l_in, send_in, recv_in = refs[:n], refs[n:2 * n], refs[2 * n], refs[2 * n + 1]
        x, y, c, chips = _position()
        for t in range(n):
            for j, chip in enumerate(chips):
                k = 2 * chip[0] + chip[1]
                _remote(p_in[t].at[k], l_in[t].at[k], send_in.at[3 * t + j], recv_in.at[3 * t + j], (*chip, c)).wait_recv()
                _remote(p_in[t].at[k], l_in[t].at[2 * x + y], send_in.at[3 * t + j], recv_in.at[3 * t + j],
                        (*chip, c)).wait_send()

    both = list(parts) + list(lands)
    res = _split_call(
        body, name, tuple(pltpu.HBM(b.shape, b.dtype) for b in both),
        [HBM] * (2 * n) + [SEM, SEM] + [ANY] * nd, tuple([HBM] * (2 * n)),
        {t: t for t in range(2 * n)})(*_hbm(both), send, recv, *deps)
    return list(res[n:])


def _all_gather(name, shards):
    n = len(shards)

    def body(*refs):
        src, dst = refs[:n], refs[n:2 * n]
        send_sems, recv_sems, local_sems = refs[2 * n:]
        x, y, c, chips = _position()
        me, sibling = (x, y, c), (x, y, 1 - c)

        def copy(t, k, block, to, from_shard=False):
            slot = dst[t].at[_slot(*block)]
            return _remote(src[t] if from_shard else slot, slot, send_sems.at[t, k], recv_sems.at[t, k], to)

        mine = [pltpu.make_async_copy(src[t], dst[t].at[_slot(x, y, c)], local_sems.at[t]) for t in range(n)]
        first, passed = [], []
        for t in range(n):
            mine[t].start()
            first.append(copy(t, 0, me, sibling, True))
            first += [copy(t, 1 + j, me, (*chip, c), True) for j, chip in enumerate(chips)]
        for cp in first:
            cp.start()
        for j, chip in enumerate(chips):
            for t in range(n):
                copy(t, 1 + j, (*chip, c), me).wait_recv()
                fwd = copy(t, 4 + j, (*chip, c), sibling)
                fwd.start()
                passed.append(fwd)
        for t in range(n):
            copy(t, 0, sibling, me).wait_recv()
            for j, chip in enumerate(chips):
                copy(t, 4 + j, (*chip, 1 - c), me).wait_recv()
        for cp in first + passed:
            cp.wait_send()
        for t in range(n):
            mine[t].wait()

    outs = _pcall(
        body, name=name, out_shape=[_sds((NDEV,) + s.shape, s.dtype) for s in shards],
        in_specs=[ANY] * n, out_specs=[ANY] * n,
        scratch=[pltpu.SemaphoreType.DMA((n, 7)), pltpu.SemaphoreType.DMA((n, 7)), pltpu.SemaphoreType.DMA((n,))],
    )(*shards)
    return list(outs)


def _pair_sum(name, grad, recv, where):
    _, r, w = grad.shape
    tr = _row_tile(r, w)
    g4 = grad.reshape(NCHIP, 2, r, w)

    def body(where_ref, g_ref, r_ref, o_ref, own_ref):
        val = (g_ref[...].astype(F32) + r_ref[...].astype(F32)).astype(o_ref.dtype)
        o_ref[...] = val

        @pl.when(pl.program_id(1) == where_ref[1])
        def _():
            own_ref[...] = val

    out = _sds((NCHIP, r, w), grad.dtype)
    return _pcall(
        body, name=name, out_shape=[out, out], grid=(r // tr, NCHIP), prefetch=1,
        in_specs=[pl.BlockSpec((None, None, tr, w), lambda i, k, wr: (k, wr[0], i, 0)),
                  pl.BlockSpec((None, tr, w), lambda i, k, wr: (k, i, 0))],
        out_specs=[pl.BlockSpec((None, tr, w), lambda i, k, wr: (k, i, 0)),
                   pl.BlockSpec((None, tr, w), lambda i, k, wr: (wr[1], i, 0))],
    )(where, g4, recv)


def _row_tile(rows, width, budget=2 * 1024 * 1024):
    best = None
    for t in range(16, rows + 1, 16):
        if rows % t == 0 and t * width * 4 <= budget:
            best = t
    if best is None and rows * width * 4 <= budget:
        best = rows
    assert best is not None, (rows, width)
    return best


def _gemm(name, grid, operands, prods, extras, outs, epilogue, *, nk=1, acc_shape=None, aliases=None, separate=False,
          deps=()):
    n_op, n_ex, n_out = len(operands), len(extras), len(outs)

    def body(*refs):
        ops, ex, out_refs = refs[:n_op], refs[n_op:n_op + n_ex], refs[n_op + n_ex:n_op + n_ex + n_out]
        parts = []
        for pr in prods:
            a, b = ops[pr[0]], ops[pr[1]]
            av = pr[3](a) if len(pr) > 3 and pr[3] else a[...]
            bv = pr[4](b) if len(pr) > 4 and pr[4] else b[...]
            parts.append(lax.dot_general(av, bv, pr[2], preferred_element_type=F32))
        if separate:
            epilogue(parts, ex, out_refs)
            return
        part = parts[0]
        for p in parts[1:]:
            part = part + p
        if nk == 1:
            epilogue(part, ex, out_refs)
        else:
            acc = refs[-1]
            k = pl.program_id(len(grid) - 1)

            @pl.when(k == 0)
            def _():
                acc[...] = part

            @pl.when(k > 0)
            def _():
                acc[...] += part

            @pl.when(k == nk - 1)
            def _():
                epilogue(acc[...], ex, out_refs)

    res = _pcall(
        body, name=name, out_shape=[o[0] for o in outs], grid=grid,
        in_specs=[o[1] for o in operands] + [e[1] for e in extras], out_specs=[o[1] for o in outs],
        scratch=[pltpu.VMEM(acc_shape, F32)] if nk > 1 else [], aliases=aliases, deps=deps,
    )(*[o[0] for o in operands], *[e[0] for e in extras])
    return list(res)


def _store(acc, ex, outs):
    outs[0][...] = acc.astype(outs[0].dtype)


def _store_add_extra(acc, ex, outs):
    v = acc
    for e in ex:
        v = v + e[...]
    outs[0][...] = v.astype(outs[0].dtype)


def _stacked(ref):
    b = ref[...]
    return b.reshape(b.shape[0] * b.shape[1], b.shape[2])


def _pick(c):
    return lambda ref: ref[c]


def _cols(c, width):
    return lambda ref: ref[:, c * width:(c + 1) * width]


def _gelu_parts(z):
    c = math.sqrt(2.0 / math.pi)
    t = jnp.tanh(c * (z + 0.044715 * (z * z * z)))
    val = 0.5 * z * (1.0 + t)
    grad = 0.5 * (1.0 + t) + 0.5 * z * (1.0 - t * t) * (c * (1.0 + 3.0 * 0.044715 * z * z))
    return val, grad


def _rms_fwd(name, h, g, deps=()):
    s, d = h.shape
    tr = _row_tile(s, d)

    def body(h_ref, g_ref, o_ref):
        hv = h_ref[...]
        r = lax.rsqrt(jnp.mean(hv * hv, axis=-1, keepdims=True) + RMS_EPS)
        o_ref[...] = (hv * r * g_ref[...]).astype(o_ref.dtype)

    return _pcall(
        body, name=name, out_shape=_sds((s, d), BF16), grid=(s // tr,),
        in_specs=[pl.BlockSpec((tr, d), lambda i: (i, 0)), pl.BlockSpec((1, d), lambda i: (0, 0))],
        out_specs=pl.BlockSpec((tr, d), lambda i: (i, 0)), deps=deps,
    )(h, g.reshape(1, d))


def _accumulate(ref, val, first):
    @pl.when(first)
    def _():
        ref[...] = val

    @pl.when(jnp.logical_not(first))
    def _():
        ref[...] += val


def _rms_bwd(name, h, g, dy, res, deps=()):
    s, d = h.shape
    tr = _row_tile(s, d, budget=1024 * 1024)

    def body(h_ref, g_ref, dy_ref, res_ref, dh_ref, dhb_ref, dg_ref, cs_ref):
        hv = h_ref[...]
        r = lax.rsqrt(jnp.mean(hv * hv, axis=-1, keepdims=True) + RMS_EPS)
        xhat = hv * r
        dyv = dy_ref[...]
        dxh = dyv * g_ref[...]
        dh = res_ref[...] + r * (dxh - xhat * jnp.mean(dxh * xhat, axis=-1, keepdims=True))
        dh_ref[...] = dh
        dhb_ref[...] = dh.astype(BF16)
        first = pl.program_id(0) == 0
        _accumulate(dg_ref, jnp.sum(dyv * xhat, axis=0, keepdims=True), first)
        _accumulate(cs_ref, jnp.sum(dh, axis=0, keepdims=True), first)

    row = pl.BlockSpec((tr, d), lambda i: (i, 0))
    vec = pl.BlockSpec((1, d), lambda i: (0, 0))
    return _pcall(
        body, name=name, out_shape=[_sds((s, d), F32), _sds((s, d), BF16), _sds((1, d), F32), _sds((1, d), F32)],
        grid=(s // tr,), in_specs=[row, vec, row, row], out_specs=[row, row, vec, vec], deps=deps,
    )(h, g.reshape(1, d), dy, res)


def _loss_bwd(name, h, g, target):
    s, d = h.shape
    tr = _row_tile(s, d, budget=1024 * 1024)

    def body(h_ref, g_ref, t_ref, loss_ref, dh_ref, dhb_ref, dg_ref):
        hv = h_ref[...]
        r = lax.rsqrt(jnp.mean(hv * hv, axis=-1, keepdims=True) + RMS_EPS)
        xhat = hv * r
        diff = xhat * g_ref[...] - t_ref[...]
        part = jnp.sum(jnp.sum(diff * diff, axis=1, keepdims=True), axis=0, keepdims=True) * (0.5 / d)
        dyv = diff * (1.0 / d)
        dxh = dyv * g_ref[...]
        dh = r * (dxh - xhat * jnp.mean(dxh * xhat, axis=-1, keepdims=True))
        dh_ref[...] = dh
        dhb_ref[...] = dh.astype(BF16)
        first = pl.program_id(0) == 0
        _accumulate(loss_ref, part, first)
        _accumulate(dg_ref, jnp.sum(dyv * xhat, axis=0, keepdims=True), first)

    row = pl.BlockSpec((tr, d), lambda i: (i, 0))
    vec = pl.BlockSpec((1, d), lambda i: (0, 0))
    one = pl.BlockSpec((1, 1), lambda i: (0, 0))
    return _pcall(
        body, name=name, out_shape=[_sds((1, 1), F32), _sds((s, d), F32), _sds((s, d), BF16), _sds((1, d), F32)],
        grid=(s // tr,), in_specs=[row, vec, row], out_specs=[one, row, row, vec],
    )(h, g.reshape(1, d), target)


def _tril_mask():
    return lax.broadcasted_iota(jnp.int32, (CHUNK, CHUNK), 0) >= lax.broadcasted_iota(jnp.int32, (CHUNK, CHUNK), 1)


def _gmlp_fwd(name, zp, gv, ws, bst):
    s, d2 = zp.shape
    d = d2 // 2
    gw = d // GROUPS

    def body(zp_ref, gv_ref, ws_ref, bst_ref, o_ref):
        u, _ = _gelu_parts(zp_ref[:, :d])
        v, _ = _gelu_parts(zp_ref[:, d:])
        rv = lax.rsqrt(jnp.mean(v * v, axis=-1, keepdims=True) + RMS_EPS)
        vn = (v * rv * gv_ref[...]).astype(BF16)
        tril = _tril_mask()
        for g in range(GROUPS):
            sl = slice(g * gw, (g + 1) * gw)
            wc = jnp.where(tril, ws_ref[g], 0.0).astype(BF16)
            sg = jnp.dot(wc, vn[:, sl], preferred_element_type=F32) + bst_ref[:, g:g + 1]
            o_ref[:, sl] = (u[:, sl] * sg).astype(o_ref.dtype)

    return _pcall(
        body, name=name, out_shape=_sds((s, d), BF16), grid=(s // CHUNK,),
        in_specs=[pl.BlockSpec((CHUNK, d2), lambda i: (i, 0)), pl.BlockSpec((1, d), lambda i: (0, 0)),
                  pl.BlockSpec((GROUPS, CHUNK, CHUNK), lambda i: (0, 0, 0)),
                  pl.BlockSpec((CHUNK, GROUPS), lambda i: (0, 0))],
        out_specs=pl.BlockSpec((CHUNK, d), lambda i: (i, 0)),
    )(zp, gv, ws, bst)


def _gmlp_bwd(name, zp, dgated, gv, ws, bst):
    s, d2 = zp.shape
    d = d2 // 2
    gw = d // GROUPS

    def body(zp_ref, dg_ref, gv_ref, ws_ref, bst_ref, dzp_ref, dws_ref, dbs_ref, dgv_ref, dvn_ref):
        u, gu = _gelu_parts(zp_ref[:, :d])
        v, gvv = _gelu_parts(zp_ref[:, d:])
        rv = lax.rsqrt(jnp.mean(v * v, axis=-1, keepdims=True) + RMS_EPS)
        vhat = v * rv
        vn = (vhat * gv_ref[...]).astype(BF16)
        tril = _tril_mask()
        first = pl.program_id(0) == 0
        ones = jnp.ones((8, gw), F32)

        @pl.when(first)
        def _():
            dws_ref[...] = jnp.zeros_like(dws_ref)
            dbs_ref[...] = jnp.zeros_like(dbs_ref)

        for g in range(GROUPS):
            sl = slice(g * gw, (g + 1) * gw)
            wc = jnp.where(tril, ws_ref[g], 0.0).astype(BF16)
            sg = jnp.dot(wc, vn[:, sl], preferred_element_type=F32) + bst_ref[:, g:g + 1]
            dgs = dg_ref[:, sl]
            ds = dgs * u[:, sl]
            dsb = ds.astype(BF16)
            dzp_ref[:, sl] = (dgs * sg * gu[:, sl]).astype(dzp_ref.dtype)
            dvn_ref[:, sl] = lax.dot_general(wc, dsb, TN, preferred_element_type=F32)
            dw = lax.dot_general(dsb, vn[:, sl], NT, preferred_element_type=F32)
            dws_ref[g] += jnp.where(tril, dw, 0.0)
            dbs_ref[g] += lax.dot_general(ones, ds, NT, preferred_element_type=F32, precision=lax.Precision.HIGHEST)
        dvn = dvn_ref[...]
        dvh = dvn * gv_ref[...]
        dv = rv * (dvh - vhat * jnp.mean(dvh * vhat, axis=-1, keepdims=True))
        dzp_ref[:, d:] = (dv * gvv).astype(dzp_ref.dtype)
        _accumulate(dgv_ref, jnp.sum(dvn * vhat, axis=0, keepdims=True), first)

    return _pcall(
        body, name=name,
        out_shape=[_sds((s, d2), BF16), _sds((GROUPS, CHUNK, CHUNK), F32), _sds((GROUPS, 8, CHUNK), F32),
                   _sds((1, d), F32)],
        grid=(s // CHUNK,),
        in_specs=[pl.BlockSpec((CHUNK, d2), lambda i: (i, 0)), pl.BlockSpec((CHUNK, d), lambda i: (i, 0)),
                  pl.BlockSpec((1, d), lambda i: (0, 0)), pl.BlockSpec((GROUPS, CHUNK, CHUNK), lambda i: (0, 0, 0)),
                  pl.BlockSpec((CHUNK, GROUPS), lambda i: (0, 0))],
        out_specs=[pl.BlockSpec((CHUNK, d2), lambda i: (i, 0)),
                   pl.BlockSpec((GROUPS, CHUNK, CHUNK), lambda i: (0, 0, 0)),
                   pl.BlockSpec((GROUPS, 8, CHUNK), lambda i: (0, 0, 0)), pl.BlockSpec((1, d), lambda i: (0, 0))],
        scratch=[pltpu.VMEM((CHUNK, d), F32)],
    )(zp, dgated, gv, ws, bst)


def _bucket_table():
    dist = np.arange(BLOCK)[:, None] + BLOCK - np.arange(2 * BLOCK)[None, :]
    in_window = (dist >= 0) & (dist < BLOCK)
    dd = np.clip(dist, 0, None)
    max_exact = N_BUCKETS // 2
    dl = np.maximum(dd, 1).astype(np.float32)
    large = max_exact + (np.log(dl / np.float32(max_exact)) / np.float32(math.log(MAX_DISTANCE / max_exact))
                         * np.float32(N_BUCKETS - max_exact)).astype(np.int32)
    large = np.minimum(large, N_BUCKETS - 1)
    bucket = np.where(dd < max_exact, dd, large)
    return np.where(in_window, bucket, -1).astype(np.int32).reshape(1, -1)


def _bias_table(name, rel_bias_t, buckets):
    nh = rel_bias_t.shape[0]
    p = buckets.shape[1]
    tp = 4096

    def body(rb_ref, bk_ref, o_ref):
        bk = bk_ref[...]
        onehot = (lax.broadcasted_iota(jnp.int32, (N_BUCKETS, tp), 0) == bk).astype(F32)
        val = jnp.dot(rb_ref[...], onehot, preferred_element_type=F32, precision=lax.Precision.HIGHEST)
        o_ref[...] = jnp.where(bk >= 0, val, NEG_INF)

    return _pcall(
        body, name=name, out_shape=_sds((nh, p), F32), grid=(p // tp,),
        in_specs=[pl.BlockSpec((nh, N_BUCKETS), lambda i: (0, 0)), pl.BlockSpec((1, tp), lambda i: (0, i))],
        out_specs=pl.BlockSpec((nh, tp), lambda i: (0, i)),
    )(rel_bias_t, buckets)


def _bias_grad(name, dbiases, buckets):
    nh, p = dbiases[0].shape
    n = len(dbiases)
    tp = 4096

    def body(*refs):
        bk_ref, o_ref = refs[n], refs[n + 1]
        onehot = (lax.broadcasted_iota(jnp.int32, (N_BUCKETS, tp), 0) == bk_ref[...]).astype(F32)
        db = refs[0][...]
        for r in refs[1:n]:
            db = db + r[...]
        part = lax.dot_general(onehot, db, NT, preferred_element_type=F32, precision=lax.Precision.HIGHEST)
        _accumulate(o_ref, part, pl.program_id(0) == 0)

    return _pcall(
        body, name=name, out_shape=_sds((N_BUCKETS, nh), F32), grid=(p // tp,),
        in_specs=[pl.BlockSpec((nh, tp), lambda i: (0, i))] * n + [pl.BlockSpec((1, tp), lambda i: (0, i))],
        out_specs=pl.BlockSpec((N_BUCKETS, nh), lambda i: (0, 0)),
    )(*dbiases, buckets)


def _stack_heads(ref):
    return jnp.concatenate([ref[:, hh * HEAD_DIM:(hh + 1) * HEAD_DIM] for hh in range(KV_GROUP)], axis=0)


def _attn_probs(q, kb, b_ref, s_ref):
    kh, i = pl.program_id(0), pl.program_id(1)
    penalty = jnp.where(i > 0, 0.0, NEG_INF).astype(F32)
    col = lax.broadcasted_iota(jnp.int32, (1, 2 * BLOCK), 1)
    bias = b_ref[...].reshape(KV_GROUP * BLOCK, 2 * BLOCK) + jnp.where(col < BLOCK, penalty, 0.0)
    sink = jnp.concatenate([jnp.full((BLOCK, 1), s_ref[kh * KV_GROUP + hh], F32) for hh in range(KV_GROUP)], axis=0)
    s = lax.dot_general(q, kb, NT, preferred_element_type=F32) * 0.125 + bias
    m = jnp.maximum(jnp.max(s, axis=-1, keepdims=True), sink)
    p = jnp.exp(s - m)
    es = jnp.exp(sink - m)
    inv = 1.0 / (jnp.sum(p, axis=-1, keepdims=True) + es)
    return p * inv, es * inv


def _attn_specs(nkv):
    gq = KV_GROUP * HEAD_DIM
    q_spec = pl.BlockSpec((BLOCK, gq), lambda kh, i: (i, kh))
    prev = pl.BlockSpec((None, BLOCK, HEAD_DIM), lambda kh, i: (kh, jnp.maximum(i - 1, 0), 0))
    cur = pl.BlockSpec((None, BLOCK, HEAD_DIM), lambda kh, i: (kh, i, 0))
    bias = pl.BlockSpec((KV_GROUP, BLOCK, 2 * BLOCK), lambda kh, i: (kh, 0, 0))
    smem = pl.BlockSpec(memory_space=pltpu.SMEM)
    return q_spec, prev, cur, bias, smem


def _prob_specs():
    probs = pl.BlockSpec((None, None, KV_GROUP * BLOCK, 2 * BLOCK), lambda kh, i: (kh, i, 0, 0))
    sink_probs = pl.BlockSpec((None, None, KV_GROUP * BLOCK, 1), lambda kh, i: (kh, i, 0, 0))
    return probs, sink_probs


def _attn_fwd(name, q, k, v, bias, sinks):
    s, dq = q.shape
    nkv = k.shape[0]
    q_spec, prev, cur, bias_spec, smem = _attn_specs(nkv)
    p_spec, ps_spec = _prob_specs()

    def body(q_ref, kp_ref, kc_ref, vp_ref, vc_ref, b_ref, s_ref, o_ref, p_ref, ps_ref):
        kb = jnp.concatenate([kp_ref[...], kc_ref[...]], axis=0)
        vb = jnp.concatenate([vp_ref[...], vc_ref[...]], axis=0)
        p, ps = _attn_probs(_stack_heads(q_ref), kb, b_ref, s_ref)
        pb = p.astype(BF16)
        p_ref[...] = pb
        ps_ref[...] = ps
        o = jnp.dot(pb, vb, preferred_element_type=F32)
        for hh in range(KV_GROUP):
            o_ref[:, hh * HEAD_DIM:(hh + 1) * HEAD_DIM] = o[hh * BLOCK:(hh + 1) * BLOCK].astype(o_ref.dtype)

    return _pcall(
        body, name=name,
        out_shape=[_sds((s, dq), BF16), _sds((nkv, s // BLOCK, KV_GROUP * BLOCK, 2 * BLOCK), BF16),
                   _sds((nkv, s // BLOCK, KV_GROUP * BLOCK, 1), F32)],
        grid=(nkv, s // BLOCK),
        in_specs=[q_spec, prev, cur, prev, cur, bias_spec, smem], out_specs=[q_spec, p_spec, ps_spec],
    )(q, k, k, v, v, bias, sinks)


def _attn_bwd(name, q, k, v, do, probs, sink_probs):
    s, dq = q.shape
    nkv = k.shape[0]
    gq = KV_GROUP * HEAD_DIM
    q_spec, prev, cur, bias_spec, _ = _attn_specs(nkv)
    p_spec, ps_spec = _prob_specs()

    def body(q_ref, do_ref, kp_ref, kc_ref, vp_ref, vc_ref, p_ref, ps_ref,
             dq_ref, dbq_ref, dkc_ref, dkp_ref, dvc_ref, dvp_ref, dbias_ref, dsink_ref):
        @pl.when(pl.program_id(1) == 0)
        def _():
            dbias_ref[...] = jnp.zeros_like(dbias_ref)
            dsink_ref[...] = jnp.zeros_like(dsink_ref)
            dbq_ref[...] = jnp.zeros_like(dbq_ref)

        kb = jnp.concatenate([kp_ref[...], kc_ref[...]], axis=0)
        vb = jnp.concatenate([vp_ref[...], vc_ref[...]], axis=0)
        q, do = _stack_heads(q_ref), _stack_heads(do_ref)
        pb = p_ref[...]
        p = pb.astype(F32)
        dp = lax.dot_general(do, vb, NT, preferred_element_type=F32)
        delta = jnp.sum(p * dp, axis=-1, keepdims=True)
        ds = p * (dp - delta)
        dsb = ds.astype(BF16)
        dq = jnp.dot(dsb, kb, preferred_element_type=F32) * 0.125
        dsk = -(ps_ref[...] * delta)
        for hh in range(KV_GROUP):
            sl, rows = slice(hh * HEAD_DIM, (hh + 1) * HEAD_DIM), slice(hh * BLOCK, (hh + 1) * BLOCK)
            dq_ref[:, sl] = dq[rows].astype(dq_ref.dtype)
            dbq_ref[:, sl] += jnp.sum(dq[rows], axis=0, keepdims=True)
            dsink_ref[:, hh:hh + 1] += jnp.sum(dsk[rows], axis=0, keepdims=True)
        dkb = lax.dot_general(dsb, q, TN, preferred_element_type=F32) * 0.125
        dvb = lax.dot_general(pb, do, TN, preferred_element_type=F32)
        dkp_ref[...], dkc_ref[...] = dkb[:BLOCK], dkb[BLOCK:]
        dvp_ref[...], dvc_ref[...] = dvb[:BLOCK], dvb[BLOCK:]
        dbias_ref[...] += ds.reshape(KV_GROUP, BLOCK, 2 * BLOCK)

    kv_out = _sds((nkv, s, HEAD_DIM), F32)
    return _pcall(
        body, name=name,
        out_shape=[_sds((s, dq), BF16), _sds((1, dq), F32), kv_out, kv_out, kv_out, kv_out,
                   _sds((nkv * KV_GROUP, BLOCK, 2 * BLOCK), F32), _sds((nkv, 1, KV_GROUP), F32)],
        grid=(nkv, s // BLOCK),
        in_specs=[q_spec, q_spec, prev, cur, prev, cur, p_spec, ps_spec],
        out_specs=[q_spec, pl.BlockSpec((1, gq), lambda kh, i: (0, kh)), cur, cur, cur, cur, bias_spec,
                   pl.BlockSpec((None, 1, KV_GROUP), lambda kh, i: (kh, 0, 0))],
    )(q, do, k, k, v, v, probs, sink_probs)


def _kv_grad(name, parts):
    nkv, s, _ = parts[0][0].shape
    nb = s // BLOCK
    w = 2 * nkv * HEAD_DIM
    n = len(parts)

    def body(*refs):
        o_ref, cs_ref = refs[4 * n], refs[4 * n + 1]
        i = pl.program_id(0)
        keep = jnp.where(i < nb - 1, 1.0, 0.0).astype(F32)

        @pl.when(i == 0)
        def _():
            cs_ref[...] = jnp.zeros_like(cs_ref)

        for which in range(2):
            for hh in range(nkv):
                val = None
                for l in range(n):
                    cur_ref, nxt_ref = refs[4 * l + 2 * which], refs[4 * l + 2 * which + 1]
                    t = cur_ref[hh] + keep * nxt_ref[hh]
                    val = t if val is None else val + t
                sl = slice((which * nkv + hh) * HEAD_DIM, (which * nkv + hh + 1) * HEAD_DIM)
                o_ref[:, sl] = val.astype(o_ref.dtype)
                cs_ref[:, sl] += jnp.sum(val, axis=0, keepdims=True)

    cur = pl.BlockSpec((nkv, BLOCK, HEAD_DIM), lambda i: (0, i, 0))
    nxt = pl.BlockSpec((nkv, BLOCK, HEAD_DIM), lambda i: (0, jnp.minimum(i + 1, nb - 1), 0))
    flat = [a for p in parts for a in p]
    return _pcall(
        body, name=name, out_shape=[_sds((s, w), BF16), _sds((1, w), F32)], grid=(nb,),
        in_specs=[cur, nxt] * (2 * n),
        out_specs=[pl.BlockSpec((BLOCK, w), lambda i: (i, 0)), pl.BlockSpec((1, w), lambda i: (0, 0))],
    )(*flat)


def _adamw_math(w, g, m, v):
    m = ADAM_B1 * m + (1.0 - ADAM_B1) * g
    v = ADAM_B2 * v + (1.0 - ADAM_B2) * (g * g)
    m_hat = m / (1.0 - ADAM_B1 ** ADAM_STEP)
    v_hat = v / (1.0 - ADAM_B2 ** ADAM_STEP)
    delta = -ADAM_LR * (m_hat / (jnp.sqrt(v_hat) + ADAM_EPS) + ADAM_WD * w)
    return delta, m, v


def _adamw_shard(name, w, m, v, parts, row0, layer, prev, deps=()):
    _, r, wd = w.shape
    tr = _row_tile(r, wd, budget=3 * 512 * 1024)
    assert row0 % tr == 0

    def body(w_ref, m_ref, v_ref, p_ref, a0, a1, a2, a3, g_ref, d_ref, nm_ref, nv_ref):
        g = p_ref[0].astype(F32)
        for k in range(1, NCHIP):
            g = g + p_ref[k].astype(F32)
        delta, nm, nv = _adamw_math(w_ref[...], g, m_ref[...], v_ref[...])
        g_ref[...], d_ref[...], nm_ref[...], nv_ref[...] = g, delta, nm, nv

    par = pl.BlockSpec((None, tr, wd), lambda i: (layer, i, 0))
    out = _sds(w.shape, F32)
    return _pcall(
        body, name=name, out_shape=[out, out, out, out], grid=(r // tr,),
        in_specs=[par, par, par, pl.BlockSpec((NCHIP, tr, wd), lambda i: (0, row0 // tr + i, 0)), ANY, ANY, ANY, ANY],
        out_specs=[par, par, par, par], aliases={4: 0, 5: 1, 6: 2, 7: 3}, deps=deps,
    )(w, m, v, parts, *prev)


def _sum_devices(name, gathered):
    _, r, wd = gathered.shape

    def body(g_ref, o_ref):
        acc = g_ref[0]
        for k in range(1, NDEV):
            acc = acc + g_ref[k]
        o_ref[...] = acc

    return _pcall(body, name=name, out_shape=_sds((r, wd), F32), grid=(1,),
                  in_specs=[pl.BlockSpec((NDEV, r, wd), lambda i: (0, 0, 0))],
                  out_specs=pl.BlockSpec((r, wd), lambda i: (0, 0)))(gathered)


def _adamw_flat(name, w, g, m, v):
    shape = w.shape

    def body(w_ref, g_ref, m_ref, v_ref, d_ref, nm_ref, nv_ref):
        d_ref[...], nm_ref[...], nv_ref[...] = _adamw_math(w_ref[...], g_ref[...], m_ref[...], v_ref[...])

    spec = pl.BlockSpec(shape, lambda i: (0, 0))
    out = _sds(shape, F32)
    return _pcall(body, name=name, out_shape=[out, out, out], grid=(1,), in_specs=[spec] * 4,
                  out_specs=[spec] * 3)(w, g, m, v)


def _cast_into(name, src, layer, buf, row0, me):
    _, r, wd = src.shape
    tr = _row_tile(r, wd)
    assert row0 % tr == 0

    def body(me_ref, s_ref, b_ref, o_ref):
        o_ref[...] = s_ref[...].astype(o_ref.dtype)

    return _pcall(
        body, name=name, out_shape=_sds(buf.shape, buf.dtype), grid=(r // tr,), prefetch=1,
        in_specs=[pl.BlockSpec((None, tr, wd), lambda i, mr: (layer, i, 0)), ANY],
        out_specs=pl.BlockSpec((None, tr, wd), lambda i, mr: (mr[0], row0 // tr + i, 0)), aliases={2: 0},
    )(me, src, buf)


def _pack(arrays):
    rows = []
    for a in arrays:
        flat = a.reshape(-1).astype(F32)
        pad = (-flat.shape[0]) % 1024
        rows.append(jnp.pad(flat, (0, pad)).reshape(-1, 128))
    return jnp.concatenate(rows, axis=0)


def _unpack(packed, shapes):
    out, r = [], 0
    for shp in shapes:
        n = int(np.prod(shp))
        nr = (n + 1023) // 1024 * 8
        out.append(packed[r:r + nr].reshape(-1)[:n].reshape(shp))
        r += nr
    return out


def kernel(x, mix_norm, ffn_norm, a_w_in, a_norm_v, a_w_s, a_b_s, a_w_out, kv_norm, w_kv, b_kv, b_w_q, b_b_q, b_sinks, b_w_o, b_b_o, rel_bias, ffn_w_gate, ffn_w_up, ffn_w_down, final_norm, loss_target, m_mix_norm, m_ffn_norm, m_a_w_in, m_a_norm_v, m_a_w_s, m_a_b_s, m_a_w_out, m_kv_norm, m_w_kv, m_b_kv, m_b_w_q, m_b_b_q, m_b_sinks, m_b_w_o, m_b_b_o, m_rel_bias, m_ffn_w_gate, m_ffn_w_up, m_ffn_w_down, m_final_norm, v_mix_norm, v_ffn_norm, v_a_w_in, v_a_norm_v, v_a_w_s, v_a_b_s, v_a_w_out, v_kv_norm, v_w_kv, v_b_kv, v_b_w_q, v_b_b_q, v_b_sinks, v_b_w_o, v_b_b_o, v_rel_bias, v_ffn_w_gate, v_ffn_w_up, v_ffn_w_down, v_final_norm):
    _, S, D = x.shape
    LA, LB, L = a_w_in.shape[0], b_w_q.shape[0], ffn_w_gate.shape[0]
    F = ffn_w_gate.shape[2]
    DS = D // NDEV
    ZC = a_w_in.shape[2]
    KVW = w_kv.shape[1]
    NKV = KVW // (2 * HEAD_DIM)
    NH = D // HEAD_DIM
    assert ZC * NDEV == 2 * D and NH == NKV * KV_GROUP and S % BLOCK == 0
    TM = min(1024, S)
    TN_ = min(1024, D)
    TS = min(512, D)
    KC = 4

    ix, iy, ic = lax.axis_index("x"), lax.axis_index("y"), lax.axis_index("c")
    me = (4 * ix + 2 * iy + ic).astype(jnp.int32)
    me1 = me.reshape(1)
    where = jnp.stack([ic, 2 * ix + iy]).astype(jnp.int32)

    def tr3(a):
        return jnp.transpose(a, (0, 2, 1))

    gate_t, up_t = tr3(ffn_w_gate), tr3(ffn_w_up)
    w_kv3 = w_kv.reshape((1,) + w_kv.shape)

    def layer_arrays(l):
        arrs = [("gu", 2 * F, D, [(gate_t, l, 0), (up_t, l, F)]), ("down", F, D, [(ffn_w_down, l, 0)])]
        if l < LA:
            arrs += [("win", D, ZC, [(a_w_in, l, 0)]), ("wout", DS, D, [(a_w_out, l, 0)])]
            if l == LA - 1:
                arrs.append(("wkv", DS, KVW, [(w_kv3, 0, 0)]))
        else:
            i_b = l - LA
            arrs.append(("wqo", 2 * DS, D, [(b_w_q, i_b, 0), (b_w_o, i_b, DS)]))
        return arrs

    gathers = []

    def gather_begin(l, deps):
        g = gathers[l]
        g["send"], g["recv"], g["bufs"], g["token"] = _relay_start(f"relay_start{l}", g["bufs"], deps)

    for l in range(L):
        keys, bufs = [], []
        for key, rows, width, sources in layer_arrays(l):
            buf = lax.empty((NDEV, rows, width), BF16)
            for si, (src, li, row0) in enumerate(sources):
                buf = _cast_into(f"cast_{key}{l}_{si}", src, li, buf, row0, me1)
            keys.append(key)
            bufs.append(buf)
        gathers.append(dict(keys=keys, bufs=bufs))
        if l == 0:
            nv_rows = _pack([a_norm_v])
            nv = _cast_into("put_norm_v", nv_rows.reshape((1,) + nv_rows.shape), 0,
                            lax.empty((NDEV,) + nv_rows.shape, F32), 0, me1)
            nv_send, nv_recv, nv_bufs, token = _gather_start("gather_norm_v_start", [nv], [])
            gather_begin(0, [token])

    def gather_relay(l, deps):
        g = gathers[l]
        g["fsend"], g["frecv"], g["bufs"], tok = _relay_neighbors(f"relay_neighbors{l}", g["bufs"], g["send"], g["recv"],
                                                                  deps)
        if l + 1 < L:
            gather_begin(l + 1, [tok])
            tok = gathers[l + 1]["token"]
        return tok

    def finish_gather(l, deps):
        g = gathers[l]
        gsend, grecv, bufs = _relay_diagonal(f"relay_diagonal{l}", g["bufs"], g["fsend"], g["frecv"], deps)
        bufs = _relay_finish(f"relay_finish{l}", bufs, g["send"], g["recv"], g["fsend"], g["frecv"], gsend, grecv)
        return dict(zip(g["keys"], bufs))

    token = gather_relay(0, [gathers[0]["token"]] + [b for g in gathers[1:] for b in g["bufs"]])

    buckets = jnp.asarray(_bucket_table())
    bias = _bias_table("bias_table", rel_bias.T, buckets).reshape(NH, BLOCK, 2 * BLOCK)

    def rows_full(tm):
        return pl.BlockSpec((tm, D), lambda i, j: (i, 0))

    def tile(tm, tn):
        return pl.BlockSpec((tm, tn), lambda i, j: (i, j))

    vec_tile = pl.BlockSpec((1, TN_), lambda i, j: (0, j))

    def ffn_forward(l, wl, h_mid, tag, deps):
        xf = _rms_fwd(f"ffn_norm_fwd{tag}", h_mid, ffn_norm[l], deps=deps)

        def ep(parts, ex, outs):
            a, b = parts
            sg = jax.nn.sigmoid(a)
            silu = a * sg
            outs[0][0] = (b * (sg * (1.0 + a * (1.0 - sg)))).astype(BF16)
            outs[0][1] = silu.astype(BF16)
            outs[1][...] = (silu * b).astype(BF16)

        ab, hid = _gemm(
            f"ffn_up{tag}", (S // TM, NDEV),
            [(xf, rows_full(TM)),
             (wl["gu"], pl.BlockSpec((None, F, D), lambda i, e: (e, 0, 0))),
             (wl["gu"], pl.BlockSpec((None, F, D), lambda i, e: (e, 1, 0)))],
            [(0, 1, NT), (0, 2, NT)], [],
            [(_sds((2, NDEV, S, F), BF16), pl.BlockSpec((2, None, TM, F), lambda i, e: (0, e, i, 0))),
             (_sds((NDEV, S, F), BF16), pl.BlockSpec((None, TM, F), lambda i, e: (e, i, 0)))],
            ep, separate=True)
        (h_out,) = _gemm(
            f"ffn_down{tag}", (S // TM, D // TN_, NDEV // KC),
            [(hid, pl.BlockSpec((KC, TM, F), lambda i, j, k: (k, i, 0))),
             (wl["down"], pl.BlockSpec((KC, F, TN_), lambda i, j, k: (k, 0, j)))],
            [(0, 1, NN, _pick(c), _pick(c)) for c in range(KC)],
            [(h_mid, pl.BlockSpec((TM, TN_), lambda i, j, k: (i, j)))],
            [(_sds((S, D), F32), pl.BlockSpec((TM, TN_), lambda i, j, k: (i, j)))],
            _store_add_extra, nk=NDEV // KC, acc_shape=(TM, TN_))
        return dict(h_mid=h_mid, xf=xf, ab=ab, hid=hid), h_out

    def stacked_rows_gemm(name, a, wmat, blk, extras, ep, out_dtype):
        return _gemm(
            name, (S // TM, D // TN_),
            [(a, rows_full(TM)), (wmat, pl.BlockSpec((NDEV, DS, TN_), lambda i, j: (0, blk, j)))],
            [(0, 1, NN, None, _stacked)], extras,
            [(_sds((S, D), out_dtype), tile(TM, TN_))], ep)[0]

    def back_rows_gemm(name, a, wmat, blk, out_dtype, deps=()):
        return _gemm(
            name, (S // TM, NDEV),
            [(a, rows_full(TM)), (wmat, pl.BlockSpec((None, DS, D), lambda i, e: (e, blk, 0)))],
            [(0, 1, NT)], [], [(_sds((S, D), out_dtype), pl.BlockSpec((TM, DS), lambda i, e: (i, e)))], _store,
            deps=deps)[0]

    def grad_rows_gemm(name, act, d_bf, buf, blk):
        return _gemm(
            name, (NDEV,),
            [(act, pl.BlockSpec((S, DS), lambda e: (0, e))), (d_bf, pl.BlockSpec((S, D), lambda e: (0, 0)))],
            [(0, 1, TN)], [(buf, ANY)],
            [(_sds(buf.shape, BF16), pl.BlockSpec((None, DS, D), lambda e: (e, blk, 0)))],
            _store, aliases={2: 0})[0]

    saved, weights = [], []
    h = x.reshape(S, D)
    k_heads = v_heads = hn = h_kv = norm_v = None
    for layer in range(L):
        wl = finish_gather(layer, [token] if layer == 0 else [h])
        weights.append(wl)
        if layer == 0:
            nv_fsend, nv_frecv, nv_bufs = _gather_forward("gather_norm_v_forward", nv_bufs, nv_send, nv_recv,
                                                          [wl["down"]])
            (nv_all,) = _gather_finish("gather_norm_v_finish", nv_bufs, nv_send, nv_recv, nv_fsend, nv_frecv)
            norm_v = jnp.transpose(nv_all.reshape(NDEV, -1)[:, :LA * DS].reshape(NDEV, LA, DS), (1, 0, 2)).reshape(LA, D)
        sv = dict(h_in=h)
        xn = _rms_fwd(f"mix_norm_fwd{layer}", h, mix_norm[layer])
        sv["xn"] = xn
        if layer < LA:
            i_a = layer
            (zp,) = _gemm(
                f"gmlp_in{layer}", (S // TM, NDEV),
                [(xn, rows_full(TM)), (wl["win"], pl.BlockSpec((None, D, ZC), lambda i, e: (e, 0, 0)))],
                [(0, 1, NN)], [], [(_sds((S, 2 * D), F32), pl.BlockSpec((TM, ZC), lambda i, e: (i, e)))], _store)
            bst = a_b_s[i_a].T
            gated = _gmlp_fwd(f"gmlp_gate{layer}", zp, norm_v[i_a].reshape(1, D), a_w_s[i_a], bst)
            sv.update(zp=zp, gated=gated, bst=bst)
            h_mid = stacked_rows_gemm(f"gmlp_out{layer}", gated, wl["wout"], 0, [(h, tile(TM, TN_))],
                                      _store_add_extra, F32)
        else:
            i_b = layer - LA
            q = stacked_rows_gemm(f"attn_q{layer}", xn, wl["wqo"], 0, [(b_b_q[i_b].reshape(1, D), vec_tile)],
                                  _store_add_extra, BF16)
            attn, probs, sink_probs = _attn_fwd(f"attn_fwd{layer}", q, k_heads, v_heads, bias, b_sinks[i_b])
            sv.update(q=q, attn=attn, probs=probs, sink_probs=sink_probs)
            h_mid = stacked_rows_gemm(f"attn_o{layer}", attn, wl["wqo"], 1,
                                      [(h, tile(TM, TN_)), (b_b_o[i_b].reshape(1, D), vec_tile)],
                                      _store_add_extra, F32)
        relay_token = [gather_relay(layer + 1, [h_mid])] if layer + 1 < L else []
        fsv, h = ffn_forward(layer, wl, h_mid, str(layer), relay_token)
        sv.update(fsv)
        saved.append(sv)
        if layer == LA - 1:
            h_kv = h
            hn = _rms_fwd("kv_norm_fwd", h, kv_norm)

            def kv_ep(acc, ex, outs):
                val = acc + ex[0][...]
                for hh in range(NKV):
                    outs[0][hh] = val[:, hh * HEAD_DIM:(hh + 1) * HEAD_DIM].astype(BF16)
                    outs[1][hh] = val[:, (NKV + hh) * HEAD_DIM:(NKV + hh + 1) * HEAD_DIM].astype(BF16)

            k_heads, v_heads = _gemm(
                "kv_proj", (S // TM,),
                [(hn, pl.BlockSpec((TM, D), lambda i: (i, 0))),
                 (wl["wkv"], pl.BlockSpec((NDEV, DS, KVW), lambda i: (0, 0, 0)))],
                [(0, 1, NN, None, _stacked)], [(b_kv.reshape(1, KVW), pl.BlockSpec((1, KVW), lambda i: (0, 0)))],
                [(_sds((NKV, S, HEAD_DIM), BF16), pl.BlockSpec((NKV, TM, HEAD_DIM), lambda i: (0, i, 0)))] * 2,
                kv_ep)

    loss11, d, d_bf, g_final = _loss_bwd("loss_bwd", h, final_norm, loss_target.reshape(S, D))
    loss = lax.psum(loss11[0, 0], AXES)

    g_mix, g_ffn = [None] * L, [None] * L
    g_ws, g_bs, g_nv = [None] * LA, [None] * LA, [None] * LA
    g_bq, g_sink, g_bo = [None] * LB, [None] * LB, [None] * LB
    dbiases = []
    kv_parts = []
    g_kvn = g_bkv = None
    exchanges = [[] for _ in range(L)]
    pending = None
    grads_wkv = None
    newest = []

    def new_grads(l):
        return {key: lax.empty((NDEV, rows, width), BF16) for key, rows, width, _ in layer_arrays(l)}

    def exchange_begin(tag, l, gl, keys):
        grads = [gl[k] for k in keys]
        lands = [lax.empty((NCHIP,) + g.shape[1:], BF16) for g in grads]
        send, recv, grads, lands, tok = _sibling_start(f"rs_sibling_start{tag}", grads, lands, [])
        newest[:] = [tok]
        return dict(tag=tag, layer=l, keys=keys, grads=grads, lands=lands, send=send, recv=recv)

    def exchange_middle(st, dep):
        tag = st["tag"]
        grads, lands = _sibling_finish(f"rs_sibling_finish{tag}", st["grads"], st["lands"], st["send"], st["recv"], [dep])
        sums, own = [], []
        for t, key in enumerate(st["keys"]):
            s_, o_ = _pair_sum(f"pair_sum_{key}{tag}", grads[t], lands[t], where)
            sums.append(s_)
            own.append(o_)
        send, recv, sums, own, tok = _chips_start(f"rs_chips_start{tag}", sums, own, [])
        newest[:] = [tok]
        st.update(sums=sums, own=own, send2=send, recv2=recv)
        exchanges[st["layer"]].append(st)

    def exchange_end(st, dep):
        lands = _chips_finish(f"rs_chips_finish{st['tag']}", st["sums"], st["own"], st["send2"], st["recv2"], [dep])
        return dict(zip(st["keys"], lands))

    for layer in reversed(range(L)):
        sv, wl = saved[layer], weights[layer]
        tag = str(layer)
        gl = new_grads(layer)
        if grads_wkv is not None and layer == LA - 1:
            gl["wkv"] = grads_wkv
        def dhid_ep(acc, ex, outs):
            outs[0][0] = (acc * ex[0][0].astype(F32)).astype(BF16)
            outs[0][1] = (acc * ex[0][1].astype(F32)).astype(BF16)

        ab_spec = pl.BlockSpec((2, None, TM, F), lambda i, e: (0, e, i, 0))
        (dab,) = _gemm(
            f"ffn_dhid{tag}", (S // TM, NDEV),
            [(d_bf, rows_full(TM)), (wl["down"], pl.BlockSpec((None, F, D), lambda i, e: (e, 0, 0)))],
            [(0, 1, NT)], [(sv["ab"], ab_spec)], [(_sds((2, NDEV, S, F), BF16), ab_spec)], dhid_ep,
            deps=list(newest))
        if pending:
            exchange_middle(pending, dab)
        (gl["down"],) = _gemm(
            f"ffn_dwdown{tag}", (NDEV,),
            [(sv["hid"], pl.BlockSpec((None, S, F), lambda e: (e, 0, 0))),
             (d_bf, pl.BlockSpec((S, D), lambda e: (0, 0)))],
            [(0, 1, TN)], [(gl["down"], ANY)],
            [(_sds(gl["down"].shape, BF16), pl.BlockSpec((None, F, D), lambda e: (e, 0, 0)))],
            _store, aliases={2: 0}, deps=list(newest))
        (gl["gu"],) = _gemm(
            f"ffn_dwup{tag}", (2, NDEV),
            [(dab, pl.BlockSpec((None, None, S, F), lambda w, e: (w, e, 0, 0))),
             (sv["xf"], pl.BlockSpec((S, D), lambda w, e: (0, 0)))],
            [(0, 1, TN)], [(gl["gu"], ANY)],
            [(_sds(gl["gu"].shape, BF16), pl.BlockSpec((None, F, D), lambda w, e: (e, w, 0)))],
            _store, aliases={2: 0})
        ffn_group = exchange_begin(f"_ffn{tag}", layer, gl, ["gu", "down"])
        (dxf,) = _gemm(
            f"ffn_dx{tag}", (S // TM, D // TN_, 2 * NDEV // KC),
            [(dab.reshape(2 * NDEV // KC, KC, S, F), pl.BlockSpec((None, KC, TM, F), lambda i, j, k: (k, 0, i, 0))),
             (wl["gu"], pl.BlockSpec((KC, F, TN_), lambda i, j, k: (k % (NDEV // KC), k // (NDEV // KC), j)))],
            [(0, 1, NN, _pick(c), _pick(c)) for c in range(KC)], [],
            [(_sds((S, D), F32), pl.BlockSpec((TM, TN_), lambda i, j, k: (i, j)))],
            _store, nk=2 * NDEV // KC, acc_shape=(TM, TN_), deps=list(newest))
        exchange_middle(ffn_group, dxf)
        d, d_bf, g_ffn[layer], colsum = _rms_bwd(f"ffn_norm_bwd{tag}", sv["h_mid"], ffn_norm[layer], dxf, d,
                                                 deps=list(newest))
        if layer < LA:
            i_a = layer
            dgated = back_rows_gemm(f"gmlp_dgated{tag}", d_bf, wl["wout"], 0, F32)
            gl["wout"] = grad_rows_gemm(f"gmlp_dwout{tag}", sv["gated"], d_bf, gl["wout"], 0)
            dzp, g_ws[i_a], dbs, g_nv[i_a] = _gmlp_bwd(f"gmlp_bwd{tag}", sv["zp"], dgated,
                                                       norm_v[i_a].reshape(1, D), a_w_s[i_a], sv["bst"])
            g_bs[i_a] = dbs[:, 0, :]
            (gl["win"],) = _gemm(
                f"gmlp_dwin{tag}", (NDEV, D // TS),
                [(sv["xn"], pl.BlockSpec((S, TS), lambda e, i: (0, i))),
                 (dzp, pl.BlockSpec((S, ZC), lambda e, i: (0, e)))],
                [(0, 1, TN)], [(gl["win"], ANY)],
                [(_sds(gl["win"].shape, BF16), pl.BlockSpec((None, TS, ZC), lambda e, i: (e, i, 0)))],
                _store, aliases={2: 0})
            (dxn,) = _gemm(
                f"gmlp_dx{tag}", (S // TM, D // TN_, NDEV // KC),
                [(dzp, pl.BlockSpec((TM, KC * ZC), lambda i, j, k: (i, k))),
                 (wl["win"], pl.BlockSpec((KC, TN_, ZC), lambda i, j, k: (k, j, 0)))],
                [(0, 1, NT, _cols(c, ZC), _pick(c)) for c in range(KC)], [],
                [(_sds((S, D), F32), pl.BlockSpec((TM, TN_), lambda i, j, k: (i, j)))],
                _store, nk=NDEV // KC, acc_shape=(TM, TN_))
        else:
            i_b = layer - LA
            g_bo[i_b] = colsum
            dattn = back_rows_gemm(f"attn_dout{tag}", d_bf, wl["wqo"], 1, BF16)
            gl["wqo"] = grad_rows_gemm(f"attn_dwo{tag}", sv["attn"], d_bf, gl["wqo"], 1)
            dq, g_bq[i_b], dkc, dkp, dvc, dvp, dbias, dsink = _attn_bwd(
                f"attn_bwd{tag}", sv["q"], k_heads, v_heads, dattn, sv["probs"], sv["sink_probs"])
            kv_parts.append((dkc, dkp, dvc, dvp))
            g_sink[i_b] = dsink.reshape(NH)
            dbiases.append(dbias.reshape(NH, BLOCK * 2 * BLOCK))
            gl["wqo"] = grad_rows_gemm(f"attn_dwq{tag}", sv["xn"], dq, gl["wqo"], 0)
            dxn = back_rows_gemm(f"attn_dx{tag}", dq, wl["wqo"], 0, F32)
        d, d_bf, g_mix[layer], _ = _rms_bwd(f"mix_norm_bwd{tag}", sv["h_in"], mix_norm[layer], dxn, d)
        pending = exchange_begin(f"_mix{tag}", layer, gl, [k for k in gl if k not in ("gu", "down")])
        if layer == LA:
            wkv = weights[LA - 1]["wkv"]
            dkv, g_bkv = _kv_grad("kv_grad", kv_parts)
            (grads_wkv,) = _gemm(
                "kv_dw", (NDEV,),
                [(hn, pl.BlockSpec((S, DS), lambda e: (0, e))), (dkv, pl.BlockSpec((S, KVW), lambda e: (0, 0)))],
                [(0, 1, TN)], [(lax.empty((NDEV, DS, KVW), BF16), ANY)],
                [(_sds((NDEV, DS, KVW), BF16), pl.BlockSpec((None, DS, KVW), lambda e: (e, 0, 0)))],
                _store, aliases={2: 0}, deps=list(newest))
            (dhn,) = _gemm(
                "kv_dx", (S // TM, NDEV),
                [(dkv, pl.BlockSpec((TM, KVW), lambda i, e: (i, 0))),
                 (wkv, pl.BlockSpec((None, DS, KVW), lambda i, e: (e, 0, 0)))],
                [(0, 1, NT)], [], [(_sds((S, D), F32), pl.BlockSpec((TM, DS), lambda i, e: (i, e)))], _store)
            d, d_bf, g_kvn, _ = _rms_bwd("kv_norm_bwd", h_kv, kv_norm, dhn, d)
    grad_x = d.reshape(x.shape)

    exchange_middle(pending, d)

    g_rel = _bias_grad("bias_grad", dbiases, buckets)
    small_local = _pack([jnp.concatenate(g_mix, axis=0), jnp.concatenate(g_ffn, axis=0), jnp.stack(g_ws),
                         jnp.stack(g_bs), g_kvn, g_bkv, jnp.concatenate(g_bq, axis=0), jnp.stack(g_sink),
                         jnp.concatenate(g_bo, axis=0), g_rel, g_final, jnp.concatenate(g_nv, axis=0)])
    small_slot = _cast_into("put_small_grads", small_local.reshape((1,) + small_local.shape), 0,
                            lax.empty((NDEV,) + small_local.shape, F32), 0, me1)
    s_send, s_recv, s_bufs, s_tok = _gather_start("gather_small_start", [small_slot], list(newest))

    results = {}
    after = [s_tok]

    def upd(pname, w, m, v, l, li, lands, row0):
        w3 = w if w.ndim == 3 else w.reshape((1,) + w.shape)
        prev = results.get(pname) or [lax.empty(w3.shape, F32) for _ in range(4)]
        results[pname] = _adamw_shard(f"adamw_{pname}{l}", w3, m.reshape(w3.shape), v.reshape(w3.shape), lands,
                                      row0, li, prev, deps=list(after))
        after[:] = [results[pname][0]]

    for l in reversed(range(L)):
        for st in exchanges[l]:
            lands = exchange_end(st, after[0])
            if "gu" in lands:
                upd("ffn_w_gate", gate_t, tr3(m_ffn_w_gate), tr3(v_ffn_w_gate), l, l, lands["gu"], 0)
                upd("ffn_w_up", up_t, tr3(m_ffn_w_up), tr3(v_ffn_w_up), l, l, lands["gu"], F)
                upd("ffn_w_down", ffn_w_down, m_ffn_w_down, v_ffn_w_down, l, l, lands["down"], 0)
            if "win" in lands:
                upd("a_w_in", a_w_in, m_a_w_in, v_a_w_in, l, l, lands["win"], 0)
                upd("a_w_out", a_w_out, m_a_w_out, v_a_w_out, l, l, lands["wout"], 0)
            if "wkv" in lands:
                upd("w_kv", w_kv, m_w_kv, v_w_kv, l, 0, lands["wkv"], 0)
            if "wqo" in lands:
                upd("b_w_q", b_w_q, m_b_w_q, v_b_w_q, l, l - LA, lands["wqo"], 0)
                upd("b_w_o", b_w_o, m_b_w_o, v_b_w_o, l, l - LA, lands["wqo"], DS)
    for pname in ("ffn_w_gate", "ffn_w_up"):
        results[pname] = [tr3(r) for r in results[pname]]
    results["w_kv"] = [r.reshape(w_kv.shape) for r in results["w_kv"]]

    small_w = [mix_norm, ffn_norm, a_w_s, a_b_s, kv_norm, b_kv, b_b_q, b_sinks, b_b_o, rel_bias, final_norm]
    small_m = [m_mix_norm, m_ffn_norm, m_a_w_s, m_a_b_s, m_kv_norm, m_b_kv, m_b_b_q, m_b_sinks, m_b_b_o, m_rel_bias,
               m_final_norm]
    small_v = [v_mix_norm, v_ffn_norm, v_a_w_s, v_a_b_s, v_kv_norm, v_b_kv, v_b_b_q, v_b_sinks, v_b_b_o, v_rel_bias,
               v_final_norm]
    shapes = [w.shape for w in small_w] + [(LA, D)]
    s_fsend, s_frecv, s_bufs = _gather_forward("gather_small_forward", s_bufs, s_send, s_recv, list(after))
    (small_all,) = _gather_finish("gather_small_finish", s_bufs, s_send, s_recv, s_fsend, s_frecv)
    small_sum = _sum_devices("sum_small_grads", small_all)
    small_g = _unpack(small_sum, shapes)
    g_normv = lax.dynamic_slice_in_dim(small_g[-1], me * DS, DS, axis=1)
    small_g = small_g[:-1] + [g_normv]
    small_w, small_m, small_v = small_w + [a_norm_v], small_m + [m_a_norm_v], small_v + [v_a_norm_v]
    shapes = [w.shape for w in small_w]
    s_delta, s_m, s_v = _adamw_flat("adamw_small", _pack(small_w), _pack(small_g), _pack(small_m), _pack(small_v))
    s_delta, s_m, s_v = _unpack(s_delta, shapes), _unpack(s_m, shapes), _unpack(s_v, shapes)

    names = ["mix_norm", "ffn_norm", "a_w_in", "a_norm_v", "a_w_s", "a_b_s", "a_w_out", "kv_norm", "w_kv", "b_kv",
             "b_w_q", "b_b_q", "b_sinks", "b_w_o", "b_b_o", "rel_bias", "ffn_w_gate", "ffn_w_up", "ffn_w_down",
             "final_norm"]
    small_names = ["mix_norm", "ffn_norm", "a_w_s", "a_b_s", "kv_norm", "b_kv", "b_b_q", "b_sinks", "b_b_o", "rel_bias",
                   "final_norm", "a_norm_v"]
    res = {}
    for idx, nm in enumerate(small_names):
        res[nm] = (small_g[idx].reshape(shapes[idx]), s_delta[idx], s_m[idx], s_v[idx])
    for nm, u in results.items():
        res[nm] = tuple(u)
    out = [loss, grad_x]
    for part in range(4):
        out += [res[nm][part] for nm in names]
    return tuple(out)
```

```python
import math

import numpy as np
import jax
import jax.numpy as jnp
from jax import lax
from jax.experimental import pallas as pl
from jax.experimental.pallas import tpu as pltpu

F32 = jnp.float32
BF16 = jnp.bfloat16
AXES = ("x", "y", "c")
NDEV = 8
NCHIP = 4
CHUNK = 128
GROUPS = 8
HEAD_DIM = 64
KV_GROUP = 8
BLOCK = 128
N_BUCKETS = 32
MAX_DISTANCE = 128
RMS_EPS = 1e-5
NEG_INF = -1e30
ADAM_LR, ADAM_B1, ADAM_B2, ADAM_EPS, ADAM_WD, ADAM_STEP = 0.001, 0.9, 0.999, 1e-08, 0.01, 10
VMEM_LIMIT_BYTES = 56 * 1024 * 1024

NN = (((1,), (0,)), ((), ()))
NT = (((1,), (1,)), ((), ()))
TN = (((0,), (0,)), ((), ()))
ANY = pl.BlockSpec(memory_space=pl.ANY)
HBM = pl.BlockSpec(memory_space=pltpu.HBM)
SEM = pl.BlockSpec(memory_space=pltpu.SEMAPHORE)
MESH = pl.DeviceIdType.MESH
EFFECT = pltpu.SideEffectType.DATAFLOW_SIDE_EFFECTING


def _pcall(body, *, name, out_shape, in_specs, out_specs, grid=(), scratch=(), aliases=None, prefetch=0, deps=()):
    n_in, n_dep = len(in_specs), len(deps)
    if n_dep:
        inner = body

        def body(*refs):
            return inner(*refs[:prefetch + n_in], *refs[prefetch + n_in + n_dep:])

        in_specs = list(in_specs) + [ANY] * n_dep
    params = dict(vmem_limit_bytes=VMEM_LIMIT_BYTES)
    if grid:
        params["dimension_semantics"] = ("arbitrary",) * len(grid)
    kw = dict(name=name, out_shape=out_shape, compiler_params=pltpu.CompilerParams(**params),
              input_output_aliases=aliases or {})
    if prefetch:
        kw["grid_spec"] = pltpu.PrefetchScalarGridSpec(num_scalar_prefetch=prefetch, grid=grid, in_specs=in_specs,
                                                       out_specs=out_specs, scratch_shapes=list(scratch))
    else:
        kw.update(grid=grid, in_specs=in_specs, out_specs=out_specs, scratch_shapes=list(scratch))
    call = pl.pallas_call(body, **kw)
    return lambda *args: call(*args, *deps)


def _sds(shape, dtype):
    return jax.ShapeDtypeStruct(tuple(shape), dtype)


def _position():
    x, y, c = lax.axis_index("x"), lax.axis_index("y"), lax.axis_index("c")
    chips = [(1 - x, y), (x, 1 - y), (1 - x, 1 - y)]
    return x, y, c, chips


def _slot(px, py, pc):
    return 4 * px + 2 * py + pc


def _remote(ref_src, ref_dst, send, recv, to):
    return pltpu.make_async_remote_copy(src_ref=ref_src, dst_ref=ref_dst, send_sem=send, recv_sem=recv,
                                        device_id=to, device_id_type=MESH)


def _hbm(arrays):
    return [pltpu.with_memory_space_constraint(a, pltpu.HBM) for a in arrays]


def _split_call(body, name, out_shape, in_specs, out_specs, aliases):
    return pl.pallas_call(body, name=name, out_shape=out_shape, in_specs=in_specs, out_specs=out_specs,
                          input_output_aliases=aliases, compiler_params=pltpu.CompilerParams(has_side_effects=EFFECT))


def _token_shape():
    return _sds((8, 128), F32)


def _gather_start(name, bufs, deps):
    n, nd = len(bufs), len(deps)

    def body(*refs):
        ins, send, recv, token = refs[:n], refs[n + nd], refs[n + nd + 1], refs[2 * n + nd + 2]
        x, y, c, chips = _position()
        peers = [(x, y, 1 - c)] + [(*chip, c) for chip in chips]
        for t in range(n):
            mine = ins[t].at[_slot(x, y, c)]
            for k, peer in enumerate(peers):
                _remote(mine, mine, send.at[4 * t + k], recv.at[4 * t + k], peer).start()
        token[...] = jnp.zeros_like(token)

    res = _split_call(
        body, name,
        (pltpu.SemaphoreType.DMA((4 * n,)), pltpu.SemaphoreType.DMA((4 * n,)), *[pltpu.HBM(b.shape, b.dtype) for b in bufs],
         _token_shape()),
        [HBM] * n + [ANY] * nd, (SEM, SEM, *[HBM] * n, pl.BlockSpec(memory_space=pltpu.VMEM)),
        {t: 2 + t for t in range(n)})(*_hbm(bufs), *deps)
    return res[0], res[1], list(res[2:2 + n]), res[2 + n]


def _gather_forward(name, bufs, send, recv, deps):
    n, nd = len(bufs), len(deps)

    def body(*refs):
        ins, send_in, recv_in = refs[:n], refs[n], refs[n + 1]
        fsend, frecv = refs[n + 2 + nd], refs[n + 3 + nd]
        x, y, c, chips = _position()
        for j, chip in enumerate(chips):
            for t in range(n):
                blk = ins[t].at[_slot(*chip, c)]
                _remote(blk, blk, send_in.at[4 * t + 1 + j], recv_in.at[4 * t + 1 + j], (*chip, c)).wait_recv()
                _remote(blk, blk, fsend.at[3 * t + j], frecv.at[3 * t + j], (x, y, 1 - c)).start()

    res = _split_call(
        body, name,
        (pltpu.SemaphoreType.DMA((3 * n,)), pltpu.SemaphoreType.DMA((3 * n,)), *[pltpu.HBM(b.shape, b.dtype) for b in bufs]),
        [HBM] * n + [SEM, SEM] + [ANY] * nd, (SEM, SEM, *[HBM] * n),
        {t: 2 + t for t in range(n)})(*_hbm(bufs), send, recv, *deps)
    return res[0], res[1], list(res[2:])


def _gather_finish(name, bufs, send, recv, fsend, frecv):
    n = len(bufs)

    def body(*refs):
        ins, send_in, recv_in, fs_in, fr_in = refs[:n], refs[n], refs[n + 1], refs[n + 2], refs[n + 3]
        x, y, c, chips = _position()
        sibling = (x, y, 1 - c)
        peers = [sibling] + [(*chip, c) for chip in chips]
        for t in range(n):
            blk = ins[t].at[_slot(x, y, 1 - c)]
            _remote(blk, blk, send_in.at[4 * t], recv_in.at[4 * t], sibling).wait_recv()
            for j, chip in enumerate(chips):
                blk = ins[t].at[_slot(*chip, 1 - c)]
                _remote(blk, blk, fs_in.at[3 * t + j], fr_in.at[3 * t + j], sibling).wait_recv()
            mine = ins[t].at[_slot(x, y, c)]
            for k, peer in enumerate(peers):
                _remote(mine, mine, send_in.at[4 * t + k], recv_in.at[4 * t + k], peer).wait_send()
            for j, chip in enumerate(chips):
                blk = ins[t].at[_slot(*chip, c)]
                _remote(blk, blk, fs_in.at[3 * t + j], fr_in.at[3 * t + j], sibling).wait_send()

    res = _split_call(
        body, name, tuple(pltpu.HBM(b.shape, b.dtype) for b in bufs),
        [HBM] * n + [SEM] * 4, tuple([HBM] * n), {t: t for t in range(n)})(*_hbm(bufs), send, recv, fsend, frecv)
    return list(res)


def _halves(ref):
    rows = ref.shape[0] // 2
    return ref.at[pl.ds(0, rows)], ref.at[pl.ds(rows, rows)]


def _relay_start(name, bufs, deps):
    n, nd = len(bufs), len(deps)

    def body(*refs):
        ins, send, recv, token = refs[:n], refs[n + nd], refs[n + nd + 1], refs[2 * n + nd + 2]
        x, y, c, _ = _position()
        peers = [(x, y, 1 - c), (1 - x, y, c), (x, 1 - y, c)]
        for t in range(n):
            mine = ins[t].at[_slot(x, y, c)]
            for k, peer in enumerate(peers):
                _remote(mine, mine, send.at[3 * t + k], recv.at[3 * t + k], peer).start()
        token[...] = jnp.zeros_like(token)

    res = _split_call(
        body, name,
        (pltpu.SemaphoreType.DMA((3 * n,)), pltpu.SemaphoreType.DMA((3 * n,)), *[pltpu.HBM(b.shape, b.dtype) for b in bufs],
         _token_shape()),
        [HBM] * n + [ANY] * nd, (SEM, SEM, *[HBM] * n, pl.BlockSpec(memory_space=pltpu.VMEM)),
        {t: 2 + t for t in range(n)})(*_hbm(bufs), *deps)
    return res[0], res[1], list(res[2:2 + n]), res[2 + n]


def _relay_neighbors(name, bufs, send, recv, deps):
    n, nd = len(bufs), len(deps)

    def body(*refs):
        ins, send_in, recv_in = refs[:n], refs[n], refs[n + 1]
        fsend, frecv, token = refs[n + 2 + nd], refs[n + 3 + nd], refs[2 * n + 4 + nd]
        x, y, c, _ = _position()
        sibling, xn, yn = (x, y, 1 - c), (1 - x, y, c), (x, 1 - y, c)
        for t in range(n):
            blk = ins[t].at[_slot(*xn)]
            _remote(blk, blk, send_in.at[3 * t + 1], recv_in.at[3 * t + 1], xn).wait_recv()
            _remote(blk, blk, fsend.at[4 * t], frecv.at[4 * t], sibling).start()
            half = _halves(blk)[0]
            _remote(half, half, fsend.at[4 * t + 1], frecv.at[4 * t + 1], yn).start()
        for t in range(n):
            blk = ins[t].at[_slot(*yn)]
            _remote(blk, blk, send_in.at[3 * t + 2], recv_in.at[3 * t + 2], yn).wait_recv()
            _remote(blk, blk, fsend.at[4 * t + 2], frecv.at[4 * t + 2], sibling).start()
            half = _halves(blk)[1]
            _remote(half, half, fsend.at[4 * t + 3], frecv.at[4 * t + 3], xn).start()
        token[...] = jnp.zeros_like(token)

    res = _split_call(
        body, name,
        (pltpu.SemaphoreType.DMA((4 * n,)), pltpu.SemaphoreType.DMA((4 * n,)), *[pltpu.HBM(b.shape, b.dtype) for b in bufs],
         _token_shape()),
        [HBM] * n + [SEM, SEM] + [ANY] * nd, (SEM, SEM, *[HBM] * n, pl.BlockSpec(memory_space=pltpu.VMEM)),
        {t: 2 + t for t in range(n)})(*_hbm(bufs), send, recv, *deps)
    return res[0], res[1], list(res[2:2 + n]), res[2 + n]


def _relay_diagonal(name, bufs, fsend, frecv, deps):
    n, nd = len(bufs), len(deps)

    def body(*refs):
        ins, fs_in, fr_in = refs[:n], refs[n], refs[n + 1]
        gsend, grecv = refs[n + 2 + nd], refs[n + 3 + nd]
        x, y, c, _ = _position()
        for t in range(n):
            blk = ins[t].at[_slot(1 - x, 1 - y, c)]
            first, second = _halves(blk)
            _remote(first, first, fs_in.at[4 * t + 1], fr_in.at[4 * t + 1], (x, 1 - y, c)).wait_recv()
            _remote(second, second, fs_in.at[4 * t + 3], fr_in.at[4 * t + 3], (1 - x, y, c)).wait_recv()
            _remote(blk, blk, gsend.at[t], grecv.at[t], (x, y, 1 - c)).start()

    res = _split_call(
        body, name,
        (pltpu.SemaphoreType.DMA((n,)), pltpu.SemaphoreType.DMA((n,)), *[pltpu.HBM(b.shape, b.dtype) for b in bufs]),
        [HBM] * n + [SEM, SEM] + [ANY] * nd, (SEM, SEM, *[HBM] * n),
        {t: 2 + t for t in range(n)})(*_hbm(bufs), fsend, frecv, *deps)
    return res[0], res[1], list(res[2:])


def _relay_finish(name, bufs, send, recv, fsend, frecv, gsend, grecv):
    n = len(bufs)

    def body(*refs):
        ins = refs[:n]
        send_in, recv_in, fs_in, fr_in, gs_in, gr_in = refs[n:n + 6]
        x, y, c, _ = _position()
        sibling, xn, yn = (x, y, 1 - c), (1 - x, y, c), (x, 1 - y, c)
        for t in range(n):
            blk = ins[t].at[_slot(x, y, 1 - c)]
            _remote(blk, blk, send_in.at[3 * t], recv_in.at[3 * t], sibling).wait_recv()
            blk = ins[t].at[_slot(1 - x, y, 1 - c)]
            _remote(blk, blk, fs_in.at[4 * t], fr_in.at[4 * t], sibling).wait_recv()
            blk = ins[t].at[_slot(x, 1 - y, 1 - c)]
            _remote(blk, blk, fs_in.at[4 * t + 2], fr_in.at[4 * t + 2], sibling).wait_recv()
            blk = ins[t].at[_slot(1 - x, 1 - y, 1 - c)]
            _remote(blk, blk, gs_in.at[t], gr_in.at[t], sibling).wait_recv()
            mine = ins[t].at[_slot(x, y, c)]
            for k, peer in enumerate([sibling, xn, yn]):
                _remote(mine, mine, send_in.at[3 * t + k], recv_in.at[3 * t + k], peer).wait_send()
            bx, by = ins[t].at[_slot(*xn)], ins[t].at[_slot(*yn)]
            _remote(bx, bx, fs_in.at[4 * t], fr_in.at[4 * t], sibling).wait_send()
            _remote(_halves(bx)[0], _halves(bx)[0], fs_in.at[4 * t + 1], fr_in.at[4 * t + 1], yn).wait_send()
            _remote(by, by, fs_in.at[4 * t + 2], fr_in.at[4 * t + 2], sibling).wait_send()
            _remote(_halves(by)[1], _halves(by)[1], fs_in.at[4 * t + 3], fr_in.at[4 * t + 3], xn).wait_send()
            bd = ins[t].at[_slot(1 - x, 1 - y, c)]
            _remote(bd, bd, gs_in.at[t], gr_in.at[t], sibling).wait_send()

    res = _split_call(
        body, name, tuple(pltpu.HBM(b.shape, b.dtype) for b in bufs),
        [HBM] * n + [SEM] * 6, tuple([HBM] * n), {t: t for t in range(n)})(
            *_hbm(bufs), send, recv, fsend, frecv, gsend, grecv)
    return list(res)


def _sibling_start(name, grads, lands, deps):
    n, nd = len(grads), len(deps)

    def body(*refs):
        g_in, l_in = refs[:n], refs[n:2 * n]
        send, recv, token = refs[2 * n + nd], refs[2 * n + nd + 1], refs[4 * n + nd + 2]
        x, y, c, _ = _position()
        for t in range(n):
            for k in range(NCHIP):
                _remote(g_in[t].at[2 * k + (1 - c)], l_in[t].at[k], send.at[NCHIP * t + k], recv.at[NCHIP * t + k],
                        (x, y, 1 - c)).start()
        token[...] = jnp.zeros_like(token)

    both = list(grads) + list(lands)
    res = _split_call(
        body, name,
        (pltpu.SemaphoreType.DMA((NCHIP * n,)), pltpu.SemaphoreType.DMA((NCHIP * n,)),
         *[pltpu.HBM(b.shape, b.dtype) for b in both], _token_shape()),
        [HBM] * (2 * n) + [ANY] * nd, (SEM, SEM, *[HBM] * (2 * n), pl.BlockSpec(memory_space=pltpu.VMEM)),
        {t: 2 + t for t in range(2 * n)})(*_hbm(both), *deps)
    return res[0], res[1], list(res[2:2 + n]), list(res[2 + n:2 + 2 * n]), res[2 + 2 * n]


def _sibling_finish(name, grads, lands, send, recv, deps):
    n, nd = len(grads), len(deps)

    def body(*refs):
        g_in, l_in, send_in, recv_in = refs[:n], refs[n:2 * n], refs[2 * n], refs[2 * n + 1]
        x, y, c, _ = _position()
        for t in range(n):
            for k in range(NCHIP):
                cp = _remote(g_in[t].at[2 * k + (1 - c)], l_in[t].at[k], send_in.at[NCHIP * t + k],
                             recv_in.at[NCHIP * t + k], (x, y, 1 - c))
                cp.wait_send()
                cp.wait_recv()

    both = list(grads) + list(lands)
    res = _split_call(
        body, name, tuple(pltpu.HBM(b.shape, b.dtype) for b in both),
        [HBM] * (2 * n) + [SEM, SEM] + [ANY] * nd, tuple([HBM] * (2 * n)),
        {t: t for t in range(2 * n)})(*_hbm(both), send, recv, *deps)
    return list(res[:n]), list(res[n:])


def _chips_start(name, parts, lands, deps):
    n, nd = len(parts), len(deps)

    def body(*refs):
        p_in, l_in = refs[:n], refs[n:2 * n]
        send, recv, token = refs[2 * n + nd], refs[2 * n + nd + 1], refs[4 * n + nd + 2]
        x, y, c, chips = _position()
        for t in range(n):
            for j, chip in enumerate(chips):
                _remote(p_in[t].at[2 * chip[0] + chip[1]], l_in[t].at[2 * x + y], send.at[3 * t + j], recv.at[3 * t + j],
                        (*chip, c)).start()
        token[...] = jnp.zeros_like(token)

    both = list(parts) + list(lands)
    res = _split_call(
        body, name,
        (pltpu.SemaphoreType.DMA((3 * n,)), pltpu.SemaphoreType.DMA((3 * n,)), *[pltpu.HBM(b.shape, b.dtype) for b in both],
         _token_shape()),
        [HBM] * (2 * n) + [ANY] * nd, (SEM, SEM, *[HBM] * (2 * n), pl.BlockSpec(memory_space=pltpu.VMEM)),
        {t: 2 + t for t in range(2 * n)})(*_hbm(both), *deps)
    return res[0], res[1], list(res[2:2 + n]), list(res[2 + n:2 + 2 * n]), res[2 + 2 * n]


def _chips_finish(name, parts, lands, send, recv, deps):
    n, nd = len(parts), len(deps)

    def body(*refs):
        p_in, l_in, send_in, recv_in = refs[:n], refs[n:2 * n], refs[2 * n], refs[2 * n + 1]
        x, y, c, chips = _position()
        for t in range(n):
            for j, chip in enumerate(chips):
                k = 2 * chip[0] + chip[1]
                _remote(p_in[t].at[k], l_in[t].at[k], send_in.at[3 * t + j], recv_in.at[3 * t + j], (*chip, c)).wait_recv()
                _remote(p_in[t].at[k], l_in[t].at[2 * x + y], send_in.at[3 * t + j], recv_in.at[3 * t + j],
                        (*chip, c)).wait_send()

    both = list(parts) + list(lands)
    res = _split_call(
        body, name, tuple(pltpu.HBM(b.shape, b.dtype) for b in both),
        [HBM] * (2 * n) + [SEM, SEM] + [ANY] * nd, tuple([HBM] * (2 * n)),
        {t: t for t in range(2 * n)})(*_hbm(both), send, recv, *deps)
    return list(res[n:])


def _all_gather(name, shards):
    n = len(shards)

    def body(*refs):
        src, dst = refs[:n], refs[n:2 * n]
        send_sems, recv_sems, local_sems = refs[2 * n:]
        x, y, c, chips = _position()
        me, sibling = (x, y, c), (x, y, 1 - c)

        def copy(t, k, block, to, from_shard=False):
            slot = dst[t].at[_slot(*block)]
            return _remote(src[t] if from_shard else slot, slot, send_sems.at[t, k], recv_sems.at[t, k], to)

        mine = [pltpu.make_async_copy(src[t], dst[t].at[_slot(x, y, c)], local_sems.at[t]) for t in range(n)]
        first, passed = [], []
        for t in range(n):
            mine[t].start()
            first.append(copy(t, 0, me, sibling, True))
            first += [copy(t, 1 + j, me, (*chip, c), True) for j, chip in enumerate(chips)]
        for cp in first:
            cp.start()
        for j, chip in enumerate(chips):
            for t in range(n):
                copy(t, 1 + j, (*chip, c), me).wait_recv()
                fwd = copy(t, 4 + j, (*chip, c), sibling)
                fwd.start()
                passed.append(fwd)
        for t in range(n):
            copy(t, 0, sibling, me).wait_recv()
            for j, chip in enumerate(chips):
                copy(t, 4 + j, (*chip, 1 - c), me).wait_recv()
        for cp in first + passed:
            cp.wait_send()
        for t in range(n):
            mine[t].wait()

    outs = _pcall(
        body, name=name, out_shape=[_sds((NDEV,) + s.shape, s.dtype) for s in shards],
        in_specs=[ANY] * n, out_specs=[ANY] * n,
        scratch=[pltpu.SemaphoreType.DMA((n, 7)), pltpu.SemaphoreType.DMA((n, 7)), pltpu.SemaphoreType.DMA((n,))],
    )(*shards)
    return list(outs)


def _pair_sum(name, grad, recv, where):
    _, r, w = grad.shape
    tr = _row_tile(r, w, budget=4 * 1024 * 1024)
    g4 = grad.reshape(NCHIP, 2, r, w)

    def body(where_ref, g_ref, r_ref, o_ref, own_ref):
        val = (g_ref[...].astype(F32) + r_ref[...].astype(F32)).astype(o_ref.dtype)
        o_ref[...] = val

        @pl.when(pl.program_id(1) == where_ref[1])
        def _():
            own_ref[...] = val

    out = _sds((NCHIP, r, w), grad.dtype)
    return _pcall(
        body, name=name, out_shape=[out, out], grid=(r // tr, NCHIP), prefetch=1,
        in_specs=[pl.BlockSpec((None, None, tr, w), lambda i, k, wr: (k, wr[0], i, 0)),
                  pl.BlockSpec((None, tr, w), lambda i, k, wr: (k, i, 0))],
        out_specs=[pl.BlockSpec((None, tr, w), lambda i, k, wr: (k, i, 0)),
                   pl.BlockSpec((None, tr, w), lambda i, k, wr: (wr[1], i, 0))],
    )(where, g4, recv)


def _row_tile(rows, width, budget=2 * 1024 * 1024):
    best = None
    for t in range(16, rows + 1, 16):
        if rows % t == 0 and t * width * 4 <= budget:
            best = t
    if best is None and rows * width * 4 <= budget:
        best = rows
    assert best is not None, (rows, width)
    return best


def _gemm(name, grid, operands, prods, extras, outs, epilogue, *, nk=1, acc_shape=None, aliases=None, separate=False,
          deps=()):
    n_op, n_ex, n_out = len(operands), len(extras), len(outs)

    def body(*refs):
        ops, ex, out_refs = refs[:n_op], refs[n_op:n_op + n_ex], refs[n_op + n_ex:n_op + n_ex + n_out]
        parts = []
        for pr in prods:
            a, b = ops[pr[0]], ops[pr[1]]
            av = pr[3](a) if len(pr) > 3 and pr[3] else a[...]
            bv = pr[4](b) if len(pr) > 4 and pr[4] else b[...]
            parts.append(lax.dot_general(av, bv, pr[2], preferred_element_type=F32))
        if separate:
            epilogue(parts, ex, out_refs)
            return
        part = parts[0]
        for p in parts[1:]:
            part = part + p
        if nk == 1:
            epilogue(part, ex, out_refs)
        else:
            acc = refs[-1]
            k = pl.program_id(len(grid) - 1)

            @pl.when(k == 0)
            def _():
                acc[...] = part

            @pl.when(k > 0)
            def _():
                acc[...] += part

            @pl.when(k == nk - 1)
            def _():
                epilogue(acc[...], ex, out_refs)

    res = _pcall(
        body, name=name, out_shape=[o[0] for o in outs], grid=grid,
        in_specs=[o[1] for o in operands] + [e[1] for e in extras], out_specs=[o[1] for o in outs],
        scratch=[pltpu.VMEM(acc_shape, F32)] if nk > 1 else [], aliases=aliases, deps=deps,
    )(*[o[0] for o in operands], *[e[0] for e in extras])
    return list(res)


def _store(acc, ex, outs):
    outs[0][...] = acc.astype(outs[0].dtype)


def _store_add_extra(acc, ex, outs):
    v = acc
    for e in ex:
        v = v + e[...]
    outs[0][...] = v.astype(outs[0].dtype)


def _stacked(ref):
    b = ref[...]
    return b.reshape(b.shape[0] * b.shape[1], b.shape[2])


def _pick(c):
    return lambda ref: ref[c]


def _cols(c, width):
    return lambda ref: ref[:, c * width:(c + 1) * width]


def _gelu_parts(z):
    c = math.sqrt(2.0 / math.pi)
    t = jnp.tanh(c * (z + 0.044715 * (z * z * z)))
    val = 0.5 * z * (1.0 + t)
    grad = 0.5 * (1.0 + t) + 0.5 * z * (1.0 - t * t) * (c * (1.0 + 3.0 * 0.044715 * z * z))
    return val, grad


def _rms_fwd(name, h, g, deps=()):
    s, d = h.shape
    tr = _row_tile(s, d)

    def body(h_ref, g_ref, o_ref):
        hv = h_ref[...]
        r = lax.rsqrt(jnp.mean(hv * hv, axis=-1, keepdims=True) + RMS_EPS)
        o_ref[...] = (hv * r * g_ref[...]).astype(o_ref.dtype)

    return _pcall(
        body, name=name, out_shape=_sds((s, d), BF16), grid=(s // tr,),
        in_specs=[pl.BlockSpec((tr, d), lambda i: (i, 0)), pl.BlockSpec((1, d), lambda i: (0, 0))],
        out_specs=pl.BlockSpec((tr, d), lambda i: (i, 0)), deps=deps,
    )(h, g.reshape(1, d))


def _accumulate(ref, val, first):
    @pl.when(first)
    def _():
        ref[...] = val

    @pl.when(jnp.logical_not(first))
    def _():
        ref[...] += val


def _rms_bwd(name, h, g, dy, res, deps=()):
    s, d = h.shape
    tr = _row_tile(s, d, budget=2 * 1024 * 1024)

    def body(h_ref, g_ref, dy_ref, res_ref, dh_ref, dhb_ref, dg_ref, cs_ref):
        hv = h_ref[...]
        r = lax.rsqrt(jnp.mean(hv * hv, axis=-1, keepdims=True) + RMS_EPS)
        xhat = hv * r
        dyv = dy_ref[...]
        dxh = dyv * g_ref[...]
        dh = res_ref[...] + r * (dxh - xhat * jnp.mean(dxh * xhat, axis=-1, keepdims=True))
        dh_ref[...] = dh
        dhb_ref[...] = dh.astype(BF16)
        first = pl.program_id(0) == 0
        _accumulate(dg_ref, jnp.sum(dyv * xhat, axis=0, keepdims=True), first)
        _accumulate(cs_ref, jnp.sum(dh, axis=0, keepdims=True), first)

    row = pl.BlockSpec((tr, d), lambda i: (i, 0))
    vec = pl.BlockSpec((1, d), lambda i: (0, 0))
    return _pcall(
        body, name=name, out_shape=[_sds((s, d), F32), _sds((s, d), BF16), _sds((1, d), F32), _sds((1, d), F32)],
        grid=(s // tr,), in_specs=[row, vec, row, row], out_specs=[row, row, vec, vec], deps=deps,
    )(h, g.reshape(1, d), dy, res)


def _loss_bwd(name, h, g, target):
    s, d = h.shape
    tr = _row_tile(s, d, budget=1024 * 1024)

    def body(h_ref, g_ref, t_ref, loss_ref, dh_ref, dhb_ref, dg_ref):
        hv = h_ref[...]
        r = lax.rsqrt(jnp.mean(hv * hv, axis=-1, keepdims=True) + RMS_EPS)
        xhat = hv * r
        diff = xhat * g_ref[...] - t_ref[...]
        part = jnp.sum(jnp.sum(diff * diff, axis=1, keepdims=True), axis=0, keepdims=True) * (0.5 / d)
        dyv = diff * (1.0 / d)
        dxh = dyv * g_ref[...]
        dh = r * (dxh - xhat * jnp.mean(dxh * xhat, axis=-1, keepdims=True))
        dh_ref[...] = dh
        dhb_ref[...] = dh.astype(BF16)
        first = pl.program_id(0) == 0
        _accumulate(loss_ref, part, first)
        _accumulate(dg_ref, jnp.sum(dyv * xhat, axis=0, keepdims=True), first)

    row = pl.BlockSpec((tr, d), lambda i: (i, 0))
    vec = pl.BlockSpec((1, d), lambda i: (0, 0))
    one = pl.BlockSpec((1, 1), lambda i: (0, 0))
    return _pcall(
        body, name=name, out_shape=[_sds((1, 1), F32), _sds((s, d), F32), _sds((s, d), BF16), _sds((1, d), F32)],
        grid=(s // tr,), in_specs=[row, vec, row], out_specs=[one, row, row, vec],
    )(h, g.reshape(1, d), target)


def _tril_mask():
    return lax.broadcasted_iota(jnp.int32, (CHUNK, CHUNK), 0) >= lax.broadcasted_iota(jnp.int32, (CHUNK, CHUNK), 1)


def _gmlp_fwd(name, zp, gv, ws, bst):
    s, d2 = zp.shape
    d = d2 // 2
    gw = d // GROUPS

    def body(zp_ref, gv_ref, ws_ref, bst_ref, o_ref):
        u, _ = _gelu_parts(zp_ref[:, :d])
        v, _ = _gelu_parts(zp_ref[:, d:])
        rv = lax.rsqrt(jnp.mean(v * v, axis=-1, keepdims=True) + RMS_EPS)
        vn = (v * rv * gv_ref[...]).astype(BF16)
        tril = _tril_mask()
        for g in range(GROUPS):
            sl = slice(g * gw, (g + 1) * gw)
            wc = jnp.where(tril, ws_ref[g], 0.0).astype(BF16)
            sg = jnp.dot(wc, vn[:, sl], preferred_element_type=F32) + bst_ref[:, g:g + 1]
            o_ref[:, sl] = (u[:, sl] * sg).astype(o_ref.dtype)

    return _pcall(
        body, name=name, out_shape=_sds((s, d), BF16), grid=(s // CHUNK,),
        in_specs=[pl.BlockSpec((CHUNK, d2), lambda i: (i, 0)), pl.BlockSpec((1, d), lambda i: (0, 0)),
                  pl.BlockSpec((GROUPS, CHUNK, CHUNK), lambda i: (0, 0, 0)),
                  pl.BlockSpec((CHUNK, GROUPS), lambda i: (0, 0))],
        out_specs=pl.BlockSpec((CHUNK, d), lambda i: (i, 0)),
    )(zp, gv, ws, bst)


def _gmlp_bwd(name, zp, dgated, gv, ws, bst):
    s, d2 = zp.shape
    d = d2 // 2
    gw = d // GROUPS

    def body(zp_ref, dg_ref, gv_ref, ws_ref, bst_ref, dzp_ref, dws_ref, dbs_ref, dgv_ref, dvn_ref):
        u, gu = _gelu_parts(zp_ref[:, :d])
        v, gvv = _gelu_parts(zp_ref[:, d:])
        rv = lax.rsqrt(jnp.mean(v * v, axis=-1, keepdims=True) + RMS_EPS)
        vhat = v * rv
        vn = (vhat * gv_ref[...]).astype(BF16)
        tril = _tril_mask()
        first = pl.program_id(0) == 0
        ones = jnp.ones((8, gw), F32)

        @pl.when(first)
        def _():
            dws_ref[...] = jnp.zeros_like(dws_ref)
            dbs_ref[...] = jnp.zeros_like(dbs_ref)

        for g in range(GROUPS):
            sl = slice(g * gw, (g + 1) * gw)
            wc = jnp.where(tril, ws_ref[g], 0.0).astype(BF16)
            sg = jnp.dot(wc, vn[:, sl], preferred_element_type=F32) + bst_ref[:, g:g + 1]
            dgs = dg_ref[:, sl]
            ds = dgs * u[:, sl]
            dsb = ds.astype(BF16)
            dzp_ref[:, sl] = (dgs * sg * gu[:, sl]).astype(dzp_ref.dtype)
            dvn_ref[:, sl] = lax.dot_general(wc, dsb, TN, preferred_element_type=F32)
            dw = lax.dot_general(dsb, vn[:, sl], NT, preferred_element_type=F32)
            dws_ref[g] += jnp.where(tril, dw, 0.0)
            dbs_ref[g] += lax.dot_general(ones, ds, NT, preferred_element_type=F32, precision=lax.Precision.HIGHEST)
        dvn = dvn_ref[...]
        dvh = dvn * gv_ref[...]
        dv = rv * (dvh - vhat * jnp.mean(dvh * vhat, axis=-1, keepdims=True))
        dzp_ref[:, d:] = (dv * gvv).astype(dzp_ref.dtype)
        _accumulate(dgv_ref, jnp.sum(dvn * vhat, axis=0, keepdims=True), first)

    return _pcall(
        body, name=name,
        out_shape=[_sds((s, d2), BF16), _sds((GROUPS, CHUNK, CHUNK), F32), _sds((GROUPS, 8, CHUNK), F32),
                   _sds((1, d), F32)],
        grid=(s // CHUNK,),
        in_specs=[pl.BlockSpec((CHUNK, d2), lambda i: (i, 0)), pl.BlockSpec((CHUNK, d), lambda i: (i, 0)),
                  pl.BlockSpec((1, d), lambda i: (0, 0)), pl.BlockSpec((GROUPS, CHUNK, CHUNK), lambda i: (0, 0, 0)),
                  pl.BlockSpec((CHUNK, GROUPS), lambda i: (0, 0))],
        out_specs=[pl.BlockSpec((CHUNK, d2), lambda i: (i, 0)),
                   pl.BlockSpec((GROUPS, CHUNK, CHUNK), lambda i: (0, 0, 0)),
                   pl.BlockSpec((GROUPS, 8, CHUNK), lambda i: (0, 0, 0)), pl.BlockSpec((1, d), lambda i: (0, 0))],
        scratch=[pltpu.VMEM((CHUNK, d), F32)],
    )(zp, dgated, gv, ws, bst)


def _bucket_table():
    dist = np.arange(BLOCK)[:, None] + BLOCK - np.arange(2 * BLOCK)[None, :]
    in_window = (dist >= 0) & (dist < BLOCK)
    dd = np.clip(dist, 0, None)
    max_exact = N_BUCKETS // 2
    dl = np.maximum(dd, 1).astype(np.float32)
    large = max_exact + (np.log(dl / np.float32(max_exact)) / np.float32(math.log(MAX_DISTANCE / max_exact))
                         * np.float32(N_BUCKETS - max_exact)).astype(np.int32)
    large = np.minimum(large, N_BUCKETS - 1)
    bucket = np.where(dd < max_exact, dd, large)
    return np.where(in_window, bucket, -1).astype(np.int32).reshape(1, -1)


def _bias_table(name, rel_bias_t, buckets):
    nh = rel_bias_t.shape[0]
    p = buckets.shape[1]
    tp = 4096

    def body(rb_ref, bk_ref, o_ref):
        bk = bk_ref[...]
        onehot = (lax.broadcasted_iota(jnp.int32, (N_BUCKETS, tp), 0) == bk).astype(F32)
        val = jnp.dot(rb_ref[...], onehot, preferred_element_type=F32, precision=lax.Precision.HIGHEST)
        o_ref[...] = jnp.where(bk >= 0, val, NEG_INF)

    return _pcall(
        body, name=name, out_shape=_sds((nh, p), F32), grid=(p // tp,),
        in_specs=[pl.BlockSpec((nh, N_BUCKETS), lambda i: (0, 0)), pl.BlockSpec((1, tp), lambda i: (0, i))],
        out_specs=pl.BlockSpec((nh, tp), lambda i: (0, i)),
    )(rel_bias_t, buckets)


def _bias_grad(name, dbiases, buckets):
    nh, p = dbiases[0].shape
    n = len(dbiases)
    tp = 4096

    def body(*refs):
        bk_ref, o_ref = refs[n], refs[n + 1]
        onehot = (lax.broadcasted_iota(jnp.int32, (N_BUCKETS, tp), 0) == bk_ref[...]).astype(F32)
        db = refs[0][...]
        for r in refs[1:n]:
            db = db + r[...]
        part = lax.dot_general(onehot, db, NT, preferred_element_type=F32, precision=lax.Precision.HIGHEST)
        _accumulate(o_ref, part, pl.program_id(0) == 0)

    return _pcall(
        body, name=name, out_shape=_sds((N_BUCKETS, nh), F32), grid=(p // tp,),
        in_specs=[pl.BlockSpec((nh, tp), lambda i: (0, i))] * n + [pl.BlockSpec((1, tp), lambda i: (0, i))],
        out_specs=pl.BlockSpec((N_BUCKETS, nh), lambda i: (0, 0)),
    )(*dbiases, buckets)


def _stack_heads(ref, g):
    base = g * KV_GROUP * HEAD_DIM
    return jnp.concatenate([ref[:, base + hh * HEAD_DIM:base + (hh + 1) * HEAD_DIM] for hh in range(KV_GROUP)], axis=0)


def _attn_probs(q, kb, bias, s_ref, first_head):
    penalty = jnp.where(pl.program_id(1) > 0, 0.0, NEG_INF).astype(F32)
    col = lax.broadcasted_iota(jnp.int32, (1, 2 * BLOCK), 1)
    bias = bias.reshape(KV_GROUP * BLOCK, 2 * BLOCK) + jnp.where(col < BLOCK, penalty, 0.0)
    sink = jnp.concatenate([jnp.full((BLOCK, 1), s_ref[first_head + hh], F32) for hh in range(KV_GROUP)], axis=0)
    s = lax.dot_general(q, kb, NT, preferred_element_type=F32) * 0.125 + bias
    m = jnp.maximum(jnp.max(s, axis=-1, keepdims=True), sink)
    p = jnp.exp(s - m)
    es = jnp.exp(sink - m)
    inv = 1.0 / (jnp.sum(p, axis=-1, keepdims=True) + es)
    return p * inv, es * inv


def _attn_specs(ng):
    gq = ng * KV_GROUP * HEAD_DIM
    q_spec = pl.BlockSpec((BLOCK, gq), lambda kh, i: (i, kh))
    prev = pl.BlockSpec((ng, BLOCK, HEAD_DIM), lambda kh, i: (kh, jnp.maximum(i - 1, 0), 0))
    cur = pl.BlockSpec((ng, BLOCK, HEAD_DIM), lambda kh, i: (kh, i, 0))
    bias = pl.BlockSpec((ng * KV_GROUP, BLOCK, 2 * BLOCK), lambda kh, i: (kh, 0, 0))
    smem = pl.BlockSpec(memory_space=pltpu.SMEM)
    probs = pl.BlockSpec((ng, None, KV_GROUP * BLOCK, 2 * BLOCK), lambda kh, i: (kh, i, 0, 0))
    sink_probs = pl.BlockSpec((ng, None, KV_GROUP * BLOCK, 1), lambda kh, i: (kh, i, 0, 0))
    return q_spec, prev, cur, bias, smem, probs, sink_probs


def _kv_heads_per_step(nkv):
    return 2 if nkv % 2 == 0 else 1


def _attn_fwd(name, q, k, v, bias, sinks):
    s, dq = q.shape
    nkv = k.shape[0]
    ng = 1
    q_spec, prev, cur, bias_spec, smem, p_spec, ps_spec = _attn_specs(ng)

    def body(q_ref, kp_ref, kc_ref, vp_ref, vc_ref, b_ref, s_ref, o_ref, p_ref, ps_ref):
        for g in range(ng):
            kb = jnp.concatenate([kp_ref[g], kc_ref[g]], axis=0)
            vb = jnp.concatenate([vp_ref[g], vc_ref[g]], axis=0)
            p, ps = _attn_probs(_stack_heads(q_ref, g), kb, b_ref[g * KV_GROUP:(g + 1) * KV_GROUP], s_ref,
                                (pl.program_id(0) * ng + g) * KV_GROUP)
            pb = p.astype(BF16)
            p_ref[g] = pb
            ps_ref[g] = ps
            o = jnp.dot(pb, vb, preferred_element_type=F32)
            for hh in range(KV_GROUP):
                col = (g * KV_GROUP + hh) * HEAD_DIM
                o_ref[:, col:col + HEAD_DIM] = o[hh * BLOCK:(hh + 1) * BLOCK].astype(o_ref.dtype)

    return _pcall(
        body, name=name,
        out_shape=[_sds((s, dq), BF16), _sds((nkv, s // BLOCK, KV_GROUP * BLOCK, 2 * BLOCK), BF16),
                   _sds((nkv, s // BLOCK, KV_GROUP * BLOCK, 1), F32)],
        grid=(nkv // ng, s // BLOCK),
        in_specs=[q_spec, prev, cur, prev, cur, bias_spec, smem], out_specs=[q_spec, p_spec, ps_spec],
    )(q, k, k, v, v, bias, sinks)


def _attn_bwd(name, q, k, v, do, probs, sink_probs):
    s, dq = q.shape
    nkv = k.shape[0]
    ng = _kv_heads_per_step(nkv)
    gq = ng * KV_GROUP * HEAD_DIM
    q_spec, prev, cur, bias_spec, _, p_spec, ps_spec = _attn_specs(ng)

    def body(q_ref, do_ref, kp_ref, kc_ref, vp_ref, vc_ref, p_ref, ps_ref,
             dq_ref, dbq_ref, dkc_ref, dkp_ref, dvc_ref, dvp_ref, dbias_ref, dsink_ref):
        @pl.when(pl.program_id(1) == 0)
        def _():
            dbias_ref[...] = jnp.zeros_like(dbias_ref)
            dsink_ref[...] = jnp.zeros_like(dsink_ref)
            dbq_ref[...] = jnp.zeros_like(dbq_ref)

        for g in range(ng):
            kb = jnp.concatenate([kp_ref[g], kc_ref[g]], axis=0)
            vb = jnp.concatenate([vp_ref[g], vc_ref[g]], axis=0)
            q, do = _stack_heads(q_ref, g), _stack_heads(do_ref, g)
            pb = p_ref[g]
            p = pb.astype(F32)
            dp = lax.dot_general(do, vb, NT, preferred_element_type=F32)
            delta = jnp.sum(p * dp, axis=-1, keepdims=True)
            ds = p * (dp - delta)
            dsb = ds.astype(BF16)
            dq = jnp.dot(dsb, kb, preferred_element_type=F32) * 0.125
            dsk = -(ps_ref[g] * delta)
            for hh in range(KV_GROUP):
                col, rows = (g * KV_GROUP + hh) * HEAD_DIM, slice(hh * BLOCK, (hh + 1) * BLOCK)
                dq_ref[:, col:col + HEAD_DIM] = dq[rows].astype(dq_ref.dtype)
                dbq_ref[:, col:col + HEAD_DIM] += jnp.sum(dq[rows], axis=0, keepdims=True)
                dsink_ref[g, :, hh:hh + 1] += jnp.sum(dsk[rows], axis=0, keepdims=True)
            dkb = lax.dot_general(dsb, q, TN, preferred_element_type=F32) * 0.125
            dvb = lax.dot_general(pb, do, TN, preferred_element_type=F32)
            dkp_ref[g], dkc_ref[g] = dkb[:BLOCK], dkb[BLOCK:]
            dvp_ref[g], dvc_ref[g] = dvb[:BLOCK], dvb[BLOCK:]
            dbias_ref[g * KV_GROUP:(g + 1) * KV_GROUP] += ds.reshape(KV_GROUP, BLOCK, 2 * BLOCK)

    kv_out = _sds((nkv, s, HEAD_DIM), F32)
    return _pcall(
        body, name=name,
        out_shape=[_sds((s, dq), BF16), _sds((1, dq), F32), kv_out, kv_out, kv_out, kv_out,
                   _sds((nkv * KV_GROUP, BLOCK, 2 * BLOCK), F32), _sds((nkv, 1, KV_GROUP), F32)],
        grid=(nkv // ng, s // BLOCK),
        in_specs=[q_spec, q_spec, prev, cur, prev, cur, p_spec, ps_spec],
        out_specs=[q_spec, pl.BlockSpec((1, gq), lambda kh, i: (0, kh)), cur, cur, cur, cur, bias_spec,
                   pl.BlockSpec((ng, 1, KV_GROUP), lambda kh, i: (kh, 0, 0))],
    )(q, do, k, k, v, v, probs, sink_probs)


def _kv_grad(name, parts):
    nkv, s, _ = parts[0][0].shape
    nb = s // BLOCK
    w = 2 * nkv * HEAD_DIM
    n = len(parts)

    def body(*refs):
        o_ref, cs_ref = refs[4 * n], refs[4 * n + 1]
        i = pl.program_id(0)
        keep = jnp.where(i < nb - 1, 1.0, 0.0).astype(F32)

        @pl.when(i == 0)
        def _():
            cs_ref[...] = jnp.zeros_like(cs_ref)

        for which in range(2):
            for hh in range(nkv):
                val = None
                for l in range(n):
                    cur_ref, nxt_ref = refs[4 * l + 2 * which], refs[4 * l + 2 * which + 1]
                    t = cur_ref[hh] + keep * nxt_ref[hh]
                    val = t if val is None else val + t
                sl = slice((which * nkv + hh) * HEAD_DIM, (which * nkv + hh + 1) * HEAD_DIM)
                o_ref[:, sl] = val.astype(o_ref.dtype)
                cs_ref[:, sl] += jnp.sum(val, axis=0, keepdims=True)

    cur = pl.BlockSpec((nkv, BLOCK, HEAD_DIM), lambda i: (0, i, 0))
    nxt = pl.BlockSpec((nkv, BLOCK, HEAD_DIM), lambda i: (0, jnp.minimum(i + 1, nb - 1), 0))
    flat = [a for p in parts for a in p]
    return _pcall(
        body, name=name, out_shape=[_sds((s, w), BF16), _sds((1, w), F32)], grid=(nb,),
        in_specs=[cur, nxt] * (2 * n),
        out_specs=[pl.BlockSpec((BLOCK, w), lambda i: (i, 0)), pl.BlockSpec((1, w), lambda i: (0, 0))],
    )(*flat)


def _adamw_math(w, g, m, v):
    m = ADAM_B1 * m + (1.0 - ADAM_B1) * g
    v = ADAM_B2 * v + (1.0 - ADAM_B2) * (g * g)
    m_hat = m / (1.0 - ADAM_B1 ** ADAM_STEP)
    v_hat = v / (1.0 - ADAM_B2 ** ADAM_STEP)
    delta = -ADAM_LR * (m_hat / (jnp.sqrt(v_hat) + ADAM_EPS) + ADAM_WD * w)
    return delta, m, v


def _adamw_shard(name, w, m, v, parts, row0, layer, prev, deps=()):
    _, r, wd = w.shape
    tr = _row_tile(r, wd, budget=3 * 512 * 1024)
    assert row0 % tr == 0

    def body(w_ref, m_ref, v_ref, p_ref, a0, a1, a2, a3, g_ref, d_ref, nm_ref, nv_ref):
        g = p_ref[0].astype(F32)
        for k in range(1, NCHIP):
            g = g + p_ref[k].astype(F32)
        delta, nm, nv = _adamw_math(w_ref[...], g, m_ref[...], v_ref[...])
        g_ref[...], d_ref[...], nm_ref[...], nv_ref[...] = g, delta, nm, nv

    par = pl.BlockSpec((None, tr, wd), lambda i: (layer, i, 0))
    out = _sds(w.shape, F32)
    return _pcall(
        body, name=name, out_shape=[out, out, out, out], grid=(r // tr,),
        in_specs=[par, par, par, pl.BlockSpec((NCHIP, tr, wd), lambda i: (0, row0 // tr + i, 0)), ANY, ANY, ANY, ANY],
        out_specs=[par, par, par, par], aliases={4: 0, 5: 1, 6: 2, 7: 3}, deps=deps,
    )(w, m, v, parts, *prev)


def _sum_devices(name, gathered):
    _, r, wd = gathered.shape

    def body(g_ref, o_ref):
        acc = g_ref[0]
        for k in range(1, NDEV):
            acc = acc + g_ref[k]
        o_ref[...] = acc

    return _pcall(body, name=name, out_shape=_sds((r, wd), F32), grid=(1,),
                  in_specs=[pl.BlockSpec((NDEV, r, wd), lambda i: (0, 0, 0))],
                  out_specs=pl.BlockSpec((r, wd), lambda i: (0, 0)))(gathered)


def _adamw_flat(name, w, g, m, v):
    shape = w.shape

    def body(w_ref, g_ref, m_ref, v_ref, d_ref, nm_ref, nv_ref):
        d_ref[...], nm_ref[...], nv_ref[...] = _adamw_math(w_ref[...], g_ref[...], m_ref[...], v_ref[...])

    spec = pl.BlockSpec(shape, lambda i: (0, 0))
    out = _sds(shape, F32)
    return _pcall(body, name=name, out_shape=[out, out, out], grid=(1,), in_specs=[spec] * 4,
                  out_specs=[spec] * 3)(w, g, m, v)


def _cast_into(name, src, layer, buf, row0, me):
    _, r, wd = src.shape
    tr = _row_tile(r, wd)
    assert row0 % tr == 0

    def body(me_ref, s_ref, b_ref, o_ref):
        o_ref[...] = s_ref[...].astype(o_ref.dtype)

    return _pcall(
        body, name=name, out_shape=_sds(buf.shape, buf.dtype), grid=(r // tr,), prefetch=1,
        in_specs=[pl.BlockSpec((None, tr, wd), lambda i, mr: (layer, i, 0)), ANY],
        out_specs=pl.BlockSpec((None, tr, wd), lambda i, mr: (mr[0], row0 // tr + i, 0)), aliases={2: 0},
    )(me, src, buf)


def _pack(arrays):
    rows = []
    for a in arrays:
        flat = a.reshape(-1).astype(F32)
        pad = (-flat.shape[0]) % 1024
        rows.append(jnp.pad(flat, (0, pad)).reshape(-1, 128))
    return jnp.concatenate(rows, axis=0)


def _unpack(packed, shapes):
    out, r = [], 0
    for shp in shapes:
        n = int(np.prod(shp))
        nr = (n + 1023) // 1024 * 8
        out.append(packed[r:r + nr].reshape(-1)[:n].reshape(shp))
        r += nr
    return out


def kernel(x, mix_norm, ffn_norm, a_w_in, a_norm_v, a_w_s, a_b_s, a_w_out, kv_norm, w_kv, b_kv, b_w_q, b_b_q, b_sinks, b_w_o, b_b_o, rel_bias, ffn_w_gate, ffn_w_up, ffn_w_down, final_norm, loss_target, m_mix_norm, m_ffn_norm, m_a_w_in, m_a_norm_v, m_a_w_s, m_a_b_s, m_a_w_out, m_kv_norm, m_w_kv, m_b_kv, m_b_w_q, m_b_b_q, m_b_sinks, m_b_w_o, m_b_b_o, m_rel_bias, m_ffn_w_gate, m_ffn_w_up, m_ffn_w_down, m_final_norm, v_mix_norm, v_ffn_norm, v_a_w_in, v_a_norm_v, v_a_w_s, v_a_b_s, v_a_w_out, v_kv_norm, v_w_kv, v_b_kv, v_b_w_q, v_b_b_q, v_b_sinks, v_b_w_o, v_b_b_o, v_rel_bias, v_ffn_w_gate, v_ffn_w_up, v_ffn_w_down, v_final_norm):
    _, S, D = x.shape
    LA, LB, L = a_w_in.shape[0], b_w_q.shape[0], ffn_w_gate.shape[0]
    F = ffn_w_gate.shape[2]
    DS = D // NDEV
    ZC = a_w_in.shape[2]
    KVW = w_kv.shape[1]
    NKV = KVW // (2 * HEAD_DIM)
    NH = D // HEAD_DIM
    assert ZC * NDEV == 2 * D and NH == NKV * KV_GROUP and S % BLOCK == 0
    TM = min(1024, S)
    TN_ = min(1024, D)
    TS = min(512, D)
    KC = 4

    ix, iy, ic = lax.axis_index("x"), lax.axis_index("y"), lax.axis_index("c")
    me = (4 * ix + 2 * iy + ic).astype(jnp.int32)
    me1 = me.reshape(1)
    where = jnp.stack([ic, 2 * ix + iy]).astype(jnp.int32)

    def tr3(a):
        return jnp.transpose(a, (0, 2, 1))

    gate_t, up_t = tr3(ffn_w_gate), tr3(ffn_w_up)
    w_kv3 = w_kv.reshape((1,) + w_kv.shape)

    def layer_arrays(l):
        arrs = [("gu", 2 * F, D, [(gate_t, l, 0), (up_t, l, F)]), ("down", F, D, [(ffn_w_down, l, 0)])]
        if l < LA:
            arrs += [("win", D, ZC, [(a_w_in, l, 0)]), ("wout", DS, D, [(a_w_out, l, 0)])]
            if l == LA - 1:
                arrs.append(("wkv", DS, KVW, [(w_kv3, 0, 0)]))
        else:
            i_b = l - LA
            arrs.append(("wqo", 2 * DS, D, [(b_w_q, i_b, 0), (b_w_o, i_b, DS)]))
        return arrs

    gathers = []

    def gather_begin(l, deps):
        g = gathers[l]
        g["send"], g["recv"], g["bufs"], g["token"] = _relay_start(f"relay_start{l}", g["bufs"], deps)

    for l in range(L):
        keys, bufs = [], []
        for key, rows, width, sources in layer_arrays(l):
            buf = lax.empty((NDEV, rows, width), BF16)
            for si, (src, li, row0) in enumerate(sources):
                buf = _cast_into(f"cast_{key}{l}_{si}", src, li, buf, row0, me1)
            keys.append(key)
            bufs.append(buf)
        gathers.append(dict(keys=keys, bufs=bufs))
        if l == 0:
            nv_rows = _pack([a_norm_v])
            nv = _cast_into("put_norm_v", nv_rows.reshape((1,) + nv_rows.shape), 0,
                            lax.empty((NDEV,) + nv_rows.shape, F32), 0, me1)
            nv_send, nv_recv, nv_bufs, token = _gather_start("gather_norm_v_start", [nv], [])
            gather_begin(0, [token])

    def gather_relay(l, deps):
        g = gathers[l]
        g["fsend"], g["frecv"], g["bufs"], tok = _relay_neighbors(f"relay_neighbors{l}", g["bufs"], g["send"], g["recv"],
                                                                  deps)
        if l + 1 < L:
            gather_begin(l + 1, [tok])
            tok = gathers[l + 1]["token"]
        return tok

    def finish_gather(l, deps):
        g = gathers[l]
        gsend, grecv, bufs = _relay_diagonal(f"relay_diagonal{l}", g["bufs"], g["fsend"], g["frecv"], deps)
        bufs = _relay_finish(f"relay_finish{l}", bufs, g["send"], g["recv"], g["fsend"], g["frecv"], gsend, grecv)
        return dict(zip(g["keys"], bufs))

    token = gather_relay(0, [gathers[0]["token"]] + [b for g in gathers[1:] for b in g["bufs"]])

    buckets = jnp.asarray(_bucket_table())
    bias = _bias_table("bias_table", rel_bias.T, buckets).reshape(NH, BLOCK, 2 * BLOCK)

    def rows_full(tm):
        return pl.BlockSpec((tm, D), lambda i, j: (i, 0))

    def tile(tm, tn):
        return pl.BlockSpec((tm, tn), lambda i, j: (i, j))

    vec_tile = pl.BlockSpec((1, TN_), lambda i, j: (0, j))

    def ffn_forward(l, wl, h_mid, tag, deps):
        xf = _rms_fwd(f"ffn_norm_fwd{tag}", h_mid, ffn_norm[l], deps=deps)

        def ep(parts, ex, outs):
            a, b = parts
            sg = jax.nn.sigmoid(a)
            silu = a * sg
            outs[0][0] = (b * (sg * (1.0 + a * (1.0 - sg)))).astype(BF16)
            outs[0][1] = silu.astype(BF16)
            outs[1][...] = (silu * b).astype(BF16)

        ab, hid = _gemm(
            f"ffn_up{tag}", (S // TM, NDEV),
            [(xf, rows_full(TM)),
             (wl["gu"], pl.BlockSpec((None, F, D), lambda i, e: (e, 0, 0))),
             (wl["gu"], pl.BlockSpec((None, F, D), lambda i, e: (e, 1, 0)))],
            [(0, 1, NT), (0, 2, NT)], [],
            [(_sds((2, NDEV, S, F), BF16), pl.BlockSpec((2, None, TM, F), lambda i, e: (0, e, i, 0))),
             (_sds((NDEV, S, F), BF16), pl.BlockSpec((None, TM, F), lambda i, e: (e, i, 0)))],
            ep, separate=True)
        (h_out,) = _gemm(
            f"ffn_down{tag}", (S // TM, D // TN_, NDEV // KC),
            [(hid, pl.BlockSpec((KC, TM, F), lambda i, j, k: (k, i, 0))),
             (wl["down"], pl.BlockSpec((KC, F, TN_), lambda i, j, k: (k, 0, j)))],
            [(0, 1, NN, _pick(c), _pick(c)) for c in range(KC)],
            [(h_mid, pl.BlockSpec((TM, TN_), lambda i, j, k: (i, j)))],
            [(_sds((S, D), F32), pl.BlockSpec((TM, TN_), lambda i, j, k: (i, j)))],
            _store_add_extra, nk=NDEV // KC, acc_shape=(TM, TN_))
        return dict(h_mid=h_mid, xf=xf, ab=ab, hid=hid), h_out

    def stacked_rows_gemm(name, a, wmat, blk, extras, ep, out_dtype):
        return _gemm(
            name, (S // TM, D // TN_),
            [(a, rows_full(TM)), (wmat, pl.BlockSpec((NDEV, DS, TN_), lambda i, j: (0, blk, j)))],
            [(0, 1, NN, None, _stacked)], extras,
            [(_sds((S, D), out_dtype), tile(TM, TN_))], ep)[0]

    def back_rows_gemm(name, a, wmat, blk, out_dtype, deps=()):
        return _gemm(
            name, (S // TM, NDEV),
            [(a, rows_full(TM)), (wmat, pl.BlockSpec((None, DS, D), lambda i, e: (e, blk, 0)))],
            [(0, 1, NT)], [], [(_sds((S, D), out_dtype), pl.BlockSpec((TM, DS), lambda i, e: (i, e)))], _store,
            deps=deps)[0]

    def grad_rows_gemm(name, act, d_bf, buf, blk):
        return _gemm(
            name, (NDEV,),
            [(act, pl.BlockSpec((S, DS), lambda e: (0, e))), (d_bf, pl.BlockSpec((S, D), lambda e: (0, 0)))],
            [(0, 1, TN)], [(buf, ANY)],
            [(_sds(buf.shape, BF16), pl.BlockSpec((None, DS, D), lambda e: (e, blk, 0)))],
            _store, aliases={2: 0})[0]

    saved, weights = [], []
    h = x.reshape(S, D)
    k_heads = v_heads = hn = h_kv = norm_v = None
    for layer in range(L):
        wl = finish_gather(layer, [token] if layer == 0 else [h])
        weights.append(wl)
        if layer == 0:
            nv_fsend, nv_frecv, nv_bufs = _gather_forward("gather_norm_v_forward", nv_bufs, nv_send, nv_recv,
                                                          [wl["down"]])
            (nv_all,) = _gather_finish("gather_norm_v_finish", nv_bufs, nv_send, nv_recv, nv_fsend, nv_frecv)
            norm_v = jnp.transpose(nv_all.reshape(NDEV, -1)[:, :LA * DS].reshape(NDEV, LA, DS), (1, 0, 2)).reshape(LA, D)
        sv = dict(h_in=h)
        xn = _rms_fwd(f"mix_norm_fwd{layer}", h, mix_norm[layer])
        sv["xn"] = xn
        if layer < LA:
            i_a = layer
            (zp,) = _gemm(
                f"gmlp_in{layer}", (S // TM, NDEV),
                [(xn, rows_full(TM)), (wl["win"], pl.BlockSpec((None, D, ZC), lambda i, e: (e, 0, 0)))],
                [(0, 1, NN)], [], [(_sds((S, 2 * D), F32), pl.BlockSpec((TM, ZC), lambda i, e: (i, e)))], _store)
            bst = a_b_s[i_a].T
            gated = _gmlp_fwd(f"gmlp_gate{layer}", zp, norm_v[i_a].reshape(1, D), a_w_s[i_a], bst)
            sv.update(zp=zp, gated=gated, bst=bst)
            h_mid = stacked_rows_gemm(f"gmlp_out{layer}", gated, wl["wout"], 0, [(h, tile(TM, TN_))],
                                      _store_add_extra, F32)
        else:
            i_b = layer - LA
            q = stacked_rows_gemm(f"attn_q{layer}", xn, wl["wqo"], 0, [(b_b_q[i_b].reshape(1, D), vec_tile)],
                                  _store_add_extra, BF16)
            attn, probs, sink_probs = _attn_fwd(f"attn_fwd{layer}", q, k_heads, v_heads, bias, b_sinks[i_b])
            sv.update(q=q, attn=attn, probs=probs, sink_probs=sink_probs)
            h_mid = stacked_rows_gemm(f"attn_o{layer}", attn, wl["wqo"], 1,
                                      [(h, tile(TM, TN_)), (b_b_o[i_b].reshape(1, D), vec_tile)],
                                      _store_add_extra, F32)
        relay_token = [gather_relay(layer + 1, [h_mid])] if layer + 1 < L else []
        fsv, h = ffn_forward(layer, wl, h_mid, str(layer), relay_token)
        sv.update(fsv)
        saved.append(sv)
        if layer == LA - 1:
            h_kv = h
            hn = _rms_fwd("kv_norm_fwd", h, kv_norm)

            def kv_ep(acc, ex, outs):
                val = acc + ex[0][...]
                for hh in range(NKV):
                    outs[0][hh] = val[:, hh * HEAD_DIM:(hh + 1) * HEAD_DIM].astype(BF16)
                    outs[1][hh] = val[:, (NKV + hh) * HEAD_DIM:(NKV + hh + 1) * HEAD_DIM].astype(BF16)

            k_heads, v_heads = _gemm(
                "kv_proj", (S // TM,),
                [(hn, pl.BlockSpec((TM, D), lambda i: (i, 0))),
                 (wl["wkv"], pl.BlockSpec((NDEV, DS, KVW), lambda i: (0, 0, 0)))],
                [(0, 1, NN, None, _stacked)], [(b_kv.reshape(1, KVW), pl.BlockSpec((1, KVW), lambda i: (0, 0)))],
                [(_sds((NKV, S, HEAD_DIM), BF16), pl.BlockSpec((NKV, TM, HEAD_DIM), lambda i: (0, i, 0)))] * 2,
                kv_ep)

    loss11, d, d_bf, g_final = _loss_bwd("loss_bwd", h, final_norm, loss_target.reshape(S, D))
    loss = lax.psum(loss11[0, 0], AXES)

    g_mix, g_ffn = [None] * L, [None] * L
    g_ws, g_bs, g_nv = [None] * LA, [None] * LA, [None] * LA
    g_bq, g_sink, g_bo = [None] * LB, [None] * LB, [None] * LB
    dbiases = []
    kv_parts = []
    g_kvn = g_bkv = None
    exchanges = [[] for _ in range(L)]
    pending = None
    grads_wkv = None
    newest = []

    def new_grads(l):
        return {key: lax.empty((NDEV, rows, width), BF16) for key, rows, width, _ in layer_arrays(l)}

    def exchange_begin(tag, l, gl, keys):
        grads = [gl[k] for k in keys]
        lands = [lax.empty((NCHIP,) + g.shape[1:], BF16) for g in grads]
        send, recv, grads, lands, tok = _sibling_start(f"rs_sibling_start{tag}", grads, lands, [])
        newest[:] = [tok]
        return dict(tag=tag, layer=l, keys=keys, grads=grads, lands=lands, send=send, recv=recv)

    def exchange_middle(st, dep):
        tag = st["tag"]
        grads, lands = _sibling_finish(f"rs_sibling_finish{tag}", st["grads"], st["lands"], st["send"], st["recv"], [dep])
        sums, own = [], []
        for t, key in enumerate(st["keys"]):
            s_, o_ = _pair_sum(f"pair_sum_{key}{tag}", grads[t], lands[t], where)
            sums.append(s_)
            own.append(o_)
        send, recv, sums, own, tok = _chips_start(f"rs_chips_start{tag}", sums, own, [])
        newest[:] = [tok]
        st.update(sums=sums, own=own, send2=send, recv2=recv)
        exchanges[st["layer"]].append(st)

    def exchange_end(st, dep):
        lands = _chips_finish(f"rs_chips_finish{st['tag']}", st["sums"], st["own"], st["send2"], st["recv2"], [dep])
        return dict(zip(st["keys"], lands))

    for layer in reversed(range(L)):
        sv, wl = saved[layer], weights[layer]
        tag = str(layer)
        gl = new_grads(layer)
        if grads_wkv is not None and layer == LA - 1:
            gl["wkv"] = grads_wkv
        def dhid_ep(acc, ex, outs):
            outs[0][0] = (acc * ex[0][0].astype(F32)).astype(BF16)
            outs[0][1] = (acc * ex[0][1].astype(F32)).astype(BF16)

        ab_spec = pl.BlockSpec((2, None, TM, F), lambda i, e: (0, e, i, 0))
        (dab,) = _gemm(
            f"ffn_dhid{tag}", (S // TM, NDEV),
            [(d_bf, rows_full(TM)), (wl["down"], pl.BlockSpec((None, F, D), lambda i, e: (e, 0, 0)))],
            [(0, 1, NT)], [(sv["ab"], ab_spec)], [(_sds((2, NDEV, S, F), BF16), ab_spec)], dhid_ep,
            deps=list(newest))
        if pending:
            exchange_middle(pending, dab)
        (gl["down"],) = _gemm(
            f"ffn_dwdown{tag}", (NDEV,),
            [(sv["hid"], pl.BlockSpec((None, S, F), lambda e: (e, 0, 0))),
             (d_bf, pl.BlockSpec((S, D), lambda e: (0, 0)))],
            [(0, 1, TN)], [(gl["down"], ANY)],
            [(_sds(gl["down"].shape, BF16), pl.BlockSpec((None, F, D), lambda e: (e, 0, 0)))],
            _store, aliases={2: 0}, deps=list(newest))
        (gl["gu"],) = _gemm(
            f"ffn_dwup{tag}", (2, NDEV),
            [(dab, pl.BlockSpec((None, None, S, F), lambda w, e: (w, e, 0, 0))),
             (sv["xf"], pl.BlockSpec((S, D), lambda w, e: (0, 0)))],
            [(0, 1, TN)], [(gl["gu"], ANY)],
            [(_sds(gl["gu"].shape, BF16), pl.BlockSpec((None, F, D), lambda w, e: (e, w, 0)))],
            _store, aliases={2: 0})
        ffn_group = exchange_begin(f"_ffn{tag}", layer, gl, ["gu", "down"])
        (dxf,) = _gemm(
            f"ffn_dx{tag}", (S // TM, D // TN_, 2 * NDEV // KC),
            [(dab.reshape(2 * NDEV // KC, KC, S, F), pl.BlockSpec((None, KC, TM, F), lambda i, j, k: (k, 0, i, 0))),
             (wl["gu"], pl.BlockSpec((KC, F, TN_), lambda i, j, k: (k % (NDEV // KC), k // (NDEV // KC), j)))],
            [(0, 1, NN, _pick(c), _pick(c)) for c in range(KC)], [],
            [(_sds((S, D), F32), pl.BlockSpec((TM, TN_), lambda i, j, k: (i, j)))],
            _store, nk=2 * NDEV // KC, acc_shape=(TM, TN_), deps=list(newest))
        exchange_middle(ffn_group, dxf)
        d, d_bf, g_ffn[layer], colsum = _rms_bwd(f"ffn_norm_bwd{tag}", sv["h_mid"], ffn_norm[layer], dxf, d,
                                                 deps=list(newest))
        if layer < LA:
            i_a = layer
            dgated = back_rows_gemm(f"gmlp_dgated{tag}", d_bf, wl["wout"], 0, F32)
            gl["wout"] = grad_rows_gemm(f"gmlp_dwout{tag}", sv["gated"], d_bf, gl["wout"], 0)
            dzp, g_ws[i_a], dbs, g_nv[i_a] = _gmlp_bwd(f"gmlp_bwd{tag}", sv["zp"], dgated,
                                                       norm_v[i_a].reshape(1, D), a_w_s[i_a], sv["bst"])
            g_bs[i_a] = dbs[:, 0, :]
            (gl["win"],) = _gemm(
                f"gmlp_dwin{tag}", (NDEV, D // TS),
                [(sv["xn"], pl.BlockSpec((S, TS), lambda e, i: (0, i))),
                 (dzp, pl.BlockSpec((S, ZC), lambda e, i: (0, e)))],
                [(0, 1, TN)], [(gl["win"], ANY)],
                [(_sds(gl["win"].shape, BF16), pl.BlockSpec((None, TS, ZC), lambda e, i: (e, i, 0)))],
                _store, aliases={2: 0})
            (dxn,) = _gemm(
                f"gmlp_dx{tag}", (S // TM, D // TN_, NDEV // KC),
                [(dzp, pl.BlockSpec((TM, KC * ZC), lambda i, j, k: (i, k))),
                 (wl["win"], pl.BlockSpec((KC, TN_, ZC), lambda i, j, k: (k, j, 0)))],
                [(0, 1, NT, _cols(c, ZC), _pick(c)) for c in range(KC)], [],
                [(_sds((S, D), F32), pl.BlockSpec((TM, TN_), lambda i, j, k: (i, j)))],
                _store, nk=NDEV // KC, acc_shape=(TM, TN_))
        else:
            i_b = layer - LA
            g_bo[i_b] = colsum
            dattn = back_rows_gemm(f"attn_dout{tag}", d_bf, wl["wqo"], 1, BF16)
            gl["wqo"] = grad_rows_gemm(f"attn_dwo{tag}", sv["attn"], d_bf, gl["wqo"], 1)
            dq, g_bq[i_b], dkc, dkp, dvc, dvp, dbias, dsink = _attn_bwd(
                f"attn_bwd{tag}", sv["q"], k_heads, v_heads, dattn, sv["probs"], sv["sink_probs"])
            kv_parts.append((dkc, dkp, dvc, dvp))
            g_sink[i_b] = dsink.reshape(NH)
            dbiases.append(dbias.reshape(NH, BLOCK * 2 * BLOCK))
            gl["wqo"] = grad_rows_gemm(f"attn_dwq{tag}", sv["xn"], dq, gl["wqo"], 0)
            dxn = back_rows_gemm(f"attn_dx{tag}", dq, wl["wqo"], 0, F32)
        d, d_bf, g_mix[layer], _ = _rms_bwd(f"mix_norm_bwd{tag}", sv["h_in"], mix_norm[layer], dxn, d)
        pending = exchange_begin(f"_mix{tag}", layer, gl, [k for k in gl if k not in ("gu", "down")])
        if layer == LA:
            wkv = weights[LA - 1]["wkv"]
            dkv, g_bkv = _kv_grad("kv_grad", kv_parts)
            (grads_wkv,) = _gemm(
                "kv_dw", (NDEV,),
                [(hn, pl.BlockSpec((S, DS), lambda e: (0, e))), (dkv, pl.BlockSpec((S, KVW), lambda e: (0, 0)))],
                [(0, 1, TN)], [(lax.empty((NDEV, DS, KVW), BF16), ANY)],
                [(_sds((NDEV, DS, KVW), BF16), pl.BlockSpec((None, DS, KVW), lambda e: (e, 0, 0)))],
                _store, aliases={2: 0}, deps=list(newest))
            (dhn,) = _gemm(
                "kv_dx", (S // TM, NDEV),
                [(dkv, pl.BlockSpec((TM, KVW), lambda i, e: (i, 0))),
                 (wkv, pl.BlockSpec((None, DS, KVW), lambda i, e: (e, 0, 0)))],
                [(0, 1, NT)], [], [(_sds((S, D), F32), pl.BlockSpec((TM, DS), lambda i, e: (i, e)))], _store)
            d, d_bf, g_kvn, _ = _rms_bwd("kv_norm_bwd", h_kv, kv_norm, dhn, d)
    grad_x = d.reshape(x.shape)

    exchange_middle(pending, d)

    g_rel = _bias_grad("bias_grad", dbiases, buckets)
    small_local = _pack([jnp.concatenate(g_mix, axis=0), jnp.concatenate(g_ffn, axis=0), jnp.stack(g_ws),
                         jnp.stack(g_bs), g_kvn, g_bkv, jnp.concatenate(g_bq, axis=0), jnp.stack(g_sink),
                         jnp.concatenate(g_bo, axis=0), g_rel, g_final, jnp.concatenate(g_nv, axis=0)])
    small_slot = _cast_into("put_small_grads", small_local.reshape((1,) + small_local.shape), 0,
                            lax.empty((NDEV,) + small_local.shape, F32), 0, me1)
    s_send, s_recv, s_bufs, s_tok = _gather_start("gather_small_start", [small_slot], list(newest))

    results = {}
    after = [s_tok]

    def upd(pname, w, m, v, l, li, lands, row0):
        w3 = w if w.ndim == 3 else w.reshape((1,) + w.shape)
        prev = results.get(pname) or [lax.empty(w3.shape, F32) for _ in range(4)]
        results[pname] = _adamw_shard(f"adamw_{pname}{l}", w3, m.reshape(w3.shape), v.reshape(w3.shape), lands,
                                      row0, li, prev, deps=list(after))
        after[:] = [results[pname][0]]

    for l in reversed(range(L)):
        for st in exchanges[l]:
            lands = exchange_end(st, after[0])
            if "gu" in lands:
                upd("ffn_w_gate", gate_t, tr3(m_ffn_w_gate), tr3(v_ffn_w_gate), l, l, lands["gu"], 0)
                upd("ffn_w_up", up_t, tr3(m_ffn_w_up), tr3(v_ffn_w_up), l, l, lands["gu"], F)
                upd("ffn_w_down", ffn_w_down, m_ffn_w_down, v_ffn_w_down, l, l, lands["down"], 0)
            if "win" in lands:
                upd("a_w_in", a_w_in, m_a_w_in, v_a_w_in, l, l, lands["win"], 0)
                upd("a_w_out", a_w_out, m_a_w_out, v_a_w_out, l, l, lands["wout"], 0)
            if "wkv" in lands:
                upd("w_kv", w_kv, m_w_kv, v_w_kv, l, 0, lands["wkv"], 0)
            if "wqo" in lands:
                upd("b_w_q", b_w_q, m_b_w_q, v_b_w_q, l, l - LA, lands["wqo"], 0)
                upd("b_w_o", b_w_o, m_b_w_o, v_b_w_o, l, l - LA, lands["wqo"], DS)
    for pname in ("ffn_w_gate", "ffn_w_up"):
        results[pname] = [tr3(r) for r in results[pname]]
    results["w_kv"] = [r.reshape(w_kv.shape) for r in results["w_kv"]]

    small_w = [mix_norm, ffn_norm, a_w_s, a_b_s, kv_norm, b_kv, b_b_q, b_sinks, b_b_o, rel_bias, final_norm]
    small_m = [m_mix_norm, m_ffn_norm, m_a_w_s, m_a_b_s, m_kv_norm, m_b_kv, m_b_b_q, m_b_sinks, m_b_b_o, m_rel_bias,
               m_final_norm]
    small_v = [v_mix_norm, v_ffn_norm, v_a_w_s, v_a_b_s, v_kv_norm, v_b_kv, v_b_b_q, v_b_sinks, v_b_b_o, v_rel_bias,
               v_final_norm]
    shapes = [w.shape for w in small_w] + [(LA, D)]
    s_fsend, s_frecv, s_bufs = _gather_forward("gather_small_forward", s_bufs, s_send, s_recv, list(after))
    (small_all,) = _gather_finish("gather_small_finish", s_bufs, s_send, s_recv, s_fsend, s_frecv)
    small_sum = _sum_devices("sum_small_grads", small_all)
    small_g = _unpack(small_sum, shapes)
    g_normv = lax.dynamic_slice_in_dim(small_g[-1], me * DS, DS, axis=1)
    small_g = small_g[:-1] + [g_normv]
    small_w, small_m, small_v = small_w + [a_norm_v], small_m + [m_a_norm_v], small_v + [v_a_norm_v]
    shapes = [w.shape for w in small_w]
    s_delta, s_m, s_v = _adamw_flat("adamw_small", _pack(small_w), _pack(small_g), _pack(small_m), _pack(small_v))
    s_delta, s_m, s_v = _unpack(s_delta, shapes), _unpack(s_m, shapes), _unpack(s_v, shapes)

    names = ["mix_norm", "ffn_norm", "a_w_in", "a_norm_v", "a_w_s", "a_b_s", "a_w_out", "kv_norm", "w_kv", "b_kv",
             "b_w_q", "b_b_q", "b_sinks", "b_w_o", "b_b_o", "rel_bias", "ffn_w_gate", "ffn_w_up", "ffn_w_down",
             "final_norm"]
    small_names = ["mix_norm", "ffn_norm", "a_w_s", "a_b_s", "kv_norm", "b_kv", "b_b_q", "b_sinks", "b_b_o", "rel_bias",
                   "final_norm", "a_norm_v"]
    res = {}
    for idx, nm in enumerate(small_names):
        res[nm] = (small_g[idx].reshape(shapes[idx]), s_delta[idx], s_m[idx], s_v[idx])
    for nm, u in results.items():
        res[nm] = tuple(u)
    out = [loss, grad_x]
    for part in range(4):
        out += [res[nm][part] for nm in names]
    return tuple(out)
```

```python
import math

import numpy as np
import jax
import jax.numpy as jnp
from jax import lax
from jax.experimental import pallas as pl
from jax.experimental.pallas import tpu as pltpu

F32 = jnp.float32
BF16 = jnp.bfloat16
AXES = ("x", "y", "c")
NDEV = 8
NCHIP = 4
CHUNK = 128
GROUPS = 8
HEAD_DIM = 64
KV_GROUP = 8
BLOCK = 128
N_BUCKETS = 32
MAX_DISTANCE = 128
RMS_EPS = 1e-5
NEG_INF = -1e30
ADAM_LR, ADAM_B1, ADAM_B2, ADAM_EPS, ADAM_WD, ADAM_STEP = 0.001, 0.9, 0.999, 1e-08, 0.01, 10
VMEM_LIMIT_BYTES = 56 * 1024 * 1024

NN = (((1,), (0,)), ((), ()))
NT = (((1,), (1,)), ((), ()))
TN = (((0,), (0,)), ((), ()))
ANY = pl.BlockSpec(memory_space=pl.ANY)
HBM = pl.BlockSpec(memory_space=pltpu.HBM)
SEM = pl.BlockSpec(memory_space=pltpu.SEMAPHORE)
MESH = pl.DeviceIdType.MESH
EFFECT = pltpu.SideEffectType.DATAFLOW_SIDE_EFFECTING


def _pcall(body, *, name, out_shape, in_specs, out_specs, grid=(), scratch=(), aliases=None, prefetch=0, deps=()):
    n_in, n_dep = len(in_specs), len(deps)
    if n_dep:
        inner = body

        def body(*refs):
            return inner(*refs[:prefetch + n_in], *refs[prefetch + n_in + n_dep:])

        in_specs = list(in_specs) + [ANY] * n_dep
    params = dict(vmem_limit_bytes=VMEM_LIMIT_BYTES)
    if grid:
        params["dimension_semantics"] = ("arbitrary",) * len(grid)
    kw = dict(name=name, out_shape=out_shape, compiler_params=pltpu.CompilerParams(**params),
              input_output_aliases=aliases or {})
    if prefetch:
        kw["grid_spec"] = pltpu.PrefetchScalarGridSpec(num_scalar_prefetch=prefetch, grid=grid, in_specs=in_specs,
                                                       out_specs=out_specs, scratch_shapes=list(scratch))
    else:
        kw.update(grid=grid, in_specs=in_specs, out_specs=out_specs, scratch_shapes=list(scratch))
    call = pl.pallas_call(body, **kw)
    return lambda *args: call(*args, *deps)


def _sds(shape, dtype):
    return jax.ShapeDtypeStruct(tuple(shape), dtype)


def _position():
    x, y, c = lax.axis_index("x"), lax.axis_index("y"), lax.axis_index("c")
    chips = [(1 - x, y), (x, 1 - y), (1 - x, 1 - y)]
    return x, y, c, chips


def _slot(px, py, pc):
    return 4 * px + 2 * py + pc


def _remote(ref_src, ref_dst, send, recv, to):
    return pltpu.make_async_remote_copy(src_ref=ref_src, dst_ref=ref_dst, send_sem=send, recv_sem=recv,
                                        device_id=to, device_id_type=MESH)


def _hbm(arrays):
    return [pltpu.with_memory_space_constraint(a, pltpu.HBM) for a in arrays]


def _split_call(body, name, out_shape, in_specs, out_specs, aliases):
    return pl.pallas_call(body, name=name, out_shape=out_shape, in_specs=in_specs, out_specs=out_specs,
                          input_output_aliases=aliases, compiler_params=pltpu.CompilerParams(has_side_effects=EFFECT))


def _token_shape():
    return _sds((8, 128), F32)


def _gather_start(name, bufs, deps):
    n, nd = len(bufs), len(deps)

    def body(*refs):
        ins, send, recv, token = refs[:n], refs[n + nd], refs[n + nd + 1], refs[2 * n + nd + 2]
        x, y, c, chips = _position()
        peers = [(x, y, 1 - c)] + [(*chip, c) for chip in chips]
        for t in range(n):
            mine = ins[t].at[_slot(x, y, c)]
            for k, peer in enumerate(peers):
                _remote(mine, mine, send.at[4 * t + k], recv.at[4 * t + k], peer).start()
        token[...] = jnp.zeros_like(token)

    res = _split_call(
        body, name,
        (pltpu.SemaphoreType.DMA((4 * n,)), pltpu.SemaphoreType.DMA((4 * n,)), *[pltpu.HBM(b.shape, b.dtype) for b in bufs],
         _token_shape()),
        [HBM] * n + [ANY] * nd, (SEM, SEM, *[HBM] * n, pl.BlockSpec(memory_space=pltpu.VMEM)),
        {t: 2 + t for t in range(n)})(*_hbm(bufs), *deps)
    return res[0], res[1], list(res[2:2 + n]), res[2 + n]


def _gather_forward(name, bufs, send, recv, deps):
    n, nd = len(bufs), len(deps)

    def body(*refs):
        ins, send_in, recv_in = refs[:n], refs[n], refs[n + 1]
        fsend, frecv = refs[n + 2 + nd], refs[n + 3 + nd]
        x, y, c, chips = _position()
        for j, chip in enumerate(chips):
            for t in range(n):
                blk = ins[t].at[_slot(*chip, c)]
                _remote(blk, blk, send_in.at[4 * t + 1 + j], recv_in.at[4 * t + 1 + j], (*chip, c)).wait_recv()
                _remote(blk, blk, fsend.at[3 * t + j], frecv.at[3 * t + j], (x, y, 1 - c)).start()

    res = _split_call(
        body, name,
        (pltpu.SemaphoreType.DMA((3 * n,)), pltpu.SemaphoreType.DMA((3 * n,)), *[pltpu.HBM(b.shape, b.dtype) for b in bufs]),
        [HBM] * n + [SEM, SEM] + [ANY] * nd, (SEM, SEM, *[HBM] * n),
        {t: 2 + t for t in range(n)})(*_hbm(bufs), send, recv, *deps)
    return res[0], res[1], list(res[2:])


def _gather_finish(name, bufs, send, recv, fsend, frecv):
    n = len(bufs)

    def body(*refs):
        ins, send_in, recv_in, fs_in, fr_in = refs[:n], refs[n], refs[n + 1], refs[n + 2], refs[n + 3]
        x, y, c, chips = _position()
        sibling = (x, y, 1 - c)
        peers = [sibling] + [(*chip, c) for chip in chips]
        for t in range(n):
            blk = ins[t].at[_slot(x, y, 1 - c)]
            _remote(blk, blk, send_in.at[4 * t], recv_in.at[4 * t], sibling).wait_recv()
            for j, chip in enumerate(chips):
                blk = ins[t].at[_slot(*chip, 1 - c)]
                _remote(blk, blk, fs_in.at[3 * t + j], fr_in.at[3 * t + j], sibling).wait_recv()
            mine = ins[t].at[_slot(x, y, c)]
            for k, peer in enumerate(peers):
                _remote(mine, mine, send_in.at[4 * t + k], recv_in.at[4 * t + k], peer).wait_send()
            for j, chip in enumerate(chips):
                blk = ins[t].at[_slot(*chip, c)]
                _remote(blk, blk, fs_in.at[3 * t + j], fr_in.at[3 * t + j], sibling).wait_send()

    res = _split_call(
        body, name, tuple(pltpu.HBM(b.shape, b.dtype) for b in bufs),
        [HBM] * n + [SEM] * 4, tuple([HBM] * n), {t: t for t in range(n)})(*_hbm(bufs), send, recv, fsend, frecv)
    return list(res)


def _halves(ref):
    rows = ref.shape[0] // 2
    return ref.at[pl.ds(0, rows)], ref.at[pl.ds(rows, rows)]


def _relay_start(name, bufs, deps):
    n, nd = len(bufs), len(deps)

    def body(*refs):
        ins, send, recv, token = refs[:n], refs[n + nd], refs[n + nd + 1], refs[2 * n + nd + 2]
        x, y, c, _ = _position()
        peers = [(x, y, 1 - c), (1 - x, y, c), (x, 1 - y, c)]
        for t in range(n):
            mine = ins[t].at[_slot(x, y, c)]
            for k, peer in enumerate(peers):
                _remote(mine, mine, send.at[3 * t + k], recv.at[3 * t + k], peer).start()
        token[...] = jnp.zeros_like(token)

    res = _split_call(
        body, name,
        (pltpu.SemaphoreType.DMA((3 * n,)), pltpu.SemaphoreType.DMA((3 * n,)), *[pltpu.HBM(b.shape, b.dtype) for b in bufs],
         _token_shape()),
        [HBM] * n + [ANY] * nd, (SEM, SEM, *[HBM] * n, pl.BlockSpec(memory_space=pltpu.VMEM)),
        {t: 2 + t for t in range(n)})(*_hbm(bufs), *deps)
    return res[0], res[1], list(res[2:2 + n]), res[2 + n]


def _relay_neighbors(name, bufs, send, recv, deps):
    n, nd = len(bufs), len(deps)

    def body(*refs):
        ins, send_in, recv_in = refs[:n], refs[n], refs[n + 1]
        fsend, frecv, token = refs[n + 2 + nd], refs[n + 3 + nd], refs[2 * n + 4 + nd]
        x, y, c, _ = _position()
        sibling, xn, yn = (x, y, 1 - c), (1 - x, y, c), (x, 1 - y, c)
        for t in range(n):
            blk = ins[t].at[_slot(*xn)]
            _remote(blk, blk, send_in.at[3 * t + 1], recv_in.at[3 * t + 1], xn).wait_recv()
            _remote(blk, blk, fsend.at[4 * t], frecv.at[4 * t], sibling).start()
            half = _halves(blk)[0]
            _remote(half, half, fsend.at[4 * t + 1], frecv.at[4 * t + 1], yn).start()
        for t in range(n):
            blk = ins[t].at[_slot(*yn)]
            _remote(blk, blk, send_in.at[3 * t + 2], recv_in.at[3 * t + 2], yn).wait_recv()
            _remote(blk, blk, fsend.at[4 * t + 2], frecv.at[4 * t + 2], sibling).start()
            half = _halves(blk)[1]
            _remote(half, half, fsend.at[4 * t + 3], frecv.at[4 * t + 3], xn).start()
        token[...] = jnp.zeros_like(token)

    res = _split_call(
        body, name,
        (pltpu.SemaphoreType.DMA((4 * n,)), pltpu.SemaphoreType.DMA((4 * n,)), *[pltpu.HBM(b.shape, b.dtype) for b in bufs],
         _token_shape()),
        [HBM] * n + [SEM, SEM] + [ANY] * nd, (SEM, SEM, *[HBM] * n, pl.BlockSpec(memory_space=pltpu.VMEM)),
        {t: 2 + t for t in range(n)})(*_hbm(bufs), send, recv, *deps)
    return res[0], res[1], list(res[2:2 + n]), res[2 + n]


def _relay_diagonal(name, bufs, fsend, frecv, deps):
    n, nd = len(bufs), len(deps)

    def body(*refs):
        ins, fs_in, fr_in = refs[:n], refs[n], refs[n + 1]
        gsend, grecv = refs[n + 2 + nd], refs[n + 3 + nd]
        x, y, c, _ = _position()
        for t in range(n):
            blk = ins[t].at[_slot(1 - x, 1 - y, c)]
            first, second = _halves(blk)
            _remote(first, first, fs_in.at[4 * t + 1], fr_in.at[4 * t + 1], (x, 1 - y, c)).wait_recv()
            _remote(second, second, fs_in.at[4 * t + 3], fr_in.at[4 * t + 3], (1 - x, y, c)).wait_recv()
            _remote(blk, blk, gsend.at[t], grecv.at[t], (x, y, 1 - c)).start()

    res = _split_call(
        body, name,
        (pltpu.SemaphoreType.DMA((n,)), pltpu.SemaphoreType.DMA((n,)), *[pltpu.HBM(b.shape, b.dtype) for b in bufs]),
        [HBM] * n + [SEM, SEM] + [ANY] * nd, (SEM, SEM, *[HBM] * n),
        {t: 2 + t for t in range(n)})(*_hbm(bufs), fsend, frecv, *deps)
    return res[0], res[1], list(res[2:])


def _relay_finish(name, bufs, send, recv, fsend, frecv, gsend, grecv):
    n = len(bufs)

    def body(*refs):
        ins = refs[:n]
        send_in, recv_in, fs_in, fr_in, gs_in, gr_in = refs[n:n + 6]
        x, y, c, _ = _position()
        sibling, xn, yn = (x, y, 1 - c), (1 - x, y, c), (x, 1 - y, c)
        for t in range(n):
            blk = ins[t].at[_slot(x, y, 1 - c)]
            _remote(blk, blk, send_in.at[3 * t], recv_in.at[3 * t], sibling).wait_recv()
            blk = ins[t].at[_slot(1 - x, y, 1 - c)]
            _remote(blk, blk, fs_in.at[4 * t], fr_in.at[4 * t], sibling).wait_recv()
            blk = ins[t].at[_slot(x, 1 - y, 1 - c)]
            _remote(blk, blk, fs_in.at[4 * t + 2], fr_in.at[4 * t + 2], sibling).wait_recv()
            blk = ins[t].at[_slot(1 - x, 1 - y, 1 - c)]
            _remote(blk, blk, gs_in.at[t], gr_in.at[t], sibling).wait_recv()
            mine = ins[t].at[_slot(x, y, c)]
            for k, peer in enumerate([sibling, xn, yn]):
                _remote(mine, mine, send_in.at[3 * t + k], recv_in.at[3 * t + k], peer).wait_send()
            bx, by = ins[t].at[_slot(*xn)], ins[t].at[_slot(*yn)]
            _remote(bx, bx, fs_in.at[4 * t], fr_in.at[4 * t], sibling).wait_send()
            _remote(_halves(bx)[0], _halves(bx)[0], fs_in.at[4 * t + 1], fr_in.at[4 * t + 1], yn).wait_send()
            _remote(by, by, fs_in.at[4 * t + 2], fr_in.at[4 * t + 2], sibling).wait_send()
            _remote(_halves(by)[1], _halves(by)[1], fs_in.at[4 * t + 3], fr_in.at[4 * t + 3], xn).wait_send()
            bd = ins[t].at[_slot(1 - x, 1 - y, c)]
            _remote(bd, bd, gs_in.at[t], gr_in.at[t], sibling).wait_send()

    res = _split_call(
        body, name, tuple(pltpu.HBM(b.shape, b.dtype) for b in bufs),
        [HBM] * n + [SEM] * 6, tuple([HBM] * n), {t: t for t in range(n)})(
            *_hbm(bufs), send, recv, fsend, frecv, gsend, grecv)
    return list(res)


WHOLE, SHARDS, SHARDS2 = 0, 1, 2


def _sibling_copies(srcs, lands, kinds, c):
    pairs = []
    for s_ref, l_ref, kind in zip(srcs, lands, kinds):
        if kind == WHOLE:
            pairs.append((s_ref, l_ref))
        elif kind == SHARDS:
            pairs += [(s_ref.at[2 * k + (1 - c)], l_ref.at[k]) for k in range(NCHIP)]
        else:
            pairs += [(s_ref.at[w, 2 * k + (1 - c)], l_ref.at[w, k]) for w in range(2) for k in range(NCHIP)]
    return pairs


def _count_copies(kinds):
    return sum({WHOLE: 1, SHARDS: NCHIP, SHARDS2: 2 * NCHIP}[k] for k in kinds)


def _sibling_start(name, srcs, lands, deps, kinds=None):
    n, nd = len(srcs), len(deps)
    kinds = kinds or [SHARDS] * n
    ncp = _count_copies(kinds)

    def body(*refs):
        s_in, l_in = refs[:n], refs[n:2 * n]
        send, recv, token = refs[2 * n + nd], refs[2 * n + nd + 1], refs[4 * n + nd + 2]
        x, y, c, _ = _position()
        for i, (src, dst) in enumerate(_sibling_copies(s_in, l_in, kinds, c)):
            _remote(src, dst, send.at[i], recv.at[i], (x, y, 1 - c)).start()
        token[...] = jnp.zeros_like(token)

    both = list(srcs) + list(lands)
    res = _split_call(
        body, name,
        (pltpu.SemaphoreType.DMA((ncp,)), pltpu.SemaphoreType.DMA((ncp,)),
         *[pltpu.HBM(b.shape, b.dtype) for b in both], _token_shape()),
        [HBM] * (2 * n) + [ANY] * nd, (SEM, SEM, *[HBM] * (2 * n), pl.BlockSpec(memory_space=pltpu.VMEM)),
        {t: 2 + t for t in range(2 * n)})(*_hbm(both), *deps)
    return res[0], res[1], list(res[2:2 + n]), list(res[2 + n:2 + 2 * n]), res[2 + 2 * n]


def _sibling_finish(name, srcs, lands, send, recv, deps, kinds=None):
    n, nd = len(srcs), len(deps)
    kinds = kinds or [SHARDS] * n

    def body(*refs):
        s_in, l_in, send_in, recv_in = refs[:n], refs[n:2 * n], refs[2 * n], refs[2 * n + 1]
        x, y, c, _ = _position()
        for i, (src, dst) in enumerate(_sibling_copies(s_in, l_in, kinds, c)):
            cp = _remote(src, dst, send_in.at[i], recv_in.at[i], (x, y, 1 - c))
            cp.wait_send()
            cp.wait_recv()

    both = list(srcs) + list(lands)
    res = _split_call(
        body, name, tuple(pltpu.HBM(b.shape, b.dtype) for b in both),
        [HBM] * (2 * n) + [SEM, SEM] + [ANY] * nd, tuple([HBM] * (2 * n)),
        {t: t for t in range(2 * n)})(*_hbm(both), send, recv, *deps)
    return list(res[:n]), list(res[n:])


def _chips_start(name, parts, lands, deps):
    n, nd = len(parts), len(deps)

    def body(*refs):
        p_in, l_in = refs[:n], refs[n:2 * n]
        send, recv, token = refs[2 * n + nd], refs[2 * n + nd + 1], refs[4 * n + nd + 2]
        x, y, c, chips = _position()
        for t in range(n):
            for j, chip in enumerate(chips):
                _remote(p_in[t].at[2 * chip[0] + chip[1]], l_in[t].at[2 * x + y], send.at[3 * t + j], recv.at[3 * t + j],
                        (*chip, c)).start()
        token[...] = jnp.zeros_like(token)

    both = list(parts) + list(lands)
    res = _split_call(
        body, name,
        (pltpu.SemaphoreType.DMA((3 * n,)), pltpu.SemaphoreType.DMA((3 * n,)), *[pltpu.HBM(b.shape, b.dtype) for b in both],
         _token_shape()),
        [HBM] * (2 * n) + [ANY] * nd, (SEM, SEM, *[HBM] * (2 * n), pl.BlockSpec(memory_space=pltpu.VMEM)),
        {t: 2 + t for t in range(2 * n)})(*_hbm(both), *deps)
    return res[0], res[1], list(res[2:2 + n]), list(res[2 + n:2 + 2 * n]), res[2 + 2 * n]


def _chips_finish(name, parts, lands, send, recv, deps):
    n, nd = len(parts), len(deps)

    def body(*refs):
        p_in, l_in, send_in, recv_in = refs[:n], refs[n:2 * n], refs[2 * n], refs[2 * n + 1]
        x, y, c, chips = _position()
        for t in range(n):
            for j, chip in enumerate(chips):
                k = 2 * chip[0] + chip[1]
                _remote(p_in[t].at[k], l_in[t].at[k], send_in.at[3 * t + j], recv_in.at[3 * t + j], (*chip, c)).wait_recv()
                _remote(p_in[t].at[k], l_in[t].at[2 * x + y], send_in.at[3 * t + j], recv_in.at[3 * t + j],
                        (*chip, c)).wait_send()

    both = list(parts) + list(lands)
    res = _split_call(
        body, name, tuple(pltpu.HBM(b.shape, b.dtype) for b in both),
        [HBM] * (2 * n) + [SEM, SEM] + [ANY] * nd, tuple([HBM] * (2 * n)),
        {t: t for t in range(2 * n)})(*_hbm(both), send, recv, *deps)
    return list(res[:n]), list(res[n:])


def _pair_sum(name, grad, recv, where):
    _, r, w = grad.shape
    tr = _row_tile(r, w, budget=4 * 1024 * 1024)
    g4 = grad.reshape(NCHIP, 2, r, w)

    def body(where_ref, g_ref, r_ref, o_ref, own_ref):
        val = (g_ref[...].astype(F32) + r_ref[...].astype(F32)).astype(o_ref.dtype)
        o_ref[...] = val

        @pl.when(pl.program_id(1) == where_ref[1])
        def _():
            own_ref[...] = val

    out = _sds((NCHIP, r, w), grad.dtype)
    return _pcall(
        body, name=name, out_shape=[out, out], grid=(r // tr, NCHIP), prefetch=1,
        in_specs=[pl.BlockSpec((None, None, tr, w), lambda i, k, wr: (k, wr[0], i, 0)),
                  pl.BlockSpec((None, tr, w), lambda i, k, wr: (k, i, 0))],
        out_specs=[pl.BlockSpec((None, tr, w), lambda i, k, wr: (k, i, 0)),
                   pl.BlockSpec((None, tr, w), lambda i, k, wr: (wr[1], i, 0))],
    )(where, g4, recv)


def _row_tile(rows, width, budget=2 * 1024 * 1024):
    best = None
    for t in range(16, rows + 1, 16):
        if rows % t == 0 and t * width * 4 <= budget:
            best = t
    if best is None and rows * width * 4 <= budget:
        best = rows
    assert best is not None, (rows, width)
    return best


def _gemm(name, grid, operands, prods, extras, outs, epilogue, *, nk=1, acc_shape=None, aliases=None, separate=False,
          deps=(), prefetch=()):
    n_op, n_ex, n_out = len(operands), len(extras), len(outs)

    def body(*refs):
        refs = refs[len(prefetch):]
        ops, ex, out_refs = refs[:n_op], refs[n_op:n_op + n_ex], refs[n_op + n_ex:n_op + n_ex + n_out]
        parts = []
        for pr in prods:
            a, b = ops[pr[0]], ops[pr[1]]
            av = pr[3](a) if len(pr) > 3 and pr[3] else a[...]
            bv = pr[4](b) if len(pr) > 4 and pr[4] else b[...]
            parts.append(lax.dot_general(av, bv, pr[2], preferred_element_type=F32))
        if separate:
            epilogue(parts, ex, out_refs)
            return
        part = parts[0]
        for p in parts[1:]:
            part = part + p
        if nk == 1:
            epilogue(part, ex, out_refs)
        else:
            acc = refs[-1]
            k = pl.program_id(len(grid) - 1)

            @pl.when(k == 0)
            def _():
                acc[...] = part

            @pl.when(k > 0)
            def _():
                acc[...] += part

            @pl.when(k == nk - 1)
            def _():
                epilogue(acc[...], ex, out_refs)

    res = _pcall(
        body, name=name, out_shape=[o[0] for o in outs], grid=grid,
        in_specs=[o[1] for o in operands] + [e[1] for e in extras], out_specs=[o[1] for o in outs],
        scratch=[pltpu.VMEM(acc_shape, F32)] if nk > 1 else [], aliases=aliases, deps=deps, prefetch=len(prefetch),
    )(*prefetch, *[o[0] for o in operands], *[e[0] for e in extras])
    return list(res)


def _store(acc, ex, outs):
    outs[0][...] = acc.astype(outs[0].dtype)


def _store_add_extra(acc, ex, outs):
    v = acc
    for e in ex:
        v = v + e[...]
    outs[0][...] = v.astype(outs[0].dtype)


def _stacked(ref):
    b = ref[...]
    return b.reshape(b.shape[0] * b.shape[1], b.shape[2])


def _pick(c):
    return lambda ref: ref[c]


def _cols(c, width):
    return lambda ref: ref[:, c * width:(c + 1) * width]


def _gelu_parts(z):
    c = math.sqrt(2.0 / math.pi)
    t = jnp.tanh(c * (z + 0.044715 * (z * z * z)))
    val = 0.5 * z * (1.0 + t)
    grad = 0.5 * (1.0 + t) + 0.5 * z * (1.0 - t * t) * (c * (1.0 + 3.0 * 0.044715 * z * z))
    return val, grad


def _rms_fwd(name, h, g, deps=()):
    s, d = h.shape
    tr = _row_tile(s, d)

    def body(h_ref, g_ref, o_ref):
        hv = h_ref[...]
        r = lax.rsqrt(jnp.mean(hv * hv, axis=-1, keepdims=True) + RMS_EPS)
        o_ref[...] = (hv * r * g_ref[...]).astype(o_ref.dtype)

    return _pcall(
        body, name=name, out_shape=_sds((s, d), BF16), grid=(s // tr,),
        in_specs=[pl.BlockSpec((tr, d), lambda i: (i, 0)), pl.BlockSpec((1, d), lambda i: (0, 0))],
        out_specs=pl.BlockSpec((tr, d), lambda i: (i, 0)), deps=deps,
    )(h, g.reshape(1, d))


def _accumulate(ref, val, first):
    @pl.when(first)
    def _():
        ref[...] = val

    @pl.when(jnp.logical_not(first))
    def _():
        ref[...] += val


def _rms_bwd(name, h, g, dy, res, deps=()):
    s, d = h.shape
    tr = _row_tile(s, d, budget=2 * 1024 * 1024)

    def body(h_ref, g_ref, dy_ref, res_ref, dh_ref, dhb_ref, dg_ref, cs_ref):
        hv = h_ref[...]
        r = lax.rsqrt(jnp.mean(hv * hv, axis=-1, keepdims=True) + RMS_EPS)
        xhat = hv * r
        dyv = dy_ref[...]
        dxh = dyv * g_ref[...]
        dh = res_ref[...] + r * (dxh - xhat * jnp.mean(dxh * xhat, axis=-1, keepdims=True))
        dh_ref[...] = dh
        dhb_ref[...] = dh.astype(BF16)
        first = pl.program_id(0) == 0
        _accumulate(dg_ref, jnp.sum(dyv * xhat, axis=0, keepdims=True), first)
        _accumulate(cs_ref, jnp.sum(dh, axis=0, keepdims=True), first)

    row = pl.BlockSpec((tr, d), lambda i: (i, 0))
    vec = pl.BlockSpec((1, d), lambda i: (0, 0))
    return _pcall(
        body, name=name, out_shape=[_sds((s, d), F32), _sds((s, d), BF16), _sds((1, d), F32), _sds((1, d), F32)],
        grid=(s // tr,), in_specs=[row, vec, row, row], out_specs=[row, row, vec, vec], deps=deps,
    )(h, g.reshape(1, d), dy, res)


def _loss_bwd(name, h, g, target):
    s, d = h.shape
    tr = _row_tile(s, d, budget=1024 * 1024)

    def body(h_ref, g_ref, t_ref, loss_ref, dh_ref, dhb_ref, dg_ref):
        hv = h_ref[...]
        r = lax.rsqrt(jnp.mean(hv * hv, axis=-1, keepdims=True) + RMS_EPS)
        xhat = hv * r
        diff = xhat * g_ref[...] - t_ref[...]
        part = jnp.sum(jnp.sum(diff * diff, axis=1, keepdims=True), axis=0, keepdims=True) * (0.5 / d)
        dyv = diff * (1.0 / d)
        dxh = dyv * g_ref[...]
        dh = r * (dxh - xhat * jnp.mean(dxh * xhat, axis=-1, keepdims=True))
        dh_ref[...] = dh
        dhb_ref[...] = dh.astype(BF16)
        first = pl.program_id(0) == 0
        _accumulate(loss_ref, part, first)
        _accumulate(dg_ref, jnp.sum(dyv * xhat, axis=0, keepdims=True), first)

    row = pl.BlockSpec((tr, d), lambda i: (i, 0))
    vec = pl.BlockSpec((1, d), lambda i: (0, 0))
    one = pl.BlockSpec((1, 1), lambda i: (0, 0))
    return _pcall(
        body, name=name, out_shape=[_sds((1, 1), F32), _sds((s, d), F32), _sds((s, d), BF16), _sds((1, d), F32)],
        grid=(s // tr,), in_specs=[row, vec, row], out_specs=[one, row, row, vec],
    )(h, g.reshape(1, d), target)


def _tril_mask():
    return lax.broadcasted_iota(jnp.int32, (CHUNK, CHUNK), 0) >= lax.broadcasted_iota(jnp.int32, (CHUNK, CHUNK), 1)


def _gmlp_fwd(name, zp, gv, ws, bst):
    s, d2 = zp.shape
    d = d2 // 2
    gw = d // GROUPS

    def body(zp_ref, gv_ref, ws_ref, bst_ref, o_ref):
        u, _ = _gelu_parts(zp_ref[:, :d])
        v, _ = _gelu_parts(zp_ref[:, d:])
        rv = lax.rsqrt(jnp.mean(v * v, axis=-1, keepdims=True) + RMS_EPS)
        vn = (v * rv * gv_ref[...]).astype(BF16)
        tril = _tril_mask()
        for g in range(GROUPS):
            sl = slice(g * gw, (g + 1) * gw)
            wc = jnp.where(tril, ws_ref[g], 0.0).astype(BF16)
            sg = jnp.dot(wc, vn[:, sl], preferred_element_type=F32) + bst_ref[:, g:g + 1]
            o_ref[:, sl] = (u[:, sl] * sg).astype(o_ref.dtype)

    return _pcall(
        body, name=name, out_shape=_sds((s, d), BF16), grid=(s // CHUNK,),
        in_specs=[pl.BlockSpec((CHUNK, d2), lambda i: (i, 0)), pl.BlockSpec((1, d), lambda i: (0, 0)),
                  pl.BlockSpec((GROUPS, CHUNK, CHUNK), lambda i: (0, 0, 0)),
                  pl.BlockSpec((CHUNK, GROUPS), lambda i: (0, 0))],
        out_specs=pl.BlockSpec((CHUNK, d), lambda i: (i, 0)),
    )(zp, gv, ws, bst)


def _gmlp_bwd(name, zp, dgated, gv, ws, bst):
    s, d2 = zp.shape
    d = d2 // 2
    gw = d // GROUPS

    def body(zp_ref, dg_ref, gv_ref, ws_ref, bst_ref, dzp_ref, dws_ref, dbs_ref, dgv_ref, dvn_ref):
        u, gu = _gelu_parts(zp_ref[:, :d])
        v, gvv = _gelu_parts(zp_ref[:, d:])
        rv = lax.rsqrt(jnp.mean(v * v, axis=-1, keepdims=True) + RMS_EPS)
        vhat = v * rv
        vn = (vhat * gv_ref[...]).astype(BF16)
        tril = _tril_mask()
        first = pl.program_id(0) == 0
        ones = jnp.ones((8, gw), F32)

        @pl.when(first)
        def _():
            dws_ref[...] = jnp.zeros_like(dws_ref)
            dbs_ref[...] = jnp.zeros_like(dbs_ref)

        for g in range(GROUPS):
            sl = slice(g * gw, (g + 1) * gw)
            wc = jnp.where(tril, ws_ref[g], 0.0).astype(BF16)
            sg = jnp.dot(wc, vn[:, sl], preferred_element_type=F32) + bst_ref[:, g:g + 1]
            dgs = dg_ref[:, sl]
            ds = dgs * u[:, sl]
            dsb = ds.astype(BF16)
            dzp_ref[:, sl] = (dgs * sg * gu[:, sl]).astype(dzp_ref.dtype)
            dvn_ref[:, sl] = lax.dot_general(wc, dsb, TN, preferred_element_type=F32)
            dw = lax.dot_general(dsb, vn[:, sl], NT, preferred_element_type=F32)
            dws_ref[g] += jnp.where(tril, dw, 0.0)
            dbs_ref[g] += lax.dot_general(ones, ds, NT, preferred_element_type=F32, precision=lax.Precision.HIGHEST)
        dvn = dvn_ref[...]
        dvh = dvn * gv_ref[...]
        dv = rv * (dvh - vhat * jnp.mean(dvh * vhat, axis=-1, keepdims=True))
        dzp_ref[:, d:] = (dv * gvv).astype(dzp_ref.dtype)
        _accumulate(dgv_ref, jnp.sum(dvn * vhat, axis=0, keepdims=True), first)

    return _pcall(
        body, name=name,
        out_shape=[_sds((s, d2), BF16), _sds((GROUPS, CHUNK, CHUNK), F32), _sds((GROUPS, 8, CHUNK), F32),
                   _sds((1, d), F32)],
        grid=(s // CHUNK,),
        in_specs=[pl.BlockSpec((CHUNK, d2), lambda i: (i, 0)), pl.BlockSpec((CHUNK, d), lambda i: (i, 0)),
                  pl.BlockSpec((1, d), lambda i: (0, 0)), pl.BlockSpec((GROUPS, CHUNK, CHUNK), lambda i: (0, 0, 0)),
                  pl.BlockSpec((CHUNK, GROUPS), lambda i: (0, 0))],
        out_specs=[pl.BlockSpec((CHUNK, d2), lambda i: (i, 0)),
                   pl.BlockSpec((GROUPS, CHUNK, CHUNK), lambda i: (0, 0, 0)),
                   pl.BlockSpec((GROUPS, 8, CHUNK), lambda i: (0, 0, 0)), pl.BlockSpec((1, d), lambda i: (0, 0))],
        scratch=[pltpu.VMEM((CHUNK, d), F32)],
    )(zp, dgated, gv, ws, bst)


def _bucket_table():
    dist = np.arange(BLOCK)[:, None] + BLOCK - np.arange(2 * BLOCK)[None, :]
    in_window = (dist >= 0) & (dist < BLOCK)
    dd = np.clip(dist, 0, None)
    max_exact = N_BUCKETS // 2
    dl = np.maximum(dd, 1).astype(np.float32)
    large = max_exact + (np.log(dl / np.float32(max_exact)) / np.float32(math.log(MAX_DISTANCE / max_exact))
                         * np.float32(N_BUCKETS - max_exact)).astype(np.int32)
    large = np.minimum(large, N_BUCKETS - 1)
    bucket = np.where(dd < max_exact, dd, large)
    return np.where(in_window, bucket, -1).astype(np.int32).reshape(1, -1)


def _bias_table(name, rel_bias_t, buckets):
    nh = rel_bias_t.shape[0]
    p = buckets.shape[1]
    tp = 4096

    def body(rb_ref, bk_ref, o_ref):
        bk = bk_ref[...]
        onehot = (lax.broadcasted_iota(jnp.int32, (N_BUCKETS, tp), 0) == bk).astype(F32)
        val = jnp.dot(rb_ref[...], onehot, preferred_element_type=F32, precision=lax.Precision.HIGHEST)
        o_ref[...] = jnp.where(bk >= 0, val, NEG_INF)

    return _pcall(
        body, name=name, out_shape=_sds((nh, p), F32), grid=(p // tp,),
        in_specs=[pl.BlockSpec((nh, N_BUCKETS), lambda i: (0, 0)), pl.BlockSpec((1, tp), lambda i: (0, i))],
        out_specs=pl.BlockSpec((nh, tp), lambda i: (0, i)),
    )(rel_bias_t, buckets)


def _bias_grad(name, dbiases, buckets):
    nh, p = dbiases[0].shape
    n = len(dbiases)
    tp = 4096

    def body(*refs):
        bk_ref, o_ref = refs[n], refs[n + 1]
        onehot = (lax.broadcasted_iota(jnp.int32, (N_BUCKETS, tp), 0) == bk_ref[...]).astype(F32)
        db = refs[0][...]
        for r in refs[1:n]:
            db = db + r[...]
        part = lax.dot_general(onehot, db, NT, preferred_element_type=F32, precision=lax.Precision.HIGHEST)
        _accumulate(o_ref, part, pl.program_id(0) == 0)

    return _pcall(
        body, name=name, out_shape=_sds((N_BUCKETS, nh), F32), grid=(p // tp,),
        in_specs=[pl.BlockSpec((nh, tp), lambda i: (0, i))] * n + [pl.BlockSpec((1, tp), lambda i: (0, i))],
        out_specs=pl.BlockSpec((N_BUCKETS, nh), lambda i: (0, 0)),
    )(*dbiases, buckets)


def _stack_heads(ref, g):
    base = g * KV_GROUP * HEAD_DIM
    return jnp.concatenate([ref[:, base + hh * HEAD_DIM:base + (hh + 1) * HEAD_DIM] for hh in range(KV_GROUP)], axis=0)


def _attn_probs(q, kb, bias, s_ref, first_head):
    penalty = jnp.where(pl.program_id(1) > 0, 0.0, NEG_INF).astype(F32)
    col = lax.broadcasted_iota(jnp.int32, (1, 2 * BLOCK), 1)
    bias = bias.reshape(KV_GROUP * BLOCK, 2 * BLOCK) + jnp.where(col < BLOCK, penalty, 0.0)
    sink = jnp.concatenate([jnp.full((BLOCK, 1), s_ref[first_head + hh], F32) for hh in range(KV_GROUP)], axis=0)
    s = lax.dot_general(q, kb, NT, preferred_element_type=F32) * 0.125 + bias
    m = jnp.maximum(jnp.max(s, axis=-1, keepdims=True), sink)
    p = jnp.exp(s - m)
    es = jnp.exp(sink - m)
    inv = 1.0 / (jnp.sum(p, axis=-1, keepdims=True) + es)
    return p * inv, es * inv


def _attn_specs(ng):
    gq = ng * KV_GROUP * HEAD_DIM
    q_spec = pl.BlockSpec((BLOCK, gq), lambda kh, i: (i, kh))
    prev = pl.BlockSpec((ng, BLOCK, HEAD_DIM), lambda kh, i: (kh, jnp.maximum(i - 1, 0), 0))
    cur = pl.BlockSpec((ng, BLOCK, HEAD_DIM), lambda kh, i: (kh, i, 0))
    bias = pl.BlockSpec((ng * KV_GROUP, BLOCK, 2 * BLOCK), lambda kh, i: (kh, 0, 0))
    smem = pl.BlockSpec(memory_space=pltpu.SMEM)
    probs = pl.BlockSpec((ng, None, KV_GROUP * BLOCK, 2 * BLOCK), lambda kh, i: (kh, i, 0, 0))
    sink_probs = pl.BlockSpec((ng, None, KV_GROUP * BLOCK, 1), lambda kh, i: (kh, i, 0, 0))
    return q_spec, prev, cur, bias, smem, probs, sink_probs


def _kv_heads_per_step(nkv):
    return 2 if nkv % 2 == 0 else 1


def _attn_fwd(name, q, k, v, bias, sinks):
    s, dq = q.shape
    nkv = k.shape[0]
    ng = 1
    q_spec, prev, cur, bias_spec, smem, p_spec, ps_spec = _attn_specs(ng)

    def body(q_ref, kp_ref, kc_ref, vp_ref, vc_ref, b_ref, s_ref, o_ref, p_ref, ps_ref):
        for g in range(ng):
            kb = jnp.concatenate([kp_ref[g], kc_ref[g]], axis=0)
            vb = jnp.concatenate([vp_ref[g], vc_ref[g]], axis=0)
            p, ps = _attn_probs(_stack_heads(q_ref, g), kb, b_ref[g * KV_GROUP:(g + 1) * KV_GROUP], s_ref,
                                (pl.program_id(0) * ng + g) * KV_GROUP)
            pb = p.astype(BF16)
            p_ref[g] = pb
            ps_ref[g] = ps
            o = jnp.dot(pb, vb, preferred_element_type=F32)
            for hh in range(KV_GROUP):
                col = (g * KV_GROUP + hh) * HEAD_DIM
                o_ref[:, col:col + HEAD_DIM] = o[hh * BLOCK:(hh + 1) * BLOCK].astype(o_ref.dtype)

    return _pcall(
        body, name=name,
        out_shape=[_sds((s, dq), BF16), _sds((nkv, s // BLOCK, KV_GROUP * BLOCK, 2 * BLOCK), BF16),
                   _sds((nkv, s // BLOCK, KV_GROUP * BLOCK, 1), F32)],
        grid=(nkv // ng, s // BLOCK),
        in_specs=[q_spec, prev, cur, prev, cur, bias_spec, smem], out_specs=[q_spec, p_spec, ps_spec],
    )(q, k, k, v, v, bias, sinks)


def _attn_bwd(name, q, k, v, do, probs, sink_probs):
    s, dq = q.shape
    nkv = k.shape[0]
    ng = _kv_heads_per_step(nkv)
    gq = ng * KV_GROUP * HEAD_DIM
    q_spec, prev, cur, bias_spec, _, p_spec, ps_spec = _attn_specs(ng)

    def body(q_ref, do_ref, kp_ref, kc_ref, vp_ref, vc_ref, p_ref, ps_ref,
             dq_ref, dbq_ref, dkc_ref, dkp_ref, dvc_ref, dvp_ref, dbias_ref, dsink_ref):
        @pl.when(pl.program_id(1) == 0)
        def _():
            dbias_ref[...] = jnp.zeros_like(dbias_ref)
            dsink_ref[...] = jnp.zeros_like(dsink_ref)
            dbq_ref[...] = jnp.zeros_like(dbq_ref)

        for g in range(ng):
            kb = jnp.concatenate([kp_ref[g], kc_ref[g]], axis=0)
            vb = jnp.concatenate([vp_ref[g], vc_ref[g]], axis=0)
            q, do = _stack_heads(q_ref, g), _stack_heads(do_ref, g)
            pb = p_ref[g]
            p = pb.astype(F32)
            dp = lax.dot_general(do, vb, NT, preferred_element_type=F32)
            delta = jnp.sum(p * dp, axis=-1, keepdims=True)
            ds = p * (dp - delta)
            dsb = ds.astype(BF16)
            dq = jnp.dot(dsb, kb, preferred_element_type=F32) * 0.125
            dsk = -(ps_ref[g] * delta)
            for hh in range(KV_GROUP):
                col, rows = (g * KV_GROUP + hh) * HEAD_DIM, slice(hh * BLOCK, (hh + 1) * BLOCK)
                dq_ref[:, col:col + HEAD_DIM] = dq[rows].astype(dq_ref.dtype)
                dbq_ref[:, col:col + HEAD_DIM] += jnp.sum(dq[rows], axis=0, keepdims=True)
                dsink_ref[g, :, hh:hh + 1] += jnp.sum(dsk[rows], axis=0, keepdims=True)
            dkb = lax.dot_general(dsb, q, TN, preferred_element_type=F32) * 0.125
            dvb = lax.dot_general(pb, do, TN, preferred_element_type=F32)
            dkp_ref[g], dkc_ref[g] = dkb[:BLOCK], dkb[BLOCK:]
            dvp_ref[g], dvc_ref[g] = dvb[:BLOCK], dvb[BLOCK:]
            dbias_ref[g * KV_GROUP:(g + 1) * KV_GROUP] += ds.reshape(KV_GROUP, BLOCK, 2 * BLOCK)

    kv_out = _sds((nkv, s, HEAD_DIM), F32)
    return _pcall(
        body, name=name,
        out_shape=[_sds((s, dq), BF16), _sds((1, dq), F32), kv_out, kv_out, kv_out, kv_out,
                   _sds((nkv * KV_GROUP, BLOCK, 2 * BLOCK), F32), _sds((nkv, 1, KV_GROUP), F32)],
        grid=(nkv // ng, s // BLOCK),
        in_specs=[q_spec, q_spec, prev, cur, prev, cur, p_spec, ps_spec],
        out_specs=[q_spec, pl.BlockSpec((1, gq), lambda kh, i: (0, kh)), cur, cur, cur, cur, bias_spec,
                   pl.BlockSpec((ng, 1, KV_GROUP), lambda kh, i: (kh, 0, 0))],
    )(q, do, k, k, v, v, probs, sink_probs)


def _kv_grad(name, parts):
    nkv, s, _ = parts[0][0].shape
    nb = s // BLOCK
    w = 2 * nkv * HEAD_DIM
    n = len(parts)

    def body(*refs):
        o_ref, cs_ref = refs[4 * n], refs[4 * n + 1]
        i = pl.program_id(0)
        keep = jnp.where(i < nb - 1, 1.0, 0.0).astype(F32)

        @pl.when(i == 0)
        def _():
            cs_ref[...] = jnp.zeros_like(cs_ref)

        for which in range(2):
            for hh in range(nkv):
                val = None
                for l in range(n):
                    cur_ref, nxt_ref = refs[4 * l + 2 * which], refs[4 * l + 2 * which + 1]
                    t = cur_ref[hh] + keep * nxt_ref[hh]
                    val = t if val is None else val + t
                sl = slice((which * nkv + hh) * HEAD_DIM, (which * nkv + hh + 1) * HEAD_DIM)
                o_ref[:, sl] = val.astype(o_ref.dtype)
                cs_ref[:, sl] += jnp.sum(val, axis=0, keepdims=True)

    cur = pl.BlockSpec((nkv, BLOCK, HEAD_DIM), lambda i: (0, i, 0))
    nxt = pl.BlockSpec((nkv, BLOCK, HEAD_DIM), lambda i: (0, jnp.minimum(i + 1, nb - 1), 0))
    flat = [a for p in parts for a in p]
    return _pcall(
        body, name=name, out_shape=[_sds((s, w), BF16), _sds((1, w), F32)], grid=(nb,),
        in_specs=[cur, nxt] * (2 * n),
        out_specs=[pl.BlockSpec((BLOCK, w), lambda i: (i, 0)), pl.BlockSpec((1, w), lambda i: (0, 0))],
    )(*flat)


def _adamw_math(w, g, m, v):
    m = ADAM_B1 * m + (1.0 - ADAM_B1) * g
    v = ADAM_B2 * v + (1.0 - ADAM_B2) * (g * g)
    m_hat = m / (1.0 - ADAM_B1 ** ADAM_STEP)
    v_hat = v / (1.0 - ADAM_B2 ** ADAM_STEP)
    delta = -ADAM_LR * (m_hat / (jnp.sqrt(v_hat) + ADAM_EPS) + ADAM_WD * w)
    return delta, m, v


def _adamw_shard(name, w, m, v, parts, row0, layer, prev, deps=(), own=None, where=None):
    _, r, wd = w.shape
    tr = _row_tile(r, wd, budget=3 * 512 * 1024)
    assert row0 % tr == 0

    def step(w_ref, m_ref, v_ref, g, g_ref, d_ref, nm_ref, nv_ref):
        delta, nm, nv = _adamw_math(w_ref[...], g, m_ref[...], v_ref[...])
        g_ref[...], d_ref[...], nm_ref[...], nv_ref[...] = g, delta, nm, nv

    out = _sds(w.shape, F32)
    if own is None:
        def body(w_ref, m_ref, v_ref, p_ref, a0, a1, a2, a3, g_ref, d_ref, nm_ref, nv_ref):
            g = p_ref[0].astype(F32)
            for k in range(1, NCHIP):
                g = g + p_ref[k].astype(F32)
            step(w_ref, m_ref, v_ref, g, g_ref, d_ref, nm_ref, nv_ref)

        par = pl.BlockSpec((None, tr, wd), lambda i: (layer, i, 0))
        return _pcall(
            body, name=name, out_shape=[out, out, out, out], grid=(r // tr,),
            in_specs=[par, par, par, pl.BlockSpec((NCHIP, tr, wd), lambda i: (0, row0 // tr + i, 0)), ANY, ANY, ANY, ANY],
            out_specs=[par, par, par, par], aliases={4: 0, 5: 1, 6: 2, 7: 3}, deps=deps,
        )(w, m, v, parts, *prev)

    def body(where_ref, w_ref, m_ref, v_ref, p_ref, o_ref, a0, a1, a2, a3, g_ref, d_ref, nm_ref, nv_ref):
        mine = lax.broadcasted_iota(jnp.int32, (tr, wd), 0) * 0 + where_ref[1]
        g = None
        for k in range(NCHIP):
            t = jnp.where(mine == k, o_ref[...], p_ref[k]).astype(F32)
            g = t if g is None else g + t
        step(w_ref, m_ref, v_ref, g, g_ref, d_ref, nm_ref, nv_ref)

    par = pl.BlockSpec((None, tr, wd), lambda i, wr: (layer, i, 0))
    return _pcall(
        body, name=name, out_shape=[out, out, out, out], grid=(r // tr,), prefetch=1,
        in_specs=[par, par, par, pl.BlockSpec((NCHIP, tr, wd), lambda i, wr: (0, row0 // tr + i, 0)),
                  pl.BlockSpec((None, tr, wd), lambda i, wr: (wr[1], row0 // tr + i, 0)), ANY, ANY, ANY, ANY],
        out_specs=[par, par, par, par], aliases={6: 0, 7: 1, 8: 2, 9: 3}, deps=deps,
    )(where, w, m, v, parts, own, *prev)


def _sum_devices(name, gathered):
    _, r, wd = gathered.shape

    def body(g_ref, o_ref):
        acc = g_ref[0]
        for k in range(1, NDEV):
            acc = acc + g_ref[k]
        o_ref[...] = acc

    return _pcall(body, name=name, out_shape=_sds((r, wd), F32), grid=(1,),
                  in_specs=[pl.BlockSpec((NDEV, r, wd), lambda i: (0, 0, 0))],
                  out_specs=pl.BlockSpec((r, wd), lambda i: (0, 0)))(gathered)


def _adamw_flat(name, w, g, m, v):
    shape = w.shape

    def body(w_ref, g_ref, m_ref, v_ref, d_ref, nm_ref, nv_ref):
        d_ref[...], nm_ref[...], nv_ref[...] = _adamw_math(w_ref[...], g_ref[...], m_ref[...], v_ref[...])

    spec = pl.BlockSpec(shape, lambda i: (0, 0))
    out = _sds(shape, F32)
    return _pcall(body, name=name, out_shape=[out, out, out], grid=(1,), in_specs=[spec] * 4,
                  out_specs=[spec] * 3)(w, g, m, v)


def _cast_into(name, src, layer, buf, row0, me):
    _, r, wd = src.shape
    tr = _row_tile(r, wd)
    assert row0 % tr == 0

    def body(me_ref, s_ref, b_ref, o_ref):
        o_ref[...] = s_ref[...].astype(o_ref.dtype)

    return _pcall(
        body, name=name, out_shape=_sds(buf.shape, buf.dtype), grid=(r // tr,), prefetch=1,
        in_specs=[pl.BlockSpec((None, tr, wd), lambda i, mr: (layer, i, 0)), ANY],
        out_specs=pl.BlockSpec((None, tr, wd), lambda i, mr: (mr[0], row0 // tr + i, 0)), aliases={2: 0},
    )(me, src, buf)


def _pack(arrays):
    rows = []
    for a in arrays:
        flat = a.reshape(-1).astype(F32)
        pad = (-flat.shape[0]) % 1024
        rows.append(jnp.pad(flat, (0, pad)).reshape(-1, 128))
    return jnp.concatenate(rows, axis=0)


def _unpack(packed, shapes):
    out, r = [], 0
    for shp in shapes:
        n = int(np.prod(shp))
        nr = (n + 1023) // 1024 * 8
        out.append(packed[r:r + nr].reshape(-1)[:n].reshape(shp))
        r += nr
    return out


def kernel(x, mix_norm, ffn_norm, a_w_in, a_norm_v, a_w_s, a_b_s, a_w_out, kv_norm, w_kv, b_kv, b_w_q, b_b_q, b_sinks, b_w_o, b_b_o, rel_bias, ffn_w_gate, ffn_w_up, ffn_w_down, final_norm, loss_target, m_mix_norm, m_ffn_norm, m_a_w_in, m_a_norm_v, m_a_w_s, m_a_b_s, m_a_w_out, m_kv_norm, m_w_kv, m_b_kv, m_b_w_q, m_b_b_q, m_b_sinks, m_b_w_o, m_b_b_o, m_rel_bias, m_ffn_w_gate, m_ffn_w_up, m_ffn_w_down, m_final_norm, v_mix_norm, v_ffn_norm, v_a_w_in, v_a_norm_v, v_a_w_s, v_a_b_s, v_a_w_out, v_kv_norm, v_w_kv, v_b_kv, v_b_w_q, v_b_b_q, v_b_sinks, v_b_w_o, v_b_b_o, v_rel_bias, v_ffn_w_gate, v_ffn_w_up, v_ffn_w_down, v_final_norm):
    _, S, D = x.shape
    LA, LB, L = a_w_in.shape[0], b_w_q.shape[0], ffn_w_gate.shape[0]
    F = ffn_w_gate.shape[2]
    DS = D // NDEV
    ZC = a_w_in.shape[2]
    KVW = w_kv.shape[1]
    NKV = KVW // (2 * HEAD_DIM)
    NH = D // HEAD_DIM
    assert ZC * NDEV == 2 * D and NH == NKV * KV_GROUP and S % BLOCK == 0
    TM = min(1024, S)
    TN_ = min(1024, D)
    TS = min(512, D)
    KC = 4

    ix, iy, ic = lax.axis_index("x"), lax.axis_index("y"), lax.axis_index("c")
    me = (4 * ix + 2 * iy + ic).astype(jnp.int32)
    me1 = me.reshape(1)
    where = jnp.stack([ic, 2 * ix + iy]).astype(jnp.int32)

    def tr3(a):
        return jnp.transpose(a, (0, 2, 1))

    gate_t, up_t = tr3(ffn_w_gate), tr3(ffn_w_up)
    w_kv3 = w_kv.reshape((1,) + w_kv.shape)

    def layer_arrays(l):
        arrs = [("gu", 2 * F, D, [(gate_t, l, 0), (up_t, l, F)]), ("down", F, D, [(ffn_w_down, l, 0)])]
        if l < LA:
            arrs += [("win", D, ZC, [(a_w_in, l, 0)]), ("wout", DS, D, [(a_w_out, l, 0)])]
            if l == LA - 1:
                arrs.append(("wkv", DS, KVW, [(w_kv3, 0, 0)]))
        else:
            i_b = l - LA
            arrs.append(("wqo", 2 * DS, D, [(b_w_q, i_b, 0), (b_w_o, i_b, DS)]))
        return arrs

    gathers = []

    def gather_begin(l, deps):
        g = gathers[l]
        g["send"], g["recv"], g["bufs"], g["token"] = _relay_start(f"relay_start{l}", g["bufs"], deps)

    for l in range(L):
        keys, bufs = [], []
        for key, rows, width, sources in layer_arrays(l):
            buf = lax.empty((NDEV, rows, width), BF16)
            for si, (src, li, row0) in enumerate(sources):
                buf = _cast_into(f"cast_{key}{l}_{si}", src, li, buf, row0, me1)
            keys.append(key)
            bufs.append(buf)
        gathers.append(dict(keys=keys, bufs=bufs))
        if l == 0:
            nv_rows = _pack([a_norm_v])
            nv = _cast_into("put_norm_v", nv_rows.reshape((1,) + nv_rows.shape), 0,
                            lax.empty((NDEV,) + nv_rows.shape, F32), 0, me1)
            nv_send, nv_recv, nv_bufs, token = _gather_start("gather_norm_v_start", [nv], [])
            gather_begin(0, [token])

    def gather_relay(l, deps):
        g = gathers[l]
        g["fsend"], g["frecv"], g["bufs"], tok = _relay_neighbors(f"relay_neighbors{l}", g["bufs"], g["send"], g["recv"],
                                                                  deps)
        if l + 1 < L:
            gather_begin(l + 1, [tok])
            tok = gathers[l + 1]["token"]
        return tok

    def finish_gather(l, deps):
        g = gathers[l]
        gsend, grecv, bufs = _relay_diagonal(f"relay_diagonal{l}", g["bufs"], g["fsend"], g["frecv"], deps)
        bufs = _relay_finish(f"relay_finish{l}", bufs, g["send"], g["recv"], g["fsend"], g["frecv"], gsend, grecv)
        return dict(zip(g["keys"], bufs))

    token = gather_relay(0, [gathers[0]["token"]] + [b for g in gathers[1:] for b in g["bufs"]])

    buckets = jnp.asarray(_bucket_table())
    bias = _bias_table("bias_table", rel_bias.T, buckets).reshape(NH, BLOCK, 2 * BLOCK)

    def rows_full(tm):
        return pl.BlockSpec((tm, D), lambda i, j: (i, 0))

    def tile(tm, tn):
        return pl.BlockSpec((tm, tn), lambda i, j: (i, j))

    vec_tile = pl.BlockSpec((1, TN_), lambda i, j: (0, j))

    def ffn_forward(l, wl, h_mid, tag, deps):
        xf = _rms_fwd(f"ffn_norm_fwd{tag}", h_mid, ffn_norm[l], deps=deps)

        def ep(parts, ex, outs):
            a, b = parts
            sg = jax.nn.sigmoid(a)
            silu = a * sg
            outs[0][0] = (b * (sg * (1.0 + a * (1.0 - sg)))).astype(BF16)
            outs[0][1] = silu.astype(BF16)
            outs[1][...] = (silu * b).astype(BF16)

        ab, hid = _gemm(
            f"ffn_up{tag}", (S // TM, NDEV),
            [(xf, rows_full(TM)),
             (wl["gu"], pl.BlockSpec((None, F, D), lambda i, e: (e, 0, 0))),
             (wl["gu"], pl.BlockSpec((None, F, D), lambda i, e: (e, 1, 0)))],
            [(0, 1, NT), (0, 2, NT)], [],
            [(_sds((2, NDEV, S, F), BF16), pl.BlockSpec((2, None, TM, F), lambda i, e: (0, e, i, 0))),
             (_sds((NDEV, S, F), BF16), pl.BlockSpec((None, TM, F), lambda i, e: (e, i, 0)))],
            ep, separate=True)
        (h_out,) = _gemm(
            f"ffn_down{tag}", (S // TM, D // TN_, NDEV // KC),
            [(hid, pl.BlockSpec((KC, TM, F), lambda i, j, k: (k, i, 0))),
             (wl["down"], pl.BlockSpec((KC, F, TN_), lambda i, j, k: (k, 0, j)))],
            [(0, 1, NN, _pick(c), _pick(c)) for c in range(KC)],
            [(h_mid, pl.BlockSpec((TM, TN_), lambda i, j, k: (i, j)))],
            [(_sds((S, D), F32), pl.BlockSpec((TM, TN_), lambda i, j, k: (i, j)))],
            _store_add_extra, nk=NDEV // KC, acc_shape=(TM, TN_))
        return dict(h_mid=h_mid, xf=xf, ab=ab, hid=hid), h_out

    def stacked_rows_gemm(name, a, wmat, blk, extras, ep, out_dtype):
        return _gemm(
            name, (S // TM, D // TN_),
            [(a, rows_full(TM)), (wmat, pl.BlockSpec((NDEV, DS, TN_), lambda i, j: (0, blk, j)))],
            [(0, 1, NN, None, _stacked)], extras,
            [(_sds((S, D), out_dtype), tile(TM, TN_))], ep)[0]

    def back_rows_gemm(name, a, wmat, blk, out_dtype, deps=()):
        return _gemm(
            name, (S // TM, NDEV),
            [(a, rows_full(TM)), (wmat, pl.BlockSpec((None, DS, D), lambda i, e: (e, blk, 0)))],
            [(0, 1, NT)], [], [(_sds((S, D), out_dtype), pl.BlockSpec((TM, DS), lambda i, e: (i, e)))], _store,
            deps=deps)[0]

    def grad_rows_gemm(name, act, d_bf, buf, blk):
        return _gemm(
            name, (NDEV,),
            [(act, pl.BlockSpec((S, DS), lambda e: (0, e))), (d_bf, pl.BlockSpec((S, D), lambda e: (0, 0)))],
            [(0, 1, TN)], [(buf, ANY)],
            [(_sds(buf.shape, BF16), pl.BlockSpec((None, DS, D), lambda e: (e, blk, 0)))],
            _store, aliases={2: 0})[0]

    saved, weights = [], []
    h = x.reshape(S, D)
    k_heads = v_heads = hn = h_kv = norm_v = None
    for layer in range(L):
        wl = finish_gather(layer, [token] if layer == 0 else [h])
        weights.append(wl)
        if layer == 0:
            nv_fsend, nv_frecv, nv_bufs = _gather_forward("gather_norm_v_forward", nv_bufs, nv_send, nv_recv,
                                                          [wl["down"]])
            (nv_all,) = _gather_finish("gather_norm_v_finish", nv_bufs, nv_send, nv_recv, nv_fsend, nv_frecv)
            norm_v = jnp.transpose(nv_all.reshape(NDEV, -1)[:, :LA * DS].reshape(NDEV, LA, DS), (1, 0, 2)).reshape(LA, D)
        sv = dict(h_in=h)
        xn = _rms_fwd(f"mix_norm_fwd{layer}", h, mix_norm[layer])
        sv["xn"] = xn
        if layer < LA:
            i_a = layer
            (zp,) = _gemm(
                f"gmlp_in{layer}", (S // TM, NDEV),
                [(xn, rows_full(TM)), (wl["win"], pl.BlockSpec((None, D, ZC), lambda i, e: (e, 0, 0)))],
                [(0, 1, NN)], [], [(_sds((S, 2 * D), F32), pl.BlockSpec((TM, ZC), lambda i, e: (i, e)))], _store)
            bst = a_b_s[i_a].T
            gated = _gmlp_fwd(f"gmlp_gate{layer}", zp, norm_v[i_a].reshape(1, D), a_w_s[i_a], bst)
            sv.update(zp=zp, gated=gated, bst=bst)
            h_mid = stacked_rows_gemm(f"gmlp_out{layer}", gated, wl["wout"], 0, [(h, tile(TM, TN_))],
                                      _store_add_extra, F32)
        else:
            i_b = layer - LA
            q = stacked_rows_gemm(f"attn_q{layer}", xn, wl["wqo"], 0, [(b_b_q[i_b].reshape(1, D), vec_tile)],
                                  _store_add_extra, BF16)
            attn, probs, sink_probs = _attn_fwd(f"attn_fwd{layer}", q, k_heads, v_heads, bias, b_sinks[i_b])
            sv.update(q=q, attn=attn, probs=probs, sink_probs=sink_probs)
            h_mid = stacked_rows_gemm(f"attn_o{layer}", attn, wl["wqo"], 1,
                                      [(h, tile(TM, TN_)), (b_b_o[i_b].reshape(1, D), vec_tile)],
                                      _store_add_extra, F32)
        relay_token = [gather_relay(layer + 1, [h_mid])] if layer + 1 < L else []
        fsv, h = ffn_forward(layer, wl, h_mid, str(layer), relay_token)
        sv.update(fsv)
        saved.append(sv)
        if layer == LA - 1:
            h_kv = h
            hn = _rms_fwd("kv_norm_fwd", h, kv_norm)

            def kv_ep(acc, ex, outs):
                val = acc + ex[0][...]
                for hh in range(NKV):
                    outs[0][hh] = val[:, hh * HEAD_DIM:(hh + 1) * HEAD_DIM].astype(BF16)
                    outs[1][hh] = val[:, (NKV + hh) * HEAD_DIM:(NKV + hh + 1) * HEAD_DIM].astype(BF16)

            k_heads, v_heads = _gemm(
                "kv_proj", (S // TM,),
                [(hn, pl.BlockSpec((TM, D), lambda i: (i, 0))),
                 (wl["wkv"], pl.BlockSpec((NDEV, DS, KVW), lambda i: (0, 0, 0)))],
                [(0, 1, NN, None, _stacked)], [(b_kv.reshape(1, KVW), pl.BlockSpec((1, KVW), lambda i: (0, 0)))],
                [(_sds((NKV, S, HEAD_DIM), BF16), pl.BlockSpec((NKV, TM, HEAD_DIM), lambda i: (0, i, 0)))] * 2,
                kv_ep)

    loss11, d, d_bf, g_final = _loss_bwd("loss_bwd", h, final_norm, loss_target.reshape(S, D))
    loss = lax.psum(loss11[0, 0], AXES)

    g_mix, g_ffn = [None] * L, [None] * L
    g_ws, g_bs, g_nv = [None] * LA, [None] * LA, [None] * LA
    g_bq, g_sink, g_bo = [None] * LB, [None] * LB, [None] * LB
    dbiases = []
    kv_parts = []
    g_kvn = g_bkv = None
    exchanges = [[] for _ in range(L)]
    pending = None
    grads_wkv = None
    newest = []

    def new_grads(l):
        return {key: lax.empty((NDEV, rows, width), BF16) for key, rows, width, _ in layer_arrays(l)}

    def exchange_begin(tag, l, gl, keys):
        grads = [gl[k] for k in keys]
        lands = [lax.empty((NCHIP,) + g.shape[1:], BF16) for g in grads]
        send, recv, grads, lands, tok = _sibling_start(f"rs_sibling_start{tag}", grads, lands, [])
        newest[:] = [tok]
        return dict(tag=tag, layer=l, keys=keys, grads=grads, lands=lands, send=send, recv=recv)

    def exchange_middle(st, dep):
        tag = st["tag"]
        grads, lands = _sibling_finish(f"rs_sibling_finish{tag}", st["grads"], st["lands"], st["send"], st["recv"], [dep])
        sums, own = [], []
        for t, key in enumerate(st["keys"]):
            s_, o_ = _pair_sum(f"pair_sum_{key}{tag}", grads[t], lands[t], where)
            sums.append(s_)
            own.append(o_)
        send, recv, sums, own, tok = _chips_start(f"rs_chips_start{tag}", sums, own, [])
        newest[:] = [tok]
        st.update(sums=sums, own=own, send2=send, recv2=recv)
        exchanges[st["layer"]].append(st)

    def exchange_end(st, dep):
        sums, lands = _chips_finish(f"rs_chips_finish{st['tag']}", st["sums"], st["own"], st["send2"], st["recv2"], [dep])
        own = dict(zip(st["keys"], sums)) if st.get("direct") else {k: None for k in st["keys"]}
        return dict(zip(st["keys"], lands)), own

    for layer in reversed(range(L)):
        sv, wl = saved[layer], weights[layer]
        tag = str(layer)
        gl = new_grads(layer)
        if grads_wkv is not None and layer == LA - 1:
            gl["wkv"] = grads_wkv
        def dhid_ep(acc, ex, outs):
            outs[0][0] = (acc * ex[0][0].astype(F32)).astype(BF16)
            outs[0][1] = (acc * ex[0][1].astype(F32)).astype(BF16)

        ab_spec = pl.BlockSpec((2, None, TM, F), lambda i, e: (0, e, i, 0))
        (dab,) = _gemm(
            f"ffn_dhid{tag}", (S // TM, NDEV),
            [(d_bf, rows_full(TM)), (wl["down"], pl.BlockSpec((None, F, D), lambda i, e: (e, 0, 0)))],
            [(0, 1, NT)], [(sv["ab"], ab_spec)], [(_sds((2, NDEV, S, F), BF16), ab_spec)], dhid_ep,
            deps=list(newest))
        if pending:
            exchange_middle(pending, dab)
        act_kinds = [SHARDS, WHOLE, SHARDS2, WHOLE]
        act_lands = [lax.empty((NCHIP, S, F), BF16), lax.empty((S, D), BF16), lax.empty((2, NCHIP, S, F), BF16),
                     lax.empty((S, D), BF16)]
        a_send, a_recv, act, act_lands, tok = _sibling_start(f"act_start{tag}", [sv["hid"], d_bf, dab, sv["xf"]], act_lands,
                                                             list(newest), act_kinds)
        newest[:] = [tok]
        (dxf,) = _gemm(
            f"ffn_dx{tag}", (S // TM, D // TN_, 2 * NDEV // KC),
            [(act[2].reshape(2 * NDEV // KC, KC, S, F), pl.BlockSpec((None, KC, TM, F), lambda i, j, k: (k, 0, i, 0))),
             (wl["gu"], pl.BlockSpec((KC, F, TN_), lambda i, j, k: (k % (NDEV // KC), k // (NDEV // KC), j)))],
            [(0, 1, NN, _pick(c), _pick(c)) for c in range(KC)], [],
            [(_sds((S, D), F32), pl.BlockSpec((TM, TN_), lambda i, j, k: (i, j)))],
            _store, nk=2 * NDEV // KC, acc_shape=(TM, TN_), deps=list(newest))
        (hid_o, dout_o, dab_o, xf_o), (hid_s, dout_s, dab_s, xf_s) = _sibling_finish(
            f"act_finish{tag}", act, act_lands, a_send, a_recv, [dxf], act_kinds)
        (p_down,) = _gemm(
            f"ffn_dwdown{tag}", (NCHIP, D // TN_),
            [(hid_o.reshape(NCHIP, 2, S, F), pl.BlockSpec((None, None, S, F), lambda k, j, wr: (k, wr[0], 0, 0))),
             (dout_o, pl.BlockSpec((S, TN_), lambda k, j, wr: (0, j))),
             (hid_s, pl.BlockSpec((None, S, F), lambda k, j, wr: (k, 0, 0))),
             (dout_s, pl.BlockSpec((S, TN_), lambda k, j, wr: (0, j)))],
            [(0, 1, TN), (2, 3, TN)], [],
            [(_sds((NCHIP, F, D), BF16), pl.BlockSpec((None, F, TN_), lambda k, j, wr: (k, 0, j)))],
            _store, prefetch=[where])
        (p_gu,) = _gemm(
            f"ffn_dwup{tag}", (2, NCHIP, D // TN_),
            [(dab_o.reshape(2, NCHIP, 2, S, F),
              pl.BlockSpec((None, None, None, S, F), lambda w, k, j, wr: (w, k, wr[0], 0, 0))),
             (xf_o, pl.BlockSpec((S, TN_), lambda w, k, j, wr: (0, j))),
             (dab_s, pl.BlockSpec((None, None, S, F), lambda w, k, j, wr: (w, k, 0, 0))),
             (xf_s, pl.BlockSpec((S, TN_), lambda w, k, j, wr: (0, j)))],
            [(0, 1, TN), (2, 3, TN)], [],
            [(_sds((NCHIP, 2 * F, D), BF16), pl.BlockSpec((None, F, TN_), lambda w, k, j, wr: (k, w, j)))],
            _store, prefetch=[where])
        send2, recv2, sums, own, tok = _chips_start(
            f"rs_chips_start_ffn{tag}", [p_gu, p_down], [lax.empty(p_gu.shape, BF16), lax.empty(p_down.shape, BF16)], [])
        newest[:] = [tok]
        exchanges[layer].append(dict(tag=f"_ffn{tag}", layer=layer, keys=["gu", "down"], sums=sums, own=own, send2=send2,
                                     recv2=recv2, direct=True))
        d, d_bf, g_ffn[layer], colsum = _rms_bwd(f"ffn_norm_bwd{tag}", sv["h_mid"], ffn_norm[layer], dxf, d,
                                                 deps=list(newest))
        if layer < LA:
            i_a = layer
            dgated = back_rows_gemm(f"gmlp_dgated{tag}", d_bf, wl["wout"], 0, F32)
            gl["wout"] = grad_rows_gemm(f"gmlp_dwout{tag}", sv["gated"], d_bf, gl["wout"], 0)
            dzp, g_ws[i_a], dbs, g_nv[i_a] = _gmlp_bwd(f"gmlp_bwd{tag}", sv["zp"], dgated,
                                                       norm_v[i_a].reshape(1, D), a_w_s[i_a], sv["bst"])
            g_bs[i_a] = dbs[:, 0, :]
            (gl["win"],) = _gemm(
                f"gmlp_dwin{tag}", (NDEV, D // TS),
                [(sv["xn"], pl.BlockSpec((S, TS), lambda e, i: (0, i))),
                 (dzp, pl.BlockSpec((S, ZC), lambda e, i: (0, e)))],
                [(0, 1, TN)], [(gl["win"], ANY)],
                [(_sds(gl["win"].shape, BF16), pl.BlockSpec((None, TS, ZC), lambda e, i: (e, i, 0)))],
                _store, aliases={2: 0})
            (dxn,) = _gemm(
                f"gmlp_dx{tag}", (S // TM, D // TN_, NDEV // KC),
                [(dzp, pl.BlockSpec((TM, KC * ZC), lambda i, j, k: (i, k))),
                 (wl["win"], pl.BlockSpec((KC, TN_, ZC), lambda i, j, k: (k, j, 0)))],
                [(0, 1, NT, _cols(c, ZC), _pick(c)) for c in range(KC)], [],
                [(_sds((S, D), F32), pl.BlockSpec((TM, TN_), lambda i, j, k: (i, j)))],
                _store, nk=NDEV // KC, acc_shape=(TM, TN_))
        else:
            i_b = layer - LA
            g_bo[i_b] = colsum
            dattn = back_rows_gemm(f"attn_dout{tag}", d_bf, wl["wqo"], 1, BF16)
            gl["wqo"] = grad_rows_gemm(f"attn_dwo{tag}", sv["attn"], d_bf, gl["wqo"], 1)
            dq, g_bq[i_b], dkc, dkp, dvc, dvp, dbias, dsink = _attn_bwd(
                f"attn_bwd{tag}", sv["q"], k_heads, v_heads, dattn, sv["probs"], sv["sink_probs"])
            kv_parts.append((dkc, dkp, dvc, dvp))
            g_sink[i_b] = dsink.reshape(NH)
            dbiases.append(dbias.reshape(NH, BLOCK * 2 * BLOCK))
            gl["wqo"] = grad_rows_gemm(f"attn_dwq{tag}", sv["xn"], dq, gl["wqo"], 0)
            dxn = back_rows_gemm(f"attn_dx{tag}", dq, wl["wqo"], 0, F32)
        d, d_bf, g_mix[layer], _ = _rms_bwd(f"mix_norm_bwd{tag}", sv["h_in"], mix_norm[layer], dxn, d)
        pending = exchange_begin(f"_mix{tag}", layer, gl, [k for k in gl if k not in ("gu", "down")])
        if layer == LA:
            wkv = weights[LA - 1]["wkv"]
            dkv, g_bkv = _kv_grad("kv_grad", kv_parts)
            (grads_wkv,) = _gemm(
                "kv_dw", (NDEV,),
                [(hn, pl.BlockSpec((S, DS), lambda e: (0, e))), (dkv, pl.BlockSpec((S, KVW), lambda e: (0, 0)))],
                [(0, 1, TN)], [(lax.empty((NDEV, DS, KVW), BF16), ANY)],
                [(_sds((NDEV, DS, KVW), BF16), pl.BlockSpec((None, DS, KVW), lambda e: (e, 0, 0)))],
                _store, aliases={2: 0}, deps=list(newest))
            (dhn,) = _gemm(
                "kv_dx", (S // TM, NDEV),
                [(dkv, pl.BlockSpec((TM, KVW), lambda i, e: (i, 0))),
                 (wkv, pl.BlockSpec((None, DS, KVW), lambda i, e: (e, 0, 0)))],
                [(0, 1, NT)], [], [(_sds((S, D), F32), pl.BlockSpec((TM, DS), lambda i, e: (i, e)))], _store)
            d, d_bf, g_kvn, _ = _rms_bwd("kv_norm_bwd", h_kv, kv_norm, dhn, d)
    grad_x = d.reshape(x.shape)

    exchange_middle(pending, d)

    g_rel = _bias_grad("bias_grad", dbiases, buckets)
    small_local = _pack([jnp.concatenate(g_mix, axis=0), jnp.concatenate(g_ffn, axis=0), jnp.stack(g_ws),
                         jnp.stack(g_bs), g_kvn, g_bkv, jnp.concatenate(g_bq, axis=0), jnp.stack(g_sink),
                         jnp.concatenate(g_bo, axis=0), g_rel, g_final, jnp.concatenate(g_nv, axis=0)])
    small_slot = _cast_into("put_small_grads", small_local.reshape((1,) + small_local.shape), 0,
                            lax.empty((NDEV,) + small_local.shape, F32), 0, me1)
    s_send, s_recv, s_bufs, s_tok = _gather_start("gather_small_start", [small_slot], list(newest))

    results = {}
    after = [s_tok]

    def upd(pname, w, m, v, l, li, lands, row0, own=None):
        w3 = w if w.ndim == 3 else w.reshape((1,) + w.shape)
        prev = results.get(pname) or [lax.empty(w3.shape, F32) for _ in range(4)]
        results[pname] = _adamw_shard(f"adamw_{pname}{l}", w3, m.reshape(w3.shape), v.reshape(w3.shape), lands,
                                      row0, li, prev, deps=list(after), own=own, where=where)
        after[:] = [results[pname][0]]

    for l in reversed(range(L)):
        for st in exchanges[l]:
            lands, own = exchange_end(st, after[0])
            if "gu" in lands:
                upd("ffn_w_gate", gate_t, tr3(m_ffn_w_gate), tr3(v_ffn_w_gate), l, l, lands["gu"], 0, own["gu"])
                upd("ffn_w_up", up_t, tr3(m_ffn_w_up), tr3(v_ffn_w_up), l, l, lands["gu"], F, own["gu"])
                upd("ffn_w_down", ffn_w_down, m_ffn_w_down, v_ffn_w_down, l, l, lands["down"], 0, own["down"])
            if "win" in lands:
                upd("a_w_in", a_w_in, m_a_w_in, v_a_w_in, l, l, lands["win"], 0)
                upd("a_w_out", a_w_out, m_a_w_out, v_a_w_out, l, l, lands["wout"], 0)
            if "wkv" in lands:
                upd("w_kv", w_kv, m_w_kv, v_w_kv, l, 0, lands["wkv"], 0)
            if "wqo" in lands:
                upd("b_w_q", b_w_q, m_b_w_q, v_b_w_q, l, l - LA, lands["wqo"], 0)
                upd("b_w_o", b_w_o, m_b_w_o, v_b_w_o, l, l - LA, lands["wqo"], DS)
    for pname in ("ffn_w_gate", "ffn_w_up"):
        results[pname] = [tr3(r) for r in results[pname]]
    results["w_kv"] = [r.reshape(w_kv.shape) for r in results["w_kv"]]

    small_w = [mix_norm, ffn_norm, a_w_s, a_b_s, kv_norm, b_kv, b_b_q, b_sinks, b_b_o, rel_bias, final_norm]
    small_m = [m_mix_norm, m_ffn_norm, m_a_w_s, m_a_b_s, m_kv_norm, m_b_kv, m_b_b_q, m_b_sinks, m_b_b_o, m_rel_bias,
               m_final_norm]
    small_v = [v_mix_norm, v_ffn_norm, v_a_w_s, v_a_b_s, v_kv_norm, v_b_kv, v_b_b_q, v_b_sinks, v_b_b_o, v_rel_bias,
               v_final_norm]
    shapes = [w.shape for w in small_w] + [(LA, D)]
    s_fsend, s_frecv, s_bufs = _gather_forward("gather_small_forward", s_bufs, s_send, s_recv, list(after))
    (small_all,) = _gather_finish("gather_small_finish", s_bufs, s_send, s_recv, s_fsend, s_frecv)
    small_sum = _sum_devices("sum_small_grads", small_all)
    small_g = _unpack(small_sum, shapes)
    g_normv = lax.dynamic_slice_in_dim(small_g[-1], me * DS, DS, axis=1)
    small_g = small_g[:-1] + [g_normv]
    small_w, small_m, small_v = small_w + [a_norm_v], small_m + [m_a_norm_v], small_v + [v_a_norm_v]
    shapes = [w.shape for w in small_w]
    s_delta, s_m, s_v = _adamw_flat("adamw_small", _pack(small_w), _pack(small_g), _pack(small_m), _pack(small_v))
    s_delta, s_m, s_v = _unpack(s_delta, shapes), _unpack(s_m, shapes), _unpack(s_v, shapes)

    names = ["mix_norm", "ffn_norm", "a_w_in", "a_norm_v", "a_w_s", "a_b_s", "a_w_out", "kv_norm", "w_kv", "b_kv",
             "b_w_q", "b_b_q", "b_sinks", "b_w_o", "b_b_o", "rel_bias", "ffn_w_gate", "ffn_w_up", "ffn_w_down",
             "final_norm"]
    small_names = ["mix_norm", "ffn_norm", "a_w_s", "a_b_s", "kv_norm", "b_kv", "b_b_q", "b_sinks", "b_b_o", "rel_bias",
                   "final_norm", "a_norm_v"]
    res = {}
    for idx, nm in enumerate(small_names):
        res[nm] = (small_g[idx].reshape(shapes[idx]), s_delta[idx], s_m[idx], s_v[idx])
    for nm, u in results.items():
        res[nm] = tuple(u)
    out = [loss, grad_x]
    for part in range(4):
        out += [res[nm][part] for nm in names]
    return tuple(out)
```

```python
import math

import numpy as np
import jax
import jax.numpy as jnp
from jax import lax
from jax.experimental import pallas as pl
from jax.experimental.pallas import tpu as pltpu

F32 = jnp.float32
BF16 = jnp.bfloat16
AXES = ("x", "y", "c")
NDEV = 8
NCHIP = 4
CHUNK = 128
GROUPS = 8
HEAD_DIM = 64
KV_GROUP = 8
BLOCK = 128
N_BUCKETS = 32
MAX_DISTANCE = 128
RMS_EPS = 1e-5
NEG_INF = -1e30
ADAM_LR, ADAM_B1, ADAM_B2, ADAM_EPS, ADAM_WD, ADAM_STEP = 0.001, 0.9, 0.999, 1e-08, 0.01, 10
VMEM_LIMIT_BYTES = 56 * 1024 * 1024

NN = (((1,), (0,)), ((), ()))
NT = (((1,), (1,)), ((), ()))
TN = (((0,), (0,)), ((), ()))
ANY = pl.BlockSpec(memory_space=pl.ANY)
HBM = pl.BlockSpec(memory_space=pltpu.HBM)
SEM = pl.BlockSpec(memory_space=pltpu.SEMAPHORE)
MESH = pl.DeviceIdType.MESH
EFFECT = pltpu.SideEffectType.DATAFLOW_SIDE_EFFECTING


def _pcall(body, *, name, out_shape, in_specs, out_specs, grid=(), scratch=(), aliases=None, prefetch=0, deps=()):
    n_in, n_dep = len(in_specs), len(deps)
    if n_dep:
        inner = body

        def body(*refs):
            return inner(*refs[:prefetch + n_in], *refs[prefetch + n_in + n_dep:])

        in_specs = list(in_specs) + [ANY] * n_dep
    params = dict(vmem_limit_bytes=VMEM_LIMIT_BYTES)
    if grid:
        params["dimension_semantics"] = ("arbitrary",) * len(grid)
    kw = dict(name=name, out_shape=out_shape, compiler_params=pltpu.CompilerParams(**params),
              input_output_aliases=aliases or {})
    if prefetch:
        kw["grid_spec"] = pltpu.PrefetchScalarGridSpec(num_scalar_prefetch=prefetch, grid=grid, in_specs=in_specs,
                                                       out_specs=out_specs, scratch_shapes=list(scratch))
    else:
        kw.update(grid=grid, in_specs=in_specs, out_specs=out_specs, scratch_shapes=list(scratch))
    call = pl.pallas_call(body, **kw)
    return lambda *args: call(*args, *deps)


def _sds(shape, dtype):
    return jax.ShapeDtypeStruct(tuple(shape), dtype)


def _position():
    x, y, c = lax.axis_index("x"), lax.axis_index("y"), lax.axis_index("c")
    chips = [(1 - x, y), (x, 1 - y), (1 - x, 1 - y)]
    return x, y, c, chips


def _slot(px, py, pc):
    return 4 * px + 2 * py + pc


def _remote(ref_src, ref_dst, send, recv, to):
    return pltpu.make_async_remote_copy(src_ref=ref_src, dst_ref=ref_dst, send_sem=send, recv_sem=recv,
                                        device_id=to, device_id_type=MESH)


def _hbm(arrays):
    return [pltpu.with_memory_space_constraint(a, pltpu.HBM) for a in arrays]


def _split_call(body, name, out_shape, in_specs, out_specs, aliases):
    return pl.pallas_call(body, name=name, out_shape=out_shape, in_specs=in_specs, out_specs=out_specs,
                          input_output_aliases=aliases, compiler_params=pltpu.CompilerParams(has_side_effects=EFFECT))


def _token_shape():
    return _sds((8, 128), F32)


def _gather_start(name, bufs, deps):
    n, nd = len(bufs), len(deps)

    def body(*refs):
        ins, send, recv, token = refs[:n], refs[n + nd], refs[n + nd + 1], refs[2 * n + nd + 2]
        x, y, c, chips = _position()
        peers = [(x, y, 1 - c)] + [(*chip, c) for chip in chips]
        for t in range(n):
            mine = ins[t].at[_slot(x, y, c)]
            for k, peer in enumerate(peers):
                _remote(mine, mine, send.at[4 * t + k], recv.at[4 * t + k], peer).start()
        token[...] = jnp.zeros_like(token)

    res = _split_call(
        body, name,
        (pltpu.SemaphoreType.DMA((4 * n,)), pltpu.SemaphoreType.DMA((4 * n,)), *[pltpu.HBM(b.shape, b.dtype) for b in bufs],
         _token_shape()),
        [HBM] * n + [ANY] * nd, (SEM, SEM, *[HBM] * n, pl.BlockSpec(memory_space=pltpu.VMEM)),
        {t: 2 + t for t in range(n)})(*_hbm(bufs), *deps)
    return res[0], res[1], list(res[2:2 + n]), res[2 + n]


def _gather_forward(name, bufs, send, recv, deps):
    n, nd = len(bufs), len(deps)

    def body(*refs):
        ins, send_in, recv_in = refs[:n], refs[n], refs[n + 1]
        fsend, frecv = refs[n + 2 + nd], refs[n + 3 + nd]
        x, y, c, chips = _position()
        for j, chip in enumerate(chips):
            for t in range(n):
                blk = ins[t].at[_slot(*chip, c)]
                _remote(blk, blk, send_in.at[4 * t + 1 + j], recv_in.at[4 * t + 1 + j], (*chip, c)).wait_recv()
                _remote(blk, blk, fsend.at[3 * t + j], frecv.at[3 * t + j], (x, y, 1 - c)).start()

    res = _split_call(
        body, name,
        (pltpu.SemaphoreType.DMA((3 * n,)), pltpu.SemaphoreType.DMA((3 * n,)), *[pltpu.HBM(b.shape, b.dtype) for b in bufs]),
        [HBM] * n + [SEM, SEM] + [ANY] * nd, (SEM, SEM, *[HBM] * n),
        {t: 2 + t for t in range(n)})(*_hbm(bufs), send, recv, *deps)
    return res[0], res[1], list(res[2:])


def _gather_finish(name, bufs, send, recv, fsend, frecv):
    n = len(bufs)

    def body(*refs):
        ins, send_in, recv_in, fs_in, fr_in = refs[:n], refs[n], refs[n + 1], refs[n + 2], refs[n + 3]
        x, y, c, chips = _position()
        sibling = (x, y, 1 - c)
        peers = [sibling] + [(*chip, c) for chip in chips]
        for t in range(n):
            blk = ins[t].at[_slot(x, y, 1 - c)]
            _remote(blk, blk, send_in.at[4 * t], recv_in.at[4 * t], sibling).wait_recv()
            for j, chip in enumerate(chips):
                blk = ins[t].at[_slot(*chip, 1 - c)]
                _remote(blk, blk, fs_in.at[3 * t + j], fr_in.at[3 * t + j], sibling).wait_recv()
            mine = ins[t].at[_slot(x, y, c)]
            for k, peer in enumerate(peers):
                _remote(mine, mine, send_in.at[4 * t + k], recv_in.at[4 * t + k], peer).wait_send()
            for j, chip in enumerate(chips):
                blk = ins[t].at[_slot(*chip, c)]
                _remote(blk, blk, fs_in.at[3 * t + j], fr_in.at[3 * t + j], sibling).wait_send()

    res = _split_call(
        body, name, tuple(pltpu.HBM(b.shape, b.dtype) for b in bufs),
        [HBM] * n + [SEM] * 4, tuple([HBM] * n), {t: t for t in range(n)})(*_hbm(bufs), send, recv, fsend, frecv)
    return list(res)


def _halves(ref):
    rows = ref.shape[0] // 2
    return ref.at[pl.ds(0, rows)], ref.at[pl.ds(rows, rows)]


def _relay_start(name, bufs, deps):
    n, nd = len(bufs), len(deps)

    def body(*refs):
        ins, send, recv, token = refs[:n], refs[n + nd], refs[n + nd + 1], refs[2 * n + nd + 2]
        x, y, c, _ = _position()
        peers = [(x, y, 1 - c), (1 - x, y, c), (x, 1 - y, c)]
        for t in range(n):
            mine = ins[t].at[_slot(x, y, c)]
            for k, peer in enumerate(peers):
                _remote(mine, mine, send.at[3 * t + k], recv.at[3 * t + k], peer).start()
        token[...] = jnp.zeros_like(token)

    res = _split_call(
        body, name,
        (pltpu.SemaphoreType.DMA((3 * n,)), pltpu.SemaphoreType.DMA((3 * n,)), *[pltpu.HBM(b.shape, b.dtype) for b in bufs],
         _token_shape()),
        [HBM] * n + [ANY] * nd, (SEM, SEM, *[HBM] * n, pl.BlockSpec(memory_space=pltpu.VMEM)),
        {t: 2 + t for t in range(n)})(*_hbm(bufs), *deps)
    return res[0], res[1], list(res[2:2 + n]), res[2 + n]


def _relay_neighbors(name, bufs, send, recv, deps):
    n, nd = len(bufs), len(deps)

    def body(*refs):
        ins, send_in, recv_in = refs[:n], refs[n], refs[n + 1]
        fsend, frecv, token = refs[n + 2 + nd], refs[n + 3 + nd], refs[2 * n + 4 + nd]
        x, y, c, _ = _position()
        sibling, xn, yn = (x, y, 1 - c), (1 - x, y, c), (x, 1 - y, c)
        for t in range(n):
            blk = ins[t].at[_slot(*xn)]
            _remote(blk, blk, send_in.at[3 * t + 1], recv_in.at[3 * t + 1], xn).wait_recv()
            _remote(blk, blk, fsend.at[4 * t], frecv.at[4 * t], sibling).start()
            half = _halves(blk)[0]
            _remote(half, half, fsend.at[4 * t + 1], frecv.at[4 * t + 1], yn).start()
        for t in range(n):
            blk = ins[t].at[_slot(*yn)]
            _remote(blk, blk, send_in.at[3 * t + 2], recv_in.at[3 * t + 2], yn).wait_recv()
            _remote(blk, blk, fsend.at[4 * t + 2], frecv.at[4 * t + 2], sibling).start()
            half = _halves(blk)[1]
            _remote(half, half, fsend.at[4 * t + 3], frecv.at[4 * t + 3], xn).start()
        token[...] = jnp.zeros_like(token)

    res = _split_call(
        body, name,
        (pltpu.SemaphoreType.DMA((4 * n,)), pltpu.SemaphoreType.DMA((4 * n,)), *[pltpu.HBM(b.shape, b.dtype) for b in bufs],
         _token_shape()),
        [HBM] * n + [SEM, SEM] + [ANY] * nd, (SEM, SEM, *[HBM] * n, pl.BlockSpec(memory_space=pltpu.VMEM)),
        {t: 2 + t for t in range(n)})(*_hbm(bufs), send, recv, *deps)
    return res[0], res[1], list(res[2:2 + n]), res[2 + n]


def _relay_diagonal(name, bufs, fsend, frecv, deps):
    n, nd = len(bufs), len(deps)

    def body(*refs):
        ins, fs_in, fr_in = refs[:n], refs[n], refs[n + 1]
        gsend, grecv = refs[n + 2 + nd], refs[n + 3 + nd]
        x, y, c, _ = _position()
        for t in range(n):
            blk = ins[t].at[_slot(1 - x, 1 - y, c)]
            first, second = _halves(blk)
            _remote(first, first, fs_in.at[4 * t + 1], fr_in.at[4 * t + 1], (x, 1 - y, c)).wait_recv()
            _remote(second, second, fs_in.at[4 * t + 3], fr_in.at[4 * t + 3], (1 - x, y, c)).wait_recv()
            _remote(blk, blk, gsend.at[t], grecv.at[t], (x, y, 1 - c)).start()

    res = _split_call(
        body, name,
        (pltpu.SemaphoreType.DMA((n,)), pltpu.SemaphoreType.DMA((n,)), *[pltpu.HBM(b.shape, b.dtype) for b in bufs]),
        [HBM] * n + [SEM, SEM] + [ANY] * nd, (SEM, SEM, *[HBM] * n),
        {t: 2 + t for t in range(n)})(*_hbm(bufs), fsend, frecv, *deps)
    return res[0], res[1], list(res[2:])


def _relay_finish(name, bufs, send, recv, fsend, frecv, gsend, grecv):
    n = len(bufs)

    def body(*refs):
        ins = refs[:n]
        send_in, recv_in, fs_in, fr_in, gs_in, gr_in = refs[n:n + 6]
        x, y, c, _ = _position()
        sibling, xn, yn = (x, y, 1 - c), (1 - x, y, c), (x, 1 - y, c)
        for t in range(n):
            blk = ins[t].at[_slot(x, y, 1 - c)]
            _remote(blk, blk, send_in.at[3 * t], recv_in.at[3 * t], sibling).wait_recv()
            blk = ins[t].at[_slot(1 - x, y, 1 - c)]
            _remote(blk, blk, fs_in.at[4 * t], fr_in.at[4 * t], sibling).wait_recv()
            blk = ins[t].at[_slot(x, 1 - y, 1 - c)]
            _remote(blk, blk, fs_in.at[4 * t + 2], fr_in.at[4 * t + 2], sibling).wait_recv()
            blk = ins[t].at[_slot(1 - x, 1 - y, 1 - c)]
            _remote(blk, blk, gs_in.at[t], gr_in.at[t], sibling).wait_recv()
            mine = ins[t].at[_slot(x, y, c)]
            for k, peer in enumerate([sibling, xn, yn]):
                _remote(mine, mine, send_in.at[3 * t + k], recv_in.at[3 * t + k], peer).wait_send()
            bx, by = ins[t].at[_slot(*xn)], ins[t].at[_slot(*yn)]
            _remote(bx, bx, fs_in.at[4 * t], fr_in.at[4 * t], sibling).wait_send()
            _remote(_halves(bx)[0], _halves(bx)[0], fs_in.at[4 * t + 1], fr_in.at[4 * t + 1], yn).wait_send()
            _remote(by, by, fs_in.at[4 * t + 2], fr_in.at[4 * t + 2], sibling).wait_send()
            _remote(_halves(by)[1], _halves(by)[1], fs_in.at[4 * t + 3], fr_in.at[4 * t + 3], xn).wait_send()
            bd = ins[t].at[_slot(1 - x, 1 - y, c)]
            _remote(bd, bd, gs_in.at[t], gr_in.at[t], sibling).wait_send()

    res = _split_call(
        body, name, tuple(pltpu.HBM(b.shape, b.dtype) for b in bufs),
        [HBM] * n + [SEM] * 6, tuple([HBM] * n), {t: t for t in range(n)})(
            *_hbm(bufs), send, recv, fsend, frecv, gsend, grecv)
    return list(res)


WHOLE, SHARDS, SHARDS2 = 0, 1, 2


def _sibling_copies(srcs, lands, kinds, c):
    pairs = []
    for s_ref, l_ref, kind in zip(srcs, lands, kinds):
        if kind == WHOLE:
            pairs.append((s_ref, l_ref))
        elif kind == SHARDS:
            pairs += [(s_ref.at[2 * k + (1 - c)], l_ref.at[k]) for k in range(NCHIP)]
        else:
            pairs += [(s_ref.at[w, 2 * k + (1 - c)], l_ref.at[w, k]) for w in range(2) for k in range(NCHIP)]
    return pairs


def _count_copies(kinds):
    return sum({WHOLE: 1, SHARDS: NCHIP, SHARDS2: 2 * NCHIP}[k] for k in kinds)


def _sibling_start(name, srcs, lands, deps, kinds=None):
    n, nd = len(srcs), len(deps)
    kinds = kinds or [SHARDS] * n
    ncp = _count_copies(kinds)

    def body(*refs):
        s_in, l_in = refs[:n], refs[n:2 * n]
        send, recv, token = refs[2 * n + nd], refs[2 * n + nd + 1], refs[4 * n + nd + 2]
        x, y, c, _ = _position()
        for i, (src, dst) in enumerate(_sibling_copies(s_in, l_in, kinds, c)):
            _remote(src, dst, send.at[i], recv.at[i], (x, y, 1 - c)).start()
        token[...] = jnp.zeros_like(token)

    both = list(srcs) + list(lands)
    res = _split_call(
        body, name,
        (pltpu.SemaphoreType.DMA((ncp,)), pltpu.SemaphoreType.DMA((ncp,)),
         *[pltpu.HBM(b.shape, b.dtype) for b in both], _token_shape()),
        [HBM] * (2 * n) + [ANY] * nd, (SEM, SEM, *[HBM] * (2 * n), pl.BlockSpec(memory_space=pltpu.VMEM)),
        {t: 2 + t for t in range(2 * n)})(*_hbm(both), *deps)
    return res[0], res[1], list(res[2:2 + n]), list(res[2 + n:2 + 2 * n]), res[2 + 2 * n]


def _sibling_finish(name, srcs, lands, send, recv, deps, kinds=None):
    n, nd = len(srcs), len(deps)
    kinds = kinds or [SHARDS] * n

    def body(*refs):
        s_in, l_in, send_in, recv_in = refs[:n], refs[n:2 * n], refs[2 * n], refs[2 * n + 1]
        x, y, c, _ = _position()
        for i, (src, dst) in enumerate(_sibling_copies(s_in, l_in, kinds, c)):
            cp = _remote(src, dst, send_in.at[i], recv_in.at[i], (x, y, 1 - c))
            cp.wait_send()
            cp.wait_recv()

    both = list(srcs) + list(lands)
    res = _split_call(
        body, name, tuple(pltpu.HBM(b.shape, b.dtype) for b in both),
        [HBM] * (2 * n) + [SEM, SEM] + [ANY] * nd, tuple([HBM] * (2 * n)),
        {t: t for t in range(2 * n)})(*_hbm(both), send, recv, *deps)
    return list(res[:n]), list(res[n:])


def _chips_start(name, parts, lands, deps):
    n, nd = len(parts), len(deps)

    def body(*refs):
        p_in, l_in = refs[:n], refs[n:2 * n]
        send, recv, token = refs[2 * n + nd], refs[2 * n + nd + 1], refs[4 * n + nd + 2]
        x, y, c, chips = _position()
        for t in range(n):
            for j, chip in enumerate(chips):
                _remote(p_in[t].at[2 * chip[0] + chip[1]], l_in[t].at[2 * x + y], send.at[3 * t + j], recv.at[3 * t + j],
                        (*chip, c)).start()
        token[...] = jnp.zeros_like(token)

    both = list(parts) + list(lands)
    res = _split_call(
        body, name,
        (pltpu.SemaphoreType.DMA((3 * n,)), pltpu.SemaphoreType.DMA((3 * n,)), *[pltpu.HBM(b.shape, b.dtype) for b in both],
         _token_shape()),
        [HBM] * (2 * n) + [ANY] * nd, (SEM, SEM, *[HBM] * (2 * n), pl.BlockSpec(memory_space=pltpu.VMEM)),
        {t: 2 + t for t in range(2 * n)})(*_hbm(both), *deps)
    return res[0], res[1], list(res[2:2 + n]), list(res[2 + n:2 + 2 * n]), res[2 + 2 * n]


def _chips_finish(name, parts, lands, send, recv, deps):
    n, nd = len(parts), len(deps)

    def body(*refs):
        p_in, l_in, send_in, recv_in = refs[:n], refs[n:2 * n], refs[2 * n], refs[2 * n + 1]
        x, y, c, chips = _position()
        for t in range(n):
            for j, chip in enumerate(chips):
                k = 2 * chip[0] + chip[1]
                _remote(p_in[t].at[k], l_in[t].at[k], send_in.at[3 * t + j], recv_in.at[3 * t + j], (*chip, c)).wait_recv()
                _remote(p_in[t].at[k], l_in[t].at[2 * x + y], send_in.at[3 * t + j], recv_in.at[3 * t + j],
                        (*chip, c)).wait_send()

    both = list(parts) + list(lands)
    res = _split_call(
        body, name, tuple(pltpu.HBM(b.shape, b.dtype) for b in both),
        [HBM] * (2 * n) + [SEM, SEM] + [ANY] * nd, tuple([HBM] * (2 * n)),
        {t: t for t in range(2 * n)})(*_hbm(both), send, recv, *deps)
    return list(res[:n]), list(res[n:])


def _pair_sum(name, grad, recv, where):
    _, r, w = grad.shape
    tr = _row_tile(r, w, budget=4 * 1024 * 1024)
    g4 = grad.reshape(NCHIP, 2, r, w)

    def body(where_ref, g_ref, r_ref, o_ref, own_ref):
        val = (g_ref[...].astype(F32) + r_ref[...].astype(F32)).astype(o_ref.dtype)
        o_ref[...] = val

        @pl.when(pl.program_id(1) == where_ref[1])
        def _():
            own_ref[...] = val

    out = _sds((NCHIP, r, w), grad.dtype)
    return _pcall(
        body, name=name, out_shape=[out, out], grid=(r // tr, NCHIP), prefetch=1,
        in_specs=[pl.BlockSpec((None, None, tr, w), lambda i, k, wr: (k, wr[0], i, 0)),
                  pl.BlockSpec((None, tr, w), lambda i, k, wr: (k, i, 0))],
        out_specs=[pl.BlockSpec((None, tr, w), lambda i, k, wr: (k, i, 0)),
                   pl.BlockSpec((None, tr, w), lambda i, k, wr: (wr[1], i, 0))],
    )(where, g4, recv)


def _row_tile(rows, width, budget=2 * 1024 * 1024):
    best = None
    for t in range(16, rows + 1, 16):
        if rows % t == 0 and t * width * 4 <= budget:
            best = t
    if best is None and rows * width * 4 <= budget:
        best = rows
    assert best is not None, (rows, width)
    return best


def _gemm(name, grid, operands, prods, extras, outs, epilogue, *, nk=1, acc_shape=None, aliases=None, separate=False,
          deps=(), prefetch=()):
    n_op, n_ex, n_out = len(operands), len(extras), len(outs)

    def body(*refs):
        refs = refs[len(prefetch):]
        ops, ex, out_refs = refs[:n_op], refs[n_op:n_op + n_ex], refs[n_op + n_ex:n_op + n_ex + n_out]
        parts = []
        for pr in prods:
            a, b = ops[pr[0]], ops[pr[1]]
            av = pr[3](a) if len(pr) > 3 and pr[3] else a[...]
            bv = pr[4](b) if len(pr) > 4 and pr[4] else b[...]
            parts.append(lax.dot_general(av, bv, pr[2], preferred_element_type=F32))
        if separate:
            epilogue(parts, ex, out_refs)
            return
        part = parts[0]
        for p in parts[1:]:
            part = part + p
        if nk == 1:
            epilogue(part, ex, out_refs)
        else:
            acc = refs[-1]
            k = pl.program_id(len(grid) - 1)

            @pl.when(k == 0)
            def _():
                acc[...] = part

            @pl.when(k > 0)
            def _():
                acc[...] += part

            @pl.when(k == nk - 1)
            def _():
                epilogue(acc[...], ex, out_refs)

    res = _pcall(
        body, name=name, out_shape=[o[0] for o in outs], grid=grid,
        in_specs=[o[1] for o in operands] + [e[1] for e in extras], out_specs=[o[1] for o in outs],
        scratch=[pltpu.VMEM(acc_shape, F32)] if nk > 1 else [], aliases=aliases, deps=deps, prefetch=len(prefetch),
    )(*prefetch, *[o[0] for o in operands], *[e[0] for e in extras])
    return list(res)


def _store(acc, ex, outs):
    outs[0][...] = acc.astype(outs[0].dtype)


def _store_add_extra(acc, ex, outs):
    v = acc
    for e in ex:
        v = v + e[...]
    outs[0][...] = v.astype(outs[0].dtype)


def _stacked(ref):
    b = ref[...]
    return b.reshape(b.shape[0] * b.shape[1], b.shape[2])


def _pick(c):
    return lambda ref: ref[c]


def _cols(c, width):
    return lambda ref: ref[:, c * width:(c + 1) * width]


def _gelu_parts(z):
    c = math.sqrt(2.0 / math.pi)
    t = jnp.tanh(c * (z + 0.044715 * (z * z * z)))
    val = 0.5 * z * (1.0 + t)
    grad = 0.5 * (1.0 + t) + 0.5 * z * (1.0 - t * t) * (c * (1.0 + 3.0 * 0.044715 * z * z))
    return val, grad


def _rms_fwd(name, h, g, deps=()):
    s, d = h.shape
    tr = _row_tile(s, d)

    def body(h_ref, g_ref, o_ref):
        hv = h_ref[...]
        r = lax.rsqrt(jnp.mean(hv * hv, axis=-1, keepdims=True) + RMS_EPS)
        o_ref[...] = (hv * r * g_ref[...]).astype(o_ref.dtype)

    return _pcall(
        body, name=name, out_shape=_sds((s, d), BF16), grid=(s // tr,),
        in_specs=[pl.BlockSpec((tr, d), lambda i: (i, 0)), pl.BlockSpec((1, d), lambda i: (0, 0))],
        out_specs=pl.BlockSpec((tr, d), lambda i: (i, 0)), deps=deps,
    )(h, g.reshape(1, d))


def _accumulate(ref, val, first):
    @pl.when(first)
    def _():
        ref[...] = val

    @pl.when(jnp.logical_not(first))
    def _():
        ref[...] += val


def _rms_bwd(name, h, g, dy, res, deps=()):
    s, d = h.shape
    tr = _row_tile(s, d, budget=2 * 1024 * 1024)

    def body(h_ref, g_ref, dy_ref, res_ref, dh_ref, dhb_ref, dg_ref, cs_ref):
        hv = h_ref[...]
        r = lax.rsqrt(jnp.mean(hv * hv, axis=-1, keepdims=True) + RMS_EPS)
        xhat = hv * r
        dyv = dy_ref[...]
        dxh = dyv * g_ref[...]
        dh = res_ref[...] + r * (dxh - xhat * jnp.mean(dxh * xhat, axis=-1, keepdims=True))
        dh_ref[...] = dh
        dhb_ref[...] = dh.astype(BF16)
        first = pl.program_id(0) == 0
        _accumulate(dg_ref, jnp.sum(dyv * xhat, axis=0, keepdims=True), first)
        _accumulate(cs_ref, jnp.sum(dh, axis=0, keepdims=True), first)

    row = pl.BlockSpec((tr, d), lambda i: (i, 0))
    vec = pl.BlockSpec((1, d), lambda i: (0, 0))
    return _pcall(
        body, name=name, out_shape=[_sds((s, d), F32), _sds((s, d), BF16), _sds((1, d), F32), _sds((1, d), F32)],
        grid=(s // tr,), in_specs=[row, vec, row, row], out_specs=[row, row, vec, vec], deps=deps,
    )(h, g.reshape(1, d), dy, res)


def _loss_bwd(name, h, g, target):
    s, d = h.shape
    tr = _row_tile(s, d, budget=1024 * 1024)

    def body(h_ref, g_ref, t_ref, loss_ref, dh_ref, dhb_ref, dg_ref):
        hv = h_ref[...]
        r = lax.rsqrt(jnp.mean(hv * hv, axis=-1, keepdims=True) + RMS_EPS)
        xhat = hv * r
        diff = xhat * g_ref[...] - t_ref[...]
        part = jnp.sum(jnp.sum(diff * diff, axis=1, keepdims=True), axis=0, keepdims=True) * (0.5 / d)
        dyv = diff * (1.0 / d)
        dxh = dyv * g_ref[...]
        dh = r * (dxh - xhat * jnp.mean(dxh * xhat, axis=-1, keepdims=True))
        dh_ref[...] = dh
        dhb_ref[...] = dh.astype(BF16)
        first = pl.program_id(0) == 0
        _accumulate(loss_ref, part, first)
        _accumulate(dg_ref, jnp.sum(dyv * xhat, axis=0, keepdims=True), first)

    row = pl.BlockSpec((tr, d), lambda i: (i, 0))
    vec = pl.BlockSpec((1, d), lambda i: (0, 0))
    one = pl.BlockSpec((1, 1), lambda i: (0, 0))
    return _pcall(
        body, name=name, out_shape=[_sds((1, 1), F32), _sds((s, d), F32), _sds((s, d), BF16), _sds((1, d), F32)],
        grid=(s // tr,), in_specs=[row, vec, row], out_specs=[one, row, row, vec],
    )(h, g.reshape(1, d), target)


def _tril_mask():
    return lax.broadcasted_iota(jnp.int32, (CHUNK, CHUNK), 0) >= lax.broadcasted_iota(jnp.int32, (CHUNK, CHUNK), 1)


def _gmlp_fwd(name, zp, gv, ws, bst):
    s, d2 = zp.shape
    d = d2 // 2
    gw = d // GROUPS

    def body(zp_ref, gv_ref, ws_ref, bst_ref, o_ref):
        u, _ = _gelu_parts(zp_ref[:, :d])
        v, _ = _gelu_parts(zp_ref[:, d:])
        rv = lax.rsqrt(jnp.mean(v * v, axis=-1, keepdims=True) + RMS_EPS)
        vn = (v * rv * gv_ref[...]).astype(BF16)
        tril = _tril_mask()
        for g in range(GROUPS):
            sl = slice(g * gw, (g + 1) * gw)
            wc = jnp.where(tril, ws_ref[g], 0.0).astype(BF16)
            sg = jnp.dot(wc, vn[:, sl], preferred_element_type=F32) + bst_ref[:, g:g + 1]
            o_ref[:, sl] = (u[:, sl] * sg).astype(o_ref.dtype)

    return _pcall(
        body, name=name, out_shape=_sds((s, d), BF16), grid=(s // CHUNK,),
        in_specs=[pl.BlockSpec((CHUNK, d2), lambda i: (i, 0)), pl.BlockSpec((1, d), lambda i: (0, 0)),
                  pl.BlockSpec((GROUPS, CHUNK, CHUNK), lambda i: (0, 0, 0)),
                  pl.BlockSpec((CHUNK, GROUPS), lambda i: (0, 0))],
        out_specs=pl.BlockSpec((CHUNK, d), lambda i: (i, 0)),
    )(zp, gv, ws, bst)


def _gmlp_bwd(name, zp, dgated, gv, ws, bst):
    s, d2 = zp.shape
    d = d2 // 2
    gw = d // GROUPS

    def body(zp_ref, dg_ref, gv_ref, ws_ref, bst_ref, dzp_ref, dws_ref, dbs_ref, dgv_ref, dvn_ref):
        u, gu = _gelu_parts(zp_ref[:, :d])
        v, gvv = _gelu_parts(zp_ref[:, d:])
        rv = lax.rsqrt(jnp.mean(v * v, axis=-1, keepdims=True) + RMS_EPS)
        vhat = v * rv
        vn = (vhat * gv_ref[...]).astype(BF16)
        tril = _tril_mask()
        first = pl.program_id(0) == 0
        ones = jnp.ones((8, gw), F32)

        @pl.when(first)
        def _():
            dws_ref[...] = jnp.zeros_like(dws_ref)
            dbs_ref[...] = jnp.zeros_like(dbs_ref)

        for g in range(GROUPS):
            sl = slice(g * gw, (g + 1) * gw)
            wc = jnp.where(tril, ws_ref[g], 0.0).astype(BF16)
            sg = jnp.dot(wc, vn[:, sl], preferred_element_type=F32) + bst_ref[:, g:g + 1]
            dgs = dg_ref[:, sl]
            ds = dgs * u[:, sl]
            dsb = ds.astype(BF16)
            dzp_ref[:, sl] = (dgs * sg * gu[:, sl]).astype(dzp_ref.dtype)
            dvn_ref[:, sl] = lax.dot_general(wc, dsb, TN, preferred_element_type=F32)
            dw = lax.dot_general(dsb, vn[:, sl], NT, preferred_element_type=F32)
            dws_ref[g] += jnp.where(tril, dw, 0.0)
            dbs_ref[g] += lax.dot_general(ones, ds, NT, preferred_element_type=F32, precision=lax.Precision.HIGHEST)
        dvn = dvn_ref[...]
        dvh = dvn * gv_ref[...]
        dv = rv * (dvh - vhat * jnp.mean(dvh * vhat, axis=-1, keepdims=True))
        dzp_ref[:, d:] = (dv * gvv).astype(dzp_ref.dtype)
        _accumulate(dgv_ref, jnp.sum(dvn * vhat, axis=0, keepdims=True), first)

    return _pcall(
        body, name=name,
        out_shape=[_sds((s, d2), BF16), _sds((GROUPS, CHUNK, CHUNK), F32), _sds((GROUPS, 8, CHUNK), F32),
                   _sds((1, d), F32)],
        grid=(s // CHUNK,),
        in_specs=[pl.BlockSpec((CHUNK, d2), lambda i: (i, 0)), pl.BlockSpec((CHUNK, d), lambda i: (i, 0)),
                  pl.BlockSpec((1, d), lambda i: (0, 0)), pl.BlockSpec((GROUPS, CHUNK, CHUNK), lambda i: (0, 0, 0)),
                  pl.BlockSpec((CHUNK, GROUPS), lambda i: (0, 0))],
        out_specs=[pl.BlockSpec((CHUNK, d2), lambda i: (i, 0)),
                   pl.BlockSpec((GROUPS, CHUNK, CHUNK), lambda i: (0, 0, 0)),
                   pl.BlockSpec((GROUPS, 8, CHUNK), lambda i: (0, 0, 0)), pl.BlockSpec((1, d), lambda i: (0, 0))],
        scratch=[pltpu.VMEM((CHUNK, d), F32)],
    )(zp, dgated, gv, ws, bst)


def _bucket_table():
    dist = np.arange(BLOCK)[:, None] + BLOCK - np.arange(2 * BLOCK)[None, :]
    in_window = (dist >= 0) & (dist < BLOCK)
    dd = np.clip(dist, 0, None)
    max_exact = N_BUCKETS // 2
    dl = np.maximum(dd, 1).astype(np.float32)
    large = max_exact + (np.log(dl / np.float32(max_exact)) / np.float32(math.log(MAX_DISTANCE / max_exact))
                         * np.float32(N_BUCKETS - max_exact)).astype(np.int32)
    large = np.minimum(large, N_BUCKETS - 1)
    bucket = np.where(dd < max_exact, dd, large)
    return np.where(in_window, bucket, -1).astype(np.int32).reshape(1, -1)


def _bias_table(name, rel_bias_t, buckets):
    nh = rel_bias_t.shape[0]
    p = buckets.shape[1]
    tp = 4096

    def body(rb_ref, bk_ref, o_ref):
        bk = bk_ref[...]
        onehot = (lax.broadcasted_iota(jnp.int32, (N_BUCKETS, tp), 0) == bk).astype(F32)
        val = jnp.dot(rb_ref[...], onehot, preferred_element_type=F32, precision=lax.Precision.HIGHEST)
        o_ref[...] = jnp.where(bk >= 0, val, NEG_INF)

    return _pcall(
        body, name=name, out_shape=_sds((nh, p), F32), grid=(p // tp,),
        in_specs=[pl.BlockSpec((nh, N_BUCKETS), lambda i: (0, 0)), pl.BlockSpec((1, tp), lambda i: (0, i))],
        out_specs=pl.BlockSpec((nh, tp), lambda i: (0, i)),
    )(rel_bias_t, buckets)


def _bias_grad(name, dbiases, buckets):
    nh, p = dbiases[0].shape
    n = len(dbiases)
    tp = 4096

    def body(*refs):
        bk_ref, o_ref = refs[n], refs[n + 1]
        onehot = (lax.broadcasted_iota(jnp.int32, (N_BUCKETS, tp), 0) == bk_ref[...]).astype(F32)
        db = refs[0][...]
        for r in refs[1:n]:
            db = db + r[...]
        part = lax.dot_general(onehot, db, NT, preferred_element_type=F32, precision=lax.Precision.HIGHEST)
        _accumulate(o_ref, part, pl.program_id(0) == 0)

    return _pcall(
        body, name=name, out_shape=_sds((N_BUCKETS, nh), F32), grid=(p // tp,),
        in_specs=[pl.BlockSpec((nh, tp), lambda i: (0, i))] * n + [pl.BlockSpec((1, tp), lambda i: (0, i))],
        out_specs=pl.BlockSpec((N_BUCKETS, nh), lambda i: (0, 0)),
    )(*dbiases, buckets)


def _stack_heads(ref, g):
    base = g * KV_GROUP * HEAD_DIM
    return jnp.concatenate([ref[:, base + hh * HEAD_DIM:base + (hh + 1) * HEAD_DIM] for hh in range(KV_GROUP)], axis=0)


def _attn_probs(q, kb, bias, s_ref, first_head):
    penalty = jnp.where(pl.program_id(1) > 0, 0.0, NEG_INF).astype(F32)
    col = lax.broadcasted_iota(jnp.int32, (1, 2 * BLOCK), 1)
    bias = bias.reshape(KV_GROUP * BLOCK, 2 * BLOCK) + jnp.where(col < BLOCK, penalty, 0.0)
    sink = jnp.concatenate([jnp.full((BLOCK, 1), s_ref[first_head + hh], F32) for hh in range(KV_GROUP)], axis=0)
    s = lax.dot_general(q, kb, NT, preferred_element_type=F32) * 0.125 + bias
    m = jnp.maximum(jnp.max(s, axis=-1, keepdims=True), sink)
    p = jnp.exp(s - m)
    es = jnp.exp(sink - m)
    inv = 1.0 / (jnp.sum(p, axis=-1, keepdims=True) + es)
    return p * inv, es * inv


def _attn_specs(ng):
    gq = ng * KV_GROUP * HEAD_DIM
    q_spec = pl.BlockSpec((BLOCK, gq), lambda kh, i: (i, kh))
    prev = pl.BlockSpec((ng, BLOCK, HEAD_DIM), lambda kh, i: (kh, jnp.maximum(i - 1, 0), 0))
    cur = pl.BlockSpec((ng, BLOCK, HEAD_DIM), lambda kh, i: (kh, i, 0))
    bias = pl.BlockSpec((ng * KV_GROUP, BLOCK, 2 * BLOCK), lambda kh, i: (kh, 0, 0))
    smem = pl.BlockSpec(memory_space=pltpu.SMEM)
    probs = pl.BlockSpec((ng, None, KV_GROUP * BLOCK, 2 * BLOCK), lambda kh, i: (kh, i, 0, 0))
    sink_probs = pl.BlockSpec((ng, None, KV_GROUP * BLOCK, 1), lambda kh, i: (kh, i, 0, 0))
    return q_spec, prev, cur, bias, smem, probs, sink_probs


def _kv_heads_per_step(nkv):
    return 2 if nkv % 2 == 0 else 1


def _attn_fwd(name, q, k, v, bias, sinks, deps=()):
    s, dq = q.shape
    nkv = k.shape[0]
    ng = 1
    q_spec, prev, cur, bias_spec, smem, p_spec, ps_spec = _attn_specs(ng)

    def body(q_ref, kp_ref, kc_ref, vp_ref, vc_ref, b_ref, s_ref, o_ref, p_ref, ps_ref):
        for g in range(ng):
            kb = jnp.concatenate([kp_ref[g], kc_ref[g]], axis=0)
            vb = jnp.concatenate([vp_ref[g], vc_ref[g]], axis=0)
            p, ps = _attn_probs(_stack_heads(q_ref, g), kb, b_ref[g * KV_GROUP:(g + 1) * KV_GROUP], s_ref,
                                (pl.program_id(0) * ng + g) * KV_GROUP)
            pb = p.astype(BF16)
            p_ref[g] = pb
            ps_ref[g] = ps
            o = jnp.dot(pb, vb, preferred_element_type=F32)
            for hh in range(KV_GROUP):
                col = (g * KV_GROUP + hh) * HEAD_DIM
                o_ref[:, col:col + HEAD_DIM] = o[hh * BLOCK:(hh + 1) * BLOCK].astype(o_ref.dtype)

    return _pcall(
        body, name=name,
        out_shape=[_sds((s, dq), BF16), _sds((nkv, s // BLOCK, KV_GROUP * BLOCK, 2 * BLOCK), BF16),
                   _sds((nkv, s // BLOCK, KV_GROUP * BLOCK, 1), F32)],
        grid=(nkv // ng, s // BLOCK),
        in_specs=[q_spec, prev, cur, prev, cur, bias_spec, smem], out_specs=[q_spec, p_spec, ps_spec], deps=deps,
    )(q, k, k, v, v, bias, sinks)


def _attn_bwd(name, q, k, v, do, probs, sink_probs):
    s, dq = q.shape
    nkv = k.shape[0]
    ng = _kv_heads_per_step(nkv)
    gq = ng * KV_GROUP * HEAD_DIM
    q_spec, prev, cur, bias_spec, _, p_spec, ps_spec = _attn_specs(ng)

    def body(q_ref, do_ref, kp_ref, kc_ref, vp_ref, vc_ref, p_ref, ps_ref,
             dq_ref, dbq_ref, dkc_ref, dkp_ref, dvc_ref, dvp_ref, dbias_ref, dsink_ref):
        @pl.when(pl.program_id(1) == 0)
        def _():
            dbias_ref[...] = jnp.zeros_like(dbias_ref)
            dsink_ref[...] = jnp.zeros_like(dsink_ref)
            dbq_ref[...] = jnp.zeros_like(dbq_ref)

        for g in range(ng):
            kb = jnp.concatenate([kp_ref[g], kc_ref[g]], axis=0)
            vb = jnp.concatenate([vp_ref[g], vc_ref[g]], axis=0)
            q, do = _stack_heads(q_ref, g), _stack_heads(do_ref, g)
            pb = p_ref[g]
            p = pb.astype(F32)
            dp = lax.dot_general(do, vb, NT, preferred_element_type=F32)
            delta = jnp.sum(p * dp, axis=-1, keepdims=True)
            ds = p * (dp - delta)
            dsb = ds.astype(BF16)
            dq = jnp.dot(dsb, kb, preferred_element_type=F32) * 0.125
            dsk = -(ps_ref[g] * delta)
            for hh in range(KV_GROUP):
                col, rows = (g * KV_GROUP + hh) * HEAD_DIM, slice(hh * BLOCK, (hh + 1) * BLOCK)
                dq_ref[:, col:col + HEAD_DIM] = dq[rows].astype(dq_ref.dtype)
                dbq_ref[:, col:col + HEAD_DIM] += jnp.sum(dq[rows], axis=0, keepdims=True)
                dsink_ref[g, :, hh:hh + 1] += jnp.sum(dsk[rows], axis=0, keepdims=True)
            dkb = lax.dot_general(dsb, q, TN, preferred_element_type=F32) * 0.125
            dvb = lax.dot_general(pb, do, TN, preferred_element_type=F32)
            dkp_ref[g], dkc_ref[g] = dkb[:BLOCK], dkb[BLOCK:]
            dvp_ref[g], dvc_ref[g] = dvb[:BLOCK], dvb[BLOCK:]
            dbias_ref[g * KV_GROUP:(g + 1) * KV_GROUP] += ds.reshape(KV_GROUP, BLOCK, 2 * BLOCK)

    kv_out = _sds((nkv, s, HEAD_DIM), F32)
    return _pcall(
        body, name=name,
        out_shape=[_sds((s, dq), BF16), _sds((1, dq), F32), kv_out, kv_out, kv_out, kv_out,
                   _sds((nkv * KV_GROUP, BLOCK, 2 * BLOCK), F32), _sds((nkv, 1, KV_GROUP), F32)],
        grid=(nkv // ng, s // BLOCK),
        in_specs=[q_spec, q_spec, prev, cur, prev, cur, p_spec, ps_spec],
        out_specs=[q_spec, pl.BlockSpec((1, gq), lambda kh, i: (0, kh)), cur, cur, cur, cur, bias_spec,
                   pl.BlockSpec((ng, 1, KV_GROUP), lambda kh, i: (kh, 0, 0))],
    )(q, do, k, k, v, v, probs, sink_probs)


def _kv_grad(name, parts):
    nkv, s, _ = parts[0][0].shape
    nb = s // BLOCK
    w = 2 * nkv * HEAD_DIM
    n = len(parts)

    def body(*refs):
        o_ref, cs_ref = refs[4 * n], refs[4 * n + 1]
        i = pl.program_id(0)
        keep = jnp.where(i < nb - 1, 1.0, 0.0).astype(F32)

        @pl.when(i == 0)
        def _():
            cs_ref[...] = jnp.zeros_like(cs_ref)

        for which in range(2):
            for hh in range(nkv):
                val = None
                for l in range(n):
                    cur_ref, nxt_ref = refs[4 * l + 2 * which], refs[4 * l + 2 * which + 1]
                    t = cur_ref[hh] + keep * nxt_ref[hh]
                    val = t if val is None else val + t
                sl = slice((which * nkv + hh) * HEAD_DIM, (which * nkv + hh + 1) * HEAD_DIM)
                o_ref[:, sl] = val.astype(o_ref.dtype)
                cs_ref[:, sl] += jnp.sum(val, axis=0, keepdims=True)

    cur = pl.BlockSpec((nkv, BLOCK, HEAD_DIM), lambda i: (0, i, 0))
    nxt = pl.BlockSpec((nkv, BLOCK, HEAD_DIM), lambda i: (0, jnp.minimum(i + 1, nb - 1), 0))
    flat = [a for p in parts for a in p]
    return _pcall(
        body, name=name, out_shape=[_sds((s, w), BF16), _sds((1, w), F32)], grid=(nb,),
        in_specs=[cur, nxt] * (2 * n),
        out_specs=[pl.BlockSpec((BLOCK, w), lambda i: (i, 0)), pl.BlockSpec((1, w), lambda i: (0, 0))],
    )(*flat)


def _adamw_math(w, g, m, v):
    m = ADAM_B1 * m + (1.0 - ADAM_B1) * g
    v = ADAM_B2 * v + (1.0 - ADAM_B2) * (g * g)
    m_hat = m / (1.0 - ADAM_B1 ** ADAM_STEP)
    v_hat = v / (1.0 - ADAM_B2 ** ADAM_STEP)
    delta = -ADAM_LR * (m_hat / (jnp.sqrt(v_hat) + ADAM_EPS) + ADAM_WD * w)
    return delta, m, v


def _adamw_shard(name, w, m, v, parts, row0, layer, prev, deps=(), own=None, where=None):
    _, r, wd = w.shape
    tr = _row_tile(r, wd, budget=3 * 512 * 1024)
    assert row0 % tr == 0

    def step(w_ref, m_ref, v_ref, g, g_ref, d_ref, nm_ref, nv_ref):
        delta, nm, nv = _adamw_math(w_ref[...], g, m_ref[...], v_ref[...])
        g_ref[...], d_ref[...], nm_ref[...], nv_ref[...] = g, delta, nm, nv

    out = _sds(w.shape, F32)
    if own is None:
        def body(w_ref, m_ref, v_ref, p_ref, a0, a1, a2, a3, g_ref, d_ref, nm_ref, nv_ref):
            g = p_ref[0].astype(F32)
            for k in range(1, NCHIP):
                g = g + p_ref[k].astype(F32)
            step(w_ref, m_ref, v_ref, g, g_ref, d_ref, nm_ref, nv_ref)

        par = pl.BlockSpec((None, tr, wd), lambda i: (layer, i, 0))
        return _pcall(
            body, name=name, out_shape=[out, out, out, out], grid=(r // tr,),
            in_specs=[par, par, par, pl.BlockSpec((NCHIP, tr, wd), lambda i: (0, row0 // tr + i, 0)), ANY, ANY, ANY, ANY],
            out_specs=[par, par, par, par], aliases={4: 0, 5: 1, 6: 2, 7: 3}, deps=deps,
        )(w, m, v, parts, *prev)

    def body(where_ref, w_ref, m_ref, v_ref, p_ref, o_ref, a0, a1, a2, a3, g_ref, d_ref, nm_ref, nv_ref):
        mine = lax.broadcasted_iota(jnp.int32, (tr, wd), 0) * 0 + where_ref[1]
        g = None
        for k in range(NCHIP):
            t = jnp.where(mine == k, o_ref[...], p_ref[k]).astype(F32)
            g = t if g is None else g + t
        step(w_ref, m_ref, v_ref, g, g_ref, d_ref, nm_ref, nv_ref)

    par = pl.BlockSpec((None, tr, wd), lambda i, wr: (layer, i, 0))
    return _pcall(
        body, name=name, out_shape=[out, out, out, out], grid=(r // tr,), prefetch=1,
        in_specs=[par, par, par, pl.BlockSpec((NCHIP, tr, wd), lambda i, wr: (0, row0 // tr + i, 0)),
                  pl.BlockSpec((None, tr, wd), lambda i, wr: (wr[1], row0 // tr + i, 0)), ANY, ANY, ANY, ANY],
        out_specs=[par, par, par, par], aliases={6: 0, 7: 1, 8: 2, 9: 3}, deps=deps,
    )(where, w, m, v, parts, own, *prev)


def _sum_devices(name, gathered):
    _, r, wd = gathered.shape

    def body(g_ref, o_ref):
        acc = g_ref[0]
        for k in range(1, NDEV):
            acc = acc + g_ref[k]
        o_ref[...] = acc

    return _pcall(body, name=name, out_shape=_sds((r, wd), F32), grid=(1,),
                  in_specs=[pl.BlockSpec((NDEV, r, wd), lambda i: (0, 0, 0))],
                  out_specs=pl.BlockSpec((r, wd), lambda i: (0, 0)))(gathered)


def _adamw_flat(name, w, g, m, v):
    shape = w.shape

    def body(w_ref, g_ref, m_ref, v_ref, d_ref, nm_ref, nv_ref):
        d_ref[...], nm_ref[...], nv_ref[...] = _adamw_math(w_ref[...], g_ref[...], m_ref[...], v_ref[...])

    spec = pl.BlockSpec(shape, lambda i: (0, 0))
    out = _sds(shape, F32)
    return _pcall(body, name=name, out_shape=[out, out, out], grid=(1,), in_specs=[spec] * 4,
                  out_specs=[spec] * 3)(w, g, m, v)


def _cast_into(name, src, layer, buf, row0, me):
    _, r, wd = src.shape
    tr = _row_tile(r, wd)
    assert row0 % tr == 0

    def body(me_ref, s_ref, b_ref, o_ref):
        o_ref[...] = s_ref[...].astype(o_ref.dtype)

    return _pcall(
        body, name=name, out_shape=_sds(buf.shape, buf.dtype), grid=(r // tr,), prefetch=1,
        in_specs=[pl.BlockSpec((None, tr, wd), lambda i, mr: (layer, i, 0)), ANY],
        out_specs=pl.BlockSpec((None, tr, wd), lambda i, mr: (mr[0], row0 // tr + i, 0)), aliases={2: 0},
    )(me, src, buf)


def _pack(arrays):
    rows = []
    for a in arrays:
        flat = a.reshape(-1).astype(F32)
        pad = (-flat.shape[0]) % 1024
        rows.append(jnp.pad(flat, (0, pad)).reshape(-1, 128))
    return jnp.concatenate(rows, axis=0)


def _unpack(packed, shapes):
    out, r = [], 0
    for shp in shapes:
        n = int(np.prod(shp))
        nr = (n + 1023) // 1024 * 8
        out.append(packed[r:r + nr].reshape(-1)[:n].reshape(shp))
        r += nr
    return out


def kernel(x, mix_norm, ffn_norm, a_w_in, a_norm_v, a_w_s, a_b_s, a_w_out, kv_norm, w_kv, b_kv, b_w_q, b_b_q, b_sinks, b_w_o, b_b_o, rel_bias, ffn_w_gate, ffn_w_up, ffn_w_down, final_norm, loss_target, m_mix_norm, m_ffn_norm, m_a_w_in, m_a_norm_v, m_a_w_s, m_a_b_s, m_a_w_out, m_kv_norm, m_w_kv, m_b_kv, m_b_w_q, m_b_b_q, m_b_sinks, m_b_w_o, m_b_b_o, m_rel_bias, m_ffn_w_gate, m_ffn_w_up, m_ffn_w_down, m_final_norm, v_mix_norm, v_ffn_norm, v_a_w_in, v_a_norm_v, v_a_w_s, v_a_b_s, v_a_w_out, v_kv_norm, v_w_kv, v_b_kv, v_b_w_q, v_b_b_q, v_b_sinks, v_b_w_o, v_b_b_o, v_rel_bias, v_ffn_w_gate, v_ffn_w_up, v_ffn_w_down, v_final_norm):
    _, S, D = x.shape
    LA, LB, L = a_w_in.shape[0], b_w_q.shape[0], ffn_w_gate.shape[0]
    F = ffn_w_gate.shape[2]
    DS = D // NDEV
    ZC = a_w_in.shape[2]
    KVW = w_kv.shape[1]
    NKV = KVW // (2 * HEAD_DIM)
    NH = D // HEAD_DIM
    assert ZC * NDEV == 2 * D and NH == NKV * KV_GROUP and S % BLOCK == 0
    TM = min(1024, S)
    TN_ = min(1024, D)
    TS = min(512, D)
    KC = 4

    ix, iy, ic = lax.axis_index("x"), lax.axis_index("y"), lax.axis_index("c")
    me = (4 * ix + 2 * iy + ic).astype(jnp.int32)
    me1 = me.reshape(1)
    where = jnp.stack([ic, 2 * ix + iy]).astype(jnp.int32)

    def tr3(a):
        return jnp.transpose(a, (0, 2, 1))

    gate_t, up_t = tr3(ffn_w_gate), tr3(ffn_w_up)
    w_kv3 = w_kv.reshape((1,) + w_kv.shape)

    def layer_arrays(l):
        arrs = [("gu", 2 * F, D, [(gate_t, l, 0), (up_t, l, F)]), ("down", F, D, [(ffn_w_down, l, 0)])]
        if l < LA:
            arrs += [("win", D, ZC, [(a_w_in, l, 0)]), ("wout", DS, D, [(a_w_out, l, 0)])]
            if l == LA - 1:
                arrs.append(("wkv", DS, KVW, [(w_kv3, 0, 0)]))
        else:
            i_b = l - LA
            arrs.append(("wqo", 2 * DS, D, [(b_w_q, i_b, 0), (b_w_o, i_b, DS)]))
        return arrs

    gathers = []

    def gather_begin(g_idx, deps):
        g = gathers[g_idx]
        g["send"], g["recv"], g["bufs"], g["token"] = _relay_start(f"relay_start{g_idx}", g["bufs"], deps)

    for l in range(L):
        mixer, ffn = dict(keys=[], bufs=[]), dict(keys=[], bufs=[])
        for key, rows, width, sources in layer_arrays(l):
            buf = lax.empty((NDEV, rows, width), BF16)
            for si, (src, li, row0) in enumerate(sources):
                buf = _cast_into(f"cast_{key}{l}_{si}", src, li, buf, row0, me1)
            group = ffn if key in ("gu", "down") else mixer
            group["keys"].append(key)
            group["bufs"].append(buf)
        gathers += [mixer, ffn]
        if l == 0:
            nv_rows = _pack([a_norm_v])
            nv = _cast_into("put_norm_v", nv_rows.reshape((1,) + nv_rows.shape), 0,
                            lax.empty((NDEV,) + nv_rows.shape, F32), 0, me1)
            nv_send, nv_recv, nv_bufs, token = _gather_start("gather_norm_v_start", [nv], [])
            gather_begin(0, [token])

    def gather_relay(g_idx, deps):
        g = gathers[g_idx]
        g["fsend"], g["frecv"], g["bufs"], tok = _relay_neighbors(f"relay_neighbors{g_idx}", g["bufs"], g["send"],
                                                                  g["recv"], deps)
        if g_idx + 1 < len(gathers):
            gather_begin(g_idx + 1, [tok])
            tok = gathers[g_idx + 1]["token"]
        return tok

    def finish_gather(g_idx, deps):
        g = gathers[g_idx]
        gsend, grecv, bufs = _relay_diagonal(f"relay_diagonal{g_idx}", g["bufs"], g["fsend"], g["frecv"], deps)
        bufs = _relay_finish(f"relay_finish{g_idx}", bufs, g["send"], g["recv"], g["fsend"], g["frecv"], gsend, grecv)
        return dict(zip(g["keys"], bufs))

    token = gather_relay(0, [gathers[0]["token"]] + [b for g in gathers[1:] for b in g["bufs"]])

    buckets = jnp.asarray(_bucket_table())
    bias = _bias_table("bias_table", rel_bias.T, buckets).reshape(NH, BLOCK, 2 * BLOCK)

    def rows_full(tm):
        return pl.BlockSpec((tm, D), lambda i, j: (i, 0))

    def tile(tm, tn):
        return pl.BlockSpec((tm, tn), lambda i, j: (i, j))

    vec_tile = pl.BlockSpec((1, TN_), lambda i, j: (0, j))

    def ffn_forward(l, wl, h_mid, tag, relay):
        xf = _rms_fwd(f"ffn_norm_fwd{tag}", h_mid, ffn_norm[l])

        def ep(parts, ex, outs):
            a, b = parts
            sg = jax.nn.sigmoid(a)
            silu = a * sg
            outs[0][0] = (b * (sg * (1.0 + a * (1.0 - sg)))).astype(BF16)
            outs[0][1] = silu.astype(BF16)
            outs[1][...] = (silu * b).astype(BF16)

        ab, hid = _gemm(
            f"ffn_up{tag}", (S // TM, NDEV),
            [(xf, rows_full(TM)),
             (wl["gu"], pl.BlockSpec((None, F, D), lambda i, e: (e, 0, 0))),
             (wl["gu"], pl.BlockSpec((None, F, D), lambda i, e: (e, 1, 0)))],
            [(0, 1, NT), (0, 2, NT)], [],
            [(_sds((2, NDEV, S, F), BF16), pl.BlockSpec((2, None, TM, F), lambda i, e: (0, e, i, 0))),
             (_sds((NDEV, S, F), BF16), pl.BlockSpec((None, TM, F), lambda i, e: (e, i, 0)))],
            ep, separate=True)
        tok = relay(hid)
        (h_out,) = _gemm(
            f"ffn_down{tag}", (S // TM, D // TN_, NDEV // KC),
            [(hid, pl.BlockSpec((KC, TM, F), lambda i, j, k: (k, i, 0))),
             (wl["down"], pl.BlockSpec((KC, F, TN_), lambda i, j, k: (k, 0, j)))],
            [(0, 1, NN, _pick(c), _pick(c)) for c in range(KC)],
            [(h_mid, pl.BlockSpec((TM, TN_), lambda i, j, k: (i, j)))],
            [(_sds((S, D), F32), pl.BlockSpec((TM, TN_), lambda i, j, k: (i, j)))],
            _store_add_extra, nk=NDEV // KC, acc_shape=(TM, TN_), deps=[] if tok is None else [tok])
        return dict(h_mid=h_mid, xf=xf, ab=ab, hid=hid), h_out

    def stacked_rows_gemm(name, a, wmat, blk, extras, ep, out_dtype, deps=()):
        return _gemm(
            name, (S // TM, D // TN_),
            [(a, rows_full(TM)), (wmat, pl.BlockSpec((NDEV, DS, TN_), lambda i, j: (0, blk, j)))],
            [(0, 1, NN, None, _stacked)], extras,
            [(_sds((S, D), out_dtype), tile(TM, TN_))], ep, deps=deps)[0]

    def back_rows_gemm(name, a, wmat, blk, out_dtype, deps=()):
        return _gemm(
            name, (S // TM, NDEV),
            [(a, rows_full(TM)), (wmat, pl.BlockSpec((None, DS, D), lambda i, e: (e, blk, 0)))],
            [(0, 1, NT)], [], [(_sds((S, D), out_dtype), pl.BlockSpec((TM, DS), lambda i, e: (i, e)))], _store,
            deps=deps)[0]

    def grad_rows_gemm(name, act, d_bf, buf, blk):
        return _gemm(
            name, (NDEV,),
            [(act, pl.BlockSpec((S, DS), lambda e: (0, e))), (d_bf, pl.BlockSpec((S, D), lambda e: (0, 0)))],
            [(0, 1, TN)], [(buf, ANY)],
            [(_sds(buf.shape, BF16), pl.BlockSpec((None, DS, D), lambda e: (e, blk, 0)))],
            _store, aliases={2: 0})[0]

    saved, weights = [], []
    h = x.reshape(S, D)
    k_heads = v_heads = hn = h_kv = norm_v = None
    for layer in range(L):
        wl = finish_gather(2 * layer, [token] if layer == 0 else [h])
        weights.append(wl)
        if layer == 0:
            nv_fsend, nv_frecv, nv_bufs = _gather_forward("gather_norm_v_forward", nv_bufs, nv_send, nv_recv,
                                                          [wl["win"]])
            (nv_all,) = _gather_finish("gather_norm_v_finish", nv_bufs, nv_send, nv_recv, nv_fsend, nv_frecv)
            norm_v = jnp.transpose(nv_all.reshape(NDEV, -1)[:, :LA * DS].reshape(NDEV, LA, DS), (1, 0, 2)).reshape(LA, D)
        sv = dict(h_in=h)
        xn = _rms_fwd(f"mix_norm_fwd{layer}", h, mix_norm[layer])
        sv["xn"] = xn
        if layer < LA:
            i_a = layer
            (zp,) = _gemm(
                f"gmlp_in{layer}", (S // TM, NDEV),
                [(xn, rows_full(TM)), (wl["win"], pl.BlockSpec((None, D, ZC), lambda i, e: (e, 0, 0)))],
                [(0, 1, NN)], [], [(_sds((S, 2 * D), F32), pl.BlockSpec((TM, ZC), lambda i, e: (i, e)))], _store)
            bst = a_b_s[i_a].T
            gated = _gmlp_fwd(f"gmlp_gate{layer}", zp, norm_v[i_a].reshape(1, D), a_w_s[i_a], bst)
            sv.update(zp=zp, gated=gated, bst=bst)
            relay_token = gather_relay(2 * layer + 1, [gated])
            h_mid = stacked_rows_gemm(f"gmlp_out{layer}", gated, wl["wout"], 0, [(h, tile(TM, TN_))],
                                      _store_add_extra, F32, deps=[relay_token])
        else:
            i_b = layer - LA
            q = stacked_rows_gemm(f"attn_q{layer}", xn, wl["wqo"], 0, [(b_b_q[i_b].reshape(1, D), vec_tile)],
                                  _store_add_extra, BF16)
            relay_token = gather_relay(2 * layer + 1, [q])
            attn, probs, sink_probs = _attn_fwd(f"attn_fwd{layer}", q, k_heads, v_heads, bias, b_sinks[i_b],
                                                deps=[relay_token])
            sv.update(q=q, attn=attn, probs=probs, sink_probs=sink_probs)
            h_mid = stacked_rows_gemm(f"attn_o{layer}", attn, wl["wqo"], 1,
                                      [(h, tile(TM, TN_)), (b_b_o[i_b].reshape(1, D), vec_tile)],
                                      _store_add_extra, F32)
        wl.update(finish_gather(2 * layer + 1, [h_mid]))
        next_relay = (lambda dep: gather_relay(2 * layer + 2, [dep])) if layer + 1 < L else (lambda dep: None)
        fsv, h = ffn_forward(layer, wl, h_mid, str(layer), next_relay)
        sv.update(fsv)
        saved.append(sv)
        if layer == LA - 1:
            h_kv = h
            hn = _rms_fwd("kv_norm_fwd", h, kv_norm)

            def kv_ep(acc, ex, outs):
                val = acc + ex[0][...]
                for hh in range(NKV):
                    outs[0][hh] = val[:, hh * HEAD_DIM:(hh + 1) * HEAD_DIM].astype(BF16)
                    outs[1][hh] = val[:, (NKV + hh) * HEAD_DIM:(NKV + hh + 1) * HEAD_DIM].astype(BF16)

            k_heads, v_heads = _gemm(
                "kv_proj", (S // TM,),
                [(hn, pl.BlockSpec((TM, D), lambda i: (i, 0))),
                 (wl["wkv"], pl.BlockSpec((NDEV, DS, KVW), lambda i: (0, 0, 0)))],
                [(0, 1, NN, None, _stacked)], [(b_kv.reshape(1, KVW), pl.BlockSpec((1, KVW), lambda i: (0, 0)))],
                [(_sds((NKV, S, HEAD_DIM), BF16), pl.BlockSpec((NKV, TM, HEAD_DIM), lambda i: (0, i, 0)))] * 2,
                kv_ep)

    loss11, d, d_bf, g_final = _loss_bwd("loss_bwd", h, final_norm, loss_target.reshape(S, D))
    loss = lax.psum(loss11[0, 0], AXES)

    g_mix, g_ffn = [None] * L, [None] * L
    g_ws, g_bs, g_nv = [None] * LA, [None] * LA, [None] * LA
    g_bq, g_sink, g_bo = [None] * LB, [None] * LB, [None] * LB
    dbiases = []
    kv_parts = []
    g_kvn = g_bkv = None
    exchanges = [[] for _ in range(L)]
    pending = None
    grads_wkv = None
    newest = []

    def new_grads(l):
        return {key: lax.empty((NDEV, rows, width), BF16) for key, rows, width, _ in layer_arrays(l)}

    def exchange_begin(tag, l, gl, keys):
        grads = [gl[k] for k in keys]
        lands = [lax.empty((NCHIP,) + g.shape[1:], BF16) for g in grads]
        send, recv, grads, lands, tok = _sibling_start(f"rs_sibling_start{tag}", grads, lands, [])
        newest[:] = [tok]
        return dict(tag=tag, layer=l, keys=keys, grads=grads, lands=lands, send=send, recv=recv)

    def exchange_middle(st, dep):
        tag = st["tag"]
        grads, lands = _sibling_finish(f"rs_sibling_finish{tag}", st["grads"], st["lands"], st["send"], st["recv"], [dep])
        sums, own = [], []
        for t, key in enumerate(st["keys"]):
            s_, o_ = _pair_sum(f"pair_sum_{key}{tag}", grads[t], lands[t], where)
            sums.append(s_)
            own.append(o_)
        send, recv, sums, own, tok = _chips_start(f"rs_chips_start{tag}", sums, own, [])
        newest[:] = [tok]
        st.update(sums=sums, own=own, send2=send, recv2=recv)
        exchanges[st["layer"]].append(st)

    def exchange_end(st, dep):
        sums, lands = _chips_finish(f"rs_chips_finish{st['tag']}", st["sums"], st["own"], st["send2"], st["recv2"], [dep])
        own = dict(zip(st["keys"], sums)) if st.get("direct") else {k: None for k in st["keys"]}
        return dict(zip(st["keys"], lands)), own

    for layer in reversed(range(L)):
        sv, wl = saved[layer], weights[layer]
        tag = str(layer)
        gl = new_grads(layer)
        if grads_wkv is not None and layer == LA - 1:
            gl["wkv"] = grads_wkv
        def dhid_ep(acc, ex, outs):
            outs[0][0] = (acc * ex[0][0].astype(F32)).astype(BF16)
            outs[0][1] = (acc * ex[0][1].astype(F32)).astype(BF16)

        ab_spec = pl.BlockSpec((2, None, TM, F), lambda i, e: (0, e, i, 0))
        (dab,) = _gemm(
            f"ffn_dhid{tag}", (S // TM, NDEV),
            [(d_bf, rows_full(TM)), (wl["down"], pl.BlockSpec((None, F, D), lambda i, e: (e, 0, 0)))],
            [(0, 1, NT)], [(sv["ab"], ab_spec)], [(_sds((2, NDEV, S, F), BF16), ab_spec)], dhid_ep,
            deps=list(newest))
        if pending:
            exchange_middle(pending, dab)
        act_kinds = [SHARDS, WHOLE, SHARDS2, WHOLE]
        act_lands = [lax.empty((NCHIP, S, F), BF16), lax.empty((S, D), BF16), lax.empty((2, NCHIP, S, F), BF16),
                     lax.empty((S, D), BF16)]
        a_send, a_recv, act, act_lands, tok = _sibling_start(f"act_start{tag}", [sv["hid"], d_bf, dab, sv["xf"]], act_lands,
                                                             list(newest), act_kinds)
        newest[:] = [tok]
        (dxf,) = _gemm(
            f"ffn_dx{tag}", (S // TM, D // TN_, 2 * NDEV // KC),
            [(act[2].reshape(2 * NDEV // KC, KC, S, F), pl.BlockSpec((None, KC, TM, F), lambda i, j, k: (k, 0, i, 0))),
             (wl["gu"], pl.BlockSpec((KC, F, TN_), lambda i, j, k: (k % (NDEV // KC), k // (NDEV // KC), j)))],
            [(0, 1, NN, _pick(c), _pick(c)) for c in range(KC)], [],
            [(_sds((S, D), F32), pl.BlockSpec((TM, TN_), lambda i, j, k: (i, j)))],
            _store, nk=2 * NDEV // KC, acc_shape=(TM, TN_), deps=list(newest))
        (hid_o, dout_o, dab_o, xf_o), (hid_s, dout_s, dab_s, xf_s) = _sibling_finish(
            f"act_finish{tag}", act, act_lands, a_send, a_recv, [dxf], act_kinds)
        (p_down,) = _gemm(
            f"ffn_dwdown{tag}", (NCHIP, D // TN_),
            [(hid_o.reshape(NCHIP, 2, S, F), pl.BlockSpec((None, None, S, F), lambda k, j, wr: (k, wr[0], 0, 0))),
             (dout_o, pl.BlockSpec((S, TN_), lambda k, j, wr: (0, j))),
             (hid_s, pl.BlockSpec((None, S, F), lambda k, j, wr: (k, 0, 0))),
             (dout_s, pl.BlockSpec((S, TN_), lambda k, j, wr: (0, j)))],
            [(0, 1, TN), (2, 3, TN)], [],
            [(_sds((NCHIP, F, D), BF16), pl.BlockSpec((None, F, TN_), lambda k, j, wr: (k, 0, j)))],
            _store, prefetch=[where])
        (p_gu,) = _gemm(
            f"ffn_dwup{tag}", (2, NCHIP, D // TN_),
            [(dab_o.reshape(2, NCHIP, 2, S, F),
              pl.BlockSpec((None, None, None, S, F), lambda w, k, j, wr: (w, k, wr[0], 0, 0))),
             (xf_o, pl.BlockSpec((S, TN_), lambda w, k, j, wr: (0, j))),
             (dab_s, pl.BlockSpec((None, None, S, F), lambda w, k, j, wr: (w, k, 0, 0))),
             (xf_s, pl.BlockSpec((S, TN_), lambda w, k, j, wr: (0, j)))],
            [(0, 1, TN), (2, 3, TN)], [],
            [(_sds((NCHIP, 2 * F, D), BF16), pl.BlockSpec((None, F, TN_), lambda w, k, j, wr: (k, w, j)))],
            _store, prefetch=[where])
        send2, recv2, sums, own, tok = _chips_start(
            f"rs_chips_start_ffn{tag}", [p_gu, p_down], [lax.empty(p_gu.shape, BF16), lax.empty(p_down.shape, BF16)], [])
        newest[:] = [tok]
        exchanges[layer].append(dict(tag=f"_ffn{tag}", layer=layer, keys=["gu", "down"], sums=sums, own=own, send2=send2,
                                     recv2=recv2, direct=True))
        d, d_bf, g_ffn[layer], colsum = _rms_bwd(f"ffn_norm_bwd{tag}", sv["h_mid"], ffn_norm[layer], dxf, d,
                                                 deps=list(newest))
        if layer < LA:
            i_a = layer
            dgated = back_rows_gemm(f"gmlp_dgated{tag}", d_bf, wl["wout"], 0, F32)
            gl["wout"] = grad_rows_gemm(f"gmlp_dwout{tag}", sv["gated"], d_bf, gl["wout"], 0)
            dzp, g_ws[i_a], dbs, g_nv[i_a] = _gmlp_bwd(f"gmlp_bwd{tag}", sv["zp"], dgated,
                                                       norm_v[i_a].reshape(1, D), a_w_s[i_a], sv["bst"])
            g_bs[i_a] = dbs[:, 0, :]
            (gl["win"],) = _gemm(
                f"gmlp_dwin{tag}", (NDEV, D // TS),
                [(sv["xn"], pl.BlockSpec((S, TS), lambda e, i: (0, i))),
                 (dzp, pl.BlockSpec((S, ZC), lambda e, i: (0, e)))],
                [(0, 1, TN)], [(gl["win"], ANY)],
                [(_sds(gl["win"].shape, BF16), pl.BlockSpec((None, TS, ZC), lambda e, i: (e, i, 0)))],
                _store, aliases={2: 0})
            (dxn,) = _gemm(
                f"gmlp_dx{tag}", (S // TM, D // TN_, NDEV // KC),
                [(dzp, pl.BlockSpec((TM, KC * ZC), lambda i, j, k: (i, k))),
                 (wl["win"], pl.BlockSpec((KC, TN_, ZC), lambda i, j, k: (k, j, 0)))],
                [(0, 1, NT, _cols(c, ZC), _pick(c)) for c in range(KC)], [],
                [(_sds((S, D), F32), pl.BlockSpec((TM, TN_), lambda i, j, k: (i, j)))],
                _store, nk=NDEV // KC, acc_shape=(TM, TN_))
        else:
            i_b = layer - LA
            g_bo[i_b] = colsum
            dattn = back_rows_gemm(f"attn_dout{tag}", d_bf, wl["wqo"], 1, BF16)
            gl["wqo"] = grad_rows_gemm(f"attn_dwo{tag}", sv["attn"], d_bf, gl["wqo"], 1)
            dq, g_bq[i_b], dkc, dkp, dvc, dvp, dbias, dsink = _attn_bwd(
                f"attn_bwd{tag}", sv["q"], k_heads, v_heads, dattn, sv["probs"], sv["sink_probs"])
            kv_parts.append((dkc, dkp, dvc, dvp))
            g_sink[i_b] = dsink.reshape(NH)
            dbiases.append(dbias.reshape(NH, BLOCK * 2 * BLOCK))
            gl["wqo"] = grad_rows_gemm(f"attn_dwq{tag}", sv["xn"], dq, gl["wqo"], 0)
            dxn = back_rows_gemm(f"attn_dx{tag}", dq, wl["wqo"], 0, F32)
        d, d_bf, g_mix[layer], _ = _rms_bwd(f"mix_norm_bwd{tag}", sv["h_in"], mix_norm[layer], dxn, d)
        pending = exchange_begin(f"_mix{tag}", layer, gl, [k for k in gl if k not in ("gu", "down")])
        if layer == LA:
            wkv = weights[LA - 1]["wkv"]
            dkv, g_bkv = _kv_grad("kv_grad", kv_parts)
            (grads_wkv,) = _gemm(
                "kv_dw", (NDEV,),
                [(hn, pl.BlockSpec((S, DS), lambda e: (0, e))), (dkv, pl.BlockSpec((S, KVW), lambda e: (0, 0)))],
                [(0, 1, TN)], [(lax.empty((NDEV, DS, KVW), BF16), ANY)],
                [(_sds((NDEV, DS, KVW), BF16), pl.BlockSpec((None, DS, KVW), lambda e: (e, 0, 0)))],
                _store, aliases={2: 0}, deps=list(newest))
            (dhn,) = _gemm(
                "kv_dx", (S // TM, NDEV),
                [(dkv, pl.BlockSpec((TM, KVW), lambda i, e: (i, 0))),
                 (wkv, pl.BlockSpec((None, DS, KVW), lambda i, e: (e, 0, 0)))],
                [(0, 1, NT)], [], [(_sds((S, D), F32), pl.BlockSpec((TM, DS), lambda i, e: (i, e)))], _store)
            d, d_bf, g_kvn, _ = _rms_bwd("kv_norm_bwd", h_kv, kv_norm, dhn, d)
    grad_x = d.reshape(x.shape)

    exchange_middle(pending, d)

    g_rel = _bias_grad("bias_grad", dbiases, buckets)
    small_local = _pack([jnp.concatenate(g_mix, axis=0), jnp.concatenate(g_ffn, axis=0), jnp.stack(g_ws),
                         jnp.stack(g_bs), g_kvn, g_bkv, jnp.concatenate(g_bq, axis=0), jnp.stack(g_sink),
                         jnp.concatenate(g_bo, axis=0), g_rel, g_final, jnp.concatenate(g_nv, axis=0)])
    small_slot = _cast_into("put_small_grads", small_local.reshape((1,) + small_local.shape), 0,
                            lax.empty((NDEV,) + small_local.shape, F32), 0, me1)
    s_send, s_recv, s_bufs, s_tok = _gather_start("gather_small_start", [small_slot], list(newest))

    results = {}
    after = [s_tok]

    def upd(pname, w, m, v, l, li, lands, row0, own=None):
        w3 = w if w.ndim == 3 else w.reshape((1,) + w.shape)
        prev = results.get(pname) or [lax.empty(w3.shape, F32) for _ in range(4)]
        results[pname] = _adamw_shard(f"adamw_{pname}{l}", w3, m.reshape(w3.shape), v.reshape(w3.shape), lands,
                                      row0, li, prev, deps=list(after), own=own, where=where)
        after[:] = [results[pname][0]]

    for l in reversed(range(L)):
        for st in exchanges[l]:
            lands, own = exchange_end(st, after[0])
            if "gu" in lands:
                upd("ffn_w_gate", gate_t, tr3(m_ffn_w_gate), tr3(v_ffn_w_gate), l, l, lands["gu"], 0, own["gu"])
                upd("ffn_w_up", up_t, tr3(m_ffn_w_up), tr3(v_ffn_w_up), l, l, lands["gu"], F, own["gu"])
                upd("ffn_w_down", ffn_w_down, m_ffn_w_down, v_ffn_w_down, l, l, lands["down"], 0, own["down"])
            if "win" in lands:
                upd("a_w_in", a_w_in, m_a_w_in, v_a_w_in, l, l, lands["win"], 0)
                upd("a_w_out", a_w_out, m_a_w_out, v_a_w_out, l, l, lands["wout"], 0)
            if "wkv" in lands:
                upd("w_kv", w_kv, m_w_kv, v_w_kv, l, 0, lands["wkv"], 0)
            if "wqo" in lands:
                upd("b_w_q", b_w_q, m_b_w_q, v_b_w_q, l, l - LA, lands["wqo"], 0)
                upd("b_w_o", b_w_o, m_b_w_o, v_b_w_o, l, l - LA, lands["wqo"], DS)
    for pname in ("ffn_w_gate", "ffn_w_up"):
        results[pname] = [tr3(r) for r in results[pname]]
    results["w_kv"] = [r.reshape(w_kv.shape) for r in results["w_kv"]]

    small_w = [mix_norm, ffn_norm, a_w_s, a_b_s, kv_norm, b_kv, b_b_q, b_sinks, b_b_o, rel_bias, final_norm]
    small_m = [m_mix_norm, m_ffn_norm, m_a_w_s, m_a_b_s, m_kv_norm, m_b_kv, m_b_b_q, m_b_sinks, m_b_b_o, m_rel_bias,
               m_final_norm]
    small_v = [v_mix_norm, v_ffn_norm, v_a_w_s, v_a_b_s, v_kv_norm, v_b_kv, v_b_b_q, v_b_sinks, v_b_b_o, v_rel_bias,
               v_final_norm]
    shapes = [w.shape for w in small_w] + [(LA, D)]
    s_fsend, s_frecv, s_bufs = _gather_forward("gather_small_forward", s_bufs, s_send, s_recv, list(after))
    (small_all,) = _gather_finish("gather_small_finish", s_bufs, s_send, s_recv, s_fsend, s_frecv)
    small_sum = _sum_devices("sum_small_grads", small_all)
    small_g = _unpack(small_sum, shapes)
    g_normv = lax.dynamic_slice_in_dim(small_g[-1], me * DS, DS, axis=1)
    small_g = small_g[:-1] + [g_normv]
    small_w, small_m, small_v = small_w + [a_norm_v], small_m + [m_a_norm_v], small_v + [v_a_norm_v]
    shapes = [w.shape for w in small_w]
    s_delta, s_m, s_v = _adamw_flat("adamw_small", _pack(small_w), _pack(small_g), _pack(small_m), _pack(small_v))
    s_delta, s_m, s_v = _unpack(s_delta, shapes), _unpack(s_m, shapes), _unpack(s_v, shapes)

    names = ["mix_norm", "ffn_norm", "a_w_in", "a_norm_v", "a_w_s", "a_b_s", "a_w_out", "kv_norm", "w_kv", "b_kv",
             "b_w_q", "b_b_q", "b_sinks", "b_w_o", "b_b_o", "rel_bias", "ffn_w_gate", "ffn_w_up", "ffn_w_down",
             "final_norm"]
    small_names = ["mix_norm", "ffn_norm", "a_w_s", "a_b_s", "kv_norm", "b_kv", "b_b_q", "b_sinks", "b_b_o", "rel_bias",
                   "final_norm", "a_norm_v"]
    res = {}
    for idx, nm in enumerate(small_names):
        res[nm] = (small_g[idx].reshape(shapes[idx]), s_delta[idx], s_m[idx], s_v[idx])
    for nm, u in results.items():
        res[nm] = tuple(u)
    out = [loss, grad_x]
    for part in range(4):
        out += [res[nm][part] for nm in names]
    return tuple(out)
```

```python
import math

import numpy as np
import jax
import jax.numpy as jnp
from jax import lax
from jax.experimental import pallas as pl
from jax.experimental.pallas import tpu as pltpu

F32 = jnp.float32
BF16 = jnp.bfloat16
AXES = ("x", "y", "c")
NDEV = 8
NCHIP = 4
CHUNK = 128
GROUPS = 8
HEAD_DIM = 64
KV_GROUP = 8
BLOCK = 128
N_BUCKETS = 32
MAX_DISTANCE = 128
RMS_EPS = 1e-5
NEG_INF = -1e30
ADAM_LR, ADAM_B1, ADAM_B2, ADAM_EPS, ADAM_WD, ADAM_STEP = 0.001, 0.9, 0.999, 1e-08, 0.01, 10
VMEM_LIMIT_BYTES = 56 * 1024 * 1024

NN = (((1,), (0,)), ((), ()))
NT = (((1,), (1,)), ((), ()))
TN = (((0,), (0,)), ((), ()))
ANY = pl.BlockSpec(memory_space=pl.ANY)
HBM = pl.BlockSpec(memory_space=pltpu.HBM)
SEM = pl.BlockSpec(memory_space=pltpu.SEMAPHORE)
MESH = pl.DeviceIdType.MESH
EFFECT = pltpu.SideEffectType.DATAFLOW_SIDE_EFFECTING


def _pcall(body, *, name, out_shape, in_specs, out_specs, grid=(), scratch=(), aliases=None, prefetch=0, deps=()):
    n_in, n_dep = len(in_specs), len(deps)
    if n_dep:
        inner = body

        def body(*refs):
            return inner(*refs[:prefetch + n_in], *refs[prefetch + n_in + n_dep:])

        in_specs = list(in_specs) + [ANY] * n_dep
    params = dict(vmem_limit_bytes=VMEM_LIMIT_BYTES)
    if grid:
        params["dimension_semantics"] = ("arbitrary",) * len(grid)
    kw = dict(name=name, out_shape=out_shape, compiler_params=pltpu.CompilerParams(**params),
              input_output_aliases=aliases or {})
    if prefetch:
        kw["grid_spec"] = pltpu.PrefetchScalarGridSpec(num_scalar_prefetch=prefetch, grid=grid, in_specs=in_specs,
                                                       out_specs=out_specs, scratch_shapes=list(scratch))
    else:
        kw.update(grid=grid, in_specs=in_specs, out_specs=out_specs, scratch_shapes=list(scratch))
    call = pl.pallas_call(body, **kw)
    return lambda *args: call(*args, *deps)


def _sds(shape, dtype):
    return jax.ShapeDtypeStruct(tuple(shape), dtype)


def _position():
    x, y, c = lax.axis_index("x"), lax.axis_index("y"), lax.axis_index("c")
    chips = [(1 - x, y), (x, 1 - y), (1 - x, 1 - y)]
    return x, y, c, chips


def _slot(px, py, pc):
    return 4 * px + 2 * py + pc


def _remote(ref_src, ref_dst, send, recv, to):
    return pltpu.make_async_remote_copy(src_ref=ref_src, dst_ref=ref_dst, send_sem=send, recv_sem=recv,
                                        device_id=to, device_id_type=MESH)


def _hbm(arrays):
    return [pltpu.with_memory_space_constraint(a, pltpu.HBM) for a in arrays]


def _split_call(body, name, out_shape, in_specs, out_specs, aliases):
    return pl.pallas_call(body, name=name, out_shape=out_shape, in_specs=in_specs, out_specs=out_specs,
                          input_output_aliases=aliases, compiler_params=pltpu.CompilerParams(has_side_effects=EFFECT))


def _token_shape():
    return _sds((8, 128), F32)


def _gather_start(name, bufs, deps):
    n, nd = len(bufs), len(deps)

    def body(*refs):
        ins, send, recv, token = refs[:n], refs[n + nd], refs[n + nd + 1], refs[2 * n + nd + 2]
        x, y, c, chips = _position()
        peers = [(x, y, 1 - c)] + [(*chip, c) for chip in chips]
        for t in range(n):
            mine = ins[t].at[_slot(x, y, c)]
            for k, peer in enumerate(peers):
                _remote(mine, mine, send.at[4 * t + k], recv.at[4 * t + k], peer).start()
        token[...] = jnp.zeros_like(token)

    res = _split_call(
        body, name,
        (pltpu.SemaphoreType.DMA((4 * n,)), pltpu.SemaphoreType.DMA((4 * n,)), *[pltpu.HBM(b.shape, b.dtype) for b in bufs],
         _token_shape()),
        [HBM] * n + [ANY] * nd, (SEM, SEM, *[HBM] * n, pl.BlockSpec(memory_space=pltpu.VMEM)),
        {t: 2 + t for t in range(n)})(*_hbm(bufs), *deps)
    return res[0], res[1], list(res[2:2 + n]), res[2 + n]


def _gather_forward(name, bufs, send, recv, deps):
    n, nd = len(bufs), len(deps)

    def body(*refs):
        ins, send_in, recv_in = refs[:n], refs[n], refs[n + 1]
        fsend, frecv = refs[n + 2 + nd], refs[n + 3 + nd]
        x, y, c, chips = _position()
        for j, chip in enumerate(chips):
            for t in range(n):
                blk = ins[t].at[_slot(*chip, c)]
                _remote(blk, blk, send_in.at[4 * t + 1 + j], recv_in.at[4 * t + 1 + j], (*chip, c)).wait_recv()
                _remote(blk, blk, fsend.at[3 * t + j], frecv.at[3 * t + j], (x, y, 1 - c)).start()

    res = _split_call(
        body, name,
        (pltpu.SemaphoreType.DMA((3 * n,)), pltpu.SemaphoreType.DMA((3 * n,)), *[pltpu.HBM(b.shape, b.dtype) for b in bufs]),
        [HBM] * n + [SEM, SEM] + [ANY] * nd, (SEM, SEM, *[HBM] * n),
        {t: 2 + t for t in range(n)})(*_hbm(bufs), send, recv, *deps)
    return res[0], res[1], list(res[2:])


def _gather_finish(name, bufs, send, recv, fsend, frecv):
    n = len(bufs)

    def body(*refs):
        ins, send_in, recv_in, fs_in, fr_in = refs[:n], refs[n], refs[n + 1], refs[n + 2], refs[n + 3]
        x, y, c, chips = _position()
        sibling = (x, y, 1 - c)
        peers = [sibling] + [(*chip, c) for chip in chips]
        for t in range(n):
            blk = ins[t].at[_slot(x, y, 1 - c)]
            _remote(blk, blk, send_in.at[4 * t], recv_in.at[4 * t], sibling).wait_recv()
            for j, chip in enumerate(chips):
                blk = ins[t].at[_slot(*chip, 1 - c)]
                _remote(blk, blk, fs_in.at[3 * t + j], fr_in.at[3 * t + j], sibling).wait_recv()
            mine = ins[t].at[_slot(x, y, c)]
            for k, peer in enumerate(peers):
                _remote(mine, mine, send_in.at[4 * t + k], recv_in.at[4 * t + k], peer).wait_send()
            for j, chip in enumerate(chips):
                blk = ins[t].at[_slot(*chip, c)]
                _remote(blk, blk, fs_in.at[3 * t + j], fr_in.at[3 * t + j], sibling).wait_send()

    res = _split_call(
        body, name, tuple(pltpu.HBM(b.shape, b.dtype) for b in bufs),
        [HBM] * n + [SEM] * 4, tuple([HBM] * n), {t: t for t in range(n)})(*_hbm(bufs), send, recv, fsend, frecv)
    return list(res)


def _halves(ref):
    rows = ref.shape[0] // 2
    return ref.at[pl.ds(0, rows)], ref.at[pl.ds(rows, rows)]


def _relay_start(name, bufs, deps):
    n, nd = len(bufs), len(deps)

    def body(*refs):
        ins, send, recv, token = refs[:n], refs[n + nd], refs[n + nd + 1], refs[2 * n + nd + 2]
        x, y, c, _ = _position()
        peers = [(x, y, 1 - c), (1 - x, y, c), (x, 1 - y, c)]
        for t in range(n):
            mine = ins[t].at[_slot(x, y, c)]
            for k, peer in enumerate(peers):
                _remote(mine, mine, send.at[3 * t + k], recv.at[3 * t + k], peer).start()
        token[...] = jnp.zeros_like(token)

    res = _split_call(
        body, name,
        (pltpu.SemaphoreType.DMA((3 * n,)), pltpu.SemaphoreType.DMA((3 * n,)), *[pltpu.HBM(b.shape, b.dtype) for b in bufs],
         _token_shape()),
        [HBM] * n + [ANY] * nd, (SEM, SEM, *[HBM] * n, pl.BlockSpec(memory_space=pltpu.VMEM)),
        {t: 2 + t for t in range(n)})(*_hbm(bufs), *deps)
    return res[0], res[1], list(res[2:2 + n]), res[2 + n]


def _relay_neighbors(name, bufs, send, recv, deps):
    n, nd = len(bufs), len(deps)

    def body(*refs):
        ins, send_in, recv_in = refs[:n], refs[n], refs[n + 1]
        fsend, frecv, token = refs[n + 2 + nd], refs[n + 3 + nd], refs[2 * n + 4 + nd]
        x, y, c, _ = _position()
        sibling, xn, yn = (x, y, 1 - c), (1 - x, y, c), (x, 1 - y, c)
        for t in range(n):
            blk = ins[t].at[_slot(*xn)]
            _remote(blk, blk, send_in.at[3 * t + 1], recv_in.at[3 * t + 1], xn).wait_recv()
            _remote(blk, blk, fsend.at[4 * t], frecv.at[4 * t], sibling).start()
            half = _halves(blk)[0]
            _remote(half, half, fsend.at[4 * t + 1], frecv.at[4 * t + 1], yn).start()
        for t in range(n):
            blk = ins[t].at[_slot(*yn)]
            _remote(blk, blk, send_in.at[3 * t + 2], recv_in.at[3 * t + 2], yn).wait_recv()
            _remote(blk, blk, fsend.at[4 * t + 2], frecv.at[4 * t + 2], sibling).start()
            half = _halves(blk)[1]
            _remote(half, half, fsend.at[4 * t + 3], frecv.at[4 * t + 3], xn).start()
        token[...] = jnp.zeros_like(token)

    res = _split_call(
        body, name,
        (pltpu.SemaphoreType.DMA((4 * n,)), pltpu.SemaphoreType.DMA((4 * n,)), *[pltpu.HBM(b.shape, b.dtype) for b in bufs],
         _token_shape()),
        [HBM] * n + [SEM, SEM] + [ANY] * nd, (SEM, SEM, *[HBM] * n, pl.BlockSpec(memory_space=pltpu.VMEM)),
        {t: 2 + t for t in range(n)})(*_hbm(bufs), send, recv, *deps)
    return res[0], res[1], list(res[2:2 + n]), res[2 + n]


def _relay_diagonal(name, bufs, fsend, frecv, deps):
    n, nd = len(bufs), len(deps)

    def body(*refs):
        ins, fs_in, fr_in = refs[:n], refs[n], refs[n + 1]
        gsend, grecv = refs[n + 2 + nd], refs[n + 3 + nd]
        x, y, c, _ = _position()
        for t in range(n):
            blk = ins[t].at[_slot(1 - x, 1 - y, c)]
            first, second = _halves(blk)
            _remote(first, first, fs_in.at[4 * t + 1], fr_in.at[4 * t + 1], (x, 1 - y, c)).wait_recv()
            _remote(second, second, fs_in.at[4 * t + 3], fr_in.at[4 * t + 3], (1 - x, y, c)).wait_recv()
            _remote(blk, blk, gsend.at[t], grecv.at[t], (x, y, 1 - c)).start()

    res = _split_call(
        body, name,
        (pltpu.SemaphoreType.DMA((n,)), pltpu.SemaphoreType.DMA((n,)), *[pltpu.HBM(b.shape, b.dtype) for b in bufs]),
        [HBM] * n + [SEM, SEM] + [ANY] * nd, (SEM, SEM, *[HBM] * n),
        {t: 2 + t for t in range(n)})(*_hbm(bufs), fsend, frecv, *deps)
    return res[0], res[1], list(res[2:])


def _relay_finish(name, bufs, send, recv, fsend, frecv, gsend, grecv):
    n = len(bufs)

    def body(*refs):
        ins = refs[:n]
        send_in, recv_in, fs_in, fr_in, gs_in, gr_in = refs[n:n + 6]
        x, y, c, _ = _position()
        sibling, xn, yn = (x, y, 1 - c), (1 - x, y, c), (x, 1 - y, c)
        for t in range(n):
            blk = ins[t].at[_slot(x, y, 1 - c)]
            _remote(blk, blk, send_in.at[3 * t], recv_in.at[3 * t], sibling).wait_recv()
            blk = ins[t].at[_slot(1 - x, y, 1 - c)]
            _remote(blk, blk, fs_in.at[4 * t], fr_in.at[4 * t], sibling).wait_recv()
            blk = ins[t].at[_slot(x, 1 - y, 1 - c)]
            _remote(blk, blk, fs_in.at[4 * t + 2], fr_in.at[4 * t + 2], sibling).wait_recv()
            blk = ins[t].at[_slot(1 - x, 1 - y, 1 - c)]
            _remote(blk, blk, gs_in.at[t], gr_in.at[t], sibling).wait_recv()
            mine = ins[t].at[_slot(x, y, c)]
            for k, peer in enumerate([sibling, xn, yn]):
                _remote(mine, mine, send_in.at[3 * t + k], recv_in.at[3 * t + k], peer).wait_send()
            bx, by = ins[t].at[_slot(*xn)], ins[t].at[_slot(*yn)]
            _remote(bx, bx, fs_in.at[4 * t], fr_in.at[4 * t], sibling).wait_send()
            _remote(_halves(bx)[0], _halves(bx)[0], fs_in.at[4 * t + 1], fr_in.at[4 * t + 1], yn).wait_send()
            _remote(by, by, fs_in.at[4 * t + 2], fr_in.at[4 * t + 2], sibling).wait_send()
            _remote(_halves(by)[1], _halves(by)[1], fs_in.at[4 * t + 3], fr_in.at[4 * t + 3], xn).wait_send()
            bd = ins[t].at[_slot(1 - x, 1 - y, c)]
            _remote(bd, bd, gs_in.at[t], gr_in.at[t], sibling).wait_send()

    res = _split_call(
        body, name, tuple(pltpu.HBM(b.shape, b.dtype) for b in bufs),
        [HBM] * n + [SEM] * 6, tuple([HBM] * n), {t: t for t in range(n)})(
            *_hbm(bufs), send, recv, fsend, frecv, gsend, grecv)
    return list(res)


WHOLE, SHARDS, SHARDS2 = 0, 1, 2


def _sibling_copies(srcs, lands, kinds, c):
    pairs = []
    for s_ref, l_ref, kind in zip(srcs, lands, kinds):
        if kind == WHOLE:
            pairs.append((s_ref, l_ref))
        elif kind == SHARDS:
            pairs += [(s_ref.at[2 * k + (1 - c)], l_ref.at[k]) for k in range(NCHIP)]
        else:
            pairs += [(s_ref.at[w, 2 * k + (1 - c)], l_ref.at[w, k]) for w in range(2) for k in range(NCHIP)]
    return pairs


def _count_copies(kinds):
    return sum({WHOLE: 1, SHARDS: NCHIP, SHARDS2: 2 * NCHIP}[k] for k in kinds)


def _sibling_start(name, srcs, lands, deps, kinds=None):
    n, nd = len(srcs), len(deps)
    kinds = kinds or [SHARDS] * n
    ncp = _count_copies(kinds)

    def body(*refs):
        s_in, l_in = refs[:n], refs[n:2 * n]
        send, recv, token = refs[2 * n + nd], refs[2 * n + nd + 1], refs[4 * n + nd + 2]
        x, y, c, _ = _position()
        for i, (src, dst) in enumerate(_sibling_copies(s_in, l_in, kinds, c)):
            _remote(src, dst, send.at[i], recv.at[i], (x, y, 1 - c)).start()
        token[...] = jnp.zeros_like(token)

    both = list(srcs) + list(lands)
    res = _split_call(
        body, name,
        (pltpu.SemaphoreType.DMA((ncp,)), pltpu.SemaphoreType.DMA((ncp,)),
         *[pltpu.HBM(b.shape, b.dtype) for b in both], _token_shape()),
        [HBM] * (2 * n) + [ANY] * nd, (SEM, SEM, *[HBM] * (2 * n), pl.BlockSpec(memory_space=pltpu.VMEM)),
        {t: 2 + t for t in range(2 * n)})(*_hbm(both), *deps)
    return res[0], res[1], list(res[2:2 + n]), list(res[2 + n:2 + 2 * n]), res[2 + 2 * n]


def _sibling_finish(name, srcs, lands, send, recv, deps, kinds=None):
    n, nd = len(srcs), len(deps)
    kinds = kinds or [SHARDS] * n

    def body(*refs):
        s_in, l_in, send_in, recv_in = refs[:n], refs[n:2 * n], refs[2 * n], refs[2 * n + 1]
        x, y, c, _ = _position()
        for i, (src, dst) in enumerate(_sibling_copies(s_in, l_in, kinds, c)):
            cp = _remote(src, dst, send_in.at[i], recv_in.at[i], (x, y, 1 - c))
            cp.wait_send()
            cp.wait_recv()

    both = list(srcs) + list(lands)
    res = _split_call(
        body, name, tuple(pltpu.HBM(b.shape, b.dtype) for b in both),
        [HBM] * (2 * n) + [SEM, SEM] + [ANY] * nd, tuple([HBM] * (2 * n)),
        {t: t for t in range(2 * n)})(*_hbm(both), send, recv, *deps)
    return list(res[:n]), list(res[n:])


def _chips_start(name, parts, lands, deps):
    n, nd = len(parts), len(deps)

    def body(*refs):
        p_in, l_in = refs[:n], refs[n:2 * n]
        send, recv, token = refs[2 * n + nd], refs[2 * n + nd + 1], refs[4 * n + nd + 2]
        x, y, c, chips = _position()
        for t in range(n):
            for j, chip in enumerate(chips):
                _remote(p_in[t].at[2 * chip[0] + chip[1]], l_in[t].at[2 * x + y], send.at[3 * t + j], recv.at[3 * t + j],
                        (*chip, c)).start()
        token[...] = jnp.zeros_like(token)

    both = list(parts) + list(lands)
    res = _split_call(
        body, name,
        (pltpu.SemaphoreType.DMA((3 * n,)), pltpu.SemaphoreType.DMA((3 * n,)), *[pltpu.HBM(b.shape, b.dtype) for b in both],
         _token_shape()),
        [HBM] * (2 * n) + [ANY] * nd, (SEM, SEM, *[HBM] * (2 * n), pl.BlockSpec(memory_space=pltpu.VMEM)),
        {t: 2 + t for t in range(2 * n)})(*_hbm(both), *deps)
    return res[0], res[1], list(res[2:2 + n]), list(res[2 + n:2 + 2 * n]), res[2 + 2 * n]


def _chips_finish(name, parts, lands, send, recv, deps):
    n, nd = len(parts), len(deps)

    def body(*refs):
        p_in, l_in, send_in, recv_in = refs[:n], refs[n:2 * n], refs[2 * n], refs[2 * n + 1]
        x, y, c, chips = _position()
        for t in range(n):
            for j, chip in enumerate(chips):
                k = 2 * chip[0] + chip[1]
                _remote(p_in[t].at[k], l_in[t].at[k], send_in.at[3 * t + j], recv_in.at[3 * t + j], (*chip, c)).wait_recv()
                _remote(p_in[t].at[k], l_in[t].at[2 * x + y], send_in.at[3 * t + j], recv_in.at[3 * t + j],
                        (*chip, c)).wait_send()

    both = list(parts) + list(lands)
    res = _split_call(
        body, name, tuple(pltpu.HBM(b.shape, b.dtype) for b in both),
        [HBM] * (2 * n) + [SEM, SEM] + [ANY] * nd, tuple([HBM] * (2 * n)),
        {t: t for t in range(2 * n)})(*_hbm(both), send, recv, *deps)
    return list(res[:n]), list(res[n:])


def _pair_sum(name, grad, recv, where):
    _, r, w = grad.shape
    tr = _row_tile(r, w, budget=4 * 1024 * 1024)
    g4 = grad.reshape(NCHIP, 2, r, w)

    def body(where_ref, g_ref, r_ref, o_ref, own_ref):
        val = (g_ref[...].astype(F32) + r_ref[...].astype(F32)).astype(o_ref.dtype)
        o_ref[...] = val

        @pl.when(pl.program_id(1) == where_ref[1])
        def _():
            own_ref[...] = val

    out = _sds((NCHIP, r, w), grad.dtype)
    return _pcall(
        body, name=name, out_shape=[out, out], grid=(r // tr, NCHIP), prefetch=1,
        in_specs=[pl.BlockSpec((None, None, tr, w), lambda i, k, wr: (k, wr[0], i, 0)),
                  pl.BlockSpec((None, tr, w), lambda i, k, wr: (k, i, 0))],
        out_specs=[pl.BlockSpec((None, tr, w), lambda i, k, wr: (k, i, 0)),
                   pl.BlockSpec((None, tr, w), lambda i, k, wr: (wr[1], i, 0))],
    )(where, g4, recv)


def _row_tile(rows, width, budget=2 * 1024 * 1024):
    best = None
    for t in range(16, rows + 1, 16):
        if rows % t == 0 and t * width * 4 <= budget:
            best = t
    if best is None and rows * width * 4 <= budget:
        best = rows
    assert best is not None, (rows, width)
    return best


def _gemm(name, grid, operands, prods, extras, outs, epilogue, *, nk=1, acc_shape=None, aliases=None, separate=False,
          deps=(), prefetch=()):
    n_op, n_ex, n_out = len(operands), len(extras), len(outs)

    def body(*refs):
        refs = refs[len(prefetch):]
        ops, ex, out_refs = refs[:n_op], refs[n_op:n_op + n_ex], refs[n_op + n_ex:n_op + n_ex + n_out]
        parts = []
        for pr in prods:
            a, b = ops[pr[0]], ops[pr[1]]
            av = pr[3](a) if len(pr) > 3 and pr[3] else a[...]
            bv = pr[4](b) if len(pr) > 4 and pr[4] else b[...]
            parts.append(lax.dot_general(av, bv, pr[2], preferred_element_type=F32))
        if separate:
            epilogue(parts, ex, out_refs)
            return
        part = parts[0]
        for p in parts[1:]:
            part = part + p
        if nk == 1:
            epilogue(part, ex, out_refs)
        else:
            acc = refs[-1]
            k = pl.program_id(len(grid) - 1)

            @pl.when(k == 0)
            def _():
                acc[...] = part

            @pl.when(k > 0)
            def _():
                acc[...] += part

            @pl.when(k == nk - 1)
            def _():
                epilogue(acc[...], ex, out_refs)

    res = _pcall(
        body, name=name, out_shape=[o[0] for o in outs], grid=grid,
        in_specs=[o[1] for o in operands] + [e[1] for e in extras], out_specs=[o[1] for o in outs],
        scratch=[pltpu.VMEM(acc_shape, F32)] if nk > 1 else [], aliases=aliases, deps=deps, prefetch=len(prefetch),
    )(*prefetch, *[o[0] for o in operands], *[e[0] for e in extras])
    return list(res)


def _store(acc, ex, outs):
    outs[0][...] = acc.astype(outs[0].dtype)


def _store_add_extra(acc, ex, outs):
    v = acc
    for e in ex:
        v = v + e[...]
    outs[0][...] = v.astype(outs[0].dtype)


def _stacked(ref):
    b = ref[...]
    return b.reshape(b.shape[0] * b.shape[1], b.shape[2])


def _pick(c):
    return lambda ref: ref[c]


def _cols(c, width):
    return lambda ref: ref[:, c * width:(c + 1) * width]


def _gelu_parts(z):
    c = math.sqrt(2.0 / math.pi)
    t = jnp.tanh(c * (z + 0.044715 * (z * z * z)))
    val = 0.5 * z * (1.0 + t)
    grad = 0.5 * (1.0 + t) + 0.5 * z * (1.0 - t * t) * (c * (1.0 + 3.0 * 0.044715 * z * z))
    return val, grad


def _rms_fwd(name, h, g, deps=()):
    s, d = h.shape
    tr = _row_tile(s, d)

    def body(h_ref, g_ref, o_ref):
        hv = h_ref[...]
        r = lax.rsqrt(jnp.mean(hv * hv, axis=-1, keepdims=True) + RMS_EPS)
        o_ref[...] = (hv * r * g_ref[...]).astype(o_ref.dtype)

    return _pcall(
        body, name=name, out_shape=_sds((s, d), BF16), grid=(s // tr,),
        in_specs=[pl.BlockSpec((tr, d), lambda i: (i, 0)), pl.BlockSpec((1, d), lambda i: (0, 0))],
        out_specs=pl.BlockSpec((tr, d), lambda i: (i, 0)), deps=deps,
    )(h, g.reshape(1, d))


def _accumulate(ref, val, first):
    @pl.when(first)
    def _():
        ref[...] = val

    @pl.when(jnp.logical_not(first))
    def _():
        ref[...] += val


def _rms_bwd(name, h, g, dy, res, deps=()):
    s, d = h.shape
    tr = _row_tile(s, d, budget=2 * 1024 * 1024)

    def body(h_ref, g_ref, dy_ref, res_ref, dh_ref, dhb_ref, dg_ref, cs_ref):
        hv = h_ref[...]
        r = lax.rsqrt(jnp.mean(hv * hv, axis=-1, keepdims=True) + RMS_EPS)
        xhat = hv * r
        dyv = dy_ref[...]
        dxh = dyv * g_ref[...]
        dh = res_ref[...] + r * (dxh - xhat * jnp.mean(dxh * xhat, axis=-1, keepdims=True))
        dh_ref[...] = dh
        dhb_ref[...] = dh.astype(BF16)
        first = pl.program_id(0) == 0
        _accumulate(dg_ref, jnp.sum(dyv * xhat, axis=0, keepdims=True), first)
        _accumulate(cs_ref, jnp.sum(dh, axis=0, keepdims=True), first)

    row = pl.BlockSpec((tr, d), lambda i: (i, 0))
    vec = pl.BlockSpec((1, d), lambda i: (0, 0))
    return _pcall(
        body, name=name, out_shape=[_sds((s, d), F32), _sds((s, d), BF16), _sds((1, d), F32), _sds((1, d), F32)],
        grid=(s // tr,), in_specs=[row, vec, row, row], out_specs=[row, row, vec, vec], deps=deps,
    )(h, g.reshape(1, d), dy, res)


def _loss_bwd(name, h, g, target):
    s, d = h.shape
    tr = _row_tile(s, d, budget=1024 * 1024)

    def body(h_ref, g_ref, t_ref, loss_ref, dh_ref, dhb_ref, dg_ref):
        hv = h_ref[...]
        r = lax.rsqrt(jnp.mean(hv * hv, axis=-1, keepdims=True) + RMS_EPS)
        xhat = hv * r
        diff = xhat * g_ref[...] - t_ref[...]
        part = jnp.sum(jnp.sum(diff * diff, axis=1, keepdims=True), axis=0, keepdims=True) * (0.5 / d)
        dyv = diff * (1.0 / d)
        dxh = dyv * g_ref[...]
        dh = r * (dxh - xhat * jnp.mean(dxh * xhat, axis=-1, keepdims=True))
        dh_ref[...] = dh
        dhb_ref[...] = dh.astype(BF16)
        first = pl.program_id(0) == 0
        _accumulate(loss_ref, part, first)
        _accumulate(dg_ref, jnp.sum(dyv * xhat, axis=0, keepdims=True), first)

    row = pl.BlockSpec((tr, d), lambda i: (i, 0))
    vec = pl.BlockSpec((1, d), lambda i: (0, 0))
    one = pl.BlockSpec((1, 1), lambda i: (0, 0))
    return _pcall(
        body, name=name, out_shape=[_sds((1, 1), F32), _sds((s, d), F32), _sds((s, d), BF16), _sds((1, d), F32)],
        grid=(s // tr,), in_specs=[row, vec, row], out_specs=[one, row, row, vec],
    )(h, g.reshape(1, d), target)


def _tril_mask():
    return lax.broadcasted_iota(jnp.int32, (CHUNK, CHUNK), 0) >= lax.broadcasted_iota(jnp.int32, (CHUNK, CHUNK), 1)


def _gmlp_fwd(name, zp, gv, ws, bst):
    s, d2 = zp.shape
    d = d2 // 2
    gw = d // GROUPS

    def body(zp_ref, gv_ref, ws_ref, bst_ref, o_ref):
        u, _ = _gelu_parts(zp_ref[:, :d])
        v, _ = _gelu_parts(zp_ref[:, d:])
        rv = lax.rsqrt(jnp.mean(v * v, axis=-1, keepdims=True) + RMS_EPS)
        vn = (v * rv * gv_ref[...]).astype(BF16)
        tril = _tril_mask()
        for g in range(GROUPS):
            sl = slice(g * gw, (g + 1) * gw)
            wc = jnp.where(tril, ws_ref[g], 0.0).astype(BF16)
            sg = jnp.dot(wc, vn[:, sl], preferred_element_type=F32) + bst_ref[:, g:g + 1]
            o_ref[:, sl] = (u[:, sl] * sg).astype(o_ref.dtype)

    return _pcall(
        body, name=name, out_shape=_sds((s, d), BF16), grid=(s // CHUNK,),
        in_specs=[pl.BlockSpec((CHUNK, d2), lambda i: (i, 0)), pl.BlockSpec((1, d), lambda i: (0, 0)),
                  pl.BlockSpec((GROUPS, CHUNK, CHUNK), lambda i: (0, 0, 0)),
                  pl.BlockSpec((CHUNK, GROUPS), lambda i: (0, 0))],
        out_specs=pl.BlockSpec((CHUNK, d), lambda i: (i, 0)),
    )(zp, gv, ws, bst)


def _gmlp_bwd(name, zp, dgated, gv, ws, bst):
    s, d2 = zp.shape
    d = d2 // 2
    gw = d // GROUPS

    def body(zp_ref, dg_ref, gv_ref, ws_ref, bst_ref, dzp_ref, dws_ref, dbs_ref, dgv_ref, dvn_ref):
        u, gu = _gelu_parts(zp_ref[:, :d])
        v, gvv = _gelu_parts(zp_ref[:, d:])
        rv = lax.rsqrt(jnp.mean(v * v, axis=-1, keepdims=True) + RMS_EPS)
        vhat = v * rv
        vn = (vhat * gv_ref[...]).astype(BF16)
        tril = _tril_mask()
        first = pl.program_id(0) == 0
        ones = jnp.ones((8, gw), F32)

        @pl.when(first)
        def _():
            dws_ref[...] = jnp.zeros_like(dws_ref)
            dbs_ref[...] = jnp.zeros_like(dbs_ref)

        for g in range(GROUPS):
            sl = slice(g * gw, (g + 1) * gw)
            wc = jnp.where(tril, ws_ref[g], 0.0).astype(BF16)
            sg = jnp.dot(wc, vn[:, sl], preferred_element_type=F32) + bst_ref[:, g:g + 1]
            dgs = dg_ref[:, sl]
            ds = dgs * u[:, sl]
            dsb = ds.astype(BF16)
            dzp_ref[:, sl] = (dgs * sg * gu[:, sl]).astype(dzp_ref.dtype)
            dvn_ref[:, sl] = lax.dot_general(wc, dsb, TN, preferred_element_type=F32)
            dw = lax.dot_general(dsb, vn[:, sl], NT, preferred_element_type=F32)
            dws_ref[g] += jnp.where(tril, dw, 0.0)
            dbs_ref[g] += lax.dot_general(ones, ds, NT, preferred_element_type=F32, precision=lax.Precision.HIGHEST)
        dvn = dvn_ref[...]
        dvh = dvn * gv_ref[...]
        dv = rv * (dvh - vhat * jnp.mean(dvh * vhat, axis=-1, keepdims=True))
        dzp_ref[:, d:] = (dv * gvv).astype(dzp_ref.dtype)
        _accumulate(dgv_ref, jnp.sum(dvn * vhat, axis=0, keepdims=True), first)

    return _pcall(
        body, name=name,
        out_shape=[_sds((s, d2), BF16), _sds((GROUPS, CHUNK, CHUNK), F32), _sds((GROUPS, 8, CHUNK), F32),
                   _sds((1, d), F32)],
        grid=(s // CHUNK,),
        in_specs=[pl.BlockSpec((CHUNK, d2), lambda i: (i, 0)), pl.BlockSpec((CHUNK, d), lambda i: (i, 0)),
                  pl.BlockSpec((1, d), lambda i: (0, 0)), pl.BlockSpec((GROUPS, CHUNK, CHUNK), lambda i: (0, 0, 0)),
                  pl.BlockSpec((CHUNK, GROUPS), lambda i: (0, 0))],
        out_specs=[pl.BlockSpec((CHUNK, d2), lambda i: (i, 0)),
                   pl.BlockSpec((GROUPS, CHUNK, CHUNK), lambda i: (0, 0, 0)),
                   pl.BlockSpec((GROUPS, 8, CHUNK), lambda i: (0, 0, 0)), pl.BlockSpec((1, d), lambda i: (0, 0))],
        scratch=[pltpu.VMEM((CHUNK, d), F32)],
    )(zp, dgated, gv, ws, bst)


def _bucket_table():
    dist = np.arange(BLOCK)[:, None] + BLOCK - np.arange(2 * BLOCK)[None, :]
    in_window = (dist >= 0) & (dist < BLOCK)
    dd = np.clip(dist, 0, None)
    max_exact = N_BUCKETS // 2
    dl = np.maximum(dd, 1).astype(np.float32)
    large = max_exact + (np.log(dl / np.float32(max_exact)) / np.float32(math.log(MAX_DISTANCE / max_exact))
                         * np.float32(N_BUCKETS - max_exact)).astype(np.int32)
    large = np.minimum(large, N_BUCKETS - 1)
    bucket = np.where(dd < max_exact, dd, large)
    return np.where(in_window, bucket, -1).astype(np.int32).reshape(1, -1)


def _bias_table(name, rel_bias_t, buckets):
    nh = rel_bias_t.shape[0]
    p = buckets.shape[1]
    tp = 4096

    def body(rb_ref, bk_ref, o_ref):
        bk = bk_ref[...]
        onehot = (lax.broadcasted_iota(jnp.int32, (N_BUCKETS, tp), 0) == bk).astype(F32)
        val = jnp.dot(rb_ref[...], onehot, preferred_element_type=F32, precision=lax.Precision.HIGHEST)
        o_ref[...] = jnp.where(bk >= 0, val, NEG_INF)

    return _pcall(
        body, name=name, out_shape=_sds((nh, p), F32), grid=(p // tp,),
        in_specs=[pl.BlockSpec((nh, N_BUCKETS), lambda i: (0, 0)), pl.BlockSpec((1, tp), lambda i: (0, i))],
        out_specs=pl.BlockSpec((nh, tp), lambda i: (0, i)),
    )(rel_bias_t, buckets)


def _bias_grad(name, dbiases, buckets):
    nh, p = dbiases[0].shape
    n = len(dbiases)
    tp = 4096

    def body(*refs):
        bk_ref, o_ref = refs[n], refs[n + 1]
        onehot = (lax.broadcasted_iota(jnp.int32, (N_BUCKETS, tp), 0) == bk_ref[...]).astype(F32)
        db = refs[0][...]
        for r in refs[1:n]:
            db = db + r[...]
        part = lax.dot_general(onehot, db, NT, preferred_element_type=F32, precision=lax.Precision.HIGHEST)
        _accumulate(o_ref, part, pl.program_id(0) == 0)

    return _pcall(
        body, name=name, out_shape=_sds((N_BUCKETS, nh), F32), grid=(p // tp,),
        in_specs=[pl.BlockSpec((nh, tp), lambda i: (0, i))] * n + [pl.BlockSpec((1, tp), lambda i: (0, i))],
        out_specs=pl.BlockSpec((N_BUCKETS, nh), lambda i: (0, 0)),
    )(*dbiases, buckets)


def _stack_heads(ref, g):
    base = g * KV_GROUP * HEAD_DIM
    return jnp.concatenate([ref[:, base + hh * HEAD_DIM:base + (hh + 1) * HEAD_DIM] for hh in range(KV_GROUP)], axis=0)


def _attn_probs(q, kb, bias, s_ref, first_head):
    penalty = jnp.where(pl.program_id(1) > 0, 0.0, NEG_INF).astype(F32)
    col = lax.broadcasted_iota(jnp.int32, (1, 2 * BLOCK), 1)
    bias = bias.reshape(KV_GROUP * BLOCK, 2 * BLOCK) + jnp.where(col < BLOCK, penalty, 0.0)
    sink = jnp.concatenate([jnp.full((BLOCK, 1), s_ref[first_head + hh], F32) for hh in range(KV_GROUP)], axis=0)
    s = lax.dot_general(q, kb, NT, preferred_element_type=F32) * 0.125 + bias
    m = jnp.maximum(jnp.max(s, axis=-1, keepdims=True), sink)
    p = jnp.exp(s - m)
    es = jnp.exp(sink - m)
    inv = 1.0 / (jnp.sum(p, axis=-1, keepdims=True) + es)
    return p * inv, es * inv


def _attn_specs(ng):
    gq = ng * KV_GROUP * HEAD_DIM
    q_spec = pl.BlockSpec((BLOCK, gq), lambda kh, i: (i, kh))
    prev = pl.BlockSpec((ng, BLOCK, HEAD_DIM), lambda kh, i: (kh, jnp.maximum(i - 1, 0), 0))
    cur = pl.BlockSpec((ng, BLOCK, HEAD_DIM), lambda kh, i: (kh, i, 0))
    bias = pl.BlockSpec((ng * KV_GROUP, BLOCK, 2 * BLOCK), lambda kh, i: (kh, 0, 0))
    smem = pl.BlockSpec(memory_space=pltpu.SMEM)
    probs = pl.BlockSpec((ng, None, KV_GROUP * BLOCK, 2 * BLOCK), lambda kh, i: (kh, i, 0, 0))
    sink_probs = pl.BlockSpec((ng, None, KV_GROUP * BLOCK, 1), lambda kh, i: (kh, i, 0, 0))
    return q_spec, prev, cur, bias, smem, probs, sink_probs


def _kv_heads_per_step(nkv):
    return 2 if nkv % 2 == 0 else 1


def _attn_fwd(name, q, k, v, bias, sinks, deps=()):
    s, dq = q.shape
    nkv = k.shape[0]
    ng = 1
    q_spec, prev, cur, bias_spec, smem, p_spec, ps_spec = _attn_specs(ng)

    def body(q_ref, kp_ref, kc_ref, vp_ref, vc_ref, b_ref, s_ref, o_ref, p_ref, ps_ref):
        for g in range(ng):
            kb = jnp.concatenate([kp_ref[g], kc_ref[g]], axis=0)
            vb = jnp.concatenate([vp_ref[g], vc_ref[g]], axis=0)
            p, ps = _attn_probs(_stack_heads(q_ref, g), kb, b_ref[g * KV_GROUP:(g + 1) * KV_GROUP], s_ref,
                                (pl.program_id(0) * ng + g) * KV_GROUP)
            pb = p.astype(BF16)
            p_ref[g] = pb
            ps_ref[g] = ps
            o = jnp.dot(pb, vb, preferred_element_type=F32)
            for hh in range(KV_GROUP):
                col = (g * KV_GROUP + hh) * HEAD_DIM
                o_ref[:, col:col + HEAD_DIM] = o[hh * BLOCK:(hh + 1) * BLOCK].astype(o_ref.dtype)

    return _pcall(
        body, name=name,
        out_shape=[_sds((s, dq), BF16), _sds((nkv, s // BLOCK, KV_GROUP * BLOCK, 2 * BLOCK), BF16),
                   _sds((nkv, s // BLOCK, KV_GROUP * BLOCK, 1), F32)],
        grid=(nkv // ng, s // BLOCK),
        in_specs=[q_spec, prev, cur, prev, cur, bias_spec, smem], out_specs=[q_spec, p_spec, ps_spec], deps=deps,
    )(q, k, k, v, v, bias, sinks)


def _attn_bwd(name, q, k, v, do, probs, sink_probs):
    s, dq = q.shape
    nkv = k.shape[0]
    ng = _kv_heads_per_step(nkv)
    gq = ng * KV_GROUP * HEAD_DIM
    q_spec, prev, cur, bias_spec, _, p_spec, ps_spec = _attn_specs(ng)

    def body(q_ref, do_ref, kp_ref, kc_ref, vp_ref, vc_ref, p_ref, ps_ref,
             dq_ref, dbq_ref, dkc_ref, dkp_ref, dvc_ref, dvp_ref, dbias_ref, dsink_ref):
        @pl.when(pl.program_id(1) == 0)
        def _():
            dbias_ref[...] = jnp.zeros_like(dbias_ref)
            dsink_ref[...] = jnp.zeros_like(dsink_ref)
            dbq_ref[...] = jnp.zeros_like(dbq_ref)

        for g in range(ng):
            kb = jnp.concatenate([kp_ref[g], kc_ref[g]], axis=0)
            vb = jnp.concatenate([vp_ref[g], vc_ref[g]], axis=0)
            q, do = _stack_heads(q_ref, g), _stack_heads(do_ref, g)
            pb = p_ref[g]
            p = pb.astype(F32)
            dp = lax.dot_general(do, vb, NT, preferred_element_type=F32)
            delta = jnp.sum(p * dp, axis=-1, keepdims=True)
            ds = p * (dp - delta)
            dsb = ds.astype(BF16)
            dq = jnp.dot(dsb, kb, preferred_element_type=F32) * 0.125
            dsk = -(ps_ref[g] * delta)
            for hh in range(KV_GROUP):
                col, rows = (g * KV_GROUP + hh) * HEAD_DIM, slice(hh * BLOCK, (hh + 1) * BLOCK)
                dq_ref[:, col:col + HEAD_DIM] = dq[rows].astype(dq_ref.dtype)
                dbq_ref[:, col:col + HEAD_DIM] += jnp.sum(dq[rows], axis=0, keepdims=True)
                dsink_ref[g, :, hh:hh + 1] += jnp.sum(dsk[rows], axis=0, keepdims=True)
            dkb = lax.dot_general(dsb, q, TN, preferred_element_type=F32) * 0.125
            dvb = lax.dot_general(pb, do, TN, preferred_element_type=F32)
            dkp_ref[g], dkc_ref[g] = dkb[:BLOCK], dkb[BLOCK:]
            dvp_ref[g], dvc_ref[g] = dvb[:BLOCK], dvb[BLOCK:]
            dbias_ref[g * KV_GROUP:(g + 1) * KV_GROUP] += ds.reshape(KV_GROUP, BLOCK, 2 * BLOCK)

    kv_out = _sds((nkv, s, HEAD_DIM), F32)
    return _pcall(
        body, name=name,
        out_shape=[_sds((s, dq), BF16), _sds((1, dq), F32), kv_out, kv_out, kv_out, kv_out,
                   _sds((nkv * KV_GROUP, BLOCK, 2 * BLOCK), F32), _sds((nkv, 1, KV_GROUP), F32)],
        grid=(nkv // ng, s // BLOCK),
        in_specs=[q_spec, q_spec, prev, cur, prev, cur, p_spec, ps_spec],
        out_specs=[q_spec, pl.BlockSpec((1, gq), lambda kh, i: (0, kh)), cur, cur, cur, cur, bias_spec,
                   pl.BlockSpec((ng, 1, KV_GROUP), lambda kh, i: (kh, 0, 0))],
    )(q, do, k, k, v, v, probs, sink_probs)


def _kv_grad(name, parts):
    nkv, s, _ = parts[0][0].shape
    nb = s // BLOCK
    w = 2 * nkv * HEAD_DIM
    n = len(parts)

    def body(*refs):
        o_ref, cs_ref = refs[4 * n], refs[4 * n + 1]
        i = pl.program_id(0)
        keep = jnp.where(i < nb - 1, 1.0, 0.0).astype(F32)

        @pl.when(i == 0)
        def _():
            cs_ref[...] = jnp.zeros_like(cs_ref)

        for which in range(2):
            for hh in range(nkv):
                val = None
                for l in range(n):
                    cur_ref, nxt_ref = refs[4 * l + 2 * which], refs[4 * l + 2 * which + 1]
                    t = cur_ref[hh] + keep * nxt_ref[hh]
                    val = t if val is None else val + t
                sl = slice((which * nkv + hh) * HEAD_DIM, (which * nkv + hh + 1) * HEAD_DIM)
                o_ref[:, sl] = val.astype(o_ref.dtype)
                cs_ref[:, sl] += jnp.sum(val, axis=0, keepdims=True)

    cur = pl.BlockSpec((nkv, BLOCK, HEAD_DIM), lambda i: (0, i, 0))
    nxt = pl.BlockSpec((nkv, BLOCK, HEAD_DIM), lambda i: (0, jnp.minimum(i + 1, nb - 1), 0))
    flat = [a for p in parts for a in p]
    return _pcall(
        body, name=name, out_shape=[_sds((s, w), BF16), _sds((1, w), F32)], grid=(nb,),
        in_specs=[cur, nxt] * (2 * n),
        out_specs=[pl.BlockSpec((BLOCK, w), lambda i: (i, 0)), pl.BlockSpec((1, w), lambda i: (0, 0))],
    )(*flat)


def _adamw_math(w, g, m, v):
    m = ADAM_B1 * m + (1.0 - ADAM_B1) * g
    v = ADAM_B2 * v + (1.0 - ADAM_B2) * (g * g)
    m_hat = m / (1.0 - ADAM_B1 ** ADAM_STEP)
    v_hat = v / (1.0 - ADAM_B2 ** ADAM_STEP)
    delta = -ADAM_LR * (m_hat / (jnp.sqrt(v_hat) + ADAM_EPS) + ADAM_WD * w)
    return delta, m, v


def _adamw_shard(name, w, m, v, parts, row0, layer, prev, deps=(), own=None, where=None):
    _, r, wd = w.shape
    tr = _row_tile(r, wd, budget=3 * 512 * 1024)
    assert row0 % tr == 0

    def step(w_ref, m_ref, v_ref, g, g_ref, d_ref, nm_ref, nv_ref):
        delta, nm, nv = _adamw_math(w_ref[...], g, m_ref[...], v_ref[...])
        g_ref[...], d_ref[...], nm_ref[...], nv_ref[...] = g, delta, nm, nv

    out = _sds(w.shape, F32)
    if own is None:
        def body(w_ref, m_ref, v_ref, p_ref, a0, a1, a2, a3, g_ref, d_ref, nm_ref, nv_ref):
            g = p_ref[0].astype(F32)
            for k in range(1, NCHIP):
                g = g + p_ref[k].astype(F32)
            step(w_ref, m_ref, v_ref, g, g_ref, d_ref, nm_ref, nv_ref)

        par = pl.BlockSpec((None, tr, wd), lambda i: (layer, i, 0))
        return _pcall(
            body, name=name, out_shape=[out, out, out, out], grid=(r // tr,),
            in_specs=[par, par, par, pl.BlockSpec((NCHIP, tr, wd), lambda i: (0, row0 // tr + i, 0)), ANY, ANY, ANY, ANY],
            out_specs=[par, par, par, par], aliases={4: 0, 5: 1, 6: 2, 7: 3}, deps=deps,
        )(w, m, v, parts, *prev)

    def body(where_ref, w_ref, m_ref, v_ref, p_ref, o_ref, a0, a1, a2, a3, g_ref, d_ref, nm_ref, nv_ref):
        mine = lax.broadcasted_iota(jnp.int32, (tr, wd), 0) * 0 + where_ref[1]
        g = None
        for k in range(NCHIP):
            t = jnp.where(mine == k, o_ref[...], p_ref[k]).astype(F32)
            g = t if g is None else g + t
        step(w_ref, m_ref, v_ref, g, g_ref, d_ref, nm_ref, nv_ref)

    par = pl.BlockSpec((None, tr, wd), lambda i, wr: (layer, i, 0))
    return _pcall(
        body, name=name, out_shape=[out, out, out, out], grid=(r // tr,), prefetch=1,
        in_specs=[par, par, par, pl.BlockSpec((NCHIP, tr, wd), lambda i, wr: (0, row0 // tr + i, 0)),
                  pl.BlockSpec((None, tr, wd), lambda i, wr: (wr[1], row0 // tr + i, 0)), ANY, ANY, ANY, ANY],
        out_specs=[par, par, par, par], aliases={6: 0, 7: 1, 8: 2, 9: 3}, deps=deps,
    )(where, w, m, v, parts, own, *prev)


def _sum_devices(name, gathered):
    _, r, wd = gathered.shape

    def body(g_ref, o_ref):
        acc = g_ref[0]
        for k in range(1, NDEV):
            acc = acc + g_ref[k]
        o_ref[...] = acc

    return _pcall(body, name=name, out_shape=_sds((r, wd), F32), grid=(1,),
                  in_specs=[pl.BlockSpec((NDEV, r, wd), lambda i: (0, 0, 0))],
                  out_specs=pl.BlockSpec((r, wd), lambda i: (0, 0)))(gathered)


def _adamw_flat(name, w, g, m, v):
    shape = w.shape

    def body(w_ref, g_ref, m_ref, v_ref, d_ref, nm_ref, nv_ref):
        d_ref[...], nm_ref[...], nv_ref[...] = _adamw_math(w_ref[...], g_ref[...], m_ref[...], v_ref[...])

    spec = pl.BlockSpec(shape, lambda i: (0, 0))
    out = _sds(shape, F32)
    return _pcall(body, name=name, out_shape=[out, out, out], grid=(1,), in_specs=[spec] * 4,
                  out_specs=[spec] * 3)(w, g, m, v)


def _cast_into(name, src, layer, buf, row0, me):
    _, r, wd = src.shape
    tr = _row_tile(r, wd)
    assert row0 % tr == 0

    def body(me_ref, s_ref, b_ref, o_ref):
        o_ref[...] = s_ref[...].astype(o_ref.dtype)

    return _pcall(
        body, name=name, out_shape=_sds(buf.shape, buf.dtype), grid=(r // tr,), prefetch=1,
        in_specs=[pl.BlockSpec((None, tr, wd), lambda i, mr: (layer, i, 0)), ANY],
        out_specs=pl.BlockSpec((None, tr, wd), lambda i, mr: (mr[0], row0 // tr + i, 0)), aliases={2: 0},
    )(me, src, buf)


def _pack(arrays):
    rows = []
    for a in arrays:
        flat = a.reshape(-1).astype(F32)
        pad = (-flat.shape[0]) % 1024
        rows.append(jnp.pad(flat, (0, pad)).reshape(-1, 128))
    return jnp.concatenate(rows, axis=0)


def _unpack(packed, shapes):
    out, r = [], 0
    for shp in shapes:
        n = int(np.prod(shp))
        nr = (n + 1023) // 1024 * 8
        out.append(packed[r:r + nr].reshape(-1)[:n].reshape(shp))
        r += nr
    return out


def kernel(x, mix_norm, ffn_norm, a_w_in, a_norm_v, a_w_s, a_b_s, a_w_out, kv_norm, w_kv, b_kv, b_w_q, b_b_q, b_sinks, b_w_o, b_b_o, rel_bias, ffn_w_gate, ffn_w_up, ffn_w_down, final_norm, loss_target, m_mix_norm, m_ffn_norm, m_a_w_in, m_a_norm_v, m_a_w_s, m_a_b_s, m_a_w_out, m_kv_norm, m_w_kv, m_b_kv, m_b_w_q, m_b_b_q, m_b_sinks, m_b_w_o, m_b_b_o, m_rel_bias, m_ffn_w_gate, m_ffn_w_up, m_ffn_w_down, m_final_norm, v_mix_norm, v_ffn_norm, v_a_w_in, v_a_norm_v, v_a_w_s, v_a_b_s, v_a_w_out, v_kv_norm, v_w_kv, v_b_kv, v_b_w_q, v_b_b_q, v_b_sinks, v_b_w_o, v_b_b_o, v_rel_bias, v_ffn_w_gate, v_ffn_w_up, v_ffn_w_down, v_final_norm):
    _, S, D = x.shape
    LA, LB, L = a_w_in.shape[0], b_w_q.shape[0], ffn_w_gate.shape[0]
    F = ffn_w_gate.shape[2]
    DS = D // NDEV
    ZC = a_w_in.shape[2]
    KVW = w_kv.shape[1]
    NKV = KVW // (2 * HEAD_DIM)
    NH = D // HEAD_DIM
    assert ZC * NDEV == 2 * D and NH == NKV * KV_GROUP and S % BLOCK == 0
    TM = min(1024, S)
    TN_ = min(1024, D)
    TS = min(512, D)
    KC = 4

    ix, iy, ic = lax.axis_index("x"), lax.axis_index("y"), lax.axis_index("c")
    me = (4 * ix + 2 * iy + ic).astype(jnp.int32)
    me1 = me.reshape(1)
    where = jnp.stack([ic, 2 * ix + iy]).astype(jnp.int32)

    def tr3(a):
        return jnp.transpose(a, (0, 2, 1))

    gate_t, up_t = tr3(ffn_w_gate), tr3(ffn_w_up)
    w_kv3 = w_kv.reshape((1,) + w_kv.shape)

    def layer_arrays(l):
        arrs = [("gu", 2 * F, D, [(gate_t, l, 0), (up_t, l, F)]), ("down", F, D, [(ffn_w_down, l, 0)])]
        if l < LA:
            arrs += [("win", D, ZC, [(a_w_in, l, 0)]), ("wout", DS, D, [(a_w_out, l, 0)])]
            if l == LA - 1:
                arrs.append(("wkv", DS, KVW, [(w_kv3, 0, 0)]))
        else:
            i_b = l - LA
            arrs.append(("wqo", 2 * DS, D, [(b_w_q, i_b, 0), (b_w_o, i_b, DS)]))
        return arrs

    gathers = []

    def gather_begin(g_idx, deps):
        g = gathers[g_idx]
        g["send"], g["recv"], g["bufs"], g["token"] = _relay_start(f"relay_start{g_idx}", g["bufs"], deps)

    for l in range(L):
        mixer, ffn = dict(keys=[], bufs=[]), dict(keys=[], bufs=[])
        for key, rows, width, sources in layer_arrays(l):
            buf = lax.empty((NDEV, rows, width), BF16)
            for si, (src, li, row0) in enumerate(sources):
                buf = _cast_into(f"cast_{key}{l}_{si}", src, li, buf, row0, me1)
            group = ffn if key in ("gu", "down") else mixer
            group["keys"].append(key)
            group["bufs"].append(buf)
        gathers += [mixer, ffn]
        if l == 0:
            nv_rows = _pack([a_norm_v])
            nv = _cast_into("put_norm_v", nv_rows.reshape((1,) + nv_rows.shape), 0,
                            lax.empty((NDEV,) + nv_rows.shape, F32), 0, me1)
            nv_send, nv_recv, nv_bufs, token = _gather_start("gather_norm_v_start", [nv], [])
            gather_begin(0, [token])

    def gather_relay(g_idx, deps):
        g = gathers[g_idx]
        g["fsend"], g["frecv"], g["bufs"], tok = _relay_neighbors(f"relay_neighbors{g_idx}", g["bufs"], g["send"],
                                                                  g["recv"], deps)
        if g_idx + 1 < len(gathers):
            gather_begin(g_idx + 1, [tok])
            tok = gathers[g_idx + 1]["token"]
        return tok

    def finish_gather(g_idx, deps):
        g = gathers[g_idx]
        gsend, grecv, bufs = _relay_diagonal(f"relay_diagonal{g_idx}", g["bufs"], g["fsend"], g["frecv"], deps)
        bufs = _relay_finish(f"relay_finish{g_idx}", bufs, g["send"], g["recv"], g["fsend"], g["frecv"], gsend, grecv)
        return dict(zip(g["keys"], bufs))

    token = gather_relay(0, [gathers[0]["token"]] + [b for g in gathers[1:] for b in g["bufs"]])

    buckets = jnp.asarray(_bucket_table())
    bias = _bias_table("bias_table", rel_bias.T, buckets).reshape(NH, BLOCK, 2 * BLOCK)

    def rows_full(tm):
        return pl.BlockSpec((tm, D), lambda i, j: (i, 0))

    def tile(tm, tn):
        return pl.BlockSpec((tm, tn), lambda i, j: (i, j))

    vec_tile = pl.BlockSpec((1, TN_), lambda i, j: (0, j))

    def ffn_forward(l, wl, h_mid, tag, deps):
        xf = _rms_fwd(f"ffn_norm_fwd{tag}", h_mid, ffn_norm[l], deps=deps)

        def ep(parts, ex, outs):
            a, b = parts
            sg = jax.nn.sigmoid(a)
            silu = a * sg
            outs[0][0] = (b * (sg * (1.0 + a * (1.0 - sg)))).astype(BF16)
            outs[0][1] = silu.astype(BF16)
            outs[1][...] = (silu * b).astype(BF16)

        ab, hid = _gemm(
            f"ffn_up{tag}", (S // TM, NDEV),
            [(xf, rows_full(TM)),
             (wl["gu"], pl.BlockSpec((None, F, D), lambda i, e: (e, 0, 0))),
             (wl["gu"], pl.BlockSpec((None, F, D), lambda i, e: (e, 1, 0)))],
            [(0, 1, NT), (0, 2, NT)], [],
            [(_sds((2, NDEV, S, F), BF16), pl.BlockSpec((2, None, TM, F), lambda i, e: (0, e, i, 0))),
             (_sds((NDEV, S, F), BF16), pl.BlockSpec((None, TM, F), lambda i, e: (e, i, 0)))],
            ep, separate=True)
        (h_out,) = _gemm(
            f"ffn_down{tag}", (S // TM, D // TN_, NDEV // KC),
            [(hid, pl.BlockSpec((KC, TM, F), lambda i, j, k: (k, i, 0))),
             (wl["down"], pl.BlockSpec((KC, F, TN_), lambda i, j, k: (k, 0, j)))],
            [(0, 1, NN, _pick(c), _pick(c)) for c in range(KC)],
            [(h_mid, pl.BlockSpec((TM, TN_), lambda i, j, k: (i, j)))],
            [(_sds((S, D), F32), pl.BlockSpec((TM, TN_), lambda i, j, k: (i, j)))],
            _store_add_extra, nk=NDEV // KC, acc_shape=(TM, TN_))
        return dict(h_mid=h_mid, xf=xf, ab=ab, hid=hid), h_out

    def stacked_rows_gemm(name, a, wmat, blk, extras, ep, out_dtype, deps=()):
        return _gemm(
            name, (S // TM, D // TN_),
            [(a, rows_full(TM)), (wmat, pl.BlockSpec((NDEV, DS, TN_), lambda i, j: (0, blk, j)))],
            [(0, 1, NN, None, _stacked)], extras,
            [(_sds((S, D), out_dtype), tile(TM, TN_))], ep, deps=deps)[0]

    def back_rows_gemm(name, a, wmat, blk, out_dtype, deps=()):
        return _gemm(
            name, (S // TM, NDEV),
            [(a, rows_full(TM)), (wmat, pl.BlockSpec((None, DS, D), lambda i, e: (e, blk, 0)))],
            [(0, 1, NT)], [], [(_sds((S, D), out_dtype), pl.BlockSpec((TM, DS), lambda i, e: (i, e)))], _store,
            deps=deps)[0]

    def grad_rows_gemm(name, act, d_bf, buf, blk):
        return _gemm(
            name, (NDEV,),
            [(act, pl.BlockSpec((S, DS), lambda e: (0, e))), (d_bf, pl.BlockSpec((S, D), lambda e: (0, 0)))],
            [(0, 1, TN)], [(buf, ANY)],
            [(_sds(buf.shape, BF16), pl.BlockSpec((None, DS, D), lambda e: (e, blk, 0)))],
            _store, aliases={2: 0})[0]

    saved, weights = [], []
    h = x.reshape(S, D)
    k_heads = v_heads = hn = h_kv = norm_v = None
    for layer in range(L):
        wl = finish_gather(2 * layer, [token] if layer == 0 else [h])
        weights.append(wl)
        if layer == 0:
            nv_fsend, nv_frecv, nv_bufs = _gather_forward("gather_norm_v_forward", nv_bufs, nv_send, nv_recv,
                                                          [wl["win"]])
            (nv_all,) = _gather_finish("gather_norm_v_finish", nv_bufs, nv_send, nv_recv, nv_fsend, nv_frecv)
            norm_v = jnp.transpose(nv_all.reshape(NDEV, -1)[:, :LA * DS].reshape(NDEV, LA, DS), (1, 0, 2)).reshape(LA, D)
        sv = dict(h_in=h)
        xn = _rms_fwd(f"mix_norm_fwd{layer}", h, mix_norm[layer])
        sv["xn"] = xn
        if layer < LA:
            i_a = layer
            (zp,) = _gemm(
                f"gmlp_in{layer}", (S // TM, NDEV),
                [(xn, rows_full(TM)), (wl["win"], pl.BlockSpec((None, D, ZC), lambda i, e: (e, 0, 0)))],
                [(0, 1, NN)], [], [(_sds((S, 2 * D), F32), pl.BlockSpec((TM, ZC), lambda i, e: (i, e)))], _store)
            bst = a_b_s[i_a].T
            gated = _gmlp_fwd(f"gmlp_gate{layer}", zp, norm_v[i_a].reshape(1, D), a_w_s[i_a], bst)
            sv.update(zp=zp, gated=gated, bst=bst)
            relay_token = gather_relay(2 * layer + 1, [gated])
            h_mid = stacked_rows_gemm(f"gmlp_out{layer}", gated, wl["wout"], 0, [(h, tile(TM, TN_))],
                                      _store_add_extra, F32, deps=[relay_token])
        else:
            i_b = layer - LA
            q = stacked_rows_gemm(f"attn_q{layer}", xn, wl["wqo"], 0, [(b_b_q[i_b].reshape(1, D), vec_tile)],
                                  _store_add_extra, BF16)
            relay_token = gather_relay(2 * layer + 1, [q])
            attn, probs, sink_probs = _attn_fwd(f"attn_fwd{layer}", q, k_heads, v_heads, bias, b_sinks[i_b],
                                                deps=[relay_token])
            sv.update(q=q, attn=attn, probs=probs, sink_probs=sink_probs)
            h_mid = stacked_rows_gemm(f"attn_o{layer}", attn, wl["wqo"], 1,
                                      [(h, tile(TM, TN_)), (b_b_o[i_b].reshape(1, D), vec_tile)],
                                      _store_add_extra, F32)
        wl.update(finish_gather(2 * layer + 1, [h_mid]))
        ffn_deps = [gather_relay(2 * layer + 2, [wl["down"]])] if layer + 1 < L else []
        fsv, h = ffn_forward(layer, wl, h_mid, str(layer), ffn_deps)
        sv.update(fsv)
        saved.append(sv)
        if layer == LA - 1:
            h_kv = h
            hn = _rms_fwd("kv_norm_fwd", h, kv_norm)

            def kv_ep(acc, ex, outs):
                val = acc + ex[0][...]
                for hh in range(NKV):
                    outs[0][hh] = val[:, hh * HEAD_DIM:(hh + 1) * HEAD_DIM].astype(BF16)
                    outs[1][hh] = val[:, (NKV + hh) * HEAD_DIM:(NKV + hh + 1) * HEAD_DIM].astype(BF16)

            k_heads, v_heads = _gemm(
                "kv_proj", (S // TM,),
                [(hn, pl.BlockSpec((TM, D), lambda i: (i, 0))),
                 (wl["wkv"], pl.BlockSpec((NDEV, DS, KVW), lambda i: (0, 0, 0)))],
                [(0, 1, NN, None, _stacked)], [(b_kv.reshape(1, KVW), pl.BlockSpec((1, KVW), lambda i: (0, 0)))],
                [(_sds((NKV, S, HEAD_DIM), BF16), pl.BlockSpec((NKV, TM, HEAD_DIM), lambda i: (0, i, 0)))] * 2,
                kv_ep)

    loss11, d, d_bf, g_final = _loss_bwd("loss_bwd", h, final_norm, loss_target.reshape(S, D))
    loss = lax.psum(loss11[0, 0], AXES)

    g_mix, g_ffn = [None] * L, [None] * L
    g_ws, g_bs, g_nv = [None] * LA, [None] * LA, [None] * LA
    g_bq, g_sink, g_bo = [None] * LB, [None] * LB, [None] * LB
    dbiases = []
    kv_parts = []
    g_kvn = g_bkv = None
    exchanges = [[] for _ in range(L)]
    pending = None
    grads_wkv = None
    newest = []

    def new_grads(l):
        return {key: lax.empty((NDEV, rows, width), BF16) for key, rows, width, _ in layer_arrays(l)}

    def exchange_begin(tag, l, gl, keys):
        grads = [gl[k] for k in keys]
        lands = [lax.empty((NCHIP,) + g.shape[1:], BF16) for g in grads]
        send, recv, grads, lands, tok = _sibling_start(f"rs_sibling_start{tag}", grads, lands, [])
        newest[:] = [tok]
        return dict(tag=tag, layer=l, keys=keys, grads=grads, lands=lands, send=send, recv=recv)

    def exchange_middle(st, dep):
        tag = st["tag"]
        grads, lands = _sibling_finish(f"rs_sibling_finish{tag}", st["grads"], st["lands"], st["send"], st["recv"], [dep])
        sums, own = [], []
        for t, key in enumerate(st["keys"]):
            s_, o_ = _pair_sum(f"pair_sum_{key}{tag}", grads[t], lands[t], where)
            sums.append(s_)
            own.append(o_)
        send, recv, sums, own, tok = _chips_start(f"rs_chips_start{tag}", sums, own, [])
        newest[:] = [tok]
        st.update(sums=sums, own=own, send2=send, recv2=recv)
        exchanges[st["layer"]].append(st)

    def exchange_end(st, dep):
        sums, lands = _chips_finish(f"rs_chips_finish{st['tag']}", st["sums"], st["own"], st["send2"], st["recv2"], [dep])
        own = dict(zip(st["keys"], sums)) if st.get("direct") else {k: None for k in st["keys"]}
        return dict(zip(st["keys"], lands)), own

    for layer in reversed(range(L)):
        sv, wl = saved[layer], weights[layer]
        tag = str(layer)
        gl = new_grads(layer)
        if grads_wkv is not None and layer == LA - 1:
            gl["wkv"] = grads_wkv
        def dhid_ep(acc, ex, outs):
            outs[0][0] = (acc * ex[0][0].astype(F32)).astype(BF16)
            outs[0][1] = (acc * ex[0][1].astype(F32)).astype(BF16)

        ab_spec = pl.BlockSpec((2, None, TM, F), lambda i, e: (0, e, i, 0))
        (dab,) = _gemm(
            f"ffn_dhid{tag}", (S // TM, NDEV),
            [(d_bf, rows_full(TM)), (wl["down"], pl.BlockSpec((None, F, D), lambda i, e: (e, 0, 0)))],
            [(0, 1, NT)], [(sv["ab"], ab_spec)], [(_sds((2, NDEV, S, F), BF16), ab_spec)], dhid_ep,
            deps=list(newest))
        if pending:
            exchange_middle(pending, dab)
        act_kinds = [SHARDS, WHOLE, SHARDS2, WHOLE]
        act_lands = [lax.empty((NCHIP, S, F), BF16), lax.empty((S, D), BF16), lax.empty((2, NCHIP, S, F), BF16),
                     lax.empty((S, D), BF16)]
        a_send, a_recv, act, act_lands, tok = _sibling_start(f"act_start{tag}", [sv["hid"], d_bf, dab, sv["xf"]], act_lands,
                                                             list(newest), act_kinds)
        newest[:] = [tok]
        (dxf,) = _gemm(
            f"ffn_dx{tag}", (S // TM, D // TN_, 2 * NDEV // KC),
            [(act[2].reshape(2 * NDEV // KC, KC, S, F), pl.BlockSpec((None, KC, TM, F), lambda i, j, k: (k, 0, i, 0))),
             (wl["gu"], pl.BlockSpec((KC, F, TN_), lambda i, j, k: (k % (NDEV // KC), k // (NDEV // KC), j)))],
            [(0, 1, NN, _pick(c), _pick(c)) for c in range(KC)], [],
            [(_sds((S, D), F32), pl.BlockSpec((TM, TN_), lambda i, j, k: (i, j)))],
            _store, nk=2 * NDEV // KC, acc_shape=(TM, TN_), deps=list(newest))
        (hid_o, dout_o, dab_o, xf_o), (hid_s, dout_s, dab_s, xf_s) = _sibling_finish(
            f"act_finish{tag}", act, act_lands, a_send, a_recv, [dxf], act_kinds)
        (p_down,) = _gemm(
            f"ffn_dwdown{tag}", (NCHIP, D // TN_),
            [(hid_o.reshape(NCHIP, 2, S, F), pl.BlockSpec((None, None, S, F), lambda k, j, wr: (k, wr[0], 0, 0))),
             (dout_o, pl.BlockSpec((S, TN_), lambda k, j, wr: (0, j))),
             (hid_s, pl.BlockSpec((None, S, F), lambda k, j, wr: (k, 0, 0))),
             (dout_s, pl.BlockSpec((S, TN_), lambda k, j, wr: (0, j)))],
            [(0, 1, TN), (2, 3, TN)], [],
            [(_sds((NCHIP, F, D), BF16), pl.BlockSpec((None, F, TN_), lambda k, j, wr: (k, 0, j)))],
            _store, prefetch=[where])
        (p_gu,) = _gemm(
            f"ffn_dwup{tag}", (2, NCHIP, D // TN_),
            [(dab_o.reshape(2, NCHIP, 2, S, F),
              pl.BlockSpec((None, None, None, S, F), lambda w, k, j, wr: (w, k, wr[0], 0, 0))),
             (xf_o, pl.BlockSpec((S, TN_), lambda w, k, j, wr: (0, j))),
             (dab_s, pl.BlockSpec((None, None, S, F), lambda w, k, j, wr: (w, k, 0, 0))),
             (xf_s, pl.BlockSpec((S, TN_), lambda w, k, j, wr: (0, j)))],
            [(0, 1, TN), (2, 3, TN)], [],
            [(_sds((NCHIP, 2 * F, D), BF16), pl.BlockSpec((None, F, TN_), lambda w, k, j, wr: (k, w, j)))],
            _store, prefetch=[where])
        send2, recv2, sums, own, tok = _chips_start(
            f"rs_chips_start_ffn{tag}", [p_gu, p_down], [lax.empty(p_gu.shape, BF16), lax.empty(p_down.shape, BF16)], [])
        newest[:] = [tok]
        exchanges[layer].append(dict(tag=f"_ffn{tag}", layer=layer, keys=["gu", "down"], sums=sums, own=own, send2=send2,
                                     recv2=recv2, direct=True))
        d, d_bf, g_ffn[layer], colsum = _rms_bwd(f"ffn_norm_bwd{tag}", sv["h_mid"], ffn_norm[layer], dxf, d,
                                                 deps=list(newest))
        if layer < LA:
            i_a = layer
            dgated = back_rows_gemm(f"gmlp_dgated{tag}", d_bf, wl["wout"], 0, F32)
            gl["wout"] = grad_rows_gemm(f"gmlp_dwout{tag}", sv["gated"], d_bf, gl["wout"], 0)
            dzp, g_ws[i_a], dbs, g_nv[i_a] = _gmlp_bwd(f"gmlp_bwd{tag}", sv["zp"], dgated,
                                                       norm_v[i_a].reshape(1, D), a_w_s[i_a], sv["bst"])
            g_bs[i_a] = dbs[:, 0, :]
            (gl["win"],) = _gemm(
                f"gmlp_dwin{tag}", (NDEV, D // TS),
                [(sv["xn"], pl.BlockSpec((S, TS), lambda e, i: (0, i))),
                 (dzp, pl.BlockSpec((S, ZC), lambda e, i: (0, e)))],
                [(0, 1, TN)], [(gl["win"], ANY)],
                [(_sds(gl["win"].shape, BF16), pl.BlockSpec((None, TS, ZC), lambda e, i: (e, i, 0)))],
                _store, aliases={2: 0})
            (dxn,) = _gemm(
                f"gmlp_dx{tag}", (S // TM, D // TN_, NDEV // KC),
                [(dzp, pl.BlockSpec((TM, KC * ZC), lambda i, j, k: (i, k))),
                 (wl["win"], pl.BlockSpec((KC, TN_, ZC), lambda i, j, k: (k, j, 0)))],
                [(0, 1, NT, _cols(c, ZC), _pick(c)) for c in range(KC)], [],
                [(_sds((S, D), F32), pl.BlockSpec((TM, TN_), lambda i, j, k: (i, j)))],
                _store, nk=NDEV // KC, acc_shape=(TM, TN_))
        else:
            i_b = layer - LA
            g_bo[i_b] = colsum
            dattn = back_rows_gemm(f"attn_dout{tag}", d_bf, wl["wqo"], 1, BF16)
            gl["wqo"] = grad_rows_gemm(f"attn_dwo{tag}", sv["attn"], d_bf, gl["wqo"], 1)
            dq, g_bq[i_b], dkc, dkp, dvc, dvp, dbias, dsink = _attn_bwd(
                f"attn_bwd{tag}", sv["q"], k_heads, v_heads, dattn, sv["probs"], sv["sink_probs"])
            kv_parts.append((dkc, dkp, dvc, dvp))
            g_sink[i_b] = dsink.reshape(NH)
            dbiases.append(dbias.reshape(NH, BLOCK * 2 * BLOCK))
            gl["wqo"] = grad_rows_gemm(f"attn_dwq{tag}", sv["xn"], dq, gl["wqo"], 0)
            dxn = back_rows_gemm(f"attn_dx{tag}", dq, wl["wqo"], 0, F32)
        d, d_bf, g_mix[layer], _ = _rms_bwd(f"mix_norm_bwd{tag}", sv["h_in"], mix_norm[layer], dxn, d)
        pending = exchange_begin(f"_mix{tag}", layer, gl, [k for k in gl if k not in ("gu", "down")])
        if layer == LA:
            wkv = weights[LA - 1]["wkv"]
            dkv, g_bkv = _kv_grad("kv_grad", kv_parts)
            (grads_wkv,) = _gemm(
                "kv_dw", (NDEV,),
                [(hn, pl.BlockSpec((S, DS), lambda e: (0, e))), (dkv, pl.BlockSpec((S, KVW), lambda e: (0, 0)))],
                [(0, 1, TN)], [(lax.empty((NDEV, DS, KVW), BF16), ANY)],
                [(_sds((NDEV, DS, KVW), BF16), pl.BlockSpec((None, DS, KVW), lambda e: (e, 0, 0)))],
                _store, aliases={2: 0}, deps=list(newest))
            (dhn,) = _gemm(
                "kv_dx", (S // TM, NDEV),
                [(dkv, pl.BlockSpec((TM, KVW), lambda i, e: (i, 0))),
                 (wkv, pl.BlockSpec((None, DS, KVW), lambda i, e: (e, 0, 0)))],
                [(0, 1, NT)], [], [(_sds((S, D), F32), pl.BlockSpec((TM, DS), lambda i, e: (i, e)))], _store)
            d, d_bf, g_kvn, _ = _rms_bwd("kv_norm_bwd", h_kv, kv_norm, dhn, d)
    grad_x = d.reshape(x.shape)

    exchange_middle(pending, d)

    g_rel = _bias_grad("bias_grad", dbiases, buckets)
    small_local = _pack([jnp.concatenate(g_mix, axis=0), jnp.concatenate(g_ffn, axis=0), jnp.stack(g_ws),
                         jnp.stack(g_bs), g_kvn, g_bkv, jnp.concatenate(g_bq, axis=0), jnp.stack(g_sink),
                         jnp.concatenate(g_bo, axis=0), g_rel, g_final, jnp.concatenate(g_nv, axis=0)])
    small_slot = _cast_into("put_small_grads", small_local.reshape((1,) + small_local.shape), 0,
                            lax.empty((NDEV,) + small_local.shape, F32), 0, me1)
    s_send, s_recv, s_bufs, s_tok = _gather_start("gather_small_start", [small_slot], list(newest))

    results = {}
    after = [s_tok]

    def upd(pname, w, m, v, l, li, lands, row0, own=None):
        w3 = w if w.ndim == 3 else w.reshape((1,) + w.shape)
        prev = results.get(pname) or [lax.empty(w3.shape, F32) for _ in range(4)]
        results[pname] = _adamw_shard(f"adamw_{pname}{l}", w3, m.reshape(w3.shape), v.reshape(w3.shape), lands,
                                      row0, li, prev, deps=list(after), own=own, where=where)
        after[:] = [results[pname][0]]

    for l in reversed(range(L)):
        for st in exchanges[l]:
            lands, own = exchange_end(st, after[0])
            if "gu" in lands:
                upd("ffn_w_gate", gate_t, tr3(m_ffn_w_gate), tr3(v_ffn_w_gate), l, l, lands["gu"], 0, own["gu"])
                upd("ffn_w_up", up_t, tr3(m_ffn_w_up), tr3(v_ffn_w_up), l, l, lands["gu"], F, own["gu"])
                upd("ffn_w_down", ffn_w_down, m_ffn_w_down, v_ffn_w_down, l, l, lands["down"], 0, own["down"])
            if "win" in lands:
                upd("a_w_in", a_w_in, m_a_w_in, v_a_w_in, l, l, lands["win"], 0)
                upd("a_w_out", a_w_out, m_a_w_out, v_a_w_out, l, l, lands["wout"], 0)
            if "wkv" in lands:
                upd("w_kv", w_kv, m_w_kv, v_w_kv, l, 0, lands["wkv"], 0)
            if "wqo" in lands:
                upd("b_w_q", b_w_q, m_b_w_q, v_b_w_q, l, l - LA, lands["wqo"], 0)
                upd("b_w_o", b_w_o, m_b_w_o, v_b_w_o, l, l - LA, lands["wqo"], DS)
    for pname in ("ffn_w_gate", "ffn_w_up"):
        results[pname] = [tr3(r) for r in results[pname]]
    results["w_kv"] = [r.reshape(w_kv.shape) for r in results["w_kv"]]

    small_w = [mix_norm, ffn_norm, a_w_s, a_b_s, kv_norm, b_kv, b_b_q, b_sinks, b_b_o, rel_bias, final_norm]
    small_m = [m_mix_norm, m_ffn_norm, m_a_w_s, m_a_b_s, m_kv_norm, m_b_kv, m_b_b_q, m_b_sinks, m_b_b_o, m_rel_bias,
               m_final_norm]
    small_v = [v_mix_norm, v_ffn_norm, v_a_w_s, v_a_b_s, v_kv_norm, v_b_kv, v_b_b_q, v_b_sinks, v_b_b_o, v_rel_bias,
               v_final_norm]
    shapes = [w.shape for w in small_w] + [(LA, D)]
    s_fsend, s_frecv, s_bufs = _gather_forward("gather_small_forward", s_bufs, s_send, s_recv, list(after))
    (small_all,) = _gather_finish("gather_small_finish", s_bufs, s_send, s_recv, s_fsend, s_frecv)
    small_sum = _sum_devices("sum_small_grads", small_all)
    small_g = _unpack(small_sum, shapes)
    g_normv = lax.dynamic_slice_in_dim(small_g[-1], me * DS, DS, axis=1)
    small_g = small_g[:-1] + [g_normv]
    small_w, small_m, small_v = small_w + [a_norm_v], small_m + [m_a_norm_v], small_v + [v_a_norm_v]
    shapes = [w.shape for w in small_w]
    s_delta, s_m, s_v = _adamw_flat("adamw_small", _pack(small_w), _pack(small_g), _pack(small_m), _pack(small_v))
    s_delta, s_m, s_v = _unpack(s_delta, shapes), _unpack(s_m, shapes), _unpack(s_v, shapes)

    names = ["mix_norm", "ffn_norm", "a_w_in", "a_norm_v", "a_w_s", "a_b_s", "a_w_out", "kv_norm", "w_kv", "b_kv",
             "b_w_q", "b_b_q", "b_sinks", "b_w_o", "b_b_o", "rel_bias", "ffn_w_gate", "ffn_w_up", "ffn_w_down",
             "final_norm"]
    small_names = ["mix_norm", "ffn_norm", "a_w_s", "a_b_s", "kv_norm", "b_kv", "b_b_q", "b_sinks", "b_b_o", "rel_bias",
                   "final_norm", "a_norm_v"]
    res = {}
    for idx, nm in enumerate(small_names):
        res[nm] = (small_g[idx].reshape(shapes[idx]), s_delta[idx], s_m[idx], s_v[idx])
    for nm, u in results.items():
        res[nm] = tuple(u)
    out = [loss, grad_x]
    for part in range(4):
        out += [res[nm][part] for nm in names]
    return tuple(out)
```

```python
import math

import numpy as np
import jax
import jax.numpy as jnp
from jax import lax
from jax.experimental import pallas as pl
from jax.experimental.pallas import tpu as pltpu

F32 = jnp.float32
BF16 = jnp.bfloat16
AXES = ("x", "y", "c")
NDEV = 8
NCHIP = 4
CHUNK = 128
GROUPS = 8
HEAD_DIM = 64
KV_GROUP = 8
BLOCK = 128
N_BUCKETS = 32
MAX_DISTANCE = 128
RMS_EPS = 1e-5
NEG_INF = -1e30
ADAM_LR, ADAM_B1, ADAM_B2, ADAM_EPS, ADAM_WD, ADAM_STEP = 0.001, 0.9, 0.999, 1e-08, 0.01, 10
VMEM_LIMIT_BYTES = 56 * 1024 * 1024

NN = (((1,), (0,)), ((), ()))
NT = (((1,), (1,)), ((), ()))
TN = (((0,), (0,)), ((), ()))
ANY = pl.BlockSpec(memory_space=pl.ANY)
HBM = pl.BlockSpec(memory_space=pltpu.HBM)
SEM = pl.BlockSpec(memory_space=pltpu.SEMAPHORE)
MESH = pl.DeviceIdType.MESH
EFFECT = pltpu.SideEffectType.DATAFLOW_SIDE_EFFECTING


def _pcall(body, *, name, out_shape, in_specs, out_specs, grid=(), scratch=(), aliases=None, prefetch=0, deps=()):
    n_in, n_dep = len(in_specs), len(deps)
    if n_dep:
        inner = body

        def body(*refs):
            return inner(*refs[:prefetch + n_in], *refs[prefetch + n_in + n_dep:])

        in_specs = list(in_specs) + [ANY] * n_dep
    params = dict(vmem_limit_bytes=VMEM_LIMIT_BYTES)
    if grid:
        params["dimension_semantics"] = ("arbitrary",) * len(grid)
    kw = dict(name=name, out_shape=out_shape, compiler_params=pltpu.CompilerParams(**params),
              input_output_aliases=aliases or {})
    if prefetch:
        kw["grid_spec"] = pltpu.PrefetchScalarGridSpec(num_scalar_prefetch=prefetch, grid=grid, in_specs=in_specs,
                                                       out_specs=out_specs, scratch_shapes=list(scratch))
    else:
        kw.update(grid=grid, in_specs=in_specs, out_specs=out_specs, scratch_shapes=list(scratch))
    call = pl.pallas_call(body, **kw)
    return lambda *args: call(*args, *deps)


def _sds(shape, dtype):
    return jax.ShapeDtypeStruct(tuple(shape), dtype)


def _position():
    x, y, c = lax.axis_index("x"), lax.axis_index("y"), lax.axis_index("c")
    chips = [(1 - x, y), (x, 1 - y), (1 - x, 1 - y)]
    return x, y, c, chips


def _slot(px, py, pc):
    return 4 * px + 2 * py + pc


def _remote(ref_src, ref_dst, send, recv, to):
    return pltpu.make_async_remote_copy(src_ref=ref_src, dst_ref=ref_dst, send_sem=send, recv_sem=recv,
                                        device_id=to, device_id_type=MESH)


def _hbm(arrays):
    return [pltpu.with_memory_space_constraint(a, pltpu.HBM) for a in arrays]


def _split_call(body, name, out_shape, in_specs, out_specs, aliases):
    return pl.pallas_call(body, name=name, out_shape=out_shape, in_specs=in_specs, out_specs=out_specs,
                          input_output_aliases=aliases, compiler_params=pltpu.CompilerParams(has_side_effects=EFFECT))


def _token_shape():
    return _sds((8, 128), F32)


def _gather_start(name, bufs, deps):
    n, nd = len(bufs), len(deps)

    def body(*refs):
        ins, send, recv, token = refs[:n], refs[n + nd], refs[n + nd + 1], refs[2 * n + nd + 2]
        x, y, c, chips = _position()
        peers = [(x, y, 1 - c)] + [(*chip, c) for chip in chips]
        for t in range(n):
            mine = ins[t].at[_slot(x, y, c)]
            for k, peer in enumerate(peers):
                _remote(mine, mine, send.at[4 * t + k], recv.at[4 * t + k], peer).start()
        token[...] = jnp.zeros_like(token)

    res = _split_call(
        body, name,
        (pltpu.SemaphoreType.DMA((4 * n,)), pltpu.SemaphoreType.DMA((4 * n,)), *[pltpu.HBM(b.shape, b.dtype) for b in bufs],
         _token_shape()),
        [HBM] * n + [ANY] * nd, (SEM, SEM, *[HBM] * n, pl.BlockSpec(memory_space=pltpu.VMEM)),
        {t: 2 + t for t in range(n)})(*_hbm(bufs), *deps)
    return res[0], res[1], list(res[2:2 + n]), res[2 + n]


def _gather_forward(name, bufs, send, recv, deps):
    n, nd = len(bufs), len(deps)

    def body(*refs):
        ins, send_in, recv_in = refs[:n], refs[n], refs[n + 1]
        fsend, frecv = refs[n + 2 + nd], refs[n + 3 + nd]
        x, y, c, chips = _position()
        for j, chip in enumerate(chips):
            for t in range(n):
                blk = ins[t].at[_slot(*chip, c)]
                _remote(blk, blk, send_in.at[4 * t + 1 + j], recv_in.at[4 * t + 1 + j], (*chip, c)).wait_recv()
                _remote(blk, blk, fsend.at[3 * t + j], frecv.at[3 * t + j], (x, y, 1 - c)).start()

    res = _split_call(
        body, name,
        (pltpu.SemaphoreType.DMA((3 * n,)), pltpu.SemaphoreType.DMA((3 * n,)), *[pltpu.HBM(b.shape, b.dtype) for b in bufs]),
        [HBM] * n + [SEM, SEM] + [ANY] * nd, (SEM, SEM, *[HBM] * n),
        {t: 2 + t for t in range(n)})(*_hbm(bufs), send, recv, *deps)
    return res[0], res[1], list(res[2:])


def _gather_finish(name, bufs, send, recv, fsend, frecv):
    n = len(bufs)

    def body(*refs):
        ins, send_in, recv_in, fs_in, fr_in = refs[:n], refs[n], refs[n + 1], refs[n + 2], refs[n + 3]
        x, y, c, chips = _position()
        sibling = (x, y, 1 - c)
        peers = [sibling] + [(*chip, c) for chip in chips]
        for t in range(n):
            blk = ins[t].at[_slot(x, y, 1 - c)]
            _remote(blk, blk, send_in.at[4 * t], recv_in.at[4 * t], sibling).wait_recv()
            for j, chip in enumerate(chips):
                blk = ins[t].at[_slot(*chip, 1 - c)]
                _remote(blk, blk, fs_in.at[3 * t + j], fr_in.at[3 * t + j], sibling).wait_recv()
            mine = ins[t].at[_slot(x, y, c)]
            for k, peer in enumerate(peers):
                _remote(mine, mine, send_in.at[4 * t + k], recv_in.at[4 * t + k], peer).wait_send()
            for j, chip in enumerate(chips):
                blk = ins[t].at[_slot(*chip, c)]
                _remote(blk, blk, fs_in.at[3 * t + j], fr_in.at[3 * t + j], sibling).wait_send()

    res = _split_call(
        body, name, tuple(pltpu.HBM(b.shape, b.dtype) for b in bufs),
        [HBM] * n + [SEM] * 4, tuple([HBM] * n), {t: t for t in range(n)})(*_hbm(bufs), send, recv, fsend, frecv)
    return list(res)


def _halves(ref):
    rows = ref.shape[0] // 2
    return ref.at[pl.ds(0, rows)], ref.at[pl.ds(rows, rows)]


def _relay_start(name, bufs, deps):
    n, nd = len(bufs), len(deps)

    def body(*refs):
        ins, send, recv, token = refs[:n], refs[n + nd], refs[n + nd + 1], refs[2 * n + nd + 2]
        x, y, c, _ = _position()
        peers = [(x, y, 1 - c), (1 - x, y, c), (x, 1 - y, c)]
        for t in range(n):
            mine = ins[t].at[_slot(x, y, c)]
            for k, peer in enumerate(peers):
                _remote(mine, mine, send.at[3 * t + k], recv.at[3 * t + k], peer).start()
        token[...] = jnp.zeros_like(token)

    res = _split_call(
        body, name,
        (pltpu.SemaphoreType.DMA((3 * n,)), pltpu.SemaphoreType.DMA((3 * n,)), *[pltpu.HBM(b.shape, b.dtype) for b in bufs],
         _token_shape()),
        [HBM] * n + [ANY] * nd, (SEM, SEM, *[HBM] * n, pl.BlockSpec(memory_space=pltpu.VMEM)),
        {t: 2 + t for t in range(n)})(*_hbm(bufs), *deps)
    return res[0], res[1], list(res[2:2 + n]), res[2 + n]


def _relay_neighbors(name, bufs, send, recv, deps):
    n, nd = len(bufs), len(deps)

    def body(*refs):
        ins, send_in, recv_in = refs[:n], refs[n], refs[n + 1]
        fsend, frecv, token = refs[n + 2 + nd], refs[n + 3 + nd], refs[2 * n + 4 + nd]
        x, y, c, _ = _position()
        sibling, xn, yn = (x, y, 1 - c), (1 - x, y, c), (x, 1 - y, c)
        for t in range(n):
            blk = ins[t].at[_slot(*xn)]
            _remote(blk, blk, send_in.at[3 * t + 1], recv_in.at[3 * t + 1], xn).wait_recv()
            _remote(blk, blk, fsend.at[4 * t], frecv.at[4 * t], sibling).start()
            half = _halves(blk)[0]
            _remote(half, half, fsend.at[4 * t + 1], frecv.at[4 * t + 1], yn).start()
        for t in range(n):
            blk = ins[t].at[_slot(*yn)]
            _remote(blk, blk, send_in.at[3 * t + 2], recv_in.at[3 * t + 2], yn).wait_recv()
            _remote(blk, blk, fsend.at[4 * t + 2], frecv.at[4 * t + 2], sibling).start()
            half = _halves(blk)[1]
            _remote(half, half, fsend.at[4 * t + 3], frecv.at[4 * t + 3], xn).start()
        token[...] = jnp.zeros_like(token)

    res = _split_call(
        body, name,
        (pltpu.SemaphoreType.DMA((4 * n,)), pltpu.SemaphoreType.DMA((4 * n,)), *[pltpu.HBM(b.shape, b.dtype) for b in bufs],
         _token_shape()),
        [HBM] * n + [SEM, SEM] + [ANY] * nd, (SEM, SEM, *[HBM] * n, pl.BlockSpec(memory_space=pltpu.VMEM)),
        {t: 2 + t for t in range(n)})(*_hbm(bufs), send, recv, *deps)
    return res[0], res[1], list(res[2:2 + n]), res[2 + n]


def _relay_diagonal(name, bufs, fsend, frecv, deps):
    n, nd = len(bufs), len(deps)

    def body(*refs):
        ins, fs_in, fr_in = refs[:n], refs[n], refs[n + 1]
        gsend, grecv = refs[n + 2 + nd], refs[n + 3 + nd]
        x, y, c, _ = _position()
        for t in range(n):
            blk = ins[t].at[_slot(1 - x, 1 - y, c)]
            first, second = _halves(blk)
            _remote(first, first, fs_in.at[4 * t + 1], fr_in.at[4 * t + 1], (x, 1 - y, c)).wait_recv()
            _remote(second, second, fs_in.at[4 * t + 3], fr_in.at[4 * t + 3], (1 - x, y, c)).wait_recv()
            _remote(blk, blk, gsend.at[t], grecv.at[t], (x, y, 1 - c)).start()

    res = _split_call(
        body, name,
        (pltpu.SemaphoreType.DMA((n,)), pltpu.SemaphoreType.DMA((n,)), *[pltpu.HBM(b.shape, b.dtype) for b in bufs]),
        [HBM] * n + [SEM, SEM] + [ANY] * nd, (SEM, SEM, *[HBM] * n),
        {t: 2 + t for t in range(n)})(*_hbm(bufs), fsend, frecv, *deps)
    return res[0], res[1], list(res[2:])


def _relay_finish(name, bufs, send, recv, fsend, frecv, gsend, grecv):
    n = len(bufs)

    def body(*refs):
        ins = refs[:n]
        send_in, recv_in, fs_in, fr_in, gs_in, gr_in = refs[n:n + 6]
        x, y, c, _ = _position()
        sibling, xn, yn = (x, y, 1 - c), (1 - x, y, c), (x, 1 - y, c)
        for t in range(n):
            blk = ins[t].at[_slot(x, y, 1 - c)]
            _remote(blk, blk, send_in.at[3 * t], recv_in.at[3 * t], sibling).wait_recv()
            blk = ins[t].at[_slot(1 - x, y, 1 - c)]
            _remote(blk, blk, fs_in.at[4 * t], fr_in.at[4 * t], sibling).wait_recv()
            blk = ins[t].at[_slot(x, 1 - y, 1 - c)]
            _remote(blk, blk, fs_in.at[4 * t + 2], fr_in.at[4 * t + 2], sibling).wait_recv()
            blk = ins[t].at[_slot(1 - x, 1 - y, 1 - c)]
            _remote(blk, blk, gs_in.at[t], gr_in.at[t], sibling).wait_recv()
            mine = ins[t].at[_slot(x, y, c)]
            for k, peer in enumerate([sibling, xn, yn]):
                _remote(mine, mine, send_in.at[3 * t + k], recv_in.at[3 * t + k], peer).wait_send()
            bx, by = ins[t].at[_slot(*xn)], ins[t].at[_slot(*yn)]
            _remote(bx, bx, fs_in.at[4 * t], fr_in.at[4 * t], sibling).wait_send()
            _remote(_halves(bx)[0], _halves(bx)[0], fs_in.at[4 * t + 1], fr_in.at[4 * t + 1], yn).wait_send()
            _remote(by, by, fs_in.at[4 * t + 2], fr_in.at[4 * t + 2], sibling).wait_send()
            _remote(_halves(by)[1], _halves(by)[1], fs_in.at[4 * t + 3], fr_in.at[4 * t + 3], xn).wait_send()
            bd = ins[t].at[_slot(1 - x, 1 - y, c)]
            _remote(bd, bd, gs_in.at[t], gr_in.at[t], sibling).wait_send()

    res = _split_call(
        body, name, tuple(pltpu.HBM(b.shape, b.dtype) for b in bufs),
        [HBM] * n + [SEM] * 6, tuple([HBM] * n), {t: t for t in range(n)})(
            *_hbm(bufs), send, recv, fsend, frecv, gsend, grecv)
    return list(res)


WHOLE, SHARDS, SHARDS2 = 0, 1, 2


def _sibling_copies(srcs, lands, kinds, c):
    pairs = []
    for s_ref, l_ref, kind in zip(srcs, lands, kinds):
        if kind == WHOLE:
            pairs.append((s_ref, l_ref))
        elif kind == SHARDS:
            pairs += [(s_ref.at[2 * k + (1 - c)], l_ref.at[k]) for k in range(NCHIP)]
        else:
            pairs += [(s_ref.at[w, 2 * k + (1 - c)], l_ref.at[w, k]) for w in range(2) for k in range(NCHIP)]
    return pairs


def _count_copies(kinds):
    return sum({WHOLE: 1, SHARDS: NCHIP, SHARDS2: 2 * NCHIP}[k] for k in kinds)


def _sibling_start(name, srcs, lands, deps, kinds=None):
    n, nd = len(srcs), len(deps)
    kinds = kinds or [SHARDS] * n
    ncp = _count_copies(kinds)

    def body(*refs):
        s_in, l_in = refs[:n], refs[n:2 * n]
        send, recv, token = refs[2 * n + nd], refs[2 * n + nd + 1], refs[4 * n + nd + 2]
        x, y, c, _ = _position()
        for i, (src, dst) in enumerate(_sibling_copies(s_in, l_in, kinds, c)):
            _remote(src, dst, send.at[i], recv.at[i], (x, y, 1 - c)).start()
        token[...] = jnp.zeros_like(token)

    both = list(srcs) + list(lands)
    res = _split_call(
        body, name,
        (pltpu.SemaphoreType.DMA((ncp,)), pltpu.SemaphoreType.DMA((ncp,)),
         *[pltpu.HBM(b.shape, b.dtype) for b in both], _token_shape()),
        [HBM] * (2 * n) + [ANY] * nd, (SEM, SEM, *[HBM] * (2 * n), pl.BlockSpec(memory_space=pltpu.VMEM)),
        {t: 2 + t for t in range(2 * n)})(*_hbm(both), *deps)
    return res[0], res[1], list(res[2:2 + n]), list(res[2 + n:2 + 2 * n]), res[2 + 2 * n]


def _sibling_finish(name, srcs, lands, send, recv, deps, kinds=None):
    n, nd = len(srcs), len(deps)
    kinds = kinds or [SHARDS] * n

    def body(*refs):
        s_in, l_in, send_in, recv_in = refs[:n], refs[n:2 * n], refs[2 * n], refs[2 * n + 1]
        x, y, c, _ = _position()
        for i, (src, dst) in enumerate(_sibling_copies(s_in, l_in, kinds, c)):
            cp = _remote(src, dst, send_in.at[i], recv_in.at[i], (x, y, 1 - c))
            cp.wait_send()
            cp.wait_recv()

    both = list(srcs) + list(lands)
    res = _split_call(
        body, name, tuple(pltpu.HBM(b.shape, b.dtype) for b in both),
        [HBM] * (2 * n) + [SEM, SEM] + [ANY] * nd, tuple([HBM] * (2 * n)),
        {t: t for t in range(2 * n)})(*_hbm(both), send, recv, *deps)
    return list(res[:n]), list(res[n:])


def _chips_start(name, parts, lands, deps):
    n, nd = len(parts), len(deps)

    def body(*refs):
        p_in, l_in = refs[:n], refs[n:2 * n]
        send, recv, token = refs[2 * n + nd], refs[2 * n + nd + 1], refs[4 * n + nd + 2]
        x, y, c, chips = _position()
        for t in range(n):
            for j, chip in enumerate(chips):
                _remote(p_in[t].at[2 * chip[0] + chip[1]], l_in[t].at[2 * x + y], send.at[3 * t + j], recv.at[3 * t + j],
                        (*chip, c)).start()
        token[...] = jnp.zeros_like(token)

    both = list(parts) + list(lands)
    res = _split_call(
        body, name,
        (pltpu.SemaphoreType.DMA((3 * n,)), pltpu.SemaphoreType.DMA((3 * n,)), *[pltpu.HBM(b.shape, b.dtype) for b in both],
         _token_shape()),
        [HBM] * (2 * n) + [ANY] * nd, (SEM, SEM, *[HBM] * (2 * n), pl.BlockSpec(memory_space=pltpu.VMEM)),
        {t: 2 + t for t in range(2 * n)})(*_hbm(both), *deps)
    return res[0], res[1], list(res[2:2 + n]), list(res[2 + n:2 + 2 * n]), res[2 + 2 * n]


def _chips_finish(name, parts, lands, send, recv, deps):
    n, nd = len(parts), len(deps)

    def body(*refs):
        p_in, l_in, send_in, recv_in = refs[:n], refs[n:2 * n], refs[2 * n], refs[2 * n + 1]
        x, y, c, chips = _position()
        for t in range(n):
            for j, chip in enumerate(chips):
                k = 2 * chip[0] + chip[1]
                _remote(p_in[t].at[k], l_in[t].at[k], send_in.at[3 * t + j], recv_in.at[3 * t + j], (*chip, c)).wait_recv()
                _remote(p_in[t].at[k], l_in[t].at[2 * x + y], send_in.at[3 * t + j], recv_in.at[3 * t + j],
                        (*chip, c)).wait_send()

    both = list(parts) + list(lands)
    res = _split_call(
        body, name, tuple(pltpu.HBM(b.shape, b.dtype) for b in both),
        [HBM] * (2 * n) + [SEM, SEM] + [ANY] * nd, tuple([HBM] * (2 * n)),
        {t: t for t in range(2 * n)})(*_hbm(both), send, recv, *deps)
    return list(res[:n]), list(res[n:])


def _pair_sum(name, grad, recv, where):
    _, r, w = grad.shape
    tr = _row_tile(r, w, budget=4 * 1024 * 1024)
    g4 = grad.reshape(NCHIP, 2, r, w)

    def body(where_ref, g_ref, r_ref, o_ref, own_ref):
        val = (g_ref[...].astype(F32) + r_ref[...].astype(F32)).astype(o_ref.dtype)
        o_ref[...] = val

        @pl.when(pl.program_id(1) == where_ref[1])
        def _():
            own_ref[...] = val

    out = _sds((NCHIP, r, w), grad.dtype)
    return _pcall(
        body, name=name, out_shape=[out, out], grid=(r // tr, NCHIP), prefetch=1,
        in_specs=[pl.BlockSpec((None, None, tr, w), lambda i, k, wr: (k, wr[0], i, 0)),
                  pl.BlockSpec((None, tr, w), lambda i, k, wr: (k, i, 0))],
        out_specs=[pl.BlockSpec((None, tr, w), lambda i, k, wr: (k, i, 0)),
                   pl.BlockSpec((None, tr, w), lambda i, k, wr: (wr[1], i, 0))],
    )(where, g4, recv)


def _row_tile(rows, width, budget=2 * 1024 * 1024):
    best = None
    for t in range(16, rows + 1, 16):
        if rows % t == 0 and t * width * 4 <= budget:
            best = t
    if best is None and rows * width * 4 <= budget:
        best = rows
    assert best is not None, (rows, width)
    return best


def _gemm(name, grid, operands, prods, extras, outs, epilogue, *, nk=1, acc_shape=None, aliases=None, separate=False,
          deps=(), prefetch=()):
    n_op, n_ex, n_out = len(operands), len(extras), len(outs)

    def body(*refs):
        refs = refs[len(prefetch):]
        ops, ex, out_refs = refs[:n_op], refs[n_op:n_op + n_ex], refs[n_op + n_ex:n_op + n_ex + n_out]
        parts = []
        for pr in prods:
            a, b = ops[pr[0]], ops[pr[1]]
            av = pr[3](a) if len(pr) > 3 and pr[3] else a[...]
            bv = pr[4](b) if len(pr) > 4 and pr[4] else b[...]
            parts.append(lax.dot_general(av, bv, pr[2], preferred_element_type=F32))
        if separate:
            epilogue(parts, ex, out_refs)
            return
        part = parts[0]
        for p in parts[1:]:
            part = part + p
        if nk == 1:
            epilogue(part, ex, out_refs)
        else:
            acc = refs[-1]
            k = pl.program_id(len(grid) - 1)

            @pl.when(k == 0)
            def _():
                acc[...] = part

            @pl.when(k > 0)
            def _():
                acc[...] += part

            @pl.when(k == nk - 1)
            def _():
                epilogue(acc[...], ex, out_refs)

    res = _pcall(
        body, name=name, out_shape=[o[0] for o in outs], grid=grid,
        in_specs=[o[1] for o in operands] + [e[1] for e in extras], out_specs=[o[1] for o in outs],
        scratch=[pltpu.VMEM(acc_shape, F32)] if nk > 1 else [], aliases=aliases, deps=deps, prefetch=len(prefetch),
    )(*prefetch, *[o[0] for o in operands], *[e[0] for e in extras])
    return list(res)


def _store(acc, ex, outs):
    outs[0][...] = acc.astype(outs[0].dtype)


def _store_add_extra(acc, ex, outs):
    v = acc
    for e in ex:
        v = v + e[...]
    outs[0][...] = v.astype(outs[0].dtype)


def _stacked(ref):
    b = ref[...]
    return b.reshape(b.shape[0] * b.shape[1], b.shape[2])


def _pick(c):
    return lambda ref: ref[c]


def _cols(c, width):
    return lambda ref: ref[:, c * width:(c + 1) * width]


def _grad_cols(name, act, dy, buf, ts):
    s, k = act.shape
    nd, _, n = buf.shape

    def body(a_ref, dy_ref, b_ref, o_ref, at_ref):
        @pl.when(pl.program_id(1) == 0)
        def _():
            at_ref[...] = a_ref[...].T

        o_ref[...] = jnp.dot(at_ref[...], dy_ref[...], preferred_element_type=F32).astype(o_ref.dtype)

    return _pcall(
        body, name=name, out_shape=_sds(buf.shape, buf.dtype), grid=(k // ts, nd),
        in_specs=[pl.BlockSpec((s, ts), lambda i, e: (0, i)), pl.BlockSpec((s, n), lambda i, e: (0, e)), ANY],
        out_specs=pl.BlockSpec((None, ts, n), lambda i, e: (e, i, 0)), aliases={2: 0},
        scratch=[pltpu.VMEM((ts, s), act.dtype)],
    )(act, dy, buf)


def _gelu_parts(z):
    c = math.sqrt(2.0 / math.pi)
    t = jnp.tanh(c * (z + 0.044715 * (z * z * z)))
    val = 0.5 * z * (1.0 + t)
    grad = 0.5 * (1.0 + t) + 0.5 * z * (1.0 - t * t) * (c * (1.0 + 3.0 * 0.044715 * z * z))
    return val, grad


def _rms_fwd(name, h, g, deps=()):
    s, d = h.shape
    tr = _row_tile(s, d)

    def body(h_ref, g_ref, o_ref):
        hv = h_ref[...]
        r = lax.rsqrt(jnp.mean(hv * hv, axis=-1, keepdims=True) + RMS_EPS)
        o_ref[...] = (hv * r * g_ref[...]).astype(o_ref.dtype)

    return _pcall(
        body, name=name, out_shape=_sds((s, d), BF16), grid=(s // tr,),
        in_specs=[pl.BlockSpec((tr, d), lambda i: (i, 0)), pl.BlockSpec((1, d), lambda i: (0, 0))],
        out_specs=pl.BlockSpec((tr, d), lambda i: (i, 0)), deps=deps,
    )(h, g.reshape(1, d))


def _accumulate(ref, val, first):
    @pl.when(first)
    def _():
        ref[...] = val

    @pl.when(jnp.logical_not(first))
    def _():
        ref[...] += val


def _rms_bwd(name, h, g, dy, res, deps=()):
    s, d = h.shape
    tr = _row_tile(s, d, budget=2 * 1024 * 1024)

    def body(h_ref, g_ref, dy_ref, res_ref, dh_ref, dhb_ref, dg_ref, cs_ref):
        hv = h_ref[...]
        r = lax.rsqrt(jnp.mean(hv * hv, axis=-1, keepdims=True) + RMS_EPS)
        xhat = hv * r
        dyv = dy_ref[...]
        dxh = dyv * g_ref[...]
        dh = res_ref[...] + r * (dxh - xhat * jnp.mean(dxh * xhat, axis=-1, keepdims=True))
        dh_ref[...] = dh
        dhb_ref[...] = dh.astype(BF16)
        first = pl.program_id(0) == 0
        _accumulate(dg_ref, jnp.sum(dyv * xhat, axis=0, keepdims=True), first)
        _accumulate(cs_ref, jnp.sum(dh, axis=0, keepdims=True), first)

    row = pl.BlockSpec((tr, d), lambda i: (i, 0))
    vec = pl.BlockSpec((1, d), lambda i: (0, 0))
    return _pcall(
        body, name=name, out_shape=[_sds((s, d), F32), _sds((s, d), BF16), _sds((1, d), F32), _sds((1, d), F32)],
        grid=(s // tr,), in_specs=[row, vec, row, row], out_specs=[row, row, vec, vec], deps=deps,
    )(h, g.reshape(1, d), dy, res)


def _loss_bwd(name, h, g, target):
    s, d = h.shape
    tr = _row_tile(s, d, budget=1024 * 1024)

    def body(h_ref, g_ref, t_ref, loss_ref, dh_ref, dhb_ref, dg_ref):
        hv = h_ref[...]
        r = lax.rsqrt(jnp.mean(hv * hv, axis=-1, keepdims=True) + RMS_EPS)
        xhat = hv * r
        diff = xhat * g_ref[...] - t_ref[...]
        part = jnp.sum(jnp.sum(diff * diff, axis=1, keepdims=True), axis=0, keepdims=True) * (0.5 / d)
        dyv = diff * (1.0 / d)
        dxh = dyv * g_ref[...]
        dh = r * (dxh - xhat * jnp.mean(dxh * xhat, axis=-1, keepdims=True))
        dh_ref[...] = dh
        dhb_ref[...] = dh.astype(BF16)
        first = pl.program_id(0) == 0
        _accumulate(loss_ref, part, first)
        _accumulate(dg_ref, jnp.sum(dyv * xhat, axis=0, keepdims=True), first)

    row = pl.BlockSpec((tr, d), lambda i: (i, 0))
    vec = pl.BlockSpec((1, d), lambda i: (0, 0))
    one = pl.BlockSpec((1, 1), lambda i: (0, 0))
    return _pcall(
        body, name=name, out_shape=[_sds((1, 1), F32), _sds((s, d), F32), _sds((s, d), BF16), _sds((1, d), F32)],
        grid=(s // tr,), in_specs=[row, vec, row], out_specs=[one, row, row, vec],
    )(h, g.reshape(1, d), target)


def _tril_mask():
    return lax.broadcasted_iota(jnp.int32, (CHUNK, CHUNK), 0) >= lax.broadcasted_iota(jnp.int32, (CHUNK, CHUNK), 1)


def _gmlp_fwd(name, zp, gv, ws, bst):
    s, d2 = zp.shape
    d = d2 // 2
    gw = d // GROUPS

    def body(zp_ref, gv_ref, ws_ref, bst_ref, o_ref):
        u, _ = _gelu_parts(zp_ref[:, :d])
        v, _ = _gelu_parts(zp_ref[:, d:])
        rv = lax.rsqrt(jnp.mean(v * v, axis=-1, keepdims=True) + RMS_EPS)
        vn = (v * rv * gv_ref[...]).astype(BF16)
        tril = _tril_mask()
        for g in range(GROUPS):
            sl = slice(g * gw, (g + 1) * gw)
            wc = jnp.where(tril, ws_ref[g], 0.0).astype(BF16)
            sg = jnp.dot(wc, vn[:, sl], preferred_element_type=F32) + bst_ref[:, g:g + 1]
            o_ref[:, sl] = (u[:, sl] * sg).astype(o_ref.dtype)

    return _pcall(
        body, name=name, out_shape=_sds((s, d), BF16), grid=(s // CHUNK,),
        in_specs=[pl.BlockSpec((CHUNK, d2), lambda i: (i, 0)), pl.BlockSpec((1, d), lambda i: (0, 0)),
                  pl.BlockSpec((GROUPS, CHUNK, CHUNK), lambda i: (0, 0, 0)),
                  pl.BlockSpec((CHUNK, GROUPS), lambda i: (0, 0))],
        out_specs=pl.BlockSpec((CHUNK, d), lambda i: (i, 0)),
    )(zp, gv, ws, bst)


def _gmlp_bwd(name, zp, dgated, gv, ws, bst):
    s, d2 = zp.shape
    d = d2 // 2
    gw = d // GROUPS

    def body(zp_ref, dg_ref, gv_ref, ws_ref, bst_ref, dzp_ref, dws_ref, dbs_ref, dgv_ref, dvn_ref):
        u, gu = _gelu_parts(zp_ref[:, :d])
        v, gvv = _gelu_parts(zp_ref[:, d:])
        rv = lax.rsqrt(jnp.mean(v * v, axis=-1, keepdims=True) + RMS_EPS)
        vhat = v * rv
        vn = (vhat * gv_ref[...]).astype(BF16)
        tril = _tril_mask()
        first = pl.program_id(0) == 0
        ones = jnp.ones((8, gw), F32)

        @pl.when(first)
        def _():
            dws_ref[...] = jnp.zeros_like(dws_ref)
            dbs_ref[...] = jnp.zeros_like(dbs_ref)

        for g in range(GROUPS):
            sl = slice(g * gw, (g + 1) * gw)
            wc = jnp.where(tril, ws_ref[g], 0.0).astype(BF16)
            sg = jnp.dot(wc, vn[:, sl], preferred_element_type=F32) + bst_ref[:, g:g + 1]
            dgs = dg_ref[:, sl]
            ds = dgs * u[:, sl]
            dsb = ds.astype(BF16)
            dzp_ref[:, sl] = (dgs * sg * gu[:, sl]).astype(dzp_ref.dtype)
            dvn_ref[:, sl] = lax.dot_general(wc, dsb, TN, preferred_element_type=F32)
            dw = lax.dot_general(dsb, vn[:, sl], NT, preferred_element_type=F32)
            dws_ref[g] += jnp.where(tril, dw, 0.0)
            dbs_ref[g] += lax.dot_general(ones, ds, NT, preferred_element_type=F32, precision=lax.Precision.HIGHEST)
        dvn = dvn_ref[...]
        dvh = dvn * gv_ref[...]
        dv = rv * (dvh - vhat * jnp.mean(dvh * vhat, axis=-1, keepdims=True))
        dzp_ref[:, d:] = (dv * gvv).astype(dzp_ref.dtype)
        _accumulate(dgv_ref, jnp.sum(dvn * vhat, axis=0, keepdims=True), first)

    return _pcall(
        body, name=name,
        out_shape=[_sds((s, d2), BF16), _sds((GROUPS, CHUNK, CHUNK), F32), _sds((GROUPS, 8, CHUNK), F32),
                   _sds((1, d), F32)],
        grid=(s // CHUNK,),
        in_specs=[pl.BlockSpec((CHUNK, d2), lambda i: (i, 0)), pl.BlockSpec((CHUNK, d), lambda i: (i, 0)),
                  pl.BlockSpec((1, d), lambda i: (0, 0)), pl.BlockSpec((GROUPS, CHUNK, CHUNK), lambda i: (0, 0, 0)),
                  pl.BlockSpec((CHUNK, GROUPS), lambda i: (0, 0))],
        out_specs=[pl.BlockSpec((CHUNK, d2), lambda i: (i, 0)),
                   pl.BlockSpec((GROUPS, CHUNK, CHUNK), lambda i: (0, 0, 0)),
                   pl.BlockSpec((GROUPS, 8, CHUNK), lambda i: (0, 0, 0)), pl.BlockSpec((1, d), lambda i: (0, 0))],
        scratch=[pltpu.VMEM((CHUNK, d), F32)],
    )(zp, dgated, gv, ws, bst)


def _bucket_table():
    dist = np.arange(BLOCK)[:, None] + BLOCK - np.arange(2 * BLOCK)[None, :]
    in_window = (dist >= 0) & (dist < BLOCK)
    dd = np.clip(dist, 0, None)
    max_exact = N_BUCKETS // 2
    dl = np.maximum(dd, 1).astype(np.float32)
    large = max_exact + (np.log(dl / np.float32(max_exact)) / np.float32(math.log(MAX_DISTANCE / max_exact))
                         * np.float32(N_BUCKETS - max_exact)).astype(np.int32)
    large = np.minimum(large, N_BUCKETS - 1)
    bucket = np.where(dd < max_exact, dd, large)
    return np.where(in_window, bucket, -1).astype(np.int32).reshape(1, -1)


def _bias_table(name, rel_bias_t, buckets):
    nh = rel_bias_t.shape[0]
    p = buckets.shape[1]
    tp = 4096

    def body(rb_ref, bk_ref, o_ref):
        bk = bk_ref[...]
        onehot = (lax.broadcasted_iota(jnp.int32, (N_BUCKETS, tp), 0) == bk).astype(F32)
        val = jnp.dot(rb_ref[...], onehot, preferred_element_type=F32, precision=lax.Precision.HIGHEST)
        o_ref[...] = jnp.where(bk >= 0, val, NEG_INF)

    return _pcall(
        body, name=name, out_shape=_sds((nh, p), F32), grid=(p // tp,),
        in_specs=[pl.BlockSpec((nh, N_BUCKETS), lambda i: (0, 0)), pl.BlockSpec((1, tp), lambda i: (0, i))],
        out_specs=pl.BlockSpec((nh, tp), lambda i: (0, i)),
    )(rel_bias_t, buckets)


def _bias_grad(name, dbiases, buckets):
    nh, p = dbiases[0].shape
    n = len(dbiases)
    tp = 4096

    def body(*refs):
        bk_ref, o_ref = refs[n], refs[n + 1]
        onehot = (lax.broadcasted_iota(jnp.int32, (N_BUCKETS, tp), 0) == bk_ref[...]).astype(F32)
        db = refs[0][...]
        for r in refs[1:n]:
            db = db + r[...]
        part = lax.dot_general(onehot, db, NT, preferred_element_type=F32, precision=lax.Precision.HIGHEST)
        _accumulate(o_ref, part, pl.program_id(0) == 0)

    return _pcall(
        body, name=name, out_shape=_sds((N_BUCKETS, nh), F32), grid=(p // tp,),
        in_specs=[pl.BlockSpec((nh, tp), lambda i: (0, i))] * n + [pl.BlockSpec((1, tp), lambda i: (0, i))],
        out_specs=pl.BlockSpec((N_BUCKETS, nh), lambda i: (0, 0)),
    )(*dbiases, buckets)


def _stack_heads(ref, g):
    base = g * KV_GROUP * HEAD_DIM
    return jnp.concatenate([ref[:, base + hh * HEAD_DIM:base + (hh + 1) * HEAD_DIM] for hh in range(KV_GROUP)], axis=0)


def _attn_probs(q, kb, bias, s_ref, first_head):
    penalty = jnp.where(pl.program_id(1) > 0, 0.0, NEG_INF).astype(F32)
    col = lax.broadcasted_iota(jnp.int32, (1, 2 * BLOCK), 1)
    bias = bias.reshape(KV_GROUP * BLOCK, 2 * BLOCK) + jnp.where(col < BLOCK, penalty, 0.0)
    sink = jnp.concatenate([jnp.full((BLOCK, 1), s_ref[first_head + hh], F32) for hh in range(KV_GROUP)], axis=0)
    s = lax.dot_general(q, kb, NT, preferred_element_type=F32) * 0.125 + bias
    m = jnp.maximum(jnp.max(s, axis=-1, keepdims=True), sink)
    p = jnp.exp(s - m)
    es = jnp.exp(sink - m)
    inv = 1.0 / (jnp.sum(p, axis=-1, keepdims=True) + es)
    return p * inv, es * inv


def _attn_specs(ng):
    gq = ng * KV_GROUP * HEAD_DIM
    q_spec = pl.BlockSpec((BLOCK, gq), lambda kh, i: (i, kh))
    prev = pl.BlockSpec((ng, BLOCK, HEAD_DIM), lambda kh, i: (kh, jnp.maximum(i - 1, 0), 0))
    cur = pl.BlockSpec((ng, BLOCK, HEAD_DIM), lambda kh, i: (kh, i, 0))
    bias = pl.BlockSpec((ng * KV_GROUP, BLOCK, 2 * BLOCK), lambda kh, i: (kh, 0, 0))
    smem = pl.BlockSpec(memory_space=pltpu.SMEM)
    probs = pl.BlockSpec((ng, None, KV_GROUP * BLOCK, 2 * BLOCK), lambda kh, i: (kh, i, 0, 0))
    sink_probs = pl.BlockSpec((ng, None, KV_GROUP * BLOCK, 1), lambda kh, i: (kh, i, 0, 0))
    return q_spec, prev, cur, bias, smem, probs, sink_probs


def _kv_heads_per_step(nkv):
    return 2 if nkv % 2 == 0 else 1


def _attn_fwd(name, q, k, v, bias, sinks, deps=()):
    s, dq = q.shape
    nkv = k.shape[0]
    ng = 1
    q_spec, prev, cur, bias_spec, smem, p_spec, ps_spec = _attn_specs(ng)

    def body(q_ref, kp_ref, kc_ref, vp_ref, vc_ref, b_ref, s_ref, o_ref, p_ref, ps_ref):
        for g in range(ng):
            kb = jnp.concatenate([kp_ref[g], kc_ref[g]], axis=0)
            vb = jnp.concatenate([vp_ref[g], vc_ref[g]], axis=0)
            p, ps = _attn_probs(_stack_heads(q_ref, g), kb, b_ref[g * KV_GROUP:(g + 1) * KV_GROUP], s_ref,
                                (pl.program_id(0) * ng + g) * KV_GROUP)
            pb = p.astype(BF16)
            p_ref[g] = pb
            ps_ref[g] = ps
            o = jnp.dot(pb, vb, preferred_element_type=F32)
            for hh in range(KV_GROUP):
                col = (g * KV_GROUP + hh) * HEAD_DIM
                o_ref[:, col:col + HEAD_DIM] = o[hh * BLOCK:(hh + 1) * BLOCK].astype(o_ref.dtype)

    return _pcall(
        body, name=name,
        out_shape=[_sds((s, dq), BF16), _sds((nkv, s // BLOCK, KV_GROUP * BLOCK, 2 * BLOCK), BF16),
                   _sds((nkv, s // BLOCK, KV_GROUP * BLOCK, 1), F32)],
        grid=(nkv // ng, s // BLOCK),
        in_specs=[q_spec, prev, cur, prev, cur, bias_spec, smem], out_specs=[q_spec, p_spec, ps_spec], deps=deps,
    )(q, k, k, v, v, bias, sinks)


def _attn_bwd(name, q, k, v, do, probs, sink_probs):
    s, dq = q.shape
    nkv = k.shape[0]
    ng = _kv_heads_per_step(nkv)
    gq = ng * KV_GROUP * HEAD_DIM
    q_spec, prev, cur, bias_spec, _, p_spec, ps_spec = _attn_specs(ng)

    def body(q_ref, do_ref, kp_ref, kc_ref, vp_ref, vc_ref, p_ref, ps_ref,
             dq_ref, dbq_ref, dkc_ref, dkp_ref, dvc_ref, dvp_ref, dbias_ref, dsink_ref):
        @pl.when(pl.program_id(1) == 0)
        def _():
            dbias_ref[...] = jnp.zeros_like(dbias_ref)
            dsink_ref[...] = jnp.zeros_like(dsink_ref)
            dbq_ref[...] = jnp.zeros_like(dbq_ref)

        for g in range(ng):
            kb = jnp.concatenate([kp_ref[g], kc_ref[g]], axis=0)
            vb = jnp.concatenate([vp_ref[g], vc_ref[g]], axis=0)
            q, do = _stack_heads(q_ref, g), _stack_heads(do_ref, g)
            pb = p_ref[g]
            p = pb.astype(F32)
            dp = lax.dot_general(do, vb, NT, preferred_element_type=F32)
            delta = jnp.sum(p * dp, axis=-1, keepdims=True)
            ds = p * (dp - delta)
            dsb = ds.astype(BF16)
            dq = jnp.dot(dsb, kb, preferred_element_type=F32) * 0.125
            dsk = -(ps_ref[g] * delta)
            for hh in range(KV_GROUP):
                col, rows = (g * KV_GROUP + hh) * HEAD_DIM, slice(hh * BLOCK, (hh + 1) * BLOCK)
                dq_ref[:, col:col + HEAD_DIM] = dq[rows].astype(dq_ref.dtype)
                dbq_ref[:, col:col + HEAD_DIM] += jnp.sum(dq[rows], axis=0, keepdims=True)
                dsink_ref[g, :, hh:hh + 1] += jnp.sum(dsk[rows], axis=0, keepdims=True)
            dkb = lax.dot_general(dsb, q, TN, preferred_element_type=F32) * 0.125
            dvb = lax.dot_general(pb, do, TN, preferred_element_type=F32)
            dkp_ref[g], dkc_ref[g] = dkb[:BLOCK], dkb[BLOCK:]
            dvp_ref[g], dvc_ref[g] = dvb[:BLOCK], dvb[BLOCK:]
            dbias_ref[g * KV_GROUP:(g + 1) * KV_GROUP] += ds.reshape(KV_GROUP, BLOCK, 2 * BLOCK)

    kv_out = _sds((nkv, s, HEAD_DIM), F32)
    return _pcall(
        body, name=name,
        out_shape=[_sds((s, dq), BF16), _sds((1, dq), F32), kv_out, kv_out, kv_out, kv_out,
                   _sds((nkv * KV_GROUP, BLOCK, 2 * BLOCK), F32), _sds((nkv, 1, KV_GROUP), F32)],
        grid=(nkv // ng, s // BLOCK),
        in_specs=[q_spec, q_spec, prev, cur, prev, cur, p_spec, ps_spec],
        out_specs=[q_spec, pl.BlockSpec((1, gq), lambda kh, i: (0, kh)), cur, cur, cur, cur, bias_spec,
                   pl.BlockSpec((ng, 1, KV_GROUP), lambda kh, i: (kh, 0, 0))],
    )(q, do, k, k, v, v, probs, sink_probs)


def _kv_grad(name, parts):
    nkv, s, _ = parts[0][0].shape
    nb = s // BLOCK
    w = 2 * nkv * HEAD_DIM
    n = len(parts)

    def body(*refs):
        o_ref, cs_ref = refs[4 * n], refs[4 * n + 1]
        i = pl.program_id(0)
        keep = jnp.where(i < nb - 1, 1.0, 0.0).astype(F32)

        @pl.when(i == 0)
        def _():
            cs_ref[...] = jnp.zeros_like(cs_ref)

        for which in range(2):
            for hh in range(nkv):
                val = None
                for l in range(n):
                    cur_ref, nxt_ref = refs[4 * l + 2 * which], refs[4 * l + 2 * which + 1]
                    t = cur_ref[hh] + keep * nxt_ref[hh]
                    val = t if val is None else val + t
                sl = slice((which * nkv + hh) * HEAD_DIM, (which * nkv + hh + 1) * HEAD_DIM)
                o_ref[:, sl] = val.astype(o_ref.dtype)
                cs_ref[:, sl] += jnp.sum(val, axis=0, keepdims=True)

    cur = pl.BlockSpec((nkv, BLOCK, HEAD_DIM), lambda i: (0, i, 0))
    nxt = pl.BlockSpec((nkv, BLOCK, HEAD_DIM), lambda i: (0, jnp.minimum(i + 1, nb - 1), 0))
    flat = [a for p in parts for a in p]
    return _pcall(
        body, name=name, out_shape=[_sds((s, w), BF16), _sds((1, w), F32)], grid=(nb,),
        in_specs=[cur, nxt] * (2 * n),
        out_specs=[pl.BlockSpec((BLOCK, w), lambda i: (i, 0)), pl.BlockSpec((1, w), lambda i: (0, 0))],
    )(*flat)


def _adamw_math(w, g, m, v):
    m = ADAM_B1 * m + (1.0 - ADAM_B1) * g
    v = ADAM_B2 * v + (1.0 - ADAM_B2) * (g * g)
    m_hat = m / (1.0 - ADAM_B1 ** ADAM_STEP)
    v_hat = v / (1.0 - ADAM_B2 ** ADAM_STEP)
    delta = -ADAM_LR * (m_hat / (jnp.sqrt(v_hat) + ADAM_EPS) + ADAM_WD * w)
    return delta, m, v


def _adamw_shard(name, w, m, v, parts, row0, layer, prev, deps=(), own=None, where=None):
    _, r, wd = w.shape
    tr = _row_tile(r, wd, budget=3 * 512 * 1024)
    assert row0 % tr == 0

    def step(w_ref, m_ref, v_ref, g, g_ref, d_ref, nm_ref, nv_ref):
        delta, nm, nv = _adamw_math(w_ref[...], g, m_ref[...], v_ref[...])
        g_ref[...], d_ref[...], nm_ref[...], nv_ref[...] = g, delta, nm, nv

    out = _sds(w.shape, F32)
    if own is None:
        def body(w_ref, m_ref, v_ref, p_ref, a0, a1, a2, a3, g_ref, d_ref, nm_ref, nv_ref):
            g = p_ref[0].astype(F32)
            for k in range(1, NCHIP):
                g = g + p_ref[k].astype(F32)
            step(w_ref, m_ref, v_ref, g, g_ref, d_ref, nm_ref, nv_ref)

        par = pl.BlockSpec((None, tr, wd), lambda i: (layer, i, 0))
        return _pcall(
            body, name=name, out_shape=[out, out, out, out], grid=(r // tr,),
            in_specs=[par, par, par, pl.BlockSpec((NCHIP, tr, wd), lambda i: (0, row0 // tr + i, 0)), ANY, ANY, ANY, ANY],
            out_specs=[par, par, par, par], aliases={4: 0, 5: 1, 6: 2, 7: 3}, deps=deps,
        )(w, m, v, parts, *prev)

    def body(where_ref, w_ref, m_ref, v_ref, p_ref, o_ref, a0, a1, a2, a3, g_ref, d_ref, nm_ref, nv_ref):
        mine = lax.broadcasted_iota(jnp.int32, (tr, wd), 0) * 0 + where_ref[1]
        g = None
        for k in range(NCHIP):
            t = jnp.where(mine == k, o_ref[...], p_ref[k]).astype(F32)
            g = t if g is None else g + t
        step(w_ref, m_ref, v_ref, g, g_ref, d_ref, nm_ref, nv_ref)

    par = pl.BlockSpec((None, tr, wd), lambda i, wr: (layer, i, 0))
    return _pcall(
        body, name=name, out_shape=[out, out, out, out], grid=(r // tr,), prefetch=1,
        in_specs=[par, par, par, pl.BlockSpec((NCHIP, tr, wd), lambda i, wr: (0, row0 // tr + i, 0)),
                  pl.BlockSpec((None, tr, wd), lambda i, wr: (wr[1], row0 // tr + i, 0)), ANY, ANY, ANY, ANY],
        out_specs=[par, par, par, par], aliases={6: 0, 7: 1, 8: 2, 9: 3}, deps=deps,
    )(where, w, m, v, parts, own, *prev)


def _sum_devices(name, gathered):
    _, r, wd = gathered.shape

    def body(g_ref, o_ref):
        acc = g_ref[0]
        for k in range(1, NDEV):
            acc = acc + g_ref[k]
        o_ref[...] = acc

    return _pcall(body, name=name, out_shape=_sds((r, wd), F32), grid=(1,),
                  in_specs=[pl.BlockSpec((NDEV, r, wd), lambda i: (0, 0, 0))],
                  out_specs=pl.BlockSpec((r, wd), lambda i: (0, 0)))(gathered)


def _adamw_flat(name, w, g, m, v):
    shape = w.shape

    def body(w_ref, g_ref, m_ref, v_ref, d_ref, nm_ref, nv_ref):
        d_ref[...], nm_ref[...], nv_ref[...] = _adamw_math(w_ref[...], g_ref[...], m_ref[...], v_ref[...])

    spec = pl.BlockSpec(shape, lambda i: (0, 0))
    out = _sds(shape, F32)
    return _pcall(body, name=name, out_shape=[out, out, out], grid=(1,), in_specs=[spec] * 4,
                  out_specs=[spec] * 3)(w, g, m, v)


def _cast_into(name, src, layer, buf, row0, me):
    _, r, wd = src.shape
    tr = _row_tile(r, wd)
    assert row0 % tr == 0

    def body(me_ref, s_ref, b_ref, o_ref):
        o_ref[...] = s_ref[...].astype(o_ref.dtype)

    return _pcall(
        body, name=name, out_shape=_sds(buf.shape, buf.dtype), grid=(r // tr,), prefetch=1,
        in_specs=[pl.BlockSpec((None, tr, wd), lambda i, mr: (layer, i, 0)), ANY],
        out_specs=pl.BlockSpec((None, tr, wd), lambda i, mr: (mr[0], row0 // tr + i, 0)), aliases={2: 0},
    )(me, src, buf)


def _pack(arrays):
    rows = []
    for a in arrays:
        flat = a.reshape(-1).astype(F32)
        pad = (-flat.shape[0]) % 1024
        rows.append(jnp.pad(flat, (0, pad)).reshape(-1, 128))
    return jnp.concatenate(rows, axis=0)


def _unpack(packed, shapes):
    out, r = [], 0
    for shp in shapes:
        n = int(np.prod(shp))
        nr = (n + 1023) // 1024 * 8
        out.append(packed[r:r + nr].reshape(-1)[:n].reshape(shp))
        r += nr
    return out


def kernel(x, mix_norm, ffn_norm, a_w_in, a_norm_v, a_w_s, a_b_s, a_w_out, kv_norm, w_kv, b_kv, b_w_q, b_b_q, b_sinks, b_w_o, b_b_o, rel_bias, ffn_w_gate, ffn_w_up, ffn_w_down, final_norm, loss_target, m_mix_norm, m_ffn_norm, m_a_w_in, m_a_norm_v, m_a_w_s, m_a_b_s, m_a_w_out, m_kv_norm, m_w_kv, m_b_kv, m_b_w_q, m_b_b_q, m_b_sinks, m_b_w_o, m_b_b_o, m_rel_bias, m_ffn_w_gate, m_ffn_w_up, m_ffn_w_down, m_final_norm, v_mix_norm, v_ffn_norm, v_a_w_in, v_a_norm_v, v_a_w_s, v_a_b_s, v_a_w_out, v_kv_norm, v_w_kv, v_b_kv, v_b_w_q, v_b_b_q, v_b_sinks, v_b_w_o, v_b_b_o, v_rel_bias, v_ffn_w_gate, v_ffn_w_up, v_ffn_w_down, v_final_norm):
    _, S, D = x.shape
    LA, LB, L = a_w_in.shape[0], b_w_q.shape[0], ffn_w_gate.shape[0]
    F = ffn_w_gate.shape[2]
    DS = D // NDEV
    ZC = a_w_in.shape[2]
    KVW = w_kv.shape[1]
    NKV = KVW // (2 * HEAD_DIM)
    NH = D // HEAD_DIM
    assert ZC * NDEV == 2 * D and NH == NKV * KV_GROUP and S % BLOCK == 0
    TM = min(1024, S)
    TN_ = min(1024, D)
    TS = min(512, D)
    KC = 4

    ix, iy, ic = lax.axis_index("x"), lax.axis_index("y"), lax.axis_index("c")
    me = (4 * ix + 2 * iy + ic).astype(jnp.int32)
    me1 = me.reshape(1)
    where = jnp.stack([ic, 2 * ix + iy]).astype(jnp.int32)

    def tr3(a):
        return jnp.transpose(a, (0, 2, 1))

    gate_t, up_t = tr3(ffn_w_gate), tr3(ffn_w_up)
    w_kv3 = w_kv.reshape((1,) + w_kv.shape)

    def layer_arrays(l):
        arrs = [("gu", 2 * F, D, [(gate_t, l, 0), (up_t, l, F)]), ("down", F, D, [(ffn_w_down, l, 0)])]
        if l < LA:
            arrs += [("win", D, ZC, [(a_w_in, l, 0)]), ("wout", DS, D, [(a_w_out, l, 0)])]
            if l == LA - 1:
                arrs.append(("wkv", DS, KVW, [(w_kv3, 0, 0)]))
        else:
            i_b = l - LA
            arrs.append(("wqo", 2 * DS, D, [(b_w_q, i_b, 0), (b_w_o, i_b, DS)]))
        return arrs

    gathers = []

    def gather_begin(g_idx, deps):
        g = gathers[g_idx]
        g["send"], g["recv"], g["bufs"], g["token"] = _relay_start(f"relay_start{g_idx}", g["bufs"], deps)

    for l in range(L):
        mixer, ffn = dict(keys=[], bufs=[]), dict(keys=[], bufs=[])
        for key, rows, width, sources in layer_arrays(l):
            buf = lax.empty((NDEV, rows, width), BF16)
            for si, (src, li, row0) in enumerate(sources):
                buf = _cast_into(f"cast_{key}{l}_{si}", src, li, buf, row0, me1)
            group = ffn if key in ("gu", "down") else mixer
            group["keys"].append(key)
            group["bufs"].append(buf)
        gathers += [mixer, ffn]
        if l == 0:
            nv_rows = _pack([a_norm_v])
            nv = _cast_into("put_norm_v", nv_rows.reshape((1,) + nv_rows.shape), 0,
                            lax.empty((NDEV,) + nv_rows.shape, F32), 0, me1)
            nv_send, nv_recv, nv_bufs, token = _gather_start("gather_norm_v_start", [nv], [])
            gather_begin(0, [token])

    def gather_relay(g_idx, deps):
        g = gathers[g_idx]
        g["fsend"], g["frecv"], g["bufs"], tok = _relay_neighbors(f"relay_neighbors{g_idx}", g["bufs"], g["send"],
                                                                  g["recv"], deps)
        if g_idx + 1 < len(gathers):
            gather_begin(g_idx + 1, [tok])
            tok = gathers[g_idx + 1]["token"]
        return tok

    def finish_gather(g_idx, deps):
        g = gathers[g_idx]
        gsend, grecv, bufs = _relay_diagonal(f"relay_diagonal{g_idx}", g["bufs"], g["fsend"], g["frecv"], deps)
        bufs = _relay_finish(f"relay_finish{g_idx}", bufs, g["send"], g["recv"], g["fsend"], g["frecv"], gsend, grecv)
        return dict(zip(g["keys"], bufs))

    token = gather_relay(0, [gathers[0]["token"]] + [b for g in gathers[1:] for b in g["bufs"]])

    buckets = jnp.asarray(_bucket_table())
    bias = _bias_table("bias_table", rel_bias.T, buckets).reshape(NH, BLOCK, 2 * BLOCK)

    def rows_full(tm):
        return pl.BlockSpec((tm, D), lambda i, j: (i, 0))

    def tile(tm, tn):
        return pl.BlockSpec((tm, tn), lambda i, j: (i, j))

    vec_tile = pl.BlockSpec((1, TN_), lambda i, j: (0, j))

    def ffn_forward(l, wl, h_mid, tag, deps):
        xf = _rms_fwd(f"ffn_norm_fwd{tag}", h_mid, ffn_norm[l], deps=deps)

        def ep(parts, ex, outs):
            a, b = parts
            sg = jax.nn.sigmoid(a)
            silu = a * sg
            outs[0][0] = (b * (sg * (1.0 + a * (1.0 - sg)))).astype(BF16)
            outs[0][1] = silu.astype(BF16)
            outs[1][...] = (silu * b).astype(BF16)

        ab, hid = _gemm(
            f"ffn_up{tag}", (S // TM, NDEV),
            [(xf, rows_full(TM)),
             (wl["gu"], pl.BlockSpec((None, F, D), lambda i, e: (e, 0, 0))),
             (wl["gu"], pl.BlockSpec((None, F, D), lambda i, e: (e, 1, 0)))],
            [(0, 1, NT), (0, 2, NT)], [],
            [(_sds((2, NDEV, S, F), BF16), pl.BlockSpec((2, None, TM, F), lambda i, e: (0, e, i, 0))),
             (_sds((NDEV, S, F), BF16), pl.BlockSpec((None, TM, F), lambda i, e: (e, i, 0)))],
            ep, separate=True)
        (h_out,) = _gemm(
            f"ffn_down{tag}", (S // TM, D // TN_, NDEV // KC),
            [(hid, pl.BlockSpec((KC, TM, F), lambda i, j, k: (k, i, 0))),
             (wl["down"], pl.BlockSpec((KC, F, TN_), lambda i, j, k: (k, 0, j)))],
            [(0, 1, NN, _pick(c), _pick(c)) for c in range(KC)],
            [(h_mid, pl.BlockSpec((TM, TN_), lambda i, j, k: (i, j)))],
            [(_sds((S, D), F32), pl.BlockSpec((TM, TN_), lambda i, j, k: (i, j)))],
            _store_add_extra, nk=NDEV // KC, acc_shape=(TM, TN_))
        return dict(h_mid=h_mid, xf=xf, ab=ab, hid=hid), h_out

    def stacked_rows_gemm(name, a, wmat, blk, extras, ep, out_dtype, deps=()):
        return _gemm(
            name, (S // TM, D // TN_),
            [(a, rows_full(TM)), (wmat, pl.BlockSpec((NDEV, DS, TN_), lambda i, j: (0, blk, j)))],
            [(0, 1, NN, None, _stacked)], extras,
            [(_sds((S, D), out_dtype), tile(TM, TN_))], ep, deps=deps)[0]

    def back_rows_gemm(name, a, wmat, blk, out_dtype, deps=()):
        return _gemm(
            name, (S // TM, NDEV // KC),
            [(a, rows_full(TM)), (wmat, pl.BlockSpec((KC, DS, D), lambda i, e: (e, blk, 0)))],
            [(0, 1, NT, None, _stacked)], [],
            [(_sds((S, D), out_dtype), pl.BlockSpec((TM, KC * DS), lambda i, e: (i, e)))], _store, deps=deps)[0]

    def grad_rows_gemm(name, act, d_bf, buf, blk):
        return _gemm(
            name, (NDEV,),
            [(act, pl.BlockSpec((S, DS), lambda e: (0, e))), (d_bf, pl.BlockSpec((S, D), lambda e: (0, 0)))],
            [(0, 1, TN)], [(buf, ANY)],
            [(_sds(buf.shape, BF16), pl.BlockSpec((None, DS, D), lambda e: (e, blk, 0)))],
            _store, aliases={2: 0})[0]

    saved, weights = [], []
    h = x.reshape(S, D)
    k_heads = v_heads = hn = h_kv = norm_v = None
    for layer in range(L):
        wl = finish_gather(2 * layer, [token] if layer == 0 else [h])
        weights.append(wl)
        if layer == 0:
            nv_fsend, nv_frecv, nv_bufs = _gather_forward("gather_norm_v_forward", nv_bufs, nv_send, nv_recv,
                                                          [wl["win"]])
            (nv_all,) = _gather_finish("gather_norm_v_finish", nv_bufs, nv_send, nv_recv, nv_fsend, nv_frecv)
            norm_v = jnp.transpose(nv_all.reshape(NDEV, -1)[:, :LA * DS].reshape(NDEV, LA, DS), (1, 0, 2)).reshape(LA, D)
        sv = dict(h_in=h)
        xn = _rms_fwd(f"mix_norm_fwd{layer}", h, mix_norm[layer])
        sv["xn"] = xn
        if layer < LA:
            i_a = layer
            (zp,) = _gemm(
                f"gmlp_in{layer}", (S // TM, NDEV),
                [(xn, rows_full(TM)), (wl["win"], pl.BlockSpec((None, D, ZC), lambda i, e: (e, 0, 0)))],
                [(0, 1, NN)], [], [(_sds((S, 2 * D), F32), pl.BlockSpec((TM, ZC), lambda i, e: (i, e)))], _store)
            bst = a_b_s[i_a].T
            gated = _gmlp_fwd(f"gmlp_gate{layer}", zp, norm_v[i_a].reshape(1, D), a_w_s[i_a], bst)
            sv.update(zp=zp, gated=gated, bst=bst)
            relay_token = gather_relay(2 * layer + 1, [gated])
            h_mid = stacked_rows_gemm(f"gmlp_out{layer}", gated, wl["wout"], 0, [(h, tile(TM, TN_))],
                                      _store_add_extra, F32, deps=[relay_token])
        else:
            i_b = layer - LA
            q = stacked_rows_gemm(f"attn_q{layer}", xn, wl["wqo"], 0, [(b_b_q[i_b].reshape(1, D), vec_tile)],
                                  _store_add_extra, BF16)
            relay_token = gather_relay(2 * layer + 1, [q])
            attn, probs, sink_probs = _attn_fwd(f"attn_fwd{layer}", q, k_heads, v_heads, bias, b_sinks[i_b],
                                                deps=[relay_token])
            sv.update(q=q, attn=attn, probs=probs, sink_probs=sink_probs)
            h_mid = stacked_rows_gemm(f"attn_o{layer}", attn, wl["wqo"], 1,
                                      [(h, tile(TM, TN_)), (b_b_o[i_b].reshape(1, D), vec_tile)],
                                      _store_add_extra, F32)
        wl.update(finish_gather(2 * layer + 1, [h_mid]))
        ffn_deps = [gather_relay(2 * layer + 2, [wl["down"]])] if layer + 1 < L else []
        fsv, h = ffn_forward(layer, wl, h_mid, str(layer), ffn_deps)
        sv.update(fsv)
        saved.append(sv)
        if layer == LA - 1:
            h_kv = h
            hn = _rms_fwd("kv_norm_fwd", h, kv_norm)

            def kv_ep(acc, ex, outs):
                val = acc + ex[0][...]
                for hh in range(NKV):
                    outs[0][hh] = val[:, hh * HEAD_DIM:(hh + 1) * HEAD_DIM].astype(BF16)
                    outs[1][hh] = val[:, (NKV + hh) * HEAD_DIM:(NKV + hh + 1) * HEAD_DIM].astype(BF16)

            k_heads, v_heads = _gemm(
                "kv_proj", (S // TM,),
                [(hn, pl.BlockSpec((TM, D), lambda i: (i, 0))),
                 (wl["wkv"], pl.BlockSpec((NDEV, DS, KVW), lambda i: (0, 0, 0)))],
                [(0, 1, NN, None, _stacked)], [(b_kv.reshape(1, KVW), pl.BlockSpec((1, KVW), lambda i: (0, 0)))],
                [(_sds((NKV, S, HEAD_DIM), BF16), pl.BlockSpec((NKV, TM, HEAD_DIM), lambda i: (0, i, 0)))] * 2,
                kv_ep)

    loss11, d, d_bf, g_final = _loss_bwd("loss_bwd", h, final_norm, loss_target.reshape(S, D))
    loss = lax.psum(loss11[0, 0], AXES)

    g_mix, g_ffn = [None] * L, [None] * L
    g_ws, g_bs, g_nv = [None] * LA, [None] * LA, [None] * LA
    g_bq, g_sink, g_bo = [None] * LB, [None] * LB, [None] * LB
    dbiases = []
    kv_parts = []
    g_kvn = g_bkv = None
    exchanges = [[] for _ in range(L)]
    pending = None
    grads_wkv = None
    newest = []

    def new_grads(l):
        return {key: lax.empty((NDEV, rows, width), BF16) for key, rows, width, _ in layer_arrays(l)}

    def exchange_begin(tag, l, gl, keys):
        grads = [gl[k] for k in keys]
        lands = [lax.empty((NCHIP,) + g.shape[1:], BF16) for g in grads]
        send, recv, grads, lands, tok = _sibling_start(f"rs_sibling_start{tag}", grads, lands, [])
        newest[:] = [tok]
        return dict(tag=tag, layer=l, keys=keys, grads=grads, lands=lands, send=send, recv=recv)

    def exchange_middle(st, dep):
        tag = st["tag"]
        grads, lands = _sibling_finish(f"rs_sibling_finish{tag}", st["grads"], st["lands"], st["send"], st["recv"], [dep])
        sums, own = [], []
        for t, key in enumerate(st["keys"]):
            s_, o_ = _pair_sum(f"pair_sum_{key}{tag}", grads[t], lands[t], where)
            sums.append(s_)
            own.append(o_)
        send, recv, sums, own, tok = _chips_start(f"rs_chips_start{tag}", sums, own, [])
        newest[:] = [tok]
        st.update(sums=sums, own=own, send2=send, recv2=recv)
        exchanges[st["layer"]].append(st)

    def exchange_end(st, dep):
        sums, lands = _chips_finish(f"rs_chips_finish{st['tag']}", st["sums"], st["own"], st["send2"], st["recv2"], [dep])
        own = dict(zip(st["keys"], sums)) if st.get("direct") else {k: None for k in st["keys"]}
        return dict(zip(st["keys"], lands)), own

    for layer in reversed(range(L)):
        sv, wl = saved[layer], weights[layer]
        tag = str(layer)
        gl = new_grads(layer)
        if grads_wkv is not None and layer == LA - 1:
            gl["wkv"] = grads_wkv
        def dhid_ep(acc, ex, outs):
            outs[0][0] = (acc * ex[0][0].astype(F32)).astype(BF16)
            outs[0][1] = (acc * ex[0][1].astype(F32)).astype(BF16)

        ab_spec = pl.BlockSpec((2, None, TM, F), lambda i, e: (0, e, i, 0))
        (dab,) = _gemm(
            f"ffn_dhid{tag}", (S // TM, NDEV),
            [(d_bf, rows_full(TM)), (wl["down"], pl.BlockSpec((None, F, D), lambda i, e: (e, 0, 0)))],
            [(0, 1, NT)], [(sv["ab"], ab_spec)], [(_sds((2, NDEV, S, F), BF16), ab_spec)], dhid_ep,
            deps=list(newest))
        if pending:
            exchange_middle(pending, dab)
        act_kinds = [SHARDS, WHOLE, SHARDS2, WHOLE]
        act_lands = [lax.empty((NCHIP, S, F), BF16), lax.empty((S, D), BF16), lax.empty((2, NCHIP, S, F), BF16),
                     lax.empty((S, D), BF16)]
        a_send, a_recv, act, act_lands, tok = _sibling_start(f"act_start{tag}", [sv["hid"], d_bf, dab, sv["xf"]], act_lands,
                                                             list(newest), act_kinds)
        newest[:] = [tok]
        (dxf,) = _gemm(
            f"ffn_dx{tag}", (S // TM, D // TN_, 2 * NDEV // KC),
            [(act[2].reshape(2 * NDEV // KC, KC, S, F), pl.BlockSpec((None, KC, TM, F), lambda i, j, k: (k, 0, i, 0))),
             (wl["gu"], pl.BlockSpec((KC, F, TN_), lambda i, j, k: (k % (NDEV // KC), k // (NDEV // KC), j)))],
            [(0, 1, NN, _pick(c), _pick(c)) for c in range(KC)], [],
            [(_sds((S, D), F32), pl.BlockSpec((TM, TN_), lambda i, j, k: (i, j)))],
            _store, nk=2 * NDEV // KC, acc_shape=(TM, TN_), deps=list(newest))
        (hid_o, dout_o, dab_o, xf_o), (hid_s, dout_s, dab_s, xf_s) = _sibling_finish(
            f"act_finish{tag}", act, act_lands, a_send, a_recv, [dxf], act_kinds)
        (p_down,) = _gemm(
            f"ffn_dwdown{tag}", (NCHIP, D // TN_),
            [(hid_o.reshape(NCHIP, 2, S, F), pl.BlockSpec((None, None, S, F), lambda k, j, wr: (k, wr[0], 0, 0))),
             (dout_o, pl.BlockSpec((S, TN_), lambda k, j, wr: (0, j))),
             (hid_s, pl.BlockSpec((None, S, F), lambda k, j, wr: (k, 0, 0))),
             (dout_s, pl.BlockSpec((S, TN_), lambda k, j, wr: (0, j)))],
            [(0, 1, TN), (2, 3, TN)], [],
            [(_sds((NCHIP, F, D), BF16), pl.BlockSpec((None, F, TN_), lambda k, j, wr: (k, 0, j)))],
            _store, prefetch=[where])
        (p_gu,) = _gemm(
            f"ffn_dwup{tag}", (2, NCHIP, D // TN_),
            [(dab_o.reshape(2, NCHIP, 2, S, F),
              pl.BlockSpec((None, None, None, S, F), lambda w, k, j, wr: (w, k, wr[0], 0, 0))),
             (xf_o, pl.BlockSpec((S, TN_), lambda w, k, j, wr: (0, j))),
             (dab_s, pl.BlockSpec((None, None, S, F), lambda w, k, j, wr: (w, k, 0, 0))),
             (xf_s, pl.BlockSpec((S, TN_), lambda w, k, j, wr: (0, j)))],
            [(0, 1, TN), (2, 3, TN)], [],
            [(_sds((NCHIP, 2 * F, D), BF16), pl.BlockSpec((None, F, TN_), lambda w, k, j, wr: (k, w, j)))],
            _store, prefetch=[where])
        send2, recv2, sums, own, tok = _chips_start(
            f"rs_chips_start_ffn{tag}", [p_gu, p_down], [lax.empty(p_gu.shape, BF16), lax.empty(p_down.shape, BF16)], [])
        newest[:] = [tok]
        exchanges[layer].append(dict(tag=f"_ffn{tag}", layer=layer, keys=["gu", "down"], sums=sums, own=own, send2=send2,
                                     recv2=recv2, direct=True))
        d, d_bf, g_ffn[layer], colsum = _rms_bwd(f"ffn_norm_bwd{tag}", sv["h_mid"], ffn_norm[layer], dxf, d,
                                                 deps=list(newest))
        if layer < LA:
            i_a = layer
            dgated = back_rows_gemm(f"gmlp_dgated{tag}", d_bf, wl["wout"], 0, F32)
            gl["wout"] = grad_rows_gemm(f"gmlp_dwout{tag}", sv["gated"], d_bf, gl["wout"], 0)
            dzp, g_ws[i_a], dbs, g_nv[i_a] = _gmlp_bwd(f"gmlp_bwd{tag}", sv["zp"], dgated,
                                                       norm_v[i_a].reshape(1, D), a_w_s[i_a], sv["bst"])
            g_bs[i_a] = dbs[:, 0, :]
            gl["win"] = _grad_cols(f"gmlp_dwin{tag}", sv["xn"], dzp, gl["win"], TS)
            (dxn,) = _gemm(
                f"gmlp_dx{tag}", (S // TM, D // TN_, NDEV // KC),
                [(dzp, pl.BlockSpec((TM, KC * ZC), lambda i, j, k: (i, k))),
                 (wl["win"], pl.BlockSpec((KC, TN_, ZC), lambda i, j, k: (k, j, 0)))],
                [(0, 1, NT, _cols(c, ZC), _pick(c)) for c in range(KC)], [],
                [(_sds((S, D), F32), pl.BlockSpec((TM, TN_), lambda i, j, k: (i, j)))],
                _store, nk=NDEV // KC, acc_shape=(TM, TN_))
        else:
            i_b = layer - LA
            g_bo[i_b] = colsum
            dattn = back_rows_gemm(f"attn_dout{tag}", d_bf, wl["wqo"], 1, BF16)
            gl["wqo"] = grad_rows_gemm(f"attn_dwo{tag}", sv["attn"], d_bf, gl["wqo"], 1)
            dq, g_bq[i_b], dkc, dkp, dvc, dvp, dbias, dsink = _attn_bwd(
                f"attn_bwd{tag}", sv["q"], k_heads, v_heads, dattn, sv["probs"], sv["sink_probs"])
            kv_parts.append((dkc, dkp, dvc, dvp))
            g_sink[i_b] = dsink.reshape(NH)
            dbiases.append(dbias.reshape(NH, BLOCK * 2 * BLOCK))
            gl["wqo"] = grad_rows_gemm(f"attn_dwq{tag}", sv["xn"], dq, gl["wqo"], 0)
            dxn = back_rows_gemm(f"attn_dx{tag}", dq, wl["wqo"], 0, F32)
        d, d_bf, g_mix[layer], _ = _rms_bwd(f"mix_norm_bwd{tag}", sv["h_in"], mix_norm[layer], dxn, d)
        pending = exchange_begin(f"_mix{tag}", layer, gl, [k for k in gl if k not in ("gu", "down")])
        if layer == LA:
            wkv = weights[LA - 1]["wkv"]
            dkv, g_bkv = _kv_grad("kv_grad", kv_parts)
            (grads_wkv,) = _gemm(
                "kv_dw", (NDEV,),
                [(hn, pl.BlockSpec((S, DS), lambda e: (0, e))), (dkv, pl.BlockSpec((S, KVW), lambda e: (0, 0)))],
                [(0, 1, TN)], [(lax.empty((NDEV, DS, KVW), BF16), ANY)],
                [(_sds((NDEV, DS, KVW), BF16), pl.BlockSpec((None, DS, KVW), lambda e: (e, 0, 0)))],
                _store, aliases={2: 0}, deps=list(newest))
            (dhn,) = _gemm(
                "kv_dx", (S // TM, NDEV),
                [(dkv, pl.BlockSpec((TM, KVW), lambda i, e: (i, 0))),
                 (wkv, pl.BlockSpec((None, DS, KVW), lambda i, e: (e, 0, 0)))],
                [(0, 1, NT)], [], [(_sds((S, D), F32), pl.BlockSpec((TM, DS), lambda i, e: (i, e)))], _store)
            d, d_bf, g_kvn, _ = _rms_bwd("kv_norm_bwd", h_kv, kv_norm, dhn, d)
    grad_x = d.reshape(x.shape)

    exchange_middle(pending, d)

    g_rel = _bias_grad("bias_grad", dbiases, buckets)
    small_local = _pack([jnp.concatenate(g_mix, axis=0), jnp.concatenate(g_ffn, axis=0), jnp.stack(g_ws),
                         jnp.stack(g_bs), g_kvn, g_bkv, jnp.concatenate(g_bq, axis=0), jnp.stack(g_sink),
                         jnp.concatenate(g_bo, axis=0), g_rel, g_final, jnp.concatenate(g_nv, axis=0)])
    small_slot = _cast_into("put_small_grads", small_local.reshape((1,) + small_local.shape), 0,
                            lax.empty((NDEV,) + small_local.shape, F32), 0, me1)
    s_send, s_recv, s_bufs, s_tok = _gather_start("gather_small_start", [small_slot], list(newest))

    results = {}
    after = [s_tok]

    def upd(pname, w, m, v, l, li, lands, row0, own=None):
        w3 = w if w.ndim == 3 else w.reshape((1,) + w.shape)
        prev = results.get(pname) or [lax.empty(w3.shape, F32) for _ in range(4)]
        results[pname] = _adamw_shard(f"adamw_{pname}{l}", w3, m.reshape(w3.shape), v.reshape(w3.shape), lands,
                                      row0, li, prev, deps=list(after), own=own, where=where)
        after[:] = [results[pname][0]]

    for l in reversed(range(L)):
        for st in exchanges[l]:
            lands, own = exchange_end(st, after[0])
            if "gu" in lands:
                upd("ffn_w_gate", gate_t, tr3(m_ffn_w_gate), tr3(v_ffn_w_gate), l, l, lands["gu"], 0, own["gu"])
                upd("ffn_w_up", up_t, tr3(m_ffn_w_up), tr3(v_ffn_w_up), l, l, lands["gu"], F, own["gu"])
                upd("ffn_w_down", ffn_w_down, m_ffn_w_down, v_ffn_w_down, l, l, lands["down"], 0, own["down"])
            if "win" in lands:
                upd("a_w_in", a_w_in, m_a_w_in, v_a_w_in, l, l, lands["win"], 0)
                upd("a_w_out", a_w_out, m_a_w_out, v_a_w_out, l, l, lands["wout"], 0)
            if "wkv" in lands:
                upd("w_kv", w_kv, m_w_kv, v_w_kv, l, 0, lands["wkv"], 0)
            if "wqo" in lands:
                upd("b_w_q", b_w_q, m_b_w_q, v_b_w_q, l, l - LA, lands["wqo"], 0)
                upd("b_w_o", b_w_o, m_b_w_o, v_b_w_o, l, l - LA, lands["wqo"], DS)
    for pname in ("ffn_w_gate", "ffn_w_up"):
        results[pname] = [tr3(r) for r in results[pname]]
    results["w_kv"] = [r.reshape(w_kv.shape) for r in results["w_kv"]]

    small_w = [mix_norm, ffn_norm, a_w_s, a_b_s, kv_norm, b_kv, b_b_q, b_sinks, b_b_o, rel_bias, final_norm]
    small_m = [m_mix_norm, m_ffn_norm, m_a_w_s, m_a_b_s, m_kv_norm, m_b_kv, m_b_b_q, m_b_sinks, m_b_b_o, m_rel_bias,
               m_final_norm]
    small_v = [v_mix_norm, v_ffn_norm, v_a_w_s, v_a_b_s, v_kv_norm, v_b_kv, v_b_b_q, v_b_sinks, v_b_b_o, v_rel_bias,
               v_final_norm]
    shapes = [w.shape for w in small_w] + [(LA, D)]
    s_fsend, s_frecv, s_bufs = _gather_forward("gather_small_forward", s_bufs, s_send, s_recv, list(after))
    (small_all,) = _gather_finish("gather_small_finish", s_bufs, s_send, s_recv, s_fsend, s_frecv)
    small_sum = _sum_devices("sum_small_grads", small_all)
    small_g = _unpack(small_sum, shapes)
    g_normv = lax.dynamic_slice_in_dim(small_g[-1], me * DS, DS, axis=1)
    small_g = small_g[:-1] + [g_normv]
    small_w, small_m, small_v = small_w + [a_norm_v], small_m + [m_a_norm_v], small_v + [v_a_norm_v]
    shapes = [w.shape for w in small_w]
    s_delta, s_m, s_v = _adamw_flat("adamw_small", _pack(small_w), _pack(small_g), _pack(small_m), _pack(small_v))
    s_delta, s_m, s_v = _unpack(s_delta, shapes), _unpack(s_m, shapes), _unpack(s_v, shapes)

    names = ["mix_norm", "ffn_norm", "a_w_in", "a_norm_v", "a_w_s", "a_b_s", "a_w_out", "kv_norm", "w_kv", "b_kv",
             "b_w_q", "b_b_q", "b_sinks", "b_w_o", "b_b_o", "rel_bias", "ffn_w_gate", "ffn_w_up", "ffn_w_down",
             "final_norm"]
    small_names = ["mix_norm", "ffn_norm", "a_w_s", "a_b_s", "kv_norm", "b_kv", "b_b_q", "b_sinks", "b_b_o", "rel_bias",
                   "final_norm", "a_norm_v"]
    res = {}
    for idx, nm in enumerate(small_names):
        res[nm] = (small_g[idx].reshape(shapes[idx]), s_delta[idx], s_m[idx], s_v[idx])
    for nm, u in results.items():
        res[nm] = tuple(u)
    out = [loss, grad_x]
    for part in range(4):
        out += [res[nm][part] for nm in names]
    return tuple(out)
```

```python
import math

import numpy as np
import jax
import jax.numpy as jnp
from jax import lax
from jax.experimental import pallas as pl
from jax.experimental.pallas import tpu as pltpu

F32 = jnp.float32
BF16 = jnp.bfloat16
AXES = ("x", "y", "c")
NDEV = 8
NCHIP = 4
CHUNK = 128
GROUPS = 8
HEAD_DIM = 64
KV_GROUP = 8
BLOCK = 128
N_BUCKETS = 32
MAX_DISTANCE = 128
RMS_EPS = 1e-5
NEG_INF = -1e30
ADAM_LR, ADAM_B1, ADAM_B2, ADAM_EPS, ADAM_WD, ADAM_STEP = 0.001, 0.9, 0.999, 1e-08, 0.01, 10
VMEM_LIMIT_BYTES = 56 * 1024 * 1024

NN = (((1,), (0,)), ((), ()))
NT = (((1,), (1,)), ((), ()))
TN = (((0,), (0,)), ((), ()))
ANY = pl.BlockSpec(memory_space=pl.ANY)
HBM = pl.BlockSpec(memory_space=pltpu.HBM)
SEM = pl.BlockSpec(memory_space=pltpu.SEMAPHORE)
MESH = pl.DeviceIdType.MESH
EFFECT = pltpu.SideEffectType.DATAFLOW_SIDE_EFFECTING


def _pcall(body, *, name, out_shape, in_specs, out_specs, grid=(), scratch=(), aliases=None, prefetch=0, deps=()):
    n_in, n_dep = len(in_specs), len(deps)
    if n_dep:
        inner = body

        def body(*refs):
            return inner(*refs[:prefetch + n_in], *refs[prefetch + n_in + n_dep:])

        in_specs = list(in_specs) + [ANY] * n_dep
    params = dict(vmem_limit_bytes=VMEM_LIMIT_BYTES)
    if grid:
        params["dimension_semantics"] = ("arbitrary",) * len(grid)
    kw = dict(name=name, out_shape=out_shape, compiler_params=pltpu.CompilerParams(**params),
              input_output_aliases=aliases or {})
    if prefetch:
        kw["grid_spec"] = pltpu.PrefetchScalarGridSpec(num_scalar_prefetch=prefetch, grid=grid, in_specs=in_specs,
                                                       out_specs=out_specs, scratch_shapes=list(scratch))
    else:
        kw.update(grid=grid, in_specs=in_specs, out_specs=out_specs, scratch_shapes=list(scratch))
    call = pl.pallas_call(body, **kw)
    return lambda *args: call(*args, *deps)


def _sds(shape, dtype):
    return jax.ShapeDtypeStruct(tuple(shape), dtype)


def _position():
    x, y, c = lax.axis_index("x"), lax.axis_index("y"), lax.axis_index("c")
    chips = [(1 - x, y), (x, 1 - y), (1 - x, 1 - y)]
    return x, y, c, chips


def _slot(px, py, pc):
    return 4 * px + 2 * py + pc


def _remote(ref_src, ref_dst, send, recv, to):
    return pltpu.make_async_remote_copy(src_ref=ref_src, dst_ref=ref_dst, send_sem=send, recv_sem=recv,
                                        device_id=to, device_id_type=MESH)


def _hbm(arrays):
    return [pltpu.with_memory_space_constraint(a, pltpu.HBM) for a in arrays]


def _split_call(body, name, out_shape, in_specs, out_specs, aliases):
    return pl.pallas_call(body, name=name, out_shape=out_shape, in_specs=in_specs, out_specs=out_specs,
                          input_output_aliases=aliases, compiler_params=pltpu.CompilerParams(has_side_effects=EFFECT))


def _token_shape():
    return _sds((8, 128), F32)


def _gather_start(name, bufs, deps):
    n, nd = len(bufs), len(deps)

    def body(*refs):
        ins, send, recv, token = refs[:n], refs[n + nd], refs[n + nd + 1], refs[2 * n + nd + 2]
        x, y, c, chips = _position()
        peers = [(x, y, 1 - c)] + [(*chip, c) for chip in chips]
        for t in range(n):
            mine = ins[t].at[_slot(x, y, c)]
            for k, peer in enumerate(peers):
                _remote(mine, mine, send.at[4 * t + k], recv.at[4 * t + k], peer).start()
        token[...] = jnp.zeros_like(token)

    res = _split_call(
        body, name,
        (pltpu.SemaphoreType.DMA((4 * n,)), pltpu.SemaphoreType.DMA((4 * n,)), *[pltpu.HBM(b.shape, b.dtype) for b in bufs],
         _token_shape()),
        [HBM] * n + [ANY] * nd, (SEM, SEM, *[HBM] * n, pl.BlockSpec(memory_space=pltpu.VMEM)),
        {t: 2 + t for t in range(n)})(*_hbm(bufs), *deps)
    return res[0], res[1], list(res[2:2 + n]), res[2 + n]


def _gather_forward(name, bufs, send, recv, deps):
    n, nd = len(bufs), len(deps)

    def body(*refs):
        ins, send_in, recv_in = refs[:n], refs[n], refs[n + 1]
        fsend, frecv = refs[n + 2 + nd], refs[n + 3 + nd]
        x, y, c, chips = _position()
        for j, chip in enumerate(chips):
            for t in range(n):
                blk = ins[t].at[_slot(*chip, c)]
                _remote(blk, blk, send_in.at[4 * t + 1 + j], recv_in.at[4 * t + 1 + j], (*chip, c)).wait_recv()
                _remote(blk, blk, fsend.at[3 * t + j], frecv.at[3 * t + j], (x, y, 1 - c)).start()

    res = _split_call(
        body, name,
        (pltpu.SemaphoreType.DMA((3 * n,)), pltpu.SemaphoreType.DMA((3 * n,)), *[pltpu.HBM(b.shape, b.dtype) for b in bufs]),
        [HBM] * n + [SEM, SEM] + [ANY] * nd, (SEM, SEM, *[HBM] * n),
        {t: 2 + t for t in range(n)})(*_hbm(bufs), send, recv, *deps)
    return res[0], res[1], list(res[2:])


def _gather_finish(name, bufs, send, recv, fsend, frecv):
    n = len(bufs)

    def body(*refs):
        ins, send_in, recv_in, fs_in, fr_in = refs[:n], refs[n], refs[n + 1], refs[n + 2], refs[n + 3]
        x, y, c, chips = _position()
        sibling = (x, y, 1 - c)
        peers = [sibling] + [(*chip, c) for chip in chips]
        for t in range(n):
            blk = ins[t].at[_slot(x, y, 1 - c)]
            _remote(blk, blk, send_in.at[4 * t], recv_in.at[4 * t], sibling).wait_recv()
            for j, chip in enumerate(chips):
                blk = ins[t].at[_slot(*chip, 1 - c)]
                _remote(blk, blk, fs_in.at[3 * t + j], fr_in.at[3 * t + j], sibling).wait_recv()
            mine = ins[t].at[_slot(x, y, c)]
            for k, peer in enumerate(peers):
                _remote(mine, mine, send_in.at[4 * t + k], recv_in.at[4 * t + k], peer).wait_send()
            for j, chip in enumerate(chips):
                blk = ins[t].at[_slot(*chip, c)]
                _remote(blk, blk, fs_in.at[3 * t + j], fr_in.at[3 * t + j], sibling).wait_send()

    res = _split_call(
        body, name, tuple(pltpu.HBM(b.shape, b.dtype) for b in bufs),
        [HBM] * n + [SEM] * 4, tuple([HBM] * n), {t: t for t in range(n)})(*_hbm(bufs), send, recv, fsend, frecv)
    return list(res)


def _halves(ref):
    rows = ref.shape[0] // 2
    return ref.at[pl.ds(0, rows)], ref.at[pl.ds(rows, rows)]


def _relay_start(name, bufs, deps):
    n, nd = len(bufs), len(deps)

    def body(*refs):
        ins, send, recv, token = refs[:n], refs[n + nd], refs[n + nd + 1], refs[2 * n + nd + 2]
        x, y, c, _ = _position()
        peers = [(x, y, 1 - c), (1 - x, y, c), (x, 1 - y, c)]
        for t in range(n):
            mine = ins[t].at[_slot(x, y, c)]
            for k, peer in enumerate(peers):
                _remote(mine, mine, send.at[3 * t + k], recv.at[3 * t + k], peer).start()
        token[...] = jnp.zeros_like(token)

    res = _split_call(
        body, name,
        (pltpu.SemaphoreType.DMA((3 * n,)), pltpu.SemaphoreType.DMA((3 * n,)), *[pltpu.HBM(b.shape, b.dtype) for b in bufs],
         _token_shape()),
        [HBM] * n + [ANY] * nd, (SEM, SEM, *[HBM] * n, pl.BlockSpec(memory_space=pltpu.VMEM)),
        {t: 2 + t for t in range(n)})(*_hbm(bufs), *deps)
    return res[0], res[1], list(res[2:2 + n]), res[2 + n]


def _relay_neighbors(name, bufs, send, recv, deps):
    n, nd = len(bufs), len(deps)

    def body(*refs):
        ins, send_in, recv_in = refs[:n], refs[n], refs[n + 1]
        fsend, frecv, token = refs[n + 2 + nd], refs[n + 3 + nd], refs[2 * n + 4 + nd]
        x, y, c, _ = _position()
        sibling, xn, yn = (x, y, 1 - c), (1 - x, y, c), (x, 1 - y, c)
        for t in range(n):
            blk = ins[t].at[_slot(*xn)]
            _remote(blk, blk, send_in.at[3 * t + 1], recv_in.at[3 * t + 1], xn).wait_recv()
            _remote(blk, blk, fsend.at[4 * t], frecv.at[4 * t], sibling).start()
            half = _halves(blk)[0]
            _remote(half, half, fsend.at[4 * t + 1], frecv.at[4 * t + 1], yn).start()
        for t in range(n):
            blk = ins[t].at[_slot(*yn)]
            _remote(blk, blk, send_in.at[3 * t + 2], recv_in.at[3 * t + 2], yn).wait_recv()
            _remote(blk, blk, fsend.at[4 * t + 2], frecv.at[4 * t + 2], sibling).start()
            half = _halves(blk)[1]
            _remote(half, half, fsend.at[4 * t + 3], frecv.at[4 * t + 3], xn).start()
        token[...] = jnp.zeros_like(token)

    res = _split_call(
        body, name,
        (pltpu.SemaphoreType.DMA((4 * n,)), pltpu.SemaphoreType.DMA((4 * n,)), *[pltpu.HBM(b.shape, b.dtype) for b in bufs],
         _token_shape()),
        [HBM] * n + [SEM, SEM] + [ANY] * nd, (SEM, SEM, *[HBM] * n, pl.BlockSpec(memory_space=pltpu.VMEM)),
        {t: 2 + t for t in range(n)})(*_hbm(bufs), send, recv, *deps)
    return res[0], res[1], list(res[2:2 + n]), res[2 + n]


def _relay_diagonal(name, bufs, fsend, frecv, deps):
    n, nd = len(bufs), len(deps)

    def body(*refs):
        ins, fs_in, fr_in = refs[:n], refs[n], refs[n + 1]
        gsend, grecv = refs[n + 2 + nd], refs[n + 3 + nd]
        x, y, c, _ = _position()
        for t in range(n):
            blk = ins[t].at[_slot(1 - x, 1 - y, c)]
            first, second = _halves(blk)
            _remote(first, first, fs_in.at[4 * t + 1], fr_in.at[4 * t + 1], (x, 1 - y, c)).wait_recv()
            _remote(second, second, fs_in.at[4 * t + 3], fr_in.at[4 * t + 3], (1 - x, y, c)).wait_recv()
            _remote(blk, blk, gsend.at[t], grecv.at[t], (x, y, 1 - c)).start()

    res = _split_call(
        body, name,
        (pltpu.SemaphoreType.DMA((n,)), pltpu.SemaphoreType.DMA((n,)), *[pltpu.HBM(b.shape, b.dtype) for b in bufs]),
        [HBM] * n + [SEM, SEM] + [ANY] * nd, (SEM, SEM, *[HBM] * n),
        {t: 2 + t for t in range(n)})(*_hbm(bufs), fsend, frecv, *deps)
    return res[0], res[1], list(res[2:])


def _relay_finish(name, bufs, send, recv, fsend, frecv, gsend, grecv):
    n = len(bufs)

    def body(*refs):
        ins = refs[:n]
        send_in, recv_in, fs_in, fr_in, gs_in, gr_in = refs[n:n + 6]
        x, y, c, _ = _position()
        sibling, xn, yn = (x, y, 1 - c), (1 - x, y, c), (x, 1 - y, c)
        for t in range(n):
            blk = ins[t].at[_slot(x, y, 1 - c)]
            _remote(blk, blk, send_in.at[3 * t], recv_in.at[3 * t], sibling).wait_recv()
            blk = ins[t].at[_slot(1 - x, y, 1 - c)]
            _remote(blk, blk, fs_in.at[4 * t], fr_in.at[4 * t], sibling).wait_recv()
            blk = ins[t].at[_slot(x, 1 - y, 1 - c)]
            _remote(blk, blk, fs_in.at[4 * t + 2], fr_in.at[4 * t + 2], sibling).wait_recv()
            blk = ins[t].at[_slot(1 - x, 1 - y, 1 - c)]
            _remote(blk, blk, gs_in.at[t], gr_in.at[t], sibling).wait_recv()
            mine = ins[t].at[_slot(x, y, c)]
            for k, peer in enumerate([sibling, xn, yn]):
                _remote(mine, mine, send_in.at[3 * t + k], recv_in.at[3 * t + k], peer).wait_send()
            bx, by = ins[t].at[_slot(*xn)], ins[t].at[_slot(*yn)]
            _remote(bx, bx, fs_in.at[4 * t], fr_in.at[4 * t], sibling).wait_send()
            _remote(_halves(bx)[0], _halves(bx)[0], fs_in.at[4 * t + 1], fr_in.at[4 * t + 1], yn).wait_send()
            _remote(by, by, fs_in.at[4 * t + 2], fr_in.at[4 * t + 2], sibling).wait_send()
            _remote(_halves(by)[1], _halves(by)[1], fs_in.at[4 * t + 3], fr_in.at[4 * t + 3], xn).wait_send()
            bd = ins[t].at[_slot(1 - x, 1 - y, c)]
            _remote(bd, bd, gs_in.at[t], gr_in.at[t], sibling).wait_send()

    res = _split_call(
        body, name, tuple(pltpu.HBM(b.shape, b.dtype) for b in bufs),
        [HBM] * n + [SEM] * 6, tuple([HBM] * n), {t: t for t in range(n)})(
            *_hbm(bufs), send, recv, fsend, frecv, gsend, grecv)
    return list(res)


WHOLE, SHARDS, SHARDS2 = 0, 1, 2


def _sibling_copies(srcs, lands, kinds, c):
    pairs = []
    for s_ref, l_ref, kind in zip(srcs, lands, kinds):
        if kind == WHOLE:
            pairs.append((s_ref, l_ref))
        elif kind == SHARDS:
            pairs += [(s_ref.at[2 * k + (1 - c)], l_ref.at[k]) for k in range(NCHIP)]
        else:
            pairs += [(s_ref.at[w, 2 * k + (1 - c)], l_ref.at[w, k]) for w in range(2) for k in range(NCHIP)]
    return pairs


def _count_copies(kinds):
    return sum({WHOLE: 1, SHARDS: NCHIP, SHARDS2: 2 * NCHIP}[k] for k in kinds)


def _sibling_start(name, srcs, lands, deps, kinds=None):
    n, nd = len(srcs), len(deps)
    kinds = kinds or [SHARDS] * n
    ncp = _count_copies(kinds)

    def body(*refs):
        s_in, l_in = refs[:n], refs[n:2 * n]
        send, recv, token = refs[2 * n + nd], refs[2 * n + nd + 1], refs[4 * n + nd + 2]
        x, y, c, _ = _position()
        for i, (src, dst) in enumerate(_sibling_copies(s_in, l_in, kinds, c)):
            _remote(src, dst, send.at[i], recv.at[i], (x, y, 1 - c)).start()
        token[...] = jnp.zeros_like(token)

    both = list(srcs) + list(lands)
    res = _split_call(
        body, name,
        (pltpu.SemaphoreType.DMA((ncp,)), pltpu.SemaphoreType.DMA((ncp,)),
         *[pltpu.HBM(b.shape, b.dtype) for b in both], _token_shape()),
        [HBM] * (2 * n) + [ANY] * nd, (SEM, SEM, *[HBM] * (2 * n), pl.BlockSpec(memory_space=pltpu.VMEM)),
        {t: 2 + t for t in range(2 * n)})(*_hbm(both), *deps)
    return res[0], res[1], list(res[2:2 + n]), list(res[2 + n:2 + 2 * n]), res[2 + 2 * n]


def _sibling_finish(name, srcs, lands, send, recv, deps, kinds=None):
    n, nd = len(srcs), len(deps)
    kinds = kinds or [SHARDS] * n

    def body(*refs):
        s_in, l_in, send_in, recv_in = refs[:n], refs[n:2 * n], refs[2 * n], refs[2 * n + 1]
        x, y, c, _ = _position()
        for i, (src, dst) in enumerate(_sibling_copies(s_in, l_in, kinds, c)):
            cp = _remote(src, dst, send_in.at[i], recv_in.at[i], (x, y, 1 - c))
            cp.wait_send()
            cp.wait_recv()

    both = list(srcs) + list(lands)
    res = _split_call(
        body, name, tuple(pltpu.HBM(b.shape, b.dtype) for b in both),
        [HBM] * (2 * n) + [SEM, SEM] + [ANY] * nd, tuple([HBM] * (2 * n)),
        {t: t for t in range(2 * n)})(*_hbm(both), send, recv, *deps)
    return list(res[:n]), list(res[n:])


def _chips_start(name, parts, lands, deps):
    n, nd = len(parts), len(deps)

    def body(*refs):
        p_in, l_in = refs[:n], refs[n:2 * n]
        send, recv, token = refs[2 * n + nd], refs[2 * n + nd + 1], refs[4 * n + nd + 2]
        x, y, c, chips = _position()
        for t in range(n):
            for j, chip in enumerate(chips):
                _remote(p_in[t].at[2 * chip[0] + chip[1]], l_in[t].at[2 * x + y], send.at[3 * t + j], recv.at[3 * t + j],
                        (*chip, c)).start()
        token[...] = jnp.zeros_like(token)

    both = list(parts) + list(lands)
    res = _split_call(
        body, name,
        (pltpu.SemaphoreType.DMA((3 * n,)), pltpu.SemaphoreType.DMA((3 * n,)), *[pltpu.HBM(b.shape, b.dtype) for b in both],
         _token_shape()),
        [HBM] * (2 * n) + [ANY] * nd, (SEM, SEM, *[HBM] * (2 * n), pl.BlockSpec(memory_space=pltpu.VMEM)),
        {t: 2 + t for t in range(2 * n)})(*_hbm(both), *deps)
    return res[0], res[1], list(res[2:2 + n]), list(res[2 + n:2 + 2 * n]), res[2 + 2 * n]


def _chips_finish(name, parts, lands, send, recv, deps):
    n, nd = len(parts), len(deps)

    def body(*refs):
        p_in, l_in, send_in, recv_in = refs[:n], refs[n:2 * n], refs[2 * n], refs[2 * n + 1]
        x, y, c, chips = _position()
        for t in range(n):
            for j, chip in enumerate(chips):
                k = 2 * chip[0] + chip[1]
                _remote(p_in[t].at[k], l_in[t].at[k], send_in.at[3 * t + j], recv_in.at[3 * t + j], (*chip, c)).wait_recv()
                _remote(p_in[t].at[k], l_in[t].at[2 * x + y], send_in.at[3 * t + j], recv_in.at[3 * t + j],
                        (*chip, c)).wait_send()

    both = list(parts) + list(lands)
    res = _split_call(
        body, name, tuple(pltpu.HBM(b.shape, b.dtype) for b in both),
        [HBM] * (2 * n) + [SEM, SEM] + [ANY] * nd, tuple([HBM] * (2 * n)),
        {t: t for t in range(2 * n)})(*_hbm(both), send, recv, *deps)
    return list(res[:n]), list(res[n:])


def _pair_sum(name, grad, recv, where):
    _, r, w = grad.shape
    tr = _row_tile(r, w, budget=4 * 1024 * 1024)
    g4 = grad.reshape(NCHIP, 2, r, w)

    def body(where_ref, g_ref, r_ref, o_ref, own_ref):
        val = (g_ref[...].astype(F32) + r_ref[...].astype(F32)).astype(o_ref.dtype)
        o_ref[...] = val

        @pl.when(pl.program_id(1) == where_ref[1])
        def _():
            own_ref[...] = val

    out = _sds((NCHIP, r, w), grad.dtype)
    return _pcall(
        body, name=name, out_shape=[out, out], grid=(r // tr, NCHIP), prefetch=1,
        in_specs=[pl.BlockSpec((None, None, tr, w), lambda i, k, wr: (k, wr[0], i, 0)),
                  pl.BlockSpec((None, tr, w), lambda i, k, wr: (k, i, 0))],
        out_specs=[pl.BlockSpec((None, tr, w), lambda i, k, wr: (k, i, 0)),
                   pl.BlockSpec((None, tr, w), lambda i, k, wr: (wr[1], i, 0))],
    )(where, g4, recv)


def _row_tile(rows, width, budget=2 * 1024 * 1024):
    best = None
    for t in range(16, rows + 1, 16):
        if rows % t == 0 and t * width * 4 <= budget:
            best = t
    if best is None and rows * width * 4 <= budget:
        best = rows
    assert best is not None, (rows, width)
    return best


def _gemm(name, grid, operands, prods, extras, outs, epilogue, *, nk=1, acc_shape=None, aliases=None, separate=False,
          deps=(), prefetch=()):
    n_op, n_ex, n_out = len(operands), len(extras), len(outs)

    def body(*refs):
        refs = refs[len(prefetch):]
        ops, ex, out_refs = refs[:n_op], refs[n_op:n_op + n_ex], refs[n_op + n_ex:n_op + n_ex + n_out]
        parts = []
        for pr in prods:
            a, b = ops[pr[0]], ops[pr[1]]
            av = pr[3](a) if len(pr) > 3 and pr[3] else a[...]
            bv = pr[4](b) if len(pr) > 4 and pr[4] else b[...]
            parts.append(lax.dot_general(av, bv, pr[2], preferred_element_type=F32))
        if separate:
            epilogue(parts, ex, out_refs)
            return
        part = parts[0]
        for p in parts[1:]:
            part = part + p
        if nk == 1:
            epilogue(part, ex, out_refs)
        else:
            acc = refs[-1]
            k = pl.program_id(len(grid) - 1)

            @pl.when(k == 0)
            def _():
                acc[...] = part

            @pl.when(k > 0)
            def _():
                acc[...] += part

            @pl.when(k == nk - 1)
            def _():
                epilogue(acc[...], ex, out_refs)

    res = _pcall(
        body, name=name, out_shape=[o[0] for o in outs], grid=grid,
        in_specs=[o[1] for o in operands] + [e[1] for e in extras], out_specs=[o[1] for o in outs],
        scratch=[pltpu.VMEM(acc_shape, F32)] if nk > 1 else [], aliases=aliases, deps=deps, prefetch=len(prefetch),
    )(*prefetch, *[o[0] for o in operands], *[e[0] for e in extras])
    return list(res)


def _store(acc, ex, outs):
    outs[0][...] = acc.astype(outs[0].dtype)


def _store_add_extra(acc, ex, outs):
    v = acc
    for e in ex:
        v = v + e[...]
    outs[0][...] = v.astype(outs[0].dtype)


def _stacked(ref):
    b = ref[...]
    return b.reshape(b.shape[0] * b.shape[1], b.shape[2])


def _pick(c):
    return lambda ref: ref[c]


def _cols(c, width):
    return lambda ref: ref[:, c * width:(c + 1) * width]


def _grad_cols(name, act, dy, buf, ts):
    s, k = act.shape
    nd, _, n = buf.shape

    def body(a_ref, dy_ref, b_ref, o_ref, at_ref):
        @pl.when(pl.program_id(1) == 0)
        def _():
            at_ref[...] = a_ref[...].T

        o_ref[...] = jnp.dot(at_ref[...], dy_ref[...], preferred_element_type=F32).astype(o_ref.dtype)

    return _pcall(
        body, name=name, out_shape=_sds(buf.shape, buf.dtype), grid=(k // ts, nd),
        in_specs=[pl.BlockSpec((s, ts), lambda i, e: (0, i)), pl.BlockSpec((s, n), lambda i, e: (0, e)), ANY],
        out_specs=pl.BlockSpec((None, ts, n), lambda i, e: (e, i, 0)), aliases={2: 0},
        scratch=[pltpu.VMEM((ts, s), act.dtype)],
    )(act, dy, buf)


def _gelu_parts(z):
    c = math.sqrt(2.0 / math.pi)
    t = jnp.tanh(c * (z + 0.044715 * (z * z * z)))
    val = 0.5 * z * (1.0 + t)
    grad = 0.5 * (1.0 + t) + 0.5 * z * (1.0 - t * t) * (c * (1.0 + 3.0 * 0.044715 * z * z))
    return val, grad


def _rms_fwd(name, h, g, deps=()):
    s, d = h.shape
    tr = _row_tile(s, d)

    def body(h_ref, g_ref, o_ref):
        hv = h_ref[...]
        r = lax.rsqrt(jnp.mean(hv * hv, axis=-1, keepdims=True) + RMS_EPS)
        o_ref[...] = (hv * r * g_ref[...]).astype(o_ref.dtype)

    return _pcall(
        body, name=name, out_shape=_sds((s, d), BF16), grid=(s // tr,),
        in_specs=[pl.BlockSpec((tr, d), lambda i: (i, 0)), pl.BlockSpec((1, d), lambda i: (0, 0))],
        out_specs=pl.BlockSpec((tr, d), lambda i: (i, 0)), deps=deps,
    )(h, g.reshape(1, d))


def _accumulate(ref, val, first):
    @pl.when(first)
    def _():
        ref[...] = val

    @pl.when(jnp.logical_not(first))
    def _():
        ref[...] += val


def _rms_bwd(name, h, g, dy, res, deps=()):
    s, d = h.shape
    tr = _row_tile(s, d, budget=2 * 1024 * 1024)

    def body(h_ref, g_ref, dy_ref, res_ref, dh_ref, dhb_ref, dg_ref, cs_ref):
        hv = h_ref[...]
        r = lax.rsqrt(jnp.mean(hv * hv, axis=-1, keepdims=True) + RMS_EPS)
        xhat = hv * r
        dyv = dy_ref[...]
        dxh = dyv * g_ref[...]
        dh = res_ref[...] + r * (dxh - xhat * jnp.mean(dxh * xhat, axis=-1, keepdims=True))
        dh_ref[...] = dh
        dhb_ref[...] = dh.astype(BF16)
        first = pl.program_id(0) == 0
        _accumulate(dg_ref, jnp.sum(dyv * xhat, axis=0, keepdims=True), first)
        _accumulate(cs_ref, jnp.sum(dh, axis=0, keepdims=True), first)

    row = pl.BlockSpec((tr, d), lambda i: (i, 0))
    vec = pl.BlockSpec((1, d), lambda i: (0, 0))
    return _pcall(
        body, name=name, out_shape=[_sds((s, d), F32), _sds((s, d), BF16), _sds((1, d), F32), _sds((1, d), F32)],
        grid=(s // tr,), in_specs=[row, vec, row, row], out_specs=[row, row, vec, vec], deps=deps,
    )(h, g.reshape(1, d), dy, res)


def _loss_bwd(name, h, g, target):
    s, d = h.shape
    tr = _row_tile(s, d, budget=1024 * 1024)

    def body(h_ref, g_ref, t_ref, loss_ref, dh_ref, dhb_ref, dg_ref):
        hv = h_ref[...]
        r = lax.rsqrt(jnp.mean(hv * hv, axis=-1, keepdims=True) + RMS_EPS)
        xhat = hv * r
        diff = xhat * g_ref[...] - t_ref[...]
        part = jnp.sum(jnp.sum(diff * diff, axis=1, keepdims=True), axis=0, keepdims=True) * (0.5 / d)
        dyv = diff * (1.0 / d)
        dxh = dyv * g_ref[...]
        dh = r * (dxh - xhat * jnp.mean(dxh * xhat, axis=-1, keepdims=True))
        dh_ref[...] = dh
        dhb_ref[...] = dh.astype(BF16)
        first = pl.program_id(0) == 0
        _accumulate(loss_ref, part, first)
        _accumulate(dg_ref, jnp.sum(dyv * xhat, axis=0, keepdims=True), first)

    row = pl.BlockSpec((tr, d), lambda i: (i, 0))
    vec = pl.BlockSpec((1, d), lambda i: (0, 0))
    one = pl.BlockSpec((1, 1), lambda i: (0, 0))
    return _pcall(
        body, name=name, out_shape=[_sds((1, 1), F32), _sds((s, d), F32), _sds((s, d), BF16), _sds((1, d), F32)],
        grid=(s // tr,), in_specs=[row, vec, row], out_specs=[one, row, row, vec],
    )(h, g.reshape(1, d), target)


def _tril_mask():
    return lax.broadcasted_iota(jnp.int32, (CHUNK, CHUNK), 0) >= lax.broadcasted_iota(jnp.int32, (CHUNK, CHUNK), 1)


def _gmlp_fwd(name, zp, gv, ws, bst):
    s, d2 = zp.shape
    d = d2 // 2
    gw = d // GROUPS

    def body(zp_ref, gv_ref, ws_ref, bst_ref, o_ref):
        u, _ = _gelu_parts(zp_ref[:, :d])
        v, _ = _gelu_parts(zp_ref[:, d:])
        rv = lax.rsqrt(jnp.mean(v * v, axis=-1, keepdims=True) + RMS_EPS)
        vn = (v * rv * gv_ref[...]).astype(BF16)
        tril = _tril_mask()
        for g in range(GROUPS):
            sl = slice(g * gw, (g + 1) * gw)
            wc = jnp.where(tril, ws_ref[g], 0.0).astype(BF16)
            sg = jnp.dot(wc, vn[:, sl], preferred_element_type=F32) + bst_ref[:, g:g + 1]
            o_ref[:, sl] = (u[:, sl] * sg).astype(o_ref.dtype)

    return _pcall(
        body, name=name, out_shape=_sds((s, d), BF16), grid=(s // CHUNK,),
        in_specs=[pl.BlockSpec((CHUNK, d2), lambda i: (i, 0)), pl.BlockSpec((1, d), lambda i: (0, 0)),
                  pl.BlockSpec((GROUPS, CHUNK, CHUNK), lambda i: (0, 0, 0)),
                  pl.BlockSpec((CHUNK, GROUPS), lambda i: (0, 0))],
        out_specs=pl.BlockSpec((CHUNK, d), lambda i: (i, 0)),
    )(zp, gv, ws, bst)


def _gmlp_bwd(name, zp, dgated, gv, ws, bst):
    s, d2 = zp.shape
    d = d2 // 2
    gw = d // GROUPS

    def body(zp_ref, dg_ref, gv_ref, ws_ref, bst_ref, dzp_ref, dws_ref, dbs_ref, dgv_ref, dvn_ref):
        u, gu = _gelu_parts(zp_ref[:, :d])
        v, gvv = _gelu_parts(zp_ref[:, d:])
        rv = lax.rsqrt(jnp.mean(v * v, axis=-1, keepdims=True) + RMS_EPS)
        vhat = v * rv
        vn = (vhat * gv_ref[...]).astype(BF16)
        tril = _tril_mask()
        first = pl.program_id(0) == 0
        ones = jnp.ones((8, gw), F32)

        @pl.when(first)
        def _():
            dws_ref[...] = jnp.zeros_like(dws_ref)
            dbs_ref[...] = jnp.zeros_like(dbs_ref)

        for g in range(GROUPS):
            sl = slice(g * gw, (g + 1) * gw)
            wc = jnp.where(tril, ws_ref[g], 0.0).astype(BF16)
            sg = jnp.dot(wc, vn[:, sl], preferred_element_type=F32) + bst_ref[:, g:g + 1]
            dgs = dg_ref[:, sl]
            ds = dgs * u[:, sl]
            dsb = ds.astype(BF16)
            dzp_ref[:, sl] = (dgs * sg * gu[:, sl]).astype(dzp_ref.dtype)
            dvn_ref[:, sl] = lax.dot_general(wc, dsb, TN, preferred_element_type=F32)
            dw = lax.dot_general(dsb, vn[:, sl], NT, preferred_element_type=F32)
            dws_ref[g] += jnp.where(tril, dw, 0.0)
            dbs_ref[g] += lax.dot_general(ones, ds, NT, preferred_element_type=F32, precision=lax.Precision.HIGHEST)
        dvn = dvn_ref[...]
        dvh = dvn * gv_ref[...]
        dv = rv * (dvh - vhat * jnp.mean(dvh * vhat, axis=-1, keepdims=True))
        dzp_ref[:, d:] = (dv * gvv).astype(dzp_ref.dtype)
        _accumulate(dgv_ref, jnp.sum(dvn * vhat, axis=0, keepdims=True), first)

    return _pcall(
        body, name=name,
        out_shape=[_sds((s, d2), BF16), _sds((GROUPS, CHUNK, CHUNK), F32), _sds((GROUPS, 8, CHUNK), F32),
                   _sds((1, d), F32)],
        grid=(s // CHUNK,),
        in_specs=[pl.BlockSpec((CHUNK, d2), lambda i: (i, 0)), pl.BlockSpec((CHUNK, d), lambda i: (i, 0)),
                  pl.BlockSpec((1, d), lambda i: (0, 0)), pl.BlockSpec((GROUPS, CHUNK, CHUNK), lambda i: (0, 0, 0)),
                  pl.BlockSpec((CHUNK, GROUPS), lambda i: (0, 0))],
        out_specs=[pl.BlockSpec((CHUNK, d2), lambda i: (i, 0)),
                   pl.BlockSpec((GROUPS, CHUNK, CHUNK), lambda i: (0, 0, 0)),
                   pl.BlockSpec((GROUPS, 8, CHUNK), lambda i: (0, 0, 0)), pl.BlockSpec((1, d), lambda i: (0, 0))],
        scratch=[pltpu.VMEM((CHUNK, d), F32)],
    )(zp, dgated, gv, ws, bst)


def _bucket_table():
    dist = np.arange(BLOCK)[:, None] + BLOCK - np.arange(2 * BLOCK)[None, :]
    in_window = (dist >= 0) & (dist < BLOCK)
    dd = np.clip(dist, 0, None)
    max_exact = N_BUCKETS // 2
    dl = np.maximum(dd, 1).astype(np.float32)
    large = max_exact + (np.log(dl / np.float32(max_exact)) / np.float32(math.log(MAX_DISTANCE / max_exact))
                         * np.float32(N_BUCKETS - max_exact)).astype(np.int32)
    large = np.minimum(large, N_BUCKETS - 1)
    bucket = np.where(dd < max_exact, dd, large)
    return np.where(in_window, bucket, -1).astype(np.int32).reshape(1, -1)


def _bias_table(name, rel_bias_t, buckets):
    nh = rel_bias_t.shape[0]
    p = buckets.shape[1]
    tp = 4096

    def body(rb_ref, bk_ref, o_ref):
        bk = bk_ref[...]
        onehot = (lax.broadcasted_iota(jnp.int32, (N_BUCKETS, tp), 0) == bk).astype(F32)
        val = jnp.dot(rb_ref[...], onehot, preferred_element_type=F32, precision=lax.Precision.HIGHEST)
        o_ref[...] = jnp.where(bk >= 0, val, NEG_INF)

    return _pcall(
        body, name=name, out_shape=_sds((nh, p), F32), grid=(p // tp,),
        in_specs=[pl.BlockSpec((nh, N_BUCKETS), lambda i: (0, 0)), pl.BlockSpec((1, tp), lambda i: (0, i))],
        out_specs=pl.BlockSpec((nh, tp), lambda i: (0, i)),
    )(rel_bias_t, buckets)


def _bias_grad(name, dbiases, buckets):
    nh, p = dbiases[0].shape
    n = len(dbiases)
    tp = 4096

    def body(*refs):
        bk_ref, o_ref = refs[n], refs[n + 1]
        onehot = (lax.broadcasted_iota(jnp.int32, (N_BUCKETS, tp), 0) == bk_ref[...]).astype(F32)
        db = refs[0][...]
        for r in refs[1:n]:
            db = db + r[...]
        part = lax.dot_general(onehot, db, NT, preferred_element_type=F32, precision=lax.Precision.HIGHEST)
        _accumulate(o_ref, part, pl.program_id(0) == 0)

    return _pcall(
        body, name=name, out_shape=_sds((N_BUCKETS, nh), F32), grid=(p // tp,),
        in_specs=[pl.BlockSpec((nh, tp), lambda i: (0, i))] * n + [pl.BlockSpec((1, tp), lambda i: (0, i))],
        out_specs=pl.BlockSpec((N_BUCKETS, nh), lambda i: (0, 0)),
    )(*dbiases, buckets)


def _stack_heads(ref, g):
    base = g * KV_GROUP * HEAD_DIM
    return jnp.concatenate([ref[:, base + hh * HEAD_DIM:base + (hh + 1) * HEAD_DIM] for hh in range(KV_GROUP)], axis=0)


def _attn_probs(q, kb, bias, s_ref, first_head):
    penalty = jnp.where(pl.program_id(1) > 0, 0.0, NEG_INF).astype(F32)
    col = lax.broadcasted_iota(jnp.int32, (1, 2 * BLOCK), 1)
    bias = bias.reshape(KV_GROUP * BLOCK, 2 * BLOCK) + jnp.where(col < BLOCK, penalty, 0.0)
    sink = jnp.concatenate([jnp.full((BLOCK, 1), s_ref[first_head + hh], F32) for hh in range(KV_GROUP)], axis=0)
    s = lax.dot_general(q, kb, NT, preferred_element_type=F32) * 0.125 + bias
    m = jnp.maximum(jnp.max(s, axis=-1, keepdims=True), sink)
    p = jnp.exp(s - m)
    es = jnp.exp(sink - m)
    inv = 1.0 / (jnp.sum(p, axis=-1, keepdims=True) + es)
    return p * inv, es * inv


def _attn_specs(ng):
    gq = ng * KV_GROUP * HEAD_DIM
    q_spec = pl.BlockSpec((BLOCK, gq), lambda kh, i: (i, kh))
    prev = pl.BlockSpec((ng, BLOCK, HEAD_DIM), lambda kh, i: (kh, jnp.maximum(i - 1, 0), 0))
    cur = pl.BlockSpec((ng, BLOCK, HEAD_DIM), lambda kh, i: (kh, i, 0))
    bias = pl.BlockSpec((ng * KV_GROUP, BLOCK, 2 * BLOCK), lambda kh, i: (kh, 0, 0))
    smem = pl.BlockSpec(memory_space=pltpu.SMEM)
    probs = pl.BlockSpec((ng, None, KV_GROUP * BLOCK, 2 * BLOCK), lambda kh, i: (kh, i, 0, 0))
    sink_probs = pl.BlockSpec((ng, None, KV_GROUP * BLOCK, 1), lambda kh, i: (kh, i, 0, 0))
    return q_spec, prev, cur, bias, smem, probs, sink_probs


def _kv_heads_per_step(nkv):
    return 2 if nkv % 2 == 0 else 1


def _attn_fwd(name, q, k, v, bias, sinks, deps=()):
    s, dq = q.shape
    nkv = k.shape[0]
    ng = 1
    q_spec, prev, cur, bias_spec, smem, p_spec, ps_spec = _attn_specs(ng)

    def body(q_ref, kp_ref, kc_ref, vp_ref, vc_ref, b_ref, s_ref, o_ref, p_ref, ps_ref):
        for g in range(ng):
            kb = jnp.concatenate([kp_ref[g], kc_ref[g]], axis=0)
            vb = jnp.concatenate([vp_ref[g], vc_ref[g]], axis=0)
            p, ps = _attn_probs(_stack_heads(q_ref, g), kb, b_ref[g * KV_GROUP:(g + 1) * KV_GROUP], s_ref,
                                (pl.program_id(0) * ng + g) * KV_GROUP)
            pb = p.astype(BF16)
            p_ref[g] = pb
            ps_ref[g] = ps
            o = jnp.dot(pb, vb, preferred_element_type=F32)
            for hh in range(KV_GROUP):
                col = (g * KV_GROUP + hh) * HEAD_DIM
                o_ref[:, col:col + HEAD_DIM] = o[hh * BLOCK:(hh + 1) * BLOCK].astype(o_ref.dtype)

    return _pcall(
        body, name=name,
        out_shape=[_sds((s, dq), BF16), _sds((nkv, s // BLOCK, KV_GROUP * BLOCK, 2 * BLOCK), BF16),
                   _sds((nkv, s // BLOCK, KV_GROUP * BLOCK, 1), F32)],
        grid=(nkv // ng, s // BLOCK),
        in_specs=[q_spec, prev, cur, prev, cur, bias_spec, smem], out_specs=[q_spec, p_spec, ps_spec], deps=deps,
    )(q, k, k, v, v, bias, sinks)


def _attn_bwd(name, q, k, v, do, probs, sink_probs):
    s, dq = q.shape
    nkv = k.shape[0]
    ng = _kv_heads_per_step(nkv)
    gq = ng * KV_GROUP * HEAD_DIM
    q_spec, prev, cur, bias_spec, _, p_spec, ps_spec = _attn_specs(ng)

    def body(q_ref, do_ref, kp_ref, kc_ref, vp_ref, vc_ref, p_ref, ps_ref,
             dq_ref, dbq_ref, dkc_ref, dkp_ref, dvc_ref, dvp_ref, dbias_ref, dsink_ref):
        @pl.when(pl.program_id(1) == 0)
        def _():
            dbias_ref[...] = jnp.zeros_like(dbias_ref)
            dsink_ref[...] = jnp.zeros_like(dsink_ref)
            dbq_ref[...] = jnp.zeros_like(dbq_ref)

        for g in range(ng):
            kb = jnp.concatenate([kp_ref[g], kc_ref[g]], axis=0)
            vb = jnp.concatenate([vp_ref[g], vc_ref[g]], axis=0)
            q, do = _stack_heads(q_ref, g), _stack_heads(do_ref, g)
            pb = p_ref[g]
            p = pb.astype(F32)
            dp = lax.dot_general(do, vb, NT, preferred_element_type=F32)
            delta = jnp.sum(p * dp, axis=-1, keepdims=True)
            ds = p * (dp - delta)
            dsb = ds.astype(BF16)
            dq = jnp.dot(dsb, kb, preferred_element_type=F32) * 0.125
            dsk = -(ps_ref[g] * delta)
            for hh in range(KV_GROUP):
                col, rows = (g * KV_GROUP + hh) * HEAD_DIM, slice(hh * BLOCK, (hh + 1) * BLOCK)
                dq_ref[:, col:col + HEAD_DIM] = dq[rows].astype(dq_ref.dtype)
                dbq_ref[:, col:col + HEAD_DIM] += jnp.sum(dq[rows], axis=0, keepdims=True)
                dsink_ref[g, :, hh:hh + 1] += jnp.sum(dsk[rows], axis=0, keepdims=True)
            dkb = lax.dot_general(dsb, q, TN, preferred_element_type=F32) * 0.125
            dvb = lax.dot_general(pb, do, TN, preferred_element_type=F32)
            dkp_ref[g], dkc_ref[g] = dkb[:BLOCK], dkb[BLOCK:]
            dvp_ref[g], dvc_ref[g] = dvb[:BLOCK], dvb[BLOCK:]
            dbias_ref[g * KV_GROUP:(g + 1) * KV_GROUP] += ds.reshape(KV_GROUP, BLOCK, 2 * BLOCK)

    kv_out = _sds((nkv, s, HEAD_DIM), F32)
    return _pcall(
        body, name=name,
        out_shape=[_sds((s, dq), BF16), _sds((1, dq), F32), kv_out, kv_out, kv_out, kv_out,
                   _sds((nkv * KV_GROUP, BLOCK, 2 * BLOCK), F32), _sds((nkv, 1, KV_GROUP), F32)],
        grid=(nkv // ng, s // BLOCK),
        in_specs=[q_spec, q_spec, prev, cur, prev, cur, p_spec, ps_spec],
        out_specs=[q_spec, pl.BlockSpec((1, gq), lambda kh, i: (0, kh)), cur, cur, cur, cur, bias_spec,
                   pl.BlockSpec((ng, 1, KV_GROUP), lambda kh, i: (kh, 0, 0))],
    )(q, do, k, k, v, v, probs, sink_probs)


def _kv_grad(name, parts):
    nkv, s, _ = parts[0][0].shape
    nb = s // BLOCK
    w = 2 * nkv * HEAD_DIM
    n = len(parts)

    def body(*refs):
        o_ref, cs_ref = refs[4 * n], refs[4 * n + 1]
        i = pl.program_id(0)
        keep = jnp.where(i < nb - 1, 1.0, 0.0).astype(F32)

        @pl.when(i == 0)
        def _():
            cs_ref[...] = jnp.zeros_like(cs_ref)

        for which in range(2):
            for hh in range(nkv):
                val = None
                for l in range(n):
                    cur_ref, nxt_ref = refs[4 * l + 2 * which], refs[4 * l + 2 * which + 1]
                    t = cur_ref[hh] + keep * nxt_ref[hh]
                    val = t if val is None else val + t
                sl = slice((which * nkv + hh) * HEAD_DIM, (which * nkv + hh + 1) * HEAD_DIM)
                o_ref[:, sl] = val.astype(o_ref.dtype)
                cs_ref[:, sl] += jnp.sum(val, axis=0, keepdims=True)

    cur = pl.BlockSpec((nkv, BLOCK, HEAD_DIM), lambda i: (0, i, 0))
    nxt = pl.BlockSpec((nkv, BLOCK, HEAD_DIM), lambda i: (0, jnp.minimum(i + 1, nb - 1), 0))
    flat = [a for p in parts for a in p]
    return _pcall(
        body, name=name, out_shape=[_sds((s, w), BF16), _sds((1, w), F32)], grid=(nb,),
        in_specs=[cur, nxt] * (2 * n),
        out_specs=[pl.BlockSpec((BLOCK, w), lambda i: (i, 0)), pl.BlockSpec((1, w), lambda i: (0, 0))],
    )(*flat)


def _adamw_math(w, g, m, v):
    m = ADAM_B1 * m + (1.0 - ADAM_B1) * g
    v = ADAM_B2 * v + (1.0 - ADAM_B2) * (g * g)
    m_hat = m / (1.0 - ADAM_B1 ** ADAM_STEP)
    v_hat = v / (1.0 - ADAM_B2 ** ADAM_STEP)
    delta = -ADAM_LR * (m_hat / (jnp.sqrt(v_hat) + ADAM_EPS) + ADAM_WD * w)
    return delta, m, v


def _adamw_shard(name, w, m, v, parts, row0, layer, prev, deps=(), own=None, where=None):
    _, r, wd = w.shape
    tr = _row_tile(r, wd, budget=3 * 512 * 1024)
    assert row0 % tr == 0

    def step(w_ref, m_ref, v_ref, g, g_ref, d_ref, nm_ref, nv_ref):
        delta, nm, nv = _adamw_math(w_ref[...], g, m_ref[...], v_ref[...])
        g_ref[...], d_ref[...], nm_ref[...], nv_ref[...] = g, delta, nm, nv

    out = _sds(w.shape, F32)
    if own is None:
        def body(w_ref, m_ref, v_ref, p_ref, a0, a1, a2, a3, g_ref, d_ref, nm_ref, nv_ref):
            g = p_ref[0].astype(F32)
            for k in range(1, NCHIP):
                g = g + p_ref[k].astype(F32)
            step(w_ref, m_ref, v_ref, g, g_ref, d_ref, nm_ref, nv_ref)

        par = pl.BlockSpec((None, tr, wd), lambda i: (layer, i, 0))
        return _pcall(
            body, name=name, out_shape=[out, out, out, out], grid=(r // tr,),
            in_specs=[par, par, par, pl.BlockSpec((NCHIP, tr, wd), lambda i: (0, row0 // tr + i, 0)), ANY, ANY, ANY, ANY],
            out_specs=[par, par, par, par], aliases={4: 0, 5: 1, 6: 2, 7: 3}, deps=deps,
        )(w, m, v, parts, *prev)

    def body(where_ref, w_ref, m_ref, v_ref, p_ref, o_ref, a0, a1, a2, a3, g_ref, d_ref, nm_ref, nv_ref):
        mine = lax.broadcasted_iota(jnp.int32, (tr, wd), 0) * 0 + where_ref[1]
        g = None
        for k in range(NCHIP):
            t = jnp.where(mine == k, o_ref[...], p_ref[k]).astype(F32)
            g = t if g is None else g + t
        step(w_ref, m_ref, v_ref, g, g_ref, d_ref, nm_ref, nv_ref)

    par = pl.BlockSpec((None, tr, wd), lambda i, wr: (layer, i, 0))
    return _pcall(
        body, name=name, out_shape=[out, out, out, out], grid=(r // tr,), prefetch=1,
        in_specs=[par, par, par, pl.BlockSpec((NCHIP, tr, wd), lambda i, wr: (0, row0 // tr + i, 0)),
                  pl.BlockSpec((None, tr, wd), lambda i, wr: (wr[1], row0 // tr + i, 0)), ANY, ANY, ANY, ANY],
        out_specs=[par, par, par, par], aliases={6: 0, 7: 1, 8: 2, 9: 3}, deps=deps,
    )(where, w, m, v, parts, own, *prev)


def _sum_devices(name, gathered):
    _, r, wd = gathered.shape

    def body(g_ref, o_ref):
        acc = g_ref[0]
        for k in range(1, NDEV):
            acc = acc + g_ref[k]
        o_ref[...] = acc

    return _pcall(body, name=name, out_shape=_sds((r, wd), F32), grid=(1,),
                  in_specs=[pl.BlockSpec((NDEV, r, wd), lambda i: (0, 0, 0))],
                  out_specs=pl.BlockSpec((r, wd), lambda i: (0, 0)))(gathered)


def _adamw_flat(name, w, g, m, v):
    shape = w.shape

    def body(w_ref, g_ref, m_ref, v_ref, d_ref, nm_ref, nv_ref):
        d_ref[...], nm_ref[...], nv_ref[...] = _adamw_math(w_ref[...], g_ref[...], m_ref[...], v_ref[...])

    spec = pl.BlockSpec(shape, lambda i: (0, 0))
    out = _sds(shape, F32)
    return _pcall(body, name=name, out_shape=[out, out, out], grid=(1,), in_specs=[spec] * 4,
                  out_specs=[spec] * 3)(w, g, m, v)


def _cast_into(name, src, layer, buf, row0, me):
    _, r, wd = src.shape
    tr = _row_tile(r, wd)
    assert row0 % tr == 0

    def body(me_ref, s_ref, b_ref, o_ref):
        o_ref[...] = s_ref[...].astype(o_ref.dtype)

    return _pcall(
        body, name=name, out_shape=_sds(buf.shape, buf.dtype), grid=(r // tr,), prefetch=1,
        in_specs=[pl.BlockSpec((None, tr, wd), lambda i, mr: (layer, i, 0)), ANY],
        out_specs=pl.BlockSpec((None, tr, wd), lambda i, mr: (mr[0], row0 // tr + i, 0)), aliases={2: 0},
    )(me, src, buf)


def _pack(arrays):
    rows = []
    for a in arrays:
        flat = a.reshape(-1).astype(F32)
        pad = (-flat.shape[0]) % 1024
        rows.append(jnp.pad(flat, (0, pad)).reshape(-1, 128))
    return jnp.concatenate(rows, axis=0)


def _unpack(packed, shapes):
    out, r = [], 0
    for shp in shapes:
        n = int(np.prod(shp))
        nr = (n + 1023) // 1024 * 8
        out.append(packed[r:r + nr].reshape(-1)[:n].reshape(shp))
        r += nr
    return out


def kernel(x, mix_norm, ffn_norm, a_w_in, a_norm_v, a_w_s, a_b_s, a_w_out, kv_norm, w_kv, b_kv, b_w_q, b_b_q, b_sinks, b_w_o, b_b_o, rel_bias, ffn_w_gate, ffn_w_up, ffn_w_down, final_norm, loss_target, m_mix_norm, m_ffn_norm, m_a_w_in, m_a_norm_v, m_a_w_s, m_a_b_s, m_a_w_out, m_kv_norm, m_w_kv, m_b_kv, m_b_w_q, m_b_b_q, m_b_sinks, m_b_w_o, m_b_b_o, m_rel_bias, m_ffn_w_gate, m_ffn_w_up, m_ffn_w_down, m_final_norm, v_mix_norm, v_ffn_norm, v_a_w_in, v_a_norm_v, v_a_w_s, v_a_b_s, v_a_w_out, v_kv_norm, v_w_kv, v_b_kv, v_b_w_q, v_b_b_q, v_b_sinks, v_b_w_o, v_b_b_o, v_rel_bias, v_ffn_w_gate, v_ffn_w_up, v_ffn_w_down, v_final_norm):
    _, S, D = x.shape
    LA, LB, L = a_w_in.shape[0], b_w_q.shape[0], ffn_w_gate.shape[0]
    F = ffn_w_gate.shape[2]
    DS = D // NDEV
    ZC = a_w_in.shape[2]
    KVW = w_kv.shape[1]
    NKV = KVW // (2 * HEAD_DIM)
    NH = D // HEAD_DIM
    assert ZC * NDEV == 2 * D and NH == NKV * KV_GROUP and S % BLOCK == 0
    TM = min(1024, S)
    TN_ = min(1024, D)
    TS = min(512, D)
    KC = 4

    ix, iy, ic = lax.axis_index("x"), lax.axis_index("y"), lax.axis_index("c")
    me = (4 * ix + 2 * iy + ic).astype(jnp.int32)
    me1 = me.reshape(1)
    where = jnp.stack([ic, 2 * ix + iy]).astype(jnp.int32)

    def tr3(a):
        return jnp.transpose(a, (0, 2, 1))

    gate_t, up_t = tr3(ffn_w_gate), tr3(ffn_w_up)
    w_kv3 = w_kv.reshape((1,) + w_kv.shape)

    def layer_arrays(l):
        arrs = [("gu", 2 * F, D, [(gate_t, l, 0), (up_t, l, F)]), ("down", F, D, [(ffn_w_down, l, 0)])]
        if l < LA:
            arrs += [("win", D, ZC, [(a_w_in, l, 0)]), ("wout", DS, D, [(a_w_out, l, 0)])]
            if l == LA - 1:
                arrs.append(("wkv", DS, KVW, [(w_kv3, 0, 0)]))
        else:
            i_b = l - LA
            arrs.append(("wqo", 2 * DS, D, [(b_w_q, i_b, 0), (b_w_o, i_b, DS)]))
        return arrs

    gathers = []

    def gather_begin(g_idx, deps):
        g = gathers[g_idx]
        g["send"], g["recv"], g["bufs"], g["token"] = _relay_start(f"relay_start{g_idx}", g["bufs"], deps)

    for l in range(L):
        mixer, ffn = dict(keys=[], bufs=[]), dict(keys=[], bufs=[])
        for key, rows, width, sources in layer_arrays(l):
            buf = lax.empty((NDEV, rows, width), BF16)
            for si, (src, li, row0) in enumerate(sources):
                buf = _cast_into(f"cast_{key}{l}_{si}", src, li, buf, row0, me1)
            group = ffn if key in ("gu", "down") else mixer
            group["keys"].append(key)
            group["bufs"].append(buf)
        gathers += [mixer, ffn]
        if l == 0:
            nv_rows = _pack([a_norm_v])
            nv = _cast_into("put_norm_v", nv_rows.reshape((1,) + nv_rows.shape), 0,
                            lax.empty((NDEV,) + nv_rows.shape, F32), 0, me1)
            nv_send, nv_recv, nv_bufs, token = _gather_start("gather_norm_v_start", [nv], [])
            gather_begin(0, [token])

    def gather_relay(g_idx, deps):
        g = gathers[g_idx]
        g["fsend"], g["frecv"], g["bufs"], tok = _relay_neighbors(f"relay_neighbors{g_idx}", g["bufs"], g["send"],
                                                                  g["recv"], deps)
        if g_idx + 1 < len(gathers):
            gather_begin(g_idx + 1, [tok])
            tok = gathers[g_idx + 1]["token"]
        return tok

    def finish_gather(g_idx, deps):
        g = gathers[g_idx]
        gsend, grecv, bufs = _relay_diagonal(f"relay_diagonal{g_idx}", g["bufs"], g["fsend"], g["frecv"], deps)
        bufs = _relay_finish(f"relay_finish{g_idx}", bufs, g["send"], g["recv"], g["fsend"], g["frecv"], gsend, grecv)
        return dict(zip(g["keys"], bufs))

    token = gather_relay(0, [gathers[0]["token"]] + [b for g in gathers[1:] for b in g["bufs"]])

    buckets = jnp.asarray(_bucket_table())
    bias = _bias_table("bias_table", rel_bias.T, buckets).reshape(NH, BLOCK, 2 * BLOCK)

    def rows_full(tm):
        return pl.BlockSpec((tm, D), lambda i, j: (i, 0))

    def tile(tm, tn):
        return pl.BlockSpec((tm, tn), lambda i, j: (i, j))

    vec_tile = pl.BlockSpec((1, TN_), lambda i, j: (0, j))

    def ffn_forward(l, wl, h_mid, tag, deps):
        xf = _rms_fwd(f"ffn_norm_fwd{tag}", h_mid, ffn_norm[l], deps=deps)

        def ep(parts, ex, outs):
            a, b = parts
            sg = jax.nn.sigmoid(a)
            silu = a * sg
            outs[0][0] = (b * (sg * (1.0 + a * (1.0 - sg)))).astype(BF16)
            outs[0][1] = silu.astype(BF16)
            outs[1][...] = (silu * b).astype(BF16)

        ab, hid = _gemm(
            f"ffn_up{tag}", (S // TM, NDEV),
            [(xf, rows_full(TM)),
             (wl["gu"], pl.BlockSpec((None, F, D), lambda i, e: (e, 0, 0))),
             (wl["gu"], pl.BlockSpec((None, F, D), lambda i, e: (e, 1, 0)))],
            [(0, 1, NT), (0, 2, NT)], [],
            [(_sds((2, NDEV, S, F), BF16), pl.BlockSpec((2, None, TM, F), lambda i, e: (0, e, i, 0))),
             (_sds((NDEV, S, F), BF16), pl.BlockSpec((None, TM, F), lambda i, e: (e, i, 0)))],
            ep, separate=True)
        tmd = min(512, S)
        (h_out,) = _gemm(
            f"ffn_down{tag}", (S // tmd, D // TN_),
            [(hid, pl.BlockSpec((NDEV, tmd, F), lambda i, j: (0, i, 0))),
             (wl["down"], pl.BlockSpec((NDEV, F, TN_), lambda i, j: (0, 0, j)))],
            [(0, 1, NN, _pick(c), _pick(c)) for c in range(NDEV)],
            [(h_mid, pl.BlockSpec((tmd, TN_), lambda i, j: (i, j)))],
            [(_sds((S, D), F32), pl.BlockSpec((tmd, TN_), lambda i, j: (i, j)))],
            _store_add_extra)
        return dict(h_mid=h_mid, xf=xf, ab=ab, hid=hid), h_out

    def stacked_rows_gemm(name, a, wmat, blk, extras, ep, out_dtype, deps=()):
        return _gemm(
            name, (S // TM, D // TN_),
            [(a, rows_full(TM)), (wmat, pl.BlockSpec((NDEV, DS, TN_), lambda i, j: (0, blk, j)))],
            [(0, 1, NN, None, _stacked)], extras,
            [(_sds((S, D), out_dtype), tile(TM, TN_))], ep, deps=deps)[0]

    def back_rows_gemm(name, a, wmat, blk, out_dtype, deps=()):
        return _gemm(
            name, (S // TM, NDEV // KC),
            [(a, rows_full(TM)), (wmat, pl.BlockSpec((KC, DS, D), lambda i, e: (e, blk, 0)))],
            [(0, 1, NT, None, _stacked)], [],
            [(_sds((S, D), out_dtype), pl.BlockSpec((TM, KC * DS), lambda i, e: (i, e)))], _store, deps=deps)[0]

    def grad_rows_gemm(name, act, d_bf, buf, blk):
        return _gemm(
            name, (NDEV,),
            [(act, pl.BlockSpec((S, DS), lambda e: (0, e))), (d_bf, pl.BlockSpec((S, D), lambda e: (0, 0)))],
            [(0, 1, TN)], [(buf, ANY)],
            [(_sds(buf.shape, BF16), pl.BlockSpec((None, DS, D), lambda e: (e, blk, 0)))],
            _store, aliases={2: 0})[0]

    saved, weights = [], []
    h = x.reshape(S, D)
    k_heads = v_heads = hn = h_kv = norm_v = None
    for layer in range(L):
        wl = finish_gather(2 * layer, [token] if layer == 0 else [h])
        weights.append(wl)
        if layer == 0:
            nv_fsend, nv_frecv, nv_bufs = _gather_forward("gather_norm_v_forward", nv_bufs, nv_send, nv_recv,
                                                          [wl["win"]])
            (nv_all,) = _gather_finish("gather_norm_v_finish", nv_bufs, nv_send, nv_recv, nv_fsend, nv_frecv)
            norm_v = jnp.transpose(nv_all.reshape(NDEV, -1)[:, :LA * DS].reshape(NDEV, LA, DS), (1, 0, 2)).reshape(LA, D)
        sv = dict(h_in=h)
        xn = _rms_fwd(f"mix_norm_fwd{layer}", h, mix_norm[layer])
        sv["xn"] = xn
        if layer < LA:
            i_a = layer
            (zp,) = _gemm(
                f"gmlp_in{layer}", (S // TM, NDEV),
                [(xn, rows_full(TM)), (wl["win"], pl.BlockSpec((None, D, ZC), lambda i, e: (e, 0, 0)))],
                [(0, 1, NN)], [], [(_sds((S, 2 * D), F32), pl.BlockSpec((TM, ZC), lambda i, e: (i, e)))], _store)
            bst = a_b_s[i_a].T
            gated = _gmlp_fwd(f"gmlp_gate{layer}", zp, norm_v[i_a].reshape(1, D), a_w_s[i_a], bst)
            sv.update(zp=zp, gated=gated, bst=bst)
            relay_token = gather_relay(2 * layer + 1, [gated])
            h_mid = stacked_rows_gemm(f"gmlp_out{layer}", gated, wl["wout"], 0, [(h, tile(TM, TN_))],
                                      _store_add_extra, F32, deps=[relay_token])
        else:
            i_b = layer - LA
            q = stacked_rows_gemm(f"attn_q{layer}", xn, wl["wqo"], 0, [(b_b_q[i_b].reshape(1, D), vec_tile)],
                                  _store_add_extra, BF16)
            relay_token = gather_relay(2 * layer + 1, [q])
            attn, probs, sink_probs = _attn_fwd(f"attn_fwd{layer}", q, k_heads, v_heads, bias, b_sinks[i_b],
                                                deps=[relay_token])
            sv.update(q=q, attn=attn, probs=probs, sink_probs=sink_probs)
            h_mid = stacked_rows_gemm(f"attn_o{layer}", attn, wl["wqo"], 1,
                                      [(h, tile(TM, TN_)), (b_b_o[i_b].reshape(1, D), vec_tile)],
                                      _store_add_extra, F32)
        wl.update(finish_gather(2 * layer + 1, [h_mid]))
        ffn_deps = [gather_relay(2 * layer + 2, [wl["down"]])] if layer + 1 < L else []
        fsv, h = ffn_forward(layer, wl, h_mid, str(layer), ffn_deps)
        sv.update(fsv)
        saved.append(sv)
        if layer == LA - 1:
            h_kv = h
            hn = _rms_fwd("kv_norm_fwd", h, kv_norm)

            def kv_ep(acc, ex, outs):
                val = acc + ex[0][...]
                for hh in range(NKV):
                    outs[0][hh] = val[:, hh * HEAD_DIM:(hh + 1) * HEAD_DIM].astype(BF16)
                    outs[1][hh] = val[:, (NKV + hh) * HEAD_DIM:(NKV + hh + 1) * HEAD_DIM].astype(BF16)

            k_heads, v_heads = _gemm(
                "kv_proj", (S // TM,),
                [(hn, pl.BlockSpec((TM, D), lambda i: (i, 0))),
                 (wl["wkv"], pl.BlockSpec((NDEV, DS, KVW), lambda i: (0, 0, 0)))],
                [(0, 1, NN, None, _stacked)], [(b_kv.reshape(1, KVW), pl.BlockSpec((1, KVW), lambda i: (0, 0)))],
                [(_sds((NKV, S, HEAD_DIM), BF16), pl.BlockSpec((NKV, TM, HEAD_DIM), lambda i: (0, i, 0)))] * 2,
                kv_ep)

    loss11, d, d_bf, g_final = _loss_bwd("loss_bwd", h, final_norm, loss_target.reshape(S, D))
    loss = lax.psum(loss11[0, 0], AXES)

    g_mix, g_ffn = [None] * L, [None] * L
    g_ws, g_bs, g_nv = [None] * LA, [None] * LA, [None] * LA
    g_bq, g_sink, g_bo = [None] * LB, [None] * LB, [None] * LB
    dbiases = []
    kv_parts = []
    g_kvn = g_bkv = None
    exchanges = [[] for _ in range(L)]
    pending = None
    grads_wkv = None
    newest = []

    def new_grads(l):
        return {key: lax.empty((NDEV, rows, width), BF16) for key, rows, width, _ in layer_arrays(l)}

    def exchange_begin(tag, l, gl, keys):
        grads = [gl[k] for k in keys]
        lands = [lax.empty((NCHIP,) + g.shape[1:], BF16) for g in grads]
        send, recv, grads, lands, tok = _sibling_start(f"rs_sibling_start{tag}", grads, lands, [])
        newest[:] = [tok]
        return dict(tag=tag, layer=l, keys=keys, grads=grads, lands=lands, send=send, recv=recv)

    def exchange_middle(st, dep):
        tag = st["tag"]
        grads, lands = _sibling_finish(f"rs_sibling_finish{tag}", st["grads"], st["lands"], st["send"], st["recv"], [dep])
        sums, own = [], []
        for t, key in enumerate(st["keys"]):
            s_, o_ = _pair_sum(f"pair_sum_{key}{tag}", grads[t], lands[t], where)
            sums.append(s_)
            own.append(o_)
        send, recv, sums, own, tok = _chips_start(f"rs_chips_start{tag}", sums, own, [])
        newest[:] = [tok]
        st.update(sums=sums, own=own, send2=send, recv2=recv)
        exchanges[st["layer"]].append(st)

    def exchange_end(st, dep):
        sums, lands = _chips_finish(f"rs_chips_finish{st['tag']}", st["sums"], st["own"], st["send2"], st["recv2"], [dep])
        own = dict(zip(st["keys"], sums)) if st.get("direct") else {k: None for k in st["keys"]}
        return dict(zip(st["keys"], lands)), own

    for layer in reversed(range(L)):
        sv, wl = saved[layer], weights[layer]
        tag = str(layer)
        gl = new_grads(layer)
        if grads_wkv is not None and layer == LA - 1:
            gl["wkv"] = grads_wkv
        def dhid_ep(acc, ex, outs):
            outs[0][0] = (acc * ex[0][0].astype(F32)).astype(BF16)
            outs[0][1] = (acc * ex[0][1].astype(F32)).astype(BF16)

        ab_spec = pl.BlockSpec((2, None, TM, F), lambda i, e: (0, e, i, 0))
        (dab,) = _gemm(
            f"ffn_dhid{tag}", (S // TM, NDEV),
            [(d_bf, rows_full(TM)), (wl["down"], pl.BlockSpec((None, F, D), lambda i, e: (e, 0, 0)))],
            [(0, 1, NT)], [(sv["ab"], ab_spec)], [(_sds((2, NDEV, S, F), BF16), ab_spec)], dhid_ep,
            deps=list(newest))
        if pending:
            exchange_middle(pending, dab)
        act_kinds = [SHARDS, WHOLE, SHARDS2, WHOLE]
        act_lands = [lax.empty((NCHIP, S, F), BF16), lax.empty((S, D), BF16), lax.empty((2, NCHIP, S, F), BF16),
                     lax.empty((S, D), BF16)]
        a_send, a_recv, act, act_lands, tok = _sibling_start(f"act_start{tag}", [sv["hid"], d_bf, dab, sv["xf"]], act_lands,
                                                             list(newest), act_kinds)
        newest[:] = [tok]
        (dxf,) = _gemm(
            f"ffn_dx{tag}", (S // TM, D // TN_, 2 * NDEV // KC),
            [(act[2].reshape(2 * NDEV // KC, KC, S, F), pl.BlockSpec((None, KC, TM, F), lambda i, j, k: (k, 0, i, 0))),
             (wl["gu"], pl.BlockSpec((KC, F, TN_), lambda i, j, k: (k % (NDEV // KC), k // (NDEV // KC), j)))],
            [(0, 1, NN, _pick(c), _pick(c)) for c in range(KC)], [],
            [(_sds((S, D), F32), pl.BlockSpec((TM, TN_), lambda i, j, k: (i, j)))],
            _store, nk=2 * NDEV // KC, acc_shape=(TM, TN_), deps=list(newest))
        (hid_o, dout_o, dab_o, xf_o), (hid_s, dout_s, dab_s, xf_s) = _sibling_finish(
            f"act_finish{tag}", act, act_lands, a_send, a_recv, [dxf], act_kinds)
        (p_down,) = _gemm(
            f"ffn_dwdown{tag}", (NCHIP, D // TN_),
            [(hid_o.reshape(NCHIP, 2, S, F), pl.BlockSpec((None, None, S, F), lambda k, j, wr: (k, wr[0], 0, 0))),
             (dout_o, pl.BlockSpec((S, TN_), lambda k, j, wr: (0, j))),
             (hid_s, pl.BlockSpec((None, S, F), lambda k, j, wr: (k, 0, 0))),
             (dout_s, pl.BlockSpec((S, TN_), lambda k, j, wr: (0, j)))],
            [(0, 1, TN), (2, 3, TN)], [],
            [(_sds((NCHIP, F, D), BF16), pl.BlockSpec((None, F, TN_), lambda k, j, wr: (k, 0, j)))],
            _store, prefetch=[where])
        (p_gu,) = _gemm(
            f"ffn_dwup{tag}", (2, NCHIP, D // TN_),
            [(dab_o.reshape(2, NCHIP, 2, S, F),
              pl.BlockSpec((None, None, None, S, F), lambda w, k, j, wr: (w, k, wr[0], 0, 0))),
             (xf_o, pl.BlockSpec((S, TN_), lambda w, k, j, wr: (0, j))),
             (dab_s, pl.BlockSpec((None, None, S, F), lambda w, k, j, wr: (w, k, 0, 0))),
             (xf_s, pl.BlockSpec((S, TN_), lambda w, k, j, wr: (0, j)))],
            [(0, 1, TN), (2, 3, TN)], [],
            [(_sds((NCHIP, 2 * F, D), BF16), pl.BlockSpec((None, F, TN_), lambda w, k, j, wr: (k, w, j)))],
            _store, prefetch=[where])
        send2, recv2, sums, own, tok = _chips_start(
            f"rs_chips_start_ffn{tag}", [p_gu, p_down], [lax.empty(p_gu.shape, BF16), lax.empty(p_down.shape, BF16)], [])
        newest[:] = [tok]
        exchanges[layer].append(dict(tag=f"_ffn{tag}", layer=layer, keys=["gu", "down"], sums=sums, own=own, send2=send2,
                                     recv2=recv2, direct=True))
        d, d_bf, g_ffn[layer], colsum = _rms_bwd(f"ffn_norm_bwd{tag}", sv["h_mid"], ffn_norm[layer], dxf, d,
                                                 deps=list(newest))
        if layer < LA:
            i_a = layer
            dgated = back_rows_gemm(f"gmlp_dgated{tag}", d_bf, wl["wout"], 0, F32)
            gl["wout"] = grad_rows_gemm(f"gmlp_dwout{tag}", sv["gated"], d_bf, gl["wout"], 0)
            dzp, g_ws[i_a], dbs, g_nv[i_a] = _gmlp_bwd(f"gmlp_bwd{tag}", sv["zp"], dgated,
                                                       norm_v[i_a].reshape(1, D), a_w_s[i_a], sv["bst"])
            g_bs[i_a] = dbs[:, 0, :]
            gl["win"] = _grad_cols(f"gmlp_dwin{tag}", sv["xn"], dzp, gl["win"], TS)
            (dxn,) = _gemm(
                f"gmlp_dx{tag}", (S // TM, D // TN_),
                [(dzp, pl.BlockSpec((TM, NDEV * ZC), lambda i, j: (i, 0))),
                 (wl["win"], pl.BlockSpec((NDEV, TN_, ZC), lambda i, j: (0, j, 0)))],
                [(0, 1, NT, _cols(c, ZC), _pick(c)) for c in range(NDEV)], [],
                [(_sds((S, D), F32), pl.BlockSpec((TM, TN_), lambda i, j: (i, j)))], _store)
        else:
            i_b = layer - LA
            g_bo[i_b] = colsum
            dattn = back_rows_gemm(f"attn_dout{tag}", d_bf, wl["wqo"], 1, BF16)
            gl["wqo"] = grad_rows_gemm(f"attn_dwo{tag}", sv["attn"], d_bf, gl["wqo"], 1)
            dq, g_bq[i_b], dkc, dkp, dvc, dvp, dbias, dsink = _attn_bwd(
                f"attn_bwd{tag}", sv["q"], k_heads, v_heads, dattn, sv["probs"], sv["sink_probs"])
            kv_parts.append((dkc, dkp, dvc, dvp))
            g_sink[i_b] = dsink.reshape(NH)
            dbiases.append(dbias.reshape(NH, BLOCK * 2 * BLOCK))
            gl["wqo"] = grad_rows_gemm(f"attn_dwq{tag}", sv["xn"], dq, gl["wqo"], 0)
            dxn = back_rows_gemm(f"attn_dx{tag}", dq, wl["wqo"], 0, F32)
        d, d_bf, g_mix[layer], _ = _rms_bwd(f"mix_norm_bwd{tag}", sv["h_in"], mix_norm[layer], dxn, d)
        pending = exchange_begin(f"_mix{tag}", layer, gl, [k for k in gl if k not in ("gu", "down")])
        if layer == LA:
            wkv = weights[LA - 1]["wkv"]
            dkv, g_bkv = _kv_grad("kv_grad", kv_parts)
            (grads_wkv,) = _gemm(
                "kv_dw", (NDEV,),
                [(hn, pl.BlockSpec((S, DS), lambda e: (0, e))), (dkv, pl.BlockSpec((S, KVW), lambda e: (0, 0)))],
                [(0, 1, TN)], [(lax.empty((NDEV, DS, KVW), BF16), ANY)],
                [(_sds((NDEV, DS, KVW), BF16), pl.BlockSpec((None, DS, KVW), lambda e: (e, 0, 0)))],
                _store, aliases={2: 0}, deps=list(newest))
            (dhn,) = _gemm(
                "kv_dx", (S // TM, NDEV),
                [(dkv, pl.BlockSpec((TM, KVW), lambda i, e: (i, 0))),
                 (wkv, pl.BlockSpec((None, DS, KVW), lambda i, e: (e, 0, 0)))],
                [(0, 1, NT)], [], [(_sds((S, D), F32), pl.BlockSpec((TM, DS), lambda i, e: (i, e)))], _store)
            d, d_bf, g_kvn, _ = _rms_bwd("kv_norm_bwd", h_kv, kv_norm, dhn, d)
    grad_x = d.reshape(x.shape)

    exchange_middle(pending, d)

    g_rel = _bias_grad("bias_grad", dbiases, buckets)
    small_local = _pack([jnp.concatenate(g_mix, axis=0), jnp.concatenate(g_ffn, axis=0), jnp.stack(g_ws),
                         jnp.stack(g_bs), g_kvn, g_bkv, jnp.concatenate(g_bq, axis=0), jnp.stack(g_sink),
                         jnp.concatenate(g_bo, axis=0), g_rel, g_final, jnp.concatenate(g_nv, axis=0)])
    small_slot = _cast_into("put_small_grads", small_local.reshape((1,) + small_local.shape), 0,
                            lax.empty((NDEV,) + small_local.shape, F32), 0, me1)
    s_send, s_recv, s_bufs, s_tok = _gather_start("gather_small_start", [small_slot], list(newest))

    results = {}
    after = [s_tok]

    def upd(pname, w, m, v, l, li, lands, row0, own=None):
        w3 = w if w.ndim == 3 else w.reshape((1,) + w.shape)
        prev = results.get(pname) or [lax.empty(w3.shape, F32) for _ in range(4)]
        results[pname] = _adamw_shard(f"adamw_{pname}{l}", w3, m.reshape(w3.shape), v.reshape(w3.shape), lands,
                                      row0, li, prev, deps=list(after), own=own, where=where)
        after[:] = [results[pname][0]]

    for l in reversed(range(L)):
        for st in exchanges[l]:
            lands, own = exchange_end(st, after[0])
            if "gu" in lands:
                upd("ffn_w_gate", gate_t, tr3(m_ffn_w_gate), tr3(v_ffn_w_gate), l, l, lands["gu"], 0, own["gu"])
                upd("ffn_w_up", up_t, tr3(m_ffn_w_up), tr3(v_ffn_w_up), l, l, lands["gu"], F, own["gu"])
                upd("ffn_w_down", ffn_w_down, m_ffn_w_down, v_ffn_w_down, l, l, lands["down"], 0, own["down"])
            if "win" in lands:
                upd("a_w_in", a_w_in, m_a_w_in, v_a_w_in, l, l, lands["win"], 0)
                upd("a_w_out", a_w_out, m_a_w_out, v_a_w_out, l, l, lands["wout"], 0)
            if "wkv" in lands:
                upd("w_kv", w_kv, m_w_kv, v_w_kv, l, 0, lands["wkv"], 0)
            if "wqo" in lands:
                upd("b_w_q", b_w_q, m_b_w_q, v_b_w_q, l, l - LA, lands["wqo"], 0)
                upd("b_w_o", b_w_o, m_b_w_o, v_b_w_o, l, l - LA, lands["wqo"], DS)
    for pname in ("ffn_w_gate", "ffn_w_up"):
        results[pname] = [tr3(r) for r in results[pname]]
    results["w_kv"] = [r.reshape(w_kv.shape) for r in results["w_kv"]]

    small_w = [mix_norm, ffn_norm, a_w_s, a_b_s, kv_norm, b_kv, b_b_q, b_sinks, b_b_o, rel_bias, final_norm]
    small_m = [m_mix_norm, m_ffn_norm, m_a_w_s, m_a_b_s, m_kv_norm, m_b_kv, m_b_b_q, m_b_sinks, m_b_b_o, m_rel_bias,
               m_final_norm]
    small_v = [v_mix_norm, v_ffn_norm, v_a_w_s, v_a_b_s, v_kv_norm, v_b_kv, v_b_b_q, v_b_sinks, v_b_b_o, v_rel_bias,
               v_final_norm]
    shapes = [w.shape for w in small_w] + [(LA, D)]
    s_fsend, s_frecv, s_bufs = _gather_forward("gather_small_forward", s_bufs, s_send, s_recv, list(after))
    (small_all,) = _gather_finish("gather_small_finish", s_bufs, s_send, s_recv, s_fsend, s_frecv)
    small_sum = _sum_devices("sum_small_grads", small_all)
    small_g = _unpack(small_sum, shapes)
    g_normv = lax.dynamic_slice_in_dim(small_g[-1], me * DS, DS, axis=1)
    small_g = small_g[:-1] + [g_normv]
    small_w, small_m, small_v = small_w + [a_norm_v], small_m + [m_a_norm_v], small_v + [v_a_norm_v]
    shapes = [w.shape for w in small_w]
    s_delta, s_m, s_v = _adamw_flat("adamw_small", _pack(small_w), _pack(small_g), _pack(small_m), _pack(small_v))
    s_delta, s_m, s_v = _unpack(s_delta, shapes), _unpack(s_m, shapes), _unpack(s_v, shapes)

    names = ["mix_norm", "ffn_norm", "a_w_in", "a_norm_v", "a_w_s", "a_b_s", "a_w_out", "kv_norm", "w_kv", "b_kv",
             "b_w_q", "b_b_q", "b_sinks", "b_w_o", "b_b_o", "rel_bias", "ffn_w_gate", "ffn_w_up", "ffn_w_down",
             "final_norm"]
    small_names = ["mix_norm", "ffn_norm", "a_w_s", "a_b_s", "kv_norm", "b_kv", "b_b_q", "b_sinks", "b_b_o", "rel_bias",
                   "final_norm", "a_norm_v"]
    res = {}
    for idx, nm in enumerate(small_names):
        res[nm] = (small_g[idx].reshape(shapes[idx]), s_delta[idx], s_m[idx], s_v[idx])
    for nm, u in results.items():
        res[nm] = tuple(u)
    out = [loss, grad_x]
    for part in range(4):
        out += [res[nm][part] for nm in names]
    return tuple(out)
```

```python
import math

import numpy as np
import jax
import jax.numpy as jnp
from jax import lax
from jax.experimental import pallas as pl
from jax.experimental.pallas import tpu as pltpu

F32 = jnp.float32
BF16 = jnp.bfloat16
AXES = ("x", "y", "c")
NDEV = 8
NCHIP = 4
CHUNK = 128
GROUPS = 8
HEAD_DIM = 64
KV_GROUP = 8
BLOCK = 128
N_BUCKETS = 32
MAX_DISTANCE = 128
RMS_EPS = 1e-5
NEG_INF = -1e30
ADAM_LR, ADAM_B1, ADAM_B2, ADAM_EPS, ADAM_WD, ADAM_STEP = 0.001, 0.9, 0.999, 1e-08, 0.01, 10
VMEM_LIMIT_BYTES = 56 * 1024 * 1024

NN = (((1,), (0,)), ((), ()))
NT = (((1,), (1,)), ((), ()))
TN = (((0,), (0,)), ((), ()))
ANY = pl.BlockSpec(memory_space=pl.ANY)
HBM = pl.BlockSpec(memory_space=pltpu.HBM)
SEM = pl.BlockSpec(memory_space=pltpu.SEMAPHORE)
MESH = pl.DeviceIdType.MESH
EFFECT = pltpu.SideEffectType.DATAFLOW_SIDE_EFFECTING


def _pcall(body, *, name, out_shape, in_specs, out_specs, grid=(), scratch=(), aliases=None, prefetch=0, deps=()):
    n_in, n_dep = len(in_specs), len(deps)
    if n_dep:
        inner = body

        def body(*refs):
            return inner(*refs[:prefetch + n_in], *refs[prefetch + n_in + n_dep:])

        in_specs = list(in_specs) + [ANY] * n_dep
    params = dict(vmem_limit_bytes=VMEM_LIMIT_BYTES)
    if grid:
        params["dimension_semantics"] = ("arbitrary",) * len(grid)
    kw = dict(name=name, out_shape=out_shape, compiler_params=pltpu.CompilerParams(**params),
              input_output_aliases=aliases or {})
    if prefetch:
        kw["grid_spec"] = pltpu.PrefetchScalarGridSpec(num_scalar_prefetch=prefetch, grid=grid, in_specs=in_specs,
                                                       out_specs=out_specs, scratch_shapes=list(scratch))
    else:
        kw.update(grid=grid, in_specs=in_specs, out_specs=out_specs, scratch_shapes=list(scratch))
    call = pl.pallas_call(body, **kw)
    return lambda *args: call(*args, *deps)


def _sds(shape, dtype):
    return jax.ShapeDtypeStruct(tuple(shape), dtype)


def _position():
    x, y, c = lax.axis_index("x"), lax.axis_index("y"), lax.axis_index("c")
    chips = [(1 - x, y), (x, 1 - y), (1 - x, 1 - y)]
    return x, y, c, chips


def _slot(px, py, pc):
    return 4 * px + 2 * py + pc


def _remote(ref_src, ref_dst, send, recv, to):
    return pltpu.make_async_remote_copy(src_ref=ref_src, dst_ref=ref_dst, send_sem=send, recv_sem=recv,
                                        device_id=to, device_id_type=MESH)


def _hbm(arrays):
    return [pltpu.with_memory_space_constraint(a, pltpu.HBM) for a in arrays]


def _split_call(body, name, out_shape, in_specs, out_specs, aliases):
    return pl.pallas_call(body, name=name, out_shape=out_shape, in_specs=in_specs, out_specs=out_specs,
                          input_output_aliases=aliases, compiler_params=pltpu.CompilerParams(has_side_effects=EFFECT))


def _token_shape():
    return _sds((8, 128), F32)


def _gather_start(name, bufs, deps):
    n, nd = len(bufs), len(deps)

    def body(*refs):
        ins, send, recv, token = refs[:n], refs[n + nd], refs[n + nd + 1], refs[2 * n + nd + 2]
        x, y, c, chips = _position()
        peers = [(x, y, 1 - c)] + [(*chip, c) for chip in chips]
        for t in range(n):
            mine = ins[t].at[_slot(x, y, c)]
            for k, peer in enumerate(peers):
                _remote(mine, mine, send.at[4 * t + k], recv.at[4 * t + k], peer).start()
        token[...] = jnp.zeros_like(token)

    res = _split_call(
        body, name,
        (pltpu.SemaphoreType.DMA((4 * n,)), pltpu.SemaphoreType.DMA((4 * n,)), *[pltpu.HBM(b.shape, b.dtype) for b in bufs],
         _token_shape()),
        [HBM] * n + [ANY] * nd, (SEM, SEM, *[HBM] * n, pl.BlockSpec(memory_space=pltpu.VMEM)),
        {t: 2 + t for t in range(n)})(*_hbm(bufs), *deps)
    return res[0], res[1], list(res[2:2 + n]), res[2 + n]


def _gather_forward(name, bufs, send, recv, deps):
    n, nd = len(bufs), len(deps)

    def body(*refs):
        ins, send_in, recv_in = refs[:n], refs[n], refs[n + 1]
        fsend, frecv = refs[n + 2 + nd], refs[n + 3 + nd]
        x, y, c, chips = _position()
        for j, chip in enumerate(chips):
            for t in range(n):
                blk = ins[t].at[_slot(*chip, c)]
                _remote(blk, blk, send_in.at[4 * t + 1 + j], recv_in.at[4 * t + 1 + j], (*chip, c)).wait_recv()
                _remote(blk, blk, fsend.at[3 * t + j], frecv.at[3 * t + j], (x, y, 1 - c)).start()

    res = _split_call(
        body, name,
        (pltpu.SemaphoreType.DMA((3 * n,)), pltpu.SemaphoreType.DMA((3 * n,)), *[pltpu.HBM(b.shape, b.dtype) for b in bufs]),
        [HBM] * n + [SEM, SEM] + [ANY] * nd, (SEM, SEM, *[HBM] * n),
        {t: 2 + t for t in range(n)})(*_hbm(bufs), send, recv, *deps)
    return res[0], res[1], list(res[2:])


def _gather_finish(name, bufs, send, recv, fsend, frecv):
    n = len(bufs)

    def body(*refs):
        ins, send_in, recv_in, fs_in, fr_in = refs[:n], refs[n], refs[n + 1], refs[n + 2], refs[n + 3]
        x, y, c, chips = _position()
        sibling = (x, y, 1 - c)
        peers = [sibling] + [(*chip, c) for chip in chips]
        for t in range(n):
            blk = ins[t].at[_slot(x, y, 1 - c)]
            _remote(blk, blk, send_in.at[4 * t], recv_in.at[4 * t], sibling).wait_recv()
            for j, chip in enumerate(chips):
                blk = ins[t].at[_slot(*chip, 1 - c)]
                _remote(blk, blk, fs_in.at[3 * t + j], fr_in.at[3 * t + j], sibling).wait_recv()
            mine = ins[t].at[_slot(x, y, c)]
            for k, peer in enumerate(peers):
                _remote(mine, mine, send_in.at[4 * t + k], recv_in.at[4 * t + k], peer).wait_send()
            for j, chip in enumerate(chips):
                blk = ins[t].at[_slot(*chip, c)]
                _remote(blk, blk, fs_in.at[3 * t + j], fr_in.at[3 * t + j], sibling).wait_send()

    res = _split_call(
        body, name, tuple(pltpu.HBM(b.shape, b.dtype) for b in bufs),
        [HBM] * n + [SEM] * 4, tuple([HBM] * n), {t: t for t in range(n)})(*_hbm(bufs), send, recv, fsend, frecv)
    return list(res)


def _halves(ref):
    rows = ref.shape[0] // 2
    return ref.at[pl.ds(0, rows)], ref.at[pl.ds(rows, rows)]


def _relay_start(name, bufs, deps):
    n, nd = len(bufs), len(deps)

    def body(*refs):
        ins, send, recv, token = refs[:n], refs[n + nd], refs[n + nd + 1], refs[2 * n + nd + 2]
        x, y, c, _ = _position()
        peers = [(x, y, 1 - c), (1 - x, y, c), (x, 1 - y, c)]
        for t in range(n):
            mine = ins[t].at[_slot(x, y, c)]
            for k, peer in enumerate(peers):
                _remote(mine, mine, send.at[3 * t + k], recv.at[3 * t + k], peer).start()
        token[...] = jnp.zeros_like(token)

    res = _split_call(
        body, name,
        (pltpu.SemaphoreType.DMA((3 * n,)), pltpu.SemaphoreType.DMA((3 * n,)), *[pltpu.HBM(b.shape, b.dtype) for b in bufs],
         _token_shape()),
        [HBM] * n + [ANY] * nd, (SEM, SEM, *[HBM] * n, pl.BlockSpec(memory_space=pltpu.VMEM)),
        {t: 2 + t for t in range(n)})(*_hbm(bufs), *deps)
    return res[0], res[1], list(res[2:2 + n]), res[2 + n]


def _relay_neighbors(name, bufs, send, recv, deps):
    n, nd = len(bufs), len(deps)

    def body(*refs):
        ins, send_in, recv_in = refs[:n], refs[n], refs[n + 1]
        fsend, frecv, token = refs[n + 2 + nd], refs[n + 3 + nd], refs[2 * n + 4 + nd]
        x, y, c, _ = _position()
        sibling, xn, yn = (x, y, 1 - c), (1 - x, y, c), (x, 1 - y, c)
        for t in range(n):
            blk = ins[t].at[_slot(*xn)]
            _remote(blk, blk, send_in.at[3 * t + 1], recv_in.at[3 * t + 1], xn).wait_recv()
            _remote(blk, blk, fsend.at[4 * t], frecv.at[4 * t], sibling).start()
            half = _halves(blk)[0]
            _remote(half, half, fsend.at[4 * t + 1], frecv.at[4 * t + 1], yn).start()
        for t in range(n):
            blk = ins[t].at[_slot(*yn)]
            _remote(blk, blk, send_in.at[3 * t + 2], recv_in.at[3 * t + 2], yn).wait_recv()
            _remote(blk, blk, fsend.at[4 * t + 2], frecv.at[4 * t + 2], sibling).start()
            half = _halves(blk)[1]
            _remote(half, half, fsend.at[4 * t + 3], frecv.at[4 * t + 3], xn).start()
        token[...] = jnp.zeros_like(token)

    res = _split_call(
        body, name,
        (pltpu.SemaphoreType.DMA((4 * n,)), pltpu.SemaphoreType.DMA((4 * n,)), *[pltpu.HBM(b.shape, b.dtype) for b in bufs],
         _token_shape()),
        [HBM] * n + [SEM, SEM] + [ANY] * nd, (SEM, SEM, *[HBM] * n, pl.BlockSpec(memory_space=pltpu.VMEM)),
        {t: 2 + t for t in range(n)})(*_hbm(bufs), send, recv, *deps)
    return res[0], res[1], list(res[2:2 + n]), res[2 + n]


def _relay_diagonal(name, bufs, fsend, frecv, deps):
    n, nd = len(bufs), len(deps)

    def body(*refs):
        ins, fs_in, fr_in = refs[:n], refs[n], refs[n + 1]
        gsend, grecv = refs[n + 2 + nd], refs[n + 3 + nd]
        x, y, c, _ = _position()
        for t in range(n):
            blk = ins[t].at[_slot(1 - x, 1 - y, c)]
            first, second = _halves(blk)
            _remote(first, first, fs_in.at[4 * t + 1], fr_in.at[4 * t + 1], (x, 1 - y, c)).wait_recv()
            _remote(second, second, fs_in.at[4 * t + 3], fr_in.at[4 * t + 3], (1 - x, y, c)).wait_recv()
            _remote(blk, blk, gsend.at[t], grecv.at[t], (x, y, 1 - c)).start()

    res = _split_call(
        body, name,
        (pltpu.SemaphoreType.DMA((n,)), pltpu.SemaphoreType.DMA((n,)), *[pltpu.HBM(b.shape, b.dtype) for b in bufs]),
        [HBM] * n + [SEM, SEM] + [ANY] * nd, (SEM, SEM, *[HBM] * n),
        {t: 2 + t for t in range(n)})(*_hbm(bufs), fsend, frecv, *deps)
    return res[0], res[1], list(res[2:])


def _relay_finish(name, bufs, send, recv, fsend, frecv, gsend, grecv):
    n = len(bufs)

    def body(*refs):
        ins = refs[:n]
        send_in, recv_in, fs_in, fr_in, gs_in, gr_in = refs[n:n + 6]
        x, y, c, _ = _position()
        sibling, xn, yn = (x, y, 1 - c), (1 - x, y, c), (x, 1 - y, c)
        for t in range(n):
            blk = ins[t].at[_slot(x, y, 1 - c)]
            _remote(blk, blk, send_in.at[3 * t], recv_in.at[3 * t], sibling).wait_recv()
            blk = ins[t].at[_slot(1 - x, y, 1 - c)]
            _remote(blk, blk, fs_in.at[4 * t], fr_in.at[4 * t], sibling).wait_recv()
            blk = ins[t].at[_slot(x, 1 - y, 1 - c)]
            _remote(blk, blk, fs_in.at[4 * t + 2], fr_in.at[4 * t + 2], sibling).wait_recv()
            blk = ins[t].at[_slot(1 - x, 1 - y, 1 - c)]
            _remote(blk, blk, gs_in.at[t], gr_in.at[t], sibling).wait_recv()
            mine = ins[t].at[_slot(x, y, c)]
            for k, peer in enumerate([sibling, xn, yn]):
                _remote(mine, mine, send_in.at[3 * t + k], recv_in.at[3 * t + k], peer).wait_send()
            bx, by = ins[t].at[_slot(*xn)], ins[t].at[_slot(*yn)]
            _remote(bx, bx, fs_in.at[4 * t], fr_in.at[4 * t], sibling).wait_send()
            _remote(_halves(bx)[0], _halves(bx)[0], fs_in.at[4 * t + 1], fr_in.at[4 * t + 1], yn).wait_send()
            _remote(by, by, fs_in.at[4 * t + 2], fr_in.at[4 * t + 2], sibling).wait_send()
            _remote(_halves(by)[1], _halves(by)[1], fs_in.at[4 * t + 3], fr_in.at[4 * t + 3], xn).wait_send()
            bd = ins[t].at[_slot(1 - x, 1 - y, c)]
            _remote(bd, bd, gs_in.at[t], gr_in.at[t], sibling).wait_send()

    res = _split_call(
        body, name, tuple(pltpu.HBM(b.shape, b.dtype) for b in bufs),
        [HBM] * n + [SEM] * 6, tuple([HBM] * n), {t: t for t in range(n)})(
            *_hbm(bufs), send, recv, fsend, frecv, gsend, grecv)
    return list(res)


WHOLE, SHARDS, SHARDS2 = 0, 1, 2


def _sibling_copies(srcs, lands, kinds, c):
    pairs = []
    for s_ref, l_ref, kind in zip(srcs, lands, kinds):
        if kind == WHOLE:
            pairs.append((s_ref, l_ref))
        elif kind == SHARDS:
            pairs += [(s_ref.at[2 * k + (1 - c)], l_ref.at[k]) for k in range(NCHIP)]
        else:
            pairs += [(s_ref.at[w, 2 * k + (1 - c)], l_ref.at[w, k]) for w in range(2) for k in range(NCHIP)]
    return pairs


def _count_copies(kinds):
    return sum({WHOLE: 1, SHARDS: NCHIP, SHARDS2: 2 * NCHIP}[k] for k in kinds)


def _sibling_start(name, srcs, lands, deps, kinds=None):
    n, nd = len(srcs), len(deps)
    kinds = kinds or [SHARDS] * n
    ncp = _count_copies(kinds)

    def body(*refs):
        s_in, l_in = refs[:n], refs[n:2 * n]
        send, recv, token = refs[2 * n + nd], refs[2 * n + nd + 1], refs[4 * n + nd + 2]
        x, y, c, _ = _position()
        for i, (src, dst) in enumerate(_sibling_copies(s_in, l_in, kinds, c)):
            _remote(src, dst, send.at[i], recv.at[i], (x, y, 1 - c)).start()
        token[...] = jnp.zeros_like(token)

    both = list(srcs) + list(lands)
    res = _split_call(
        body, name,
        (pltpu.SemaphoreType.DMA((ncp,)), pltpu.SemaphoreType.DMA((ncp,)),
         *[pltpu.HBM(b.shape, b.dtype) for b in both], _token_shape()),
        [HBM] * (2 * n) + [ANY] * nd, (SEM, SEM, *[HBM] * (2 * n), pl.BlockSpec(memory_space=pltpu.VMEM)),
        {t: 2 + t for t in range(2 * n)})(*_hbm(both), *deps)
    return res[0], res[1], list(res[2:2 + n]), list(res[2 + n:2 + 2 * n]), res[2 + 2 * n]


def _sibling_finish(name, srcs, lands, send, recv, deps, kinds=None):
    n, nd = len(srcs), len(deps)
    kinds = kinds or [SHARDS] * n

    def body(*refs):
        s_in, l_in, send_in, recv_in = refs[:n], refs[n:2 * n], refs[2 * n], refs[2 * n + 1]
        x, y, c, _ = _position()
        for i, (src, dst) in enumerate(_sibling_copies(s_in, l_in, kinds, c)):
            cp = _remote(src, dst, send_in.at[i], recv_in.at[i], (x, y, 1 - c))
            cp.wait_send()
            cp.wait_recv()

    both = list(srcs) + list(lands)
    res = _split_call(
        body, name, tuple(pltpu.HBM(b.shape, b.dtype) for b in both),
        [HBM] * (2 * n) + [SEM, SEM] + [ANY] * nd, tuple([HBM] * (2 * n)),
        {t: t for t in range(2 * n)})(*_hbm(both), send, recv, *deps)
    return list(res[:n]), list(res[n:])


def _chips_start(name, parts, lands, deps):
    n, nd = len(parts), len(deps)

    def body(*refs):
        p_in, l_in = refs[:n], refs[n:2 * n]
        send, recv, token = refs[2 * n + nd], refs[2 * n + nd + 1], refs[4 * n + nd + 2]
        x, y, c, chips = _position()
        for t in range(n):
            for j, chip in enumerate(chips):
                _remote(p_in[t].at[2 * chip[0] + chip[1]], l_in[t].at[2 * x + y], send.at[3 * t + j], recv.at[3 * t + j],
                        (*chip, c)).start()
        token[...] = jnp.zeros_like(token)

    both = list(parts) + list(lands)
    res = _split_call(
        body, name,
        (pltpu.SemaphoreType.DMA((3 * n,)), pltpu.SemaphoreType.DMA((3 * n,)), *[pltpu.HBM(b.shape, b.dtype) for b in both],
         _token_shape()),
        [HBM] * (2 * n) + [ANY] * nd, (SEM, SEM, *[HBM] * (2 * n), pl.BlockSpec(memory_space=pltpu.VMEM)),
        {t: 2 + t for t in range(2 * n)})(*_hbm(both), *deps)
    return res[0], res[1], list(res[2:2 + n]), list(res[2 + n:2 + 2 * n]), res[2 + 2 * n]


def _chips_finish(name, parts, lands, send, recv, deps):
    n, nd = len(parts), len(deps)

    def body(*refs):
        p_in, l_in, send_in, recv_in = refs[:n], refs[n:2 * n], refs[2 * n], refs[2 * n + 1]
        x, y, c, chips = _position()
        for t in range(n):
            for j, chip in enumerate(chips):
                k = 2 * chip[0] + chip[1]
                _remote(p_in[t].at[k], l_in[t].at[k], send_in.at[3 * t + j], recv_in.at[3 * t + j], (*chip, c)).wait_recv()
                _remote(p_in[t].at[k], l_in[t].at[2 * x + y], send_in.at[3 * t + j], recv_in.at[3 * t + j],
                        (*chip, c)).wait_send()

    both = list(parts) + list(lands)
    res = _split_call(
        body, name, tuple(pltpu.HBM(b.shape, b.dtype) for b in both),
        [HBM] * (2 * n) + [SEM, SEM] + [ANY] * nd, tuple([HBM] * (2 * n)),
        {t: t for t in range(2 * n)})(*_hbm(both), send, recv, *deps)
    return list(res[:n]), list(res[n:])


def _pair_sum(name, grad, recv, where):
    _, r, w = grad.shape
    tr = _row_tile(r, w, budget=4 * 1024 * 1024)
    g4 = grad.reshape(NCHIP, 2, r, w)

    def body(where_ref, g_ref, r_ref, o_ref, own_ref):
        val = (g_ref[...].astype(F32) + r_ref[...].astype(F32)).astype(o_ref.dtype)
        o_ref[...] = val

        @pl.when(pl.program_id(1) == where_ref[1])
        def _():
            own_ref[...] = val

    out = _sds((NCHIP, r, w), grad.dtype)
    return _pcall(
        body, name=name, out_shape=[out, out], grid=(r // tr, NCHIP), prefetch=1,
        in_specs=[pl.BlockSpec((None, None, tr, w), lambda i, k, wr: (k, wr[0], i, 0)),
                  pl.BlockSpec((None, tr, w), lambda i, k, wr: (k, i, 0))],
        out_specs=[pl.BlockSpec((None, tr, w), lambda i, k, wr: (k, i, 0)),
                   pl.BlockSpec((None, tr, w), lambda i, k, wr: (wr[1], i, 0))],
    )(where, g4, recv)


def _row_tile(rows, width, budget=2 * 1024 * 1024):
    best = None
    for t in range(16, rows + 1, 16):
        if rows % t == 0 and t * width * 4 <= budget:
            best = t
    if best is None and rows * width * 4 <= budget:
        best = rows
    assert best is not None, (rows, width)
    return best


def _gemm(name, grid, operands, prods, extras, outs, epilogue, *, nk=1, acc_shape=None, aliases=None, separate=False,
          deps=(), prefetch=()):
    n_op, n_ex, n_out = len(operands), len(extras), len(outs)

    def body(*refs):
        refs = refs[len(prefetch):]
        ops, ex, out_refs = refs[:n_op], refs[n_op:n_op + n_ex], refs[n_op + n_ex:n_op + n_ex + n_out]
        parts = []
        for pr in prods:
            a, b = ops[pr[0]], ops[pr[1]]
            av = pr[3](a) if len(pr) > 3 and pr[3] else a[...]
            bv = pr[4](b) if len(pr) > 4 and pr[4] else b[...]
            parts.append(lax.dot_general(av, bv, pr[2], preferred_element_type=F32))
        if separate:
            epilogue(parts, ex, out_refs)
            return
        part = parts[0]
        for p in parts[1:]:
            part = part + p
        if nk == 1:
            epilogue(part, ex, out_refs)
        else:
            acc = refs[-1]
            k = pl.program_id(len(grid) - 1)

            @pl.when(k == 0)
            def _():
                acc[...] = part

            @pl.when(k > 0)
            def _():
                acc[...] += part

            @pl.when(k == nk - 1)
            def _():
                epilogue(acc[...], ex, out_refs)

    res = _pcall(
        body, name=name, out_shape=[o[0] for o in outs], grid=grid,
        in_specs=[o[1] for o in operands] + [e[1] for e in extras], out_specs=[o[1] for o in outs],
        scratch=[pltpu.VMEM(acc_shape, F32)] if nk > 1 else [], aliases=aliases, deps=deps, prefetch=len(prefetch),
    )(*prefetch, *[o[0] for o in operands], *[e[0] for e in extras])
    return list(res)


def _store(acc, ex, outs):
    outs[0][...] = acc.astype(outs[0].dtype)


def _store_add_extra(acc, ex, outs):
    v = acc
    for e in ex:
        v = v + e[...]
    outs[0][...] = v.astype(outs[0].dtype)


def _stacked(ref):
    b = ref[...]
    return b.reshape(b.shape[0] * b.shape[1], b.shape[2])


def _pick(c):
    return lambda ref: ref[c]


def _cols(c, width):
    return lambda ref: ref[:, c * width:(c + 1) * width]


def _grad_cols(name, act, dy, buf, ts):
    s, k = act.shape
    nd, _, n = buf.shape

    def body(a_ref, dy_ref, b_ref, o_ref, at_ref):
        @pl.when(pl.program_id(1) == 0)
        def _():
            at_ref[...] = a_ref[...].T

        o_ref[...] = jnp.dot(at_ref[...], dy_ref[...], preferred_element_type=F32).astype(o_ref.dtype)

    return _pcall(
        body, name=name, out_shape=_sds(buf.shape, buf.dtype), grid=(k // ts, nd),
        in_specs=[pl.BlockSpec((s, ts), lambda i, e: (0, i)), pl.BlockSpec((s, n), lambda i, e: (0, e)), ANY],
        out_specs=pl.BlockSpec((None, ts, n), lambda i, e: (e, i, 0)), aliases={2: 0},
        scratch=[pltpu.VMEM((ts, s), act.dtype)],
    )(act, dy, buf)


def _gelu_parts(z):
    c = math.sqrt(2.0 / math.pi)
    t = jnp.tanh(c * (z + 0.044715 * (z * z * z)))
    val = 0.5 * z * (1.0 + t)
    grad = 0.5 * (1.0 + t) + 0.5 * z * (1.0 - t * t) * (c * (1.0 + 3.0 * 0.044715 * z * z))
    return val, grad


def _rms_fwd(name, h, g, deps=()):
    s, d = h.shape
    tr = _row_tile(s, d)

    def body(h_ref, g_ref, o_ref):
        hv = h_ref[...]
        r = lax.rsqrt(jnp.mean(hv * hv, axis=-1, keepdims=True) + RMS_EPS)
        o_ref[...] = (hv * r * g_ref[...]).astype(o_ref.dtype)

    return _pcall(
        body, name=name, out_shape=_sds((s, d), BF16), grid=(s // tr,),
        in_specs=[pl.BlockSpec((tr, d), lambda i: (i, 0)), pl.BlockSpec((1, d), lambda i: (0, 0))],
        out_specs=pl.BlockSpec((tr, d), lambda i: (i, 0)), deps=deps,
    )(h, g.reshape(1, d))


def _accumulate(ref, val, first):
    @pl.when(first)
    def _():
        ref[...] = val

    @pl.when(jnp.logical_not(first))
    def _():
        ref[...] += val


def _rms_bwd(name, h, g, dy, res, deps=()):
    s, d = h.shape
    tr = _row_tile(s, d, budget=2 * 1024 * 1024)

    def body(h_ref, g_ref, dy_ref, res_ref, dh_ref, dhb_ref, dg_ref, cs_ref):
        hv = h_ref[...]
        r = lax.rsqrt(jnp.mean(hv * hv, axis=-1, keepdims=True) + RMS_EPS)
        xhat = hv * r
        dyv = dy_ref[...]
        dxh = dyv * g_ref[...]
        dh = res_ref[...] + r * (dxh - xhat * jnp.mean(dxh * xhat, axis=-1, keepdims=True))
        dh_ref[...] = dh
        dhb_ref[...] = dh.astype(BF16)
        first = pl.program_id(0) == 0
        _accumulate(dg_ref, jnp.sum(dyv * xhat, axis=0, keepdims=True), first)
        _accumulate(cs_ref, jnp.sum(dh, axis=0, keepdims=True), first)

    row = pl.BlockSpec((tr, d), lambda i: (i, 0))
    vec = pl.BlockSpec((1, d), lambda i: (0, 0))
    return _pcall(
        body, name=name, out_shape=[_sds((s, d), F32), _sds((s, d), BF16), _sds((1, d), F32), _sds((1, d), F32)],
        grid=(s // tr,), in_specs=[row, vec, row, row], out_specs=[row, row, vec, vec], deps=deps,
    )(h, g.reshape(1, d), dy, res)


def _loss_bwd(name, h, g, target):
    s, d = h.shape
    tr = _row_tile(s, d, budget=1024 * 1024)

    def body(h_ref, g_ref, t_ref, loss_ref, dh_ref, dhb_ref, dg_ref):
        hv = h_ref[...]
        r = lax.rsqrt(jnp.mean(hv * hv, axis=-1, keepdims=True) + RMS_EPS)
        xhat = hv * r
        diff = xhat * g_ref[...] - t_ref[...]
        part = jnp.sum(jnp.sum(diff * diff, axis=1, keepdims=True), axis=0, keepdims=True) * (0.5 / d)
        dyv = diff * (1.0 / d)
        dxh = dyv * g_ref[...]
        dh = r * (dxh - xhat * jnp.mean(dxh * xhat, axis=-1, keepdims=True))
        dh_ref[...] = dh
        dhb_ref[...] = dh.astype(BF16)
        first = pl.program_id(0) == 0
        _accumulate(loss_ref, part, first)
        _accumulate(dg_ref, jnp.sum(dyv * xhat, axis=0, keepdims=True), first)

    row = pl.BlockSpec((tr, d), lambda i: (i, 0))
    vec = pl.BlockSpec((1, d), lambda i: (0, 0))
    one = pl.BlockSpec((1, 1), lambda i: (0, 0))
    return _pcall(
        body, name=name, out_shape=[_sds((1, 1), F32), _sds((s, d), F32), _sds((s, d), BF16), _sds((1, d), F32)],
        grid=(s // tr,), in_specs=[row, vec, row], out_specs=[one, row, row, vec],
    )(h, g.reshape(1, d), target)


def _tril_mask():
    return lax.broadcasted_iota(jnp.int32, (CHUNK, CHUNK), 0) >= lax.broadcasted_iota(jnp.int32, (CHUNK, CHUNK), 1)


def _gmlp_fwd(name, zp, gv, ws, bst):
    s, d2 = zp.shape
    d = d2 // 2
    gw = d // GROUPS

    def body(zp_ref, gv_ref, ws_ref, bst_ref, o_ref):
        u, _ = _gelu_parts(zp_ref[:, :d])
        v, _ = _gelu_parts(zp_ref[:, d:])
        rv = lax.rsqrt(jnp.mean(v * v, axis=-1, keepdims=True) + RMS_EPS)
        vn = (v * rv * gv_ref[...]).astype(BF16)
        tril = _tril_mask()
        for g in range(GROUPS):
            sl = slice(g * gw, (g + 1) * gw)
            wc = jnp.where(tril, ws_ref[g], 0.0).astype(BF16)
            sg = jnp.dot(wc, vn[:, sl], preferred_element_type=F32) + bst_ref[:, g:g + 1]
            o_ref[:, sl] = (u[:, sl] * sg).astype(o_ref.dtype)

    return _pcall(
        body, name=name, out_shape=_sds((s, d), BF16), grid=(s // CHUNK,),
        in_specs=[pl.BlockSpec((CHUNK, d2), lambda i: (i, 0)), pl.BlockSpec((1, d), lambda i: (0, 0)),
                  pl.BlockSpec((GROUPS, CHUNK, CHUNK), lambda i: (0, 0, 0)),
                  pl.BlockSpec((CHUNK, GROUPS), lambda i: (0, 0))],
        out_specs=pl.BlockSpec((CHUNK, d), lambda i: (i, 0)),
    )(zp, gv, ws, bst)


def _gmlp_bwd(name, zp, dgated, gv, ws, bst):
    s, d2 = zp.shape
    d = d2 // 2
    gw = d // GROUPS

    def body(zp_ref, dg_ref, gv_ref, ws_ref, bst_ref, dzp_ref, dws_ref, dbs_ref, dgv_ref, dvn_ref):
        u, gu = _gelu_parts(zp_ref[:, :d])
        v, gvv = _gelu_parts(zp_ref[:, d:])
        rv = lax.rsqrt(jnp.mean(v * v, axis=-1, keepdims=True) + RMS_EPS)
        vhat = v * rv
        vn = (vhat * gv_ref[...]).astype(BF16)
        tril = _tril_mask()
        first = pl.program_id(0) == 0
        ones = jnp.ones((8, gw), F32)

        @pl.when(first)
        def _():
            dws_ref[...] = jnp.zeros_like(dws_ref)
            dbs_ref[...] = jnp.zeros_like(dbs_ref)

        for g in range(GROUPS):
            sl = slice(g * gw, (g + 1) * gw)
            wc = jnp.where(tril, ws_ref[g], 0.0).astype(BF16)
            sg = jnp.dot(wc, vn[:, sl], preferred_element_type=F32) + bst_ref[:, g:g + 1]
            dgs = dg_ref[:, sl]
            ds = dgs * u[:, sl]
            dsb = ds.astype(BF16)
            dzp_ref[:, sl] = (dgs * sg * gu[:, sl]).astype(dzp_ref.dtype)
            dvn_ref[:, sl] = lax.dot_general(wc, dsb, TN, preferred_element_type=F32)
            dw = lax.dot_general(dsb, vn[:, sl], NT, preferred_element_type=F32)
            dws_ref[g] += jnp.where(tril, dw, 0.0)
            dbs_ref[g] += lax.dot_general(ones, ds, NT, preferred_element_type=F32, precision=lax.Precision.HIGHEST)
        dvn = dvn_ref[...]
        dvh = dvn * gv_ref[...]
        dv = rv * (dvh - vhat * jnp.mean(dvh * vhat, axis=-1, keepdims=True))
        dzp_ref[:, d:] = (dv * gvv).astype(dzp_ref.dtype)
        _accumulate(dgv_ref, jnp.sum(dvn * vhat, axis=0, keepdims=True), first)

    return _pcall(
        body, name=name,
        out_shape=[_sds((s, d2), BF16), _sds((GROUPS, CHUNK, CHUNK), F32), _sds((GROUPS, 8, CHUNK), F32),
                   _sds((1, d), F32)],
        grid=(s // CHUNK,),
        in_specs=[pl.BlockSpec((CHUNK, d2), lambda i: (i, 0)), pl.BlockSpec((CHUNK, d), lambda i: (i, 0)),
                  pl.BlockSpec((1, d), lambda i: (0, 0)), pl.BlockSpec((GROUPS, CHUNK, CHUNK), lambda i: (0, 0, 0)),
                  pl.BlockSpec((CHUNK, GROUPS), lambda i: (0, 0))],
        out_specs=[pl.BlockSpec((CHUNK, d2), lambda i: (i, 0)),
                   pl.BlockSpec((GROUPS, CHUNK, CHUNK), lambda i: (0, 0, 0)),
                   pl.BlockSpec((GROUPS, 8, CHUNK), lambda i: (0, 0, 0)), pl.BlockSpec((1, d), lambda i: (0, 0))],
        scratch=[pltpu.VMEM((CHUNK, d), F32)],
    )(zp, dgated, gv, ws, bst)


def _bucket_table():
    dist = np.arange(BLOCK)[:, None] + BLOCK - np.arange(2 * BLOCK)[None, :]
    in_window = (dist >= 0) & (dist < BLOCK)
    dd = np.clip(dist, 0, None)
    max_exact = N_BUCKETS // 2
    dl = np.maximum(dd, 1).astype(np.float32)
    large = max_exact + (np.log(dl / np.float32(max_exact)) / np.float32(math.log(MAX_DISTANCE / max_exact))
                         * np.float32(N_BUCKETS - max_exact)).astype(np.int32)
    large = np.minimum(large, N_BUCKETS - 1)
    bucket = np.where(dd < max_exact, dd, large)
    return np.where(in_window, bucket, -1).astype(np.int32).reshape(1, -1)


def _bias_table(name, rel_bias_t, buckets):
    nh = rel_bias_t.shape[0]
    p = buckets.shape[1]
    tp = 4096

    def body(rb_ref, bk_ref, o_ref):
        bk = bk_ref[...]
        onehot = (lax.broadcasted_iota(jnp.int32, (N_BUCKETS, tp), 0) == bk).astype(F32)
        val = jnp.dot(rb_ref[...], onehot, preferred_element_type=F32, precision=lax.Precision.HIGHEST)
        o_ref[...] = jnp.where(bk >= 0, val, NEG_INF)

    return _pcall(
        body, name=name, out_shape=_sds((nh, p), F32), grid=(p // tp,),
        in_specs=[pl.BlockSpec((nh, N_BUCKETS), lambda i: (0, 0)), pl.BlockSpec((1, tp), lambda i: (0, i))],
        out_specs=pl.BlockSpec((nh, tp), lambda i: (0, i)),
    )(rel_bias_t, buckets)


def _bias_grad(name, dbiases, buckets):
    nh, p = dbiases[0].shape
    n = len(dbiases)
    tp = 4096

    def body(*refs):
        bk_ref, o_ref = refs[n], refs[n + 1]
        onehot = (lax.broadcasted_iota(jnp.int32, (N_BUCKETS, tp), 0) == bk_ref[...]).astype(F32)
        db = refs[0][...]
        for r in refs[1:n]:
            db = db + r[...]
        part = lax.dot_general(onehot, db, NT, preferred_element_type=F32, precision=lax.Precision.HIGHEST)
        _accumulate(o_ref, part, pl.program_id(0) == 0)

    return _pcall(
        body, name=name, out_shape=_sds((N_BUCKETS, nh), F32), grid=(p // tp,),
        in_specs=[pl.BlockSpec((nh, tp), lambda i: (0, i))] * n + [pl.BlockSpec((1, tp), lambda i: (0, i))],
        out_specs=pl.BlockSpec((N_BUCKETS, nh), lambda i: (0, 0)),
    )(*dbiases, buckets)


def _stack_heads(ref, g):
    base = g * KV_GROUP * HEAD_DIM
    return jnp.concatenate([ref[:, base + hh * HEAD_DIM:base + (hh + 1) * HEAD_DIM] for hh in range(KV_GROUP)], axis=0)


def _attn_probs(q, kb, bias, s_ref, first_head):
    penalty = jnp.where(pl.program_id(1) > 0, 0.0, NEG_INF).astype(F32)
    col = lax.broadcasted_iota(jnp.int32, (1, 2 * BLOCK), 1)
    bias = bias.reshape(KV_GROUP * BLOCK, 2 * BLOCK) + jnp.where(col < BLOCK, penalty, 0.0)
    sink = jnp.concatenate([jnp.full((BLOCK, 1), s_ref[first_head + hh], F32) for hh in range(KV_GROUP)], axis=0)
    s = lax.dot_general(q, kb, NT, preferred_element_type=F32) * 0.125 + bias
    m = jnp.maximum(jnp.max(s, axis=-1, keepdims=True), sink)
    p = jnp.exp(s - m)
    es = jnp.exp(sink - m)
    inv = 1.0 / (jnp.sum(p, axis=-1, keepdims=True) + es)
    return p * inv, es * inv


def _attn_specs(ng):
    gq = ng * KV_GROUP * HEAD_DIM
    q_spec = pl.BlockSpec((BLOCK, gq), lambda kh, i: (i, kh))
    prev = pl.BlockSpec((ng, BLOCK, HEAD_DIM), lambda kh, i: (kh, jnp.maximum(i - 1, 0), 0))
    cur = pl.BlockSpec((ng, BLOCK, HEAD_DIM), lambda kh, i: (kh, i, 0))
    bias = pl.BlockSpec((ng * KV_GROUP, BLOCK, 2 * BLOCK), lambda kh, i: (kh, 0, 0))
    smem = pl.BlockSpec(memory_space=pltpu.SMEM)
    probs = pl.BlockSpec((ng, None, KV_GROUP * BLOCK, 2 * BLOCK), lambda kh, i: (kh, i, 0, 0))
    sink_probs = pl.BlockSpec((ng, None, KV_GROUP * BLOCK, 1), lambda kh, i: (kh, i, 0, 0))
    return q_spec, prev, cur, bias, smem, probs, sink_probs


def _kv_heads_per_step(nkv):
    return 2 if nkv % 2 == 0 else 1


def _attn_fwd(name, q, k, v, bias, sinks, deps=()):
    s, dq = q.shape
    nkv = k.shape[0]
    ng = 1
    q_spec, prev, cur, bias_spec, smem, p_spec, ps_spec = _attn_specs(ng)

    def body(q_ref, kp_ref, kc_ref, vp_ref, vc_ref, b_ref, s_ref, o_ref, p_ref, ps_ref):
        for g in range(ng):
            kb = jnp.concatenate([kp_ref[g], kc_ref[g]], axis=0)
            vb = jnp.concatenate([vp_ref[g], vc_ref[g]], axis=0)
            p, ps = _attn_probs(_stack_heads(q_ref, g), kb, b_ref[g * KV_GROUP:(g + 1) * KV_GROUP], s_ref,
                                (pl.program_id(0) * ng + g) * KV_GROUP)
            pb = p.astype(BF16)
            p_ref[g] = pb
            ps_ref[g] = ps
            o = jnp.dot(pb, vb, preferred_element_type=F32)
            for hh in range(KV_GROUP):
                col = (g * KV_GROUP + hh) * HEAD_DIM
                o_ref[:, col:col + HEAD_DIM] = o[hh * BLOCK:(hh + 1) * BLOCK].astype(o_ref.dtype)

    return _pcall(
        body, name=name,
        out_shape=[_sds((s, dq), BF16), _sds((nkv, s // BLOCK, KV_GROUP * BLOCK, 2 * BLOCK), BF16),
                   _sds((nkv, s // BLOCK, KV_GROUP * BLOCK, 1), F32)],
        grid=(nkv // ng, s // BLOCK),
        in_specs=[q_spec, prev, cur, prev, cur, bias_spec, smem], out_specs=[q_spec, p_spec, ps_spec], deps=deps,
    )(q, k, k, v, v, bias, sinks)


def _attn_bwd(name, q, k, v, do, probs, sink_probs):
    s, dq = q.shape
    nkv = k.shape[0]
    ng = _kv_heads_per_step(nkv)
    gq = ng * KV_GROUP * HEAD_DIM
    q_spec, prev, cur, bias_spec, _, p_spec, ps_spec = _attn_specs(ng)

    def body(q_ref, do_ref, kp_ref, kc_ref, vp_ref, vc_ref, p_ref, ps_ref,
             dq_ref, dbq_ref, dkc_ref, dkp_ref, dvc_ref, dvp_ref, dbias_ref, dsink_ref):
        @pl.when(pl.program_id(1) == 0)
        def _():
            dbias_ref[...] = jnp.zeros_like(dbias_ref)
            dsink_ref[...] = jnp.zeros_like(dsink_ref)
            dbq_ref[...] = jnp.zeros_like(dbq_ref)

        for g in range(ng):
            kb = jnp.concatenate([kp_ref[g], kc_ref[g]], axis=0)
            vb = jnp.concatenate([vp_ref[g], vc_ref[g]], axis=0)
            q, do = _stack_heads(q_ref, g), _stack_heads(do_ref, g)
            pb = p_ref[g]
            p = pb.astype(F32)
            dp = lax.dot_general(do, vb, NT, preferred_element_type=F32)
            delta = jnp.sum(p * dp, axis=-1, keepdims=True)
            ds = p * (dp - delta)
            dsb = ds.astype(BF16)
            dq = jnp.dot(dsb, kb, preferred_element_type=F32) * 0.125
            dsk = -(ps_ref[g] * delta)
            for hh in range(KV_GROUP):
                col, rows = (g * KV_GROUP + hh) * HEAD_DIM, slice(hh * BLOCK, (hh + 1) * BLOCK)
                dq_ref[:, col:col + HEAD_DIM] = dq[rows].astype(dq_ref.dtype)
                dbq_ref[:, col:col + HEAD_DIM] += jnp.sum(dq[rows], axis=0, keepdims=True)
                dsink_ref[g, :, hh:hh + 1] += jnp.sum(dsk[rows], axis=0, keepdims=True)
            dkb = lax.dot_general(dsb, q, TN, preferred_element_type=F32) * 0.125
            dvb = lax.dot_general(pb, do, TN, preferred_element_type=F32)
            dkp_ref[g], dkc_ref[g] = dkb[:BLOCK], dkb[BLOCK:]
            dvp_ref[g], dvc_ref[g] = dvb[:BLOCK], dvb[BLOCK:]
            dbias_ref[g * KV_GROUP:(g + 1) * KV_GROUP] += ds.reshape(KV_GROUP, BLOCK, 2 * BLOCK)

    kv_out = _sds((nkv, s, HEAD_DIM), F32)
    return _pcall(
        body, name=name,
        out_shape=[_sds((s, dq), BF16), _sds((1, dq), F32), kv_out, kv_out, kv_out, kv_out,
                   _sds((nkv * KV_GROUP, BLOCK, 2 * BLOCK), F32), _sds((nkv, 1, KV_GROUP), F32)],
        grid=(nkv // ng, s // BLOCK),
        in_specs=[q_spec, q_spec, prev, cur, prev, cur, p_spec, ps_spec],
        out_specs=[q_spec, pl.BlockSpec((1, gq), lambda kh, i: (0, kh)), cur, cur, cur, cur, bias_spec,
                   pl.BlockSpec((ng, 1, KV_GROUP), lambda kh, i: (kh, 0, 0))],
    )(q, do, k, k, v, v, probs, sink_probs)


def _kv_grad(name, parts):
    nkv, s, _ = parts[0][0].shape
    nb = s // BLOCK
    w = 2 * nkv * HEAD_DIM
    n = len(parts)

    def body(*refs):
        o_ref, cs_ref = refs[4 * n], refs[4 * n + 1]
        i = pl.program_id(0)
        keep = jnp.where(i < nb - 1, 1.0, 0.0).astype(F32)

        @pl.when(i == 0)
        def _():
            cs_ref[...] = jnp.zeros_like(cs_ref)

        for which in range(2):
            for hh in range(nkv):
                val = None
                for l in range(n):
                    cur_ref, nxt_ref = refs[4 * l + 2 * which], refs[4 * l + 2 * which + 1]
                    t = cur_ref[hh] + keep * nxt_ref[hh]
                    val = t if val is None else val + t
                sl = slice((which * nkv + hh) * HEAD_DIM, (which * nkv + hh + 1) * HEAD_DIM)
                o_ref[:, sl] = val.astype(o_ref.dtype)
                cs_ref[:, sl] += jnp.sum(val, axis=0, keepdims=True)

    cur = pl.BlockSpec((nkv, BLOCK, HEAD_DIM), lambda i: (0, i, 0))
    nxt = pl.BlockSpec((nkv, BLOCK, HEAD_DIM), lambda i: (0, jnp.minimum(i + 1, nb - 1), 0))
    flat = [a for p in parts for a in p]
    return _pcall(
        body, name=name, out_shape=[_sds((s, w), BF16), _sds((1, w), F32)], grid=(nb,),
        in_specs=[cur, nxt] * (2 * n),
        out_specs=[pl.BlockSpec((BLOCK, w), lambda i: (i, 0)), pl.BlockSpec((1, w), lambda i: (0, 0))],
    )(*flat)


def _adamw_math(w, g, m, v):
    m = ADAM_B1 * m + (1.0 - ADAM_B1) * g
    v = ADAM_B2 * v + (1.0 - ADAM_B2) * (g * g)
    m_hat = m / (1.0 - ADAM_B1 ** ADAM_STEP)
    v_hat = v / (1.0 - ADAM_B2 ** ADAM_STEP)
    delta = -ADAM_LR * (m_hat / (jnp.sqrt(v_hat) + ADAM_EPS) + ADAM_WD * w)
    return delta, m, v


def _adamw_shard(name, w, m, v, parts, row0, layer, prev, deps=(), own=None, where=None):
    _, r, wd = w.shape
    tr = _row_tile(r, wd, budget=1024 * 1024)
    assert row0 % tr == 0

    def step(w_ref, m_ref, v_ref, g, g_ref, d_ref, nm_ref, nv_ref):
        delta, nm, nv = _adamw_math(w_ref[...], g, m_ref[...], v_ref[...])
        g_ref[...], d_ref[...], nm_ref[...], nv_ref[...] = g, delta, nm, nv

    out = _sds(w.shape, F32)
    if own is None:
        def body(w_ref, m_ref, v_ref, p_ref, a0, a1, a2, a3, g_ref, d_ref, nm_ref, nv_ref):
            g = p_ref[0].astype(F32)
            for k in range(1, NCHIP):
                g = g + p_ref[k].astype(F32)
            step(w_ref, m_ref, v_ref, g, g_ref, d_ref, nm_ref, nv_ref)

        par = pl.BlockSpec((None, tr, wd), lambda i: (layer, i, 0))
        return _pcall(
            body, name=name, out_shape=[out, out, out, out], grid=(r // tr,),
            in_specs=[par, par, par, pl.BlockSpec((NCHIP, tr, wd), lambda i: (0, row0 // tr + i, 0)), ANY, ANY, ANY, ANY],
            out_specs=[par, par, par, par], aliases={4: 0, 5: 1, 6: 2, 7: 3}, deps=deps,
        )(w, m, v, parts, *prev)

    def body(where_ref, w_ref, m_ref, v_ref, p_ref, o_ref, a0, a1, a2, a3, g_ref, d_ref, nm_ref, nv_ref):
        mine = lax.broadcasted_iota(jnp.int32, (tr, wd), 0) * 0 + where_ref[1]
        g = None
        for k in range(NCHIP):
            t = jnp.where(mine == k, o_ref[...], p_ref[k]).astype(F32)
            g = t if g is None else g + t
        step(w_ref, m_ref, v_ref, g, g_ref, d_ref, nm_ref, nv_ref)

    par = pl.BlockSpec((None, tr, wd), lambda i, wr: (layer, i, 0))
    return _pcall(
        body, name=name, out_shape=[out, out, out, out], grid=(r // tr,), prefetch=1,
        in_specs=[par, par, par, pl.BlockSpec((NCHIP, tr, wd), lambda i, wr: (0, row0 // tr + i, 0)),
                  pl.BlockSpec((None, tr, wd), lambda i, wr: (wr[1], row0 // tr + i, 0)), ANY, ANY, ANY, ANY],
        out_specs=[par, par, par, par], aliases={6: 0, 7: 1, 8: 2, 9: 3}, deps=deps,
    )(where, w, m, v, parts, own, *prev)


def _sum_devices(name, gathered):
    _, r, wd = gathered.shape

    def body(g_ref, o_ref):
        acc = g_ref[0]
        for k in range(1, NDEV):
            acc = acc + g_ref[k]
        o_ref[...] = acc

    return _pcall(body, name=name, out_shape=_sds((r, wd), F32), grid=(1,),
                  in_specs=[pl.BlockSpec((NDEV, r, wd), lambda i: (0, 0, 0))],
                  out_specs=pl.BlockSpec((r, wd), lambda i: (0, 0)))(gathered)


def _adamw_flat(name, w, g, m, v):
    shape = w.shape

    def body(w_ref, g_ref, m_ref, v_ref, d_ref, nm_ref, nv_ref):
        d_ref[...], nm_ref[...], nv_ref[...] = _adamw_math(w_ref[...], g_ref[...], m_ref[...], v_ref[...])

    spec = pl.BlockSpec(shape, lambda i: (0, 0))
    out = _sds(shape, F32)
    return _pcall(body, name=name, out_shape=[out, out, out], grid=(1,), in_specs=[spec] * 4,
                  out_specs=[spec] * 3)(w, g, m, v)


def _cast_into(name, src, layer, buf, row0, me):
    _, r, wd = src.shape
    tr = _row_tile(r, wd)
    assert row0 % tr == 0

    def body(me_ref, s_ref, b_ref, o_ref):
        o_ref[...] = s_ref[...].astype(o_ref.dtype)

    return _pcall(
        body, name=name, out_shape=_sds(buf.shape, buf.dtype), grid=(r // tr,), prefetch=1,
        in_specs=[pl.BlockSpec((None, tr, wd), lambda i, mr: (layer, i, 0)), ANY],
        out_specs=pl.BlockSpec((None, tr, wd), lambda i, mr: (mr[0], row0 // tr + i, 0)), aliases={2: 0},
    )(me, src, buf)


def _pack(arrays):
    rows = []
    for a in arrays:
        flat = a.reshape(-1).astype(F32)
        pad = (-flat.shape[0]) % 1024
        rows.append(jnp.pad(flat, (0, pad)).reshape(-1, 128))
    return jnp.concatenate(rows, axis=0)


def _unpack(packed, shapes):
    out, r = [], 0
    for shp in shapes:
        n = int(np.prod(shp))
        nr = (n + 1023) // 1024 * 8
        out.append(packed[r:r + nr].reshape(-1)[:n].reshape(shp))
        r += nr
    return out


def kernel(x, mix_norm, ffn_norm, a_w_in, a_norm_v, a_w_s, a_b_s, a_w_out, kv_norm, w_kv, b_kv, b_w_q, b_b_q, b_sinks, b_w_o, b_b_o, rel_bias, ffn_w_gate, ffn_w_up, ffn_w_down, final_norm, loss_target, m_mix_norm, m_ffn_norm, m_a_w_in, m_a_norm_v, m_a_w_s, m_a_b_s, m_a_w_out, m_kv_norm, m_w_kv, m_b_kv, m_b_w_q, m_b_b_q, m_b_sinks, m_b_w_o, m_b_b_o, m_rel_bias, m_ffn_w_gate, m_ffn_w_up, m_ffn_w_down, m_final_norm, v_mix_norm, v_ffn_norm, v_a_w_in, v_a_norm_v, v_a_w_s, v_a_b_s, v_a_w_out, v_kv_norm, v_w_kv, v_b_kv, v_b_w_q, v_b_b_q, v_b_sinks, v_b_w_o, v_b_b_o, v_rel_bias, v_ffn_w_gate, v_ffn_w_up, v_ffn_w_down, v_final_norm):
    _, S, D = x.shape
    LA, LB, L = a_w_in.shape[0], b_w_q.shape[0], ffn_w_gate.shape[0]
    F = ffn_w_gate.shape[2]
    DS = D // NDEV
    ZC = a_w_in.shape[2]
    KVW = w_kv.shape[1]
    NKV = KVW // (2 * HEAD_DIM)
    NH = D // HEAD_DIM
    assert ZC * NDEV == 2 * D and NH == NKV * KV_GROUP and S % BLOCK == 0
    TM = min(1024, S)
    TN_ = min(1024, D)
    TS = min(512, D)
    KC = 4

    ix, iy, ic = lax.axis_index("x"), lax.axis_index("y"), lax.axis_index("c")
    me = (4 * ix + 2 * iy + ic).astype(jnp.int32)
    me1 = me.reshape(1)
    where = jnp.stack([ic, 2 * ix + iy]).astype(jnp.int32)

    def tr3(a):
        return jnp.transpose(a, (0, 2, 1))

    gate_t, up_t = tr3(ffn_w_gate), tr3(ffn_w_up)
    w_kv3 = w_kv.reshape((1,) + w_kv.shape)

    def layer_arrays(l):
        arrs = [("gu", 2 * F, D, [(gate_t, l, 0), (up_t, l, F)]), ("down", F, D, [(ffn_w_down, l, 0)])]
        if l < LA:
            arrs += [("win", D, ZC, [(a_w_in, l, 0)]), ("wout", DS, D, [(a_w_out, l, 0)])]
            if l == LA - 1:
                arrs.append(("wkv", DS, KVW, [(w_kv3, 0, 0)]))
        else:
            i_b = l - LA
            arrs.append(("wqo", 2 * DS, D, [(b_w_q, i_b, 0), (b_w_o, i_b, DS)]))
        return arrs

    gathers = []

    def gather_begin(g_idx, deps):
        g = gathers[g_idx]
        g["send"], g["recv"], g["bufs"], g["token"] = _relay_start(f"relay_start{g_idx}", g["bufs"], deps)

    for l in range(L):
        mixer, ffn = dict(keys=[], bufs=[]), dict(keys=[], bufs=[])
        for key, rows, width, sources in layer_arrays(l):
            buf = lax.empty((NDEV, rows, width), BF16)
            for si, (src, li, row0) in enumerate(sources):
                buf = _cast_into(f"cast_{key}{l}_{si}", src, li, buf, row0, me1)
            group = ffn if key in ("gu", "down") else mixer
            group["keys"].append(key)
            group["bufs"].append(buf)
        gathers += [mixer, ffn]
        if l == 0:
            nv_rows = _pack([a_norm_v])
            nv = _cast_into("put_norm_v", nv_rows.reshape((1,) + nv_rows.shape), 0,
                            lax.empty((NDEV,) + nv_rows.shape, F32), 0, me1)
            nv_send, nv_recv, nv_bufs, token = _gather_start("gather_norm_v_start", [nv], [])
            gather_begin(0, [token])

    def gather_relay(g_idx, deps):
        g = gathers[g_idx]
        g["fsend"], g["frecv"], g["bufs"], tok = _relay_neighbors(f"relay_neighbors{g_idx}", g["bufs"], g["send"],
                                                                  g["recv"], deps)
        if g_idx + 1 < len(gathers):
            gather_begin(g_idx + 1, [tok])
            tok = gathers[g_idx + 1]["token"]
        return tok

    def finish_gather(g_idx, deps):
        g = gathers[g_idx]
        gsend, grecv, bufs = _relay_diagonal(f"relay_diagonal{g_idx}", g["bufs"], g["fsend"], g["frecv"], deps)
        bufs = _relay_finish(f"relay_finish{g_idx}", bufs, g["send"], g["recv"], g["fsend"], g["frecv"], gsend, grecv)
        return dict(zip(g["keys"], bufs))

    token = gather_relay(0, [gathers[0]["token"]] + [b for g in gathers[1:] for b in g["bufs"]])

    buckets = jnp.asarray(_bucket_table())
    bias = _bias_table("bias_table", rel_bias.T, buckets).reshape(NH, BLOCK, 2 * BLOCK)

    def rows_full(tm):
        return pl.BlockSpec((tm, D), lambda i, j: (i, 0))

    def tile(tm, tn):
        return pl.BlockSpec((tm, tn), lambda i, j: (i, j))

    vec_tile = pl.BlockSpec((1, TN_), lambda i, j: (0, j))

    def ffn_forward(l, wl, h_mid, tag, deps):
        xf = _rms_fwd(f"ffn_norm_fwd{tag}", h_mid, ffn_norm[l], deps=deps)

        def ep(parts, ex, outs):
            a, b = parts
            sg = jax.nn.sigmoid(a)
            silu = a * sg
            outs[0][0] = (b * (sg * (1.0 + a * (1.0 - sg)))).astype(BF16)
            outs[0][1] = silu.astype(BF16)
            outs[1][...] = (silu * b).astype(BF16)

        ab, hid = _gemm(
            f"ffn_up{tag}", (S // TM, NDEV),
            [(xf, rows_full(TM)),
             (wl["gu"], pl.BlockSpec((None, F, D), lambda i, e: (e, 0, 0))),
             (wl["gu"], pl.BlockSpec((None, F, D), lambda i, e: (e, 1, 0)))],
            [(0, 1, NT), (0, 2, NT)], [],
            [(_sds((2, NDEV, S, F), BF16), pl.BlockSpec((2, None, TM, F), lambda i, e: (0, e, i, 0))),
             (_sds((NDEV, S, F), BF16), pl.BlockSpec((None, TM, F), lambda i, e: (e, i, 0)))],
            ep, separate=True)
        tmd = min(512, S)
        (h_out,) = _gemm(
            f"ffn_down{tag}", (S // tmd, D // TN_),
            [(hid, pl.BlockSpec((NDEV, tmd, F), lambda i, j: (0, i, 0))),
             (wl["down"], pl.BlockSpec((NDEV, F, TN_), lambda i, j: (0, 0, j)))],
            [(0, 1, NN, _pick(c), _pick(c)) for c in range(NDEV)],
            [(h_mid, pl.BlockSpec((tmd, TN_), lambda i, j: (i, j)))],
            [(_sds((S, D), F32), pl.BlockSpec((tmd, TN_), lambda i, j: (i, j)))],
            _store_add_extra)
        return dict(h_mid=h_mid, xf=xf, ab=ab, hid=hid), h_out

    def stacked_rows_gemm(name, a, wmat, blk, extras, ep, out_dtype, deps=()):
        return _gemm(
            name, (S // TM, D // TN_),
            [(a, rows_full(TM)), (wmat, pl.BlockSpec((NDEV, DS, TN_), lambda i, j: (0, blk, j)))],
            [(0, 1, NN, None, _stacked)], extras,
            [(_sds((S, D), out_dtype), tile(TM, TN_))], ep, deps=deps)[0]

    def back_rows_gemm(name, a, wmat, blk, out_dtype, deps=()):
        return _gemm(
            name, (S // TM, NDEV // KC),
            [(a, rows_full(TM)), (wmat, pl.BlockSpec((KC, DS, D), lambda i, e: (e, blk, 0)))],
            [(0, 1, NT, None, _stacked)], [],
            [(_sds((S, D), out_dtype), pl.BlockSpec((TM, KC * DS), lambda i, e: (i, e)))], _store, deps=deps)[0]

    def grad_rows_gemm(name, act, d_bf, buf, blk):
        return _gemm(
            name, (NDEV,),
            [(act, pl.BlockSpec((S, DS), lambda e: (0, e))), (d_bf, pl.BlockSpec((S, D), lambda e: (0, 0)))],
            [(0, 1, TN)], [(buf, ANY)],
            [(_sds(buf.shape, BF16), pl.BlockSpec((None, DS, D), lambda e: (e, blk, 0)))],
            _store, aliases={2: 0})[0]

    saved, weights = [], []
    h = x.reshape(S, D)
    k_heads = v_heads = hn = h_kv = norm_v = None
    for layer in range(L):
        wl = finish_gather(2 * layer, [token] if layer == 0 else [h])
        weights.append(wl)
        if layer == 0:
            nv_fsend, nv_frecv, nv_bufs = _gather_forward("gather_norm_v_forward", nv_bufs, nv_send, nv_recv,
                                                          [wl["win"]])
            (nv_all,) = _gather_finish("gather_norm_v_finish", nv_bufs, nv_send, nv_recv, nv_fsend, nv_frecv)
            norm_v = jnp.transpose(nv_all.reshape(NDEV, -1)[:, :LA * DS].reshape(NDEV, LA, DS), (1, 0, 2)).reshape(LA, D)
        sv = dict(h_in=h)
        xn = _rms_fwd(f"mix_norm_fwd{layer}", h, mix_norm[layer])
        sv["xn"] = xn
        if layer < LA:
            i_a = layer
            (zp,) = _gemm(
                f"gmlp_in{layer}", (S // TM, NDEV),
                [(xn, rows_full(TM)), (wl["win"], pl.BlockSpec((None, D, ZC), lambda i, e: (e, 0, 0)))],
                [(0, 1, NN)], [], [(_sds((S, 2 * D), F32), pl.BlockSpec((TM, ZC), lambda i, e: (i, e)))], _store)
            bst = a_b_s[i_a].T
            gated = _gmlp_fwd(f"gmlp_gate{layer}", zp, norm_v[i_a].reshape(1, D), a_w_s[i_a], bst)
            sv.update(zp=zp, gated=gated, bst=bst)
            relay_token = gather_relay(2 * layer + 1, [gated])
            h_mid = stacked_rows_gemm(f"gmlp_out{layer}", gated, wl["wout"], 0, [(h, tile(TM, TN_))],
                                      _store_add_extra, F32, deps=[relay_token])
        else:
            i_b = layer - LA
            q = stacked_rows_gemm(f"attn_q{layer}", xn, wl["wqo"], 0, [(b_b_q[i_b].reshape(1, D), vec_tile)],
                                  _store_add_extra, BF16)
            relay_token = gather_relay(2 * layer + 1, [q])
            attn, probs, sink_probs = _attn_fwd(f"attn_fwd{layer}", q, k_heads, v_heads, bias, b_sinks[i_b],
                                                deps=[relay_token])
            sv.update(q=q, attn=attn, probs=probs, sink_probs=sink_probs)
            h_mid = stacked_rows_gemm(f"attn_o{layer}", attn, wl["wqo"], 1,
                                      [(h, tile(TM, TN_)), (b_b_o[i_b].reshape(1, D), vec_tile)],
                                      _store_add_extra, F32)
        wl.update(finish_gather(2 * layer + 1, [h_mid]))
        ffn_deps = [gather_relay(2 * layer + 2, [wl["down"]])] if layer + 1 < L else []
        fsv, h = ffn_forward(layer, wl, h_mid, str(layer), ffn_deps)
        sv.update(fsv)
        saved.append(sv)
        if layer == LA - 1:
            h_kv = h
            hn = _rms_fwd("kv_norm_fwd", h, kv_norm)

            def kv_ep(acc, ex, outs):
                val = acc + ex[0][...]
                for hh in range(NKV):
                    outs[0][hh] = val[:, hh * HEAD_DIM:(hh + 1) * HEAD_DIM].astype(BF16)
                    outs[1][hh] = val[:, (NKV + hh) * HEAD_DIM:(NKV + hh + 1) * HEAD_DIM].astype(BF16)

            k_heads, v_heads = _gemm(
                "kv_proj", (S // TM,),
                [(hn, pl.BlockSpec((TM, D), lambda i: (i, 0))),
                 (wl["wkv"], pl.BlockSpec((NDEV, DS, KVW), lambda i: (0, 0, 0)))],
                [(0, 1, NN, None, _stacked)], [(b_kv.reshape(1, KVW), pl.BlockSpec((1, KVW), lambda i: (0, 0)))],
                [(_sds((NKV, S, HEAD_DIM), BF16), pl.BlockSpec((NKV, TM, HEAD_DIM), lambda i: (0, i, 0)))] * 2,
                kv_ep)

    loss11, d, d_bf, g_final = _loss_bwd("loss_bwd", h, final_norm, loss_target.reshape(S, D))
    loss = lax.psum(loss11[0, 0], AXES)

    g_mix, g_ffn = [None] * L, [None] * L
    g_ws, g_bs, g_nv = [None] * LA, [None] * LA, [None] * LA
    g_bq, g_sink, g_bo = [None] * LB, [None] * LB, [None] * LB
    dbiases = []
    kv_parts = []
    g_kvn = g_bkv = None
    exchanges = [[] for _ in range(L)]
    pending = None
    grads_wkv = None
    newest = []

    def new_grads(l):
        return {key: lax.empty((NDEV, rows, width), BF16) for key, rows, width, _ in layer_arrays(l)}

    def exchange_begin(tag, l, gl, keys):
        grads = [gl[k] for k in keys]
        lands = [lax.empty((NCHIP,) + g.shape[1:], BF16) for g in grads]
        send, recv, grads, lands, tok = _sibling_start(f"rs_sibling_start{tag}", grads, lands, [])
        newest[:] = [tok]
        return dict(tag=tag, layer=l, keys=keys, grads=grads, lands=lands, send=send, recv=recv)

    def exchange_middle(st, dep):
        tag = st["tag"]
        grads, lands = _sibling_finish(f"rs_sibling_finish{tag}", st["grads"], st["lands"], st["send"], st["recv"], [dep])
        sums, own = [], []
        for t, key in enumerate(st["keys"]):
            s_, o_ = _pair_sum(f"pair_sum_{key}{tag}", grads[t], lands[t], where)
            sums.append(s_)
            own.append(o_)
        send, recv, sums, own, tok = _chips_start(f"rs_chips_start{tag}", sums, own, [])
        newest[:] = [tok]
        st.update(sums=sums, own=own, send2=send, recv2=recv)
        exchanges[st["layer"]].append(st)

    def exchange_end(st, dep):
        sums, lands = _chips_finish(f"rs_chips_finish{st['tag']}", st["sums"], st["own"], st["send2"], st["recv2"], [dep])
        own = dict(zip(st["keys"], sums)) if st.get("direct") else {k: None for k in st["keys"]}
        return dict(zip(st["keys"], lands)), own

    for layer in reversed(range(L)):
        sv, wl = saved[layer], weights[layer]
        tag = str(layer)
        gl = new_grads(layer)
        if grads_wkv is not None and layer == LA - 1:
            gl["wkv"] = grads_wkv
        def dhid_ep(acc, ex, outs):
            outs[0][0] = (acc * ex[0][0].astype(F32)).astype(BF16)
            outs[0][1] = (acc * ex[0][1].astype(F32)).astype(BF16)

        ab_spec = pl.BlockSpec((2, None, TM, F), lambda i, e: (0, e, i, 0))
        (dab,) = _gemm(
            f"ffn_dhid{tag}", (S // TM, NDEV),
            [(d_bf, rows_full(TM)), (wl["down"], pl.BlockSpec((None, F, D), lambda i, e: (e, 0, 0)))],
            [(0, 1, NT)], [(sv["ab"], ab_spec)], [(_sds((2, NDEV, S, F), BF16), ab_spec)], dhid_ep,
            deps=list(newest))
        if pending:
            exchange_middle(pending, dab)
        act_kinds = [SHARDS, WHOLE, SHARDS2, WHOLE]
        act_lands = [lax.empty((NCHIP, S, F), BF16), lax.empty((S, D), BF16), lax.empty((2, NCHIP, S, F), BF16),
                     lax.empty((S, D), BF16)]
        a_send, a_recv, act, act_lands, tok = _sibling_start(f"act_start{tag}", [sv["hid"], d_bf, dab, sv["xf"]], act_lands,
                                                             list(newest), act_kinds)
        newest[:] = [tok]
        (dxf,) = _gemm(
            f"ffn_dx{tag}", (S // TM, D // TN_, 2 * NDEV // KC),
            [(act[2].reshape(2 * NDEV // KC, KC, S, F), pl.BlockSpec((None, KC, TM, F), lambda i, j, k: (k, 0, i, 0))),
             (wl["gu"], pl.BlockSpec((KC, F, TN_), lambda i, j, k: (k % (NDEV // KC), k // (NDEV // KC), j)))],
            [(0, 1, NN, _pick(c), _pick(c)) for c in range(KC)], [],
            [(_sds((S, D), F32), pl.BlockSpec((TM, TN_), lambda i, j, k: (i, j)))],
            _store, nk=2 * NDEV // KC, acc_shape=(TM, TN_), deps=list(newest))
        (hid_o, dout_o, dab_o, xf_o), (hid_s, dout_s, dab_s, xf_s) = _sibling_finish(
            f"act_finish{tag}", act, act_lands, a_send, a_recv, [dxf], act_kinds)
        (p_down,) = _gemm(
            f"ffn_dwdown{tag}", (NCHIP, D // TN_),
            [(hid_o.reshape(NCHIP, 2, S, F), pl.BlockSpec((None, None, S, F), lambda k, j, wr: (k, wr[0], 0, 0))),
             (dout_o, pl.BlockSpec((S, TN_), lambda k, j, wr: (0, j))),
             (hid_s, pl.BlockSpec((None, S, F), lambda k, j, wr: (k, 0, 0))),
             (dout_s, pl.BlockSpec((S, TN_), lambda k, j, wr: (0, j)))],
            [(0, 1, TN), (2, 3, TN)], [],
            [(_sds((NCHIP, F, D), BF16), pl.BlockSpec((None, F, TN_), lambda k, j, wr: (k, 0, j)))],
            _store, prefetch=[where])
        (p_gu,) = _gemm(
            f"ffn_dwup{tag}", (2, NCHIP, D // TN_),
            [(dab_o.reshape(2, NCHIP, 2, S, F),
              pl.BlockSpec((None, None, None, S, F), lambda w, k, j, wr: (w, k, wr[0], 0, 0))),
             (xf_o, pl.BlockSpec((S, TN_), lambda w, k, j, wr: (0, j))),
             (dab_s, pl.BlockSpec((None, None, S, F), lambda w, k, j, wr: (w, k, 0, 0))),
             (xf_s, pl.BlockSpec((S, TN_), lambda w, k, j, wr: (0, j)))],
            [(0, 1, TN), (2, 3, TN)], [],
            [(_sds((NCHIP, 2 * F, D), BF16), pl.BlockSpec((None, F, TN_), lambda w, k, j, wr: (k, w, j)))],
            _store, prefetch=[where])
        send2, recv2, sums, own, tok = _chips_start(
            f"rs_chips_start_ffn{tag}", [p_gu, p_down], [lax.empty(p_gu.shape, BF16), lax.empty(p_down.shape, BF16)], [])
        newest[:] = [tok]
        exchanges[layer].append(dict(tag=f"_ffn{tag}", layer=layer, keys=["gu", "down"], sums=sums, own=own, send2=send2,
                                     recv2=recv2, direct=True))
        d, d_bf, g_ffn[layer], colsum = _rms_bwd(f"ffn_norm_bwd{tag}", sv["h_mid"], ffn_norm[layer], dxf, d,
                                                 deps=list(newest))
        if layer < LA:
            i_a = layer
            dgated = back_rows_gemm(f"gmlp_dgated{tag}", d_bf, wl["wout"], 0, F32)
            gl["wout"] = grad_rows_gemm(f"gmlp_dwout{tag}", sv["gated"], d_bf, gl["wout"], 0)
            dzp, g_ws[i_a], dbs, g_nv[i_a] = _gmlp_bwd(f"gmlp_bwd{tag}", sv["zp"], dgated,
                                                       norm_v[i_a].reshape(1, D), a_w_s[i_a], sv["bst"])
            g_bs[i_a] = dbs[:, 0, :]
            gl["win"] = _grad_cols(f"gmlp_dwin{tag}", sv["xn"], dzp, gl["win"], TS)
            (dxn,) = _gemm(
                f"gmlp_dx{tag}", (S // TM, D // TN_),
                [(dzp, pl.BlockSpec((TM, NDEV * ZC), lambda i, j: (i, 0))),
                 (wl["win"], pl.BlockSpec((NDEV, TN_, ZC), lambda i, j: (0, j, 0)))],
                [(0, 1, NT, _cols(c, ZC), _pick(c)) for c in range(NDEV)], [],
                [(_sds((S, D), F32), pl.BlockSpec((TM, TN_), lambda i, j: (i, j)))], _store)
        else:
            i_b = layer - LA
            g_bo[i_b] = colsum
            dattn = back_rows_gemm(f"attn_dout{tag}", d_bf, wl["wqo"], 1, BF16)
            gl["wqo"] = grad_rows_gemm(f"attn_dwo{tag}", sv["attn"], d_bf, gl["wqo"], 1)
            dq, g_bq[i_b], dkc, dkp, dvc, dvp, dbias, dsink = _attn_bwd(
                f"attn_bwd{tag}", sv["q"], k_heads, v_heads, dattn, sv["probs"], sv["sink_probs"])
            kv_parts.append((dkc, dkp, dvc, dvp))
            g_sink[i_b] = dsink.reshape(NH)
            dbiases.append(dbias.reshape(NH, BLOCK * 2 * BLOCK))
            gl["wqo"] = grad_rows_gemm(f"attn_dwq{tag}", sv["xn"], dq, gl["wqo"], 0)
            dxn = back_rows_gemm(f"attn_dx{tag}", dq, wl["wqo"], 0, F32)
        d, d_bf, g_mix[layer], _ = _rms_bwd(f"mix_norm_bwd{tag}", sv["h_in"], mix_norm[layer], dxn, d)
        pending = exchange_begin(f"_mix{tag}", layer, gl, [k for k in gl if k not in ("gu", "down")])
        if layer == LA:
            wkv = weights[LA - 1]["wkv"]
            dkv, g_bkv = _kv_grad("kv_grad", kv_parts)
            (grads_wkv,) = _gemm(
                "kv_dw", (NDEV,),
                [(hn, pl.BlockSpec((S, DS), lambda e: (0, e))), (dkv, pl.BlockSpec((S, KVW), lambda e: (0, 0)))],
                [(0, 1, TN)], [(lax.empty((NDEV, DS, KVW), BF16), ANY)],
                [(_sds((NDEV, DS, KVW), BF16), pl.BlockSpec((None, DS, KVW), lambda e: (e, 0, 0)))],
                _store, aliases={2: 0}, deps=list(newest))
            (dhn,) = _gemm(
                "kv_dx", (S // TM, NDEV),
                [(dkv, pl.BlockSpec((TM, KVW), lambda i, e: (i, 0))),
                 (wkv, pl.BlockSpec((None, DS, KVW), lambda i, e: (e, 0, 0)))],
                [(0, 1, NT)], [], [(_sds((S, D), F32), pl.BlockSpec((TM, DS), lambda i, e: (i, e)))], _store)
            d, d_bf, g_kvn, _ = _rms_bwd("kv_norm_bwd", h_kv, kv_norm, dhn, d)
    grad_x = d.reshape(x.shape)

    exchange_middle(pending, d)

    g_rel = _bias_grad("bias_grad", dbiases, buckets)
    small_local = _pack([jnp.concatenate(g_mix, axis=0), jnp.concatenate(g_ffn, axis=0), jnp.stack(g_ws),
                         jnp.stack(g_bs), g_kvn, g_bkv, jnp.concatenate(g_bq, axis=0), jnp.stack(g_sink),
                         jnp.concatenate(g_bo, axis=0), g_rel, g_final, jnp.concatenate(g_nv, axis=0)])
    small_slot = _cast_into("put_small_grads", small_local.reshape((1,) + small_local.shape), 0,
                            lax.empty((NDEV,) + small_local.shape, F32), 0, me1)
    s_send, s_recv, s_bufs, s_tok = _gather_start("gather_small_start", [small_slot], list(newest))

    results = {}
    after = [s_tok]

    def upd(pname, w, m, v, l, li, lands, row0, own=None):
        w3 = w if w.ndim == 3 else w.reshape((1,) + w.shape)
        prev = results.get(pname) or [lax.empty(w3.shape, F32) for _ in range(4)]
        results[pname] = _adamw_shard(f"adamw_{pname}{l}", w3, m.reshape(w3.shape), v.reshape(w3.shape), lands,
                                      row0, li, prev, deps=list(after), own=own, where=where)
        after[:] = [results[pname][0]]

    for l in reversed(range(L)):
        for st in exchanges[l]:
            lands, own = exchange_end(st, after[0])
            if "gu" in lands:
                upd("ffn_w_gate", gate_t, tr3(m_ffn_w_gate), tr3(v_ffn_w_gate), l, l, lands["gu"], 0, own["gu"])
                upd("ffn_w_up", up_t, tr3(m_ffn_w_up), tr3(v_ffn_w_up), l, l, lands["gu"], F, own["gu"])
                upd("ffn_w_down", ffn_w_down, m_ffn_w_down, v_ffn_w_down, l, l, lands["down"], 0, own["down"])
            if "win" in lands:
                upd("a_w_in", a_w_in, m_a_w_in, v_a_w_in, l, l, lands["win"], 0)
                upd("a_w_out", a_w_out, m_a_w_out, v_a_w_out, l, l, lands["wout"], 0)
            if "wkv" in lands:
                upd("w_kv", w_kv, m_w_kv, v_w_kv, l, 0, lands["wkv"], 0)
            if "wqo" in lands:
                upd("b_w_q", b_w_q, m_b_w_q, v_b_w_q, l, l - LA, lands["wqo"], 0)
                upd("b_w_o", b_w_o, m_b_w_o, v_b_w_o, l, l - LA, lands["wqo"], DS)
    for pname in ("ffn_w_gate", "ffn_w_up"):
        results[pname] = [tr3(r) for r in results[pname]]
    results["w_kv"] = [r.reshape(w_kv.shape) for r in results["w_kv"]]

    small_w = [mix_norm, ffn_norm, a_w_s, a_b_s, kv_norm, b_kv, b_b_q, b_sinks, b_b_o, rel_bias, final_norm]
    small_m = [m_mix_norm, m_ffn_norm, m_a_w_s, m_a_b_s, m_kv_norm, m_b_kv, m_b_b_q, m_b_sinks, m_b_b_o, m_rel_bias,
               m_final_norm]
    small_v = [v_mix_norm, v_ffn_norm, v_a_w_s, v_a_b_s, v_kv_norm, v_b_kv, v_b_b_q, v_b_sinks, v_b_b_o, v_rel_bias,
               v_final_norm]
    shapes = [w.shape for w in small_w] + [(LA, D)]
    s_fsend, s_frecv, s_bufs = _gather_forward("gather_small_forward", s_bufs, s_send, s_recv, list(after))
    (small_all,) = _gather_finish("gather_small_finish", s_bufs, s_send, s_recv, s_fsend, s_frecv)
    small_sum = _sum_devices("sum_small_grads", small_all)
    small_g = _unpack(small_sum, shapes)
    g_normv = lax.dynamic_slice_in_dim(small_g[-1], me * DS, DS, axis=1)
    small_g = small_g[:-1] + [g_normv]
    small_w, small_m, small_v = small_w + [a_norm_v], small_m + [m_a_norm_v], small_v + [v_a_norm_v]
    shapes = [w.shape for w in small_w]
    s_delta, s_m, s_v = _adamw_flat("adamw_small", _pack(small_w), _pack(small_g), _pack(small_m), _pack(small_v))
    s_delta, s_m, s_v = _unpack(s_delta, shapes), _unpack(s_m, shapes), _unpack(s_v, shapes)

    names = ["mix_norm", "ffn_norm", "a_w_in", "a_norm_v", "a_w_s", "a_b_s", "a_w_out", "kv_norm", "w_kv", "b_kv",
             "b_w_q", "b_b_q", "b_sinks", "b_w_o", "b_b_o", "rel_bias", "ffn_w_gate", "ffn_w_up", "ffn_w_down",
             "final_norm"]
    small_names = ["mix_norm", "ffn_norm", "a_w_s", "a_b_s", "kv_norm", "b_kv", "b_b_q", "b_sinks", "b_b_o", "rel_bias",
                   "final_norm", "a_norm_v"]
    res = {}
    for idx, nm in enumerate(small_names):
        res[nm] = (small_g[idx].reshape(shapes[idx]), s_delta[idx], s_m[idx], s_v[idx])
    for nm, u in results.items():
        res[nm] = tuple(u)
    out = [loss, grad_x]
    for part in range(4):
        out += [res[nm][part] for nm in names]
    return tuple(out)
```

```python
import math

import numpy as np
import jax
import jax.numpy as jnp
from jax import lax
from jax.experimental import pallas as pl
from jax.experimental.pallas import tpu as pltpu

F32 = jnp.float32
BF16 = jnp.bfloat16
AXES = ("x", "y", "c")
NDEV = 8
NCHIP = 4
CHUNK = 128
GROUPS = 8
HEAD_DIM = 64
KV_GROUP = 8
BLOCK = 128
N_BUCKETS = 32
MAX_DISTANCE = 128
RMS_EPS = 1e-5
NEG_INF = -1e30
ADAM_LR, ADAM_B1, ADAM_B2, ADAM_EPS, ADAM_WD, ADAM_STEP = 0.001, 0.9, 0.999, 1e-08, 0.01, 10
VMEM_LIMIT_BYTES = 56 * 1024 * 1024

NN = (((1,), (0,)), ((), ()))
NT = (((1,), (1,)), ((), ()))
TN = (((0,), (0,)), ((), ()))
ANY = pl.BlockSpec(memory_space=pl.ANY)
HBM = pl.BlockSpec(memory_space=pltpu.HBM)
SEM = pl.BlockSpec(memory_space=pltpu.SEMAPHORE)
MESH = pl.DeviceIdType.MESH
EFFECT = pltpu.SideEffectType.DATAFLOW_SIDE_EFFECTING


def _pcall(body, *, name, out_shape, in_specs, out_specs, grid=(), scratch=(), aliases=None, prefetch=0, deps=()):
    n_in, n_dep = len(in_specs), len(deps)
    if n_dep:
        inner = body

        def body(*refs):
            return inner(*refs[:prefetch + n_in], *refs[prefetch + n_in + n_dep:])

        in_specs = list(in_specs) + [ANY] * n_dep
    params = dict(vmem_limit_bytes=VMEM_LIMIT_BYTES)
    if grid:
        params["dimension_semantics"] = ("arbitrary",) * len(grid)
    kw = dict(name=name, out_shape=out_shape, compiler_params=pltpu.CompilerParams(**params),
              input_output_aliases=aliases or {})
    if prefetch:
        kw["grid_spec"] = pltpu.PrefetchScalarGridSpec(num_scalar_prefetch=prefetch, grid=grid, in_specs=in_specs,
                                                       out_specs=out_specs, scratch_shapes=list(scratch))
    else:
        kw.update(grid=grid, in_specs=in_specs, out_specs=out_specs, scratch_shapes=list(scratch))
    call = pl.pallas_call(body, **kw)
    return lambda *args: call(*args, *deps)


def _sds(shape, dtype):
    return jax.ShapeDtypeStruct(tuple(shape), dtype)


def _position():
    x, y, c = lax.axis_index("x"), lax.axis_index("y"), lax.axis_index("c")
    chips = [(1 - x, y), (x, 1 - y), (1 - x, 1 - y)]
    return x, y, c, chips


def _slot(px, py, pc):
    return 4 * px + 2 * py + pc


def _remote(ref_src, ref_dst, send, recv, to):
    return pltpu.make_async_remote_copy(src_ref=ref_src, dst_ref=ref_dst, send_sem=send, recv_sem=recv,
                                        device_id=to, device_id_type=MESH)


def _hbm(arrays):
    return [pltpu.with_memory_space_constraint(a, pltpu.HBM) for a in arrays]


def _split_call(body, name, out_shape, in_specs, out_specs, aliases):
    return pl.pallas_call(body, name=name, out_shape=out_shape, in_specs=in_specs, out_specs=out_specs,
                          input_output_aliases=aliases, compiler_params=pltpu.CompilerParams(has_side_effects=EFFECT))


def _token_shape():
    return _sds((8, 128), F32)


def _gather_start(name, bufs, deps):
    n, nd = len(bufs), len(deps)

    def body(*refs):
        ins, send, recv, token = refs[:n], refs[n + nd], refs[n + nd + 1], refs[2 * n + nd + 2]
        x, y, c, chips = _position()
        peers = [(x, y, 1 - c)] + [(*chip, c) for chip in chips]
        for t in range(n):
            mine = ins[t].at[_slot(x, y, c)]
            for k, peer in enumerate(peers):
                _remote(mine, mine, send.at[4 * t + k], recv.at[4 * t + k], peer).start()
        token[...] = jnp.zeros_like(token)

    res = _split_call(
        body, name,
        (pltpu.SemaphoreType.DMA((4 * n,)), pltpu.SemaphoreType.DMA((4 * n,)), *[pltpu.HBM(b.shape, b.dtype) for b in bufs],
         _token_shape()),
        [HBM] * n + [ANY] * nd, (SEM, SEM, *[HBM] * n, pl.BlockSpec(memory_space=pltpu.VMEM)),
        {t: 2 + t for t in range(n)})(*_hbm(bufs), *deps)
    return res[0], res[1], list(res[2:2 + n]), res[2 + n]


def _gather_forward(name, bufs, send, recv, deps):
    n, nd = len(bufs), len(deps)

    def body(*refs):
        ins, send_in, recv_in = refs[:n], refs[n], refs[n + 1]
        fsend, frecv = refs[n + 2 + nd], refs[n + 3 + nd]
        x, y, c, chips = _position()
        for j, chip in enumerate(chips):
            for t in range(n):
                blk = ins[t].at[_slot(*chip, c)]
                _remote(blk, blk, send_in.at[4 * t + 1 + j], recv_in.at[4 * t + 1 + j], (*chip, c)).wait_recv()
                _remote(blk, blk, fsend.at[3 * t + j], frecv.at[3 * t + j], (x, y, 1 - c)).start()

    res = _split_call(
        body, name,
        (pltpu.SemaphoreType.DMA((3 * n,)), pltpu.SemaphoreType.DMA((3 * n,)), *[pltpu.HBM(b.shape, b.dtype) for b in bufs]),
        [HBM] * n + [SEM, SEM] + [ANY] * nd, (SEM, SEM, *[HBM] * n),
        {t: 2 + t for t in range(n)})(*_hbm(bufs), send, recv, *deps)
    return res[0], res[1], list(res[2:])


def _gather_finish(name, bufs, send, recv, fsend, frecv):
    n = len(bufs)

    def body(*refs):
        ins, send_in, recv_in, fs_in, fr_in = refs[:n], refs[n], refs[n + 1], refs[n + 2], refs[n + 3]
        x, y, c, chips = _position()
        sibling = (x, y, 1 - c)
        peers = [sibling] + [(*chip, c) for chip in chips]
        for t in range(n):
            blk = ins[t].at[_slot(x, y, 1 - c)]
            _remote(blk, blk, send_in.at[4 * t], recv_in.at[4 * t], sibling).wait_recv()
            for j, chip in enumerate(chips):
                blk = ins[t].at[_slot(*chip, 1 - c)]
                _remote(blk, blk, fs_in.at[3 * t + j], fr_in.at[3 * t + j], sibling).wait_recv()
            mine = ins[t].at[_slot(x, y, c)]
            for k, peer in enumerate(peers):
                _remote(mine, mine, send_in.at[4 * t + k], recv_in.at[4 * t + k], peer).wait_send()
            for j, chip in enumerate(chips):
                blk = ins[t].at[_slot(*chip, c)]
                _remote(blk, blk, fs_in.at[3 * t + j], fr_in.at[3 * t + j], sibling).wait_send()

    res = _split_call(
        body, name, tuple(pltpu.HBM(b.shape, b.dtype) for b in bufs),
        [HBM] * n + [SEM] * 4, tuple([HBM] * n), {t: t for t in range(n)})(*_hbm(bufs), send, recv, fsend, frecv)
    return list(res)


def _halves(ref):
    rows = ref.shape[0] // 2
    return ref.at[pl.ds(0, rows)], ref.at[pl.ds(rows, rows)]


def _relay_start(name, bufs, deps):
    n, nd = len(bufs), len(deps)

    def body(*refs):
        ins, send, recv, token = refs[:n], refs[n + nd], refs[n + nd + 1], refs[2 * n + nd + 2]
        x, y, c, _ = _position()
        peers = [(x, y, 1 - c), (1 - x, y, c), (x, 1 - y, c)]
        for t in range(n):
            mine = ins[t].at[_slot(x, y, c)]
            for k, peer in enumerate(peers):
                _remote(mine, mine, send.at[3 * t + k], recv.at[3 * t + k], peer).start()
        token[...] = jnp.zeros_like(token)

    res = _split_call(
        body, name,
        (pltpu.SemaphoreType.DMA((3 * n,)), pltpu.SemaphoreType.DMA((3 * n,)), *[pltpu.HBM(b.shape, b.dtype) for b in bufs],
         _token_shape()),
        [HBM] * n + [ANY] * nd, (SEM, SEM, *[HBM] * n, pl.BlockSpec(memory_space=pltpu.VMEM)),
        {t: 2 + t for t in range(n)})(*_hbm(bufs), *deps)
    return res[0], res[1], list(res[2:2 + n]), res[2 + n]


def _relay_neighbors(name, bufs, send, recv, deps):
    n, nd = len(bufs), len(deps)

    def body(*refs):
        ins, send_in, recv_in = refs[:n], refs[n], refs[n + 1]
        fsend, frecv, token = refs[n + 2 + nd], refs[n + 3 + nd], refs[2 * n + 4 + nd]
        x, y, c, _ = _position()
        sibling, xn, yn = (x, y, 1 - c), (1 - x, y, c), (x, 1 - y, c)
        for t in range(n):
            blk = ins[t].at[_slot(*xn)]
            _remote(blk, blk, send_in.at[3 * t + 1], recv_in.at[3 * t + 1], xn).wait_recv()
            _remote(blk, blk, fsend.at[4 * t], frecv.at[4 * t], sibling).start()
            half = _halves(blk)[0]
            _remote(half, half, fsend.at[4 * t + 1], frecv.at[4 * t + 1], yn).start()
        for t in range(n):
            blk = ins[t].at[_slot(*yn)]
            _remote(blk, blk, send_in.at[3 * t + 2], recv_in.at[3 * t + 2], yn).wait_recv()
            _remote(blk, blk, fsend.at[4 * t + 2], frecv.at[4 * t + 2], sibling).start()
            half = _halves(blk)[1]
            _remote(half, half, fsend.at[4 * t + 3], frecv.at[4 * t + 3], xn).start()
        token[...] = jnp.zeros_like(token)

    res = _split_call(
        body, name,
        (pltpu.SemaphoreType.DMA((4 * n,)), pltpu.SemaphoreType.DMA((4 * n,)), *[pltpu.HBM(b.shape, b.dtype) for b in bufs],
         _token_shape()),
        [HBM] * n + [SEM, SEM] + [ANY] * nd, (SEM, SEM, *[HBM] * n, pl.BlockSpec(memory_space=pltpu.VMEM)),
        {t: 2 + t for t in range(n)})(*_hbm(bufs), send, recv, *deps)
    return res[0], res[1], list(res[2:2 + n]), res[2 + n]


def _relay_diagonal(name, bufs, fsend, frecv, deps):
    n, nd = len(bufs), len(deps)

    def body(*refs):
        ins, fs_in, fr_in = refs[:n], refs[n], refs[n + 1]
        gsend, grecv = refs[n + 2 + nd], refs[n + 3 + nd]
        x, y, c, _ = _position()
        for t in range(n):
            blk = ins[t].at[_slot(1 - x, 1 - y, c)]
            first, second = _halves(blk)
            _remote(first, first, fs_in.at[4 * t + 1], fr_in.at[4 * t + 1], (x, 1 - y, c)).wait_recv()
            _remote(second, second, fs_in.at[4 * t + 3], fr_in.at[4 * t + 3], (1 - x, y, c)).wait_recv()
            _remote(blk, blk, gsend.at[t], grecv.at[t], (x, y, 1 - c)).start()

    res = _split_call(
        body, name,
        (pltpu.SemaphoreType.DMA((n,)), pltpu.SemaphoreType.DMA((n,)), *[pltpu.HBM(b.shape, b.dtype) for b in bufs]),
        [HBM] * n + [SEM, SEM] + [ANY] * nd, (SEM, SEM, *[HBM] * n),
        {t: 2 + t for t in range(n)})(*_hbm(bufs), fsend, frecv, *deps)
    return res[0], res[1], list(res[2:])


def _relay_finish(name, bufs, send, recv, fsend, frecv, gsend, grecv):
    n = len(bufs)

    def body(*refs):
        ins = refs[:n]
        send_in, recv_in, fs_in, fr_in, gs_in, gr_in = refs[n:n + 6]
        x, y, c, _ = _position()
        sibling, xn, yn = (x, y, 1 - c), (1 - x, y, c), (x, 1 - y, c)
        for t in range(n):
            blk = ins[t].at[_slot(x, y, 1 - c)]
            _remote(blk, blk, send_in.at[3 * t], recv_in.at[3 * t], sibling).wait_recv()
            blk = ins[t].at[_slot(1 - x, y, 1 - c)]
            _remote(blk, blk, fs_in.at[4 * t], fr_in.at[4 * t], sibling).wait_recv()
            blk = ins[t].at[_slot(x, 1 - y, 1 - c)]
            _remote(blk, blk, fs_in.at[4 * t + 2], fr_in.at[4 * t + 2], sibling).wait_recv()
            blk = ins[t].at[_slot(1 - x, 1 - y, 1 - c)]
            _remote(blk, blk, gs_in.at[t], gr_in.at[t], sibling).wait_recv()
            mine = ins[t].at[_slot(x, y, c)]
            for k, peer in enumerate([sibling, xn, yn]):
                _remote(mine, mine, send_in.at[3 * t + k], recv_in.at[3 * t + k], peer).wait_send()
            bx, by = ins[t].at[_slot(*xn)], ins[t].at[_slot(*yn)]
            _remote(bx, bx, fs_in.at[4 * t], fr_in.at[4 * t], sibling).wait_send()
            _remote(_halves(bx)[0], _halves(bx)[0], fs_in.at[4 * t + 1], fr_in.at[4 * t + 1], yn).wait_send()
            _remote(by, by, fs_in.at[4 * t + 2], fr_in.at[4 * t + 2], sibling).wait_send()
            _remote(_halves(by)[1], _halves(by)[1], fs_in.at[4 * t + 3], fr_in.at[4 * t + 3], xn).wait_send()
            bd = ins[t].at[_slot(1 - x, 1 - y, c)]
            _remote(bd, bd, gs_in.at[t], gr_in.at[t], sibling).wait_send()

    res = _split_call(
        body, name, tuple(pltpu.HBM(b.shape, b.dtype) for b in bufs),
        [HBM] * n + [SEM] * 6, tuple([HBM] * n), {t: t for t in range(n)})(
            *_hbm(bufs), send, recv, fsend, frecv, gsend, grecv)
    return list(res)


WHOLE, SHARDS, SHARDS2 = 0, 1, 2


def _sibling_copies(srcs, lands, kinds, c):
    pairs = []
    for s_ref, l_ref, kind in zip(srcs, lands, kinds):
        if kind == WHOLE:
            pairs.append((s_ref, l_ref))
        elif kind == SHARDS:
            pairs += [(s_ref.at[2 * k + (1 - c)], l_ref.at[k]) for k in range(NCHIP)]
        else:
            pairs += [(s_ref.at[w, 2 * k + (1 - c)], l_ref.at[w, k]) for w in range(2) for k in range(NCHIP)]
    return pairs


def _count_copies(kinds):
    return sum({WHOLE: 1, SHARDS: NCHIP, SHARDS2: 2 * NCHIP}[k] for k in kinds)


def _sibling_start(name, srcs, lands, deps, kinds=None):
    n, nd = len(srcs), len(deps)
    kinds = kinds or [SHARDS] * n
    ncp = _count_copies(kinds)

    def body(*refs):
        s_in, l_in = refs[:n], refs[n:2 * n]
        send, recv, token = refs[2 * n + nd], refs[2 * n + nd + 1], refs[4 * n + nd + 2]
        x, y, c, _ = _position()
        for i, (src, dst) in enumerate(_sibling_copies(s_in, l_in, kinds, c)):
            _remote(src, dst, send.at[i], recv.at[i], (x, y, 1 - c)).start()
        token[...] = jnp.zeros_like(token)

    both = list(srcs) + list(lands)
    res = _split_call(
        body, name,
        (pltpu.SemaphoreType.DMA((ncp,)), pltpu.SemaphoreType.DMA((ncp,)),
         *[pltpu.HBM(b.shape, b.dtype) for b in both], _token_shape()),
        [HBM] * (2 * n) + [ANY] * nd, (SEM, SEM, *[HBM] * (2 * n), pl.BlockSpec(memory_space=pltpu.VMEM)),
        {t: 2 + t for t in range(2 * n)})(*_hbm(both), *deps)
    return res[0], res[1], list(res[2:2 + n]), list(res[2 + n:2 + 2 * n]), res[2 + 2 * n]


def _sibling_finish(name, srcs, lands, send, recv, deps, kinds=None):
    n, nd = len(srcs), len(deps)
    kinds = kinds or [SHARDS] * n

    def body(*refs):
        s_in, l_in, send_in, recv_in = refs[:n], refs[n:2 * n], refs[2 * n], refs[2 * n + 1]
        x, y, c, _ = _position()
        for i, (src, dst) in enumerate(_sibling_copies(s_in, l_in, kinds, c)):
            cp = _remote(src, dst, send_in.at[i], recv_in.at[i], (x, y, 1 - c))
            cp.wait_send()
            cp.wait_recv()

    both = list(srcs) + list(lands)
    res = _split_call(
        body, name, tuple(pltpu.HBM(b.shape, b.dtype) for b in both),
        [HBM] * (2 * n) + [SEM, SEM] + [ANY] * nd, tuple([HBM] * (2 * n)),
        {t: t for t in range(2 * n)})(*_hbm(both), send, recv, *deps)
    return list(res[:n]), list(res[n:])


def _chips_start(name, parts, lands, deps):
    n, nd = len(parts), len(deps)

    def body(*refs):
        p_in, l_in = refs[:n], refs[n:2 * n]
        send, recv, token = refs[2 * n + nd], refs[2 * n + nd + 1], refs[4 * n + nd + 2]
        x, y, c, chips = _position()
        for t in range(n):
            for j, chip in enumerate(chips):
                _remote(p_in[t].at[2 * chip[0] + chip[1]], l_in[t].at[2 * x + y], send.at[3 * t + j], recv.at[3 * t + j],
                        (*chip, c)).start()
        token[...] = jnp.zeros_like(token)

    both = list(parts) + list(lands)
    res = _split_call(
        body, name,
        (pltpu.SemaphoreType.DMA((3 * n,)), pltpu.SemaphoreType.DMA((3 * n,)), *[pltpu.HBM(b.shape, b.dtype) for b in both],
         _token_shape()),
        [HBM] * (2 * n) + [ANY] * nd, (SEM, SEM, *[HBM] * (2 * n), pl.BlockSpec(memory_space=pltpu.VMEM)),
        {t: 2 + t for t in range(2 * n)})(*_hbm(both), *deps)
    return res[0], res[1], list(res[2:2 + n]), list(res[2 + n:2 + 2 * n]), res[2 + 2 * n]


def _chips_finish(name, parts, lands, send, recv, deps):
    n, nd = len(parts), len(deps)

    def body(*refs):
        p_in, l_in, send_in, recv_in = refs[:n], refs[n:2 * n], refs[2 * n], refs[2 * n + 1]
        x, y, c, chips = _position()
        for t in range(n):
            for j, chip in enumerate(chips):
                k = 2 * chip[0] + chip[1]
                _remote(p_in[t].at[k], l_in[t].at[k], send_in.at[3 * t + j], recv_in.at[3 * t + j], (*chip, c)).wait_recv()
                _remote(p_in[t].at[k], l_in[t].at[2 * x + y], send_in.at[3 * t + j], recv_in.at[3 * t + j],
                        (*chip, c)).wait_send()

    both = list(parts) + list(lands)
    res = _split_call(
        body, name, tuple(pltpu.HBM(b.shape, b.dtype) for b in both),
        [HBM] * (2 * n) + [SEM, SEM] + [ANY] * nd, tuple([HBM] * (2 * n)),
        {t: t for t in range(2 * n)})(*_hbm(both), send, recv, *deps)
    return list(res[:n]), list(res[n:])


def _pair_sum(name, grad, recv, where):
    _, r, w = grad.shape
    tr = _row_tile(r, w, budget=4 * 1024 * 1024)
    g4 = grad.reshape(NCHIP, 2, r, w)

    def body(where_ref, g_ref, r_ref, o_ref, own_ref):
        val = (g_ref[...].astype(F32) + r_ref[...].astype(F32)).astype(o_ref.dtype)
        o_ref[...] = val

        @pl.when(pl.program_id(1) == where_ref[1])
        def _():
            own_ref[...] = val

    out = _sds((NCHIP, r, w), grad.dtype)
    return _pcall(
        body, name=name, out_shape=[out, out], grid=(r // tr, NCHIP), prefetch=1,
        in_specs=[pl.BlockSpec((None, None, tr, w), lambda i, k, wr: (k, wr[0], i, 0)),
                  pl.BlockSpec((None, tr, w), lambda i, k, wr: (k, i, 0))],
        out_specs=[pl.BlockSpec((None, tr, w), lambda i, k, wr: (k, i, 0)),
                   pl.BlockSpec((None, tr, w), lambda i, k, wr: (wr[1], i, 0))],
    )(where, g4, recv)


def _row_tile(rows, width, budget=2 * 1024 * 1024):
    best = None
    for t in range(16, rows + 1, 16):
        if rows % t == 0 and t * width * 4 <= budget:
            best = t
    if best is None and rows * width * 4 <= budget:
        best = rows
    assert best is not None, (rows, width)
    return best


def _gemm(name, grid, operands, prods, extras, outs, epilogue, *, nk=1, acc_shape=None, aliases=None, separate=False,
          deps=(), prefetch=()):
    n_op, n_ex, n_out = len(operands), len(extras), len(outs)

    def body(*refs):
        refs = refs[len(prefetch):]
        ops, ex, out_refs = refs[:n_op], refs[n_op:n_op + n_ex], refs[n_op + n_ex:n_op + n_ex + n_out]
        parts = []
        for pr in prods:
            a, b = ops[pr[0]], ops[pr[1]]
            av = pr[3](a) if len(pr) > 3 and pr[3] else a[...]
            bv = pr[4](b) if len(pr) > 4 and pr[4] else b[...]
            parts.append(lax.dot_general(av, bv, pr[2], preferred_element_type=F32))
        if separate:
            epilogue(parts, ex, out_refs)
            return
        part = parts[0]
        for p in parts[1:]:
            part = part + p
        if nk == 1:
            epilogue(part, ex, out_refs)
        else:
            acc = refs[-1]
            k = pl.program_id(len(grid) - 1)

            @pl.when(k == 0)
            def _():
                acc[...] = part

            @pl.when(k > 0)
            def _():
                acc[...] += part

            @pl.when(k == nk - 1)
            def _():
                epilogue(acc[...], ex, out_refs)

    res = _pcall(
        body, name=name, out_shape=[o[0] for o in outs], grid=grid,
        in_specs=[o[1] for o in operands] + [e[1] for e in extras], out_specs=[o[1] for o in outs],
        scratch=[pltpu.VMEM(acc_shape, F32)] if nk > 1 else [], aliases=aliases, deps=deps, prefetch=len(prefetch),
    )(*prefetch, *[o[0] for o in operands], *[e[0] for e in extras])
    return list(res)


def _store(acc, ex, outs):
    outs[0][...] = acc.astype(outs[0].dtype)


def _store_add_extra(acc, ex, outs):
    v = acc
    for e in ex:
        v = v + e[...]
    outs[0][...] = v.astype(outs[0].dtype)


def _stacked(ref):
    b = ref[...]
    return b.reshape(b.shape[0] * b.shape[1], b.shape[2])


def _pick(c):
    return lambda ref: ref[c]


def _cols(c, width):
    return lambda ref: ref[:, c * width:(c + 1) * width]


def _grad_cols(name, act, dy, buf, ts):
    s, k = act.shape
    nd, _, n = buf.shape

    def body(a_ref, dy_ref, b_ref, o_ref, at_ref):
        @pl.when(pl.program_id(1) == 0)
        def _():
            at_ref[...] = a_ref[...].T

        o_ref[...] = jnp.dot(at_ref[...], dy_ref[...], preferred_element_type=F32).astype(o_ref.dtype)

    return _pcall(
        body, name=name, out_shape=_sds(buf.shape, buf.dtype), grid=(k // ts, nd),
        in_specs=[pl.BlockSpec((s, ts), lambda i, e: (0, i)), pl.BlockSpec((s, n), lambda i, e: (0, e)), ANY],
        out_specs=pl.BlockSpec((None, ts, n), lambda i, e: (e, i, 0)), aliases={2: 0},
        scratch=[pltpu.VMEM((ts, s), act.dtype)],
    )(act, dy, buf)


def _gelu_parts(z):
    c = math.sqrt(2.0 / math.pi)
    t = jnp.tanh(c * (z + 0.044715 * (z * z * z)))
    val = 0.5 * z * (1.0 + t)
    grad = 0.5 * (1.0 + t) + 0.5 * z * (1.0 - t * t) * (c * (1.0 + 3.0 * 0.044715 * z * z))
    return val, grad


def _rms_fwd(name, h, g, deps=()):
    s, d = h.shape
    tr = _row_tile(s, d)
    n, nbuf = s // tr, 3

    def body(h_hbm, g_ref, o_hbm, ibuf, obuf, isem, osem):
        def read(i):
            return pltpu.make_async_copy(h_hbm.at[pl.ds(i * tr, tr)], ibuf.at[i % nbuf], isem.at[i % nbuf])

        def write(i):
            return pltpu.make_async_copy(obuf.at[i % 2], o_hbm.at[pl.ds(i * tr, tr)], osem.at[i % 2])

        for i in range(min(nbuf, n)):
            read(i).start()
        for i in range(n):
            read(i).wait()
            hv = ibuf[i % nbuf]
            r = lax.rsqrt(jnp.mean(hv * hv, axis=-1, keepdims=True) + RMS_EPS)
            if i >= 2:
                write(i - 2).wait()
            obuf[i % 2] = (hv * r * g_ref[...]).astype(obuf.dtype)
            write(i).start()
            if i + nbuf < n:
                read(i + nbuf).start()
        for i in range(max(n - 2, 0), n):
            write(i).wait()

    return _pcall(
        body, name=name, out_shape=_sds((s, d), BF16),
        in_specs=[ANY, pl.BlockSpec(memory_space=pltpu.VMEM)], out_specs=ANY,
        scratch=[pltpu.VMEM((nbuf, tr, d), F32), pltpu.VMEM((2, tr, d), BF16), pltpu.SemaphoreType.DMA((nbuf,)),
                 pltpu.SemaphoreType.DMA((2,))], deps=deps,
    )(h, g.reshape(1, d))


def _accumulate(ref, val, first):
    @pl.when(first)
    def _():
        ref[...] = val

    @pl.when(jnp.logical_not(first))
    def _():
        ref[...] += val


def _rms_bwd(name, h, g, dy, res, deps=()):
    s, d = h.shape
    tr = _row_tile(s, d, budget=2 * 1024 * 1024)

    def body(h_ref, g_ref, dy_ref, res_ref, dh_ref, dhb_ref, dg_ref, cs_ref):
        hv = h_ref[...]
        r = lax.rsqrt(jnp.mean(hv * hv, axis=-1, keepdims=True) + RMS_EPS)
        xhat = hv * r
        dyv = dy_ref[...]
        dxh = dyv * g_ref[...]
        dh = res_ref[...] + r * (dxh - xhat * jnp.mean(dxh * xhat, axis=-1, keepdims=True))
        dh_ref[...] = dh
        dhb_ref[...] = dh.astype(BF16)
        first = pl.program_id(0) == 0
        _accumulate(dg_ref, jnp.sum(dyv * xhat, axis=0, keepdims=True), first)
        _accumulate(cs_ref, jnp.sum(dh, axis=0, keepdims=True), first)

    row = pl.BlockSpec((tr, d), lambda i: (i, 0))
    vec = pl.BlockSpec((1, d), lambda i: (0, 0))
    return _pcall(
        body, name=name, out_shape=[_sds((s, d), F32), _sds((s, d), BF16), _sds((1, d), F32), _sds((1, d), F32)],
        grid=(s // tr,), in_specs=[row, vec, row, row], out_specs=[row, row, vec, vec], deps=deps,
    )(h, g.reshape(1, d), dy, res)


def _loss_bwd(name, h, g, target):
    s, d = h.shape
    tr = _row_tile(s, d, budget=1024 * 1024)

    def body(h_ref, g_ref, t_ref, loss_ref, dh_ref, dhb_ref, dg_ref):
        hv = h_ref[...]
        r = lax.rsqrt(jnp.mean(hv * hv, axis=-1, keepdims=True) + RMS_EPS)
        xhat = hv * r
        diff = xhat * g_ref[...] - t_ref[...]
        part = jnp.sum(jnp.sum(diff * diff, axis=1, keepdims=True), axis=0, keepdims=True) * (0.5 / d)
        dyv = diff * (1.0 / d)
        dxh = dyv * g_ref[...]
        dh = r * (dxh - xhat * jnp.mean(dxh * xhat, axis=-1, keepdims=True))
        dh_ref[...] = dh
        dhb_ref[...] = dh.astype(BF16)
        first = pl.program_id(0) == 0
        _accumulate(loss_ref, part, first)
        _accumulate(dg_ref, jnp.sum(dyv * xhat, axis=0, keepdims=True), first)

    row = pl.BlockSpec((tr, d), lambda i: (i, 0))
    vec = pl.BlockSpec((1, d), lambda i: (0, 0))
    one = pl.BlockSpec((1, 1), lambda i: (0, 0))
    return _pcall(
        body, name=name, out_shape=[_sds((1, 1), F32), _sds((s, d), F32), _sds((s, d), BF16), _sds((1, d), F32)],
        grid=(s // tr,), in_specs=[row, vec, row], out_specs=[one, row, row, vec],
    )(h, g.reshape(1, d), target)


def _tril_mask():
    return lax.broadcasted_iota(jnp.int32, (CHUNK, CHUNK), 0) >= lax.broadcasted_iota(jnp.int32, (CHUNK, CHUNK), 1)


def _gmlp_fwd(name, zp, gv, ws, bst):
    s, d2 = zp.shape
    d = d2 // 2
    gw = d // GROUPS

    def body(zp_ref, gv_ref, ws_ref, bst_ref, o_ref):
        u, _ = _gelu_parts(zp_ref[:, :d])
        v, _ = _gelu_parts(zp_ref[:, d:])
        rv = lax.rsqrt(jnp.mean(v * v, axis=-1, keepdims=True) + RMS_EPS)
        vn = (v * rv * gv_ref[...]).astype(BF16)
        tril = _tril_mask()
        for g in range(GROUPS):
            sl = slice(g * gw, (g + 1) * gw)
            wc = jnp.where(tril, ws_ref[g], 0.0).astype(BF16)
            sg = jnp.dot(wc, vn[:, sl], preferred_element_type=F32) + bst_ref[:, g:g + 1]
            o_ref[:, sl] = (u[:, sl] * sg).astype(o_ref.dtype)

    return _pcall(
        body, name=name, out_shape=_sds((s, d), BF16), grid=(s // CHUNK,),
        in_specs=[pl.BlockSpec((CHUNK, d2), lambda i: (i, 0)), pl.BlockSpec((1, d), lambda i: (0, 0)),
                  pl.BlockSpec((GROUPS, CHUNK, CHUNK), lambda i: (0, 0, 0)),
                  pl.BlockSpec((CHUNK, GROUPS), lambda i: (0, 0))],
        out_specs=pl.BlockSpec((CHUNK, d), lambda i: (i, 0)),
    )(zp, gv, ws, bst)


def _gmlp_bwd(name, zp, dgated, gv, ws, bst):
    s, d2 = zp.shape
    d = d2 // 2
    gw = d // GROUPS

    def body(zp_ref, dg_ref, gv_ref, ws_ref, bst_ref, dzp_ref, dws_ref, dbs_ref, dgv_ref, dvn_ref):
        u, gu = _gelu_parts(zp_ref[:, :d])
        v, gvv = _gelu_parts(zp_ref[:, d:])
        rv = lax.rsqrt(jnp.mean(v * v, axis=-1, keepdims=True) + RMS_EPS)
        vhat = v * rv
        vn = (vhat * gv_ref[...]).astype(BF16)
        tril = _tril_mask()
        first = pl.program_id(0) == 0
        ones = jnp.ones((8, gw), F32)

        @pl.when(first)
        def _():
            dws_ref[...] = jnp.zeros_like(dws_ref)
            dbs_ref[...] = jnp.zeros_like(dbs_ref)

        for g in range(GROUPS):
            sl = slice(g * gw, (g + 1) * gw)
            wc = jnp.where(tril, ws_ref[g], 0.0).astype(BF16)
            sg = jnp.dot(wc, vn[:, sl], preferred_element_type=F32) + bst_ref[:, g:g + 1]
            dgs = dg_ref[:, sl]
            ds = dgs * u[:, sl]
            dsb = ds.astype(BF16)
            dzp_ref[:, sl] = (dgs * sg * gu[:, sl]).astype(dzp_ref.dtype)
            dvn_ref[:, sl] = lax.dot_general(wc, dsb, TN, preferred_element_type=F32)
            dw = lax.dot_general(dsb, vn[:, sl], NT, preferred_element_type=F32)
            dws_ref[g] += jnp.where(tril, dw, 0.0)
            dbs_ref[g] += lax.dot_general(ones, ds, NT, preferred_element_type=F32, precision=lax.Precision.HIGHEST)
        dvn = dvn_ref[...]
        dvh = dvn * gv_ref[...]
        dv = rv * (dvh - vhat * jnp.mean(dvh * vhat, axis=-1, keepdims=True))
        dzp_ref[:, d:] = (dv * gvv).astype(dzp_ref.dtype)
        _accumulate(dgv_ref, jnp.sum(dvn * vhat, axis=0, keepdims=True), first)

    return _pcall(
        body, name=name,
        out_shape=[_sds((s, d2), BF16), _sds((GROUPS, CHUNK, CHUNK), F32), _sds((GROUPS, 8, CHUNK), F32),
                   _sds((1, d), F32)],
        grid=(s // CHUNK,),
        in_specs=[pl.BlockSpec((CHUNK, d2), lambda i: (i, 0)), pl.BlockSpec((CHUNK, d), lambda i: (i, 0)),
                  pl.BlockSpec((1, d), lambda i: (0, 0)), pl.BlockSpec((GROUPS, CHUNK, CHUNK), lambda i: (0, 0, 0)),
                  pl.BlockSpec((CHUNK, GROUPS), lambda i: (0, 0))],
        out_specs=[pl.BlockSpec((CHUNK, d2), lambda i: (i, 0)),
                   pl.BlockSpec((GROUPS, CHUNK, CHUNK), lambda i: (0, 0, 0)),
                   pl.BlockSpec((GROUPS, 8, CHUNK), lambda i: (0, 0, 0)), pl.BlockSpec((1, d), lambda i: (0, 0))],
        scratch=[pltpu.VMEM((CHUNK, d), F32)],
    )(zp, dgated, gv, ws, bst)


def _bucket_table():
    dist = np.arange(BLOCK)[:, None] + BLOCK - np.arange(2 * BLOCK)[None, :]
    in_window = (dist >= 0) & (dist < BLOCK)
    dd = np.clip(dist, 0, None)
    max_exact = N_BUCKETS // 2
    dl = np.maximum(dd, 1).astype(np.float32)
    large = max_exact + (np.log(dl / np.float32(max_exact)) / np.float32(math.log(MAX_DISTANCE / max_exact))
                         * np.float32(N_BUCKETS - max_exact)).astype(np.int32)
    large = np.minimum(large, N_BUCKETS - 1)
    bucket = np.where(dd < max_exact, dd, large)
    return np.where(in_window, bucket, -1).astype(np.int32).reshape(1, -1)


def _bias_table(name, rel_bias_t, buckets):
    nh = rel_bias_t.shape[0]
    p = buckets.shape[1]
    tp = 4096

    def body(rb_ref, bk_ref, o_ref):
        bk = bk_ref[...]
        onehot = (lax.broadcasted_iota(jnp.int32, (N_BUCKETS, tp), 0) == bk).astype(F32)
        val = jnp.dot(rb_ref[...], onehot, preferred_element_type=F32, precision=lax.Precision.HIGHEST)
        o_ref[...] = jnp.where(bk >= 0, val, NEG_INF)

    return _pcall(
        body, name=name, out_shape=_sds((nh, p), F32), grid=(p // tp,),
        in_specs=[pl.BlockSpec((nh, N_BUCKETS), lambda i: (0, 0)), pl.BlockSpec((1, tp), lambda i: (0, i))],
        out_specs=pl.BlockSpec((nh, tp), lambda i: (0, i)),
    )(rel_bias_t, buckets)


def _bias_grad(name, dbiases, buckets):
    nh, p = dbiases[0].shape
    n = len(dbiases)
    tp = 4096

    def body(*refs):
        bk_ref, o_ref = refs[n], refs[n + 1]
        onehot = (lax.broadcasted_iota(jnp.int32, (N_BUCKETS, tp), 0) == bk_ref[...]).astype(F32)
        db = refs[0][...]
        for r in refs[1:n]:
            db = db + r[...]
        part = lax.dot_general(onehot, db, NT, preferred_element_type=F32, precision=lax.Precision.HIGHEST)
        _accumulate(o_ref, part, pl.program_id(0) == 0)

    return _pcall(
        body, name=name, out_shape=_sds((N_BUCKETS, nh), F32), grid=(p // tp,),
        in_specs=[pl.BlockSpec((nh, tp), lambda i: (0, i))] * n + [pl.BlockSpec((1, tp), lambda i: (0, i))],
        out_specs=pl.BlockSpec((N_BUCKETS, nh), lambda i: (0, 0)),
    )(*dbiases, buckets)


def _stack_heads(ref, g):
    base = g * KV_GROUP * HEAD_DIM
    return jnp.concatenate([ref[:, base + hh * HEAD_DIM:base + (hh + 1) * HEAD_DIM] for hh in range(KV_GROUP)], axis=0)


def _attn_probs(q, kb, bias, s_ref, first_head):
    penalty = jnp.where(pl.program_id(1) > 0, 0.0, NEG_INF).astype(F32)
    col = lax.broadcasted_iota(jnp.int32, (1, 2 * BLOCK), 1)
    bias = bias.reshape(KV_GROUP * BLOCK, 2 * BLOCK) + jnp.where(col < BLOCK, penalty, 0.0)
    sink = jnp.concatenate([jnp.full((BLOCK, 1), s_ref[first_head + hh], F32) for hh in range(KV_GROUP)], axis=0)
    s = lax.dot_general(q, kb, NT, preferred_element_type=F32) * 0.125 + bias
    m = jnp.maximum(jnp.max(s, axis=-1, keepdims=True), sink)
    p = jnp.exp(s - m)
    es = jnp.exp(sink - m)
    inv = 1.0 / (jnp.sum(p, axis=-1, keepdims=True) + es)
    return p * inv, es * inv


def _attn_specs(ng):
    gq = ng * KV_GROUP * HEAD_DIM
    q_spec = pl.BlockSpec((BLOCK, gq), lambda kh, i: (i, kh))
    prev = pl.BlockSpec((ng, BLOCK, HEAD_DIM), lambda kh, i: (kh, jnp.maximum(i - 1, 0), 0))
    cur = pl.BlockSpec((ng, BLOCK, HEAD_DIM), lambda kh, i: (kh, i, 0))
    bias = pl.BlockSpec((ng * KV_GROUP, BLOCK, 2 * BLOCK), lambda kh, i: (kh, 0, 0))
    smem = pl.BlockSpec(memory_space=pltpu.SMEM)
    probs = pl.BlockSpec((ng, None, KV_GROUP * BLOCK, 2 * BLOCK), lambda kh, i: (kh, i, 0, 0))
    sink_probs = pl.BlockSpec((ng, None, KV_GROUP * BLOCK, 1), lambda kh, i: (kh, i, 0, 0))
    return q_spec, prev, cur, bias, smem, probs, sink_probs


def _kv_heads_per_step(nkv):
    return 2 if nkv % 2 == 0 else 1


def _attn_fwd(name, q, k, v, bias, sinks, deps=()):
    s, dq = q.shape
    nkv = k.shape[0]
    ng = 1
    q_spec, prev, cur, bias_spec, smem, p_spec, ps_spec = _attn_specs(ng)

    def body(q_ref, kp_ref, kc_ref, vp_ref, vc_ref, b_ref, s_ref, o_ref, p_ref, ps_ref):
        for g in range(ng):
            kb = jnp.concatenate([kp_ref[g], kc_ref[g]], axis=0)
            vb = jnp.concatenate([vp_ref[g], vc_ref[g]], axis=0)
            p, ps = _attn_probs(_stack_heads(q_ref, g), kb, b_ref[g * KV_GROUP:(g + 1) * KV_GROUP], s_ref,
                                (pl.program_id(0) * ng + g) * KV_GROUP)
            pb = p.astype(BF16)
            p_ref[g] = pb
            ps_ref[g] = ps
            o = jnp.dot(pb, vb, preferred_element_type=F32)
            for hh in range(KV_GROUP):
                col = (g * KV_GROUP + hh) * HEAD_DIM
                o_ref[:, col:col + HEAD_DIM] = o[hh * BLOCK:(hh + 1) * BLOCK].astype(o_ref.dtype)

    return _pcall(
        body, name=name,
        out_shape=[_sds((s, dq), BF16), _sds((nkv, s // BLOCK, KV_GROUP * BLOCK, 2 * BLOCK), BF16),
                   _sds((nkv, s // BLOCK, KV_GROUP * BLOCK, 1), F32)],
        grid=(nkv // ng, s // BLOCK),
        in_specs=[q_spec, prev, cur, prev, cur, bias_spec, smem], out_specs=[q_spec, p_spec, ps_spec], deps=deps,
    )(q, k, k, v, v, bias, sinks)


def _attn_bwd(name, q, k, v, do, probs, sink_probs):
    s, dq = q.shape
    nkv = k.shape[0]
    ng = _kv_heads_per_step(nkv)
    gq = ng * KV_GROUP * HEAD_DIM
    q_spec, prev, cur, bias_spec, _, p_spec, ps_spec = _attn_specs(ng)

    def body(q_ref, do_ref, kp_ref, kc_ref, vp_ref, vc_ref, p_ref, ps_ref,
             dq_ref, dbq_ref, dkc_ref, dkp_ref, dvc_ref, dvp_ref, dbias_ref, dsink_ref):
        @pl.when(pl.program_id(1) == 0)
        def _():
            dbias_ref[...] = jnp.zeros_like(dbias_ref)
            dsink_ref[...] = jnp.zeros_like(dsink_ref)
            dbq_ref[...] = jnp.zeros_like(dbq_ref)

        for g in range(ng):
            kb = jnp.concatenate([kp_ref[g], kc_ref[g]], axis=0)
            vb = jnp.concatenate([vp_ref[g], vc_ref[g]], axis=0)
            q, do = _stack_heads(q_ref, g), _stack_heads(do_ref, g)
            pb = p_ref[g]
            p = pb.astype(F32)
            dp = lax.dot_general(do, vb, NT, preferred_element_type=F32)
            delta = jnp.sum(p * dp, axis=-1, keepdims=True)
            ds = p * (dp - delta)
            dsb = ds.astype(BF16)
            dq = jnp.dot(dsb, kb, preferred_element_type=F32) * 0.125
            dsk = -(ps_ref[g] * delta)
            for hh in range(KV_GROUP):
                col, rows = (g * KV_GROUP + hh) * HEAD_DIM, slice(hh * BLOCK, (hh + 1) * BLOCK)
                dq_ref[:, col:col + HEAD_DIM] = dq[rows].astype(dq_ref.dtype)
                dbq_ref[:, col:col + HEAD_DIM] += jnp.sum(dq[rows], axis=0, keepdims=True)
                dsink_ref[g, :, hh:hh + 1] += jnp.sum(dsk[rows], axis=0, keepdims=True)
            dkb = lax.dot_general(dsb, q, TN, preferred_element_type=F32) * 0.125
            dvb = lax.dot_general(pb, do, TN, preferred_element_type=F32)
            dkp_ref[g], dkc_ref[g] = dkb[:BLOCK], dkb[BLOCK:]
            dvp_ref[g], dvc_ref[g] = dvb[:BLOCK], dvb[BLOCK:]
            dbias_ref[g * KV_GROUP:(g + 1) * KV_GROUP] += ds.reshape(KV_GROUP, BLOCK, 2 * BLOCK)

    kv_out = _sds((nkv, s, HEAD_DIM), F32)
    return _pcall(
        body, name=name,
        out_shape=[_sds((s, dq), BF16), _sds((1, dq), F32), kv_out, kv_out, kv_out, kv_out,
                   _sds((nkv * KV_GROUP, BLOCK, 2 * BLOCK), F32), _sds((nkv, 1, KV_GROUP), F32)],
        grid=(nkv // ng, s // BLOCK),
        in_specs=[q_spec, q_spec, prev, cur, prev, cur, p_spec, ps_spec],
        out_specs=[q_spec, pl.BlockSpec((1, gq), lambda kh, i: (0, kh)), cur, cur, cur, cur, bias_spec,
                   pl.BlockSpec((ng, 1, KV_GROUP), lambda kh, i: (kh, 0, 0))],
    )(q, do, k, k, v, v, probs, sink_probs)


def _kv_grad(name, parts):
    nkv, s, _ = parts[0][0].shape
    nb = s // BLOCK
    w = 2 * nkv * HEAD_DIM
    n = len(parts)

    def body(*refs):
        o_ref, cs_ref = refs[4 * n], refs[4 * n + 1]
        i = pl.program_id(0)
        keep = jnp.where(i < nb - 1, 1.0, 0.0).astype(F32)

        @pl.when(i == 0)
        def _():
            cs_ref[...] = jnp.zeros_like(cs_ref)

        for which in range(2):
            for hh in range(nkv):
                val = None
                for l in range(n):
                    cur_ref, nxt_ref = refs[4 * l + 2 * which], refs[4 * l + 2 * which + 1]
                    t = cur_ref[hh] + keep * nxt_ref[hh]
                    val = t if val is None else val + t
                sl = slice((which * nkv + hh) * HEAD_DIM, (which * nkv + hh + 1) * HEAD_DIM)
                o_ref[:, sl] = val.astype(o_ref.dtype)
                cs_ref[:, sl] += jnp.sum(val, axis=0, keepdims=True)

    cur = pl.BlockSpec((nkv, BLOCK, HEAD_DIM), lambda i: (0, i, 0))
    nxt = pl.BlockSpec((nkv, BLOCK, HEAD_DIM), lambda i: (0, jnp.minimum(i + 1, nb - 1), 0))
    flat = [a for p in parts for a in p]
    return _pcall(
        body, name=name, out_shape=[_sds((s, w), BF16), _sds((1, w), F32)], grid=(nb,),
        in_specs=[cur, nxt] * (2 * n),
        out_specs=[pl.BlockSpec((BLOCK, w), lambda i: (i, 0)), pl.BlockSpec((1, w), lambda i: (0, 0))],
    )(*flat)


def _adamw_math(w, g, m, v):
    m = ADAM_B1 * m + (1.0 - ADAM_B1) * g
    v = ADAM_B2 * v + (1.0 - ADAM_B2) * (g * g)
    m_hat = m / (1.0 - ADAM_B1 ** ADAM_STEP)
    v_hat = v / (1.0 - ADAM_B2 ** ADAM_STEP)
    delta = -ADAM_LR * (m_hat / (jnp.sqrt(v_hat) + ADAM_EPS) + ADAM_WD * w)
    return delta, m, v


def _adamw_shard(name, w, m, v, parts, row0, layer, prev, deps=(), own=None, where=None):
    _, r, wd = w.shape
    tr = _row_tile(r, wd, budget=3 * 512 * 1024)
    assert row0 % tr == 0

    def step(w_ref, m_ref, v_ref, g, g_ref, d_ref, nm_ref, nv_ref):
        delta, nm, nv = _adamw_math(w_ref[...], g, m_ref[...], v_ref[...])
        g_ref[...], d_ref[...], nm_ref[...], nv_ref[...] = g, delta, nm, nv

    out = _sds(w.shape, F32)
    if own is None:
        def body(w_ref, m_ref, v_ref, p_ref, a0, a1, a2, a3, g_ref, d_ref, nm_ref, nv_ref):
            g = p_ref[0].astype(F32)
            for k in range(1, NCHIP):
                g = g + p_ref[k].astype(F32)
            step(w_ref, m_ref, v_ref, g, g_ref, d_ref, nm_ref, nv_ref)

        par = pl.BlockSpec((None, tr, wd), lambda i: (layer, i, 0))
        return _pcall(
            body, name=name, out_shape=[out, out, out, out], grid=(r // tr,),
            in_specs=[par, par, par, pl.BlockSpec((NCHIP, tr, wd), lambda i: (0, row0 // tr + i, 0)), ANY, ANY, ANY, ANY],
            out_specs=[par, par, par, par], aliases={4: 0, 5: 1, 6: 2, 7: 3}, deps=deps,
        )(w, m, v, parts, *prev)

    def body(where_ref, w_ref, m_ref, v_ref, p_ref, o_ref, a0, a1, a2, a3, g_ref, d_ref, nm_ref, nv_ref):
        mine = lax.broadcasted_iota(jnp.int32, (tr, wd), 0) * 0 + where_ref[1]
        g = None
        for k in range(NCHIP):
            t = jnp.where(mine == k, o_ref[...], p_ref[k]).astype(F32)
            g = t if g is None else g + t
        step(w_ref, m_ref, v_ref, g, g_ref, d_ref, nm_ref, nv_ref)

    par = pl.BlockSpec((None, tr, wd), lambda i, wr: (layer, i, 0))
    return _pcall(
        body, name=name, out_shape=[out, out, out, out], grid=(r // tr,), prefetch=1,
        in_specs=[par, par, par, pl.BlockSpec((NCHIP, tr, wd), lambda i, wr: (0, row0 // tr + i, 0)),
                  pl.BlockSpec((None, tr, wd), lambda i, wr: (wr[1], row0 // tr + i, 0)), ANY, ANY, ANY, ANY],
        out_specs=[par, par, par, par], aliases={6: 0, 7: 1, 8: 2, 9: 3}, deps=deps,
    )(where, w, m, v, parts, own, *prev)


def _sum_devices(name, gathered):
    _, r, wd = gathered.shape

    def body(g_ref, o_ref):
        acc = g_ref[0]
        for k in range(1, NDEV):
            acc = acc + g_ref[k]
        o_ref[...] = acc

    return _pcall(body, name=name, out_shape=_sds((r, wd), F32), grid=(1,),
                  in_specs=[pl.BlockSpec((NDEV, r, wd), lambda i: (0, 0, 0))],
                  out_specs=pl.BlockSpec((r, wd), lambda i: (0, 0)))(gathered)


def _adamw_flat(name, w, g, m, v):
    shape = w.shape

    def body(w_ref, g_ref, m_ref, v_ref, d_ref, nm_ref, nv_ref):
        d_ref[...], nm_ref[...], nv_ref[...] = _adamw_math(w_ref[...], g_ref[...], m_ref[...], v_ref[...])

    spec = pl.BlockSpec(shape, lambda i: (0, 0))
    out = _sds(shape, F32)
    return _pcall(body, name=name, out_shape=[out, out, out], grid=(1,), in_specs=[spec] * 4,
                  out_specs=[spec] * 3)(w, g, m, v)


def _cast_into(name, src, layer, buf, row0, me):
    _, r, wd = src.shape
    tr = _row_tile(r, wd)
    assert row0 % tr == 0

    def body(me_ref, s_ref, b_ref, o_ref):
        o_ref[...] = s_ref[...].astype(o_ref.dtype)

    return _pcall(
        body, name=name, out_shape=_sds(buf.shape, buf.dtype), grid=(r // tr,), prefetch=1,
        in_specs=[pl.BlockSpec((None, tr, wd), lambda i, mr: (layer, i, 0)), ANY],
        out_specs=pl.BlockSpec((None, tr, wd), lambda i, mr: (mr[0], row0 // tr + i, 0)), aliases={2: 0},
    )(me, src, buf)


def _pack(arrays):
    rows = []
    for a in arrays:
        flat = a.reshape(-1).astype(F32)
        pad = (-flat.shape[0]) % 1024
        rows.append(jnp.pad(flat, (0, pad)).reshape(-1, 128))
    return jnp.concatenate(rows, axis=0)


def _unpack(packed, shapes):
    out, r = [], 0
    for shp in shapes:
        n = int(np.prod(shp))
        nr = (n + 1023) // 1024 * 8
        out.append(packed[r:r + nr].reshape(-1)[:n].reshape(shp))
        r += nr
    return out


def kernel(x, mix_norm, ffn_norm, a_w_in, a_norm_v, a_w_s, a_b_s, a_w_out, kv_norm, w_kv, b_kv, b_w_q, b_b_q, b_sinks, b_w_o, b_b_o, rel_bias, ffn_w_gate, ffn_w_up, ffn_w_down, final_norm, loss_target, m_mix_norm, m_ffn_norm, m_a_w_in, m_a_norm_v, m_a_w_s, m_a_b_s, m_a_w_out, m_kv_norm, m_w_kv, m_b_kv, m_b_w_q, m_b_b_q, m_b_sinks, m_b_w_o, m_b_b_o, m_rel_bias, m_ffn_w_gate, m_ffn_w_up, m_ffn_w_down, m_final_norm, v_mix_norm, v_ffn_norm, v_a_w_in, v_a_norm_v, v_a_w_s, v_a_b_s, v_a_w_out, v_kv_norm, v_w_kv, v_b_kv, v_b_w_q, v_b_b_q, v_b_sinks, v_b_w_o, v_b_b_o, v_rel_bias, v_ffn_w_gate, v_ffn_w_up, v_ffn_w_down, v_final_norm):
    _, S, D = x.shape
    LA, LB, L = a_w_in.shape[0], b_w_q.shape[0], ffn_w_gate.shape[0]
    F = ffn_w_gate.shape[2]
    DS = D // NDEV
    ZC = a_w_in.shape[2]
    KVW = w_kv.shape[1]
    NKV = KVW // (2 * HEAD_DIM)
    NH = D // HEAD_DIM
    assert ZC * NDEV == 2 * D and NH == NKV * KV_GROUP and S % BLOCK == 0
    TM = min(1024, S)
    TN_ = min(1024, D)
    TS = min(512, D)
    KC = 4

    ix, iy, ic = lax.axis_index("x"), lax.axis_index("y"), lax.axis_index("c")
    me = (4 * ix + 2 * iy + ic).astype(jnp.int32)
    me1 = me.reshape(1)
    where = jnp.stack([ic, 2 * ix + iy]).astype(jnp.int32)

    def tr3(a):
        return jnp.transpose(a, (0, 2, 1))

    gate_t, up_t = tr3(ffn_w_gate), tr3(ffn_w_up)
    w_kv3 = w_kv.reshape((1,) + w_kv.shape)

    def layer_arrays(l):
        arrs = [("gu", 2 * F, D, [(gate_t, l, 0), (up_t, l, F)]), ("down", F, D, [(ffn_w_down, l, 0)])]
        if l < LA:
            arrs += [("win", D, ZC, [(a_w_in, l, 0)]), ("wout", DS, D, [(a_w_out, l, 0)])]
            if l == LA - 1:
                arrs.append(("wkv", DS, KVW, [(w_kv3, 0, 0)]))
        else:
            i_b = l - LA
            arrs.append(("wqo", 2 * DS, D, [(b_w_q, i_b, 0), (b_w_o, i_b, DS)]))
        return arrs

    gathers = []

    def gather_begin(g_idx, deps):
        g = gathers[g_idx]
        g["send"], g["recv"], g["bufs"], g["token"] = _relay_start(f"relay_start{g_idx}", g["bufs"], deps)

    for l in range(L):
        mixer, ffn = dict(keys=[], bufs=[]), dict(keys=[], bufs=[])
        for key, rows, width, sources in layer_arrays(l):
            buf = lax.empty((NDEV, rows, width), BF16)
            for si, (src, li, row0) in enumerate(sources):
                buf = _cast_into(f"cast_{key}{l}_{si}", src, li, buf, row0, me1)
            group = ffn if key in ("gu", "down") else mixer
            group["keys"].append(key)
            group["bufs"].append(buf)
        gathers += [mixer, ffn]
        if l == 0:
            nv_rows = _pack([a_norm_v])
            nv = _cast_into("put_norm_v", nv_rows.reshape((1,) + nv_rows.shape), 0,
                            lax.empty((NDEV,) + nv_rows.shape, F32), 0, me1)
            nv_send, nv_recv, nv_bufs, token = _gather_start("gather_norm_v_start", [nv], [])
            gather_begin(0, [token])

    def gather_relay(g_idx, deps):
        g = gathers[g_idx]
        g["fsend"], g["frecv"], g["bufs"], tok = _relay_neighbors(f"relay_neighbors{g_idx}", g["bufs"], g["send"],
                                                                  g["recv"], deps)
        if g_idx + 1 < len(gathers):
            gather_begin(g_idx + 1, [tok])
            tok = gathers[g_idx + 1]["token"]
        return tok

    def finish_gather(g_idx, deps):
        g = gathers[g_idx]
        gsend, grecv, bufs = _relay_diagonal(f"relay_diagonal{g_idx}", g["bufs"], g["fsend"], g["frecv"], deps)
        bufs = _relay_finish(f"relay_finish{g_idx}", bufs, g["send"], g["recv"], g["fsend"], g["frecv"], gsend, grecv)
        return dict(zip(g["keys"], bufs))

    token = gather_relay(0, [gathers[0]["token"]] + [b for g in gathers[1:] for b in g["bufs"]])

    buckets = jnp.asarray(_bucket_table())
    bias = _bias_table("bias_table", rel_bias.T, buckets).reshape(NH, BLOCK, 2 * BLOCK)

    def rows_full(tm):
        return pl.BlockSpec((tm, D), lambda i, j: (i, 0))

    def tile(tm, tn):
        return pl.BlockSpec((tm, tn), lambda i, j: (i, j))

    vec_tile = pl.BlockSpec((1, TN_), lambda i, j: (0, j))

    def ffn_forward(l, wl, h_mid, tag, deps):
        xf = _rms_fwd(f"ffn_norm_fwd{tag}", h_mid, ffn_norm[l], deps=deps)

        def ep(parts, ex, outs):
            a, b = parts
            sg = jax.nn.sigmoid(a)
            silu = a * sg
            outs[0][0] = (b * (sg * (1.0 + a * (1.0 - sg)))).astype(BF16)
            outs[0][1] = silu.astype(BF16)
            outs[1][...] = (silu * b).astype(BF16)

        ab, hid = _gemm(
            f"ffn_up{tag}", (S // TM, NDEV),
            [(xf, rows_full(TM)),
             (wl["gu"], pl.BlockSpec((None, F, D), lambda i, e: (e, 0, 0))),
             (wl["gu"], pl.BlockSpec((None, F, D), lambda i, e: (e, 1, 0)))],
            [(0, 1, NT), (0, 2, NT)], [],
            [(_sds((2, NDEV, S, F), BF16), pl.BlockSpec((2, None, TM, F), lambda i, e: (0, e, i, 0))),
             (_sds((NDEV, S, F), BF16), pl.BlockSpec((None, TM, F), lambda i, e: (e, i, 0)))],
            ep, separate=True)
        tmd = min(512, S)
        (h_out,) = _gemm(
            f"ffn_down{tag}", (S // tmd, D // TN_),
            [(hid, pl.BlockSpec((NDEV, tmd, F), lambda i, j: (0, i, 0))),
             (wl["down"], pl.BlockSpec((NDEV, F, TN_), lambda i, j: (0, 0, j)))],
            [(0, 1, NN, _pick(c), _pick(c)) for c in range(NDEV)],
            [(h_mid, pl.BlockSpec((tmd, TN_), lambda i, j: (i, j)))],
            [(_sds((S, D), F32), pl.BlockSpec((tmd, TN_), lambda i, j: (i, j)))],
            _store_add_extra)
        return dict(h_mid=h_mid, xf=xf, ab=ab, hid=hid), h_out

    def stacked_rows_gemm(name, a, wmat, blk, extras, ep, out_dtype, deps=()):
        return _gemm(
            name, (S // TM, D // TN_),
            [(a, rows_full(TM)), (wmat, pl.BlockSpec((NDEV, DS, TN_), lambda i, j: (0, blk, j)))],
            [(0, 1, NN, None, _stacked)], extras,
            [(_sds((S, D), out_dtype), tile(TM, TN_))], ep, deps=deps)[0]

    def back_rows_gemm(name, a, wmat, blk, out_dtype, deps=()):
        return _gemm(
            name, (S // TM, NDEV // KC),
            [(a, rows_full(TM)), (wmat, pl.BlockSpec((KC, DS, D), lambda i, e: (e, blk, 0)))],
            [(0, 1, NT, None, _stacked)], [],
            [(_sds((S, D), out_dtype), pl.BlockSpec((TM, KC * DS), lambda i, e: (i, e)))], _store, deps=deps)[0]

    def grad_rows_gemm(name, act, d_bf, buf, blk):
        return _gemm(
            name, (NDEV,),
            [(act, pl.BlockSpec((S, DS), lambda e: (0, e))), (d_bf, pl.BlockSpec((S, D), lambda e: (0, 0)))],
            [(0, 1, TN)], [(buf, ANY)],
            [(_sds(buf.shape, BF16), pl.BlockSpec((None, DS, D), lambda e: (e, blk, 0)))],
            _store, aliases={2: 0})[0]

    saved, weights = [], []
    h = x.reshape(S, D)
    k_heads = v_heads = hn = h_kv = norm_v = None
    for layer in range(L):
        wl = finish_gather(2 * layer, [token] if layer == 0 else [h])
        weights.append(wl)
        if layer == 0:
            nv_fsend, nv_frecv, nv_bufs = _gather_forward("gather_norm_v_forward", nv_bufs, nv_send, nv_recv,
                                                          [wl["win"]])
            (nv_all,) = _gather_finish("gather_norm_v_finish", nv_bufs, nv_send, nv_recv, nv_fsend, nv_frecv)
            norm_v = jnp.transpose(nv_all.reshape(NDEV, -1)[:, :LA * DS].reshape(NDEV, LA, DS), (1, 0, 2)).reshape(LA, D)
        sv = dict(h_in=h)
        xn = _rms_fwd(f"mix_norm_fwd{layer}", h, mix_norm[layer])
        sv["xn"] = xn
        if layer < LA:
            i_a = layer
            (zp,) = _gemm(
                f"gmlp_in{layer}", (S // TM, NDEV),
                [(xn, rows_full(TM)), (wl["win"], pl.BlockSpec((None, D, ZC), lambda i, e: (e, 0, 0)))],
                [(0, 1, NN)], [], [(_sds((S, 2 * D), F32), pl.BlockSpec((TM, ZC), lambda i, e: (i, e)))], _store)
            bst = a_b_s[i_a].T
            gated = _gmlp_fwd(f"gmlp_gate{layer}", zp, norm_v[i_a].reshape(1, D), a_w_s[i_a], bst)
            sv.update(zp=zp, gated=gated, bst=bst)
            relay_token = gather_relay(2 * layer + 1, [gated])
            h_mid = stacked_rows_gemm(f"gmlp_out{layer}", gated, wl["wout"], 0, [(h, tile(TM, TN_))],
                                      _store_add_extra, F32, deps=[relay_token])
        else:
            i_b = layer - LA
            q = stacked_rows_gemm(f"attn_q{layer}", xn, wl["wqo"], 0, [(b_b_q[i_b].reshape(1, D), vec_tile)],
                                  _store_add_extra, BF16)
            relay_token = gather_relay(2 * layer + 1, [q])
            attn, probs, sink_probs = _attn_fwd(f"attn_fwd{layer}", q, k_heads, v_heads, bias, b_sinks[i_b],
                                                deps=[relay_token])
            sv.update(q=q, attn=attn, probs=probs, sink_probs=sink_probs)
            h_mid = stacked_rows_gemm(f"attn_o{layer}", attn, wl["wqo"], 1,
                                      [(h, tile(TM, TN_)), (b_b_o[i_b].reshape(1, D), vec_tile)],
                                      _store_add_extra, F32)
        wl.update(finish_gather(2 * layer + 1, [h_mid]))
        ffn_deps = [gather_relay(2 * layer + 2, [wl["down"]])] if layer + 1 < L else []
        fsv, h = ffn_forward(layer, wl, h_mid, str(layer), ffn_deps)
        sv.update(fsv)
        saved.append(sv)
        if layer == LA - 1:
            h_kv = h
            hn = _rms_fwd("kv_norm_fwd", h, kv_norm)

            def kv_ep(acc, ex, outs):
                val = acc + ex[0][...]
                for hh in range(NKV):
                    outs[0][hh] = val[:, hh * HEAD_DIM:(hh + 1) * HEAD_DIM].astype(BF16)
                    outs[1][hh] = val[:, (NKV + hh) * HEAD_DIM:(NKV + hh + 1) * HEAD_DIM].astype(BF16)

            k_heads, v_heads = _gemm(
                "kv_proj", (S // TM,),
                [(hn, pl.BlockSpec((TM, D), lambda i: (i, 0))),
                 (wl["wkv"], pl.BlockSpec((NDEV, DS, KVW), lambda i: (0, 0, 0)))],
                [(0, 1, NN, None, _stacked)], [(b_kv.reshape(1, KVW), pl.BlockSpec((1, KVW), lambda i: (0, 0)))],
                [(_sds((NKV, S, HEAD_DIM), BF16), pl.BlockSpec((NKV, TM, HEAD_DIM), lambda i: (0, i, 0)))] * 2,
                kv_ep)

    loss11, d, d_bf, g_final = _loss_bwd("loss_bwd", h, final_norm, loss_target.reshape(S, D))
    loss = lax.psum(loss11[0, 0], AXES)

    g_mix, g_ffn = [None] * L, [None] * L
    g_ws, g_bs, g_nv = [None] * LA, [None] * LA, [None] * LA
    g_bq, g_sink, g_bo = [None] * LB, [None] * LB, [None] * LB
    dbiases = []
    kv_parts = []
    g_kvn = g_bkv = None
    exchanges = [[] for _ in range(L)]
    pending = None
    grads_wkv = None
    newest = []

    def new_grads(l):
        return {key: lax.empty((NDEV, rows, width), BF16) for key, rows, width, _ in layer_arrays(l)}

    def exchange_begin(tag, l, gl, keys):
        grads = [gl[k] for k in keys]
        lands = [lax.empty((NCHIP,) + g.shape[1:], BF16) for g in grads]
        send, recv, grads, lands, tok = _sibling_start(f"rs_sibling_start{tag}", grads, lands, [])
        newest[:] = [tok]
        return dict(tag=tag, layer=l, keys=keys, grads=grads, lands=lands, send=send, recv=recv)

    def exchange_middle(st, dep):
        tag = st["tag"]
        grads, lands = _sibling_finish(f"rs_sibling_finish{tag}", st["grads"], st["lands"], st["send"], st["recv"], [dep])
        sums, own = [], []
        for t, key in enumerate(st["keys"]):
            s_, o_ = _pair_sum(f"pair_sum_{key}{tag}", grads[t], lands[t], where)
            sums.append(s_)
            own.append(o_)
        send, recv, sums, own, tok = _chips_start(f"rs_chips_start{tag}", sums, own, [])
        newest[:] = [tok]
        st.update(sums=sums, own=own, send2=send, recv2=recv)
        exchanges[st["layer"]].append(st)

    def exchange_end(st, dep):
        sums, lands = _chips_finish(f"rs_chips_finish{st['tag']}", st["sums"], st["own"], st["send2"], st["recv2"], [dep])
        own = dict(zip(st["keys"], sums)) if st.get("direct") else {k: None for k in st["keys"]}
        return dict(zip(st["keys"], lands)), own

    for layer in reversed(range(L)):
        sv, wl = saved[layer], weights[layer]
        tag = str(layer)
        gl = new_grads(layer)
        if grads_wkv is not None and layer == LA - 1:
            gl["wkv"] = grads_wkv
        def dhid_ep(acc, ex, outs):
            outs[0][0] = (acc * ex[0][0].astype(F32)).astype(BF16)
            outs[0][1] = (acc * ex[0][1].astype(F32)).astype(BF16)

        ab_spec = pl.BlockSpec((2, None, TM, F), lambda i, e: (0, e, i, 0))
        (dab,) = _gemm(
            f"ffn_dhid{tag}", (S // TM, NDEV),
            [(d_bf, rows_full(TM)), (wl["down"], pl.BlockSpec((None, F, D), lambda i, e: (e, 0, 0)))],
            [(0, 1, NT)], [(sv["ab"], ab_spec)], [(_sds((2, NDEV, S, F), BF16), ab_spec)], dhid_ep,
            deps=list(newest))
        if pending:
            exchange_middle(pending, dab)
        act_kinds = [SHARDS, WHOLE, SHARDS2, WHOLE]
        act_lands = [lax.empty((NCHIP, S, F), BF16), lax.empty((S, D), BF16), lax.empty((2, NCHIP, S, F), BF16),
                     lax.empty((S, D), BF16)]
        a_send, a_recv, act, act_lands, tok = _sibling_start(f"act_start{tag}", [sv["hid"], d_bf, dab, sv["xf"]], act_lands,
                                                             list(newest), act_kinds)
        newest[:] = [tok]
        (dxf,) = _gemm(
            f"ffn_dx{tag}", (S // TM, D // TN_, 2 * NDEV // KC),
            [(act[2].reshape(2 * NDEV // KC, KC, S, F), pl.BlockSpec((None, KC, TM, F), lambda i, j, k: (k, 0, i, 0))),
             (wl["gu"], pl.BlockSpec((KC, F, TN_), lambda i, j, k: (k % (NDEV // KC), k // (NDEV // KC), j)))],
            [(0, 1, NN, _pick(c), _pick(c)) for c in range(KC)], [],
            [(_sds((S, D), F32), pl.BlockSpec((TM, TN_), lambda i, j, k: (i, j)))],
            _store, nk=2 * NDEV // KC, acc_shape=(TM, TN_), deps=list(newest))
        (hid_o, dout_o, dab_o, xf_o), (hid_s, dout_s, dab_s, xf_s) = _sibling_finish(
            f"act_finish{tag}", act, act_lands, a_send, a_recv, [dxf], act_kinds)
        (p_down,) = _gemm(
            f"ffn_dwdown{tag}", (NCHIP, D // TN_),
            [(hid_o.reshape(NCHIP, 2, S, F), pl.BlockSpec((None, None, S, F), lambda k, j, wr: (k, wr[0], 0, 0))),
             (dout_o, pl.BlockSpec((S, TN_), lambda k, j, wr: (0, j))),
             (hid_s, pl.BlockSpec((None, S, F), lambda k, j, wr: (k, 0, 0))),
             (dout_s, pl.BlockSpec((S, TN_), lambda k, j, wr: (0, j)))],
            [(0, 1, TN), (2, 3, TN)], [],
            [(_sds((NCHIP, F, D), BF16), pl.BlockSpec((None, F, TN_), lambda k, j, wr: (k, 0, j)))],
            _store, prefetch=[where])
        (p_gu,) = _gemm(
            f"ffn_dwup{tag}", (2, NCHIP, D // TN_),
            [(dab_o.reshape(2, NCHIP, 2, S, F),
              pl.BlockSpec((None, None, None, S, F), lambda w, k, j, wr: (w, k, wr[0], 0, 0))),
             (xf_o, pl.BlockSpec((S, TN_), lambda w, k, j, wr: (0, j))),
             (dab_s, pl.BlockSpec((None, None, S, F), lambda w, k, j, wr: (w, k, 0, 0))),
             (xf_s, pl.BlockSpec((S, TN_), lambda w, k, j, wr: (0, j)))],
            [(0, 1, TN), (2, 3, TN)], [],
            [(_sds((NCHIP, 2 * F, D), BF16), pl.BlockSpec((None, F, TN_), lambda w, k, j, wr: (k, w, j)))],
            _store, prefetch=[where])
        send2, recv2, sums, own, tok = _chips_start(
            f"rs_chips_start_ffn{tag}", [p_gu, p_down], [lax.empty(p_gu.shape, BF16), lax.empty(p_down.shape, BF16)], [])
        newest[:] = [tok]
        exchanges[layer].append(dict(tag=f"_ffn{tag}", layer=layer, keys=["gu", "down"], sums=sums, own=own, send2=send2,
                                     recv2=recv2, direct=True))
        d, d_bf, g_ffn[layer], colsum = _rms_bwd(f"ffn_norm_bwd{tag}", sv["h_mid"], ffn_norm[layer], dxf, d,
                                                 deps=list(newest))
        if layer < LA:
            i_a = layer
            dgated = back_rows_gemm(f"gmlp_dgated{tag}", d_bf, wl["wout"], 0, F32)
            gl["wout"] = grad_rows_gemm(f"gmlp_dwout{tag}", sv["gated"], d_bf, gl["wout"], 0)
            dzp, g_ws[i_a], dbs, g_nv[i_a] = _gmlp_bwd(f"gmlp_bwd{tag}", sv["zp"], dgated,
                                                       norm_v[i_a].reshape(1, D), a_w_s[i_a], sv["bst"])
            g_bs[i_a] = dbs[:, 0, :]
            gl["win"] = _grad_cols(f"gmlp_dwin{tag}", sv["xn"], dzp, gl["win"], TS)
            (dxn,) = _gemm(
                f"gmlp_dx{tag}", (S // TM, D // TN_),
                [(dzp, pl.BlockSpec((TM, NDEV * ZC), lambda i, j: (i, 0))),
                 (wl["win"], pl.BlockSpec((NDEV, TN_, ZC), lambda i, j: (0, j, 0)))],
                [(0, 1, NT, _cols(c, ZC), _pick(c)) for c in range(NDEV)], [],
                [(_sds((S, D), F32), pl.BlockSpec((TM, TN_), lambda i, j: (i, j)))], _store)
        else:
            i_b = layer - LA
            g_bo[i_b] = colsum
            dattn = back_rows_gemm(f"attn_dout{tag}", d_bf, wl["wqo"], 1, BF16)
            gl["wqo"] = grad_rows_gemm(f"attn_dwo{tag}", sv["attn"], d_bf, gl["wqo"], 1)
            dq, g_bq[i_b], dkc, dkp, dvc, dvp, dbias, dsink = _attn_bwd(
                f"attn_bwd{tag}", sv["q"], k_heads, v_heads, dattn, sv["probs"], sv["sink_probs"])
            kv_parts.append((dkc, dkp, dvc, dvp))
            g_sink[i_b] = dsink.reshape(NH)
            dbiases.append(dbias.reshape(NH, BLOCK * 2 * BLOCK))
            gl["wqo"] = grad_rows_gemm(f"attn_dwq{tag}", sv["xn"], dq, gl["wqo"], 0)
            dxn = back_rows_gemm(f"attn_dx{tag}", dq, wl["wqo"], 0, F32)
        d, d_bf, g_mix[layer], _ = _rms_bwd(f"mix_norm_bwd{tag}", sv["h_in"], mix_norm[layer], dxn, d)
        pending = exchange_begin(f"_mix{tag}", layer, gl, [k for k in gl if k not in ("gu", "down")])
        if layer == LA:
            wkv = weights[LA - 1]["wkv"]
            dkv, g_bkv = _kv_grad("kv_grad", kv_parts)
            (grads_wkv,) = _gemm(
                "kv_dw", (NDEV,),
                [(hn, pl.BlockSpec((S, DS), lambda e: (0, e))), (dkv, pl.BlockSpec((S, KVW), lambda e: (0, 0)))],
                [(0, 1, TN)], [(lax.empty((NDEV, DS, KVW), BF16), ANY)],
                [(_sds((NDEV, DS, KVW), BF16), pl.BlockSpec((None, DS, KVW), lambda e: (e, 0, 0)))],
                _store, aliases={2: 0}, deps=list(newest))
            (dhn,) = _gemm(
                "kv_dx", (S // TM, NDEV),
                [(dkv, pl.BlockSpec((TM, KVW), lambda i, e: (i, 0))),
                 (wkv, pl.BlockSpec((None, DS, KVW), lambda i, e: (e, 0, 0)))],
                [(0, 1, NT)], [], [(_sds((S, D), F32), pl.BlockSpec((TM, DS), lambda i, e: (i, e)))], _store)
            d, d_bf, g_kvn, _ = _rms_bwd("kv_norm_bwd", h_kv, kv_norm, dhn, d)
    grad_x = d.reshape(x.shape)

    exchange_middle(pending, d)

    g_rel = _bias_grad("bias_grad", dbiases, buckets)
    small_local = _pack([jnp.concatenate(g_mix, axis=0), jnp.concatenate(g_ffn, axis=0), jnp.stack(g_ws),
                         jnp.stack(g_bs), g_kvn, g_bkv, jnp.concatenate(g_bq, axis=0), jnp.stack(g_sink),
                         jnp.concatenate(g_bo, axis=0), g_rel, g_final, jnp.concatenate(g_nv, axis=0)])
    small_slot = _cast_into("put_small_grads", small_local.reshape((1,) + small_local.shape), 0,
                            lax.empty((NDEV,) + small_local.shape, F32), 0, me1)
    s_send, s_recv, s_bufs, s_tok = _gather_start("gather_small_start", [small_slot], list(newest))

    results = {}
    after = [s_tok]

    def upd(pname, w, m, v, l, li, lands, row0, own=None):
        w3 = w if w.ndim == 3 else w.reshape((1,) + w.shape)
        prev = results.get(pname) or [lax.empty(w3.shape, F32) for _ in range(4)]
        results[pname] = _adamw_shard(f"adamw_{pname}{l}", w3, m.reshape(w3.shape), v.reshape(w3.shape), lands,
                                      row0, li, prev, deps=list(after), own=own, where=where)
        after[:] = [results[pname][0]]

    for l in reversed(range(L)):
        for st in exchanges[l]:
            lands, own = exchange_end(st, after[0])
            if "gu" in lands:
                upd("ffn_w_gate", gate_t, tr3(m_ffn_w_gate), tr3(v_ffn_w_gate), l, l, lands["gu"], 0, own["gu"])
                upd("ffn_w_up", up_t, tr3(m_ffn_w_up), tr3(v_ffn_w_up), l, l, lands["gu"], F, own["gu"])
                upd("ffn_w_down", ffn_w_down, m_ffn_w_down, v_ffn_w_down, l, l, lands["down"], 0, own["down"])
            if "win" in lands:
                upd("a_w_in", a_w_in, m_a_w_in, v_a_w_in, l, l, lands["win"], 0)
                upd("a_w_out", a_w_out, m_a_w_out, v_a_w_out, l, l, lands["wout"], 0)
            if "wkv" in lands:
                upd("w_kv", w_kv, m_w_kv, v_w_kv, l, 0, lands["wkv"], 0)
            if "wqo" in lands:
                upd("b_w_q", b_w_q, m_b_w_q, v_b_w_q, l, l - LA, lands["wqo"], 0)
                upd("b_w_o", b_w_o, m_b_w_o, v_b_w_o, l, l - LA, lands["wqo"], DS)
    for pname in ("ffn_w_gate", "ffn_w_up"):
        results[pname] = [tr3(r) for r in results[pname]]
    results["w_kv"] = [r.reshape(w_kv.shape) for r in results["w_kv"]]

    small_w = [mix_norm, ffn_norm, a_w_s, a_b_s, kv_norm, b_kv, b_b_q, b_sinks, b_b_o, rel_bias, final_norm]
    small_m = [m_mix_norm, m_ffn_norm, m_a_w_s, m_a_b_s, m_kv_norm, m_b_kv, m_b_b_q, m_b_sinks, m_b_b_o, m_rel_bias,
               m_final_norm]
    small_v = [v_mix_norm, v_ffn_norm, v_a_w_s, v_a_b_s, v_kv_norm, v_b_kv, v_b_b_q, v_b_sinks, v_b_b_o, v_rel_bias,
               v_final_norm]
    shapes = [w.shape for w in small_w] + [(LA, D)]
    s_fsend, s_frecv, s_bufs = _gather_forward("gather_small_forward", s_bufs, s_send, s_recv, list(after))
    (small_all,) = _gather_finish("gather_small_finish", s_bufs, s_send, s_recv, s_fsend, s_frecv)
    small_sum = _sum_devices("sum_small_grads", small_all)
    small_g = _unpack(small_sum, shapes)
    g_normv = lax.dynamic_slice_in_dim(small_g[-1], me * DS, DS, axis=1)
    small_g = small_g[:-1] + [g_normv]
    small_w, small_m, small_v = small_w + [a_norm_v], small_m + [m_a_norm_v], small_v + [v_a_norm_v]
    shapes = [w.shape for w in small_w]
    s_delta, s_m, s_v = _adamw_flat("adamw_small", _pack(small_w), _pack(small_g), _pack(small_m), _pack(small_v))
    s_delta, s_m, s_v = _unpack(s_delta, shapes), _unpack(s_m, shapes), _unpack(s_v, shapes)

    names = ["mix_norm", "ffn_norm", "a_w_in", "a_norm_v", "a_w_s", "a_b_s", "a_w_out", "kv_norm", "w_kv", "b_kv",
             "b_w_q", "b_b_q", "b_sinks", "b_w_o", "b_b_o", "rel_bias", "ffn_w_gate", "ffn_w_up", "ffn_w_down",
             "final_norm"]
    small_names = ["mix_norm", "ffn_norm", "a_w_s", "a_b_s", "kv_norm", "b_kv", "b_b_q", "b_sinks", "b_b_o", "rel_bias",
                   "final_norm", "a_norm_v"]
    res = {}
    for idx, nm in enumerate(small_names):
        res[nm] = (small_g[idx].reshape(shapes[idx]), s_delta[idx], s_m[idx], s_v[idx])
    for nm, u in results.items():
        res[nm] = tuple(u)
    out = [loss, grad_x]
    for part in range(4):
        out += [res[nm][part] for nm in names]
    return tuple(out)
```
